```python
import math
import jax
import jax.numpy as jnp
from jax import lax
import numpy as np

D_MODEL = 1024
BATCH = 16
SEQ = 256
DEPTH = 2
DEC_BATCH = 2
DEC_SEQ = 1024
PAST_LEN = 256

GRID_W = 64
BLOCK = 128
WINDOW = 128
ROPE_BASE = 10000.0
A_HEADS = 8
A_KV_HEADS = 2
A_GROUP = A_HEADS // A_KV_HEADS
A_HEAD_DIM = 64
A_SCALE = A_HEAD_DIM ** -0.5
MLA_HEADS = 8
MLA_Q_RANK = 256
MLA_KV_RANK = 128
MLA_NOPE_DIM = 64
MLA_ROPE_DIM = 32
MLA_V_DIM = 64
MLA_SCALE = (MLA_NOPE_DIM + MLA_ROPE_DIM) ** -0.5
A_Q_W = A_HEADS * A_HEAD_DIM
A_KV_W = A_KV_HEADS * A_HEAD_DIM
AB_SPLITS = (A_Q_W, A_Q_W + A_KV_W, A_Q_W + 2 * A_KV_W,
             A_Q_W + 2 * A_KV_W + MLA_Q_RANK,
             A_Q_W + 2 * A_KV_W + MLA_Q_RANK + MLA_KV_RANK)
AB_IN_W = A_Q_W + 2 * A_KV_W + MLA_Q_RANK + MLA_KV_RANK + MLA_ROPE_DIM
AB_OUT_W = A_Q_W + MLA_HEADS * MLA_V_DIM
S5_WIDTH = D_MODEL
S5_GROUP_CH = 16
S5_GROUPS = S5_WIDTH // S5_GROUP_CH
S5_STATE = 64
N_EXPERTS = 64
TOP_K = 6
EXPERT_FF = 128
SHARED_FF = 128
ROUTED_SCALE = 2.5
N_EVEN = (DEPTH + 1) // 2
N_ODD = DEPTH // 2
DEEPNORM_ALPHA = (2.0 * DEPTH) ** 0.25
DEEPNORM_BETA = (8.0 * DEPTH) ** -0.25
LN_EPS = 1e-5
RMS_EPS = 1e-6
NEG_INF = -1e30

kernel_name = "hybrid_diffusion_prefix_trunk"


def layer_norm(x, g, b):
    x32 = x.astype(jnp.float32)
    mu = jnp.mean(x32, axis=-1, keepdims=True)
    var = jnp.mean(jnp.square(x32 - mu), axis=-1, keepdims=True)
    y = (x32 - mu) * lax.rsqrt(var + LN_EPS) * g.astype(jnp.float32) + b.astype(jnp.float32)
    return y.astype(x.dtype)


def rms_norm(x, g):
    x32 = x.astype(jnp.float32)
    y = x32 * lax.rsqrt(jnp.mean(jnp.square(x32), axis=-1, keepdims=True) + RMS_EPS) * g.astype(jnp.float32)
    return y.astype(x.dtype)


def rope_1d(x, pos):
    half = x.shape[-1] // 2
    inv_freq = ROPE_BASE ** (-jnp.arange(half, dtype=jnp.float32) / half)
    ang = pos[:, None] * inv_freq[None, :]
    cos = jnp.cos(ang)[None, :, None, :]
    sin = jnp.sin(ang)[None, :, None, :]
    x32 = x.astype(jnp.float32)
    x1, x2 = x32[..., :half], x32[..., half:]
    return jnp.concatenate([x1 * cos - x2 * sin, x1 * sin + x2 * cos], axis=-1).astype(x.dtype)


def axial_rope(x, row, col):
    half = x.shape[-1] // 2
    return jnp.concatenate([rope_1d(x[..., :half], row), rope_1d(x[..., half:], col)], axis=-1)


def adaln(cvec, w, b):
    m = jax.nn.silu(cvec) @ w + b
    return [p[:, None, :] for p in jnp.split(m, 6, axis=-1)]


def modulate(x, shift, scale):
    return x * (1.0 + scale) + shift


def dense_attention(q, k, v, scale, sink):
    bsz, n_q, n_kv, n_grp, dim = q.shape
    n_blk = n_q // BLOCK
    q_blocks = jnp.moveaxis(q.reshape(bsz, n_blk, BLOCK, n_kv, n_grp, dim), 1, 0)

    def one_block(q_blk):
        s = jnp.einsum("bqhgd,bkhd->bhgqk", q_blk, k).astype(jnp.float32) * scale
        if sink is not None:
            s_sink = jnp.broadcast_to(sink.astype(jnp.float32)[None, :, :, None, None], s.shape[:-1] + (1,))
            s = jnp.concatenate([s, s_sink], axis=-1)
        p = jax.nn.softmax(s, axis=-1)
        if sink is not None:
            p = p[..., :-1]
        return jnp.einsum("bhgqk,bkhe->bqhge", p.astype(v.dtype), v)

    out = lax.map(one_block, q_blocks)
    return jnp.moveaxis(out, 0, 1).reshape(bsz, n_q, n_kv, n_grp, v.shape[-1])


def windowed_prefix_attention(q, k, v, k_ctx, v_ctx, sink, scale):
    bsz, n_tok, n_kv, n_grp, dim = q.shape
    n_blk = n_tok // BLOCK
    q_b = q.reshape(bsz, n_blk, BLOCK, n_kv, n_grp, dim)
    pad = ((0, 0), (BLOCK, BLOCK), (0, 0), (0, 0))

    def band(t):
        tp = jnp.pad(t, pad).reshape(bsz, n_blk + 2, BLOCK, n_kv, t.shape[-1])
        return jnp.concatenate([tp[:, :-2], tp[:, 1:-1], tp[:, 2:]], axis=2)

    k_b, v_b = band(k), band(v)
    q_pos = jnp.arange(n_blk)[:, None] * BLOCK + jnp.arange(BLOCK)[None, :]
    k_pos = (jnp.arange(n_blk)[:, None] - 1) * BLOCK + jnp.arange(3 * BLOCK)[None, :]
    valid = ((jnp.abs(q_pos[:, :, None] - k_pos[:, None, :]) <= WINDOW)
             & (k_pos >= 0)[:, None, :] & (k_pos < n_tok)[:, None, :])
    s_win = jnp.einsum("bnqhgd,bnkhd->bnhgqk", q_b, k_b).astype(jnp.float32) * scale
    s_win = jnp.where(valid[None, :, None, None], s_win, NEG_INF)
    s_ctx = jnp.einsum("bnqhgd,bchd->bnhgqc", q_b, k_ctx).astype(jnp.float32) * scale
    s_sink = jnp.broadcast_to(sink.astype(jnp.float32)[None, None, :, :, None, None], s_win.shape[:-1] + (1,))
    p = jax.nn.softmax(jnp.concatenate([s_win, s_ctx, s_sink], axis=-1), axis=-1)
    n_win = 3 * BLOCK
    p_win = p[..., :n_win].astype(v.dtype)
    p_ctx = p[..., n_win:n_win + k_ctx.shape[1]].astype(v.dtype)
    out = (jnp.einsum("bnhgqk,bnkhd->bnqhgd", p_win, v_b)
           + jnp.einsum("bnhgqc,bchd->bnqhgd", p_ctx, v_ctx))
    return out.reshape(bsz, n_tok, n_kv, n_grp, dim)


def ab_project(h, w_in, q_norm, kv_norm):
    bsz, n_tok = h.shape[:2]
    q_a, k_a, v_a, c_q, c_kv, k_rope = jnp.split(h @ w_in, AB_SPLITS, axis=-1)
    q_a = q_a.reshape(bsz, n_tok, A_HEADS, A_HEAD_DIM)
    k_a = k_a.reshape(bsz, n_tok, A_KV_HEADS, A_HEAD_DIM)
    v_a = v_a.reshape(bsz, n_tok, A_KV_HEADS, A_HEAD_DIM)
    return q_a, k_a, v_a, rms_norm(c_q, q_norm), rms_norm(c_kv, kv_norm), k_rope


def mla_queries(c_q, w_uq):
    bsz, n_tok = c_q.shape[:2]
    q = (c_q @ w_uq).reshape(bsz, n_tok, MLA_HEADS, MLA_NOPE_DIM + MLA_ROPE_DIM)
    return q[..., :MLA_NOPE_DIM], q[..., MLA_NOPE_DIM:]


def mla_keys_values(c_kv, k_rope, w_ukv):
    bsz, n_tok = c_kv.shape[:2]
    kv = (c_kv @ w_ukv).reshape(bsz, n_tok, MLA_HEADS, MLA_NOPE_DIM + MLA_V_DIM)
    k_nope, v = kv[..., :MLA_NOPE_DIM], kv[..., MLA_NOPE_DIM:]
    k_r = jnp.broadcast_to(k_rope, (bsz, n_tok, MLA_HEADS, MLA_ROPE_DIM)).astype(k_nope.dtype)
    return jnp.concatenate([k_nope, k_r], axis=-1), v


def ab_mixer_context(h, w_in, sink, q_norm, kv_norm, w_uq, w_ukv, w_out):
    bsz, n_tok = h.shape[:2]
    q_a, k_a, v_a, c_q, c_kv, k_rope = ab_project(h, w_in, q_norm, kv_norm)
    out_a = dense_attention(q_a.reshape(bsz, n_tok, A_KV_HEADS, A_GROUP, A_HEAD_DIM), k_a, v_a,
                            A_SCALE, sink.reshape(A_KV_HEADS, A_GROUP))
    q_nope, q_rope = mla_queries(c_q, w_uq)
    k_m, v_m = mla_keys_values(c_kv, k_rope[:, :, None, :], w_ukv)
    q_m = jnp.concatenate([q_nope, q_rope], axis=-1)[:, :, :, None, :]
    out_b = dense_attention(q_m, k_m, v_m, MLA_SCALE, None)
    merged = jnp.concatenate([out_a.reshape(bsz, n_tok, A_Q_W),
                              out_b.reshape(bsz, n_tok, MLA_HEADS * MLA_V_DIM)], axis=-1)
    return merged @ w_out, k_a, v_a, c_kv, k_rope


def ab_mixer_latent(h, row, col, ctx_k, ctx_v, ctx_ckv, ctx_krope,
                    w_in, sink, q_norm, kv_norm, w_uq, w_ukv, w_out):
    bsz, n_tok = h.shape[:2]
    q_a, k_a, v_a, c_q, c_kv, k_rope = ab_project(h, w_in, q_norm, kv_norm)
    q_a = axial_rope(q_a, row, col)
    k_a = axial_rope(k_a, row, col)
    out_a = windowed_prefix_attention(q_a.reshape(bsz, n_tok, A_KV_HEADS, A_GROUP, A_HEAD_DIM),
                                      k_a, v_a, ctx_k, ctx_v,
                                      sink.reshape(A_KV_HEADS, A_GROUP), A_SCALE)
    q_nope, q_rope = mla_queries(c_q, w_uq)
    q_rope = axial_rope(q_rope, row, col)
    k_lat, v_lat = mla_keys_values(c_kv, axial_rope(k_rope[:, :, None, :], row, col), w_ukv)
    k_ctx, v_ctx = mla_keys_values(ctx_ckv, ctx_krope[:, :, None, :], w_ukv)
    q_m = jnp.concatenate([q_nope, q_rope], axis=-1)[:, :, :, None, :]
    out_b = dense_attention(q_m, jnp.concatenate([k_ctx, k_lat], axis=1),
                            jnp.concatenate([v_ctx, v_lat], axis=1), MLA_SCALE, None)
    merged = jnp.concatenate([out_a.reshape(bsz, n_tok, A_Q_W),
                              out_b.reshape(bsz, n_tok, MLA_HEADS * MLA_V_DIM)], axis=-1)
    return merged @ w_out


def ssm_combine(left, right):
    a1, b1 = left
    a2, b2 = right
    return a1 * a2, a2 * b1 + b2


def s5_scan(u_g, h0, lam_re, lam_im, log_dt, b_re, b_im, c_re, c_im, reverse):
    lam = lax.complex(lam_re.astype(jnp.float32), lam_im.astype(jnp.float32))
    dt = jnp.exp(log_dt.astype(jnp.float32))[:, None]
    lam_bar = jnp.exp(lam * dt)
    b = lax.complex(b_re.astype(jnp.float32), b_im.astype(jnp.float32))
    b_bar = ((lam_bar - 1.0) / lam)[..., None] * b
    bu = jnp.einsum("blgc,gpc->blgp", u_g.astype(jnp.complex64), b_bar)
    first = -1 if reverse else 0
    bu = bu.at[:, first].add(lam_bar[None] * h0)
    a = jnp.broadcast_to(lam_bar, bu.shape)
    _, states = lax.associative_scan(ssm_combine, (a, bu), axis=1, reverse=reverse)
    c = lax.complex(c_re.astype(jnp.float32), c_im.astype(jnp.float32))
    y = jnp.real(jnp.einsum("gcp,blgp->blgc", c, states))
    h_final = states[:, 0] if reverse else states[:, -1]
    return y, h_final


def s5_mixer(h, h0_fwd, h0_bwd, w_in, lam_re, lam_im, log_dt, b_re, b_im, c_re, c_im, d, w_out):
    bsz, n_tok = h.shape[:2]
    u = (h @ w_in).astype(jnp.float32)
    u_g = u.reshape(bsz, n_tok, S5_GROUPS, S5_GROUP_CH)
    y_f, h_f = s5_scan(u_g, h0_fwd, lam_re[0], lam_im[0], log_dt[0], b_re[0], b_im[0], c_re[0], c_im[0], False)
    y_b, h_b = s5_scan(u_g, h0_bwd, lam_re[1], lam_im[1], log_dt[1], b_re[1], b_im[1], c_re[1], c_im[1], True)
    y = (y_f + y_b).reshape(bsz, n_tok, S5_WIDTH) + d.astype(jnp.float32) * u
    z = jax.nn.gelu(y).astype(w_out.dtype) @ w_out
    z_val, z_gate = jnp.split(z, 2, axis=-1)
    return (z_val * jax.nn.sigmoid(z_gate)).astype(h.dtype), h_f, h_b


def pack_state(h_f, h_b):
    return jnp.stack([jnp.stack([jnp.real(h_f), jnp.imag(h_f)], axis=1),
                      jnp.stack([jnp.real(h_b), jnp.imag(h_b)], axis=1)], axis=1)


def moe(h, router_w, router_bias, w_gate, w_up, w_down, s_gate, s_up, s_down):
    bsz, n_tok, dim = h.shape
    t = h.reshape(-1, dim)
    n_rows = t.shape[0]
    scores = jax.nn.sigmoid((t @ router_w).astype(jnp.float32))
    _, idx = lax.top_k(scores + router_bias.astype(jnp.float32)[None, :], TOP_K)
    w_sel = jnp.take_along_axis(scores, idx, axis=-1)
    w_sel = w_sel / jnp.sum(w_sel, axis=-1, keepdims=True) * ROUTED_SCALE
    gates = jnp.zeros((n_rows, N_EXPERTS), jnp.float32).at[jnp.arange(n_rows)[:, None], idx].set(w_sel)
    hidden = jax.nn.silu(jnp.einsum("td,edf->tef", t, w_gate)) * jnp.einsum("td,edf->tef", t, w_up)
    routed = jnp.einsum("tef,efd->td", hidden * gates[..., None].astype(hidden.dtype), w_down)
    shared = (jax.nn.silu(t @ s_gate) * (t @ s_up)) @ s_down
    return (routed + shared).astype(h.dtype).reshape(bsz, n_tok, dim)


def setup_inputs(seed: int = 0) -> dict:
    key = jax.random.key(seed)
    keys = list(jax.random.split(key, 48))

    def nrm(shape, scale):
        return jax.random.normal(keys.pop(), shape, jnp.float32) * scale

    s5_n = jnp.pi * jnp.arange(S5_STATE, dtype=jnp.float32)
    return {
        "x_prompt": nrm((BATCH, SEQ, D_MODEL), 1.0),
        "x_sample": nrm((DEC_BATCH, DEC_SEQ, D_MODEL), 1.0),
        "c": nrm((DEC_BATCH, D_MODEL), 1.0),
        "cache_attn_k": nrm((DEC_BATCH, N_EVEN, PAST_LEN, A_KV_HEADS, A_HEAD_DIM), 1.0),
        "cache_attn_v": nrm((DEC_BATCH, N_EVEN, PAST_LEN, A_KV_HEADS, A_HEAD_DIM), 1.0),
        "cache_mla_ckv": nrm((DEC_BATCH, N_EVEN, PAST_LEN, MLA_KV_RANK), 1.0),
        "cache_mla_krope": nrm((DEC_BATCH, N_EVEN, PAST_LEN, MLA_ROPE_DIM), 1.0),
        "state_ssm": nrm((DEC_BATCH, N_ODD, 2, 2, S5_GROUPS, S5_STATE), 0.5),
        "c_ctx": nrm((D_MODEL,), 1.0),
        "ada_w": nrm((DEPTH, D_MODEL, 6 * D_MODEL), D_MODEL ** -0.5),
        "ada_b": nrm((DEPTH, 6 * D_MODEL), 0.01),
        "ln_mix_g": 1.0 + nrm((DEPTH, D_MODEL), 0.01),
        "ln_mix_b": nrm((DEPTH, D_MODEL), 0.01),
        "ln_ffn_g": 1.0 + nrm((DEPTH, D_MODEL), 0.01),
        "ln_ffn_b": nrm((DEPTH, D_MODEL), 0.01),
        "w_in_ab": nrm((N_EVEN, D_MODEL, AB_IN_W), D_MODEL ** -0.5),
        "attn_sink": nrm((N_EVEN, A_HEADS), 0.5),
        "mla_q_norm": 1.0 + nrm((N_EVEN, MLA_Q_RANK), 0.01),
        "mla_kv_norm": 1.0 + nrm((N_EVEN, MLA_KV_RANK), 0.01),
        "mla_w_uq": nrm((N_EVEN, MLA_Q_RANK, MLA_HEADS * (MLA_NOPE_DIM + MLA_ROPE_DIM)), MLA_Q_RANK ** -0.5),
        "mla_w_ukv": nrm((N_EVEN, MLA_KV_RANK, MLA_HEADS * (MLA_NOPE_DIM + MLA_V_DIM)), MLA_KV_RANK ** -0.5),
        "w_out_ab": nrm((N_EVEN, AB_OUT_W, D_MODEL), DEEPNORM_BETA * AB_OUT_W ** -0.5),
        "w_in_c": nrm((N_ODD, D_MODEL, S5_WIDTH), D_MODEL ** -0.5),
        "s5_lam_re": -0.5 + nrm((N_ODD, 2, S5_GROUPS, S5_STATE), 0.01),
        "s5_lam_im": s5_n + nrm((N_ODD, 2, S5_GROUPS, S5_STATE), 0.01),
        "s5_log_dt": jax.random.uniform(keys.pop(), (N_ODD, 2, S5_GROUPS), jnp.float32,
                                        math.log(1e-3), math.log(1e-1)),
        "s5_b_re": nrm((N_ODD, 2, S5_GROUPS, S5_STATE, S5_GROUP_CH), (2 * S5_GROUP_CH) ** -0.5),
        "s5_b_im": nrm((N_ODD, 2, S5_GROUPS, S5_STATE, S5_GROUP_CH), (2 * S5_GROUP_CH) ** -0.5),
        "s5_c_re": nrm((N_ODD, 2, S5_GROUPS, S5_GROUP_CH, S5_STATE), S5_STATE ** -0.5),
        "s5_c_im": nrm((N_ODD, 2, S5_GROUPS, S5_GROUP_CH, S5_STATE), S5_STATE ** -0.5),
        "s5_d": nrm((N_ODD, S5_WIDTH), 1.0),
        "w_out_c": nrm((N_ODD, S5_WIDTH, 2 * D_MODEL), DEEPNORM_BETA * S5_WIDTH ** -0.5),
        "router_w": nrm((DEPTH, D_MODEL, N_EXPERTS), D_MODEL ** -0.5),
        "router_bias": nrm((DEPTH, N_EXPERTS), 0.01),
        "exp_w_gate": nrm((DEPTH, N_EXPERTS, D_MODEL, EXPERT_FF), D_MODEL ** -0.5),
        "exp_w_up": nrm((DEPTH, N_EXPERTS, D_MODEL, EXPERT_FF), D_MODEL ** -0.5),
        "exp_w_down": nrm((DEPTH, N_EXPERTS, EXPERT_FF, D_MODEL), DEEPNORM_BETA * EXPERT_FF ** -0.5),
        "sh_w_gate": nrm((DEPTH, D_MODEL, SHARED_FF), D_MODEL ** -0.5),
        "sh_w_up": nrm((DEPTH, D_MODEL, SHARED_FF), D_MODEL ** -0.5),
        "sh_w_down": nrm((DEPTH, SHARED_FF, D_MODEL), DEEPNORM_BETA * SHARED_FF ** -0.5),
    }


def reference(x_prompt, x_sample, c, cache_attn_k, cache_attn_v, cache_mla_ckv, cache_mla_krope,
              state_ssm, c_ctx, ada_w, ada_b, ln_mix_g, ln_mix_b, ln_ffn_g, ln_ffn_b,
              w_in_ab, attn_sink, mla_q_norm, mla_kv_norm, mla_w_uq, mla_w_ukv, w_out_ab,
              w_in_c, s5_lam_re, s5_lam_im, s5_log_dt, s5_b_re, s5_b_im, s5_c_re, s5_c_im,
              s5_d, w_out_c, router_w, router_bias, exp_w_gate, exp_w_up, exp_w_down,
              sh_w_gate, sh_w_up, sh_w_down):

    def channel_mixer(h, l):
        return moe(h, router_w[l], router_bias[l], exp_w_gate[l], exp_w_up[l], exp_w_down[l],
                   sh_w_gate[l], sh_w_up[l], sh_w_down[l])

    def s5_layer(h, h0_f, h0_b, i):
        return s5_mixer(h, h0_f, h0_b, w_in_c[i], s5_lam_re[i], s5_lam_im[i], s5_log_dt[i],
                        s5_b_re[i], s5_b_im[i], s5_c_re[i], s5_c_im[i], s5_d[i], w_out_c[i])

    x = x_prompt
    new_k, new_v, new_ckv, new_krope, new_ssm = [], [], [], [], []
    for l in range(DEPTH):
        i = l // 2
        sh_m, sc_m, g_m, sh_f, sc_f, g_f = adaln(c_ctx[None, :], ada_w[l], ada_b[l])
        h = modulate(x, sh_m, sc_m)
        if l % 2 == 0:
            out, k_a, v_a, c_kv, k_rope = ab_mixer_context(
                h, w_in_ab[i], attn_sink[i], mla_q_norm[i], mla_kv_norm[i],
                mla_w_uq[i], mla_w_ukv[i], w_out_ab[i])
            new_k.append(k_a)
            new_v.append(v_a)
            new_ckv.append(c_kv)
            new_krope.append(k_rope)
        else:
            zeros = jnp.zeros((x.shape[0], S5_GROUPS, S5_STATE), jnp.complex64)
            out, h_f, h_b = s5_layer(h, zeros, zeros, i)
            new_ssm.append(pack_state(h_f, h_b))
        x = layer_norm(DEEPNORM_ALPHA * x + g_m * out, ln_mix_g[l], ln_mix_b[l])
        h = modulate(x, sh_f, sc_f)
        x = layer_norm(DEEPNORM_ALPHA * x + g_f * channel_mixer(h, l), ln_ffn_g[l], ln_ffn_b[l])
    y_prompt = x

    n_tok = x_sample.shape[1]
    n_rows = n_tok // GRID_W
    row = jnp.repeat(jnp.arange(n_rows, dtype=jnp.float32), GRID_W)
    col = jnp.tile(jnp.arange(GRID_W, dtype=jnp.float32), n_rows)
    x = x_sample
    for l in range(DEPTH):
        i = l // 2
        sh_m, sc_m, g_m, sh_f, sc_f, g_f = adaln(c, ada_w[l], ada_b[l])
        h = modulate(x, sh_m, sc_m)
        if l % 2 == 0:
            out = ab_mixer_latent(h, row, col, cache_attn_k[:, i], cache_attn_v[:, i],
                                  cache_mla_ckv[:, i], cache_mla_krope[:, i],
                                  w_in_ab[i], attn_sink[i], mla_q_norm[i], mla_kv_norm[i],
                                  mla_w_uq[i], mla_w_ukv[i], w_out_ab[i])
        else:
            st = state_ssm[:, i].astype(jnp.float32)
            h0_f = lax.complex(st[:, 0, 0], st[:, 0, 1])
            h0_b = lax.complex(st[:, 1, 0], st[:, 1, 1])
            out, _, _ = s5_layer(h, h0_f, h0_b, i)
        x = layer_norm(DEEPNORM_ALPHA * x + g_m * out, ln_mix_g[l], ln_mix_b[l])
        h = modulate(x, sh_f, sc_f)
        x = layer_norm(DEEPNORM_ALPHA * x + g_f * channel_mixer(h, l), ln_ffn_g[l], ln_ffn_b[l])
    y_sample = x

    new_attn_k = jnp.stack(new_k, axis=1)
    new_attn_v = jnp.stack(new_v, axis=1)
    new_mla_ckv = jnp.stack(new_ckv, axis=1)
    new_mla_krope = jnp.stack(new_krope, axis=1)
    new_state_ssm = jnp.stack(new_ssm, axis=1)
    return (y_prompt, y_sample, new_attn_k, new_attn_v, new_mla_ckv, new_mla_krope, new_state_ssm)
```

```python
import functools

import jax
import jax.numpy as jnp
import numpy as np
from jax import lax
from jax.experimental import pallas as pl
from jax.experimental.pallas import tpu as pltpu

f32 = jnp.float32
bf16 = jnp.bfloat16

D = 1024
N_CTX_B, CTX_L = 16, 256
N_DEN_B, DEN_L = 2, 1024
T_CTX = N_CTX_B * CTX_L
T_DEN = N_DEN_B * DEN_L
T_ALL = T_CTX + T_DEN
GRID_W = 64
WINDOW = 128
ROPE_BASE = 10000.0
A_HEADS, A_KV_HEADS, A_HD = 8, 2, 64
A_GROUP = A_HEADS // A_KV_HEADS
A_SCALE = A_HD ** -0.5
MLA_HEADS, MLA_Q_RANK, MLA_KV_RANK = 8, 256, 128
MLA_NOPE, MLA_ROPE, MLA_V = 64, 32, 64
MLA_SCALE = (MLA_NOPE + MLA_ROPE) ** -0.5
N_EXPERTS, TOP_K, EXPERT_FF, SHARED_FF = 64, 6, 128, 128
ROUTED_SCALE = 2.5
DEPTH = 2
ALPHA = (2.0 * DEPTH) ** 0.25
LN_EPS = 1e-5
RMS_EPS = 1e-6
NEG_INF = -1e30
S5_G, S5_CH, S5_P = 64, 16, 64

LANE = 128
VMEM_LIMIT = 56 * 1024 * 1024

TOK_TILE = 512


def _mod_row(tile_idx, tile_rows):
    start = tile_idx * tile_rows
    return jnp.where(start < T_CTX, 0, 1 + (start - T_CTX) // DEN_L)


def _layer_norm(y, g, b):
    mu = jnp.mean(y, axis=-1, keepdims=True)
    yc = y - mu
    var = jnp.mean(yc * yc, axis=-1, keepdims=True)
    return yc * lax.rsqrt(var + LN_EPS) * g + b


def _silu(x):
    return x * jax.nn.sigmoid(x)


def _dot(a, b):
    return jnp.dot(a, b, preferred_element_type=f32)


def _dot_nt(a, b):
    return lax.dot_general(a, b, (((1,), (1,)), ((), ())), preferred_element_type=f32)


def _dot_exact(a, b):
    return jnp.dot(a, b, preferred_element_type=f32, precision=lax.Precision.HIGHEST)


def _full(shape, n_grid):
    zeros = tuple(0 for _ in shape)
    return pl.BlockSpec(shape, lambda *_: zeros)


def _params(n_grid):
    return pltpu.CompilerParams(dimension_semantics=("arbitrary",) * n_grid, vmem_limit_bytes=VMEM_LIMIT)


ADA_TN = 1536


def _adaln_kernel(c_ref, w_ref, b_ref, o_ref):
    s = _silu(c_ref[...])
    o_ref[0] = _dot_exact(s, w_ref[0]) + b_ref[0]


def _adaln(cvec8, ada_w, ada_b):
    n = 6 * D
    return pl.pallas_call(
        _adaln_kernel,
        grid=(DEPTH, n // ADA_TN),
        in_specs=[
            pl.BlockSpec((8, D), lambda l, j: (0, 0)),
            pl.BlockSpec((1, D, ADA_TN), lambda l, j: (l, 0, j)),
            pl.BlockSpec((1, 1, ADA_TN), lambda l, j: (l, 0, j)),
        ],
        out_specs=pl.BlockSpec((1, 8, ADA_TN), lambda l, j: (l, 0, j)),
        out_shape=jax.ShapeDtypeStruct((DEPTH, 8, n), f32),
        compiler_params=_params(2),
        name="adaln",
    )(cvec8, ada_w, ada_b.reshape(DEPTH, 1, n))


def _rope_table_array(head_dim):
    q = head_dim // 4
    pos = np.arange(DEN_L)
    row, col = (pos // GRID_W).astype(np.float64), (pos % GRID_W).astype(np.float64)
    lane = np.arange(LANE) % head_dim
    is_col = lane >= head_dim // 2
    w = lane % (head_dim // 2)
    first = w < q
    inv_freq = ROPE_BASE ** (-np.arange(q, dtype=np.float64) / q)
    ang = np.where(is_col[None, :], col[:, None], row[:, None]) * inv_freq[w % q][None, :]
    cos, sin = np.cos(ang), np.sin(ang)
    sin_a = np.where(first[None, :], -sin, 0.0)
    sin_b = np.where(first[None, :], 0.0, sin)
    ident = np.stack([np.ones((TOK_TILE, LANE)), np.zeros((TOK_TILE, LANE)), np.zeros((TOK_TILE, LANE))])
    tab = np.concatenate([ident, np.stack([cos, sin_a, sin_b])], axis=1).astype(np.float32)
    return jnp.asarray(tab), q


def _rope_chunk(x, tab_ref, q):
    return x * tab_ref[0] + pltpu.roll(x, LANE - q, 1) * tab_ref[1] + pltpu.roll(x, q, 1) * tab_ref[2]


PROJ_W = 1280
C_QA, C_KA, C_VA, C_CQ, C_CKV, C_KR = 0, 512, 640, 768, 1024, 1152
MLA_NN = MLA_HEADS * MLA_NOPE


def _ab_proj_kernel(x_ref, mods_ref, w_ref, qn_ref, kvn_ref, wuq_ref, wukv_ref, ta_ref, tm_ref,
                    qa_ref, ka_ref, va_ref, ckv_ref, kr_ref, qm_ref, kvl_ref, *, qa_shift, qm_shift):
    r = _mod_row(pl.program_id(0), TOK_TILE)
    mrow = mods_ref[pl.ds(r, 1), :]
    sh, sc = mrow[:, 0:D], mrow[:, D:2 * D]
    h = (x_ref[...] * (1.0 + sc) + sh).astype(bf16)
    proj = _dot(h, w_ref[...])
    for j in range(4):
        c0 = C_QA + LANE * j
        qa_ref[:, LANE * j:LANE * (j + 1)] = _rope_chunk(proj[:, c0:c0 + LANE], ta_ref, qa_shift).astype(bf16)
    ka_ref[...] = _rope_chunk(proj[:, C_KA:C_KA + LANE], ta_ref, qa_shift)
    va_ref[...] = proj[:, C_VA:C_VA + LANE]
    cq = proj[:, C_CQ:C_CQ + MLA_Q_RANK]
    cq = cq * lax.rsqrt(jnp.mean(cq * cq, axis=-1, keepdims=True) + RMS_EPS) * qn_ref[...]
    ckv = proj[:, C_CKV:C_CKV + MLA_KV_RANK]
    ckv = ckv * lax.rsqrt(jnp.mean(ckv * ckv, axis=-1, keepdims=True) + RMS_EPS) * kvn_ref[...]
    ckv_ref[...] = ckv
    kr_ref[...] = _rope_chunk(proj[:, C_KR:C_KR + LANE], tm_ref, qm_shift)
    qm = _dot(cq.astype(bf16), wuq_ref[...])
    qm_ref[:, 0:MLA_NN] = qm[:, 0:MLA_NN].astype(bf16)
    for j in range(2):
        c0 = MLA_NN + LANE * j
        qm_ref[:, c0:c0 + LANE] = _rope_chunk(qm[:, c0:c0 + LANE], tm_ref, qm_shift).astype(bf16)
    kvl_ref[...] = _dot(ckv.astype(bf16), wukv_ref[...]).astype(bf16)


def _rope_block_index(i):
    tiles_ctx = T_CTX // TOK_TILE
    per_seq = DEN_L // TOK_TILE
    return jnp.where(i < tiles_ctx, 0, 1 + (i - tiles_ctx) % per_seq)


def _ab_proj(x_all, mods0, w_in_p, q_norm, kv_norm, w_uq_p, w_ukv_p):
    tab_a, qa_shift = _rope_table_array(A_HD)
    tab_m, qm_shift = _rope_table_array(MLA_ROPE)
    row_spec = lambda w: pl.BlockSpec((TOK_TILE, w), lambda i: (i, 0))
    tab_spec = pl.BlockSpec((3, TOK_TILE, LANE), lambda i: (0, _rope_block_index(i), 0))
    outs = [(512, bf16), (LANE, f32), (LANE, f32), (LANE, f32), (LANE, f32), (768, bf16), (1024, bf16)]
    return pl.pallas_call(
        functools.partial(_ab_proj_kernel, qa_shift=qa_shift, qm_shift=qm_shift),
        grid=(T_ALL // TOK_TILE,),
        in_specs=[row_spec(D), _full((8, 6 * D), 1), _full((D, PROJ_W), 1), _full((1, MLA_Q_RANK), 1),
                  _full((1, MLA_KV_RANK), 1), _full((MLA_Q_RANK, 768), 1), _full((MLA_KV_RANK, 1024), 1),
                  tab_spec, tab_spec],
        out_specs=[row_spec(w) for w, _ in outs],
        out_shape=[jax.ShapeDtypeStruct((T_ALL, w), dt) for w, dt in outs],
        compiler_params=_params(1),
        name="ab_proj",
    )(x_all, mods0, w_in_p, q_norm, kv_norm, w_uq_p, w_ukv_p, tab_a, tab_m)


def _softmax_parts(score_blocks, sink=None):
    m = score_blocks[0].max(axis=-1, keepdims=True)
    for s in score_blocks[1:]:
        m = jnp.maximum(m, s.max(axis=-1, keepdims=True))
    if sink is not None:
        m = jnp.maximum(m, sink)
    ps = [jnp.exp(s - m) for s in score_blocks]
    l = ps[0].sum(axis=-1, keepdims=True)
    for p in ps[1:]:
        l = l + p.sum(axis=-1, keepdims=True)
    if sink is not None:
        l = l + jnp.exp(sink - m)
    return ps, 1.0 / l


def _mix_out_ln(merged_ref, wout_ref, x, mods_ref, r, g_ref, b_ref):
    out = _dot(merged_ref[...], wout_ref[...])
    gate = mods_ref[pl.ds(r, 1), 2 * D:3 * D]
    return _layer_norm(ALPHA * x + gate * out, g_ref[...], b_ref[...])


def _ctx_attn_kernel(sink_ref, qa_ref, ka_ref, va_ref, qm_ref, kvl_ref, kr_ref, x_ref, mods_ref, wout_ref,
                     g_ref, b_ref, o_ref, merged_ref):
    ka = ka_ref[...].astype(bf16)
    va = va_ref[...].astype(bf16)
    for h in range(A_HEADS):
        j = h // A_GROUP
        q = qa_ref[:, A_HD * h:A_HD * (h + 1)]
        s = _dot_nt(q, ka[:, A_HD * j:A_HD * (j + 1)]) * A_SCALE
        (p,), rl = _softmax_parts([s], sink_ref[h])
        o = _dot(p.astype(bf16), va[:, A_HD * j:A_HD * (j + 1)]) * rl
        merged_ref[:, A_HD * h:A_HD * (h + 1)] = o.astype(bf16)
    kr = kr_ref[:, 0:MLA_ROPE].astype(bf16)
    for h in range(MLA_HEADS):
        qn = qm_ref[:, MLA_NOPE * h:MLA_NOPE * (h + 1)]
        qr = qm_ref[:, MLA_NN + MLA_ROPE * h:MLA_NN + MLA_ROPE * (h + 1)]
        kn = kvl_ref[:, MLA_NOPE * h:MLA_NOPE * (h + 1)]
        v = kvl_ref[:, MLA_NN + MLA_V * h:MLA_NN + MLA_V * (h + 1)]
        s = (_dot_nt(qn, kn) + _dot_nt(qr, kr)) * MLA_SCALE
        (p,), rl = _softmax_parts([s])
        o = _dot(p.astype(bf16), v) * rl
        merged_ref[:, MLA_NN + MLA_V * h:MLA_NN + MLA_V * (h + 1)] = o.astype(bf16)
    o_ref[...] = _mix_out_ln(merged_ref, wout_ref, x_ref[...], mods_ref, 0, g_ref, b_ref)


def _ctx_attn(sink, qa, ka, va, qm, kvl, kr, x_all, mods0, w_out, ln_g, ln_b):
    blk = lambda w: pl.BlockSpec((CTX_L, w), lambda b: (b, 0))
    return pl.pallas_call(
        _ctx_attn_kernel,
        grid=(N_CTX_B,),
        in_specs=[pl.BlockSpec(memory_space=pltpu.SMEM), blk(512), blk(LANE), blk(LANE), blk(768), blk(1024),
                  blk(LANE), blk(D), _full((8, 6 * D), 1), _full((D, D), 1), _full((1, D), 1), _full((1, D), 1)],
        out_specs=blk(D),
        out_shape=jax.ShapeDtypeStruct((T_CTX, D), f32),
        scratch_shapes=[pltpu.VMEM((CTX_L, D), bf16)],
        compiler_params=_params(1),
        name="ctx_attn",
    )(sink, qa, ka, va, qm, kvl, kr, x_all, mods0, w_out, ln_g, ln_b)


QB = 128
WIN = 3 * QB
DEN_BLK0 = T_CTX // DEN_L


def _den_attn_kernel(sink_ref, qa_ref, ka_ref, va_ref, cak_ref, cav_ref, qm_ref, kvl_ref, kr_ref, cckv_ref, ckr_ref,
                     wukv_ref, x_ref, mods_ref, wout_ref, g_ref, b_ref, o_ref, merged_ref, kvc_ref):
    b = pl.program_id(0)
    n = pl.program_id(1)

    @pl.when(n == 0)
    def _():
        kvc_ref[...] = _dot(cckv_ref[0].astype(bf16), wukv_ref[...]).astype(bf16)

    start = pl.multiple_of(jnp.clip(QB * (n - 1), 0, DEN_L - WIN), QB)
    qpos = QB * n + lax.broadcasted_iota(jnp.int32, (QB, WIN), 0)
    kpos = start + lax.broadcasted_iota(jnp.int32, (QB, WIN), 1)
    valid = jnp.abs(qpos - kpos) <= WINDOW
    kwin = ka_ref[pl.ds(start, WIN), :].astype(bf16)
    vwin = va_ref[pl.ds(start, WIN), :].astype(bf16)
    kctx = cak_ref[0].astype(bf16)
    vctx = cav_ref[0].astype(bf16)
    for h in range(A_HEADS):
        j = h // A_GROUP
        sl = slice(A_HD * j, A_HD * (j + 1))
        q = qa_ref[:, A_HD * h:A_HD * (h + 1)]
        s_win = jnp.where(valid, _dot_nt(q, kwin[:, sl]) * A_SCALE, NEG_INF)
        s_ctx = _dot_nt(q, kctx[:, sl]) * A_SCALE
        (p_win, p_ctx), rl = _softmax_parts([s_win, s_ctx], sink_ref[h])
        o = (_dot(p_win.astype(bf16), vwin[:, sl]) + _dot(p_ctx.astype(bf16), vctx[:, sl])) * rl
        merged_ref[:, A_HD * h:A_HD * (h + 1)] = o.astype(bf16)
    kr_lat = kr_ref[:, 0:MLA_ROPE].astype(bf16)
    kr_ctx = ckr_ref[0].astype(bf16)
    for h in range(MLA_HEADS):
        qn = qm_ref[:, MLA_NOPE * h:MLA_NOPE * (h + 1)]
        qr = qm_ref[:, MLA_NN + MLA_ROPE * h:MLA_NN + MLA_ROPE * (h + 1)]
        ns = slice(MLA_NOPE * h, MLA_NOPE * (h + 1))
        vs = slice(MLA_NN + MLA_V * h, MLA_NN + MLA_V * (h + 1))
        s_c = (_dot_nt(qn, kvc_ref[:, ns]) + _dot_nt(qr, kr_ctx)) * MLA_SCALE
        s_l = (_dot_nt(qn, kvl_ref[:, ns]) + _dot_nt(qr, kr_lat)) * MLA_SCALE
        (p_c, p_l), rl = _softmax_parts([s_c, s_l])
        o = (_dot(p_c.astype(bf16), kvc_ref[:, vs]) + _dot(p_l.astype(bf16), kvl_ref[:, vs])) * rl
        merged_ref[:, MLA_NN + MLA_V * h:MLA_NN + MLA_V * (h + 1)] = o.astype(bf16)
    o_ref[...] = _mix_out_ln(merged_ref, wout_ref, x_ref[...], mods_ref, 1 + b, g_ref, b_ref)


def _den_attn(sink, qa, ka, va, cache_k, cache_v, qm, kvl, kr, cache_ckv, cache_kr, w_ukv_p, x_all, mods0, w_out,
              ln_g, ln_b):
    nq = DEN_L // QB
    qblk = lambda w: pl.BlockSpec((QB, w), lambda b, n: (T_CTX // QB + b * nq + n, 0))
    seq = lambda w: pl.BlockSpec((DEN_L, w), lambda b, n: (DEN_BLK0 + b, 0))
    cache = lambda w: pl.BlockSpec((1, CTX_L, w), lambda b, n: (b, 0, 0))
    return pl.pallas_call(
        _den_attn_kernel,
        grid=(N_DEN_B, nq),
        in_specs=[pl.BlockSpec(memory_space=pltpu.SMEM), qblk(512), seq(LANE), seq(LANE), cache(LANE), cache(LANE),
                  qblk(768), seq(1024), seq(LANE), cache(MLA_KV_RANK), cache(MLA_ROPE),
                  _full((MLA_KV_RANK, 1024), 2), qblk(D), _full((8, 6 * D), 2), _full((D, D), 2), _full((1, D), 2),
                  _full((1, D), 2)],
        out_specs=pl.BlockSpec((QB, D), lambda b, n: (b * nq + n, 0)),
        out_shape=jax.ShapeDtypeStruct((T_DEN, D), f32),
        scratch_shapes=[pltpu.VMEM((QB, D), bf16), pltpu.VMEM((CTX_L, 1024), bf16)],
        compiler_params=_params(2),
        name="den_attn",
    )(sink, qa, ka, va, cache_k, cache_v, qm, kvl, kr, cache_ckv, cache_kr, w_ukv_p, x_all, mods0, w_out, ln_g, ln_b)


def _route(x1, mrow, rw_ref, rb_ref, h_ref, gates_ref):
    sh, sc = mrow[:, 3 * D:4 * D], mrow[:, 4 * D:5 * D]
    h = x1 * (1.0 + sc) + sh
    h_ref[...] = h.astype(bf16)
    scores = jax.nn.sigmoid(_dot_exact(h, rw_ref[...]))
    lane = lax.broadcasted_iota(jnp.int32, scores.shape, 1).astype(f32)
    sel = jnp.where(lane < N_EXPERTS, scores + rb_ref[...], -jnp.inf)
    gates = jnp.zeros_like(scores)
    for _ in range(TOP_K):
        m = sel.max(axis=-1, keepdims=True)
        idx = jnp.where(sel == m, lane, float(LANE)).min(axis=-1, keepdims=True)
        hit = lane == idx
        gates = jnp.where(hit, scores, gates)
        sel = jnp.where(hit, -jnp.inf, sel)
    gates_ref[...] = gates / gates.sum(axis=-1, keepdims=True) * ROUTED_SCALE


def _router_kernel(x_ref, mods_ref, rw_ref, rb_ref, h_ref, gates_ref):
    r = _mod_row(pl.program_id(0), TOK_TILE)
    _route(x_ref[...], mods_ref[pl.ds(r, 1), :], rw_ref, rb_ref, h_ref, gates_ref)


def _router(x1, mods_l, router_w_p, router_b_p):
    row_spec = lambda w: pl.BlockSpec((TOK_TILE, w), lambda i: (i, 0))
    return pl.pallas_call(
        _router_kernel,
        grid=(T_ALL // TOK_TILE,),
        in_specs=[row_spec(D), _full((8, 6 * D), 1), _full((D, LANE), 1), _full((1, LANE), 1)],
        out_specs=[row_spec(D), row_spec(LANE)],
        out_shape=[jax.ShapeDtypeStruct((T_ALL, D), bf16), jax.ShapeDtypeStruct((T_ALL, LANE), f32)],
        compiler_params=_params(1),
        name="router",
    )(x1, mods_l, router_w_p, router_b_p)


MOE_TOK = 1536
MOE_EG = 4
MOE_TILE = 512
MOE_FF = MOE_EG * EXPERT_FF


def _moe_kernel(h_ref, gates_ref, x_ref, mods_ref, wg_ref, wu_ref, wd_ref, sg_ref, su_ref, sd_ref, g_ref, b_ref,
                o_ref):
    p = pl.program_id(0)
    e = pl.program_id(1)
    n_tiles = MOE_TOK // MOE_TILE

    def gate_f(t):
        r = _mod_row(p * n_tiles + t, MOE_TILE)
        return mods_ref[pl.ds(r, 1), 5 * D:6 * D]

    def rows_of(t):
        return pl.ds(pl.multiple_of(t * MOE_TILE, MOE_TILE), MOE_TILE)

    @pl.when(e == 0)
    def _():
        sg = sg_ref[...].astype(bf16)
        su = su_ref[...].astype(bf16)
        sd = sd_ref[...].astype(bf16)

        def body(t, c):
            rows = rows_of(t)
            ht = h_ref[rows, :]
            hid = _silu(_dot(ht, sg)) * _dot(ht, su)
            o_ref[rows, :] = ALPHA * x_ref[rows, :] + gate_f(t) * _dot(hid.astype(bf16), sd)
            return c

        lax.fori_loop(0, n_tiles, body, 0)

    wg = jnp.concatenate([wg_ref[k].astype(bf16) for k in range(MOE_EG)], axis=1)
    wu = jnp.concatenate([wu_ref[k].astype(bf16) for k in range(MOE_EG)], axis=1)
    wd = jnp.concatenate([wd_ref[k].astype(bf16) for k in range(MOE_EG)], axis=0)
    src = lax.broadcasted_iota(jnp.int32, (LANE, MOE_FF), 0)
    dst = lax.shift_right_logical(lax.broadcasted_iota(jnp.int32, (LANE, MOE_FF), 1), 7) + e * MOE_EG
    expand = jnp.where(src == dst, 1.0, 0.0).astype(bf16)

    def body(t, c):
        rows = rows_of(t)
        ht = h_ref[rows, :]
        hid = _silu(_dot(ht, wg)) * _dot(ht, wu)
        gt = gates_ref[rows, :]
        g_hi = gt.astype(bf16)
        g_lo = (gt - g_hi.astype(f32)).astype(bf16)
        hid = hid * (_dot(g_hi, expand) + _dot(g_lo, expand))
        o_ref[rows, :] += gate_f(t) * _dot(hid.astype(bf16), wd)
        return c

    lax.fori_loop(0, n_tiles, body, 0)

    @pl.when(e == pl.num_programs(1) - 1)
    def _():
        def body(t, c):
            rows = rows_of(t)
            o_ref[rows, :] = _layer_norm(o_ref[rows, :], g_ref[...], b_ref[...])
            return c

        lax.fori_loop(0, n_tiles, body, 0)


def _moe(l, h, gates, x1, mods_l, wg, wu, wd, sg, su, sd, ln_g, ln_b):
    assert EXPERT_FF == LANE
    tok = lambda w: pl.BlockSpec((MOE_TOK, w), lambda p, e: (p, 0))
    return pl.pallas_call(
        _moe_kernel,
        grid=(T_ALL // MOE_TOK, N_EXPERTS // MOE_EG),
        in_specs=[tok(D), tok(LANE), tok(D), _full((8, 6 * D), 2),
                  pl.BlockSpec((None, MOE_EG, D, EXPERT_FF), lambda p, e: (l, e, 0, 0)),
                  pl.BlockSpec((None, MOE_EG, D, EXPERT_FF), lambda p, e: (l, e, 0, 0)),
                  pl.BlockSpec((None, MOE_EG, EXPERT_FF, D), lambda p, e: (l, e, 0, 0)),
                  pl.BlockSpec((None, D, SHARED_FF), lambda p, e: (l, 0, 0)),
                  pl.BlockSpec((None, D, SHARED_FF), lambda p, e: (l, 0, 0)),
                  pl.BlockSpec((None, SHARED_FF, D), lambda p, e: (l, 0, 0)),
                  _full((1, D), 2), _full((1, D), 2)],
        out_specs=tok(D),
        out_shape=jax.ShapeDtypeStruct((T_ALL, D), f32),
        compiler_params=_params(2),
        name="moe",
    )(h, gates, x1, mods_l, wg, wu, wd, sg, su, sd, ln_g, ln_b)


def _router_weights(l, router_w, router_bias):
    rw = jnp.pad(router_w[l], ((0, 0), (0, LANE - N_EXPERTS)))
    rb = jnp.pad(router_bias[l], (0, LANE - N_EXPERTS)).reshape(1, LANE)
    return rw, rb


S5_Q = 8
S5_NGB = D // LANE
S5_TILE = 256
S5_KT = S5_TILE // S5_Q


def _s5_in_kernel(x_ref, mods_ref, w_ref, u_ref, u2_ref, slab_ref, *, row_of):
    r = row_of(pl.program_id(0))
    mrow = mods_ref[pl.ds(r, 1), :]
    sh, sc = mrow[:, 0:D], mrow[:, D:2 * D]
    h = (x_ref[...] * (1.0 + sc) + sh).astype(bf16)
    u = _dot(h, w_ref[...])
    u_ref[...] = u
    for s in range(S5_NGB):
        slab_ref[s] = u[:, LANE * s:LANE * (s + 1)]
    for s in range(S5_NGB):
        for j in range(S5_Q):
            u2_ref[s, :, LANE * j:LANE * (j + 1)] = slab_ref[s, pl.ds(j, S5_KT, stride=S5_Q), :].astype(bf16)


def _s5_in(x_all, tile0, mods1, w_in_c, n_b, seq_len, row_of):
    tiles_per_seq = seq_len // S5_TILE
    return pl.pallas_call(
        functools.partial(_s5_in_kernel, row_of=row_of),
        grid=(n_b * tiles_per_seq,),
        in_specs=[pl.BlockSpec((S5_TILE, D), lambda i: (tile0 + i, 0)), _full((8, 6 * D), 1), _full((D, D), 1)],
        out_specs=[pl.BlockSpec((S5_TILE, D), lambda i: (i, 0)),
                   pl.BlockSpec((S5_NGB, S5_KT, D), lambda i: (0, i % tiles_per_seq, i // tiles_per_seq))],
        out_shape=[jax.ShapeDtypeStruct((n_b * seq_len, D), f32),
                   jax.ShapeDtypeStruct((S5_NGB, seq_len // S5_Q, n_b * D), bf16)],
        scratch_shapes=[pltpu.VMEM((S5_NGB, S5_TILE, LANE), f32)],
        compiler_params=_params(1),
        name="s5_in",
    )(x_all, mods1, w_in_c)


S5_GL = (LANE // S5_CH) * S5_P
S5_ROWS_C = (CTX_L // S5_Q) * N_CTX_B
S5_ROWS_D = (DEN_L // S5_Q) * N_DEN_B


def _split_bf16(a):
    hi = a.astype(bf16)
    return hi, (a - hi.astype(f32)).astype(bf16)


def _s5_scan_kernel(lre_ref, lim_ref, ldt_ref, btr_ref, bti_ref, ctr_ref, cti_ref, uc_ref, ud_ref, h0_ref,
                    yc_ref, yd_ref, st_ref, win_ref, mso_ref, wit_ref, a_ref, s_ref, hp_ref):
    gl = S5_GL
    rowg = lax.shift_right_logical(lax.broadcasted_iota(jnp.int32, (LANE, gl), 0), 4)
    colg = lax.shift_right_logical(lax.broadcasted_iota(jnp.int32, (LANE, gl), 1), 6)
    same_group = rowg == colg
    reps = LANE // S5_CH

    def expand(t):
        return jnp.where(same_group, jnp.concatenate([t] * reps, axis=0), 0.0)

    for d in range(2):
        fwd = d == 0
        lre, lim = lre_ref[d], lim_ref[d]
        dt = jnp.exp(ldt_ref[d])
        a, w = lre * dt, lim * dt
        pre = [jnp.exp(m * a) * jnp.cos(m * w) for m in range(S5_Q + 1)]
        pim = [jnp.exp(m * a) * jnp.sin(m * w) for m in range(S5_Q + 1)]
        xr, xi = pre[1] - 1.0, pim[1]
        den = lre * lre + lim * lim
        cfr, cfi = (xr * lre + xi * lim) / den, (xi * lre - xr * lim) / den
        btr, bti = btr_ref[d], bti_ref[d]
        bexp_r = expand(cfr * btr - cfi * bti)
        bexp_i = expand(cfr * bti + cfi * btr)
        cexp_r, cexp_i = expand(ctr_ref[d]), expand(cti_ref[d])
        for m in range(S5_Q + 1):
            a_ref[m, :, 0:gl] = cexp_r * pre[m] - cexp_i * pim[m]
            a_ref[m, :, gl:2 * gl] = -(cexp_r * pim[m] + cexp_i * pre[m])
        for j in range(S5_Q):
            m = S5_Q - 1 - j if fwd else j
            win_ref[LANE * j:LANE * (j + 1), 0:gl] = (pre[m] * bexp_r - pim[m] * bexp_i).astype(bf16)
            win_ref[LANE * j:LANE * (j + 1), gl:2 * gl] = (pre[m] * bexp_i + pim[m] * bexp_r).astype(bf16)
        for j in range(S5_Q):
            m = j + 1 if fwd else S5_Q - j
            mso_ref[LANE * j:LANE * (j + 1), :] = a_ref[m].astype(bf16)
        b2_hi, b2_lo = _split_bf16(jnp.concatenate([bexp_r, bexp_i], axis=1))
        kt = []
        for tau in range(S5_Q):
            a_hi, a_lo = _split_bf16(a_ref[tau])
            kt.append((_dot_nt(b2_hi, a_hi) + _dot_nt(b2_hi, a_lo) + _dot_nt(b2_lo, a_hi)).astype(bf16))
        zero_blk = jnp.zeros((LANE, LANE), bf16)
        for j in range(S5_Q):
            for jp in range(S5_Q):
                tau = jp - j if fwd else j - jp
                wit_ref[LANE * j:LANE * (j + 1), LANE * jp:LANE * (jp + 1)] = kt[tau] if tau >= 0 else zero_blk

        l8r, l8i = pre[S5_Q], pim[S5_Q]

        def run(u_ref, y_ref, n_b, n_k, h_init):
            rows = n_b * n_k
            s_ref[0:rows, :] = _dot(u_ref[0], win_ref[...])

            per_it = max(1, 8 // n_b)
            n_it = n_k // per_it
            it_rows = per_it * n_b

            def step(i, carry):
                hr, hi_ = carry
                it = i if fwd else n_it - 1 - i
                rs = pl.ds(pl.multiple_of(it * it_rows, it_rows), it_rows)
                s_tile = s_ref[rs, :]
                prev_r, prev_i = [None] * per_it, [None] * per_it
                for sub in (range(per_it) if fwd else reversed(range(per_it))):
                    prev_r[sub], prev_i[sub] = hr, hi_
                    sr = s_tile[sub * n_b:(sub + 1) * n_b, 0:gl]
                    si = s_tile[sub * n_b:(sub + 1) * n_b, gl:2 * gl]
                    hr, hi_ = l8r * hr - l8i * hi_ + sr, l8r * hi_ + l8i * hr + si
                hp_ref[rs, 0:gl] = jnp.concatenate(prev_r, axis=0) if per_it > 1 else prev_r[0]
                hp_ref[rs, gl:2 * gl] = jnp.concatenate(prev_i, axis=0) if per_it > 1 else prev_i[0]
                return hr, hi_

            h_fin = lax.fori_loop(0, n_it, step, h_init)
            y = _dot_nt(hp_ref[0:rows, :].astype(bf16), mso_ref[...]) + _dot(u_ref[0], wit_ref[...])
            if fwd:
                y_ref[0] = y
            else:
                y_ref[0] += y
            return h_fin

        zeros = jnp.zeros((N_CTX_B, gl), f32)
        hr, hi_ = run(uc_ref, yc_ref, N_CTX_B, CTX_L // S5_Q, (zeros, zeros))
        st_ref[d, 0] = hr
        st_ref[d, 1] = hi_
        run(ud_ref, yd_ref, N_DEN_B, DEN_L // S5_Q, (h0_ref[d, 0], h0_ref[d, 1]))


def _s5_scan(lam_re, lam_im, log_dt, bt_re, bt_im, ct_re, ct_im, u2c, u2d, h0):
    gl = S5_GL
    vec = pl.BlockSpec((2, 1, gl), lambda g: (0, 0, g))
    tab = pl.BlockSpec((2, S5_CH, gl), lambda g: (0, 0, g))
    rows = lambda n: pl.BlockSpec((1, n, D), lambda g: (g, 0, 0))
    return pl.pallas_call(
        _s5_scan_kernel,
        grid=(S5_NGB,),
        in_specs=[vec, vec, vec, tab, tab, tab, tab, rows(S5_ROWS_C), rows(S5_ROWS_D),
                  pl.BlockSpec((2, 2, N_DEN_B, gl), lambda g: (0, 0, 0, g))],
        out_specs=[rows(S5_ROWS_C), rows(S5_ROWS_D), pl.BlockSpec((2, 2, N_CTX_B, gl), lambda g: (0, 0, 0, g))],
        out_shape=[jax.ShapeDtypeStruct((S5_NGB, S5_ROWS_C, D), f32), jax.ShapeDtypeStruct((S5_NGB, S5_ROWS_D, D), f32),
                   jax.ShapeDtypeStruct((2, 2, N_CTX_B, S5_G * S5_P), f32)],
        scratch_shapes=[pltpu.VMEM((D, 2 * gl), bf16), pltpu.VMEM((D, 2 * gl), bf16), pltpu.VMEM((D, D), bf16),
                        pltpu.VMEM((S5_Q + 1, LANE, 2 * gl), f32), pltpu.VMEM((S5_ROWS_C, 2 * gl), f32),
                        pltpu.VMEM((S5_ROWS_C, 2 * gl), f32)],
        compiler_params=_params(1),
        name="s5_scan",
    )(lam_re, lam_im, log_dt, bt_re, bt_im, ct_re, ct_im, u2c, u2d, h0)


S5_CTX_TILES = T_CTX // S5_TILE
S5_DEN_TPS = DEN_L // S5_TILE


def _gelu_tanh(x):
    return 0.5 * x * (1.0 + jnp.tanh(np.sqrt(2.0 / np.pi).astype(np.float32) * (x + 0.044715 * (x * x * x))))


def _s5_out_kernel(x_ref, uc_ref, ud_ref, yc_ref, yd_ref, mods_ref, dsk_ref, wout_ref, g_ref, b_ref, rw_ref, rb_ref,
                   x1_ref, h_ref, gates_ref, slab_ref):
    i = pl.program_id(0)
    is_ctx = i < S5_CTX_TILES
    r = _mod_row(i, S5_TILE)
    mrow = mods_ref[pl.ds(r, 1), :]
    u = jnp.where(is_ctx, uc_ref[...], ud_ref[...])
    for s in range(S5_NGB):
        for j in range(S5_Q):
            blk = jnp.where(is_ctx, yc_ref[s, :, LANE * j:LANE * (j + 1)], yd_ref[s, :, LANE * j:LANE * (j + 1)])
            slab_ref[s, pl.ds(j, S5_KT, stride=S5_Q), :] = blk
    y = jnp.concatenate([slab_ref[s] for s in range(S5_NGB)], axis=1) + dsk_ref[...] * u
    z = _dot(_gelu_tanh(y).astype(bf16), wout_ref[...])
    out = z[:, 0:D] * jax.nn.sigmoid(z[:, D:2 * D])
    x1 = _layer_norm(ALPHA * x_ref[...] + mrow[:, 2 * D:3 * D] * out, g_ref[...], b_ref[...])
    x1_ref[...] = x1
    _route(x1, mrow, rw_ref, rb_ref, h_ref, gates_ref)


def _s5_out(x_all, uc, ud, yc, yd, mods1, d_skip, w_out_c, ln_g, ln_b, rw, rb):
    row_spec = lambda w: pl.BlockSpec((S5_TILE, w), lambda i: (i, 0))
    ci = lambda i: jnp.minimum(i, S5_CTX_TILES - 1)
    di = lambda i: jnp.maximum(i - S5_CTX_TILES, 0)
    return pl.pallas_call(
        _s5_out_kernel,
        grid=(T_ALL // S5_TILE,),
        in_specs=[row_spec(D),
                  pl.BlockSpec((S5_TILE, D), lambda i: (ci(i), 0)),
                  pl.BlockSpec((S5_TILE, D), lambda i: (di(i), 0)),
                  pl.BlockSpec((S5_NGB, S5_KT, D), lambda i: (0, 0, ci(i))),
                  pl.BlockSpec((S5_NGB, S5_KT, D), lambda i: (0, di(i) % S5_DEN_TPS, di(i) // S5_DEN_TPS)),
                  _full((8, 6 * D), 1), _full((1, D), 1), _full((D, 2 * D), 1), _full((1, D), 1), _full((1, D), 1),
                  _full((D, LANE), 1), _full((1, LANE), 1)],
        out_specs=[row_spec(D), row_spec(D), row_spec(LANE)],
        out_shape=[jax.ShapeDtypeStruct((T_ALL, D), f32), jax.ShapeDtypeStruct((T_ALL, D), bf16),
                   jax.ShapeDtypeStruct((T_ALL, LANE), f32)],
        scratch_shapes=[pltpu.VMEM((S5_NGB, S5_TILE, LANE), f32)],
        compiler_params=_params(1),
        name="s5_out",
    )(x_all, uc, ud, yc, yd, mods1, d_skip, w_out_c, ln_g, ln_b, rw, rb)


def kernel(x_prompt, x_sample, c, cache_attn_k, cache_attn_v, cache_mla_ckv, cache_mla_krope, state_ssm, c_ctx,
           ada_w, ada_b, ln_mix_g, ln_mix_b, ln_ffn_g, ln_ffn_b, w_in_ab, attn_sink, mla_q_norm, mla_kv_norm,
           mla_w_uq, mla_w_ukv, w_out_ab, w_in_c, s5_lam_re, s5_lam_im, s5_log_dt, s5_b_re, s5_b_im, s5_c_re,
           s5_c_im, s5_d, w_out_c, router_w, router_bias, exp_w_gate, exp_w_up, exp_w_down, sh_w_gate, sh_w_up,
           sh_w_down):
    row = lambda v: v.reshape(1, -1)
    x_all = jnp.concatenate([x_prompt.reshape(T_CTX, D), x_sample.reshape(T_DEN, D)], axis=0)
    cvec8 = jnp.concatenate([c_ctx[None, :], c, jnp.zeros((8 - 1 - N_DEN_B, D), f32)], axis=0)
    mods = _adaln(cvec8, ada_w, ada_b)

    w_in_p = jnp.pad(w_in_ab[0], ((0, 0), (0, PROJ_W - w_in_ab.shape[-1]))).astype(bf16)
    uq = mla_w_uq[0].reshape(MLA_Q_RANK, MLA_HEADS, MLA_NOPE + MLA_ROPE)
    w_uq_p = jnp.concatenate([uq[:, :, :MLA_NOPE].reshape(MLA_Q_RANK, -1), uq[:, :, MLA_NOPE:].reshape(MLA_Q_RANK, -1)],
                             axis=1).astype(bf16)
    ukv = mla_w_ukv[0].reshape(MLA_KV_RANK, MLA_HEADS, MLA_NOPE + MLA_V)
    w_ukv_p = jnp.concatenate([ukv[:, :, :MLA_NOPE].reshape(MLA_KV_RANK, -1),
                               ukv[:, :, MLA_NOPE:].reshape(MLA_KV_RANK, -1)], axis=1).astype(bf16)
    qa, ka, va, ckv, kr, qm, kvl = _ab_proj(x_all, mods[0], w_in_p, row(mla_q_norm[0]), row(mla_kv_norm[0]),
                                            w_uq_p, w_ukv_p)
    w_out_b = w_out_ab[0].astype(bf16)
    g0, b0 = row(ln_mix_g[0]), row(ln_mix_b[0])
    x1c = _ctx_attn(attn_sink[0], qa, ka, va, qm, kvl, kr, x_all, mods[0], w_out_b, g0, b0)
    x1d = _den_attn(attn_sink[0], qa, ka, va,
                    cache_attn_k[:, 0].reshape(N_DEN_B, CTX_L, A_KV_HEADS * A_HD),
                    cache_attn_v[:, 0].reshape(N_DEN_B, CTX_L, A_KV_HEADS * A_HD),
                    qm, kvl, kr, cache_mla_ckv[:, 0], cache_mla_krope[:, 0], w_ukv_p, x_all, mods[0], w_out_b, g0, b0)
    x1 = jnp.concatenate([x1c, x1d], axis=0)
    rw0, rb0 = _router_weights(0, router_w, router_bias)
    h, gates = _router(x1, mods[0], rw0, rb0)
    x2 = _moe(0, h, gates, x1, mods[0], exp_w_gate, exp_w_up, exp_w_down, sh_w_gate, sh_w_up, sh_w_down,
              row(ln_ffn_g[0]), row(ln_ffn_b[0]))

    w_in_c_b = w_in_c[0].astype(bf16)
    uc, u2c = _s5_in(x2, 0, mods[1], w_in_c_b, N_CTX_B, CTX_L, lambda i: 0)
    ud, u2d = _s5_in(x2, S5_CTX_TILES, mods[1], w_in_c_b, N_DEN_B, DEN_L, lambda i: 1 + i // S5_DEN_TPS)
    gp = S5_G * S5_P
    chan_major_b = lambda t: jnp.transpose(t[0], (0, 3, 1, 2)).reshape(2, S5_CH, gp)
    chan_major_c = lambda t: jnp.transpose(t[0], (0, 2, 1, 3)).reshape(2, S5_CH, gp)
    h0 = jnp.transpose(state_ssm[:, 0], (1, 2, 0, 3, 4)).reshape(2, 2, N_DEN_B, gp)
    yc, yd, st = _s5_scan(s5_lam_re[0].reshape(2, 1, gp), s5_lam_im[0].reshape(2, 1, gp),
                          jnp.repeat(s5_log_dt[0], S5_P, axis=-1).reshape(2, 1, gp),
                          chan_major_b(s5_b_re), chan_major_b(s5_b_im), chan_major_c(s5_c_re), chan_major_c(s5_c_im),
                          u2c.reshape(S5_NGB, S5_ROWS_C, D), u2d.reshape(S5_NGB, S5_ROWS_D, D), h0)
    rw1, rb1 = _router_weights(1, router_w, router_bias)
    x3, h, gates = _s5_out(x2, uc, ud, yc.reshape(S5_NGB, CTX_L // S5_Q, N_CTX_B * D),
                           yd.reshape(S5_NGB, DEN_L // S5_Q, N_DEN_B * D), mods[1], row(s5_d[0]),
                           w_out_c[0].astype(bf16), row(ln_mix_g[1]), row(ln_mix_b[1]), rw1, rb1)
    x4 = _moe(1, h, gates, x3, mods[1], exp_w_gate, exp_w_up, exp_w_down, sh_w_gate, sh_w_up, sh_w_down,
              row(ln_ffn_g[1]), row(ln_ffn_b[1]))

    y_prompt = x4[:T_CTX].reshape(N_CTX_B, CTX_L, D)
    y_sample = x4[T_CTX:].reshape(N_DEN_B, DEN_L, D)
    new_attn_k = ka[:T_CTX].reshape(N_CTX_B, 1, CTX_L, A_KV_HEADS, A_HD)
    new_attn_v = va[:T_CTX].reshape(N_CTX_B, 1, CTX_L, A_KV_HEADS, A_HD)
    new_mla_ckv = ckv[:T_CTX].reshape(N_CTX_B, 1, CTX_L, MLA_KV_RANK)
    new_mla_krope = kr[:T_CTX, :MLA_ROPE].reshape(N_CTX_B, 1, CTX_L, MLA_ROPE)
    new_state_ssm = jnp.transpose(st, (2, 0, 1, 3)).reshape(N_CTX_B, 1, 2, 2, S5_G, S5_P)
    return (y_prompt, y_sample, new_attn_k, new_attn_v, new_mla_ckv, new_mla_krope, new_state_ssm)
```

```python
import functools

import jax
import jax.numpy as jnp
import numpy as np
from jax import lax
from jax.experimental import pallas as pl
from jax.experimental.pallas import tpu as pltpu

f32 = jnp.float32
bf16 = jnp.bfloat16

D = 1024
N_CTX_B, CTX_L = 16, 256
N_DEN_B, DEN_L = 2, 1024
T_CTX = N_CTX_B * CTX_L
T_DEN = N_DEN_B * DEN_L
T_ALL = T_CTX + T_DEN
GRID_W = 64
WINDOW = 128
ROPE_BASE = 10000.0
A_HEADS, A_KV_HEADS, A_HD = 8, 2, 64
A_GROUP = A_HEADS // A_KV_HEADS
A_SCALE = A_HD ** -0.5
MLA_HEADS, MLA_Q_RANK, MLA_KV_RANK = 8, 256, 128
MLA_NOPE, MLA_ROPE, MLA_V = 64, 32, 64
MLA_SCALE = (MLA_NOPE + MLA_ROPE) ** -0.5
N_EXPERTS, TOP_K, EXPERT_FF, SHARED_FF = 64, 6, 128, 128
ROUTED_SCALE = 2.5
DEPTH = 2
ALPHA = (2.0 * DEPTH) ** 0.25
LN_EPS = 1e-5
RMS_EPS = 1e-6
NEG_INF = -1e30
S5_G, S5_CH, S5_P = 64, 16, 64

LANE = 128
VMEM_LIMIT = 56 * 1024 * 1024

TOK_TILE = 512


def _mod_row(tile_idx, tile_rows):
    start = tile_idx * tile_rows
    return jnp.where(start < T_CTX, 0, 1 + (start - T_CTX) // DEN_L)


def _layer_norm(y, g, b):
    mu = jnp.mean(y, axis=-1, keepdims=True)
    yc = y - mu
    var = jnp.mean(yc * yc, axis=-1, keepdims=True)
    return yc * lax.rsqrt(var + LN_EPS) * g + b


def _silu(x):
    return x * jax.nn.sigmoid(x)


def _dot(a, b):
    return jnp.dot(a, b, preferred_element_type=f32)


def _dot_nt(a, b):
    return lax.dot_general(a, b, (((1,), (1,)), ((), ())), preferred_element_type=f32)


def _dot_exact(a, b):
    return jnp.dot(a, b, preferred_element_type=f32, precision=lax.Precision.HIGHEST)


def _full(shape, n_grid):
    zeros = tuple(0 for _ in shape)
    return pl.BlockSpec(shape, lambda *_: zeros)


def _two_stream_specs(tile_rows, width):
    n_ctx = T_CTX // tile_rows
    return (pl.BlockSpec((tile_rows, width), lambda i: (jnp.minimum(i, n_ctx - 1), 0)),
            pl.BlockSpec((tile_rows, width), lambda i: (jnp.maximum(i - n_ctx, 0), 0)))


def _pick(i, tile_rows, ctx_ref, den_ref):
    return lax.cond(i < T_CTX // tile_rows, lambda: ctx_ref[...], lambda: den_ref[...])


def _params(n_grid):
    return pltpu.CompilerParams(dimension_semantics=("arbitrary",) * n_grid, vmem_limit_bytes=VMEM_LIMIT)


ADA_TN = 1536


def _adaln_kernel(c_ref, w_ref, b_ref, o_ref):
    s = _silu(c_ref[...])
    o_ref[0] = _dot_exact(s, w_ref[0]) + b_ref[0]


def _adaln(cvec8, ada_w, ada_b):
    n = 6 * D
    return pl.pallas_call(
        _adaln_kernel,
        grid=(DEPTH, n // ADA_TN),
        in_specs=[
            pl.BlockSpec((8, D), lambda l, j: (0, 0)),
            pl.BlockSpec((1, D, ADA_TN), lambda l, j: (l, 0, j)),
            pl.BlockSpec((1, 1, ADA_TN), lambda l, j: (l, 0, j)),
        ],
        out_specs=pl.BlockSpec((1, 8, ADA_TN), lambda l, j: (l, 0, j)),
        out_shape=jax.ShapeDtypeStruct((DEPTH, 8, n), f32),
        compiler_params=_params(2),
        name="adaln",
    )(cvec8, ada_w, ada_b.reshape(DEPTH, 1, n))


def _rope_table_array(head_dim):
    q = head_dim // 4
    pos = np.arange(DEN_L)
    row, col = (pos // GRID_W).astype(np.float64), (pos % GRID_W).astype(np.float64)
    lane = np.arange(LANE) % head_dim
    is_col = lane >= head_dim // 2
    w = lane % (head_dim // 2)
    first = w < q
    inv_freq = ROPE_BASE ** (-np.arange(q, dtype=np.float64) / q)
    ang = np.where(is_col[None, :], col[:, None], row[:, None]) * inv_freq[w % q][None, :]
    cos, sin = np.cos(ang), np.sin(ang)
    sin_a = np.where(first[None, :], -sin, 0.0)
    sin_b = np.where(first[None, :], 0.0, sin)
    ident = np.stack([np.ones((TOK_TILE, LANE)), np.zeros((TOK_TILE, LANE)), np.zeros((TOK_TILE, LANE))])
    tab = np.concatenate([ident, np.stack([cos, sin_a, sin_b])], axis=1).astype(np.float32)
    return jnp.asarray(tab), q


def _rope_chunk(x, tab_ref, q):
    return x * tab_ref[0] + pltpu.roll(x, LANE - q, 1) * tab_ref[1] + pltpu.roll(x, q, 1) * tab_ref[2]


PROJ_W = 1280
C_QA, C_KA, C_VA, C_CQ, C_CKV, C_KR = 0, 512, 640, 768, 1024, 1152
MLA_NN = MLA_HEADS * MLA_NOPE


def _ab_proj_kernel(xc_ref, xd_ref, mods_ref, w_ref, qn_ref, kvn_ref, wuq_ref, wukv_ref, ta_ref, tm_ref,
                    qa_ref, ka_ref, va_ref, ckv_ref, kr_ref, qm_ref, kvl_ref, *, qa_shift, qm_shift):
    i = pl.program_id(0)
    r = _mod_row(i, TOK_TILE)
    mrow = mods_ref[pl.ds(r, 1), :]
    sh, sc = mrow[:, 0:D], mrow[:, D:2 * D]
    x = _pick(i, TOK_TILE, xc_ref, xd_ref)
    h = (x * (1.0 + sc) + sh).astype(bf16)
    proj = _dot(h, w_ref[...])
    for j in range(4):
        c0 = C_QA + LANE * j
        qa_ref[:, LANE * j:LANE * (j + 1)] = _rope_chunk(proj[:, c0:c0 + LANE], ta_ref, qa_shift).astype(bf16)
    ka_ref[...] = _rope_chunk(proj[:, C_KA:C_KA + LANE], ta_ref, qa_shift)
    va_ref[...] = proj[:, C_VA:C_VA + LANE]
    cq = proj[:, C_CQ:C_CQ + MLA_Q_RANK]
    cq = cq * lax.rsqrt(jnp.mean(cq * cq, axis=-1, keepdims=True) + RMS_EPS) * qn_ref[...]
    ckv = proj[:, C_CKV:C_CKV + MLA_KV_RANK]
    ckv = ckv * lax.rsqrt(jnp.mean(ckv * ckv, axis=-1, keepdims=True) + RMS_EPS) * kvn_ref[...]
    ckv_ref[...] = ckv
    kr_ref[...] = _rope_chunk(proj[:, C_KR:C_KR + LANE], tm_ref, qm_shift)
    qm = _dot(cq.astype(bf16), wuq_ref[...])
    qm_ref[:, 0:MLA_NN] = qm[:, 0:MLA_NN].astype(bf16)
    for j in range(2):
        c0 = MLA_NN + LANE * j
        qm_ref[:, c0:c0 + LANE] = _rope_chunk(qm[:, c0:c0 + LANE], tm_ref, qm_shift).astype(bf16)
    kvl_ref[...] = _dot(ckv.astype(bf16), wukv_ref[...]).astype(bf16)


def _rope_block_index(i):
    tiles_ctx = T_CTX // TOK_TILE
    per_seq = DEN_L // TOK_TILE
    return jnp.where(i < tiles_ctx, 0, 1 + (i - tiles_ctx) % per_seq)


def _ab_proj(xc, xd, mods0, w_in_p, q_norm, kv_norm, w_uq_p, w_ukv_p):
    tab_a, qa_shift = _rope_table_array(A_HD)
    tab_m, qm_shift = _rope_table_array(MLA_ROPE)
    row_spec = lambda w: pl.BlockSpec((TOK_TILE, w), lambda i: (i, 0))
    xc_spec, xd_spec = _two_stream_specs(TOK_TILE, D)
    tab_spec = pl.BlockSpec((3, TOK_TILE, LANE), lambda i: (0, _rope_block_index(i), 0))
    outs = [(512, bf16), (LANE, f32), (LANE, f32), (LANE, f32), (LANE, f32), (768, bf16), (1024, bf16)]
    return pl.pallas_call(
        functools.partial(_ab_proj_kernel, qa_shift=qa_shift, qm_shift=qm_shift),
        grid=(T_ALL // TOK_TILE,),
        in_specs=[xc_spec, xd_spec, _full((8, 6 * D), 1), _full((D, PROJ_W), 1), _full((1, MLA_Q_RANK), 1),
                  _full((1, MLA_KV_RANK), 1), _full((MLA_Q_RANK, 768), 1), _full((MLA_KV_RANK, 1024), 1),
                  tab_spec, tab_spec],
        out_specs=[row_spec(w) for w, _ in outs],
        out_shape=[jax.ShapeDtypeStruct((T_ALL, w), dt) for w, dt in outs],
        compiler_params=_params(1),
        name="ab_proj",
    )(xc, xd, mods0, w_in_p, q_norm, kv_norm, w_uq_p, w_ukv_p, tab_a, tab_m)


def _softmax_blocks(s_refs, p_refs, sink_col=None):
    m = s_refs[0][...].max(axis=-1, keepdims=True)
    for s_ref in s_refs[1:]:
        m = jnp.maximum(m, s_ref[...].max(axis=-1, keepdims=True))
    if sink_col is not None:
        m = jnp.maximum(m, sink_col)
    l = None
    for s_ref, p_ref in zip(s_refs, p_refs):
        p = jnp.exp(s_ref[...] - m)
        p_ref[...] = p.astype(bf16)
        ps = p.sum(axis=-1, keepdims=True)
        l = ps if l is None else l + ps
    if sink_col is not None:
        l = l + jnp.exp(sink_col - m)
    return 1.0 / l


def _sink_column(sink_ref, rows_per_head):
    return jnp.concatenate([jnp.full((rows_per_head, 1), sink_ref[h], f32) for h in range(A_HEADS)], axis=0)


def _mix_out_ln(merged_ref, wout_ref, x, mods_ref, r, g_ref, b_ref):
    out = _dot(merged_ref[...], wout_ref[...])
    gate = mods_ref[pl.ds(r, 1), 2 * D:3 * D]
    return _layer_norm(ALPHA * x + gate * out, g_ref[...], b_ref[...])


def _ctx_attn_kernel(sink_ref, qa_ref, ka_ref, va_ref, qm_ref, kvl_ref, kr_ref, x_ref, mods_ref, wout_ref,
                     g_ref, b_ref, o_ref, merged_ref, sa_ref, sm_ref, pa_ref, pm_ref):
    n = CTX_L
    ka = ka_ref[...].astype(bf16)
    va = va_ref[...].astype(bf16)
    for j in range(A_KV_HEADS):
        q4 = jnp.concatenate([qa_ref[:, A_HD * h:A_HD * (h + 1)] for h in range(A_GROUP * j, A_GROUP * (j + 1))],
                             axis=0)
        sa_ref[A_GROUP * n * j:A_GROUP * n * (j + 1), :] = _dot_nt(q4, ka[:, A_HD * j:A_HD * (j + 1)]) * A_SCALE
    kr = kr_ref[:, 0:MLA_ROPE].astype(bf16)
    for h in range(MLA_HEADS):
        qn = qm_ref[:, MLA_NOPE * h:MLA_NOPE * (h + 1)]
        qr = qm_ref[:, MLA_NN + MLA_ROPE * h:MLA_NN + MLA_ROPE * (h + 1)]
        kn = kvl_ref[:, MLA_NOPE * h:MLA_NOPE * (h + 1)]
        sm_ref[n * h:n * (h + 1), :] = (_dot_nt(qn, kn) + _dot_nt(qr, kr)) * MLA_SCALE
    rla = _softmax_blocks([sa_ref], [pa_ref], _sink_column(sink_ref, n))
    rlm = _softmax_blocks([sm_ref], [pm_ref])
    for j in range(A_KV_HEADS):
        rows = slice(A_GROUP * n * j, A_GROUP * n * (j + 1))
        o4 = _dot(pa_ref[rows, :], va[:, A_HD * j:A_HD * (j + 1)]) * rla[rows]
        for g in range(A_GROUP):
            h = A_GROUP * j + g
            merged_ref[:, A_HD * h:A_HD * (h + 1)] = o4[n * g:n * (g + 1)].astype(bf16)
    for h in range(MLA_HEADS):
        rows = slice(n * h, n * (h + 1))
        v = kvl_ref[:, MLA_NN + MLA_V * h:MLA_NN + MLA_V * (h + 1)]
        merged_ref[:, MLA_NN + MLA_V * h:MLA_NN + MLA_V * (h + 1)] = (_dot(pm_ref[rows, :], v) * rlm[rows]).astype(bf16)
    o_ref[...] = _mix_out_ln(merged_ref, wout_ref, x_ref[...], mods_ref, 0, g_ref, b_ref)


def _ctx_attn(sink, qa, ka, va, qm, kvl, kr, x_all, mods0, w_out, ln_g, ln_b):
    blk = lambda w: pl.BlockSpec((CTX_L, w), lambda b: (b, 0))
    return pl.pallas_call(
        _ctx_attn_kernel,
        grid=(N_CTX_B,),
        in_specs=[pl.BlockSpec(memory_space=pltpu.SMEM), blk(512), blk(LANE), blk(LANE), blk(768), blk(1024),
                  blk(LANE), blk(D), _full((8, 6 * D), 1), _full((D, D), 1), _full((1, D), 1), _full((1, D), 1)],
        out_specs=blk(D),
        out_shape=jax.ShapeDtypeStruct((T_CTX, D), f32),
        scratch_shapes=[pltpu.VMEM((CTX_L, D), bf16),
                        pltpu.VMEM((A_HEADS * CTX_L, CTX_L), f32), pltpu.VMEM((MLA_HEADS * CTX_L, CTX_L), f32),
                        pltpu.VMEM((A_HEADS * CTX_L, CTX_L), bf16), pltpu.VMEM((MLA_HEADS * CTX_L, CTX_L), bf16)],
        compiler_params=_params(1),
        name="ctx_attn",
    )(sink, qa, ka, va, qm, kvl, kr, x_all, mods0, w_out, ln_g, ln_b)


QB = 128
WIN = 3 * QB
DEN_BLK0 = T_CTX // DEN_L


def _den_attn_kernel(sink_ref, qa_ref, ka_ref, va_ref, cak_ref, cav_ref, qm_ref, kvl_ref, kr_ref, cckv_ref, ckr_ref,
                     wukv_ref, x_ref, mods_ref, wout_ref, g_ref, b_ref, o_ref, merged_ref, kvc_ref,
                     saw_ref, sac_ref, smc_ref, sml_ref, paw_ref, pac_ref, pmc_ref, pml_ref):
    b = pl.program_id(0)
    n = pl.program_id(1)

    @pl.when(n == 0)
    def _():
        kvc_ref[...] = _dot(cckv_ref[0].astype(bf16), wukv_ref[...]).astype(bf16)

    start = pl.multiple_of(jnp.clip(QB * (n - 1), 0, DEN_L - WIN), QB)
    grp_rows = A_GROUP * QB
    qpos = QB * n + (lax.broadcasted_iota(jnp.int32, (grp_rows, WIN), 0) & (QB - 1))
    kpos = start + lax.broadcasted_iota(jnp.int32, (grp_rows, WIN), 1)
    valid = jnp.abs(qpos - kpos) <= WINDOW
    kwin = ka_ref[pl.ds(start, WIN), :].astype(bf16)
    vwin = va_ref[pl.ds(start, WIN), :].astype(bf16)
    kctx = cak_ref[0].astype(bf16)
    vctx = cav_ref[0].astype(bf16)
    for j in range(A_KV_HEADS):
        sl = slice(A_HD * j, A_HD * (j + 1))
        rows = slice(grp_rows * j, grp_rows * (j + 1))
        q4 = jnp.concatenate([qa_ref[:, A_HD * h:A_HD * (h + 1)] for h in range(A_GROUP * j, A_GROUP * (j + 1))],
                             axis=0)
        saw_ref[rows, :] = jnp.where(valid, _dot_nt(q4, kwin[:, sl]) * A_SCALE, NEG_INF)
        sac_ref[rows, :] = _dot_nt(q4, kctx[:, sl]) * A_SCALE
    kr_lat = kr_ref[:, 0:MLA_ROPE].astype(bf16)
    kr_ctx = ckr_ref[0].astype(bf16)
    for h in range(MLA_HEADS):
        qn = qm_ref[:, MLA_NOPE * h:MLA_NOPE * (h + 1)]
        qr = qm_ref[:, MLA_NN + MLA_ROPE * h:MLA_NN + MLA_ROPE * (h + 1)]
        ns = slice(MLA_NOPE * h, MLA_NOPE * (h + 1))
        rows = slice(QB * h, QB * (h + 1))
        smc_ref[rows, :] = (_dot_nt(qn, kvc_ref[:, ns]) + _dot_nt(qr, kr_ctx)) * MLA_SCALE
        sml_ref[rows, :] = (_dot_nt(qn, kvl_ref[:, ns]) + _dot_nt(qr, kr_lat)) * MLA_SCALE
    rla = _softmax_blocks([saw_ref, sac_ref], [paw_ref, pac_ref], _sink_column(sink_ref, QB))
    rlm = _softmax_blocks([smc_ref, sml_ref], [pmc_ref, pml_ref])
    for j in range(A_KV_HEADS):
        sl = slice(A_HD * j, A_HD * (j + 1))
        rows = slice(grp_rows * j, grp_rows * (j + 1))
        o4 = (_dot(paw_ref[rows, :], vwin[:, sl]) + _dot(pac_ref[rows, :], vctx[:, sl])) * rla[rows]
        for g in range(A_GROUP):
            h = A_GROUP * j + g
            merged_ref[:, A_HD * h:A_HD * (h + 1)] = o4[QB * g:QB * (g + 1)].astype(bf16)
    for h in range(MLA_HEADS):
        vs = slice(MLA_NN + MLA_V * h, MLA_NN + MLA_V * (h + 1))
        rows = slice(QB * h, QB * (h + 1))
        o = (_dot(pmc_ref[rows, :], kvc_ref[:, vs]) + _dot(pml_ref[rows, :], kvl_ref[:, vs])) * rlm[rows]
        merged_ref[:, vs] = o.astype(bf16)
    o_ref[...] = _mix_out_ln(merged_ref, wout_ref, x_ref[...], mods_ref, 1 + b, g_ref, b_ref)


def _den_attn(sink, qa, ka, va, cache_k, cache_v, qm, kvl, kr, cache_ckv, cache_kr, w_ukv_p, x_all, mods0, w_out,
              ln_g, ln_b):
    nq = DEN_L // QB
    qblk = lambda w: pl.BlockSpec((QB, w), lambda b, n: (T_CTX // QB + b * nq + n, 0))
    seq = lambda w: pl.BlockSpec((DEN_L, w), lambda b, n: (DEN_BLK0 + b, 0))
    cache = lambda w: pl.BlockSpec((1, CTX_L, w), lambda b, n: (b, 0, 0))
    return pl.pallas_call(
        _den_attn_kernel,
        grid=(N_DEN_B, nq),
        in_specs=[pl.BlockSpec(memory_space=pltpu.SMEM), qblk(512), seq(LANE), seq(LANE), cache(LANE), cache(LANE),
                  qblk(768), seq(1024), seq(LANE), cache(MLA_KV_RANK), cache(MLA_ROPE),
                  _full((MLA_KV_RANK, 1024), 2), pl.BlockSpec((QB, D), lambda b, n: (b * nq + n, 0)),
                  _full((8, 6 * D), 2), _full((D, D), 2), _full((1, D), 2),
                  _full((1, D), 2)],
        out_specs=pl.BlockSpec((QB, D), lambda b, n: (b * nq + n, 0)),
        out_shape=jax.ShapeDtypeStruct((T_DEN, D), f32),
        scratch_shapes=[pltpu.VMEM((QB, D), bf16), pltpu.VMEM((CTX_L, 1024), bf16)]
        + [pltpu.VMEM((A_HEADS * QB, w), dt) for dt in (f32, bf16) for w in (WIN, CTX_L, CTX_L, DEN_L)],
        compiler_params=_params(2),
        name="den_attn",
    )(sink, qa, ka, va, cache_k, cache_v, qm, kvl, kr, cache_ckv, cache_kr, w_ukv_p, x_all, mods0, w_out, ln_g, ln_b)


def _route(x1, mrow, rw_ref, rb_ref, h_ref, gates_ref):
    sh, sc = mrow[:, 3 * D:4 * D], mrow[:, 4 * D:5 * D]
    h = x1 * (1.0 + sc) + sh
    h_hi = h.astype(bf16)
    h_ref[...] = h_hi
    h_lo = (h - h_hi.astype(f32)).astype(bf16)
    logits = _dot(h_hi, rw_ref[0]) + (_dot(h_hi, rw_ref[1]) + _dot(h_lo, rw_ref[0]))
    scores = jax.nn.sigmoid(logits)
    lane = lax.broadcasted_iota(jnp.int32, scores.shape, 1).astype(f32)
    sel = jnp.where(lane < N_EXPERTS, scores + rb_ref[...], -jnp.inf)
    gates = jnp.zeros_like(scores)
    for _ in range(TOP_K):
        m = sel.max(axis=-1, keepdims=True)
        idx = jnp.where(sel == m, lane, float(LANE)).min(axis=-1, keepdims=True)
        hit = lane == idx
        gates = jnp.where(hit, scores, gates)
        sel = jnp.where(hit, -jnp.inf, sel)
    gates_ref[...] = gates / gates.sum(axis=-1, keepdims=True) * ROUTED_SCALE


def _router_kernel(xc_ref, xd_ref, mods_ref, rw_ref, rb_ref, x_ref, h_ref, gates_ref):
    i = pl.program_id(0)
    r = _mod_row(i, TOK_TILE)
    x1 = _pick(i, TOK_TILE, xc_ref, xd_ref)
    x_ref[...] = x1
    _route(x1, mods_ref[pl.ds(r, 1), :], rw_ref, rb_ref, h_ref, gates_ref)


def _router(x1c, x1d, mods_l, router_w_p, router_b_p):
    row_spec = lambda w: pl.BlockSpec((TOK_TILE, w), lambda i: (i, 0))
    xc_spec, xd_spec = _two_stream_specs(TOK_TILE, D)
    return pl.pallas_call(
        _router_kernel,
        grid=(T_ALL // TOK_TILE,),
        in_specs=[xc_spec, xd_spec, _full((8, 6 * D), 1), _full((2, D, LANE), 1), _full((1, LANE), 1)],
        out_specs=[row_spec(D), row_spec(D), row_spec(LANE)],
        out_shape=[jax.ShapeDtypeStruct((T_ALL, D), f32), jax.ShapeDtypeStruct((T_ALL, D), bf16),
                   jax.ShapeDtypeStruct((T_ALL, LANE), f32)],
        compiler_params=_params(1),
        name="router",
    )(x1c, x1d, mods_l, router_w_p, router_b_p)


MOE_TOK = 1536
MOE_EG = 4
MOE_TILE = 512
MOE_FF = MOE_EG * EXPERT_FF


def _moe_kernel(h_ref, gates_ref, x_ref, mods_ref, wg_ref, wu_ref, wd_ref, sg_ref, su_ref, sd_ref, g_ref, b_ref,
                o_ref):
    p = pl.program_id(0)
    e = pl.program_id(1)
    n_tiles = MOE_TOK // MOE_TILE

    def gate_f(t):
        r = _mod_row(p * n_tiles + t, MOE_TILE)
        return mods_ref[pl.ds(r, 1), 5 * D:6 * D]

    def rows_of(t):
        return pl.ds(pl.multiple_of(t * MOE_TILE, MOE_TILE), MOE_TILE)

    @pl.when(e == 0)
    def _():
        sg = sg_ref[...].astype(bf16)
        su = su_ref[...].astype(bf16)
        sd = sd_ref[...].astype(bf16)

        def body(t, c):
            rows = rows_of(t)
            ht = h_ref[rows, :]
            hid = _silu(_dot(ht, sg)) * _dot(ht, su)
            o_ref[rows, :] = ALPHA * x_ref[rows, :] + gate_f(t) * _dot(hid.astype(bf16), sd)
            return c

        lax.fori_loop(0, n_tiles, body, 0)

    wg = jnp.concatenate([wg_ref[k].astype(bf16) for k in range(MOE_EG)], axis=1)
    wu = jnp.concatenate([wu_ref[k].astype(bf16) for k in range(MOE_EG)], axis=1)
    wd = jnp.concatenate([wd_ref[k].astype(bf16) for k in range(MOE_EG)], axis=0)
    lane = lax.broadcasted_iota(jnp.int32, (MOE_TILE, LANE), 1)

    def body(t, c):
        rows = rows_of(t)
        ht = h_ref[rows, :]
        hid = _silu(_dot(ht, wg)) * _dot(ht, wu)
        gt = gates_ref[rows, :]
        parts = []
        for k in range(MOE_EG):
            col = jnp.where(lane == e * MOE_EG + k, gt, 0.0).sum(axis=-1, keepdims=True)
            parts.append((hid[:, EXPERT_FF * k:EXPERT_FF * (k + 1)] * col).astype(bf16))
        o_ref[rows, :] += gate_f(t) * _dot(jnp.concatenate(parts, axis=1), wd)
        return c

    lax.fori_loop(0, n_tiles, body, 0)

    @pl.when(e == pl.num_programs(1) - 1)
    def _():
        def body(t, c):
            rows = rows_of(t)
            o_ref[rows, :] = _layer_norm(o_ref[rows, :], g_ref[...], b_ref[...])
            return c

        lax.fori_loop(0, n_tiles, body, 0)


def _moe(l, h, gates, x1, mods_l, wg, wu, wd, sg, su, sd, ln_g, ln_b):
    tok = lambda w: pl.BlockSpec((MOE_TOK, w), lambda p, e: (p, 0))
    return pl.pallas_call(
        _moe_kernel,
        grid=(T_ALL // MOE_TOK, N_EXPERTS // MOE_EG),
        in_specs=[tok(D), tok(LANE), tok(D), _full((8, 6 * D), 2),
                  pl.BlockSpec((None, MOE_EG, D, EXPERT_FF), lambda p, e: (l, e, 0, 0)),
                  pl.BlockSpec((None, MOE_EG, D, EXPERT_FF), lambda p, e: (l, e, 0, 0)),
                  pl.BlockSpec((None, MOE_EG, EXPERT_FF, D), lambda p, e: (l, e, 0, 0)),
                  pl.BlockSpec((None, D, SHARED_FF), lambda p, e: (l, 0, 0)),
                  pl.BlockSpec((None, D, SHARED_FF), lambda p, e: (l, 0, 0)),
                  pl.BlockSpec((None, SHARED_FF, D), lambda p, e: (l, 0, 0)),
                  _full((1, D), 2), _full((1, D), 2)],
        out_specs=tok(D),
        out_shape=jax.ShapeDtypeStruct((T_ALL, D), f32),
        compiler_params=_params(2),
        name="moe",
    )(h, gates, x1, mods_l, wg, wu, wd, sg, su, sd, ln_g, ln_b)


def _router_weights(l, router_w, router_bias):
    rw = jnp.pad(router_w[l], ((0, 0), (0, LANE - N_EXPERTS)))
    rw_hi = rw.astype(bf16)
    rw_lo = (rw - rw_hi.astype(f32)).astype(bf16)
    rb = jnp.pad(router_bias[l], (0, LANE - N_EXPERTS)).reshape(1, LANE)
    return jnp.stack([rw_hi, rw_lo]), rb


S5_Q = 8
S5_NGB = D // LANE
S5_TILE = 256
S5_KT = S5_TILE // S5_Q


def _s5_in_kernel(x_ref, mods_ref, w_ref, u_ref, u2_ref, slab_ref, *, row_of):
    r = row_of(pl.program_id(0))
    mrow = mods_ref[pl.ds(r, 1), :]
    sh, sc = mrow[:, 0:D], mrow[:, D:2 * D]
    h = (x_ref[...] * (1.0 + sc) + sh).astype(bf16)
    u = _dot(h, w_ref[...])
    u_ref[...] = u
    for s in range(S5_NGB):
        slab_ref[s] = u[:, LANE * s:LANE * (s + 1)]
    for s in range(S5_NGB):
        for j in range(S5_Q):
            u2_ref[s, :, LANE * j:LANE * (j + 1)] = slab_ref[s, pl.ds(j, S5_KT, stride=S5_Q), :].astype(bf16)


def _s5_in(x_all, tile0, mods1, w_in_c, n_b, seq_len, row_of):
    tiles_per_seq = seq_len // S5_TILE
    return pl.pallas_call(
        functools.partial(_s5_in_kernel, row_of=row_of),
        grid=(n_b * tiles_per_seq,),
        in_specs=[pl.BlockSpec((S5_TILE, D), lambda i: (tile0 + i, 0)), _full((8, 6 * D), 1), _full((D, D), 1)],
        out_specs=[pl.BlockSpec((S5_TILE, D), lambda i: (i, 0)),
                   pl.BlockSpec((S5_NGB, S5_KT, D), lambda i: (0, i % tiles_per_seq, i // tiles_per_seq))],
        out_shape=[jax.ShapeDtypeStruct((n_b * seq_len, D), f32),
                   jax.ShapeDtypeStruct((S5_NGB, seq_len // S5_Q, n_b * D), bf16)],
        scratch_shapes=[pltpu.VMEM((S5_NGB, S5_TILE, LANE), f32)],
        compiler_params=_params(1),
        name="s5_in",
    )(x_all, mods1, w_in_c)


S5_GL = (LANE // S5_CH) * S5_P
S5_ROWS_C = (CTX_L // S5_Q) * N_CTX_B
S5_ROWS_D = (DEN_L // S5_Q) * N_DEN_B


def _split_bf16(a):
    hi = a.astype(bf16)
    return hi, (a - hi.astype(f32)).astype(bf16)


def _s5_scan_kernel(lre_ref, lim_ref, ldt_ref, btr_ref, bti_ref, ctr_ref, cti_ref, uc_ref, ud_ref, h0_ref,
                    yc_ref, yd_ref, st_ref, win_ref, mso_ref, wit_ref, a_ref, s_ref, hp_ref):
    gl = S5_GL
    rowg = lax.shift_right_logical(lax.broadcasted_iota(jnp.int32, (LANE, gl), 0), 4)
    colg = lax.shift_right_logical(lax.broadcasted_iota(jnp.int32, (LANE, gl), 1), 6)
    same_group = rowg == colg
    reps = LANE // S5_CH

    def expand(t):
        return jnp.where(same_group, jnp.concatenate([t] * reps, axis=0), 0.0)

    for d in range(2):
        fwd = d == 0
        lre, lim = lre_ref[d], lim_ref[d]
        dt = jnp.exp(ldt_ref[d])
        a, w = lre * dt, lim * dt
        pre = [jnp.exp(m * a) * jnp.cos(m * w) for m in range(S5_Q + 1)]
        pim = [jnp.exp(m * a) * jnp.sin(m * w) for m in range(S5_Q + 1)]
        xr, xi = pre[1] - 1.0, pim[1]
        den = lre * lre + lim * lim
        cfr, cfi = (xr * lre + xi * lim) / den, (xi * lre - xr * lim) / den
        btr, bti = btr_ref[d], bti_ref[d]
        bexp_r = expand(cfr * btr - cfi * bti)
        bexp_i = expand(cfr * bti + cfi * btr)
        cexp_r, cexp_i = expand(ctr_ref[d]), expand(cti_ref[d])
        for m in range(S5_Q + 1):
            a_ref[m, :, 0:gl] = cexp_r * pre[m] - cexp_i * pim[m]
            a_ref[m, :, gl:2 * gl] = -(cexp_r * pim[m] + cexp_i * pre[m])
        for j in range(S5_Q):
            m = S5_Q - 1 - j if fwd else j
            win_ref[LANE * j:LANE * (j + 1), 0:gl] = (pre[m] * bexp_r - pim[m] * bexp_i).astype(bf16)
            win_ref[LANE * j:LANE * (j + 1), gl:2 * gl] = (pre[m] * bexp_i + pim[m] * bexp_r).astype(bf16)
        for j in range(S5_Q):
            m = j + 1 if fwd else S5_Q - j
            mso_ref[LANE * j:LANE * (j + 1), :] = a_ref[m].astype(bf16)
        b2_hi, b2_lo = _split_bf16(jnp.concatenate([bexp_r, bexp_i], axis=1))
        kt = []
        for tau in range(S5_Q):
            a_hi, a_lo = _split_bf16(a_ref[tau])
            kt.append((_dot_nt(b2_hi, a_hi) + _dot_nt(b2_hi, a_lo) + _dot_nt(b2_lo, a_hi)).astype(bf16))
        zero_blk = jnp.zeros((LANE, LANE), bf16)
        for j in range(S5_Q):
            for jp in range(S5_Q):
                tau = jp - j if fwd else j - jp
                wit_ref[LANE * j:LANE * (j + 1), LANE * jp:LANE * (jp + 1)] = kt[tau] if tau >= 0 else zero_blk

        l8r, l8i = pre[S5_Q], pim[S5_Q]

        def run(u_ref, y_ref, n_b, n_k, h_init):
            rows = n_b * n_k
            s_ref[0:rows, :] = _dot(u_ref[0], win_ref[...])

            per_it = max(1, 8 // n_b)
            n_it = n_k // per_it
            it_rows = per_it * n_b

            def step(i, carry):
                hr, hi_ = carry
                it = i if fwd else n_it - 1 - i
                rs = pl.ds(pl.multiple_of(it * it_rows, it_rows), it_rows)
                s_tile = s_ref[rs, :]
                prev_r, prev_i = [None] * per_it, [None] * per_it
                for sub in (range(per_it) if fwd else reversed(range(per_it))):
                    prev_r[sub], prev_i[sub] = hr, hi_
                    sr = s_tile[sub * n_b:(sub + 1) * n_b, 0:gl]
                    si = s_tile[sub * n_b:(sub + 1) * n_b, gl:2 * gl]
                    hr, hi_ = l8r * hr - l8i * hi_ + sr, l8r * hi_ + l8i * hr + si
                hp_ref[rs, 0:gl] = jnp.concatenate(prev_r, axis=0) if per_it > 1 else prev_r[0]
                hp_ref[rs, gl:2 * gl] = jnp.concatenate(prev_i, axis=0) if per_it > 1 else prev_i[0]
                return hr, hi_

            h_fin = lax.fori_loop(0, n_it, step, h_init)
            y = _dot_nt(hp_ref[0:rows, :].astype(bf16), mso_ref[...]) + _dot(u_ref[0], wit_ref[...])
            if fwd:
                y_ref[0] = y
            else:
                y_ref[0] += y
            return h_fin

        zeros = jnp.zeros((N_CTX_B, gl), f32)
        hr, hi_ = run(uc_ref, yc_ref, N_CTX_B, CTX_L // S5_Q, (zeros, zeros))
        st_ref[d, 0] = hr
        st_ref[d, 1] = hi_
        run(ud_ref, yd_ref, N_DEN_B, DEN_L // S5_Q, (h0_ref[d, 0], h0_ref[d, 1]))


def _s5_scan(lam_re, lam_im, log_dt, bt_re, bt_im, ct_re, ct_im, u2c, u2d, h0):
    gl = S5_GL
    vec = pl.BlockSpec((2, 1, gl), lambda g: (0, 0, g))
    tab = pl.BlockSpec((2, S5_CH, gl), lambda g: (0, 0, g))
    rows = lambda n: pl.BlockSpec((1, n, D), lambda g: (g, 0, 0))
    return pl.pallas_call(
        _s5_scan_kernel,
        grid=(S5_NGB,),
        in_specs=[vec, vec, vec, tab, tab, tab, tab, rows(S5_ROWS_C), rows(S5_ROWS_D),
                  pl.BlockSpec((2, 2, N_DEN_B, gl), lambda g: (0, 0, 0, g))],
        out_specs=[rows(S5_ROWS_C), rows(S5_ROWS_D), pl.BlockSpec((2, 2, N_CTX_B, gl), lambda g: (0, 0, 0, g))],
        out_shape=[jax.ShapeDtypeStruct((S5_NGB, S5_ROWS_C, D), f32), jax.ShapeDtypeStruct((S5_NGB, S5_ROWS_D, D), f32),
                   jax.ShapeDtypeStruct((2, 2, N_CTX_B, S5_G * S5_P), f32)],
        scratch_shapes=[pltpu.VMEM((D, 2 * gl), bf16), pltpu.VMEM((D, 2 * gl), bf16), pltpu.VMEM((D, D), bf16),
                        pltpu.VMEM((S5_Q + 1, LANE, 2 * gl), f32), pltpu.VMEM((S5_ROWS_C, 2 * gl), f32),
                        pltpu.VMEM((S5_ROWS_C, 2 * gl), f32)],
        compiler_params=_params(1),
        name="s5_scan",
    )(lam_re, lam_im, log_dt, bt_re, bt_im, ct_re, ct_im, u2c, u2d, h0)


S5_CTX_TILES = T_CTX // S5_TILE
S5_DEN_TPS = DEN_L // S5_TILE


def _gelu_tanh(x):
    return 0.5 * x * (1.0 + jnp.tanh(np.sqrt(2.0 / np.pi).astype(np.float32) * (x + 0.044715 * (x * x * x))))


def _s5_out_kernel(x_ref, uc_ref, ud_ref, yc_ref, yd_ref, mods_ref, dsk_ref, wout_ref, g_ref, b_ref, rw_ref, rb_ref,
                   x1_ref, h_ref, gates_ref, slab_ref):
    i = pl.program_id(0)
    is_ctx = i < S5_CTX_TILES
    r = _mod_row(i, S5_TILE)
    mrow = mods_ref[pl.ds(r, 1), :]
    u = _pick(i, S5_TILE, uc_ref, ud_ref)

    def unchunk(y_ref):
        for s in range(S5_NGB):
            for j in range(S5_Q):
                slab_ref[s, pl.ds(j, S5_KT, stride=S5_Q), :] = y_ref[s, :, LANE * j:LANE * (j + 1)]

    lax.cond(is_ctx, lambda: unchunk(yc_ref), lambda: unchunk(yd_ref))
    y = jnp.concatenate([slab_ref[s] for s in range(S5_NGB)], axis=1) + dsk_ref[...] * u
    z = _dot(_gelu_tanh(y).astype(bf16), wout_ref[...])
    out = z[:, 0:D] * jax.nn.sigmoid(z[:, D:2 * D])
    x1 = _layer_norm(ALPHA * x_ref[...] + mrow[:, 2 * D:3 * D] * out, g_ref[...], b_ref[...])
    x1_ref[...] = x1
    _route(x1, mrow, rw_ref, rb_ref, h_ref, gates_ref)


def _s5_out(x_all, uc, ud, yc, yd, mods1, d_skip, w_out_c, ln_g, ln_b, rw, rb):
    row_spec = lambda w: pl.BlockSpec((S5_TILE, w), lambda i: (i, 0))
    ci = lambda i: jnp.minimum(i, S5_CTX_TILES - 1)
    di = lambda i: jnp.maximum(i - S5_CTX_TILES, 0)
    return pl.pallas_call(
        _s5_out_kernel,
        grid=(T_ALL // S5_TILE,),
        in_specs=[row_spec(D),
                  pl.BlockSpec((S5_TILE, D), lambda i: (ci(i), 0)),
                  pl.BlockSpec((S5_TILE, D), lambda i: (di(i), 0)),
                  pl.BlockSpec((S5_NGB, S5_KT, D), lambda i: (0, 0, ci(i))),
                  pl.BlockSpec((S5_NGB, S5_KT, D), lambda i: (0, di(i) % S5_DEN_TPS, di(i) // S5_DEN_TPS)),
                  _full((8, 6 * D), 1), _full((1, D), 1), _full((D, 2 * D), 1), _full((1, D), 1), _full((1, D), 1),
                  _full((2, D, LANE), 1), _full((1, LANE), 1)],
        out_specs=[row_spec(D), row_spec(D), row_spec(LANE)],
        out_shape=[jax.ShapeDtypeStruct((T_ALL, D), f32), jax.ShapeDtypeStruct((T_ALL, D), bf16),
                   jax.ShapeDtypeStruct((T_ALL, LANE), f32)],
        scratch_shapes=[pltpu.VMEM((S5_NGB, S5_TILE, LANE), f32)],
        compiler_params=_params(1),
        name="s5_out",
    )(x_all, uc, ud, yc, yd, mods1, d_skip, w_out_c, ln_g, ln_b, rw, rb)


def kernel(x_prompt, x_sample, c, cache_attn_k, cache_attn_v, cache_mla_ckv, cache_mla_krope, state_ssm, c_ctx,
           ada_w, ada_b, ln_mix_g, ln_mix_b, ln_ffn_g, ln_ffn_b, w_in_ab, attn_sink, mla_q_norm, mla_kv_norm,
           mla_w_uq, mla_w_ukv, w_out_ab, w_in_c, s5_lam_re, s5_lam_im, s5_log_dt, s5_b_re, s5_b_im, s5_c_re,
           s5_c_im, s5_d, w_out_c, router_w, router_bias, exp_w_gate, exp_w_up, exp_w_down, sh_w_gate, sh_w_up,
           sh_w_down):
    row = lambda v: v.reshape(1, -1)
    xc, xd = x_prompt.reshape(T_CTX, D), x_sample.reshape(T_DEN, D)
    cvec8 = jnp.concatenate([c_ctx[None, :], c, jnp.zeros((8 - 1 - N_DEN_B, D), f32)], axis=0)
    mods = _adaln(cvec8, ada_w, ada_b)

    w_in_p = jnp.pad(w_in_ab[0], ((0, 0), (0, PROJ_W - w_in_ab.shape[-1]))).astype(bf16)
    uq = mla_w_uq[0].reshape(MLA_Q_RANK, MLA_HEADS, MLA_NOPE + MLA_ROPE)
    w_uq_p = jnp.concatenate([uq[:, :, :MLA_NOPE].reshape(MLA_Q_RANK, -1), uq[:, :, MLA_NOPE:].reshape(MLA_Q_RANK, -1)],
                             axis=1).astype(bf16)
    ukv = mla_w_ukv[0].reshape(MLA_KV_RANK, MLA_HEADS, MLA_NOPE + MLA_V)
    w_ukv_p = jnp.concatenate([ukv[:, :, :MLA_NOPE].reshape(MLA_KV_RANK, -1),
                               ukv[:, :, MLA_NOPE:].reshape(MLA_KV_RANK, -1)], axis=1).astype(bf16)
    qa, ka, va, ckv, kr, qm, kvl = _ab_proj(xc, xd, mods[0], w_in_p, row(mla_q_norm[0]), row(mla_kv_norm[0]),
                                            w_uq_p, w_ukv_p)
    w_out_b = w_out_ab[0].astype(bf16)
    g0, b0 = row(ln_mix_g[0]), row(ln_mix_b[0])
    x1c = _ctx_attn(attn_sink[0], qa, ka, va, qm, kvl, kr, xc, mods[0], w_out_b, g0, b0)
    x1d = _den_attn(attn_sink[0], qa, ka, va,
                    cache_attn_k[:, 0].reshape(N_DEN_B, CTX_L, A_KV_HEADS * A_HD),
                    cache_attn_v[:, 0].reshape(N_DEN_B, CTX_L, A_KV_HEADS * A_HD),
                    qm, kvl, kr, cache_mla_ckv[:, 0], cache_mla_krope[:, 0], w_ukv_p, xd, mods[0], w_out_b, g0, b0)
    rw0, rb0 = _router_weights(0, router_w, router_bias)
    x1, h, gates = _router(x1c, x1d, mods[0], rw0, rb0)
    x2 = _moe(0, h, gates, x1, mods[0], exp_w_gate, exp_w_up, exp_w_down, sh_w_gate, sh_w_up, sh_w_down,
              row(ln_ffn_g[0]), row(ln_ffn_b[0]))

    w_in_c_b = w_in_c[0].astype(bf16)
    uc, u2c = _s5_in(x2, 0, mods[1], w_in_c_b, N_CTX_B, CTX_L, lambda i: 0)
    ud, u2d = _s5_in(x2, S5_CTX_TILES, mods[1], w_in_c_b, N_DEN_B, DEN_L, lambda i: 1 + i // S5_DEN_TPS)
    gp = S5_G * S5_P
    chan_major_b = lambda t: jnp.transpose(t[0], (0, 3, 1, 2)).reshape(2, S5_CH, gp)
    chan_major_c = lambda t: jnp.transpose(t[0], (0, 2, 1, 3)).reshape(2, S5_CH, gp)
    h0 = jnp.transpose(state_ssm[:, 0], (1, 2, 0, 3, 4)).reshape(2, 2, N_DEN_B, gp)
    yc, yd, st = _s5_scan(s5_lam_re[0].reshape(2, 1, gp), s5_lam_im[0].reshape(2, 1, gp),
                          jnp.repeat(s5_log_dt[0], S5_P, axis=-1).reshape(2, 1, gp),
                          chan_major_b(s5_b_re), chan_major_b(s5_b_im), chan_major_c(s5_c_re), chan_major_c(s5_c_im),
                          u2c.reshape(S5_NGB, S5_ROWS_C, D), u2d.reshape(S5_NGB, S5_ROWS_D, D), h0)
    rw1, rb1 = _router_weights(1, router_w, router_bias)
    x3, h, gates = _s5_out(x2, uc, ud, yc.reshape(S5_NGB, CTX_L // S5_Q, N_CTX_B * D),
                           yd.reshape(S5_NGB, DEN_L // S5_Q, N_DEN_B * D), mods[1], row(s5_d[0]),
                           w_out_c[0].astype(bf16), row(ln_mix_g[1]), row(ln_mix_b[1]), rw1, rb1)
    x4 = _moe(1, h, gates, x3, mods[1], exp_w_gate, exp_w_up, exp_w_down, sh_w_gate, sh_w_up, sh_w_down,
              row(ln_ffn_g[1]), row(ln_ffn_b[1]))

    y_prompt = x4[:T_CTX].reshape(N_CTX_B, CTX_L, D)
    y_sample = x4[T_CTX:].reshape(N_DEN_B, DEN_L, D)
    new_attn_k = ka[:T_CTX].reshape(N_CTX_B, 1, CTX_L, A_KV_HEADS, A_HD)
    new_attn_v = va[:T_CTX].reshape(N_CTX_B, 1, CTX_L, A_KV_HEADS, A_HD)
    new_mla_ckv = ckv[:T_CTX].reshape(N_CTX_B, 1, CTX_L, MLA_KV_RANK)
    new_mla_krope = kr[:T_CTX, :MLA_ROPE].reshape(N_CTX_B, 1, CTX_L, MLA_ROPE)
    new_state_ssm = jnp.transpose(st, (2, 0, 1, 3)).reshape(N_CTX_B, 1, 2, 2, S5_G, S5_P)
    return (y_prompt, y_sample, new_attn_k, new_attn_v, new_mla_ckv, new_mla_krope, new_state_ssm)
```

```python
import functools

import jax
import jax.numpy as jnp
import numpy as np
from jax import lax
from jax.experimental import pallas as pl
from jax.experimental.pallas import tpu as pltpu

f32 = jnp.float32
bf16 = jnp.bfloat16

D = 1024
N_CTX_B, CTX_L = 16, 256
N_DEN_B, DEN_L = 2, 1024
T_CTX = N_CTX_B * CTX_L
T_DEN = N_DEN_B * DEN_L
T_ALL = T_CTX + T_DEN
GRID_W = 64
WINDOW = 128
ROPE_BASE = 10000.0
A_HEADS, A_KV_HEADS, A_HD = 8, 2, 64
A_GROUP = A_HEADS // A_KV_HEADS
A_SCALE = A_HD ** -0.5
MLA_HEADS, MLA_Q_RANK, MLA_KV_RANK = 8, 256, 128
MLA_NOPE, MLA_ROPE, MLA_V = 64, 32, 64
MLA_SCALE = (MLA_NOPE + MLA_ROPE) ** -0.5
N_EXPERTS, TOP_K, EXPERT_FF, SHARED_FF = 64, 6, 128, 128
ROUTED_SCALE = 2.5
DEPTH = 2
ALPHA = (2.0 * DEPTH) ** 0.25
LN_EPS = 1e-5
RMS_EPS = 1e-6
NEG_INF = -1e30
S5_G, S5_CH, S5_P = 64, 16, 64

LANE = 128
VMEM_LIMIT = 56 * 1024 * 1024

TOK_TILE = 512


def _mod_row(tile_idx, tile_rows):
    start = tile_idx * tile_rows
    return jnp.where(start < T_CTX, 0, 1 + (start - T_CTX) // DEN_L)


def _layer_norm(y, g, b):
    mu = jnp.mean(y, axis=-1, keepdims=True)
    yc = y - mu
    var = jnp.mean(yc * yc, axis=-1, keepdims=True)
    return yc * lax.rsqrt(var + LN_EPS) * g + b


def _silu(x):
    return x * jax.nn.sigmoid(x)


def _dot(a, b):
    return jnp.dot(a, b, preferred_element_type=f32)


def _dot_nt(a, b):
    return lax.dot_general(a, b, (((1,), (1,)), ((), ())), preferred_element_type=f32)


def _dot_exact(a, b):
    return jnp.dot(a, b, preferred_element_type=f32, precision=lax.Precision.HIGHEST)


def _full(shape, n_grid):
    zeros = tuple(0 for _ in shape)
    return pl.BlockSpec(shape, lambda *_: zeros)


def _two_stream_specs(tile_rows, width):
    n_ctx = T_CTX // tile_rows
    return (pl.BlockSpec((tile_rows, width), lambda i: (jnp.minimum(i, n_ctx - 1), 0)),
            pl.BlockSpec((tile_rows, width), lambda i: (jnp.maximum(i - n_ctx, 0), 0)))


def _pick(i, tile_rows, ctx_ref, den_ref):
    return lax.cond(i < T_CTX // tile_rows, lambda: ctx_ref[...], lambda: den_ref[...])


def _params(n_grid):
    return pltpu.CompilerParams(dimension_semantics=("arbitrary",) * n_grid, vmem_limit_bytes=VMEM_LIMIT)


ADA_TN = 1536


def _adaln_kernel(c_ref, w_ref, b_ref, o_ref):
    s = _silu(c_ref[...])
    o_ref[0] = _dot_exact(s, w_ref[0]) + b_ref[0]


def _adaln(cvec8, ada_w, ada_b):
    n = 6 * D
    return pl.pallas_call(
        _adaln_kernel,
        grid=(DEPTH, n // ADA_TN),
        in_specs=[
            pl.BlockSpec((8, D), lambda l, j: (0, 0)),
            pl.BlockSpec((1, D, ADA_TN), lambda l, j: (l, 0, j)),
            pl.BlockSpec((1, 1, ADA_TN), lambda l, j: (l, 0, j)),
        ],
        out_specs=pl.BlockSpec((1, 8, ADA_TN), lambda l, j: (l, 0, j)),
        out_shape=jax.ShapeDtypeStruct((DEPTH, 8, n), f32),
        compiler_params=_params(2),
        name="adaln",
    )(cvec8, ada_w, ada_b.reshape(DEPTH, 1, n))


def _rope_table_array(head_dim):
    q = head_dim // 4
    pos = np.arange(DEN_L)
    row, col = (pos // GRID_W).astype(np.float64), (pos % GRID_W).astype(np.float64)
    lane = np.arange(LANE) % head_dim
    is_col = lane >= head_dim // 2
    w = lane % (head_dim // 2)
    first = w < q
    inv_freq = ROPE_BASE ** (-np.arange(q, dtype=np.float64) / q)
    ang = np.where(is_col[None, :], col[:, None], row[:, None]) * inv_freq[w % q][None, :]
    cos, sin = np.cos(ang), np.sin(ang)
    sin_a = np.where(first[None, :], -sin, 0.0)
    sin_b = np.where(first[None, :], 0.0, sin)
    ident = np.stack([np.ones((TOK_TILE, LANE)), np.zeros((TOK_TILE, LANE)), np.zeros((TOK_TILE, LANE))])
    tab = np.concatenate([ident, np.stack([cos, sin_a, sin_b])], axis=1).astype(np.float32)
    return jnp.asarray(tab), q


def _rope_chunk(x, tab_ref, q):
    return x * tab_ref[0] + pltpu.roll(x, LANE - q, 1) * tab_ref[1] + pltpu.roll(x, q, 1) * tab_ref[2]


PROJ_W = 1280
C_QA, C_KA, C_VA, C_CQ, C_CKV, C_KR = 0, 512, 640, 768, 1024, 1152
MLA_NN = MLA_HEADS * MLA_NOPE


def _ab_proj_kernel(xc_ref, xd_ref, mods_ref, w_ref, qn_ref, kvn_ref, wuq_ref, wukv_ref, ta_ref, tm_ref,
                    qa_ref, ka_ref, va_ref, ckv_ref, kr_ref, qm_ref, kvl_ref, *, qa_shift, qm_shift):
    i = pl.program_id(0)
    r = _mod_row(i, TOK_TILE)
    mrow = mods_ref[pl.ds(r, 1), :]
    sh, sc = mrow[:, 0:D], mrow[:, D:2 * D]
    x = _pick(i, TOK_TILE, xc_ref, xd_ref)
    h = (x * (1.0 + sc) + sh).astype(bf16)
    proj = _dot(h, w_ref[...])
    for j in range(4):
        c0 = C_QA + LANE * j
        qa_ref[:, LANE * j:LANE * (j + 1)] = _rope_chunk(proj[:, c0:c0 + LANE], ta_ref, qa_shift).astype(bf16)
    ka_ref[...] = _rope_chunk(proj[:, C_KA:C_KA + LANE], ta_ref, qa_shift)
    va_ref[...] = proj[:, C_VA:C_VA + LANE]
    cq = proj[:, C_CQ:C_CQ + MLA_Q_RANK]
    cq = cq * lax.rsqrt(jnp.mean(cq * cq, axis=-1, keepdims=True) + RMS_EPS) * qn_ref[...]
    ckv = proj[:, C_CKV:C_CKV + MLA_KV_RANK]
    ckv = ckv * lax.rsqrt(jnp.mean(ckv * ckv, axis=-1, keepdims=True) + RMS_EPS) * kvn_ref[...]
    ckv_ref[...] = ckv
    kr_ref[...] = _rope_chunk(proj[:, C_KR:C_KR + LANE], tm_ref, qm_shift)
    qm = _dot(cq.astype(bf16), wuq_ref[...])
    qm_ref[:, 0:MLA_NN] = qm[:, 0:MLA_NN].astype(bf16)
    for j in range(2):
        c0 = MLA_NN + LANE * j
        qm_ref[:, c0:c0 + LANE] = _rope_chunk(qm[:, c0:c0 + LANE], tm_ref, qm_shift).astype(bf16)
    kvl_ref[...] = _dot(ckv.astype(bf16), wukv_ref[...]).astype(bf16)


def _rope_block_index(i):
    tiles_ctx = T_CTX // TOK_TILE
    per_seq = DEN_L // TOK_TILE
    return jnp.where(i < tiles_ctx, 0, 1 + (i - tiles_ctx) % per_seq)


def _ab_proj(xc, xd, mods0, w_in_p, q_norm, kv_norm, w_uq_p, w_ukv_p):
    tab_a, qa_shift = _rope_table_array(A_HD)
    tab_m, qm_shift = _rope_table_array(MLA_ROPE)
    row_spec = lambda w: pl.BlockSpec((TOK_TILE, w), lambda i: (i, 0))
    xc_spec, xd_spec = _two_stream_specs(TOK_TILE, D)
    tab_spec = pl.BlockSpec((3, TOK_TILE, LANE), lambda i: (0, _rope_block_index(i), 0))
    outs = [(512, bf16), (LANE, f32), (LANE, f32), (LANE, f32), (LANE, f32), (768, bf16), (1024, bf16)]
    return pl.pallas_call(
        functools.partial(_ab_proj_kernel, qa_shift=qa_shift, qm_shift=qm_shift),
        grid=(T_ALL // TOK_TILE,),
        in_specs=[xc_spec, xd_spec, _full((8, 6 * D), 1), _full((D, PROJ_W), 1), _full((1, MLA_Q_RANK), 1),
                  _full((1, MLA_KV_RANK), 1), _full((MLA_Q_RANK, 768), 1), _full((MLA_KV_RANK, 1024), 1),
                  tab_spec, tab_spec],
        out_specs=[row_spec(w) for w, _ in outs],
        out_shape=[jax.ShapeDtypeStruct((T_ALL, w), dt) for w, dt in outs],
        compiler_params=_params(1),
        name="ab_proj",
    )(xc, xd, mods0, w_in_p, q_norm, kv_norm, w_uq_p, w_ukv_p, tab_a, tab_m)


def _softmax_blocks(s_refs, p_refs, sink_col=None):
    m = s_refs[0][...].max(axis=-1, keepdims=True)
    for s_ref in s_refs[1:]:
        m = jnp.maximum(m, s_ref[...].max(axis=-1, keepdims=True))
    if sink_col is not None:
        m = jnp.maximum(m, sink_col)
    l = None
    for s_ref, p_ref in zip(s_refs, p_refs):
        p = jnp.exp(s_ref[...] - m)
        p_ref[...] = p.astype(bf16)
        ps = p.sum(axis=-1, keepdims=True)
        l = ps if l is None else l + ps
    if sink_col is not None:
        l = l + jnp.exp(sink_col - m)
    return 1.0 / l


def _sink_column(sink_ref, rows_per_head):
    return jnp.concatenate([jnp.full((rows_per_head, 1), sink_ref[h], f32) for h in range(A_HEADS)], axis=0)


def _mix_out_ln(merged_ref, wout_ref, x, mods_ref, r, g_ref, b_ref):
    out = _dot(merged_ref[...], wout_ref[...])
    gate = mods_ref[pl.ds(r, 1), 2 * D:3 * D]
    return _layer_norm(ALPHA * x + gate * out, g_ref[...], b_ref[...])


def _ctx_attn_kernel(sink_ref, qa_ref, ka_ref, va_ref, qm_ref, kvl_ref, kr_ref, x_ref, mods_ref, wout_ref,
                     g_ref, b_ref, o_ref, merged_ref, sa_ref, sm_ref, pa_ref, pm_ref):
    n = CTX_L
    ka = ka_ref[...].astype(bf16)
    va = va_ref[...].astype(bf16)
    for j in range(A_KV_HEADS):
        q4 = jnp.concatenate([qa_ref[:, A_HD * h:A_HD * (h + 1)] for h in range(A_GROUP * j, A_GROUP * (j + 1))],
                             axis=0)
        sa_ref[A_GROUP * n * j:A_GROUP * n * (j + 1), :] = _dot_nt(q4, ka[:, A_HD * j:A_HD * (j + 1)]) * A_SCALE
    kr = kr_ref[:, 0:MLA_ROPE].astype(bf16)
    for h in range(MLA_HEADS):
        qn = qm_ref[:, MLA_NOPE * h:MLA_NOPE * (h + 1)]
        qr = qm_ref[:, MLA_NN + MLA_ROPE * h:MLA_NN + MLA_ROPE * (h + 1)]
        kn = kvl_ref[:, MLA_NOPE * h:MLA_NOPE * (h + 1)]
        sm_ref[n * h:n * (h + 1), :] = (_dot_nt(qn, kn) + _dot_nt(qr, kr)) * MLA_SCALE
    rla = _softmax_blocks([sa_ref], [pa_ref], _sink_column(sink_ref, n))
    rlm = _softmax_blocks([sm_ref], [pm_ref])
    for j in range(A_KV_HEADS):
        rows = slice(A_GROUP * n * j, A_GROUP * n * (j + 1))
        o4 = _dot(pa_ref[rows, :], va[:, A_HD * j:A_HD * (j + 1)]) * rla[rows]
        for g in range(A_GROUP):
            h = A_GROUP * j + g
            merged_ref[:, A_HD * h:A_HD * (h + 1)] = o4[n * g:n * (g + 1)].astype(bf16)
    for h in range(MLA_HEADS):
        rows = slice(n * h, n * (h + 1))
        v = kvl_ref[:, MLA_NN + MLA_V * h:MLA_NN + MLA_V * (h + 1)]
        merged_ref[:, MLA_NN + MLA_V * h:MLA_NN + MLA_V * (h + 1)] = (_dot(pm_ref[rows, :], v) * rlm[rows]).astype(bf16)
    o_ref[...] = _mix_out_ln(merged_ref, wout_ref, x_ref[...], mods_ref, 0, g_ref, b_ref)


def _ctx_attn(sink, qa, ka, va, qm, kvl, kr, x_all, mods0, w_out, ln_g, ln_b):
    blk = lambda w: pl.BlockSpec((CTX_L, w), lambda b: (b, 0))
    return pl.pallas_call(
        _ctx_attn_kernel,
        grid=(N_CTX_B,),
        in_specs=[pl.BlockSpec(memory_space=pltpu.SMEM), blk(512), blk(LANE), blk(LANE), blk(768), blk(1024),
                  blk(LANE), blk(D), _full((8, 6 * D), 1), _full((D, D), 1), _full((1, D), 1), _full((1, D), 1)],
        out_specs=blk(D),
        out_shape=jax.ShapeDtypeStruct((T_CTX, D), f32),
        scratch_shapes=[pltpu.VMEM((CTX_L, D), bf16),
                        pltpu.VMEM((A_HEADS * CTX_L, CTX_L), f32), pltpu.VMEM((MLA_HEADS * CTX_L, CTX_L), f32),
                        pltpu.VMEM((A_HEADS * CTX_L, CTX_L), bf16), pltpu.VMEM((MLA_HEADS * CTX_L, CTX_L), bf16)],
        compiler_params=_params(1),
        name="ctx_attn",
    )(sink, qa, ka, va, qm, kvl, kr, x_all, mods0, w_out, ln_g, ln_b)


QB = 128
WIN = 3 * QB
DEN_BLK0 = T_CTX // DEN_L


def _den_attn_kernel(sink_ref, qa_ref, ka_ref, va_ref, cak_ref, cav_ref, qm_ref, kvl_ref, kr_ref, cckv_ref, ckr_ref,
                     wukv_ref, x_ref, mods_ref, wout_ref, g_ref, b_ref, o_ref, merged_ref, kvc_ref,
                     saw_ref, sac_ref, smc_ref, sml_ref, paw_ref, pac_ref, pmc_ref, pml_ref):
    b = pl.program_id(0)
    n = pl.program_id(1)

    @pl.when(n == 0)
    def _():
        kvc_ref[...] = _dot(cckv_ref[0].astype(bf16), wukv_ref[...]).astype(bf16)

    start = pl.multiple_of(jnp.clip(QB * (n - 1), 0, DEN_L - WIN), QB)
    grp_rows = A_GROUP * QB
    qpos = QB * n + (lax.broadcasted_iota(jnp.int32, (grp_rows, WIN), 0) & (QB - 1))
    kpos = start + lax.broadcasted_iota(jnp.int32, (grp_rows, WIN), 1)
    valid = jnp.abs(qpos - kpos) <= WINDOW
    kwin = ka_ref[pl.ds(start, WIN), :].astype(bf16)
    vwin = va_ref[pl.ds(start, WIN), :].astype(bf16)
    kctx = cak_ref[0].astype(bf16)
    vctx = cav_ref[0].astype(bf16)
    for j in range(A_KV_HEADS):
        sl = slice(A_HD * j, A_HD * (j + 1))
        rows = slice(grp_rows * j, grp_rows * (j + 1))
        q4 = jnp.concatenate([qa_ref[:, A_HD * h:A_HD * (h + 1)] for h in range(A_GROUP * j, A_GROUP * (j + 1))],
                             axis=0)
        saw_ref[rows, :] = jnp.where(valid, _dot_nt(q4, kwin[:, sl]) * A_SCALE, NEG_INF)
        sac_ref[rows, :] = _dot_nt(q4, kctx[:, sl]) * A_SCALE
    kr_lat = kr_ref[:, 0:MLA_ROPE].astype(bf16)
    kr_ctx = ckr_ref[0].astype(bf16)
    for h in range(MLA_HEADS):
        qn = qm_ref[:, MLA_NOPE * h:MLA_NOPE * (h + 1)]
        qr = qm_ref[:, MLA_NN + MLA_ROPE * h:MLA_NN + MLA_ROPE * (h + 1)]
        ns = slice(MLA_NOPE * h, MLA_NOPE * (h + 1))
        rows = slice(QB * h, QB * (h + 1))
        smc_ref[rows, :] = (_dot_nt(qn, kvc_ref[:, ns]) + _dot_nt(qr, kr_ctx)) * MLA_SCALE
        sml_ref[rows, :] = (_dot_nt(qn, kvl_ref[:, ns]) + _dot_nt(qr, kr_lat)) * MLA_SCALE
    rla = _softmax_blocks([saw_ref, sac_ref], [paw_ref, pac_ref], _sink_column(sink_ref, QB))
    rlm = _softmax_blocks([smc_ref, sml_ref], [pmc_ref, pml_ref])
    for j in range(A_KV_HEADS):
        sl = slice(A_HD * j, A_HD * (j + 1))
        rows = slice(grp_rows * j, grp_rows * (j + 1))
        o4 = (_dot(paw_ref[rows, :], vwin[:, sl]) + _dot(pac_ref[rows, :], vctx[:, sl])) * rla[rows]
        for g in range(A_GROUP):
            h = A_GROUP * j + g
            merged_ref[:, A_HD * h:A_HD * (h + 1)] = o4[QB * g:QB * (g + 1)].astype(bf16)
    for h in range(MLA_HEADS):
        vs = slice(MLA_NN + MLA_V * h, MLA_NN + MLA_V * (h + 1))
        rows = slice(QB * h, QB * (h + 1))
        o = (_dot(pmc_ref[rows, :], kvc_ref[:, vs]) + _dot(pml_ref[rows, :], kvl_ref[:, vs])) * rlm[rows]
        merged_ref[:, vs] = o.astype(bf16)
    o_ref[...] = _mix_out_ln(merged_ref, wout_ref, x_ref[...], mods_ref, 1 + b, g_ref, b_ref)


def _den_attn(sink, qa, ka, va, cache_k, cache_v, qm, kvl, kr, cache_ckv, cache_kr, w_ukv_p, x_all, mods0, w_out,
              ln_g, ln_b):
    nq = DEN_L // QB
    qblk = lambda w: pl.BlockSpec((QB, w), lambda b, n: (T_CTX // QB + b * nq + n, 0))
    seq = lambda w: pl.BlockSpec((DEN_L, w), lambda b, n: (DEN_BLK0 + b, 0))
    cache = lambda w: pl.BlockSpec((1, CTX_L, w), lambda b, n: (b, 0, 0))
    return pl.pallas_call(
        _den_attn_kernel,
        grid=(N_DEN_B, nq),
        in_specs=[pl.BlockSpec(memory_space=pltpu.SMEM), qblk(512), seq(LANE), seq(LANE), cache(LANE), cache(LANE),
                  qblk(768), seq(1024), seq(LANE), cache(MLA_KV_RANK), cache(MLA_ROPE),
                  _full((MLA_KV_RANK, 1024), 2), pl.BlockSpec((QB, D), lambda b, n: (b * nq + n, 0)),
                  _full((8, 6 * D), 2), _full((D, D), 2), _full((1, D), 2),
                  _full((1, D), 2)],
        out_specs=pl.BlockSpec((QB, D), lambda b, n: (b * nq + n, 0)),
        out_shape=jax.ShapeDtypeStruct((T_DEN, D), f32),
        scratch_shapes=[pltpu.VMEM((QB, D), bf16), pltpu.VMEM((CTX_L, 1024), bf16)]
        + [pltpu.VMEM((A_HEADS * QB, w), dt) for dt in (f32, bf16) for w in (WIN, CTX_L, CTX_L, DEN_L)],
        compiler_params=_params(2),
        name="den_attn",
    )(sink, qa, ka, va, cache_k, cache_v, qm, kvl, kr, cache_ckv, cache_kr, w_ukv_p, x_all, mods0, w_out, ln_g, ln_b)


MOE_PASS = 1536
N_PASS = T_ALL // MOE_PASS
SUB = 8
POS_BITS = 16


def _route(x1, mrow, tile_in_pass, rw_ref, rb_ref, hx_ref, sel_ref, selg_ref, cnt_ref, carry_ref):
    tile = x1.shape[0]
    sh, sc = mrow[:, 3 * D:4 * D], mrow[:, 4 * D:5 * D]
    h = x1 * (1.0 + sc) + sh
    for s in range(SUB):
        hx_ref[:, s, :] = h[:, LANE * s:LANE * (s + 1)]
    h_hi = h.astype(bf16)
    h_lo = (h - h_hi.astype(f32)).astype(bf16)
    logits = _dot(h_hi, rw_ref[0]) + (_dot(h_hi, rw_ref[1]) + _dot(h_lo, rw_ref[0]))
    scores = jax.nn.sigmoid(logits)
    lane = lax.broadcasted_iota(jnp.int32, scores.shape, 1).astype(f32)
    cand = jnp.where(lane < N_EXPERTS, scores + rb_ref[...], -jnp.inf)
    picked = jnp.zeros_like(scores)
    idxs, ws = [], []
    for _ in range(TOP_K):
        m = cand.max(axis=-1, keepdims=True)
        idx = jnp.where(cand == m, lane, float(LANE)).min(axis=-1, keepdims=True)
        hit = lane == idx
        idxs.append(idx)
        ws.append(jnp.where(hit, scores, 0.0).sum(axis=-1, keepdims=True))
        picked = jnp.where(hit, 1.0, picked)
        cand = jnp.where(hit, -jnp.inf, cand)
    wsum = ws[0]
    for w in ws[1:]:
        wsum = wsum + w

    @pl.when(tile_in_pass == 0)
    def _():
        carry_ref[...] = jnp.zeros_like(carry_ref)

    before = (lax.broadcasted_iota(jnp.int32, (tile, tile), 1) < lax.broadcasted_iota(jnp.int32, (tile, tile), 0))
    pos = carry_ref[...] + _dot(jnp.where(before, 1.0, 0.0).astype(bf16), picked.astype(bf16))
    sel = jnp.zeros_like(scores)
    selg = jnp.zeros_like(scores)
    for k in range(TOP_K):
        pos_k = jnp.where(lane == idxs[k], pos, 0.0).sum(axis=-1, keepdims=True)
        sel = jnp.where(lane == k, idxs[k] * float(1 << POS_BITS) + pos_k, sel)
        selg = jnp.where(lane == k, ws[k] / wsum * ROUTED_SCALE, selg)
    sel_ref[...] = sel.astype(jnp.int32)
    selg_ref[...] = selg
    carry_ref[...] = carry_ref[...] + picked.sum(axis=0, keepdims=True)
    cnt_ref[0] = carry_ref[...].astype(jnp.int32)


def _route_out(tile_rows):
    per_pass = MOE_PASS // tile_rows
    specs = [pl.BlockSpec((tile_rows, SUB, LANE), lambda i: (i, 0, 0)),
             pl.BlockSpec((tile_rows, LANE), lambda i: (i, 0)),
             pl.BlockSpec((tile_rows, LANE), lambda i: (i, 0)),
             pl.BlockSpec((1, 1, LANE), lambda i: (i // per_pass, 0, 0))]
    shapes = [jax.ShapeDtypeStruct((T_ALL, SUB, LANE), f32), jax.ShapeDtypeStruct((T_ALL, LANE), jnp.int32),
              jax.ShapeDtypeStruct((T_ALL, LANE), f32), jax.ShapeDtypeStruct((N_PASS, 1, LANE), jnp.int32)]
    return specs, shapes, [pltpu.VMEM((1, LANE), f32)]


def _router_kernel(xc_ref, xd_ref, mods_ref, rw_ref, rb_ref, x_ref, hx_ref, sel_ref, selg_ref, cnt_ref, carry_ref):
    i = pl.program_id(0)
    r = _mod_row(i, TOK_TILE)
    x1 = _pick(i, TOK_TILE, xc_ref, xd_ref)
    x_ref[...] = x1
    _route(x1, mods_ref[pl.ds(r, 1), :], i % (MOE_PASS // TOK_TILE), rw_ref, rb_ref, hx_ref, sel_ref, selg_ref,
           cnt_ref, carry_ref)


def _router(x1c, x1d, mods_l, router_w_p, router_b_p):
    xc_spec, xd_spec = _two_stream_specs(TOK_TILE, D)
    r_specs, r_shapes, r_scratch = _route_out(TOK_TILE)
    return pl.pallas_call(
        _router_kernel,
        grid=(T_ALL // TOK_TILE,),
        in_specs=[xc_spec, xd_spec, _full((8, 6 * D), 1), _full((2, D, LANE), 1), _full((1, LANE), 1)],
        out_specs=[pl.BlockSpec((TOK_TILE, D), lambda i: (i, 0))] + r_specs,
        out_shape=[jax.ShapeDtypeStruct((T_ALL, D), f32)] + r_shapes,
        scratch_shapes=r_scratch,
        compiler_params=_params(1),
        name="router",
    )(x1c, x1d, mods_l, router_w_p, router_b_p)


MOE_TOK = 1536
MOE_EG = 4
MOE_TILE = 512
MOE_FF = MOE_EG * EXPERT_FF


def _moe_kernel(h_ref, gates_ref, x_ref, mods_ref, wg_ref, wu_ref, wd_ref, sg_ref, su_ref, sd_ref, g_ref, b_ref,
                o_ref):
    p = pl.program_id(0)
    e = pl.program_id(1)
    n_tiles = MOE_TOK // MOE_TILE

    def gate_f(t):
        r = _mod_row(p * n_tiles + t, MOE_TILE)
        return mods_ref[pl.ds(r, 1), 5 * D:6 * D]

    def rows_of(t):
        return pl.ds(pl.multiple_of(t * MOE_TILE, MOE_TILE), MOE_TILE)

    @pl.when(e == 0)
    def _():
        sg = sg_ref[...].astype(bf16)
        su = su_ref[...].astype(bf16)
        sd = sd_ref[...].astype(bf16)

        def body(t, c):
            rows = rows_of(t)
            ht = h_ref[rows, :]
            hid = _silu(_dot(ht, sg)) * _dot(ht, su)
            o_ref[rows, :] = ALPHA * x_ref[rows, :] + gate_f(t) * _dot(hid.astype(bf16), sd)
            return c

        lax.fori_loop(0, n_tiles, body, 0)

    wg = jnp.concatenate([wg_ref[k].astype(bf16) for k in range(MOE_EG)], axis=1)
    wu = jnp.concatenate([wu_ref[k].astype(bf16) for k in range(MOE_EG)], axis=1)
    wd = jnp.concatenate([wd_ref[k].astype(bf16) for k in range(MOE_EG)], axis=0)
    lane = lax.broadcasted_iota(jnp.int32, (MOE_TILE, LANE), 1)

    def body(t, c):
        rows = rows_of(t)
        ht = h_ref[rows, :]
        hid = _silu(_dot(ht, wg)) * _dot(ht, wu)
        gt = gates_ref[rows, :]
        parts = []
        for k in range(MOE_EG):
            col = jnp.where(lane == e * MOE_EG + k, gt, 0.0).sum(axis=-1, keepdims=True)
            parts.append((hid[:, EXPERT_FF * k:EXPERT_FF * (k + 1)] * col).astype(bf16))
        o_ref[rows, :] += gate_f(t) * _dot(jnp.concatenate(parts, axis=1), wd)
        return c

    lax.fori_loop(0, n_tiles, body, 0)

    @pl.when(e == pl.num_programs(1) - 1)
    def _():
        def body(t, c):
            rows = rows_of(t)
            o_ref[rows, :] = _layer_norm(o_ref[rows, :], g_ref[...], b_ref[...])
            return c

        lax.fori_loop(0, n_tiles, body, 0)


def _moe(l, h, gates, x1, mods_l, wg, wu, wd, sg, su, sd, ln_g, ln_b):
    tok = lambda w: pl.BlockSpec((MOE_TOK, w), lambda p, e: (p, 0))
    return pl.pallas_call(
        _moe_kernel,
        grid=(T_ALL // MOE_TOK, N_EXPERTS // MOE_EG),
        in_specs=[tok(D), tok(LANE), tok(D), _full((8, 6 * D), 2),
                  pl.BlockSpec((None, MOE_EG, D, EXPERT_FF), lambda p, e: (l, e, 0, 0)),
                  pl.BlockSpec((None, MOE_EG, D, EXPERT_FF), lambda p, e: (l, e, 0, 0)),
                  pl.BlockSpec((None, MOE_EG, EXPERT_FF, D), lambda p, e: (l, e, 0, 0)),
                  pl.BlockSpec((None, D, SHARED_FF), lambda p, e: (l, 0, 0)),
                  pl.BlockSpec((None, D, SHARED_FF), lambda p, e: (l, 0, 0)),
                  pl.BlockSpec((None, SHARED_FF, D), lambda p, e: (l, 0, 0)),
                  _full((1, D), 2), _full((1, D), 2)],
        out_specs=tok(D),
        out_shape=jax.ShapeDtypeStruct((T_ALL, D), f32),
        compiler_params=_params(2),
        name="moe",
    )(h, gates, x1, mods_l, wg, wu, wd, sg, su, sd, ln_g, ln_b)


SP_TM = 256
SP_PICKS = MOE_PASS * TOP_K
SP_CAP = SP_PICKS + SUB
SP_SHARED_TILE = 512


def _moe_sparse_kernel(sel_ref, selg_ref, cnt_ref, hx_ref, wg_ref, wu_ref, wd_ref, sg_ref, su_ref, sd_ref,
                       acc_ref, tok_s, gl_s, off_s, buf_ref, gbuf_ref, y3_ref, wgu_s, wd_s):
    e = pl.program_id(1)

    @pl.when(e == 0)
    def _():
        def off_body(x, run):
            off_s[x] = run
            return run + cnt_ref[x]

        lax.fori_loop(0, N_EXPERTS, off_body, 0)
        for q in range(SUB):
            tok_s[SP_PICKS + q] = 0
            gl_s[SP_PICKS + q] = 0.0

        def place(t, c):
            for k in range(TOP_K):
                code = sel_ref[t * SUB + k]
                slot = off_s[lax.shift_right_logical(code, POS_BITS)] + (code & ((1 << POS_BITS) - 1))
                tok_s[slot] = t
                gl_s[slot] = selg_ref[t * SUB + k]
            return c

        lax.fori_loop(0, MOE_PASS, place, 0)
        buf_ref[...] = jnp.zeros_like(buf_ref)
        gbuf_ref[...] = jnp.zeros_like(gbuf_ref)
        sg = sg_ref[...].astype(bf16)
        su = su_ref[...].astype(bf16)
        sd = sd_ref[...].astype(bf16)

        def shared(t, c):
            rows = pl.ds(pl.multiple_of(t * SP_SHARED_TILE, SP_SHARED_TILE), SP_SHARED_TILE)
            x = jnp.concatenate([hx_ref[rows, s, :] for s in range(SUB)], axis=1).astype(bf16)
            y = _dot((_silu(_dot(x, sg)) * _dot(x, su)).astype(bf16), sd)
            for s in range(SUB):
                acc_ref[rows, s, :] = y[:, LANE * s:LANE * (s + 1)]
            return c

        lax.fori_loop(0, MOE_PASS // SP_SHARED_TILE, shared, 0)

    for k in range(MOE_EG):
        x_id = e * MOE_EG + k
        n = cnt_ref[x_id]
        o = off_s[x_id]
        wgu_s[:, 0:EXPERT_FF] = wg_ref[k].astype(bf16)
        wgu_s[:, EXPERT_FF:2 * EXPERT_FF] = wu_ref[k].astype(bf16)
        wd_s[...] = wd_ref[k].astype(bf16)

        def tile_body(ti, c, n=n, o=o):
            t0 = ti * SP_TM
            n_grp = (jnp.minimum(SP_TM, n - t0) + SUB - 1) // SUB

            def gather(gi, c2):
                base = pl.multiple_of(gi * SUB, SUB)
                for r in range(SUB):
                    j = t0 + base + r
                    buf_ref[base + r] = hx_ref[tok_s[o + j]]
                    g = jnp.where(j < n, gl_s[o + j], 0.0)
                    gbuf_ref[pl.ds(base + r, 1), :] = jnp.full((1, LANE), g, f32)
                return c2

            lax.fori_loop(0, n_grp, gather, 0)
            ab = None
            for s in range(SUB // 2):
                x = jnp.concatenate([buf_ref[:, 2 * s, :], buf_ref[:, 2 * s + 1, :]], axis=1).astype(bf16)
                part = _dot(x, wgu_s[2 * LANE * s:2 * LANE * (s + 1), :])
                ab = part if ab is None else ab + part
            hid = (_silu(ab[:, 0:EXPERT_FF]) * ab[:, EXPERT_FF:2 * EXPERT_FF]) * gbuf_ref[...]
            y = _dot(hid.astype(bf16), wd_s[...])
            for s in range(SUB):
                y3_ref[:, s, :] = y[:, LANE * s:LANE * (s + 1)]

            def scatter(gi, c2):
                base = pl.multiple_of(gi * SUB, SUB)
                for r in range(SUB):
                    t = tok_s[o + t0 + base + r]
                    acc_ref[t] = acc_ref[t] + y3_ref[base + r]
                return c2

            lax.fori_loop(0, n_grp, scatter, 0)
            return c

        lax.fori_loop(0, (n + SP_TM - 1) // SP_TM, tile_body, 0)


def _moe_sparse(l, sel_flat, selg_flat, cnt_flat, hx, wg, wu, wd, sg, su, sd):
    smem = lambda n: pl.BlockSpec((n,), lambda p, e: (p,), memory_space=pltpu.SMEM)
    tok3 = pl.BlockSpec((MOE_PASS, SUB, LANE), lambda p, e: (p, 0, 0))
    return pl.pallas_call(
        _moe_sparse_kernel,
        grid=(N_PASS, N_EXPERTS // MOE_EG),
        in_specs=[smem(MOE_PASS * SUB), smem(MOE_PASS * SUB), smem(LANE), tok3,
                  pl.BlockSpec((None, MOE_EG, D, EXPERT_FF), lambda p, e: (l, e, 0, 0)),
                  pl.BlockSpec((None, MOE_EG, D, EXPERT_FF), lambda p, e: (l, e, 0, 0)),
                  pl.BlockSpec((None, MOE_EG, EXPERT_FF, D), lambda p, e: (l, e, 0, 0)),
                  pl.BlockSpec((None, D, SHARED_FF), lambda p, e: (l, 0, 0)),
                  pl.BlockSpec((None, D, SHARED_FF), lambda p, e: (l, 0, 0)),
                  pl.BlockSpec((None, SHARED_FF, D), lambda p, e: (l, 0, 0))],
        out_specs=tok3,
        out_shape=jax.ShapeDtypeStruct((T_ALL, SUB, LANE), f32),
        scratch_shapes=[pltpu.SMEM((SP_CAP,), jnp.int32), pltpu.SMEM((SP_CAP,), f32), pltpu.SMEM((LANE,), jnp.int32),
                        pltpu.VMEM((SP_TM, SUB, LANE), f32), pltpu.VMEM((SP_TM, LANE), f32),
                        pltpu.VMEM((SP_TM, SUB, LANE), f32), pltpu.VMEM((D, 2 * EXPERT_FF), bf16),
                        pltpu.VMEM((EXPERT_FF, D), bf16)],
        compiler_params=_params(2),
        name="moe_sparse",
    )(sel_flat, selg_flat, cnt_flat, hx, wg, wu, wd, sg, su, sd)


def _moe_finish_kernel(x_ref, acc_ref, mods_ref, g_ref, b_ref, o_ref):
    r = _mod_row(pl.program_id(0), TOK_TILE)
    moe = jnp.concatenate([acc_ref[:, s, :] for s in range(SUB)], axis=1)
    o_ref[...] = _layer_norm(ALPHA * x_ref[...] + mods_ref[pl.ds(r, 1), 5 * D:6 * D] * moe, g_ref[...], b_ref[...])


def _moe_finish(x1, acc3, mods_l, ln_g, ln_b):
    row_spec = pl.BlockSpec((TOK_TILE, D), lambda i: (i, 0))
    return pl.pallas_call(
        _moe_finish_kernel,
        grid=(T_ALL // TOK_TILE,),
        in_specs=[row_spec, pl.BlockSpec((TOK_TILE, SUB, LANE), lambda i: (i, 0, 0)), _full((8, 6 * D), 1),
                  _full((1, D), 1), _full((1, D), 1)],
        out_specs=row_spec,
        out_shape=jax.ShapeDtypeStruct((T_ALL, D), f32),
        compiler_params=_params(1),
        name="moe_finish",
    )(x1, acc3, mods_l, ln_g, ln_b)


def _channel_mixer(l, x1, hx, sel, selg, cnt, mods_l, exp_w_gate, exp_w_up, exp_w_down, sh_w_gate, sh_w_up,
                   sh_w_down, ln_g, ln_b):
    acc3 = _moe_sparse(l, sel[:, :SUB].reshape(-1), selg[:, :SUB].reshape(-1), cnt.reshape(-1), hx,
                       exp_w_gate, exp_w_up, exp_w_down, sh_w_gate, sh_w_up, sh_w_down)
    return _moe_finish(x1, acc3, mods_l, ln_g, ln_b)


def _router_weights(l, router_w, router_bias):
    rw = jnp.pad(router_w[l], ((0, 0), (0, LANE - N_EXPERTS)))
    rw_hi = rw.astype(bf16)
    rw_lo = (rw - rw_hi.astype(f32)).astype(bf16)
    rb = jnp.pad(router_bias[l], (0, LANE - N_EXPERTS)).reshape(1, LANE)
    return jnp.stack([rw_hi, rw_lo]), rb


S5_Q = 8
S5_NGB = D // LANE
S5_TILE = 256
S5_KT = S5_TILE // S5_Q


def _s5_in_kernel(x_ref, mods_ref, w_ref, u_ref, u2_ref, slab_ref, *, row_of):
    r = row_of(pl.program_id(0))
    mrow = mods_ref[pl.ds(r, 1), :]
    sh, sc = mrow[:, 0:D], mrow[:, D:2 * D]
    h = (x_ref[...] * (1.0 + sc) + sh).astype(bf16)
    u = _dot(h, w_ref[...])
    u_ref[...] = u
    for s in range(S5_NGB):
        slab_ref[s] = u[:, LANE * s:LANE * (s + 1)]
    for s in range(S5_NGB):
        for j in range(S5_Q):
            u2_ref[s, :, LANE * j:LANE * (j + 1)] = slab_ref[s, pl.ds(j, S5_KT, stride=S5_Q), :].astype(bf16)


def _s5_in(x_all, tile0, mods1, w_in_c, n_b, seq_len, row_of):
    tiles_per_seq = seq_len // S5_TILE
    return pl.pallas_call(
        functools.partial(_s5_in_kernel, row_of=row_of),
        grid=(n_b * tiles_per_seq,),
        in_specs=[pl.BlockSpec((S5_TILE, D), lambda i: (tile0 + i, 0)), _full((8, 6 * D), 1), _full((D, D), 1)],
        out_specs=[pl.BlockSpec((S5_TILE, D), lambda i: (i, 0)),
                   pl.BlockSpec((S5_NGB, S5_KT, D), lambda i: (0, i % tiles_per_seq, i // tiles_per_seq))],
        out_shape=[jax.ShapeDtypeStruct((n_b * seq_len, D), f32),
                   jax.ShapeDtypeStruct((S5_NGB, seq_len // S5_Q, n_b * D), bf16)],
        scratch_shapes=[pltpu.VMEM((S5_NGB, S5_TILE, LANE), f32)],
        compiler_params=_params(1),
        name="s5_in",
    )(x_all, mods1, w_in_c)


S5_GL = (LANE // S5_CH) * S5_P
S5_ROWS_C = (CTX_L // S5_Q) * N_CTX_B
S5_ROWS_D = (DEN_L // S5_Q) * N_DEN_B


def _split_bf16(a):
    hi = a.astype(bf16)
    return hi, (a - hi.astype(f32)).astype(bf16)


def _s5_scan_kernel(lre_ref, lim_ref, ldt_ref, btr_ref, bti_ref, ctr_ref, cti_ref, uc_ref, ud_ref, h0_ref,
                    yc_ref, yd_ref, st_ref, win_ref, mso_ref, wit_ref, a_ref, s_ref, hp_ref):
    gl = S5_GL
    rowg = lax.shift_right_logical(lax.broadcasted_iota(jnp.int32, (LANE, gl), 0), 4)
    colg = lax.shift_right_logical(lax.broadcasted_iota(jnp.int32, (LANE, gl), 1), 6)
    same_group = rowg == colg
    reps = LANE // S5_CH

    def expand(t):
        return jnp.where(same_group, jnp.concatenate([t] * reps, axis=0), 0.0)

    for d in range(2):
        fwd = d == 0
        lre, lim = lre_ref[d], lim_ref[d]
        dt = jnp.exp(ldt_ref[d])
        a, w = lre * dt, lim * dt
        pre = [jnp.exp(m * a) * jnp.cos(m * w) for m in range(S5_Q + 1)]
        pim = [jnp.exp(m * a) * jnp.sin(m * w) for m in range(S5_Q + 1)]
        xr, xi = pre[1] - 1.0, pim[1]
        den = lre * lre + lim * lim
        cfr, cfi = (xr * lre + xi * lim) / den, (xi * lre - xr * lim) / den
        btr, bti = btr_ref[d], bti_ref[d]
        bexp_r = expand(cfr * btr - cfi * bti)
        bexp_i = expand(cfr * bti + cfi * btr)
        cexp_r, cexp_i = expand(ctr_ref[d]), expand(cti_ref[d])
        for m in range(S5_Q + 1):
            a_ref[m, :, 0:gl] = cexp_r * pre[m] - cexp_i * pim[m]
            a_ref[m, :, gl:2 * gl] = -(cexp_r * pim[m] + cexp_i * pre[m])
        for j in range(S5_Q):
            m = S5_Q - 1 - j if fwd else j
            win_ref[LANE * j:LANE * (j + 1), 0:gl] = (pre[m] * bexp_r - pim[m] * bexp_i).astype(bf16)
            win_ref[LANE * j:LANE * (j + 1), gl:2 * gl] = (pre[m] * bexp_i + pim[m] * bexp_r).astype(bf16)
        for j in range(S5_Q):
            m = j + 1 if fwd else S5_Q - j
            mso_ref[LANE * j:LANE * (j + 1), :] = a_ref[m].astype(bf16)
        b2_hi, b2_lo = _split_bf16(jnp.concatenate([bexp_r, bexp_i], axis=1))
        kt = []
        for tau in range(S5_Q):
            a_hi, a_lo = _split_bf16(a_ref[tau])
            kt.append((_dot_nt(b2_hi, a_hi) + _dot_nt(b2_hi, a_lo) + _dot_nt(b2_lo, a_hi)).astype(bf16))
        zero_blk = jnp.zeros((LANE, LANE), bf16)
        for j in range(S5_Q):
            for jp in range(S5_Q):
                tau = jp - j if fwd else j - jp
                wit_ref[LANE * j:LANE * (j + 1), LANE * jp:LANE * (jp + 1)] = kt[tau] if tau >= 0 else zero_blk

        l8r, l8i = pre[S5_Q], pim[S5_Q]

        def run(u_ref, y_ref, n_b, n_k, h_init):
            rows = n_b * n_k
            s_ref[0:rows, :] = _dot(u_ref[0], win_ref[...])

            per_it = max(1, 8 // n_b)
            n_it = n_k // per_it
            it_rows = per_it * n_b

            def step(i, carry):
                hr, hi_ = carry
                it = i if fwd else n_it - 1 - i
                rs = pl.ds(pl.multiple_of(it * it_rows, it_rows), it_rows)
                s_tile = s_ref[rs, :]
                prev_r, prev_i = [None] * per_it, [None] * per_it
                for sub in (range(per_it) if fwd else reversed(range(per_it))):
                    prev_r[sub], prev_i[sub] = hr, hi_
                    sr = s_tile[sub * n_b:(sub + 1) * n_b, 0:gl]
                    si = s_tile[sub * n_b:(sub + 1) * n_b, gl:2 * gl]
                    hr, hi_ = l8r * hr - l8i * hi_ + sr, l8r * hi_ + l8i * hr + si
                hp_ref[rs, 0:gl] = jnp.concatenate(prev_r, axis=0) if per_it > 1 else prev_r[0]
                hp_ref[rs, gl:2 * gl] = jnp.concatenate(prev_i, axis=0) if per_it > 1 else prev_i[0]
                return hr, hi_

            h_fin = lax.fori_loop(0, n_it, step, h_init)
            y = _dot_nt(hp_ref[0:rows, :].astype(bf16), mso_ref[...]) + _dot(u_ref[0], wit_ref[...])
            if fwd:
                y_ref[0] = y
            else:
                y_ref[0] += y
            return h_fin

        zeros = jnp.zeros((N_CTX_B, gl), f32)
        hr, hi_ = run(uc_ref, yc_ref, N_CTX_B, CTX_L // S5_Q, (zeros, zeros))
        st_ref[d, 0] = hr
        st_ref[d, 1] = hi_
        run(ud_ref, yd_ref, N_DEN_B, DEN_L // S5_Q, (h0_ref[d, 0], h0_ref[d, 1]))


def _s5_scan(lam_re, lam_im, log_dt, bt_re, bt_im, ct_re, ct_im, u2c, u2d, h0):
    gl = S5_GL
    vec = pl.BlockSpec((2, 1, gl), lambda g: (0, 0, g))
    tab = pl.BlockSpec((2, S5_CH, gl), lambda g: (0, 0, g))
    rows = lambda n: pl.BlockSpec((1, n, D), lambda g: (g, 0, 0))
    return pl.pallas_call(
        _s5_scan_kernel,
        grid=(S5_NGB,),
        in_specs=[vec, vec, vec, tab, tab, tab, tab, rows(S5_ROWS_C), rows(S5_ROWS_D),
                  pl.BlockSpec((2, 2, N_DEN_B, gl), lambda g: (0, 0, 0, g))],
        out_specs=[rows(S5_ROWS_C), rows(S5_ROWS_D), pl.BlockSpec((2, 2, N_CTX_B, gl), lambda g: (0, 0, 0, g))],
        out_shape=[jax.ShapeDtypeStruct((S5_NGB, S5_ROWS_C, D), f32), jax.ShapeDtypeStruct((S5_NGB, S5_ROWS_D, D), f32),
                   jax.ShapeDtypeStruct((2, 2, N_CTX_B, S5_G * S5_P), f32)],
        scratch_shapes=[pltpu.VMEM((D, 2 * gl), bf16), pltpu.VMEM((D, 2 * gl), bf16), pltpu.VMEM((D, D), bf16),
                        pltpu.VMEM((S5_Q + 1, LANE, 2 * gl), f32), pltpu.VMEM((S5_ROWS_C, 2 * gl), f32),
                        pltpu.VMEM((S5_ROWS_C, 2 * gl), f32)],
        compiler_params=_params(1),
        name="s5_scan",
    )(lam_re, lam_im, log_dt, bt_re, bt_im, ct_re, ct_im, u2c, u2d, h0)


S5_CTX_TILES = T_CTX // S5_TILE
S5_DEN_TPS = DEN_L // S5_TILE


def _gelu_tanh(x):
    return 0.5 * x * (1.0 + jnp.tanh(np.sqrt(2.0 / np.pi).astype(np.float32) * (x + 0.044715 * (x * x * x))))


def _s5_out_kernel(x_ref, uc_ref, ud_ref, yc_ref, yd_ref, mods_ref, dsk_ref, wout_ref, g_ref, b_ref, rw_ref, rb_ref,
                   x1_ref, hx_ref, sel_ref, selg_ref, cnt_ref, slab_ref, carry_ref):
    i = pl.program_id(0)
    is_ctx = i < S5_CTX_TILES
    r = _mod_row(i, S5_TILE)
    mrow = mods_ref[pl.ds(r, 1), :]
    u = _pick(i, S5_TILE, uc_ref, ud_ref)

    def unchunk(y_ref):
        for s in range(S5_NGB):
            for j in range(S5_Q):
                slab_ref[s, pl.ds(j, S5_KT, stride=S5_Q), :] = y_ref[s, :, LANE * j:LANE * (j + 1)]

    lax.cond(is_ctx, lambda: unchunk(yc_ref), lambda: unchunk(yd_ref))
    y = jnp.concatenate([slab_ref[s] for s in range(S5_NGB)], axis=1) + dsk_ref[...] * u
    z = _dot(_gelu_tanh(y).astype(bf16), wout_ref[...])
    out = z[:, 0:D] * jax.nn.sigmoid(z[:, D:2 * D])
    x1 = _layer_norm(ALPHA * x_ref[...] + mrow[:, 2 * D:3 * D] * out, g_ref[...], b_ref[...])
    x1_ref[...] = x1
    _route(x1, mrow, i % (MOE_PASS // S5_TILE), rw_ref, rb_ref, hx_ref, sel_ref, selg_ref, cnt_ref, carry_ref)


def _s5_out(x_all, uc, ud, yc, yd, mods1, d_skip, w_out_c, ln_g, ln_b, rw, rb):
    row_spec = lambda w: pl.BlockSpec((S5_TILE, w), lambda i: (i, 0))
    r_specs, r_shapes, r_scratch = _route_out(S5_TILE)
    ci = lambda i: jnp.minimum(i, S5_CTX_TILES - 1)
    di = lambda i: jnp.maximum(i - S5_CTX_TILES, 0)
    return pl.pallas_call(
        _s5_out_kernel,
        grid=(T_ALL // S5_TILE,),
        in_specs=[row_spec(D),
                  pl.BlockSpec((S5_TILE, D), lambda i: (ci(i), 0)),
                  pl.BlockSpec((S5_TILE, D), lambda i: (di(i), 0)),
                  pl.BlockSpec((S5_NGB, S5_KT, D), lambda i: (0, 0, ci(i))),
                  pl.BlockSpec((S5_NGB, S5_KT, D), lambda i: (0, di(i) % S5_DEN_TPS, di(i) // S5_DEN_TPS)),
                  _full((8, 6 * D), 1), _full((1, D), 1), _full((D, 2 * D), 1), _full((1, D), 1), _full((1, D), 1),
                  _full((2, D, LANE), 1), _full((1, LANE), 1)],
        out_specs=[row_spec(D)] + r_specs,
        out_shape=[jax.ShapeDtypeStruct((T_ALL, D), f32)] + r_shapes,
        scratch_shapes=[pltpu.VMEM((S5_NGB, S5_TILE, LANE), f32)] + r_scratch,
        compiler_params=_params(1),
        name="s5_out",
    )(x_all, uc, ud, yc, yd, mods1, d_skip, w_out_c, ln_g, ln_b, rw, rb)


def kernel(x_prompt, x_sample, c, cache_attn_k, cache_attn_v, cache_mla_ckv, cache_mla_krope, state_ssm, c_ctx,
           ada_w, ada_b, ln_mix_g, ln_mix_b, ln_ffn_g, ln_ffn_b, w_in_ab, attn_sink, mla_q_norm, mla_kv_norm,
           mla_w_uq, mla_w_ukv, w_out_ab, w_in_c, s5_lam_re, s5_lam_im, s5_log_dt, s5_b_re, s5_b_im, s5_c_re,
           s5_c_im, s5_d, w_out_c, router_w, router_bias, exp_w_gate, exp_w_up, exp_w_down, sh_w_gate, sh_w_up,
           sh_w_down):
    row = lambda v: v.reshape(1, -1)
    xc, xd = x_prompt.reshape(T_CTX, D), x_sample.reshape(T_DEN, D)
    cvec8 = jnp.concatenate([c_ctx[None, :], c, jnp.zeros((8 - 1 - N_DEN_B, D), f32)], axis=0)
    mods = _adaln(cvec8, ada_w, ada_b)

    w_in_p = jnp.pad(w_in_ab[0], ((0, 0), (0, PROJ_W - w_in_ab.shape[-1]))).astype(bf16)
    uq = mla_w_uq[0].reshape(MLA_Q_RANK, MLA_HEADS, MLA_NOPE + MLA_ROPE)
    w_uq_p = jnp.concatenate([uq[:, :, :MLA_NOPE].reshape(MLA_Q_RANK, -1), uq[:, :, MLA_NOPE:].reshape(MLA_Q_RANK, -1)],
                             axis=1).astype(bf16)
    ukv = mla_w_ukv[0].reshape(MLA_KV_RANK, MLA_HEADS, MLA_NOPE + MLA_V)
    w_ukv_p = jnp.concatenate([ukv[:, :, :MLA_NOPE].reshape(MLA_KV_RANK, -1),
                               ukv[:, :, MLA_NOPE:].reshape(MLA_KV_RANK, -1)], axis=1).astype(bf16)
    qa, ka, va, ckv, kr, qm, kvl = _ab_proj(xc, xd, mods[0], w_in_p, row(mla_q_norm[0]), row(mla_kv_norm[0]),
                                            w_uq_p, w_ukv_p)
    w_out_b = w_out_ab[0].astype(bf16)
    g0, b0 = row(ln_mix_g[0]), row(ln_mix_b[0])
    x1c = _ctx_attn(attn_sink[0], qa, ka, va, qm, kvl, kr, xc, mods[0], w_out_b, g0, b0)
    x1d = _den_attn(attn_sink[0], qa, ka, va,
                    cache_attn_k[:, 0].reshape(N_DEN_B, CTX_L, A_KV_HEADS * A_HD),
                    cache_attn_v[:, 0].reshape(N_DEN_B, CTX_L, A_KV_HEADS * A_HD),
                    qm, kvl, kr, cache_mla_ckv[:, 0], cache_mla_krope[:, 0], w_ukv_p, xd, mods[0], w_out_b, g0, b0)
    rw0, rb0 = _router_weights(0, router_w, router_bias)
    x1, hx, sel, selg, cnt = _router(x1c, x1d, mods[0], rw0, rb0)
    x2 = _channel_mixer(0, x1, hx, sel, selg, cnt, mods[0], exp_w_gate, exp_w_up, exp_w_down, sh_w_gate, sh_w_up,
                        sh_w_down, row(ln_ffn_g[0]), row(ln_ffn_b[0]))

    w_in_c_b = w_in_c[0].astype(bf16)
    uc, u2c = _s5_in(x2, 0, mods[1], w_in_c_b, N_CTX_B, CTX_L, lambda i: 0)
    ud, u2d = _s5_in(x2, S5_CTX_TILES, mods[1], w_in_c_b, N_DEN_B, DEN_L, lambda i: 1 + i // S5_DEN_TPS)
    gp = S5_G * S5_P
    chan_major_b = lambda t: jnp.transpose(t[0], (0, 3, 1, 2)).reshape(2, S5_CH, gp)
    chan_major_c = lambda t: jnp.transpose(t[0], (0, 2, 1, 3)).reshape(2, S5_CH, gp)
    h0 = jnp.transpose(state_ssm[:, 0], (1, 2, 0, 3, 4)).reshape(2, 2, N_DEN_B, gp)
    yc, yd, st = _s5_scan(s5_lam_re[0].reshape(2, 1, gp), s5_lam_im[0].reshape(2, 1, gp),
                          jnp.repeat(s5_log_dt[0], S5_P, axis=-1).reshape(2, 1, gp),
                          chan_major_b(s5_b_re), chan_major_b(s5_b_im), chan_major_c(s5_c_re), chan_major_c(s5_c_im),
                          u2c.reshape(S5_NGB, S5_ROWS_C, D), u2d.reshape(S5_NGB, S5_ROWS_D, D), h0)
    rw1, rb1 = _router_weights(1, router_w, router_bias)
    x3, hx, sel, selg, cnt = _s5_out(x2, uc, ud, yc.reshape(S5_NGB, CTX_L // S5_Q, N_CTX_B * D),
                                     yd.reshape(S5_NGB, DEN_L // S5_Q, N_DEN_B * D), mods[1], row(s5_d[0]),
                                     w_out_c[0].astype(bf16), row(ln_mix_g[1]), row(ln_mix_b[1]), rw1, rb1)
    x4 = _channel_mixer(1, x3, hx, sel, selg, cnt, mods[1], exp_w_gate, exp_w_up, exp_w_down, sh_w_gate, sh_w_up,
                        sh_w_down, row(ln_ffn_g[1]), row(ln_ffn_b[1]))

    y_prompt = x4[:T_CTX].reshape(N_CTX_B, CTX_L, D)
    y_sample = x4[T_CTX:].reshape(N_DEN_B, DEN_L, D)
    new_attn_k = ka[:T_CTX].reshape(N_CTX_B, 1, CTX_L, A_KV_HEADS, A_HD)
    new_attn_v = va[:T_CTX].reshape(N_CTX_B, 1, CTX_L, A_KV_HEADS, A_HD)
    new_mla_ckv = ckv[:T_CTX].reshape(N_CTX_B, 1, CTX_L, MLA_KV_RANK)
    new_mla_krope = kr[:T_CTX, :MLA_ROPE].reshape(N_CTX_B, 1, CTX_L, MLA_ROPE)
    new_state_ssm = jnp.transpose(st, (2, 0, 1, 3)).reshape(N_CTX_B, 1, 2, 2, S5_G, S5_P)
    return (y_prompt, y_sample, new_attn_k, new_attn_v, new_mla_ckv, new_mla_krope, new_state_ssm)
```

```python
import functools

import jax
import jax.numpy as jnp
import numpy as np
from jax import lax
from jax.experimental import pallas as pl
from jax.experimental.pallas import tpu as pltpu

f32 = jnp.float32
bf16 = jnp.bfloat16

D = 1024
N_CTX_B, CTX_L = 16, 256
N_DEN_B, DEN_L = 2, 1024
T_CTX = N_CTX_B * CTX_L
T_DEN = N_DEN_B * DEN_L
T_ALL = T_CTX + T_DEN
GRID_W = 64
WINDOW = 128
ROPE_BASE = 10000.0
A_HEADS, A_KV_HEADS, A_HD = 8, 2, 64
A_GROUP = A_HEADS // A_KV_HEADS
A_SCALE = A_HD ** -0.5
MLA_HEADS, MLA_Q_RANK, MLA_KV_RANK = 8, 256, 128
MLA_NOPE, MLA_ROPE, MLA_V = 64, 32, 64
MLA_SCALE = (MLA_NOPE + MLA_ROPE) ** -0.5
N_EXPERTS, TOP_K, EXPERT_FF, SHARED_FF = 64, 6, 128, 128
ROUTED_SCALE = 2.5
DEPTH = 2
ALPHA = (2.0 * DEPTH) ** 0.25
LN_EPS = 1e-5
RMS_EPS = 1e-6
NEG_INF = -1e30
S5_G, S5_CH, S5_P = 64, 16, 64

LANE = 128
VMEM_LIMIT = 56 * 1024 * 1024

TOK_TILE = 512


def _mod_row(tile_idx, tile_rows):
    start = tile_idx * tile_rows
    return jnp.where(start < T_CTX, 0, 1 + (start - T_CTX) // DEN_L)


def _layer_norm(y, g, b):
    mu = jnp.mean(y, axis=-1, keepdims=True)
    yc = y - mu
    var = jnp.mean(yc * yc, axis=-1, keepdims=True)
    return yc * lax.rsqrt(var + LN_EPS) * g + b


def _silu(x):
    return x * jax.nn.sigmoid(x)


def _dot(a, b):
    return jnp.dot(a, b, preferred_element_type=f32)


def _dot_nt(a, b):
    return lax.dot_general(a, b, (((1,), (1,)), ((), ())), preferred_element_type=f32)


def _split_bf16(a):
    hi = a.astype(bf16)
    return hi, (a - hi.astype(f32)).astype(bf16)


def _full(shape, n_grid):
    zeros = tuple(0 for _ in shape)
    return pl.BlockSpec(shape, lambda *_: zeros)


def _two_stream_specs(tile_rows, width):
    n_ctx = T_CTX // tile_rows
    return (pl.BlockSpec((tile_rows, width), lambda i: (jnp.minimum(i, n_ctx - 1), 0)),
            pl.BlockSpec((tile_rows, width), lambda i: (jnp.maximum(i - n_ctx, 0), 0)))


def _pick(i, tile_rows, ctx_ref, den_ref):
    return lax.cond(i < T_CTX // tile_rows, lambda: ctx_ref[...], lambda: den_ref[...])


def _params(n_grid):
    return pltpu.CompilerParams(dimension_semantics=("arbitrary",) * n_grid, vmem_limit_bytes=VMEM_LIMIT)


ADA_TN = 1536


def _adaln_kernel(c_ref, w_ref, b_ref, o_ref):
    s_hi, s_lo = _split_bf16(_silu(c_ref[...]))
    w_hi, w_lo = _split_bf16(w_ref[0])
    o_ref[0] = _dot(s_hi, w_hi) + (_dot(s_hi, w_lo) + _dot(s_lo, w_hi)) + b_ref[0]


def _adaln(cvec8, ada_w, ada_b):
    n = 6 * D
    return pl.pallas_call(
        _adaln_kernel,
        grid=(DEPTH, n // ADA_TN),
        in_specs=[
            pl.BlockSpec((8, D), lambda l, j: (0, 0)),
            pl.BlockSpec((1, D, ADA_TN), lambda l, j: (l, 0, j)),
            pl.BlockSpec((1, 1, ADA_TN), lambda l, j: (l, 0, j)),
        ],
        out_specs=pl.BlockSpec((1, 8, ADA_TN), lambda l, j: (l, 0, j)),
        out_shape=jax.ShapeDtypeStruct((DEPTH, 8, n), f32),
        compiler_params=_params(2),
        name="adaln",
    )(cvec8, ada_w, ada_b.reshape(DEPTH, 1, n))


def _rope_table_array(head_dim):
    q = head_dim // 4
    pos = np.arange(DEN_L)
    row, col = (pos // GRID_W).astype(np.float64), (pos % GRID_W).astype(np.float64)
    lane = np.arange(LANE) % head_dim
    is_col = lane >= head_dim // 2
    w = lane % (head_dim // 2)
    first = w < q
    inv_freq = ROPE_BASE ** (-np.arange(q, dtype=np.float64) / q)
    ang = np.where(is_col[None, :], col[:, None], row[:, None]) * inv_freq[w % q][None, :]
    cos, sin = np.cos(ang), np.sin(ang)
    sin_a = np.where(first[None, :], -sin, 0.0)
    sin_b = np.where(first[None, :], 0.0, sin)
    ident = np.stack([np.ones((TOK_TILE, LANE)), np.zeros((TOK_TILE, LANE)), np.zeros((TOK_TILE, LANE))])
    tab = np.concatenate([ident, np.stack([cos, sin_a, sin_b])], axis=1).astype(np.float32)
    return jnp.asarray(tab), q


def _rope_chunk(x, tab_ref, q):
    return x * tab_ref[0] + pltpu.roll(x, LANE - q, 1) * tab_ref[1] + pltpu.roll(x, q, 1) * tab_ref[2]


PROJ_W = 1280
C_QA, C_KA, C_VA, C_CQ, C_CKV, C_KR = 0, 512, 640, 768, 1024, 1152
MLA_NN = MLA_HEADS * MLA_NOPE


def _ab_proj_kernel(xc_ref, xd_ref, mods_ref, w_ref, qn_ref, kvn_ref, wuq_ref, wukv_ref, ta_ref, tm_ref,
                    qa_ref, ka_ref, va_ref, ckv_ref, kr_ref, qm_ref, kvl_ref, *, qa_shift, qm_shift):
    i = pl.program_id(0)
    r = _mod_row(i, TOK_TILE)
    mrow = mods_ref[pl.ds(r, 1), :]
    sh, sc = mrow[:, 0:D], mrow[:, D:2 * D]
    x = _pick(i, TOK_TILE, xc_ref, xd_ref)
    h = (x * (1.0 + sc) + sh).astype(bf16)
    proj = _dot(h, w_ref[...])
    for j in range(4):
        c0 = C_QA + LANE * j
        qa_ref[:, LANE * j:LANE * (j + 1)] = _rope_chunk(proj[:, c0:c0 + LANE], ta_ref, qa_shift).astype(bf16)
    ka_ref[...] = _rope_chunk(proj[:, C_KA:C_KA + LANE], ta_ref, qa_shift)
    va_ref[...] = proj[:, C_VA:C_VA + LANE]
    cq = proj[:, C_CQ:C_CQ + MLA_Q_RANK]
    cq = cq * lax.rsqrt(jnp.mean(cq * cq, axis=-1, keepdims=True) + RMS_EPS) * qn_ref[...]
    ckv = proj[:, C_CKV:C_CKV + MLA_KV_RANK]
    ckv = ckv * lax.rsqrt(jnp.mean(ckv * ckv, axis=-1, keepdims=True) + RMS_EPS) * kvn_ref[...]
    ckv_ref[...] = ckv
    kr_ref[...] = _rope_chunk(proj[:, C_KR:C_KR + LANE], tm_ref, qm_shift)
    qm = _dot(cq.astype(bf16), wuq_ref[...])
    qm_ref[:, 0:MLA_NN] = qm[:, 0:MLA_NN].astype(bf16)
    for j in range(2):
        c0 = MLA_NN + LANE * j
        qm_ref[:, c0:c0 + LANE] = _rope_chunk(qm[:, c0:c0 + LANE], tm_ref, qm_shift).astype(bf16)
    kvl_ref[...] = _dot(ckv.astype(bf16), wukv_ref[...]).astype(bf16)


def _rope_block_index(i):
    tiles_ctx = T_CTX // TOK_TILE
    per_seq = DEN_L // TOK_TILE
    return jnp.where(i < tiles_ctx, 0, 1 + (i - tiles_ctx) % per_seq)


def _ab_proj(xc, xd, mods0, w_in_p, q_norm, kv_norm, w_uq_p, w_ukv_p):
    tab_a, qa_shift = _rope_table_array(A_HD)
    tab_m, qm_shift = _rope_table_array(MLA_ROPE)
    row_spec = lambda w: pl.BlockSpec((TOK_TILE, w), lambda i: (i, 0))
    xc_spec, xd_spec = _two_stream_specs(TOK_TILE, D)
    tab_spec = pl.BlockSpec((3, TOK_TILE, LANE), lambda i: (0, _rope_block_index(i), 0))
    outs = [(512, bf16), (LANE, f32), (LANE, f32), (LANE, f32), (LANE, f32), (768, bf16), (1024, bf16)]
    return pl.pallas_call(
        functools.partial(_ab_proj_kernel, qa_shift=qa_shift, qm_shift=qm_shift),
        grid=(T_ALL // TOK_TILE,),
        in_specs=[xc_spec, xd_spec, _full((8, 6 * D), 1), _full((D, PROJ_W), 1), _full((1, MLA_Q_RANK), 1),
                  _full((1, MLA_KV_RANK), 1), _full((MLA_Q_RANK, 768), 1), _full((MLA_KV_RANK, 1024), 1),
                  tab_spec, tab_spec],
        out_specs=[row_spec(w) for w, _ in outs],
        out_shape=[jax.ShapeDtypeStruct((T_ALL, w), dt) for w, dt in outs],
        compiler_params=_params(1),
        name="ab_proj",
    )(xc, xd, mods0, w_in_p, q_norm, kv_norm, w_uq_p, w_ukv_p, tab_a, tab_m)


def _softmax_blocks(s_refs, p_refs, sink_col=None):
    m = s_refs[0][...].max(axis=-1, keepdims=True)
    for s_ref in s_refs[1:]:
        m = jnp.maximum(m, s_ref[...].max(axis=-1, keepdims=True))
    if sink_col is not None:
        m = jnp.maximum(m, sink_col)
    l = None
    for s_ref, p_ref in zip(s_refs, p_refs):
        p = jnp.exp(s_ref[...] - m)
        p_ref[...] = p.astype(bf16)
        ps = p.sum(axis=-1, keepdims=True)
        l = ps if l is None else l + ps
    if sink_col is not None:
        l = l + jnp.exp(sink_col - m)
    return 1.0 / l


def _sink_column(sink_ref, rows_per_head):
    return jnp.concatenate([jnp.full((rows_per_head, 1), sink_ref[h], f32) for h in range(A_HEADS)], axis=0)


def _mla_q(qm_ref, h):
    rows = qm_ref.shape[0]
    return jnp.concatenate([qm_ref[:, MLA_NOPE * h:MLA_NOPE * (h + 1)],
                            qm_ref[:, MLA_NN + MLA_ROPE * h:MLA_NN + MLA_ROPE * (h + 1)],
                            jnp.zeros((rows, LANE - MLA_NOPE - MLA_ROPE), bf16)], axis=1)


def _mla_k(k_nope_h, k_rope):
    rows = k_nope_h.shape[0]
    return jnp.concatenate([k_nope_h, k_rope, jnp.zeros((rows, LANE - MLA_NOPE - MLA_ROPE), bf16)], axis=1)


def _mix_out_ln(merged_ref, wout_ref, x, mods_ref, r, g_ref, b_ref):
    out = _dot(merged_ref[...], wout_ref[...])
    gate = mods_ref[pl.ds(r, 1), 2 * D:3 * D]
    return _layer_norm(ALPHA * x + gate * out, g_ref[...], b_ref[...])


def _ctx_attn_kernel(sink_ref, qa_ref, ka_ref, va_ref, qm_ref, kvl_ref, kr_ref, x_ref, mods_ref, wout_ref,
                     g_ref, b_ref, o_ref, merged_ref, sa_ref, sm_ref, pa_ref, pm_ref):
    n = CTX_L
    ka = ka_ref[...].astype(bf16)
    va = va_ref[...].astype(bf16)
    for j in range(A_KV_HEADS):
        q4 = jnp.concatenate([qa_ref[:, A_HD * h:A_HD * (h + 1)] for h in range(A_GROUP * j, A_GROUP * (j + 1))],
                             axis=0)
        sa_ref[A_GROUP * n * j:A_GROUP * n * (j + 1), :] = _dot_nt(q4, ka[:, A_HD * j:A_HD * (j + 1)]) * A_SCALE
    kr = kr_ref[:, 0:MLA_ROPE].astype(bf16)
    for h in range(MLA_HEADS):
        k_cat = _mla_k(kvl_ref[:, MLA_NOPE * h:MLA_NOPE * (h + 1)], kr)
        sm_ref[n * h:n * (h + 1), :] = _dot_nt(_mla_q(qm_ref, h), k_cat) * MLA_SCALE
    rla = _softmax_blocks([sa_ref], [pa_ref], _sink_column(sink_ref, n))
    rlm = _softmax_blocks([sm_ref], [pm_ref])
    for j in range(A_KV_HEADS):
        rows = slice(A_GROUP * n * j, A_GROUP * n * (j + 1))
        o4 = _dot(pa_ref[rows, :], va[:, A_HD * j:A_HD * (j + 1)]) * rla[rows]
        for g in range(A_GROUP):
            h = A_GROUP * j + g
            merged_ref[:, A_HD * h:A_HD * (h + 1)] = o4[n * g:n * (g + 1)].astype(bf16)
    for h in range(MLA_HEADS):
        rows = slice(n * h, n * (h + 1))
        v = kvl_ref[:, MLA_NN + MLA_V * h:MLA_NN + MLA_V * (h + 1)]
        merged_ref[:, MLA_NN + MLA_V * h:MLA_NN + MLA_V * (h + 1)] = (_dot(pm_ref[rows, :], v) * rlm[rows]).astype(bf16)
    o_ref[...] = _mix_out_ln(merged_ref, wout_ref, x_ref[...], mods_ref, 0, g_ref, b_ref)


def _ctx_attn(sink, qa, ka, va, qm, kvl, kr, x_all, mods0, w_out, ln_g, ln_b):
    blk = lambda w: pl.BlockSpec((CTX_L, w), lambda b: (b, 0))
    return pl.pallas_call(
        _ctx_attn_kernel,
        grid=(N_CTX_B,),
        in_specs=[pl.BlockSpec(memory_space=pltpu.SMEM), blk(512), blk(LANE), blk(LANE), blk(768), blk(1024),
                  blk(LANE), blk(D), _full((8, 6 * D), 1), _full((D, D), 1), _full((1, D), 1), _full((1, D), 1)],
        out_specs=blk(D),
        out_shape=jax.ShapeDtypeStruct((T_CTX, D), f32),
        scratch_shapes=[pltpu.VMEM((CTX_L, D), bf16),
                        pltpu.VMEM((A_HEADS * CTX_L, CTX_L), f32), pltpu.VMEM((MLA_HEADS * CTX_L, CTX_L), f32),
                        pltpu.VMEM((A_HEADS * CTX_L, CTX_L), bf16), pltpu.VMEM((MLA_HEADS * CTX_L, CTX_L), bf16)],
        compiler_params=_params(1),
        name="ctx_attn",
    )(sink, qa, ka, va, qm, kvl, kr, x_all, mods0, w_out, ln_g, ln_b)


QB = 256
WIN = QB + 2 * WINDOW
DEN_BLK0 = T_CTX // DEN_L
MLA_KEYS = CTX_L + DEN_L


def _den_attn_kernel(sink_ref, qa_ref, ka_ref, va_ref, cak_ref, cav_ref, qm_ref, kvl_ref, kr_ref, cckv_ref, ckr_ref,
                     wukv_ref, x_ref, mods_ref, wout_ref, g_ref, b_ref, o_ref, merged_ref, kcat_ref, vcat_ref,
                     saw_ref, sac_ref, sm_ref, paw_ref, pac_ref, pm_ref):
    b = pl.program_id(0)
    n = pl.program_id(1)

    @pl.when(n == 0)
    def _():
        kvc = _dot(cckv_ref[0].astype(bf16), wukv_ref[...]).astype(bf16)
        kr_ctx = ckr_ref[0].astype(bf16)
        kr_lat = kr_ref[:, 0:MLA_ROPE].astype(bf16)
        for h in range(MLA_HEADS):
            ns = slice(MLA_NOPE * h, MLA_NOPE * (h + 1))
            kcat_ref[h, 0:CTX_L, :] = _mla_k(kvc[:, ns], kr_ctx)
            kcat_ref[h, CTX_L:MLA_KEYS, :] = _mla_k(kvl_ref[:, ns], kr_lat)
        vcat_ref[0:CTX_L, :] = kvc[:, MLA_NN:2 * MLA_NN]
        vcat_ref[CTX_L:MLA_KEYS, :] = kvl_ref[:, MLA_NN:2 * MLA_NN]

    start = pl.multiple_of(jnp.clip(QB * n - WINDOW, 0, DEN_L - WIN), WINDOW)
    grp_rows = A_GROUP * QB
    qpos = QB * n + (lax.broadcasted_iota(jnp.int32, (grp_rows, WIN), 0) & (QB - 1))
    kpos = start + lax.broadcasted_iota(jnp.int32, (grp_rows, WIN), 1)
    valid = jnp.abs(qpos - kpos) <= WINDOW
    kwin = ka_ref[pl.ds(start, WIN), :].astype(bf16)
    vwin = va_ref[pl.ds(start, WIN), :].astype(bf16)
    kctx = cak_ref[0].astype(bf16)
    vctx = cav_ref[0].astype(bf16)
    for j in range(A_KV_HEADS):
        sl = slice(A_HD * j, A_HD * (j + 1))
        rows = slice(grp_rows * j, grp_rows * (j + 1))
        q4 = jnp.concatenate([qa_ref[:, A_HD * h:A_HD * (h + 1)] for h in range(A_GROUP * j, A_GROUP * (j + 1))],
                             axis=0)
        saw_ref[rows, :] = jnp.where(valid, _dot_nt(q4, kwin[:, sl]) * A_SCALE, NEG_INF)
        sac_ref[rows, :] = _dot_nt(q4, kctx[:, sl]) * A_SCALE
    for h in range(MLA_HEADS):
        sm_ref[QB * h:QB * (h + 1), :] = _dot_nt(_mla_q(qm_ref, h), kcat_ref[h]) * MLA_SCALE
    rla = _softmax_blocks([saw_ref, sac_ref], [paw_ref, pac_ref], _sink_column(sink_ref, QB))
    rlm = _softmax_blocks([sm_ref], [pm_ref])
    for j in range(A_KV_HEADS):
        sl = slice(A_HD * j, A_HD * (j + 1))
        rows = slice(grp_rows * j, grp_rows * (j + 1))
        o4 = (_dot(paw_ref[rows, :], vwin[:, sl]) + _dot(pac_ref[rows, :], vctx[:, sl])) * rla[rows]
        for g in range(A_GROUP):
            h = A_GROUP * j + g
            merged_ref[:, A_HD * h:A_HD * (h + 1)] = o4[QB * g:QB * (g + 1)].astype(bf16)
    for h in range(MLA_HEADS):
        rows = slice(QB * h, QB * (h + 1))
        o = _dot(pm_ref[rows, :], vcat_ref[:, MLA_V * h:MLA_V * (h + 1)]) * rlm[rows]
        merged_ref[:, MLA_NN + MLA_V * h:MLA_NN + MLA_V * (h + 1)] = o.astype(bf16)
    o_ref[...] = _mix_out_ln(merged_ref, wout_ref, x_ref[...], mods_ref, 1 + b, g_ref, b_ref)


def _den_attn(sink, qa, ka, va, cache_k, cache_v, qm, kvl, kr, cache_ckv, cache_kr, w_ukv_p, x_all, mods0, w_out,
              ln_g, ln_b):
    nq = DEN_L // QB
    qblk = lambda w: pl.BlockSpec((QB, w), lambda b, n: (T_CTX // QB + b * nq + n, 0))
    seq = lambda w: pl.BlockSpec((DEN_L, w), lambda b, n: (DEN_BLK0 + b, 0))
    cache = lambda w: pl.BlockSpec((1, CTX_L, w), lambda b, n: (b, 0, 0))
    return pl.pallas_call(
        _den_attn_kernel,
        grid=(N_DEN_B, nq),
        in_specs=[pl.BlockSpec(memory_space=pltpu.SMEM), qblk(512), seq(LANE), seq(LANE), cache(LANE), cache(LANE),
                  qblk(768), seq(1024), seq(LANE), cache(MLA_KV_RANK), cache(MLA_ROPE),
                  _full((MLA_KV_RANK, 1024), 2), pl.BlockSpec((QB, D), lambda b, n: (b * nq + n, 0)),
                  _full((8, 6 * D), 2), _full((D, D), 2), _full((1, D), 2),
                  _full((1, D), 2)],
        out_specs=pl.BlockSpec((QB, D), lambda b, n: (b * nq + n, 0)),
        out_shape=jax.ShapeDtypeStruct((T_DEN, D), f32),
        scratch_shapes=[pltpu.VMEM((QB, D), bf16), pltpu.VMEM((MLA_HEADS, MLA_KEYS, LANE), bf16),
                        pltpu.VMEM((MLA_KEYS, MLA_NN), bf16)]
        + [pltpu.VMEM((A_HEADS * QB, w), dt) for dt in (f32, bf16) for w in (WIN, CTX_L, MLA_KEYS)],
        compiler_params=_params(2),
        name="den_attn",
    )(sink, qa, ka, va, cache_k, cache_v, qm, kvl, kr, cache_ckv, cache_kr, w_ukv_p, x_all, mods0, w_out, ln_g, ln_b)


def _route(x1, mrow, rw_ref, rb_ref, h_ref, gates_ref):
    sh, sc = mrow[:, 3 * D:4 * D], mrow[:, 4 * D:5 * D]
    h = x1 * (1.0 + sc) + sh
    h_hi = h.astype(bf16)
    h_ref[...] = h_hi
    h_lo = (h - h_hi.astype(f32)).astype(bf16)
    logits = _dot(h_hi, rw_ref[0]) + (_dot(h_hi, rw_ref[1]) + _dot(h_lo, rw_ref[0]))
    scores = jax.nn.sigmoid(logits)
    lane = lax.broadcasted_iota(jnp.int32, scores.shape, 1).astype(f32)
    sel = jnp.where(lane < N_EXPERTS, scores + rb_ref[...], -jnp.inf)
    gates = jnp.zeros_like(scores)
    for _ in range(TOP_K):
        m = sel.max(axis=-1, keepdims=True)
        idx = jnp.where(sel == m, lane, float(LANE)).min(axis=-1, keepdims=True)
        hit = lane == idx
        gates = jnp.where(hit, scores, gates)
        sel = jnp.where(hit, -jnp.inf, sel)
    gates_ref[...] = gates / gates.sum(axis=-1, keepdims=True) * ROUTED_SCALE


def _router_kernel(xc_ref, xd_ref, mods_ref, rw_ref, rb_ref, x_ref, h_ref, gates_ref):
    i = pl.program_id(0)
    r = _mod_row(i, TOK_TILE)
    x1 = _pick(i, TOK_TILE, xc_ref, xd_ref)
    x_ref[...] = x1
    _route(x1, mods_ref[pl.ds(r, 1), :], rw_ref, rb_ref, h_ref, gates_ref)


def _router(x1c, x1d, mods_l, router_w_p, router_b_p):
    row_spec = lambda w: pl.BlockSpec((TOK_TILE, w), lambda i: (i, 0))
    xc_spec, xd_spec = _two_stream_specs(TOK_TILE, D)
    return pl.pallas_call(
        _router_kernel,
        grid=(T_ALL // TOK_TILE,),
        in_specs=[xc_spec, xd_spec, _full((8, 6 * D), 1), _full((2, D, LANE), 1), _full((1, LANE), 1)],
        out_specs=[row_spec(D), row_spec(D), row_spec(LANE)],
        out_shape=[jax.ShapeDtypeStruct((T_ALL, D), f32), jax.ShapeDtypeStruct((T_ALL, D), bf16),
                   jax.ShapeDtypeStruct((T_ALL, LANE), f32)],
        compiler_params=_params(1),
        name="router",
    )(x1c, x1d, mods_l, router_w_p, router_b_p)


MOE_TOK = 1536
MOE_EG = 4
MOE_TILE = 512
MOE_FF = MOE_EG * EXPERT_FF


def _moe_kernel(h_ref, gates_ref, x_ref, mods_ref, wg_ref, wu_ref, wd_ref, sg_ref, su_ref, sd_ref, g_ref, b_ref,
                o_ref):
    p = pl.program_id(0)
    e = pl.program_id(1)
    n_tiles = MOE_TOK // MOE_TILE

    def gate_f(t):
        r = _mod_row(p * n_tiles + t, MOE_TILE)
        return mods_ref[pl.ds(r, 1), 5 * D:6 * D]

    def rows_of(t):
        return pl.ds(pl.multiple_of(t * MOE_TILE, MOE_TILE), MOE_TILE)

    @pl.when(e == 0)
    def _():
        sg = sg_ref[...].astype(bf16)
        su = su_ref[...].astype(bf16)
        sd = sd_ref[...].astype(bf16)

        def body(t, c):
            rows = rows_of(t)
            ht = h_ref[rows, :]
            hid = _silu(_dot(ht, sg)) * _dot(ht, su)
            o_ref[rows, :] = ALPHA * x_ref[rows, :] + gate_f(t) * _dot(hid.astype(bf16), sd)
            return c

        lax.fori_loop(0, n_tiles, body, 0)

    wg = jnp.concatenate([wg_ref[k].astype(bf16) for k in range(MOE_EG)], axis=1)
    wu = jnp.concatenate([wu_ref[k].astype(bf16) for k in range(MOE_EG)], axis=1)
    wd = jnp.concatenate([wd_ref[k].astype(bf16) for k in range(MOE_EG)], axis=0)
    lane = lax.broadcasted_iota(jnp.int32, (MOE_TILE, LANE), 1)

    def body(t, c):
        rows = rows_of(t)
        ht = h_ref[rows, :]
        hid = _silu(_dot(ht, wg)) * _dot(ht, wu)
        gt = gates_ref[rows, :]
        parts = []
        for k in range(MOE_EG):
            col = jnp.where(lane == e * MOE_EG + k, gt, 0.0).sum(axis=-1, keepdims=True)
            parts.append((hid[:, EXPERT_FF * k:EXPERT_FF * (k + 1)] * col).astype(bf16))
        o_ref[rows, :] += gate_f(t) * _dot(jnp.concatenate(parts, axis=1), wd)
        return c

    lax.fori_loop(0, n_tiles, body, 0)

    @pl.when(e == pl.num_programs(1) - 1)
    def _():
        def body(t, c):
            rows = rows_of(t)
            o_ref[rows, :] = _layer_norm(o_ref[rows, :], g_ref[...], b_ref[...])
            return c

        lax.fori_loop(0, n_tiles, body, 0)


def _moe(l, h, gates, x1, mods_l, wg, wu, wd, sg, su, sd, ln_g, ln_b):
    tok = lambda w: pl.BlockSpec((MOE_TOK, w), lambda p, e: (p, 0))
    return pl.pallas_call(
        _moe_kernel,
        grid=(T_ALL // MOE_TOK, N_EXPERTS // MOE_EG),
        in_specs=[tok(D), tok(LANE), tok(D), _full((8, 6 * D), 2),
                  pl.BlockSpec((None, MOE_EG, D, EXPERT_FF), lambda p, e: (l, e, 0, 0)),
                  pl.BlockSpec((None, MOE_EG, D, EXPERT_FF), lambda p, e: (l, e, 0, 0)),
                  pl.BlockSpec((None, MOE_EG, EXPERT_FF, D), lambda p, e: (l, e, 0, 0)),
                  pl.BlockSpec((None, D, SHARED_FF), lambda p, e: (l, 0, 0)),
                  pl.BlockSpec((None, D, SHARED_FF), lambda p, e: (l, 0, 0)),
                  pl.BlockSpec((None, SHARED_FF, D), lambda p, e: (l, 0, 0)),
                  _full((1, D), 2), _full((1, D), 2)],
        out_specs=tok(D),
        out_shape=jax.ShapeDtypeStruct((T_ALL, D), f32),
        compiler_params=_params(2),
        name="moe",
    )(h, gates, x1, mods_l, wg, wu, wd, sg, su, sd, ln_g, ln_b)


def _router_weights(l, router_w, router_bias):
    rw = jnp.pad(router_w[l], ((0, 0), (0, LANE - N_EXPERTS)))
    rw_hi, rw_lo = _split_bf16(rw)
    rb = jnp.pad(router_bias[l], (0, LANE - N_EXPERTS)).reshape(1, LANE)
    return jnp.stack([rw_hi, rw_lo]), rb


S5_Q = 8
S5_NGB = D // LANE
S5_TILE = 256
S5_KT = S5_TILE // S5_Q


def _s5_in_kernel(x_ref, mods_ref, w_ref, u_ref, u2_ref, slab_ref, *, row_of):
    r = row_of(pl.program_id(0))
    mrow = mods_ref[pl.ds(r, 1), :]
    sh, sc = mrow[:, 0:D], mrow[:, D:2 * D]
    h = (x_ref[...] * (1.0 + sc) + sh).astype(bf16)
    u = _dot(h, w_ref[...])
    u_ref[...] = u
    for s in range(S5_NGB):
        slab_ref[s] = u[:, LANE * s:LANE * (s + 1)]
    for s in range(S5_NGB):
        for j in range(S5_Q):
            u2_ref[s, :, LANE * j:LANE * (j + 1)] = slab_ref[s, pl.ds(j, S5_KT, stride=S5_Q), :].astype(bf16)


def _s5_in(x_all, tile0, mods1, w_in_c, n_b, seq_len, row_of):
    tiles_per_seq = seq_len // S5_TILE
    return pl.pallas_call(
        functools.partial(_s5_in_kernel, row_of=row_of),
        grid=(n_b * tiles_per_seq,),
        in_specs=[pl.BlockSpec((S5_TILE, D), lambda i: (tile0 + i, 0)), _full((8, 6 * D), 1), _full((D, D), 1)],
        out_specs=[pl.BlockSpec((S5_TILE, D), lambda i: (i, 0)),
                   pl.BlockSpec((S5_NGB, S5_KT, D), lambda i: (0, i % tiles_per_seq, i // tiles_per_seq))],
        out_shape=[jax.ShapeDtypeStruct((n_b * seq_len, D), f32),
                   jax.ShapeDtypeStruct((S5_NGB, seq_len // S5_Q, n_b * D), bf16)],
        scratch_shapes=[pltpu.VMEM((S5_NGB, S5_TILE, LANE), f32)],
        compiler_params=_params(1),
        name="s5_in",
    )(x_all, mods1, w_in_c)


S5_GL = (LANE // S5_CH) * S5_P
S5_ROWS_C = (CTX_L // S5_Q) * N_CTX_B
S5_ROWS_D = (DEN_L // S5_Q) * N_DEN_B


def _s5_scan_kernel(lre_ref, lim_ref, ldt_ref, btr_ref, bti_ref, ctr_ref, cti_ref, uc_ref, ud_ref, h0_ref,
                    yc_ref, yd_ref, st_ref, win_ref, mso_ref, wit_ref, a_ref, s_ref, hp_ref):
    gl = S5_GL
    rowg = lax.shift_right_logical(lax.broadcasted_iota(jnp.int32, (LANE, gl), 0), 4)
    colg = lax.shift_right_logical(lax.broadcasted_iota(jnp.int32, (LANE, gl), 1), 6)
    same_group = rowg == colg
    reps = LANE // S5_CH

    def expand(t):
        return jnp.where(same_group, jnp.concatenate([t] * reps, axis=0), 0.0)

    for d in range(2):
        fwd = d == 0
        lre, lim = lre_ref[d], lim_ref[d]
        dt = jnp.exp(ldt_ref[d])
        a, w = lre * dt, lim * dt
        pre = [jnp.exp(m * a) * jnp.cos(m * w) for m in range(S5_Q + 1)]
        pim = [jnp.exp(m * a) * jnp.sin(m * w) for m in range(S5_Q + 1)]
        xr, xi = pre[1] - 1.0, pim[1]
        den = lre * lre + lim * lim
        cfr, cfi = (xr * lre + xi * lim) / den, (xi * lre - xr * lim) / den
        btr, bti = btr_ref[d], bti_ref[d]
        bexp_r = expand(cfr * btr - cfi * bti)
        bexp_i = expand(cfr * bti + cfi * btr)
        cexp_r, cexp_i = expand(ctr_ref[d]), expand(cti_ref[d])
        for m in range(S5_Q + 1):
            a_ref[m, :, 0:gl] = cexp_r * pre[m] - cexp_i * pim[m]
            a_ref[m, :, gl:2 * gl] = -(cexp_r * pim[m] + cexp_i * pre[m])
        for j in range(S5_Q):
            m = S5_Q - 1 - j if fwd else j
            win_ref[LANE * j:LANE * (j + 1), 0:gl] = (pre[m] * bexp_r - pim[m] * bexp_i).astype(bf16)
            win_ref[LANE * j:LANE * (j + 1), gl:2 * gl] = (pre[m] * bexp_i + pim[m] * bexp_r).astype(bf16)
        for j in range(S5_Q):
            m = j + 1 if fwd else S5_Q - j
            mso_ref[LANE * j:LANE * (j + 1), :] = a_ref[m].astype(bf16)
        b2_hi, b2_lo = _split_bf16(jnp.concatenate([bexp_r, bexp_i], axis=1))
        kt = []
        for tau in range(S5_Q):
            a_hi, a_lo = _split_bf16(a_ref[tau])
            kt.append((_dot_nt(b2_hi, a_hi) + _dot_nt(b2_hi, a_lo) + _dot_nt(b2_lo, a_hi)).astype(bf16))
        zero_blk = jnp.zeros((LANE, LANE), bf16)
        for j in range(S5_Q):
            for jp in range(S5_Q):
                tau = jp - j if fwd else j - jp
                wit_ref[LANE * j:LANE * (j + 1), LANE * jp:LANE * (jp + 1)] = kt[tau] if tau >= 0 else zero_blk

        l8r, l8i = pre[S5_Q], pim[S5_Q]

        def run(u_ref, y_ref, n_b, n_k, h_init):
            rows = n_b * n_k
            s_ref[0:rows, :] = _dot(u_ref[0], win_ref[...])

            per_it = max(1, 8 // n_b)
            n_it = n_k // per_it
            it_rows = per_it * n_b

            def step(i, carry):
                hr, hi_ = carry
                it = i if fwd else n_it - 1 - i
                rs = pl.ds(pl.multiple_of(it * it_rows, it_rows), it_rows)
                s_tile = s_ref[rs, :]
                prev_r, prev_i = [None] * per_it, [None] * per_it
                for sub in (range(per_it) if fwd else reversed(range(per_it))):
                    prev_r[sub], prev_i[sub] = hr, hi_
                    sr = s_tile[sub * n_b:(sub + 1) * n_b, 0:gl]
                    si = s_tile[sub * n_b:(sub + 1) * n_b, gl:2 * gl]
                    hr, hi_ = l8r * hr - l8i * hi_ + sr, l8r * hi_ + l8i * hr + si
                hp_ref[rs, 0:gl] = jnp.concatenate(prev_r, axis=0) if per_it > 1 else prev_r[0]
                hp_ref[rs, gl:2 * gl] = jnp.concatenate(prev_i, axis=0) if per_it > 1 else prev_i[0]
                return hr, hi_

            h_fin = lax.fori_loop(0, n_it, step, h_init)
            y = _dot_nt(hp_ref[0:rows, :].astype(bf16), mso_ref[...]) + _dot(u_ref[0], wit_ref[...])
            if fwd:
                y_ref[0] = y
            else:
                y_ref[0] += y
            return h_fin

        zeros = jnp.zeros((N_CTX_B, gl), f32)
        hr, hi_ = run(uc_ref, yc_ref, N_CTX_B, CTX_L // S5_Q, (zeros, zeros))
        st_ref[d, 0] = hr
        st_ref[d, 1] = hi_
        run(ud_ref, yd_ref, N_DEN_B, DEN_L // S5_Q, (h0_ref[d, 0], h0_ref[d, 1]))


def _s5_scan(lam_re, lam_im, log_dt, bt_re, bt_im, ct_re, ct_im, u2c, u2d, h0):
    gl = S5_GL
    vec = pl.BlockSpec((2, 1, gl), lambda g: (0, 0, g))
    tab = pl.BlockSpec((2, S5_CH, gl), lambda g: (0, 0, g))
    rows = lambda n: pl.BlockSpec((1, n, D), lambda g: (g, 0, 0))
    return pl.pallas_call(
        _s5_scan_kernel,
        grid=(S5_NGB,),
        in_specs=[vec, vec, vec, tab, tab, tab, tab, rows(S5_ROWS_C), rows(S5_ROWS_D),
                  pl.BlockSpec((2, 2, N_DEN_B, gl), lambda g: (0, 0, 0, g))],
        out_specs=[rows(S5_ROWS_C), rows(S5_ROWS_D), pl.BlockSpec((2, 2, N_CTX_B, gl), lambda g: (0, 0, 0, g))],
        out_shape=[jax.ShapeDtypeStruct((S5_NGB, S5_ROWS_C, D), f32), jax.ShapeDtypeStruct((S5_NGB, S5_ROWS_D, D), f32),
                   jax.ShapeDtypeStruct((2, 2, N_CTX_B, S5_G * S5_P), f32)],
        scratch_shapes=[pltpu.VMEM((D, 2 * gl), bf16), pltpu.VMEM((D, 2 * gl), bf16), pltpu.VMEM((D, D), bf16),
                        pltpu.VMEM((S5_Q + 1, LANE, 2 * gl), f32), pltpu.VMEM((S5_ROWS_C, 2 * gl), f32),
                        pltpu.VMEM((S5_ROWS_C, 2 * gl), f32)],
        compiler_params=_params(1),
        name="s5_scan",
    )(lam_re, lam_im, log_dt, bt_re, bt_im, ct_re, ct_im, u2c, u2d, h0)


S5_CTX_TILES = T_CTX // S5_TILE
S5_DEN_TPS = DEN_L // S5_TILE


def _gelu_tanh(x):
    return 0.5 * x * (1.0 + jnp.tanh(np.sqrt(2.0 / np.pi).astype(np.float32) * (x + 0.044715 * (x * x * x))))


def _s5_out_kernel(x_ref, uc_ref, ud_ref, yc_ref, yd_ref, mods_ref, dsk_ref, wout_ref, g_ref, b_ref, rw_ref, rb_ref,
                   x1_ref, h_ref, gates_ref, slab_ref):
    i = pl.program_id(0)
    is_ctx = i < S5_CTX_TILES
    r = _mod_row(i, S5_TILE)
    mrow = mods_ref[pl.ds(r, 1), :]
    u = _pick(i, S5_TILE, uc_ref, ud_ref)

    def unchunk(y_ref):
        for s in range(S5_NGB):
            for j in range(S5_Q):
                slab_ref[s, pl.ds(j, S5_KT, stride=S5_Q), :] = y_ref[s, :, LANE * j:LANE * (j + 1)]

    lax.cond(is_ctx, lambda: unchunk(yc_ref), lambda: unchunk(yd_ref))
    y = jnp.concatenate([slab_ref[s] for s in range(S5_NGB)], axis=1) + dsk_ref[...] * u
    z = _dot(_gelu_tanh(y).astype(bf16), wout_ref[...])
    out = z[:, 0:D] * jax.nn.sigmoid(z[:, D:2 * D])
    x1 = _layer_norm(ALPHA * x_ref[...] + mrow[:, 2 * D:3 * D] * out, g_ref[...], b_ref[...])
    x1_ref[...] = x1
    _route(x1, mrow, rw_ref, rb_ref, h_ref, gates_ref)


def _s5_out(x_all, uc, ud, yc, yd, mods1, d_skip, w_out_c, ln_g, ln_b, rw, rb):
    row_spec = lambda w: pl.BlockSpec((S5_TILE, w), lambda i: (i, 0))
    ci = lambda i: jnp.minimum(i, S5_CTX_TILES - 1)
    di = lambda i: jnp.maximum(i - S5_CTX_TILES, 0)
    return pl.pallas_call(
        _s5_out_kernel,
        grid=(T_ALL // S5_TILE,),
        in_specs=[row_spec(D),
                  pl.BlockSpec((S5_TILE, D), lambda i: (ci(i), 0)),
                  pl.BlockSpec((S5_TILE, D), lambda i: (di(i), 0)),
                  pl.BlockSpec((S5_NGB, S5_KT, D), lambda i: (0, 0, ci(i))),
                  pl.BlockSpec((S5_NGB, S5_KT, D), lambda i: (0, di(i) % S5_DEN_TPS, di(i) // S5_DEN_TPS)),
                  _full((8, 6 * D), 1), _full((1, D), 1), _full((D, 2 * D), 1), _full((1, D), 1), _full((1, D), 1),
                  _full((2, D, LANE), 1), _full((1, LANE), 1)],
        out_specs=[row_spec(D), row_spec(D), row_spec(LANE)],
        out_shape=[jax.ShapeDtypeStruct((T_ALL, D), f32), jax.ShapeDtypeStruct((T_ALL, D), bf16),
                   jax.ShapeDtypeStruct((T_ALL, LANE), f32)],
        scratch_shapes=[pltpu.VMEM((S5_NGB, S5_TILE, LANE), f32)],
        compiler_params=_params(1),
        name="s5_out",
    )(x_all, uc, ud, yc, yd, mods1, d_skip, w_out_c, ln_g, ln_b, rw, rb)


def kernel(x_prompt, x_sample, c, cache_attn_k, cache_attn_v, cache_mla_ckv, cache_mla_krope, state_ssm, c_ctx,
           ada_w, ada_b, ln_mix_g, ln_mix_b, ln_ffn_g, ln_ffn_b, w_in_ab, attn_sink, mla_q_norm, mla_kv_norm,
           mla_w_uq, mla_w_ukv, w_out_ab, w_in_c, s5_lam_re, s5_lam_im, s5_log_dt, s5_b_re, s5_b_im, s5_c_re,
           s5_c_im, s5_d, w_out_c, router_w, router_bias, exp_w_gate, exp_w_up, exp_w_down, sh_w_gate, sh_w_up,
           sh_w_down):
    row = lambda v: v.reshape(1, -1)
    xc, xd = x_prompt.reshape(T_CTX, D), x_sample.reshape(T_DEN, D)
    cvec8 = jnp.concatenate([c_ctx[None, :], c, jnp.zeros((8 - 1 - N_DEN_B, D), f32)], axis=0)
    mods = _adaln(cvec8, ada_w, ada_b)

    w_in_p = jnp.pad(w_in_ab[0], ((0, 0), (0, PROJ_W - w_in_ab.shape[-1]))).astype(bf16)
    uq = mla_w_uq[0].reshape(MLA_Q_RANK, MLA_HEADS, MLA_NOPE + MLA_ROPE)
    w_uq_p = jnp.concatenate([uq[:, :, :MLA_NOPE].reshape(MLA_Q_RANK, -1), uq[:, :, MLA_NOPE:].reshape(MLA_Q_RANK, -1)],
                             axis=1).astype(bf16)
    ukv = mla_w_ukv[0].reshape(MLA_KV_RANK, MLA_HEADS, MLA_NOPE + MLA_V)
    w_ukv_p = jnp.concatenate([ukv[:, :, :MLA_NOPE].reshape(MLA_KV_RANK, -1),
                               ukv[:, :, MLA_NOPE:].reshape(MLA_KV_RANK, -1)], axis=1).astype(bf16)
    qa, ka, va, ckv, kr, qm, kvl = _ab_proj(xc, xd, mods[0], w_in_p, row(mla_q_norm[0]), row(mla_kv_norm[0]),
                                            w_uq_p, w_ukv_p)
    w_out_b = w_out_ab[0].astype(bf16)
    g0, b0 = row(ln_mix_g[0]), row(ln_mix_b[0])
    x1c = _ctx_attn(attn_sink[0], qa, ka, va, qm, kvl, kr, xc, mods[0], w_out_b, g0, b0)
    x1d = _den_attn(attn_sink[0], qa, ka, va,
                    cache_attn_k[:, 0].reshape(N_DEN_B, CTX_L, A_KV_HEADS * A_HD),
                    cache_attn_v[:, 0].reshape(N_DEN_B, CTX_L, A_KV_HEADS * A_HD),
                    qm, kvl, kr, cache_mla_ckv[:, 0], cache_mla_krope[:, 0], w_ukv_p, xd, mods[0], w_out_b, g0, b0)
    rw0, rb0 = _router_weights(0, router_w, router_bias)
    x1, h, gates = _router(x1c, x1d, mods[0], rw0, rb0)
    x2 = _moe(0, h, gates, x1, mods[0], exp_w_gate, exp_w_up, exp_w_down, sh_w_gate, sh_w_up, sh_w_down,
              row(ln_ffn_g[0]), row(ln_ffn_b[0]))

    w_in_c_b = w_in_c[0].astype(bf16)
    uc, u2c = _s5_in(x2, 0, mods[1], w_in_c_b, N_CTX_B, CTX_L, lambda i: 0)
    ud, u2d = _s5_in(x2, S5_CTX_TILES, mods[1], w_in_c_b, N_DEN_B, DEN_L, lambda i: 1 + i // S5_DEN_TPS)
    gp = S5_G * S5_P
    chan_major_b = lambda t: jnp.transpose(t[0], (0, 3, 1, 2)).reshape(2, S5_CH, gp)
    chan_major_c = lambda t: jnp.transpose(t[0], (0, 2, 1, 3)).reshape(2, S5_CH, gp)
    h0 = jnp.transpose(state_ssm[:, 0], (1, 2, 0, 3, 4)).reshape(2, 2, N_DEN_B, gp)
    yc, yd, st = _s5_scan(s5_lam_re[0].reshape(2, 1, gp), s5_lam_im[0].reshape(2, 1, gp),
                          jnp.repeat(s5_log_dt[0], S5_P, axis=-1).reshape(2, 1, gp),
                          chan_major_b(s5_b_re), chan_major_b(s5_b_im), chan_major_c(s5_c_re), chan_major_c(s5_c_im),
                          u2c.reshape(S5_NGB, S5_ROWS_C, D), u2d.reshape(S5_NGB, S5_ROWS_D, D), h0)
    rw1, rb1 = _router_weights(1, router_w, router_bias)
    x3, h, gates = _s5_out(x2, uc, ud, yc.reshape(S5_NGB, CTX_L // S5_Q, N_CTX_B * D),
                           yd.reshape(S5_NGB, DEN_L // S5_Q, N_DEN_B * D), mods[1], row(s5_d[0]),
                           w_out_c[0].astype(bf16), row(ln_mix_g[1]), row(ln_mix_b[1]), rw1, rb1)
    x4 = _moe(1, h, gates, x3, mods[1], exp_w_gate, exp_w_up, exp_w_down, sh_w_gate, sh_w_up, sh_w_down,
              row(ln_ffn_g[1]), row(ln_ffn_b[1]))

    y_prompt = x4[:T_CTX].reshape(N_CTX_B, CTX_L, D)
    y_sample = x4[T_CTX:].reshape(N_DEN_B, DEN_L, D)
    new_attn_k = ka[:T_CTX].reshape(N_CTX_B, 1, CTX_L, A_KV_HEADS, A_HD)
    new_attn_v = va[:T_CTX].reshape(N_CTX_B, 1, CTX_L, A_KV_HEADS, A_HD)
    new_mla_ckv = ckv[:T_CTX].reshape(N_CTX_B, 1, CTX_L, MLA_KV_RANK)
    new_mla_krope = kr[:T_CTX, :MLA_ROPE].reshape(N_CTX_B, 1, CTX_L, MLA_ROPE)
    new_state_ssm = jnp.transpose(st, (2, 0, 1, 3)).reshape(N_CTX_B, 1, 2, 2, S5_G, S5_P)
    return (y_prompt, y_sample, new_attn_k, new_attn_v, new_mla_ckv, new_mla_krope, new_state_ssm)
```

```python
import functools

import jax
import jax.numpy as jnp
import numpy as np
from jax import lax
from jax.experimental import pallas as pl
from jax.experimental.pallas import tpu as pltpu

f32 = jnp.float32
bf16 = jnp.bfloat16

D = 1024
N_CTX_B, CTX_L = 16, 256
N_DEN_B, DEN_L = 2, 1024
T_CTX = N_CTX_B * CTX_L
T_DEN = N_DEN_B * DEN_L
T_ALL = T_CTX + T_DEN
GRID_W = 64
WINDOW = 128
ROPE_BASE = 10000.0
A_HEADS, A_KV_HEADS, A_HD = 8, 2, 64
A_GROUP = A_HEADS // A_KV_HEADS
A_SCALE = A_HD ** -0.5
MLA_HEADS, MLA_Q_RANK, MLA_KV_RANK = 8, 256, 128
MLA_NOPE, MLA_ROPE, MLA_V = 64, 32, 64
MLA_SCALE = (MLA_NOPE + MLA_ROPE) ** -0.5
N_EXPERTS, TOP_K, EXPERT_FF, SHARED_FF = 64, 6, 128, 128
ROUTED_SCALE = 2.5
DEPTH = 2
ALPHA = (2.0 * DEPTH) ** 0.25
LN_EPS = 1e-5
RMS_EPS = 1e-6
NEG_INF = -1e30
S5_G, S5_CH, S5_P = 64, 16, 64

LANE = 128
SUB = 8
VMEM_LIMIT = 56 * 1024 * 1024

TOK_TILE = 512


def _mod_row(tile_idx, tile_rows):
    start = tile_idx * tile_rows
    return jnp.where(start < T_CTX, 0, 1 + (start - T_CTX) // DEN_L)


def _layer_norm(y, g, b):
    mu = jnp.mean(y, axis=-1, keepdims=True)
    yc = y - mu
    var = jnp.mean(yc * yc, axis=-1, keepdims=True)
    return yc * lax.rsqrt(var + LN_EPS) * g + b


def _silu(x):
    return x * jax.nn.sigmoid(x)


def _dot(a, b):
    return jnp.dot(a, b, preferred_element_type=f32)


def _dot_nt(a, b):
    return lax.dot_general(a, b, (((1,), (1,)), ((), ())), preferred_element_type=f32)


def _split_bf16(a):
    hi = a.astype(bf16)
    return hi, (a - hi.astype(f32)).astype(bf16)


def _full(shape, n_grid):
    zeros = tuple(0 for _ in shape)
    return pl.BlockSpec(shape, lambda *_: zeros)


def _two_stream_specs(tile_rows, width):
    n_ctx = T_CTX // tile_rows
    return (pl.BlockSpec((tile_rows, width), lambda i: (jnp.minimum(i, n_ctx - 1), 0)),
            pl.BlockSpec((tile_rows, width), lambda i: (jnp.maximum(i - n_ctx, 0), 0)))


def _pick(i, tile_rows, ctx_ref, den_ref):
    return lax.cond(i < T_CTX // tile_rows, lambda: ctx_ref[...], lambda: den_ref[...])


def _params(n_grid):
    return pltpu.CompilerParams(dimension_semantics=("arbitrary",) * n_grid, vmem_limit_bytes=VMEM_LIMIT)


ADA_TN = 1536


def _adaln_kernel(c_ref, w_ref, b_ref, o_ref):
    s_hi, s_lo = _split_bf16(_silu(c_ref[...]))
    w_hi, w_lo = _split_bf16(w_ref[0])
    o_ref[0] = _dot(s_hi, w_hi) + (_dot(s_hi, w_lo) + _dot(s_lo, w_hi)) + b_ref[0]


def _adaln(cvec8, ada_w, ada_b):
    n = 6 * D
    return pl.pallas_call(
        _adaln_kernel,
        grid=(DEPTH, n // ADA_TN),
        in_specs=[
            pl.BlockSpec((8, D), lambda l, j: (0, 0)),
            pl.BlockSpec((1, D, ADA_TN), lambda l, j: (l, 0, j)),
            pl.BlockSpec((1, 1, ADA_TN), lambda l, j: (l, 0, j)),
        ],
        out_specs=pl.BlockSpec((1, 8, ADA_TN), lambda l, j: (l, 0, j)),
        out_shape=jax.ShapeDtypeStruct((DEPTH, 8, n), f32),
        compiler_params=_params(2),
        name="adaln",
    )(cvec8, ada_w, ada_b.reshape(DEPTH, 1, n))


def _rope_table_array(head_dim):
    q = head_dim // 4
    pos = np.arange(DEN_L)
    row, col = (pos // GRID_W).astype(np.float64), (pos % GRID_W).astype(np.float64)
    lane = np.arange(LANE) % head_dim
    is_col = lane >= head_dim // 2
    w = lane % (head_dim // 2)
    first = w < q
    inv_freq = ROPE_BASE ** (-np.arange(q, dtype=np.float64) / q)
    ang = np.where(is_col[None, :], col[:, None], row[:, None]) * inv_freq[w % q][None, :]
    cos, sin = np.cos(ang), np.sin(ang)
    sin_a = np.where(first[None, :], -sin, 0.0)
    sin_b = np.where(first[None, :], 0.0, sin)
    ident = np.stack([np.ones((TOK_TILE, LANE)), np.zeros((TOK_TILE, LANE)), np.zeros((TOK_TILE, LANE))])
    tab = np.concatenate([ident, np.stack([cos, sin_a, sin_b])], axis=1).astype(np.float32)
    return jnp.asarray(tab), q


def _rope_chunk(x, tab_ref, q):
    return x * tab_ref[0] + pltpu.roll(x, LANE - q, 1) * tab_ref[1] + pltpu.roll(x, q, 1) * tab_ref[2]


PROJ_W = 1280
C_QA, C_KA, C_VA, C_CQ, C_CKV, C_KR = 0, 512, 640, 768, 1024, 1152
MLA_NN = MLA_HEADS * MLA_NOPE


def _ab_proj_kernel(xc_ref, xd_ref, mods_ref, w_ref, qn_ref, kvn_ref, wuq_ref, wukv_ref, ta_ref, tm_ref,
                    qa_ref, ka_ref, va_ref, ckv_ref, kr_ref, qm_ref, kvl_ref, *, qa_shift, qm_shift):
    i = pl.program_id(0)
    r = _mod_row(i, TOK_TILE)
    mrow = mods_ref[pl.ds(r, 1), :]
    sh, sc = mrow[:, 0:D], mrow[:, D:2 * D]
    x = _pick(i, TOK_TILE, xc_ref, xd_ref)
    h = (x * (1.0 + sc) + sh).astype(bf16)
    proj = _dot(h, w_ref[...])
    for j in range(4):
        c0 = C_QA + LANE * j
        qa_ref[:, LANE * j:LANE * (j + 1)] = _rope_chunk(proj[:, c0:c0 + LANE], ta_ref, qa_shift).astype(bf16)
    ka_ref[...] = _rope_chunk(proj[:, C_KA:C_KA + LANE], ta_ref, qa_shift)
    va_ref[...] = proj[:, C_VA:C_VA + LANE]
    cq = proj[:, C_CQ:C_CQ + MLA_Q_RANK]
    cq = cq * lax.rsqrt(jnp.mean(cq * cq, axis=-1, keepdims=True) + RMS_EPS) * qn_ref[...]
    ckv = proj[:, C_CKV:C_CKV + MLA_KV_RANK]
    ckv = ckv * lax.rsqrt(jnp.mean(ckv * ckv, axis=-1, keepdims=True) + RMS_EPS) * kvn_ref[...]
    ckv_ref[...] = ckv
    kr_ref[...] = _rope_chunk(proj[:, C_KR:C_KR + LANE], tm_ref, qm_shift)
    qm = _dot(cq.astype(bf16), wuq_ref[...])
    qm_ref[:, 0:MLA_NN] = qm[:, 0:MLA_NN].astype(bf16)
    for j in range(2):
        c0 = MLA_NN + LANE * j
        qm_ref[:, c0:c0 + LANE] = _rope_chunk(qm[:, c0:c0 + LANE], tm_ref, qm_shift).astype(bf16)
    kvl_ref[...] = _dot(ckv.astype(bf16), wukv_ref[...]).astype(bf16)


def _rope_block_index(i):
    tiles_ctx = T_CTX // TOK_TILE
    per_seq = DEN_L // TOK_TILE
    return jnp.where(i < tiles_ctx, 0, 1 + (i - tiles_ctx) % per_seq)


def _ab_proj(xc, xd, mods0, w_in_p, q_norm, kv_norm, w_uq_p, w_ukv_p):
    tab_a, qa_shift = _rope_table_array(A_HD)
    tab_m, qm_shift = _rope_table_array(MLA_ROPE)
    row_spec = lambda w: pl.BlockSpec((TOK_TILE, w), lambda i: (i, 0))
    xc_spec, xd_spec = _two_stream_specs(TOK_TILE, D)
    tab_spec = pl.BlockSpec((3, TOK_TILE, LANE), lambda i: (0, _rope_block_index(i), 0))
    outs = [(512, bf16), (LANE, f32), (LANE, f32), (LANE, f32), (LANE, f32), (768, bf16), (1024, bf16)]
    return pl.pallas_call(
        functools.partial(_ab_proj_kernel, qa_shift=qa_shift, qm_shift=qm_shift),
        grid=(T_ALL // TOK_TILE,),
        in_specs=[xc_spec, xd_spec, _full((8, 6 * D), 1), _full((D, PROJ_W), 1), _full((1, MLA_Q_RANK), 1),
                  _full((1, MLA_KV_RANK), 1), _full((MLA_Q_RANK, 768), 1), _full((MLA_KV_RANK, 1024), 1),
                  tab_spec, tab_spec],
        out_specs=[row_spec(w) for w, _ in outs],
        out_shape=[jax.ShapeDtypeStruct((T_ALL, w), dt) for w, dt in outs],
        compiler_params=_params(1),
        name="ab_proj",
    )(xc, xd, mods0, w_in_p, q_norm, kv_norm, w_uq_p, w_ukv_p, tab_a, tab_m)


def _softmax_blocks(s_refs, p_refs, sink_col=None):
    m = s_refs[0][...].max(axis=-1, keepdims=True)
    for s_ref in s_refs[1:]:
        m = jnp.maximum(m, s_ref[...].max(axis=-1, keepdims=True))
    if sink_col is not None:
        m = jnp.maximum(m, sink_col)
    l = None
    for s_ref, p_ref in zip(s_refs, p_refs):
        p = jnp.exp(s_ref[...] - m)
        p_ref[...] = p.astype(bf16)
        ps = p.sum(axis=-1, keepdims=True)
        l = ps if l is None else l + ps
    if sink_col is not None:
        l = l + jnp.exp(sink_col - m)
    return 1.0 / l


def _sink_column(sink_ref, rows_per_head):
    return jnp.concatenate([jnp.full((rows_per_head, 1), sink_ref[h], f32) for h in range(A_HEADS)], axis=0)


def _mla_q(qm_ref, h):
    rows = qm_ref.shape[0]
    return jnp.concatenate([qm_ref[:, MLA_NOPE * h:MLA_NOPE * (h + 1)],
                            qm_ref[:, MLA_NN + MLA_ROPE * h:MLA_NN + MLA_ROPE * (h + 1)],
                            jnp.zeros((rows, LANE - MLA_NOPE - MLA_ROPE), bf16)], axis=1)


def _mla_k(k_nope_h, k_rope):
    rows = k_nope_h.shape[0]
    return jnp.concatenate([k_nope_h, k_rope, jnp.zeros((rows, LANE - MLA_NOPE - MLA_ROPE), bf16)], axis=1)


def _mix_out_ln(merged_ref, wout_ref, x, mods_ref, r, g_ref, b_ref):
    out = _dot(merged_ref[...], wout_ref[...])
    gate = mods_ref[pl.ds(r, 1), 2 * D:3 * D]
    return _layer_norm(ALPHA * x + gate * out, g_ref[...], b_ref[...])


def _ctx_attn_kernel(sink_ref, qa_ref, ka_ref, va_ref, qm_ref, kvl_ref, kr_ref, x_ref, mods_ref, wout_ref,
                     g_ref, b_ref, o_ref, merged_ref, sa_ref, sm_ref, pa_ref, pm_ref):
    n = CTX_L
    ka = ka_ref[...].astype(bf16)
    va = va_ref[...].astype(bf16)
    for j in range(A_KV_HEADS):
        q4 = jnp.concatenate([qa_ref[:, A_HD * h:A_HD * (h + 1)] for h in range(A_GROUP * j, A_GROUP * (j + 1))],
                             axis=0)
        sa_ref[A_GROUP * n * j:A_GROUP * n * (j + 1), :] = _dot_nt(q4, ka[:, A_HD * j:A_HD * (j + 1)]) * A_SCALE
    kr = kr_ref[:, 0:MLA_ROPE].astype(bf16)
    for h in range(MLA_HEADS):
        k_cat = _mla_k(kvl_ref[:, MLA_NOPE * h:MLA_NOPE * (h + 1)], kr)
        sm_ref[n * h:n * (h + 1), :] = _dot_nt(_mla_q(qm_ref, h), k_cat) * MLA_SCALE
    rla = _softmax_blocks([sa_ref], [pa_ref], _sink_column(sink_ref, n))
    rlm = _softmax_blocks([sm_ref], [pm_ref])
    for j in range(A_KV_HEADS):
        rows = slice(A_GROUP * n * j, A_GROUP * n * (j + 1))
        o4 = _dot(pa_ref[rows, :], va[:, A_HD * j:A_HD * (j + 1)]) * rla[rows]
        for g in range(A_GROUP):
            h = A_GROUP * j + g
            merged_ref[:, A_HD * h:A_HD * (h + 1)] = o4[n * g:n * (g + 1)].astype(bf16)
    for h in range(MLA_HEADS):
        rows = slice(n * h, n * (h + 1))
        v = kvl_ref[:, MLA_NN + MLA_V * h:MLA_NN + MLA_V * (h + 1)]
        merged_ref[:, MLA_NN + MLA_V * h:MLA_NN + MLA_V * (h + 1)] = (_dot(pm_ref[rows, :], v) * rlm[rows]).astype(bf16)
    o_ref[...] = _mix_out_ln(merged_ref, wout_ref, x_ref[...], mods_ref, 0, g_ref, b_ref)


def _ctx_attn(sink, qa, ka, va, qm, kvl, kr, x_all, mods0, w_out, ln_g, ln_b):
    blk = lambda w: pl.BlockSpec((CTX_L, w), lambda b: (b, 0))
    return pl.pallas_call(
        _ctx_attn_kernel,
        grid=(N_CTX_B,),
        in_specs=[pl.BlockSpec(memory_space=pltpu.SMEM), blk(512), blk(LANE), blk(LANE), blk(768), blk(1024),
                  blk(LANE), blk(D), _full((8, 6 * D), 1), _full((D, D), 1), _full((1, D), 1), _full((1, D), 1)],
        out_specs=blk(D),
        out_shape=jax.ShapeDtypeStruct((T_CTX, D), f32),
        scratch_shapes=[pltpu.VMEM((CTX_L, D), bf16),
                        pltpu.VMEM((A_HEADS * CTX_L, CTX_L), f32), pltpu.VMEM((MLA_HEADS * CTX_L, CTX_L), f32),
                        pltpu.VMEM((A_HEADS * CTX_L, CTX_L), bf16), pltpu.VMEM((MLA_HEADS * CTX_L, CTX_L), bf16)],
        compiler_params=_params(1),
        name="ctx_attn",
    )(sink, qa, ka, va, qm, kvl, kr, x_all, mods0, w_out, ln_g, ln_b)


QB = 256
WIN = QB + 2 * WINDOW
DEN_BLK0 = T_CTX // DEN_L
MLA_KEYS = CTX_L + DEN_L


def _den_attn_kernel(sink_ref, qa_ref, ka_ref, va_ref, cak_ref, cav_ref, qm_ref, kvl_ref, kr_ref, cckv_ref, ckr_ref,
                     wukv_ref, x_ref, mods_ref, wout_ref, g_ref, b_ref, o_ref, merged_ref, kcat_ref, vcat_ref,
                     saw_ref, sac_ref, sm_ref, paw_ref, pac_ref, pm_ref):
    b = pl.program_id(0)
    n = pl.program_id(1)

    @pl.when(n == 0)
    def _():
        kvc = _dot(cckv_ref[0].astype(bf16), wukv_ref[...]).astype(bf16)
        kr_ctx = ckr_ref[0].astype(bf16)
        kr_lat = kr_ref[:, 0:MLA_ROPE].astype(bf16)
        for h in range(MLA_HEADS):
            ns = slice(MLA_NOPE * h, MLA_NOPE * (h + 1))
            kcat_ref[h, 0:CTX_L, :] = _mla_k(kvc[:, ns], kr_ctx)
            kcat_ref[h, CTX_L:MLA_KEYS, :] = _mla_k(kvl_ref[:, ns], kr_lat)
        vcat_ref[0:CTX_L, :] = kvc[:, MLA_NN:2 * MLA_NN]
        vcat_ref[CTX_L:MLA_KEYS, :] = kvl_ref[:, MLA_NN:2 * MLA_NN]

    start = pl.multiple_of(jnp.clip(QB * n - WINDOW, 0, DEN_L - WIN), WINDOW)
    grp_rows = A_GROUP * QB
    qpos = QB * n + (lax.broadcasted_iota(jnp.int32, (grp_rows, WIN), 0) & (QB - 1))
    kpos = start + lax.broadcasted_iota(jnp.int32, (grp_rows, WIN), 1)
    valid = jnp.abs(qpos - kpos) <= WINDOW
    kwin = ka_ref[pl.ds(start, WIN), :].astype(bf16)
    vwin = va_ref[pl.ds(start, WIN), :].astype(bf16)
    kctx = cak_ref[0].astype(bf16)
    vctx = cav_ref[0].astype(bf16)
    for j in range(A_KV_HEADS):
        sl = slice(A_HD * j, A_HD * (j + 1))
        rows = slice(grp_rows * j, grp_rows * (j + 1))
        q4 = jnp.concatenate([qa_ref[:, A_HD * h:A_HD * (h + 1)] for h in range(A_GROUP * j, A_GROUP * (j + 1))],
                             axis=0)
        saw_ref[rows, :] = jnp.where(valid, _dot_nt(q4, kwin[:, sl]) * A_SCALE, NEG_INF)
        sac_ref[rows, :] = _dot_nt(q4, kctx[:, sl]) * A_SCALE
    for h in range(MLA_HEADS):
        sm_ref[QB * h:QB * (h + 1), :] = _dot_nt(_mla_q(qm_ref, h), kcat_ref[h]) * MLA_SCALE
    rla = _softmax_blocks([saw_ref, sac_ref], [paw_ref, pac_ref], _sink_column(sink_ref, QB))
    rlm = _softmax_blocks([sm_ref], [pm_ref])
    for j in range(A_KV_HEADS):
        sl = slice(A_HD * j, A_HD * (j + 1))
        rows = slice(grp_rows * j, grp_rows * (j + 1))
        o4 = (_dot(paw_ref[rows, :], vwin[:, sl]) + _dot(pac_ref[rows, :], vctx[:, sl])) * rla[rows]
        for g in range(A_GROUP):
            h = A_GROUP * j + g
            merged_ref[:, A_HD * h:A_HD * (h + 1)] = o4[QB * g:QB * (g + 1)].astype(bf16)
    for h in range(MLA_HEADS):
        rows = slice(QB * h, QB * (h + 1))
        o = _dot(pm_ref[rows, :], vcat_ref[:, MLA_V * h:MLA_V * (h + 1)]) * rlm[rows]
        merged_ref[:, MLA_NN + MLA_V * h:MLA_NN + MLA_V * (h + 1)] = o.astype(bf16)
    o_ref[...] = _mix_out_ln(merged_ref, wout_ref, x_ref[...], mods_ref, 1 + b, g_ref, b_ref)


def _den_attn(sink, qa, ka, va, cache_k, cache_v, qm, kvl, kr, cache_ckv, cache_kr, w_ukv_p, x_all, mods0, w_out,
              ln_g, ln_b):
    nq = DEN_L // QB
    qblk = lambda w: pl.BlockSpec((QB, w), lambda b, n: (T_CTX // QB + b * nq + n, 0))
    seq = lambda w: pl.BlockSpec((DEN_L, w), lambda b, n: (DEN_BLK0 + b, 0))
    cache = lambda w: pl.BlockSpec((1, CTX_L, w), lambda b, n: (b, 0, 0))
    return pl.pallas_call(
        _den_attn_kernel,
        grid=(N_DEN_B, nq),
        in_specs=[pl.BlockSpec(memory_space=pltpu.SMEM), qblk(512), seq(LANE), seq(LANE), cache(LANE), cache(LANE),
                  qblk(768), seq(1024), seq(LANE), cache(MLA_KV_RANK), cache(MLA_ROPE),
                  _full((MLA_KV_RANK, 1024), 2), pl.BlockSpec((QB, D), lambda b, n: (b * nq + n, 0)),
                  _full((8, 6 * D), 2), _full((D, D), 2), _full((1, D), 2),
                  _full((1, D), 2)],
        out_specs=pl.BlockSpec((QB, D), lambda b, n: (b * nq + n, 0)),
        out_shape=jax.ShapeDtypeStruct((T_DEN, D), f32),
        scratch_shapes=[pltpu.VMEM((QB, D), bf16), pltpu.VMEM((MLA_HEADS, MLA_KEYS, LANE), bf16),
                        pltpu.VMEM((MLA_KEYS, MLA_NN), bf16)]
        + [pltpu.VMEM((A_HEADS * QB, w), dt) for dt in (f32, bf16) for w in (WIN, CTX_L, MLA_KEYS)],
        compiler_params=_params(2),
        name="den_attn",
    )(sink, qa, ka, va, cache_k, cache_v, qm, kvl, kr, cache_ckv, cache_kr, w_ukv_p, x_all, mods0, w_out, ln_g, ln_b)


def _route(x1, mrow, rw_ref, rb_ref, h_ref, gates_ref):
    sh, sc = mrow[:, 3 * D:4 * D], mrow[:, 4 * D:5 * D]
    h = x1 * (1.0 + sc) + sh
    h_hi = h.astype(bf16)
    h_ref[...] = h_hi
    h_lo = (h - h_hi.astype(f32)).astype(bf16)
    logits = _dot(h_hi, rw_ref[0]) + (_dot(h_hi, rw_ref[1]) + _dot(h_lo, rw_ref[0]))
    scores = jax.nn.sigmoid(logits)
    lane = lax.broadcasted_iota(jnp.int32, scores.shape, 1).astype(f32)
    sel = jnp.where(lane < N_EXPERTS, scores + rb_ref[...], -jnp.inf)
    gates = jnp.zeros_like(scores)
    for _ in range(TOP_K):
        m = sel.max(axis=-1, keepdims=True)
        idx = jnp.where(sel == m, lane, float(LANE)).min(axis=-1, keepdims=True)
        hit = lane == idx
        gates = jnp.where(hit, scores, gates)
        sel = jnp.where(hit, -jnp.inf, sel)
    gates_ref[...] = gates / gates.sum(axis=-1, keepdims=True) * ROUTED_SCALE


def _router_kernel(xc_ref, xd_ref, mods_ref, rw_ref, rb_ref, x_ref, h_ref, gates_ref):
    i = pl.program_id(0)
    r = _mod_row(i, TOK_TILE)
    x1 = _pick(i, TOK_TILE, xc_ref, xd_ref)
    x_ref[...] = x1
    _route(x1, mods_ref[pl.ds(r, 1), :], rw_ref, rb_ref, h_ref, gates_ref)


def _router(x1c, x1d, mods_l, router_w_p, router_b_p):
    row_spec = lambda w: pl.BlockSpec((TOK_TILE, w), lambda i: (i, 0))
    xc_spec, xd_spec = _two_stream_specs(TOK_TILE, D)
    return pl.pallas_call(
        _router_kernel,
        grid=(T_ALL // TOK_TILE,),
        in_specs=[xc_spec, xd_spec, _full((8, 6 * D), 1), _full((2, D, LANE), 1), _full((1, LANE), 1)],
        out_specs=[row_spec(D), row_spec(D), row_spec(LANE)],
        out_shape=[jax.ShapeDtypeStruct((T_ALL, D), f32), jax.ShapeDtypeStruct((T_ALL, D), bf16),
                   jax.ShapeDtypeStruct((T_ALL, LANE), f32)],
        compiler_params=_params(1),
        name="router",
    )(x1c, x1d, mods_l, router_w_p, router_b_p)


MOE_TOK = 1536
MOE_EG = 4
MOE_TILE = 512
MOE_FF = MOE_EG * EXPERT_FF


def _moe_kernel(h_ref, gates_ref, x_ref, mods_ref, wg_ref, wu_ref, wd_ref, sg_ref, su_ref, sd_ref, g_ref, b_ref,
                o_ref):
    p = pl.program_id(0)
    e = pl.program_id(1)
    n_tiles = MOE_TOK // MOE_TILE

    def gate_f(t):
        r = _mod_row(p * n_tiles + t, MOE_TILE)
        return mods_ref[pl.ds(r, 1), 5 * D:6 * D]

    def rows_of(t):
        if isinstance(t, int):
            return pl.ds(t * MOE_TILE, MOE_TILE)
        return pl.ds(pl.multiple_of(t * MOE_TILE, MOE_TILE), MOE_TILE)

    @pl.when(e == 0)
    def _():
        sg = sg_ref[...].astype(bf16)
        su = su_ref[...].astype(bf16)
        sd = sd_ref[...].astype(bf16)

        def body(t, c):
            rows = rows_of(t)
            ht = h_ref[rows, :]
            hid = _silu(_dot(ht, sg)) * _dot(ht, su)
            o_ref[rows, :] = ALPHA * x_ref[rows, :] + gate_f(t) * _dot(hid.astype(bf16), sd)
            return c

        lax.fori_loop(0, n_tiles, body, 0)

    wg = jnp.concatenate([wg_ref[k].astype(bf16) for k in range(MOE_EG)], axis=1)
    wu = jnp.concatenate([wu_ref[k].astype(bf16) for k in range(MOE_EG)], axis=1)
    wd = jnp.concatenate([wd_ref[k].astype(bf16) for k in range(MOE_EG)], axis=0)
    lane = lax.broadcasted_iota(jnp.int32, (MOE_TILE, LANE), 1)

    def body(t, c):
        rows = rows_of(t)
        ht = h_ref[rows, :]
        hid = _silu(_dot(ht, wg)) * _dot(ht, wu)
        gt = gates_ref[rows, :]
        parts = []
        for k in range(MOE_EG):
            col = jnp.where(lane == e * MOE_EG + k, gt, 0.0).sum(axis=-1, keepdims=True)
            parts.append((hid[:, EXPERT_FF * k:EXPERT_FF * (k + 1)] * col).astype(bf16))
        o_ref[rows, :] += gate_f(t) * _dot(jnp.concatenate(parts, axis=1), wd)
        return c

    for t in range(n_tiles):
        body(t, 0)

    @pl.when(e == pl.num_programs(1) - 1)
    def _():
        def body(t, c):
            rows = rows_of(t)
            o_ref[rows, :] = _layer_norm(o_ref[rows, :], g_ref[...], b_ref[...])
            return c

        lax.fori_loop(0, n_tiles, body, 0)


def _moe(l, h, gates, x1, mods_l, wg, wu, wd, sg, su, sd, ln_g, ln_b):
    tok = lambda w: pl.BlockSpec((MOE_TOK, w), lambda p, e: (p, 0))
    return pl.pallas_call(
        _moe_kernel,
        grid=(T_ALL // MOE_TOK, N_EXPERTS // MOE_EG),
        in_specs=[tok(D), tok(LANE), tok(D), _full((8, 6 * D), 2),
                  pl.BlockSpec((None, MOE_EG, D, EXPERT_FF), lambda p, e: (l, e, 0, 0)),
                  pl.BlockSpec((None, MOE_EG, D, EXPERT_FF), lambda p, e: (l, e, 0, 0)),
                  pl.BlockSpec((None, MOE_EG, EXPERT_FF, D), lambda p, e: (l, e, 0, 0)),
                  pl.BlockSpec((None, D, SHARED_FF), lambda p, e: (l, 0, 0)),
                  pl.BlockSpec((None, D, SHARED_FF), lambda p, e: (l, 0, 0)),
                  pl.BlockSpec((None, SHARED_FF, D), lambda p, e: (l, 0, 0)),
                  _full((1, D), 2), _full((1, D), 2)],
        out_specs=tok(D),
        out_shape=jax.ShapeDtypeStruct((T_ALL, D), f32),
        compiler_params=_params(2),
        name="moe",
    )(h, gates, x1, mods_l, wg, wu, wd, sg, su, sd, ln_g, ln_b)


def _router_weights(l, router_w, router_bias):
    rw = jnp.pad(router_w[l], ((0, 0), (0, LANE - N_EXPERTS)))
    rw_hi, rw_lo = _split_bf16(rw)
    rb = jnp.pad(router_bias[l], (0, LANE - N_EXPERTS)).reshape(1, LANE)
    return jnp.stack([rw_hi, rw_lo]), rb


S5_Q = 8
S5_NGB = D // LANE
S5_TILE = 256
S5_KT = S5_TILE // S5_Q


def _s5_in_kernel(x_ref, mods_ref, w_ref, u_ref, u2_ref, slab_ref, *, row_of):
    r = row_of(pl.program_id(0))
    mrow = mods_ref[pl.ds(r, 1), :]
    sh, sc = mrow[:, 0:D], mrow[:, D:2 * D]
    h = (x_ref[...] * (1.0 + sc) + sh).astype(bf16)
    u = _dot(h, w_ref[...])
    u_ref[...] = u
    for s in range(S5_NGB):
        slab_ref[s] = u[:, LANE * s:LANE * (s + 1)]
    for s in range(S5_NGB):
        for j in range(S5_Q):
            u2_ref[s, :, LANE * j:LANE * (j + 1)] = slab_ref[s, pl.ds(j, S5_KT, stride=S5_Q), :].astype(bf16)


def _s5_in(x_all, tile0, mods1, w_in_c, n_b, seq_len, row_of):
    n_tiles = n_b * seq_len // S5_TILE
    return pl.pallas_call(
        functools.partial(_s5_in_kernel, row_of=row_of),
        grid=(n_tiles,),
        in_specs=[pl.BlockSpec((S5_TILE, D), lambda i: (tile0 + i, 0)), _full((8, 6 * D), 1), _full((D, D), 1)],
        out_specs=[pl.BlockSpec((S5_TILE, D), lambda i: (i, 0)),
                   pl.BlockSpec((S5_NGB, S5_KT, D), lambda i: (0, i, 0))],
        out_shape=[jax.ShapeDtypeStruct((n_b * seq_len, D), f32),
                   jax.ShapeDtypeStruct((S5_NGB, n_tiles * S5_KT, D), bf16)],
        scratch_shapes=[pltpu.VMEM((S5_NGB, S5_TILE, LANE), f32)],
        compiler_params=_params(1),
        name="s5_in",
    )(x_all, mods1, w_in_c)


S5_GL = (LANE // S5_CH) * S5_P
S5_ROWS_C = (CTX_L // S5_Q) * N_CTX_B
S5_ROWS_D = (DEN_L // S5_Q) * N_DEN_B


def _s5_scan_kernel(lre_ref, lim_ref, ldt_ref, btr_ref, bti_ref, ctr_ref, cti_ref, uc_ref, ud_ref, h0_ref,
                    yc_ref, yd_ref, st_ref, win_ref, mso_ref, wit_ref, a_ref, s_ref, hp_ref):
    gl = S5_GL
    rowg = lax.shift_right_logical(lax.broadcasted_iota(jnp.int32, (LANE, gl), 0), 4)
    colg = lax.shift_right_logical(lax.broadcasted_iota(jnp.int32, (LANE, gl), 1), 6)
    same_group = rowg == colg
    reps = LANE // S5_CH

    def expand(t):
        return jnp.where(same_group, jnp.concatenate([t] * reps, axis=0), 0.0)

    for d in range(2):
        fwd = d == 0
        lre, lim = lre_ref[d], lim_ref[d]
        dt = jnp.exp(ldt_ref[d])
        a, w = lre * dt, lim * dt
        pre = [jnp.exp(m * a) * jnp.cos(m * w) for m in range(S5_Q + 1)]
        pim = [jnp.exp(m * a) * jnp.sin(m * w) for m in range(S5_Q + 1)]
        xr, xi = pre[1] - 1.0, pim[1]
        den = lre * lre + lim * lim
        cfr, cfi = (xr * lre + xi * lim) / den, (xi * lre - xr * lim) / den
        btr, bti = btr_ref[d], bti_ref[d]
        bexp_r = expand(cfr * btr - cfi * bti)
        bexp_i = expand(cfr * bti + cfi * btr)
        cexp_r, cexp_i = expand(ctr_ref[d]), expand(cti_ref[d])
        for m in range(S5_Q + 1):
            a_ref[m, :, 0:gl] = cexp_r * pre[m] - cexp_i * pim[m]
            a_ref[m, :, gl:2 * gl] = -(cexp_r * pim[m] + cexp_i * pre[m])
        for j in range(S5_Q):
            m = S5_Q - 1 - j if fwd else j
            win_ref[LANE * j:LANE * (j + 1), 0:gl] = (pre[m] * bexp_r - pim[m] * bexp_i).astype(bf16)
            win_ref[LANE * j:LANE * (j + 1), gl:2 * gl] = (pre[m] * bexp_i + pim[m] * bexp_r).astype(bf16)
        for j in range(S5_Q):
            m = j + 1 if fwd else S5_Q - j
            mso_ref[LANE * j:LANE * (j + 1), :] = a_ref[m].astype(bf16)
        b2_hi, b2_lo = _split_bf16(jnp.concatenate([bexp_r, bexp_i], axis=1))
        kt = []
        for tau in range(S5_Q):
            a_hi, a_lo = _split_bf16(a_ref[tau])
            kt.append((_dot_nt(b2_hi, a_hi) + _dot_nt(b2_hi, a_lo) + _dot_nt(b2_lo, a_hi)).astype(bf16))
        zero_blk = jnp.zeros((LANE, LANE), bf16)
        for j in range(S5_Q):
            for jp in range(S5_Q):
                tau = jp - j if fwd else j - jp
                wit_ref[LANE * j:LANE * (j + 1), LANE * jp:LANE * (jp + 1)] = kt[tau] if tau >= 0 else zero_blk

        l8r, l8i = pre[S5_Q], pim[S5_Q]

        nsl = gl // LANE

        def slabs(ref, rs, first):
            return jnp.concatenate([ref[first + sl, rs, :] for sl in range(nsl)], axis=1)

        def put_slabs(ref, rs, first, val):
            for sl in range(nsl):
                ref[first + sl, rs, :] = val[:, LANE * sl:LANE * (sl + 1)]

        def advance(hr, hi_, sr, si):
            return l8r * hr - l8i * hi_ + sr, l8r * hi_ + l8i * hr + si

        def run(u_ref, y_ref, n_b, n_k, h_init):
            rows = n_b * n_k
            s = _dot(u_ref[0], win_ref[...])
            if n_b % SUB == 0:
                pitch = n_k + 1
                for bb in range(n_b):
                    dst = slice(bb * pitch, bb * pitch + n_k)
                    put_slabs(s_ref, dst, 0, s[bb * n_k:(bb + 1) * n_k, 0:gl])
                    put_slabs(s_ref, dst, nsl, s[bb * n_k:(bb + 1) * n_k, gl:2 * gl])

                def step(i, carry):
                    hr, hi_ = carry
                    rs = pl.ds(i if fwd else n_k - 1 - i, n_b, stride=pitch)
                    put_slabs(hp_ref, rs, 0, hr)
                    put_slabs(hp_ref, rs, nsl, hi_)
                    return advance(hr, hi_, slabs(s_ref, rs, 0), slabs(s_ref, rs, nsl))

                h_fin = lax.fori_loop(0, n_k, step, h_init)
                hp = jnp.concatenate(
                    [jnp.concatenate([hp_ref[sl, bb * pitch:bb * pitch + n_k, :] for sl in range(2 * nsl)], axis=1)
                     for bb in range(n_b)], axis=0).astype(bf16)
            else:
                put_slabs(s_ref, slice(0, rows), 0, s[:, 0:gl])
                put_slabs(s_ref, slice(0, rows), nsl, s[:, gl:2 * gl])
                n_it = n_k // SUB

                def step(i, carry):
                    it = i if fwd else n_it - 1 - i
                    out = []
                    for bb in range(n_b):
                        hr, hi_ = carry[bb]
                        rs = pl.ds(pl.multiple_of(bb * n_k + it * SUB, SUB), SUB)
                        s_re, s_im = slabs(s_ref, rs, 0), slabs(s_ref, rs, nsl)
                        prev_r, prev_i = [None] * SUB, [None] * SUB
                        for sub in (range(SUB) if fwd else reversed(range(SUB))):
                            prev_r[sub], prev_i[sub] = hr, hi_
                            hr, hi_ = advance(hr, hi_, s_re[sub:sub + 1], s_im[sub:sub + 1])
                        put_slabs(hp_ref, rs, 0, jnp.concatenate(prev_r, axis=0))
                        put_slabs(hp_ref, rs, nsl, jnp.concatenate(prev_i, axis=0))
                        out.append((hr, hi_))
                    return tuple(out)

                fin = lax.fori_loop(0, n_it, step, tuple((h_init[0][bb:bb + 1], h_init[1][bb:bb + 1])
                                                          for bb in range(n_b)))
                h_fin = (jnp.concatenate([f[0] for f in fin], axis=0), jnp.concatenate([f[1] for f in fin], axis=0))
                hp = jnp.concatenate([hp_ref[sl, 0:rows, :] for sl in range(2 * nsl)], axis=1).astype(bf16)
            y = _dot_nt(hp, mso_ref[...]) + _dot(u_ref[0], wit_ref[...])
            if fwd:
                y_ref[0] = y
            else:
                y_ref[0] += y
            return h_fin

        zeros = jnp.zeros((N_CTX_B, gl), f32)
        hr, hi_ = run(uc_ref, yc_ref, N_CTX_B, CTX_L // S5_Q, (zeros, zeros))
        st_ref[d, 0] = hr
        st_ref[d, 1] = hi_
        run(ud_ref, yd_ref, N_DEN_B, DEN_L // S5_Q, (h0_ref[d, 0], h0_ref[d, 1]))


def _s5_scan(lam_re, lam_im, log_dt, bt_re, bt_im, ct_re, ct_im, u2c, u2d, h0):
    gl = S5_GL
    vec = pl.BlockSpec((2, 1, gl), lambda g: (0, 0, g))
    tab = pl.BlockSpec((2, S5_CH, gl), lambda g: (0, 0, g))
    rows = lambda n: pl.BlockSpec((1, n, D), lambda g: (g, 0, 0))
    return pl.pallas_call(
        _s5_scan_kernel,
        grid=(S5_NGB,),
        in_specs=[vec, vec, vec, tab, tab, tab, tab, rows(S5_ROWS_C), rows(S5_ROWS_D),
                  pl.BlockSpec((2, 2, N_DEN_B, gl), lambda g: (0, 0, 0, g))],
        out_specs=[rows(S5_ROWS_C), rows(S5_ROWS_D), pl.BlockSpec((2, 2, N_CTX_B, gl), lambda g: (0, 0, 0, g))],
        out_shape=[jax.ShapeDtypeStruct((S5_NGB, S5_ROWS_C, D), f32), jax.ShapeDtypeStruct((S5_NGB, S5_ROWS_D, D), f32),
                   jax.ShapeDtypeStruct((2, 2, N_CTX_B, S5_G * S5_P), f32)],
        scratch_shapes=[pltpu.VMEM((D, 2 * gl), bf16), pltpu.VMEM((D, 2 * gl), bf16), pltpu.VMEM((D, D), bf16),
                        pltpu.VMEM((S5_Q + 1, LANE, 2 * gl), f32),
                        pltpu.VMEM((2 * gl // LANE, S5_ROWS_C + 2 * N_CTX_B, LANE), f32),
                        pltpu.VMEM((2 * gl // LANE, S5_ROWS_C + 2 * N_CTX_B, LANE), f32)],
        compiler_params=_params(1),
        name="s5_scan",
    )(lam_re, lam_im, log_dt, bt_re, bt_im, ct_re, ct_im, u2c, u2d, h0)


S5_CTX_TILES = T_CTX // S5_TILE
S5_DEN_TPS = DEN_L // S5_TILE


def _gelu_tanh(x):
    return 0.5 * x * (1.0 + jnp.tanh(np.sqrt(2.0 / np.pi).astype(np.float32) * (x + 0.044715 * (x * x * x))))


def _s5_out_kernel(x_ref, uc_ref, ud_ref, yc_ref, yd_ref, mods_ref, dsk_ref, wout_ref, g_ref, b_ref, rw_ref, rb_ref,
                   x1_ref, h_ref, gates_ref, slab_ref):
    i = pl.program_id(0)
    is_ctx = i < S5_CTX_TILES
    r = _mod_row(i, S5_TILE)
    mrow = mods_ref[pl.ds(r, 1), :]
    u = _pick(i, S5_TILE, uc_ref, ud_ref)

    def unchunk(y_ref):
        for s in range(S5_NGB):
            for j in range(S5_Q):
                slab_ref[s, pl.ds(j, S5_KT, stride=S5_Q), :] = y_ref[s, :, LANE * j:LANE * (j + 1)]

    lax.cond(is_ctx, lambda: unchunk(yc_ref), lambda: unchunk(yd_ref))
    y = jnp.concatenate([slab_ref[s] for s in range(S5_NGB)], axis=1) + dsk_ref[...] * u
    z = _dot(_gelu_tanh(y).astype(bf16), wout_ref[...])
    out = z[:, 0:D] * jax.nn.sigmoid(z[:, D:2 * D])
    x1 = _layer_norm(ALPHA * x_ref[...] + mrow[:, 2 * D:3 * D] * out, g_ref[...], b_ref[...])
    x1_ref[...] = x1
    _route(x1, mrow, rw_ref, rb_ref, h_ref, gates_ref)


def _s5_out(x_all, uc, ud, yc, yd, mods1, d_skip, w_out_c, ln_g, ln_b, rw, rb):
    row_spec = lambda w: pl.BlockSpec((S5_TILE, w), lambda i: (i, 0))
    ci = lambda i: jnp.minimum(i, S5_CTX_TILES - 1)
    di = lambda i: jnp.maximum(i - S5_CTX_TILES, 0)
    return pl.pallas_call(
        _s5_out_kernel,
        grid=(T_ALL // S5_TILE,),
        in_specs=[row_spec(D),
                  pl.BlockSpec((S5_TILE, D), lambda i: (ci(i), 0)),
                  pl.BlockSpec((S5_TILE, D), lambda i: (di(i), 0)),
                  pl.BlockSpec((S5_NGB, S5_KT, D), lambda i: (0, ci(i), 0)),
                  pl.BlockSpec((S5_NGB, S5_KT, D), lambda i: (0, di(i), 0)),
                  _full((8, 6 * D), 1), _full((1, D), 1), _full((D, 2 * D), 1), _full((1, D), 1), _full((1, D), 1),
                  _full((2, D, LANE), 1), _full((1, LANE), 1)],
        out_specs=[row_spec(D), row_spec(D), row_spec(LANE)],
        out_shape=[jax.ShapeDtypeStruct((T_ALL, D), f32), jax.ShapeDtypeStruct((T_ALL, D), bf16),
                   jax.ShapeDtypeStruct((T_ALL, LANE), f32)],
        scratch_shapes=[pltpu.VMEM((S5_NGB, S5_TILE, LANE), f32)],
        compiler_params=_params(1),
        name="s5_out",
    )(x_all, uc, ud, yc, yd, mods1, d_skip, w_out_c, ln_g, ln_b, rw, rb)


def kernel(x_prompt, x_sample, c, cache_attn_k, cache_attn_v, cache_mla_ckv, cache_mla_krope, state_ssm, c_ctx,
           ada_w, ada_b, ln_mix_g, ln_mix_b, ln_ffn_g, ln_ffn_b, w_in_ab, attn_sink, mla_q_norm, mla_kv_norm,
           mla_w_uq, mla_w_ukv, w_out_ab, w_in_c, s5_lam_re, s5_lam_im, s5_log_dt, s5_b_re, s5_b_im, s5_c_re,
           s5_c_im, s5_d, w_out_c, router_w, router_bias, exp_w_gate, exp_w_up, exp_w_down, sh_w_gate, sh_w_up,
           sh_w_down):
    row = lambda v: v.reshape(1, -1)
    xc, xd = x_prompt.reshape(T_CTX, D), x_sample.reshape(T_DEN, D)
    cvec8 = jnp.concatenate([c_ctx[None, :], c, jnp.zeros((8 - 1 - N_DEN_B, D), f32)], axis=0)
    mods = _adaln(cvec8, ada_w, ada_b)

    w_in_p = jnp.pad(w_in_ab[0], ((0, 0), (0, PROJ_W - w_in_ab.shape[-1]))).astype(bf16)
    uq = mla_w_uq[0].reshape(MLA_Q_RANK, MLA_HEADS, MLA_NOPE + MLA_ROPE)
    w_uq_p = jnp.concatenate([uq[:, :, :MLA_NOPE].reshape(MLA_Q_RANK, -1), uq[:, :, MLA_NOPE:].reshape(MLA_Q_RANK, -1)],
                             axis=1).astype(bf16)
    ukv = mla_w_ukv[0].reshape(MLA_KV_RANK, MLA_HEADS, MLA_NOPE + MLA_V)
    w_ukv_p = jnp.concatenate([ukv[:, :, :MLA_NOPE].reshape(MLA_KV_RANK, -1),
                               ukv[:, :, MLA_NOPE:].reshape(MLA_KV_RANK, -1)], axis=1).astype(bf16)
    qa, ka, va, ckv, kr, qm, kvl = _ab_proj(xc, xd, mods[0], w_in_p, row(mla_q_norm[0]), row(mla_kv_norm[0]),
                                            w_uq_p, w_ukv_p)
    w_out_b = w_out_ab[0].astype(bf16)
    g0, b0 = row(ln_mix_g[0]), row(ln_mix_b[0])
    x1c = _ctx_attn(attn_sink[0], qa, ka, va, qm, kvl, kr, xc, mods[0], w_out_b, g0, b0)
    x1d = _den_attn(attn_sink[0], qa, ka, va,
                    cache_attn_k[:, 0].reshape(N_DEN_B, CTX_L, A_KV_HEADS * A_HD),
                    cache_attn_v[:, 0].reshape(N_DEN_B, CTX_L, A_KV_HEADS * A_HD),
                    qm, kvl, kr, cache_mla_ckv[:, 0], cache_mla_krope[:, 0], w_ukv_p, xd, mods[0], w_out_b, g0, b0)
    rw0, rb0 = _router_weights(0, router_w, router_bias)
    x1, h, gates = _router(x1c, x1d, mods[0], rw0, rb0)
    x2 = _moe(0, h, gates, x1, mods[0], exp_w_gate, exp_w_up, exp_w_down, sh_w_gate, sh_w_up, sh_w_down,
              row(ln_ffn_g[0]), row(ln_ffn_b[0]))

    w_in_c_b = w_in_c[0].astype(bf16)
    uc, u2c = _s5_in(x2, 0, mods[1], w_in_c_b, N_CTX_B, CTX_L, lambda i: 0)
    ud, u2d = _s5_in(x2, S5_CTX_TILES, mods[1], w_in_c_b, N_DEN_B, DEN_L, lambda i: 1 + i // S5_DEN_TPS)
    gp = S5_G * S5_P
    chan_major_b = lambda t: jnp.transpose(t[0], (0, 3, 1, 2)).reshape(2, S5_CH, gp)
    chan_major_c = lambda t: jnp.transpose(t[0], (0, 2, 1, 3)).reshape(2, S5_CH, gp)
    h0 = jnp.transpose(state_ssm[:, 0], (1, 2, 0, 3, 4)).reshape(2, 2, N_DEN_B, gp)
    yc, yd, st = _s5_scan(s5_lam_re[0].reshape(2, 1, gp), s5_lam_im[0].reshape(2, 1, gp),
                          jnp.repeat(s5_log_dt[0], S5_P, axis=-1).reshape(2, 1, gp),
                          chan_major_b(s5_b_re), chan_major_b(s5_b_im), chan_major_c(s5_c_re), chan_major_c(s5_c_im),
                          u2c, u2d, h0)
    rw1, rb1 = _router_weights(1, router_w, router_bias)
    x3, h, gates = _s5_out(x2, uc, ud, yc, yd, mods[1], row(s5_d[0]),
                           w_out_c[0].astype(bf16), row(ln_mix_g[1]), row(ln_mix_b[1]), rw1, rb1)
    x4 = _moe(1, h, gates, x3, mods[1], exp_w_gate, exp_w_up, exp_w_down, sh_w_gate, sh_w_up, sh_w_down,
              row(ln_ffn_g[1]), row(ln_ffn_b[1]))

    y_prompt = x4[:T_CTX].reshape(N_CTX_B, CTX_L, D)
    y_sample = x4[T_CTX:].reshape(N_DEN_B, DEN_L, D)
    new_attn_k = ka[:T_CTX].reshape(N_CTX_B, 1, CTX_L, A_KV_HEADS, A_HD)
    new_attn_v = va[:T_CTX].reshape(N_CTX_B, 1, CTX_L, A_KV_HEADS, A_HD)
    new_mla_ckv = ckv[:T_CTX].reshape(N_CTX_B, 1, CTX_L, MLA_KV_RANK)
    new_mla_krope = kr[:T_CTX, :MLA_ROPE].reshape(N_CTX_B, 1, CTX_L, MLA_ROPE)
    new_state_ssm = jnp.transpose(st, (2, 0, 1, 3)).reshape(N_CTX_B, 1, 2, 2, S5_G, S5_P)
    return (y_prompt, y_sample, new_attn_k, new_attn_v, new_mla_ckv, new_mla_krope, new_state_ssm)
```

```python
import functools

import jax
import jax.numpy as jnp
import numpy as np
from jax import lax
from jax.experimental import pallas as pl
from jax.experimental.pallas import tpu as pltpu

f32 = jnp.float32
bf16 = jnp.bfloat16

D = 1024
N_CTX_B, CTX_L = 16, 256
N_DEN_B, DEN_L = 2, 1024
T_CTX = N_CTX_B * CTX_L
T_DEN = N_DEN_B * DEN_L
T_ALL = T_CTX + T_DEN
GRID_W = 64
WINDOW = 128
ROPE_BASE = 10000.0
A_HEADS, A_KV_HEADS, A_HD = 8, 2, 64
A_GROUP = A_HEADS // A_KV_HEADS
A_SCALE = A_HD ** -0.5
MLA_HEADS, MLA_Q_RANK, MLA_KV_RANK = 8, 256, 128
MLA_NOPE, MLA_ROPE, MLA_V = 64, 32, 64
MLA_SCALE = (MLA_NOPE + MLA_ROPE) ** -0.5
N_EXPERTS, TOP_K, EXPERT_FF, SHARED_FF = 64, 6, 128, 128
ROUTED_SCALE = 2.5
DEPTH = 2
ALPHA = (2.0 * DEPTH) ** 0.25
LN_EPS = 1e-5
RMS_EPS = 1e-6
NEG_INF = -1e30
S5_G, S5_CH, S5_P = 64, 16, 64

LANE = 128
SUB = 8
VMEM_LIMIT = 56 * 1024 * 1024

TOK_TILE = 512


def _mod_row(tile_idx, tile_rows):
    start = tile_idx * tile_rows
    return jnp.where(start < T_CTX, 0, 1 + (start - T_CTX) // DEN_L)


def _layer_norm(y, g, b):
    mu = jnp.mean(y, axis=-1, keepdims=True)
    yc = y - mu
    var = jnp.mean(yc * yc, axis=-1, keepdims=True)
    return yc * lax.rsqrt(var + LN_EPS) * g + b


def _silu(x):
    return x * jax.nn.sigmoid(x)


def _dot(a, b):
    return jnp.dot(a, b, preferred_element_type=f32)


def _dot_nt(a, b):
    return lax.dot_general(a, b, (((1,), (1,)), ((), ())), preferred_element_type=f32)


def _split_bf16(a):
    hi = a.astype(bf16)
    return hi, (a - hi.astype(f32)).astype(bf16)


def _full(shape, n_grid):
    zeros = tuple(0 for _ in shape)
    return pl.BlockSpec(shape, lambda *_: zeros)


def _two_stream_specs(tile_rows, width):
    n_ctx = T_CTX // tile_rows
    return (pl.BlockSpec((tile_rows, width), lambda i: (jnp.minimum(i, n_ctx - 1), 0)),
            pl.BlockSpec((tile_rows, width), lambda i: (jnp.maximum(i - n_ctx, 0), 0)))


def _pick(i, tile_rows, ctx_ref, den_ref):
    return lax.cond(i < T_CTX // tile_rows, lambda: ctx_ref[...], lambda: den_ref[...])


def _params(n_grid):
    return pltpu.CompilerParams(dimension_semantics=("arbitrary",) * n_grid, vmem_limit_bytes=VMEM_LIMIT)


ADA_TN = 1536


def _adaln_kernel(c_ref, w_ref, b_ref, o_ref):
    s_hi, s_lo = _split_bf16(_silu(c_ref[...]))
    w_hi, w_lo = _split_bf16(w_ref[0])
    o_ref[0] = _dot(s_hi, w_hi) + (_dot(s_hi, w_lo) + _dot(s_lo, w_hi)) + b_ref[0]


def _adaln(cvec8, ada_w, ada_b):
    n = 6 * D
    return pl.pallas_call(
        _adaln_kernel,
        grid=(DEPTH, n // ADA_TN),
        in_specs=[
            pl.BlockSpec((8, D), lambda l, j: (0, 0)),
            pl.BlockSpec((1, D, ADA_TN), lambda l, j: (l, 0, j)),
            pl.BlockSpec((1, 1, ADA_TN), lambda l, j: (l, 0, j)),
        ],
        out_specs=pl.BlockSpec((1, 8, ADA_TN), lambda l, j: (l, 0, j)),
        out_shape=jax.ShapeDtypeStruct((DEPTH, 8, n), f32),
        compiler_params=_params(2),
        name="adaln",
    )(cvec8, ada_w, ada_b.reshape(DEPTH, 1, n))


def _rope_table_array(head_dim):
    q = head_dim // 4
    pos = np.arange(DEN_L)
    row, col = (pos // GRID_W).astype(np.float64), (pos % GRID_W).astype(np.float64)
    lane = np.arange(LANE) % head_dim
    is_col = lane >= head_dim // 2
    w = lane % (head_dim // 2)
    first = w < q
    inv_freq = ROPE_BASE ** (-np.arange(q, dtype=np.float64) / q)
    ang = np.where(is_col[None, :], col[:, None], row[:, None]) * inv_freq[w % q][None, :]
    cos, sin = np.cos(ang), np.sin(ang)
    sin_a = np.where(first[None, :], -sin, 0.0)
    sin_b = np.where(first[None, :], 0.0, sin)
    ident = np.stack([np.ones((TOK_TILE, LANE)), np.zeros((TOK_TILE, LANE)), np.zeros((TOK_TILE, LANE))])
    tab = np.concatenate([ident, np.stack([cos, sin_a, sin_b])], axis=1).astype(np.float32)
    return jnp.asarray(tab), q


def _rope_chunk(x, tab_ref, q):
    return x * tab_ref[0] + pltpu.roll(x, LANE - q, 1) * tab_ref[1] + pltpu.roll(x, q, 1) * tab_ref[2]


PROJ_W = 1280
C_QA, C_KA, C_VA, C_CQ, C_CKV, C_KR = 0, 512, 640, 768, 1024, 1152
MLA_NN = MLA_HEADS * MLA_NOPE


def _ab_proj_kernel(xc_ref, xd_ref, mods_ref, w_ref, qn_ref, kvn_ref, wuq_ref, wukv_ref, ta_ref, tm_ref,
                    qa_ref, ka_ref, va_ref, ckv_ref, kr_ref, qm_ref, kvl_ref, *, qa_shift, qm_shift):
    i = pl.program_id(0)
    r = _mod_row(i, TOK_TILE)
    mrow = mods_ref[pl.ds(r, 1), :]
    sh, sc = mrow[:, 0:D], mrow[:, D:2 * D]
    x = _pick(i, TOK_TILE, xc_ref, xd_ref)
    h = (x * (1.0 + sc) + sh).astype(bf16)
    proj = _dot(h, w_ref[...])
    for j in range(4):
        c0 = C_QA + LANE * j
        qa_ref[:, LANE * j:LANE * (j + 1)] = _rope_chunk(proj[:, c0:c0 + LANE], ta_ref, qa_shift).astype(bf16)
    ka_ref[...] = _rope_chunk(proj[:, C_KA:C_KA + LANE], ta_ref, qa_shift)
    va_ref[...] = proj[:, C_VA:C_VA + LANE]
    cq = proj[:, C_CQ:C_CQ + MLA_Q_RANK]
    cq = cq * lax.rsqrt(jnp.mean(cq * cq, axis=-1, keepdims=True) + RMS_EPS) * qn_ref[...]
    ckv = proj[:, C_CKV:C_CKV + MLA_KV_RANK]
    ckv = ckv * lax.rsqrt(jnp.mean(ckv * ckv, axis=-1, keepdims=True) + RMS_EPS) * kvn_ref[...]
    ckv_ref[...] = ckv
    kr_ref[...] = _rope_chunk(proj[:, C_KR:C_KR + LANE], tm_ref, qm_shift)
    qm = _dot(cq.astype(bf16), wuq_ref[...])
    qm_ref[:, 0:MLA_NN] = qm[:, 0:MLA_NN].astype(bf16)
    for j in range(2):
        c0 = MLA_NN + LANE * j
        qm_ref[:, c0:c0 + LANE] = _rope_chunk(qm[:, c0:c0 + LANE], tm_ref, qm_shift).astype(bf16)
    kvl_ref[...] = _dot(ckv.astype(bf16), wukv_ref[...]).astype(bf16)


def _rope_block_index(i):
    tiles_ctx = T_CTX // TOK_TILE
    per_seq = DEN_L // TOK_TILE
    return jnp.where(i < tiles_ctx, 0, 1 + (i - tiles_ctx) % per_seq)


def _ab_proj(xc, xd, mods0, w_in_p, q_norm, kv_norm, w_uq_p, w_ukv_p):
    tab_a, qa_shift = _rope_table_array(A_HD)
    tab_m, qm_shift = _rope_table_array(MLA_ROPE)
    row_spec = lambda w: pl.BlockSpec((TOK_TILE, w), lambda i: (i, 0))
    xc_spec, xd_spec = _two_stream_specs(TOK_TILE, D)
    tab_spec = pl.BlockSpec((3, TOK_TILE, LANE), lambda i: (0, _rope_block_index(i), 0))
    outs = [(512, bf16), (LANE, f32), (LANE, f32), (LANE, f32), (LANE, f32), (768, bf16), (1024, bf16)]
    return pl.pallas_call(
        functools.partial(_ab_proj_kernel, qa_shift=qa_shift, qm_shift=qm_shift),
        grid=(T_ALL // TOK_TILE,),
        in_specs=[xc_spec, xd_spec, _full((8, 6 * D), 1), _full((D, PROJ_W), 1), _full((1, MLA_Q_RANK), 1),
                  _full((1, MLA_KV_RANK), 1), _full((MLA_Q_RANK, 768), 1), _full((MLA_KV_RANK, 1024), 1),
                  tab_spec, tab_spec],
        out_specs=[row_spec(w) for w, _ in outs],
        out_shape=[jax.ShapeDtypeStruct((T_ALL, w), dt) for w, dt in outs],
        compiler_params=_params(1),
        name="ab_proj",
    )(xc, xd, mods0, w_in_p, q_norm, kv_norm, w_uq_p, w_ukv_p, tab_a, tab_m)


def _softmax_blocks(s_refs, p_refs, sink_col=None):
    m = s_refs[0][...].max(axis=-1, keepdims=True)
    for s_ref in s_refs[1:]:
        m = jnp.maximum(m, s_ref[...].max(axis=-1, keepdims=True))
    if sink_col is not None:
        m = jnp.maximum(m, sink_col)
    l = None
    for s_ref, p_ref in zip(s_refs, p_refs):
        p = jnp.exp(s_ref[...] - m)
        p_ref[...] = p.astype(bf16)
        ps = p.sum(axis=-1, keepdims=True)
        l = ps if l is None else l + ps
    if sink_col is not None:
        l = l + jnp.exp(sink_col - m)
    return 1.0 / l


def _sink_column(sink_ref, rows_per_head):
    return jnp.concatenate([jnp.full((rows_per_head, 1), sink_ref[h], f32) for h in range(A_HEADS)], axis=0)


def _mla_q(qm_ref, h):
    rows = qm_ref.shape[0]
    return jnp.concatenate([qm_ref[:, MLA_NOPE * h:MLA_NOPE * (h + 1)],
                            qm_ref[:, MLA_NN + MLA_ROPE * h:MLA_NN + MLA_ROPE * (h + 1)],
                            jnp.zeros((rows, LANE - MLA_NOPE - MLA_ROPE), bf16)], axis=1)


def _mla_k(k_nope_h, k_rope):
    rows = k_nope_h.shape[0]
    return jnp.concatenate([k_nope_h, k_rope, jnp.zeros((rows, LANE - MLA_NOPE - MLA_ROPE), bf16)], axis=1)


def _mix_out_ln(merged_ref, wout_ref, x, mods_ref, r, g_ref, b_ref):
    out = _dot(merged_ref[...], wout_ref[...])
    gate = mods_ref[pl.ds(r, 1), 2 * D:3 * D]
    return _layer_norm(ALPHA * x + gate * out, g_ref[...], b_ref[...])


def _ctx_attn_kernel(sink_ref, qa_ref, ka_ref, va_ref, qm_ref, kvl_ref, kr_ref, x_ref, mods_ref, wout_ref,
                     g_ref, b_ref, o_ref, merged_ref, sa_ref, sm_ref, pa_ref, pm_ref):
    n = CTX_L
    ka = ka_ref[...].astype(bf16)
    va = va_ref[...].astype(bf16)
    for j in range(A_KV_HEADS):
        q4 = jnp.concatenate([qa_ref[:, A_HD * h:A_HD * (h + 1)] for h in range(A_GROUP * j, A_GROUP * (j + 1))],
                             axis=0)
        sa_ref[A_GROUP * n * j:A_GROUP * n * (j + 1), :] = _dot_nt(q4, ka[:, A_HD * j:A_HD * (j + 1)]) * A_SCALE
    kr = kr_ref[:, 0:MLA_ROPE].astype(bf16)
    for h in range(MLA_HEADS):
        k_cat = _mla_k(kvl_ref[:, MLA_NOPE * h:MLA_NOPE * (h + 1)], kr)
        sm_ref[n * h:n * (h + 1), :] = _dot_nt(_mla_q(qm_ref, h), k_cat) * MLA_SCALE
    rla = _softmax_blocks([sa_ref], [pa_ref], _sink_column(sink_ref, n))
    rlm = _softmax_blocks([sm_ref], [pm_ref])
    for j in range(A_KV_HEADS):
        rows = slice(A_GROUP * n * j, A_GROUP * n * (j + 1))
        o4 = _dot(pa_ref[rows, :], va[:, A_HD * j:A_HD * (j + 1)]) * rla[rows]
        for g in range(A_GROUP):
            h = A_GROUP * j + g
            merged_ref[:, A_HD * h:A_HD * (h + 1)] = o4[n * g:n * (g + 1)].astype(bf16)
    for h in range(MLA_HEADS):
        rows = slice(n * h, n * (h + 1))
        v = kvl_ref[:, MLA_NN + MLA_V * h:MLA_NN + MLA_V * (h + 1)]
        merged_ref[:, MLA_NN + MLA_V * h:MLA_NN + MLA_V * (h + 1)] = (_dot(pm_ref[rows, :], v) * rlm[rows]).astype(bf16)
    o_ref[...] = _mix_out_ln(merged_ref, wout_ref, x_ref[...], mods_ref, 0, g_ref, b_ref)


def _ctx_attn(sink, qa, ka, va, qm, kvl, kr, x_all, mods0, w_out, ln_g, ln_b):
    blk = lambda w: pl.BlockSpec((CTX_L, w), lambda b: (b, 0))
    return pl.pallas_call(
        _ctx_attn_kernel,
        grid=(N_CTX_B,),
        in_specs=[pl.BlockSpec(memory_space=pltpu.SMEM), blk(512), blk(LANE), blk(LANE), blk(768), blk(1024),
                  blk(LANE), blk(D), _full((8, 6 * D), 1), _full((D, D), 1), _full((1, D), 1), _full((1, D), 1)],
        out_specs=blk(D),
        out_shape=jax.ShapeDtypeStruct((T_CTX, D), f32),
        scratch_shapes=[pltpu.VMEM((CTX_L, D), bf16),
                        pltpu.VMEM((A_HEADS * CTX_L, CTX_L), f32), pltpu.VMEM((MLA_HEADS * CTX_L, CTX_L), f32),
                        pltpu.VMEM((A_HEADS * CTX_L, CTX_L), bf16), pltpu.VMEM((MLA_HEADS * CTX_L, CTX_L), bf16)],
        compiler_params=_params(1),
        name="ctx_attn",
    )(sink, qa, ka, va, qm, kvl, kr, x_all, mods0, w_out, ln_g, ln_b)


QB = 256
WIN = QB + 2 * WINDOW
DEN_BLK0 = T_CTX // DEN_L
MLA_KEYS = CTX_L + DEN_L


def _den_attn_kernel(sink_ref, qa_ref, ka_ref, va_ref, cak_ref, cav_ref, qm_ref, kvl_ref, kr_ref, cckv_ref, ckr_ref,
                     wukv_ref, x_ref, mods_ref, wout_ref, g_ref, b_ref, o_ref, merged_ref, kcat_ref, vcat_ref,
                     saw_ref, sac_ref, sm_ref, paw_ref, pac_ref, pm_ref):
    b = pl.program_id(0)
    n = pl.program_id(1)

    @pl.when(n == 0)
    def _():
        kvc = _dot(cckv_ref[0].astype(bf16), wukv_ref[...]).astype(bf16)
        kr_ctx = ckr_ref[0].astype(bf16)
        kr_lat = kr_ref[:, 0:MLA_ROPE].astype(bf16)
        for h in range(MLA_HEADS):
            ns = slice(MLA_NOPE * h, MLA_NOPE * (h + 1))
            kcat_ref[h, 0:CTX_L, :] = _mla_k(kvc[:, ns], kr_ctx)
            kcat_ref[h, CTX_L:MLA_KEYS, :] = _mla_k(kvl_ref[:, ns], kr_lat)
        vcat_ref[0:CTX_L, :] = kvc[:, MLA_NN:2 * MLA_NN]
        vcat_ref[CTX_L:MLA_KEYS, :] = kvl_ref[:, MLA_NN:2 * MLA_NN]

    start = pl.multiple_of(jnp.clip(QB * n - WINDOW, 0, DEN_L - WIN), WINDOW)
    grp_rows = A_GROUP * QB
    qpos = QB * n + (lax.broadcasted_iota(jnp.int32, (grp_rows, WIN), 0) & (QB - 1))
    kpos = start + lax.broadcasted_iota(jnp.int32, (grp_rows, WIN), 1)
    valid = jnp.abs(qpos - kpos) <= WINDOW
    kwin = ka_ref[pl.ds(start, WIN), :].astype(bf16)
    vwin = va_ref[pl.ds(start, WIN), :].astype(bf16)
    kctx = cak_ref[0].astype(bf16)
    vctx = cav_ref[0].astype(bf16)
    for j in range(A_KV_HEADS):
        sl = slice(A_HD * j, A_HD * (j + 1))
        rows = slice(grp_rows * j, grp_rows * (j + 1))
        q4 = jnp.concatenate([qa_ref[:, A_HD * h:A_HD * (h + 1)] for h in range(A_GROUP * j, A_GROUP * (j + 1))],
                             axis=0)
        saw_ref[rows, :] = jnp.where(valid, _dot_nt(q4, kwin[:, sl]) * A_SCALE, NEG_INF)
        sac_ref[rows, :] = _dot_nt(q4, kctx[:, sl]) * A_SCALE
    for h in range(MLA_HEADS):
        sm_ref[QB * h:QB * (h + 1), :] = _dot_nt(_mla_q(qm_ref, h), kcat_ref[h]) * MLA_SCALE
    rla = _softmax_blocks([saw_ref, sac_ref], [paw_ref, pac_ref], _sink_column(sink_ref, QB))
    rlm = _softmax_blocks([sm_ref], [pm_ref])
    for j in range(A_KV_HEADS):
        sl = slice(A_HD * j, A_HD * (j + 1))
        rows = slice(grp_rows * j, grp_rows * (j + 1))
        o4 = (_dot(paw_ref[rows, :], vwin[:, sl]) + _dot(pac_ref[rows, :], vctx[:, sl])) * rla[rows]
        for g in range(A_GROUP):
            h = A_GROUP * j + g
            merged_ref[:, A_HD * h:A_HD * (h + 1)] = o4[QB * g:QB * (g + 1)].astype(bf16)
    for h in range(MLA_HEADS):
        rows = slice(QB * h, QB * (h + 1))
        o = _dot(pm_ref[rows, :], vcat_ref[:, MLA_V * h:MLA_V * (h + 1)]) * rlm[rows]
        merged_ref[:, MLA_NN + MLA_V * h:MLA_NN + MLA_V * (h + 1)] = o.astype(bf16)
    o_ref[...] = _mix_out_ln(merged_ref, wout_ref, x_ref[...], mods_ref, 1 + b, g_ref, b_ref)


def _den_attn(sink, qa, ka, va, cache_k, cache_v, qm, kvl, kr, cache_ckv, cache_kr, w_ukv_p, x_all, mods0, w_out,
              ln_g, ln_b):
    nq = DEN_L // QB
    qblk = lambda w: pl.BlockSpec((QB, w), lambda b, n: (T_CTX // QB + b * nq + n, 0))
    seq = lambda w: pl.BlockSpec((DEN_L, w), lambda b, n: (DEN_BLK0 + b, 0))
    cache = lambda w: pl.BlockSpec((1, CTX_L, w), lambda b, n: (b, 0, 0))
    return pl.pallas_call(
        _den_attn_kernel,
        grid=(N_DEN_B, nq),
        in_specs=[pl.BlockSpec(memory_space=pltpu.SMEM), qblk(512), seq(LANE), seq(LANE), cache(LANE), cache(LANE),
                  qblk(768), seq(1024), seq(LANE), cache(MLA_KV_RANK), cache(MLA_ROPE),
                  _full((MLA_KV_RANK, 1024), 2), pl.BlockSpec((QB, D), lambda b, n: (b * nq + n, 0)),
                  _full((8, 6 * D), 2), _full((D, D), 2), _full((1, D), 2),
                  _full((1, D), 2)],
        out_specs=pl.BlockSpec((QB, D), lambda b, n: (b * nq + n, 0)),
        out_shape=jax.ShapeDtypeStruct((T_DEN, D), f32),
        scratch_shapes=[pltpu.VMEM((QB, D), bf16), pltpu.VMEM((MLA_HEADS, MLA_KEYS, LANE), bf16),
                        pltpu.VMEM((MLA_KEYS, MLA_NN), bf16)]
        + [pltpu.VMEM((A_HEADS * QB, w), dt) for dt in (f32, bf16) for w in (WIN, CTX_L, MLA_KEYS)],
        compiler_params=_params(2),
        name="den_attn",
    )(sink, qa, ka, va, cache_k, cache_v, qm, kvl, kr, cache_ckv, cache_kr, w_ukv_p, x_all, mods0, w_out, ln_g, ln_b)


SUBTILE = 256


def _route(x1, mrow, rw_ref, rb_ref):
    sh, sc = mrow[:, 3 * D:4 * D], mrow[:, 4 * D:5 * D]
    h = x1 * (1.0 + sc) + sh
    h_hi = h.astype(bf16)
    h_lo = (h - h_hi.astype(f32)).astype(bf16)
    logits = _dot(h_hi, rw_ref[0]) + (_dot(h_hi, rw_ref[1]) + _dot(h_lo, rw_ref[0]))
    scores = jax.nn.sigmoid(logits)
    lane = lax.broadcasted_iota(jnp.int32, scores.shape, 1).astype(f32)
    sel = jnp.where(lane < N_EXPERTS, scores + rb_ref[...], -jnp.inf)
    gates = jnp.zeros_like(scores)
    for _ in range(TOP_K):
        m = sel.max(axis=-1, keepdims=True)
        idx = jnp.where(sel == m, lane, float(LANE)).min(axis=-1, keepdims=True)
        hit = lane == idx
        gates = jnp.where(hit, scores, gates)
        sel = jnp.where(hit, -jnp.inf, sel)
    return h_hi, gates / gates.sum(axis=-1, keepdims=True) * ROUTED_SCALE


def _router_kernel(xc_ref, xd_ref, mods_ref, rw_ref, rb_ref, x_ref, h_ref, gates_ref):
    i = pl.program_id(0)
    r = _mod_row(i, TOK_TILE)
    mrow = mods_ref[pl.ds(r, 1), :]
    x1 = _pick(i, TOK_TILE, xc_ref, xd_ref)
    x_ref[...] = x1
    h_ref[...], gates_ref[...] = _route(x1, mrow, rw_ref, rb_ref)


def _router(x1c, x1d, mods_l, router_w_p, router_b_p):
    row_spec = lambda w: pl.BlockSpec((TOK_TILE, w), lambda i: (i, 0))
    xc_spec, xd_spec = _two_stream_specs(TOK_TILE, D)
    return pl.pallas_call(
        _router_kernel,
        grid=(T_ALL // TOK_TILE,),
        in_specs=[xc_spec, xd_spec, _full((8, 6 * D), 1), _full((2, D, LANE), 1), _full((1, LANE), 1)],
        out_specs=[row_spec(D), row_spec(D), row_spec(LANE)],
        out_shape=[jax.ShapeDtypeStruct((T_ALL, D), f32), jax.ShapeDtypeStruct((T_ALL, D), bf16),
                   jax.ShapeDtypeStruct((T_ALL, LANE), f32)],
        compiler_params=_params(1),
        name="router",
    )(x1c, x1d, mods_l, router_w_p, router_b_p)


MOE_TOK = 1536
MOE_EG = 4
MOE_TILE = 512
MOE_FF = MOE_EG * EXPERT_FF


def _moe_kernel(h_ref, gates_ref, x_ref, mods_ref, wg_ref, wu_ref, wd_ref, sg_ref, su_ref, sd_ref, g_ref, b_ref,
                o_ref):
    p = pl.program_id(0)
    e = pl.program_id(1)
    n_tiles = MOE_TOK // MOE_TILE

    def gate_f(t):
        r = _mod_row(p * n_tiles + t, MOE_TILE)
        return mods_ref[pl.ds(r, 1), 5 * D:6 * D]

    def rows_of(t):
        if isinstance(t, int):
            return pl.ds(t * MOE_TILE, MOE_TILE)
        return pl.ds(pl.multiple_of(t * MOE_TILE, MOE_TILE), MOE_TILE)

    @pl.when(e == 0)
    def _():
        sg = sg_ref[...].astype(bf16)
        su = su_ref[...].astype(bf16)
        sd = sd_ref[...].astype(bf16)

        def body(t, c):
            rows = rows_of(t)
            ht = h_ref[rows, :]
            hid = _silu(_dot(ht, sg)) * _dot(ht, su)
            o_ref[rows, :] = ALPHA * x_ref[rows, :] + gate_f(t) * _dot(hid.astype(bf16), sd)
            return c

        lax.fori_loop(0, n_tiles, body, 0)

    wg = jnp.concatenate([wg_ref[k].astype(bf16) for k in range(MOE_EG)], axis=1)
    wu = jnp.concatenate([wu_ref[k].astype(bf16) for k in range(MOE_EG)], axis=1)
    wd = jnp.concatenate([wd_ref[k].astype(bf16) for k in range(MOE_EG)], axis=0)
    lane = lax.broadcasted_iota(jnp.int32, (MOE_TILE, LANE), 1)

    def body(t, c):
        rows = rows_of(t)
        ht = h_ref[rows, :]
        hid = _silu(_dot(ht, wg)) * _dot(ht, wu)
        gt = gates_ref[rows, :]
        parts = []
        for k in range(MOE_EG):
            col = jnp.where(lane == e * MOE_EG + k, gt, 0.0).sum(axis=-1, keepdims=True)
            parts.append((hid[:, EXPERT_FF * k:EXPERT_FF * (k + 1)] * col).astype(bf16))
        o_ref[rows, :] += gate_f(t) * _dot(jnp.concatenate(parts, axis=1), wd)
        return c

    for t in range(n_tiles):
        body(t, 0)

    @pl.when(e == pl.num_programs(1) - 1)
    def _():
        def body(t, c):
            rows = rows_of(t)
            o_ref[rows, :] = _layer_norm(o_ref[rows, :], g_ref[...], b_ref[...])
            return c

        lax.fori_loop(0, n_tiles, body, 0)


def _moe(l, h, gates, x1, mods_l, wg, wu, wd, sg, su, sd, ln_g, ln_b):
    tok = lambda w: pl.BlockSpec((MOE_TOK, w), lambda p, e: (p, 0))
    return pl.pallas_call(
        _moe_kernel,
        grid=(T_ALL // MOE_TOK, N_EXPERTS // MOE_EG),
        in_specs=[tok(D), tok(LANE), tok(D), _full((8, 6 * D), 2),
                  pl.BlockSpec((None, MOE_EG, D, EXPERT_FF), lambda p, e: (l, e, 0, 0)),
                  pl.BlockSpec((None, MOE_EG, D, EXPERT_FF), lambda p, e: (l, e, 0, 0)),
                  pl.BlockSpec((None, MOE_EG, EXPERT_FF, D), lambda p, e: (l, e, 0, 0)),
                  pl.BlockSpec((None, D, SHARED_FF), lambda p, e: (l, 0, 0)),
                  pl.BlockSpec((None, D, SHARED_FF), lambda p, e: (l, 0, 0)),
                  pl.BlockSpec((None, SHARED_FF, D), lambda p, e: (l, 0, 0)),
                  _full((1, D), 2), _full((1, D), 2)],
        out_specs=tok(D),
        out_shape=jax.ShapeDtypeStruct((T_ALL, D), f32),
        compiler_params=_params(2),
        name="moe",
    )(h, gates, x1, mods_l, wg, wu, wd, sg, su, sd, ln_g, ln_b)


def _router_weights(l, router_w, router_bias):
    rw = jnp.pad(router_w[l], ((0, 0), (0, LANE - N_EXPERTS)))
    rw_hi, rw_lo = _split_bf16(rw)
    rb = jnp.pad(router_bias[l], (0, LANE - N_EXPERTS)).reshape(1, LANE)
    return jnp.stack([rw_hi, rw_lo]), rb


S5_Q = 8
S5_NGB = D // LANE
S5_TILE = 256
S5_KT = S5_TILE // S5_Q


def _s5_in_kernel(x_ref, mods_ref, w_ref, u_ref, u2_ref, slab_ref, *, row_of):
    r = row_of(pl.program_id(0))
    mrow = mods_ref[pl.ds(r, 1), :]
    sh, sc = mrow[:, 0:D], mrow[:, D:2 * D]
    h = (x_ref[...] * (1.0 + sc) + sh).astype(bf16)
    u = _dot(h, w_ref[...])
    u_ref[...] = u
    for s in range(S5_NGB):
        slab_ref[s] = u[:, LANE * s:LANE * (s + 1)]
    for s in range(S5_NGB):
        for j in range(S5_Q):
            u2_ref[s, :, LANE * j:LANE * (j + 1)] = slab_ref[s, pl.ds(j, S5_KT, stride=S5_Q), :].astype(bf16)


def _s5_in(x_all, tile0, mods1, w_in_c, n_b, seq_len, row_of):
    n_tiles = n_b * seq_len // S5_TILE
    return pl.pallas_call(
        functools.partial(_s5_in_kernel, row_of=row_of),
        grid=(n_tiles,),
        in_specs=[pl.BlockSpec((S5_TILE, D), lambda i: (tile0 + i, 0)), _full((8, 6 * D), 1), _full((D, D), 1)],
        out_specs=[pl.BlockSpec((S5_TILE, D), lambda i: (i, 0)),
                   pl.BlockSpec((S5_NGB, S5_KT, D), lambda i: (0, i, 0))],
        out_shape=[jax.ShapeDtypeStruct((n_b * seq_len, D), f32),
                   jax.ShapeDtypeStruct((S5_NGB, n_tiles * S5_KT, D), bf16)],
        scratch_shapes=[pltpu.VMEM((S5_NGB, S5_TILE, LANE), f32)],
        compiler_params=_params(1),
        name="s5_in",
    )(x_all, mods1, w_in_c)


S5_GL = (LANE // S5_CH) * S5_P
S5_ROWS_C = (CTX_L // S5_Q) * N_CTX_B
S5_ROWS_D = (DEN_L // S5_Q) * N_DEN_B


def _s5_scan_kernel(lre_ref, lim_ref, ldt_ref, btr_ref, bti_ref, ctr_ref, cti_ref, uc_ref, ud_ref, h0_ref,
                    yc_ref, yd_ref, st_ref, win_ref, mso_ref, wit_ref, a_ref, s_ref, hp_ref):
    gl = S5_GL
    rowg = lax.shift_right_logical(lax.broadcasted_iota(jnp.int32, (LANE, gl), 0), 4)
    colg = lax.shift_right_logical(lax.broadcasted_iota(jnp.int32, (LANE, gl), 1), 6)
    same_group = rowg == colg
    reps = LANE // S5_CH

    def expand(t):
        return jnp.where(same_group, jnp.concatenate([t] * reps, axis=0), 0.0)

    for d in range(2):
        fwd = d == 0
        lre, lim = lre_ref[d], lim_ref[d]
        dt = jnp.exp(ldt_ref[d])
        a, w = lre * dt, lim * dt
        pre = [jnp.exp(m * a) * jnp.cos(m * w) for m in range(S5_Q + 1)]
        pim = [jnp.exp(m * a) * jnp.sin(m * w) for m in range(S5_Q + 1)]
        xr, xi = pre[1] - 1.0, pim[1]
        den = lre * lre + lim * lim
        cfr, cfi = (xr * lre + xi * lim) / den, (xi * lre - xr * lim) / den
        btr, bti = btr_ref[d], bti_ref[d]
        bexp_r = expand(cfr * btr - cfi * bti)
        bexp_i = expand(cfr * bti + cfi * btr)
        cexp_r, cexp_i = expand(ctr_ref[d]), expand(cti_ref[d])
        for m in range(S5_Q + 1):
            a_ref[m, :, 0:gl] = cexp_r * pre[m] - cexp_i * pim[m]
            a_ref[m, :, gl:2 * gl] = -(cexp_r * pim[m] + cexp_i * pre[m])
        for j in range(S5_Q):
            m = S5_Q - 1 - j if fwd else j
            win_ref[LANE * j:LANE * (j + 1), 0:gl] = (pre[m] * bexp_r - pim[m] * bexp_i).astype(bf16)
            win_ref[LANE * j:LANE * (j + 1), gl:2 * gl] = (pre[m] * bexp_i + pim[m] * bexp_r).astype(bf16)
        for j in range(S5_Q):
            m = j + 1 if fwd else S5_Q - j
            mso_ref[LANE * j:LANE * (j + 1), :] = a_ref[m].astype(bf16)
        b2_hi, b2_lo = _split_bf16(jnp.concatenate([bexp_r, bexp_i], axis=1))
        kt = []
        for tau in range(S5_Q):
            a_hi, a_lo = _split_bf16(a_ref[tau])
            kt.append(_dot_nt(b2_hi, a_hi) + _dot_nt(b2_hi, a_lo) + _dot_nt(b2_lo, a_hi))
        for j in range(S5_Q):
            for jp in range(S5_Q):
                tau = jp - j if fwd else j - jp
                blk = slice(LANE * j, LANE * (j + 1)), slice(LANE * jp, LANE * (jp + 1))
                if fwd:
                    wit_ref[blk] = kt[tau] if tau >= 0 else jnp.zeros((LANE, LANE), f32)
                elif tau >= 0:
                    wit_ref[blk] = wit_ref[blk] + kt[tau]

        l8r, l8i = pre[S5_Q], pim[S5_Q]

        nsl = gl // LANE

        def slabs(ref, rs, first):
            return jnp.concatenate([ref[first + sl, rs, :] for sl in range(nsl)], axis=1)

        def put_slabs(ref, rs, first, val):
            for sl in range(nsl):
                ref[first + sl, rs, :] = val[:, LANE * sl:LANE * (sl + 1)]

        def advance(hr, hi_, sr, si):
            return l8r * hr - l8i * hi_ + sr, l8r * hi_ + l8i * hr + si

        def run(u_ref, y_ref, n_b, n_k, h_init):
            rows = n_b * n_k
            s = _dot(u_ref[0], win_ref[...])
            if n_b % SUB == 0:
                pitch = n_k + 1
                for bb in range(n_b):
                    dst = slice(bb * pitch, bb * pitch + n_k)
                    put_slabs(s_ref, dst, 0, s[bb * n_k:(bb + 1) * n_k, 0:gl])
                    put_slabs(s_ref, dst, nsl, s[bb * n_k:(bb + 1) * n_k, gl:2 * gl])

                def step(i, carry):
                    hr, hi_ = carry
                    rs = pl.ds(i if fwd else n_k - 1 - i, n_b, stride=pitch)
                    put_slabs(hp_ref, rs, 0, hr)
                    put_slabs(hp_ref, rs, nsl, hi_)
                    return advance(hr, hi_, slabs(s_ref, rs, 0), slabs(s_ref, rs, nsl))

                h_fin = lax.fori_loop(0, n_k, step, h_init)
                hp = jnp.concatenate(
                    [jnp.concatenate([hp_ref[sl, bb * pitch:bb * pitch + n_k, :] for sl in range(2 * nsl)], axis=1)
                     for bb in range(n_b)], axis=0).astype(bf16)
            else:
                put_slabs(s_ref, slice(0, rows), 0, s[:, 0:gl])
                put_slabs(s_ref, slice(0, rows), nsl, s[:, gl:2 * gl])
                n_it = n_k // SUB

                def step(i, carry):
                    it = i if fwd else n_it - 1 - i
                    out = []
                    for bb in range(n_b):
                        hr, hi_ = carry[bb]
                        rs = pl.ds(pl.multiple_of(bb * n_k + it * SUB, SUB), SUB)
                        s_re, s_im = slabs(s_ref, rs, 0), slabs(s_ref, rs, nsl)
                        prev_r, prev_i = [None] * SUB, [None] * SUB
                        for sub in (range(SUB) if fwd else reversed(range(SUB))):
                            prev_r[sub], prev_i[sub] = hr, hi_
                            hr, hi_ = advance(hr, hi_, s_re[sub:sub + 1], s_im[sub:sub + 1])
                        put_slabs(hp_ref, rs, 0, jnp.concatenate(prev_r, axis=0))
                        put_slabs(hp_ref, rs, nsl, jnp.concatenate(prev_i, axis=0))
                        out.append((hr, hi_))
                    return tuple(out)

                fin = lax.fori_loop(0, n_it, step, tuple((h_init[0][bb:bb + 1], h_init[1][bb:bb + 1])
                                                          for bb in range(n_b)))
                h_fin = (jnp.concatenate([f[0] for f in fin], axis=0), jnp.concatenate([f[1] for f in fin], axis=0))
                hp = jnp.concatenate([hp_ref[sl, 0:rows, :] for sl in range(2 * nsl)], axis=1).astype(bf16)
            y = _dot_nt(hp, mso_ref[...])
            if fwd:
                y_ref[0] = y
            else:
                y_ref[0] += y
            return h_fin

        zeros = jnp.zeros((N_CTX_B, gl), f32)
        hr, hi_ = run(uc_ref, yc_ref, N_CTX_B, CTX_L // S5_Q, (zeros, zeros))
        st_ref[d, 0] = hr
        st_ref[d, 1] = hi_
        run(ud_ref, yd_ref, N_DEN_B, DEN_L // S5_Q, (h0_ref[d, 0], h0_ref[d, 1]))

    wit = wit_ref[...].astype(bf16)
    yc_ref[0] += _dot(uc_ref[0], wit)
    yd_ref[0] += _dot(ud_ref[0], wit)


def _s5_scan(lam_re, lam_im, log_dt, bt_re, bt_im, ct_re, ct_im, u2c, u2d, h0):
    gl = S5_GL
    vec = pl.BlockSpec((2, 1, gl), lambda g: (0, 0, g))
    tab = pl.BlockSpec((2, S5_CH, gl), lambda g: (0, 0, g))
    rows = lambda n: pl.BlockSpec((1, n, D), lambda g: (g, 0, 0))
    return pl.pallas_call(
        _s5_scan_kernel,
        grid=(S5_NGB,),
        in_specs=[vec, vec, vec, tab, tab, tab, tab, rows(S5_ROWS_C), rows(S5_ROWS_D),
                  pl.BlockSpec((2, 2, N_DEN_B, gl), lambda g: (0, 0, 0, g))],
        out_specs=[rows(S5_ROWS_C), rows(S5_ROWS_D), pl.BlockSpec((2, 2, N_CTX_B, gl), lambda g: (0, 0, 0, g))],
        out_shape=[jax.ShapeDtypeStruct((S5_NGB, S5_ROWS_C, D), f32), jax.ShapeDtypeStruct((S5_NGB, S5_ROWS_D, D), f32),
                   jax.ShapeDtypeStruct((2, 2, N_CTX_B, S5_G * S5_P), f32)],
        scratch_shapes=[pltpu.VMEM((D, 2 * gl), bf16), pltpu.VMEM((D, 2 * gl), bf16), pltpu.VMEM((D, D), f32),
                        pltpu.VMEM((S5_Q + 1, LANE, 2 * gl), f32),
                        pltpu.VMEM((2 * gl // LANE, S5_ROWS_C + 2 * N_CTX_B, LANE), f32),
                        pltpu.VMEM((2 * gl // LANE, S5_ROWS_C + 2 * N_CTX_B, LANE), f32)],
        compiler_params=_params(1),
        name="s5_scan",
    )(lam_re, lam_im, log_dt, bt_re, bt_im, ct_re, ct_im, u2c, u2d, h0)


S5_CTX_TILES = T_CTX // S5_TILE
S5_DEN_TPS = DEN_L // S5_TILE


def _gelu_tanh(x):
    return 0.5 * x * (1.0 + jnp.tanh(np.sqrt(2.0 / np.pi).astype(np.float32) * (x + 0.044715 * (x * x * x))))


def _s5_out_kernel(x_ref, uc_ref, ud_ref, yc_ref, yd_ref, mods_ref, dsk_ref, wout_ref, g_ref, b_ref, rw_ref, rb_ref,
                   x1_ref, h_ref, gates_ref, slab_ref):
    i = pl.program_id(0)
    is_ctx = i < T_CTX // TOK_TILE
    r = _mod_row(i, TOK_TILE)
    mrow = mods_ref[pl.ds(r, 1), :]
    u = _pick(i, TOK_TILE, uc_ref, ud_ref)

    def unchunk(y_ref, seq_len):
        kt = seq_len // S5_Q
        for s in range(S5_NGB):
            for q in range(TOK_TILE // seq_len):
                for j in range(S5_Q):
                    slab_ref[s, pl.ds(q * seq_len + j, kt, stride=S5_Q), :] = (
                        y_ref[s, q * kt:(q + 1) * kt, LANE * j:LANE * (j + 1)])

    lax.cond(is_ctx, lambda: unchunk(yc_ref, min(CTX_L, TOK_TILE)), lambda: unchunk(yd_ref, min(DEN_L, TOK_TILE)))
    for rows in (slice(a, a + SUBTILE) for a in range(0, TOK_TILE, SUBTILE)):
        y = jnp.concatenate([slab_ref[s, rows, :] for s in range(S5_NGB)], axis=1) + dsk_ref[...] * u[rows]
        z = _dot(_gelu_tanh(y).astype(bf16), wout_ref[...])
        out = z[:, 0:D] * jax.nn.sigmoid(z[:, D:2 * D])
        x1 = _layer_norm(ALPHA * x_ref[rows, :] + mrow[:, 2 * D:3 * D] * out, g_ref[...], b_ref[...])
        x1_ref[rows, :] = x1
        h_ref[rows, :], gates_ref[rows, :] = _route(x1, mrow, rw_ref, rb_ref)


def _s5_out(x_all, uc, ud, yc, yd, mods1, d_skip, w_out_c, ln_g, ln_b, rw, rb):
    row_spec = lambda w: pl.BlockSpec((TOK_TILE, w), lambda i: (i, 0))
    uc_spec, ud_spec = _two_stream_specs(TOK_TILE, D)
    n_ctx = T_CTX // TOK_TILE
    chunks = TOK_TILE // S5_Q
    return pl.pallas_call(
        _s5_out_kernel,
        grid=(T_ALL // TOK_TILE,),
        in_specs=[row_spec(D), uc_spec, ud_spec,
                  pl.BlockSpec((S5_NGB, chunks, D), lambda i: (0, jnp.minimum(i, n_ctx - 1), 0)),
                  pl.BlockSpec((S5_NGB, chunks, D), lambda i: (0, jnp.maximum(i - n_ctx, 0), 0)),
                  _full((8, 6 * D), 1), _full((1, D), 1), _full((D, 2 * D), 1), _full((1, D), 1), _full((1, D), 1),
                  _full((2, D, LANE), 1), _full((1, LANE), 1)],
        out_specs=[row_spec(D), row_spec(D), row_spec(LANE)],
        out_shape=[jax.ShapeDtypeStruct((T_ALL, D), f32), jax.ShapeDtypeStruct((T_ALL, D), bf16),
                   jax.ShapeDtypeStruct((T_ALL, LANE), f32)],
        scratch_shapes=[pltpu.VMEM((S5_NGB, TOK_TILE, LANE), f32)],
        compiler_params=_params(1),
        name="s5_out",
    )(x_all, uc, ud, yc, yd, mods1, d_skip, w_out_c, ln_g, ln_b, rw, rb)


def kernel(x_prompt, x_sample, c, cache_attn_k, cache_attn_v, cache_mla_ckv, cache_mla_krope, state_ssm, c_ctx,
           ada_w, ada_b, ln_mix_g, ln_mix_b, ln_ffn_g, ln_ffn_b, w_in_ab, attn_sink, mla_q_norm, mla_kv_norm,
           mla_w_uq, mla_w_ukv, w_out_ab, w_in_c, s5_lam_re, s5_lam_im, s5_log_dt, s5_b_re, s5_b_im, s5_c_re,
           s5_c_im, s5_d, w_out_c, router_w, router_bias, exp_w_gate, exp_w_up, exp_w_down, sh_w_gate, sh_w_up,
           sh_w_down):
    row = lambda v: v.reshape(1, -1)
    xc, xd = x_prompt.reshape(T_CTX, D), x_sample.reshape(T_DEN, D)
    cvec8 = jnp.concatenate([c_ctx[None, :], c, jnp.zeros((8 - 1 - N_DEN_B, D), f32)], axis=0)
    mods = _adaln(cvec8, ada_w, ada_b)

    w_in_p = jnp.pad(w_in_ab[0], ((0, 0), (0, PROJ_W - w_in_ab.shape[-1]))).astype(bf16)
    uq = mla_w_uq[0].reshape(MLA_Q_RANK, MLA_HEADS, MLA_NOPE + MLA_ROPE)
    w_uq_p = jnp.concatenate([uq[:, :, :MLA_NOPE].reshape(MLA_Q_RANK, -1), uq[:, :, MLA_NOPE:].reshape(MLA_Q_RANK, -1)],
                             axis=1).astype(bf16)
    ukv = mla_w_ukv[0].reshape(MLA_KV_RANK, MLA_HEADS, MLA_NOPE + MLA_V)
    w_ukv_p = jnp.concatenate([ukv[:, :, :MLA_NOPE].reshape(MLA_KV_RANK, -1),
                               ukv[:, :, MLA_NOPE:].reshape(MLA_KV_RANK, -1)], axis=1).astype(bf16)
    qa, ka, va, ckv, kr, qm, kvl = _ab_proj(xc, xd, mods[0], w_in_p, row(mla_q_norm[0]), row(mla_kv_norm[0]),
                                            w_uq_p, w_ukv_p)
    w_out_b = w_out_ab[0].astype(bf16)
    g0, b0 = row(ln_mix_g[0]), row(ln_mix_b[0])
    x1c = _ctx_attn(attn_sink[0], qa, ka, va, qm, kvl, kr, xc, mods[0], w_out_b, g0, b0)
    x1d = _den_attn(attn_sink[0], qa, ka, va,
                    cache_attn_k[:, 0].reshape(N_DEN_B, CTX_L, A_KV_HEADS * A_HD),
                    cache_attn_v[:, 0].reshape(N_DEN_B, CTX_L, A_KV_HEADS * A_HD),
                    qm, kvl, kr, cache_mla_ckv[:, 0], cache_mla_krope[:, 0], w_ukv_p, xd, mods[0], w_out_b, g0, b0)
    rw0, rb0 = _router_weights(0, router_w, router_bias)
    x1, h, gates = _router(x1c, x1d, mods[0], rw0, rb0)
    x2 = _moe(0, h, gates, x1, mods[0], exp_w_gate, exp_w_up, exp_w_down, sh_w_gate, sh_w_up, sh_w_down,
              row(ln_ffn_g[0]), row(ln_ffn_b[0]))

    w_in_c_b = w_in_c[0].astype(bf16)
    uc, u2c = _s5_in(x2, 0, mods[1], w_in_c_b, N_CTX_B, CTX_L, lambda i: 0)
    ud, u2d = _s5_in(x2, S5_CTX_TILES, mods[1], w_in_c_b, N_DEN_B, DEN_L, lambda i: 1 + i // S5_DEN_TPS)
    gp = S5_G * S5_P
    chan_major_b = lambda t: jnp.transpose(t[0], (0, 3, 1, 2)).reshape(2, S5_CH, gp)
    chan_major_c = lambda t: jnp.transpose(t[0], (0, 2, 1, 3)).reshape(2, S5_CH, gp)
    h0 = jnp.transpose(state_ssm[:, 0], (1, 2, 0, 3, 4)).reshape(2, 2, N_DEN_B, gp)
    yc, yd, st = _s5_scan(s5_lam_re[0].reshape(2, 1, gp), s5_lam_im[0].reshape(2, 1, gp),
                          jnp.repeat(s5_log_dt[0], S5_P, axis=-1).reshape(2, 1, gp),
                          chan_major_b(s5_b_re), chan_major_b(s5_b_im), chan_major_c(s5_c_re), chan_major_c(s5_c_im),
                          u2c, u2d, h0)
    rw1, rb1 = _router_weights(1, router_w, router_bias)
    x3, h, gates = _s5_out(x2, uc, ud, yc, yd, mods[1], row(s5_d[0]),
                           w_out_c[0].astype(bf16), row(ln_mix_g[1]), row(ln_mix_b[1]), rw1, rb1)
    x4 = _moe(1, h, gates, x3, mods[1], exp_w_gate, exp_w_up, exp_w_down, sh_w_gate, sh_w_up, sh_w_down,
              row(ln_ffn_g[1]), row(ln_ffn_b[1]))

    y_prompt = x4[:T_CTX].reshape(N_CTX_B, CTX_L, D)
    y_sample = x4[T_CTX:].reshape(N_DEN_B, DEN_L, D)
    new_attn_k = ka[:T_CTX].reshape(N_CTX_B, 1, CTX_L, A_KV_HEADS, A_HD)
    new_attn_v = va[:T_CTX].reshape(N_CTX_B, 1, CTX_L, A_KV_HEADS, A_HD)
    new_mla_ckv = ckv[:T_CTX].reshape(N_CTX_B, 1, CTX_L, MLA_KV_RANK)
    new_mla_krope = kr[:T_CTX, :MLA_ROPE].reshape(N_CTX_B, 1, CTX_L, MLA_ROPE)
    new_state_ssm = jnp.transpose(st, (2, 0, 1, 3)).reshape(N_CTX_B, 1, 2, 2, S5_G, S5_P)
    return (y_prompt, y_sample, new_attn_k, new_attn_v, new_mla_ckv, new_mla_krope, new_state_ssm)
```

```python
import functools

import jax
import jax.numpy as jnp
import numpy as np
from jax import lax
from jax.experimental import pallas as pl
from jax.experimental.pallas import tpu as pltpu

f32 = jnp.float32
bf16 = jnp.bfloat16

D = 1024
N_CTX_B, CTX_L = 16, 256
N_DEN_B, DEN_L = 2, 1024
T_CTX = N_CTX_B * CTX_L
T_DEN = N_DEN_B * DEN_L
T_ALL = T_CTX + T_DEN
GRID_W = 64
WINDOW = 128
ROPE_BASE = 10000.0
A_HEADS, A_KV_HEADS, A_HD = 8, 2, 64
A_GROUP = A_HEADS // A_KV_HEADS
A_SCALE = A_HD ** -0.5
MLA_HEADS, MLA_Q_RANK, MLA_KV_RANK = 8, 256, 128
MLA_NOPE, MLA_ROPE, MLA_V = 64, 32, 64
MLA_SCALE = (MLA_NOPE + MLA_ROPE) ** -0.5
N_EXPERTS, TOP_K, EXPERT_FF, SHARED_FF = 64, 6, 128, 128
ROUTED_SCALE = 2.5
DEPTH = 2
ALPHA = (2.0 * DEPTH) ** 0.25
LN_EPS = 1e-5
RMS_EPS = 1e-6
NEG_INF = -1e30
S5_G, S5_CH, S5_P = 64, 16, 64

LANE = 128
SUB = 8
VMEM_LIMIT = 56 * 1024 * 1024

TOK_TILE = 512


def _mod_row(tile_idx, tile_rows):
    start = tile_idx * tile_rows
    return jnp.where(start < T_CTX, 0, 1 + (start - T_CTX) // DEN_L)


def _layer_norm(y, g, b):
    mu = jnp.mean(y, axis=-1, keepdims=True)
    yc = y - mu
    var = jnp.mean(yc * yc, axis=-1, keepdims=True)
    return yc * lax.rsqrt(var + LN_EPS) * g + b


def _silu(x):
    return x * jax.nn.sigmoid(x)


def _dot(a, b):
    return jnp.dot(a, b, preferred_element_type=f32)


def _dot_nt(a, b):
    return lax.dot_general(a, b, (((1,), (1,)), ((), ())), preferred_element_type=f32)


def _split_bf16(a):
    hi = a.astype(bf16)
    return hi, (a - hi.astype(f32)).astype(bf16)


def _full(shape, n_grid):
    zeros = tuple(0 for _ in shape)
    return pl.BlockSpec(shape, lambda *_: zeros)


def _two_stream_specs(tile_rows, width):
    n_ctx = T_CTX // tile_rows
    return (pl.BlockSpec((tile_rows, width), lambda i: (jnp.minimum(i, n_ctx - 1), 0)),
            pl.BlockSpec((tile_rows, width), lambda i: (jnp.maximum(i - n_ctx, 0), 0)))


def _pick(i, tile_rows, ctx_ref, den_ref):
    return lax.cond(i < T_CTX // tile_rows, lambda: ctx_ref[...], lambda: den_ref[...])


def _params(n_grid):
    return pltpu.CompilerParams(dimension_semantics=("arbitrary",) * n_grid, vmem_limit_bytes=VMEM_LIMIT)


ADA_TN = 1536


def _adaln_kernel(c_ref, w_ref, b_ref, o_ref):
    s_hi, s_lo = _split_bf16(_silu(c_ref[...]))
    w_hi, w_lo = _split_bf16(w_ref[0])
    o_ref[0] = _dot(s_hi, w_hi) + (_dot(s_hi, w_lo) + _dot(s_lo, w_hi)) + b_ref[0]


def _adaln(cvec8, ada_w, ada_b):
    n = 6 * D
    return pl.pallas_call(
        _adaln_kernel,
        grid=(DEPTH, n // ADA_TN),
        in_specs=[
            pl.BlockSpec((8, D), lambda l, j: (0, 0)),
            pl.BlockSpec((1, D, ADA_TN), lambda l, j: (l, 0, j)),
            pl.BlockSpec((1, 1, ADA_TN), lambda l, j: (l, 0, j)),
        ],
        out_specs=pl.BlockSpec((1, 8, ADA_TN), lambda l, j: (l, 0, j)),
        out_shape=jax.ShapeDtypeStruct((DEPTH, 8, n), f32),
        compiler_params=_params(2),
        name="adaln",
    )(cvec8, ada_w, ada_b.reshape(DEPTH, 1, n))


def _rope_table_array(head_dim):
    q = head_dim // 4
    pos = np.arange(DEN_L)
    row, col = (pos // GRID_W).astype(np.float64), (pos % GRID_W).astype(np.float64)
    lane = np.arange(LANE) % head_dim
    is_col = lane >= head_dim // 2
    w = lane % (head_dim // 2)
    first = w < q
    inv_freq = ROPE_BASE ** (-np.arange(q, dtype=np.float64) / q)
    ang = np.where(is_col[None, :], col[:, None], row[:, None]) * inv_freq[w % q][None, :]
    cos, sin = np.cos(ang), np.sin(ang)
    sin_a = np.where(first[None, :], -sin, 0.0)
    sin_b = np.where(first[None, :], 0.0, sin)
    ident = np.stack([np.ones((TOK_TILE, LANE)), np.zeros((TOK_TILE, LANE)), np.zeros((TOK_TILE, LANE))])
    tab = np.concatenate([ident, np.stack([cos, sin_a, sin_b])], axis=1).astype(np.float32)
    return jnp.asarray(tab), q


def _rope_chunk(x, tab_ref, q):
    return x * tab_ref[0] + pltpu.roll(x, LANE - q, 1) * tab_ref[1] + pltpu.roll(x, q, 1) * tab_ref[2]


PROJ_W = 1280
C_QA, C_KA, C_VA, C_CQ, C_CKV, C_KR = 0, 512, 640, 768, 1024, 1152
MLA_NN = MLA_HEADS * MLA_NOPE


def _ab_proj_kernel(xc_ref, xd_ref, mods_ref, w_ref, qn_ref, kvn_ref, wuq_ref, wukv_ref, ta_ref, tm_ref,
                    qa_ref, ka_ref, va_ref, ckv_ref, kr_ref, qm_ref, kvl_ref, *, qa_shift, qm_shift):
    i = pl.program_id(0)
    r = _mod_row(i, TOK_TILE)
    mrow = mods_ref[pl.ds(r, 1), :]
    sh, sc = mrow[:, 0:D], mrow[:, D:2 * D]
    x = _pick(i, TOK_TILE, xc_ref, xd_ref)
    h = (x * (1.0 + sc) + sh).astype(bf16)
    proj = _dot(h, w_ref[...])
    for j in range(4):
        c0 = C_QA + LANE * j
        qa_ref[:, LANE * j:LANE * (j + 1)] = _rope_chunk(proj[:, c0:c0 + LANE], ta_ref, qa_shift).astype(bf16)
    ka_ref[...] = _rope_chunk(proj[:, C_KA:C_KA + LANE], ta_ref, qa_shift)
    va_ref[...] = proj[:, C_VA:C_VA + LANE]
    cq = proj[:, C_CQ:C_CQ + MLA_Q_RANK]
    cq = cq * lax.rsqrt(jnp.mean(cq * cq, axis=-1, keepdims=True) + RMS_EPS) * qn_ref[...]
    ckv = proj[:, C_CKV:C_CKV + MLA_KV_RANK]
    ckv = ckv * lax.rsqrt(jnp.mean(ckv * ckv, axis=-1, keepdims=True) + RMS_EPS) * kvn_ref[...]
    ckv_ref[...] = ckv
    kr_ref[...] = _rope_chunk(proj[:, C_KR:C_KR + LANE], tm_ref, qm_shift)
    qm = _dot(cq.astype(bf16), wuq_ref[...])
    qm_ref[:, 0:MLA_NN] = qm[:, 0:MLA_NN].astype(bf16)
    for j in range(2):
        c0 = MLA_NN + LANE * j
        qm_ref[:, c0:c0 + LANE] = _rope_chunk(qm[:, c0:c0 + LANE], tm_ref, qm_shift).astype(bf16)
    kvl_ref[...] = _dot(ckv.astype(bf16), wukv_ref[...]).astype(bf16)


def _rope_block_index(i):
    tiles_ctx = T_CTX // TOK_TILE
    per_seq = DEN_L // TOK_TILE
    return jnp.where(i < tiles_ctx, 0, 1 + (i - tiles_ctx) % per_seq)


def _ab_proj(xc, xd, mods0, w_in_p, q_norm, kv_norm, w_uq_p, w_ukv_p):
    tab_a, qa_shift = _rope_table_array(A_HD)
    tab_m, qm_shift = _rope_table_array(MLA_ROPE)
    row_spec = lambda w: pl.BlockSpec((TOK_TILE, w), lambda i: (i, 0))
    xc_spec, xd_spec = _two_stream_specs(TOK_TILE, D)
    tab_spec = pl.BlockSpec((3, TOK_TILE, LANE), lambda i: (0, _rope_block_index(i), 0))
    outs = [(512, bf16), (LANE, f32), (LANE, f32), (LANE, f32), (LANE, f32), (768, bf16), (1024, bf16)]
    return pl.pallas_call(
        functools.partial(_ab_proj_kernel, qa_shift=qa_shift, qm_shift=qm_shift),
        grid=(T_ALL // TOK_TILE,),
        in_specs=[xc_spec, xd_spec, _full((8, 6 * D), 1), _full((D, PROJ_W), 1), _full((1, MLA_Q_RANK), 1),
                  _full((1, MLA_KV_RANK), 1), _full((MLA_Q_RANK, 768), 1), _full((MLA_KV_RANK, 1024), 1),
                  tab_spec, tab_spec],
        out_specs=[row_spec(w) for w, _ in outs],
        out_shape=[jax.ShapeDtypeStruct((T_ALL, w), dt) for w, dt in outs],
        compiler_params=_params(1),
        name="ab_proj",
    )(xc, xd, mods0, w_in_p, q_norm, kv_norm, w_uq_p, w_ukv_p, tab_a, tab_m)


def _softmax_blocks(s_refs, p_refs, sink_col=None):
    m = s_refs[0][...].max(axis=-1, keepdims=True)
    for s_ref in s_refs[1:]:
        m = jnp.maximum(m, s_ref[...].max(axis=-1, keepdims=True))
    if sink_col is not None:
        m = jnp.maximum(m, sink_col)
    l = None
    for s_ref, p_ref in zip(s_refs, p_refs):
        p = jnp.exp(s_ref[...] - m)
        p_ref[...] = p.astype(bf16)
        ps = p.sum(axis=-1, keepdims=True)
        l = ps if l is None else l + ps
    if sink_col is not None:
        l = l + jnp.exp(sink_col - m)
    return 1.0 / l


def _sink_column(sink_ref, rows_per_head):
    return jnp.concatenate([jnp.full((rows_per_head, 1), sink_ref[h], f32) for h in range(A_HEADS)], axis=0)


def _mla_q(qm_ref, h):
    rows = qm_ref.shape[0]
    return jnp.concatenate([qm_ref[:, MLA_NOPE * h:MLA_NOPE * (h + 1)],
                            qm_ref[:, MLA_NN + MLA_ROPE * h:MLA_NN + MLA_ROPE * (h + 1)],
                            jnp.zeros((rows, LANE - MLA_NOPE - MLA_ROPE), bf16)], axis=1)


def _mla_k(k_nope_h, k_rope):
    rows = k_nope_h.shape[0]
    return jnp.concatenate([k_nope_h, k_rope, jnp.zeros((rows, LANE - MLA_NOPE - MLA_ROPE), bf16)], axis=1)


def _mix_out_ln(merged_ref, wout_ref, x, mods_ref, r, g_ref, b_ref):
    out = _dot(merged_ref[...], wout_ref[...])
    gate = mods_ref[pl.ds(r, 1), 2 * D:3 * D]
    return _layer_norm(ALPHA * x + gate * out, g_ref[...], b_ref[...])


def _ctx_attn_kernel(sink_ref, qa_ref, ka_ref, va_ref, qm_ref, kvl_ref, kr_ref, x_ref, mods_ref, wout_ref,
                     g_ref, b_ref, o_ref, merged_ref, sa_ref, sm_ref, pa_ref, pm_ref):
    n = CTX_L
    ka = ka_ref[...].astype(bf16)
    va = va_ref[...].astype(bf16)
    for j in range(A_KV_HEADS):
        q4 = jnp.concatenate([qa_ref[:, A_HD * h:A_HD * (h + 1)] for h in range(A_GROUP * j, A_GROUP * (j + 1))],
                             axis=0)
        sa_ref[A_GROUP * n * j:A_GROUP * n * (j + 1), :] = _dot_nt(q4, ka[:, A_HD * j:A_HD * (j + 1)]) * A_SCALE
    kr = kr_ref[:, 0:MLA_ROPE].astype(bf16)
    for h in range(MLA_HEADS):
        k_cat = _mla_k(kvl_ref[:, MLA_NOPE * h:MLA_NOPE * (h + 1)], kr)
        sm_ref[n * h:n * (h + 1), :] = _dot_nt(_mla_q(qm_ref, h), k_cat) * MLA_SCALE
    rla = _softmax_blocks([sa_ref], [pa_ref], _sink_column(sink_ref, n))
    rlm = _softmax_blocks([sm_ref], [pm_ref])
    for j in range(A_KV_HEADS):
        rows = slice(A_GROUP * n * j, A_GROUP * n * (j + 1))
        o4 = _dot(pa_ref[rows, :], va[:, A_HD * j:A_HD * (j + 1)]) * rla[rows]
        for g in range(A_GROUP):
            h = A_GROUP * j + g
            merged_ref[:, A_HD * h:A_HD * (h + 1)] = o4[n * g:n * (g + 1)].astype(bf16)
    for h in range(MLA_HEADS):
        rows = slice(n * h, n * (h + 1))
        v = kvl_ref[:, MLA_NN + MLA_V * h:MLA_NN + MLA_V * (h + 1)]
        merged_ref[:, MLA_NN + MLA_V * h:MLA_NN + MLA_V * (h + 1)] = (_dot(pm_ref[rows, :], v) * rlm[rows]).astype(bf16)
    o_ref[...] = _mix_out_ln(merged_ref, wout_ref, x_ref[...], mods_ref, 0, g_ref, b_ref)


def _ctx_attn(sink, qa, ka, va, qm, kvl, kr, x_all, mods0, w_out, ln_g, ln_b):
    blk = lambda w: pl.BlockSpec((CTX_L, w), lambda b: (b, 0))
    return pl.pallas_call(
        _ctx_attn_kernel,
        grid=(N_CTX_B,),
        in_specs=[pl.BlockSpec(memory_space=pltpu.SMEM), blk(512), blk(LANE), blk(LANE), blk(768), blk(1024),
                  blk(LANE), blk(D), _full((8, 6 * D), 1), _full((D, D), 1), _full((1, D), 1), _full((1, D), 1)],
        out_specs=blk(D),
        out_shape=jax.ShapeDtypeStruct((T_CTX, D), f32),
        scratch_shapes=[pltpu.VMEM((CTX_L, D), bf16),
                        pltpu.VMEM((A_HEADS * CTX_L, CTX_L), f32), pltpu.VMEM((MLA_HEADS * CTX_L, CTX_L), f32),
                        pltpu.VMEM((A_HEADS * CTX_L, CTX_L), bf16), pltpu.VMEM((MLA_HEADS * CTX_L, CTX_L), bf16)],
        compiler_params=_params(1),
        name="ctx_attn",
    )(sink, qa, ka, va, qm, kvl, kr, x_all, mods0, w_out, ln_g, ln_b)


QB = 256
WIN = QB + 2 * WINDOW
DEN_BLK0 = T_CTX // DEN_L
MLA_KEYS = CTX_L + DEN_L


def _den_attn_kernel(sink_ref, qa_ref, ka_ref, va_ref, cak_ref, cav_ref, qm_ref, kvl_ref, kr_ref, cckv_ref, ckr_ref,
                     wukv_ref, x_ref, mods_ref, wout_ref, g_ref, b_ref, o_ref, merged_ref, kcat_ref, vcat_ref,
                     saw_ref, sac_ref, sm_ref, paw_ref, pac_ref, pm_ref):
    b = pl.program_id(0)
    n = pl.program_id(1)

    @pl.when(n == 0)
    def _():
        kvc = _dot(cckv_ref[0].astype(bf16), wukv_ref[...]).astype(bf16)
        kr_ctx = ckr_ref[0].astype(bf16)
        kr_lat = kr_ref[:, 0:MLA_ROPE].astype(bf16)
        for h in range(MLA_HEADS):
            ns = slice(MLA_NOPE * h, MLA_NOPE * (h + 1))
            kcat_ref[h, 0:CTX_L, :] = _mla_k(kvc[:, ns], kr_ctx)
            kcat_ref[h, CTX_L:MLA_KEYS, :] = _mla_k(kvl_ref[:, ns], kr_lat)
        vcat_ref[0:CTX_L, :] = kvc[:, MLA_NN:2 * MLA_NN]
        vcat_ref[CTX_L:MLA_KEYS, :] = kvl_ref[:, MLA_NN:2 * MLA_NN]

    start = pl.multiple_of(jnp.clip(QB * n - WINDOW, 0, DEN_L - WIN), WINDOW)
    grp_rows = A_GROUP * QB
    qpos = QB * n + (lax.broadcasted_iota(jnp.int32, (grp_rows, WIN), 0) & (QB - 1))
    kpos = start + lax.broadcasted_iota(jnp.int32, (grp_rows, WIN), 1)
    valid = jnp.abs(qpos - kpos) <= WINDOW
    kwin = ka_ref[pl.ds(start, WIN), :].astype(bf16)
    vwin = va_ref[pl.ds(start, WIN), :].astype(bf16)
    kctx = cak_ref[0].astype(bf16)
    vctx = cav_ref[0].astype(bf16)
    for j in range(A_KV_HEADS):
        sl = slice(A_HD * j, A_HD * (j + 1))
        rows = slice(grp_rows * j, grp_rows * (j + 1))
        q4 = jnp.concatenate([qa_ref[:, A_HD * h:A_HD * (h + 1)] for h in range(A_GROUP * j, A_GROUP * (j + 1))],
                             axis=0)
        saw_ref[rows, :] = jnp.where(valid, _dot_nt(q4, kwin[:, sl]) * A_SCALE, NEG_INF)
        sac_ref[rows, :] = _dot_nt(q4, kctx[:, sl]) * A_SCALE
    for h in range(MLA_HEADS):
        sm_ref[QB * h:QB * (h + 1), :] = _dot_nt(_mla_q(qm_ref, h), kcat_ref[h]) * MLA_SCALE
    rla = _softmax_blocks([saw_ref, sac_ref], [paw_ref, pac_ref], _sink_column(sink_ref, QB))
    rlm = _softmax_blocks([sm_ref], [pm_ref])
    for j in range(A_KV_HEADS):
        sl = slice(A_HD * j, A_HD * (j + 1))
        rows = slice(grp_rows * j, grp_rows * (j + 1))
        o4 = (_dot(paw_ref[rows, :], vwin[:, sl]) + _dot(pac_ref[rows, :], vctx[:, sl])) * rla[rows]
        for g in range(A_GROUP):
            h = A_GROUP * j + g
            merged_ref[:, A_HD * h:A_HD * (h + 1)] = o4[QB * g:QB * (g + 1)].astype(bf16)
    for h in range(MLA_HEADS):
        rows = slice(QB * h, QB * (h + 1))
        o = _dot(pm_ref[rows, :], vcat_ref[:, MLA_V * h:MLA_V * (h + 1)]) * rlm[rows]
        merged_ref[:, MLA_NN + MLA_V * h:MLA_NN + MLA_V * (h + 1)] = o.astype(bf16)
    o_ref[...] = _mix_out_ln(merged_ref, wout_ref, x_ref[...], mods_ref, 1 + b, g_ref, b_ref)


def _den_attn(sink, qa, ka, va, cache_k, cache_v, qm, kvl, kr, cache_ckv, cache_kr, w_ukv_p, x_all, mods0, w_out,
              ln_g, ln_b):
    nq = DEN_L // QB
    qblk = lambda w: pl.BlockSpec((QB, w), lambda b, n: (T_CTX // QB + b * nq + n, 0))
    seq = lambda w: pl.BlockSpec((DEN_L, w), lambda b, n: (DEN_BLK0 + b, 0))
    cache = lambda w: pl.BlockSpec((1, CTX_L, w), lambda b, n: (b, 0, 0))
    return pl.pallas_call(
        _den_attn_kernel,
        grid=(N_DEN_B, nq),
        in_specs=[pl.BlockSpec(memory_space=pltpu.SMEM), qblk(512), seq(LANE), seq(LANE), cache(LANE), cache(LANE),
                  qblk(768), seq(1024), seq(LANE), cache(MLA_KV_RANK), cache(MLA_ROPE),
                  _full((MLA_KV_RANK, 1024), 2), pl.BlockSpec((QB, D), lambda b, n: (b * nq + n, 0)),
                  _full((8, 6 * D), 2), _full((D, D), 2), _full((1, D), 2),
                  _full((1, D), 2)],
        out_specs=pl.BlockSpec((QB, D), lambda b, n: (b * nq + n, 0)),
        out_shape=jax.ShapeDtypeStruct((T_DEN, D), f32),
        scratch_shapes=[pltpu.VMEM((QB, D), bf16), pltpu.VMEM((MLA_HEADS, MLA_KEYS, LANE), bf16),
                        pltpu.VMEM((MLA_KEYS, MLA_NN), bf16)]
        + [pltpu.VMEM((A_HEADS * QB, w), dt) for dt in (f32, bf16) for w in (WIN, CTX_L, MLA_KEYS)],
        compiler_params=_params(2),
        name="den_attn",
    )(sink, qa, ka, va, cache_k, cache_v, qm, kvl, kr, cache_ckv, cache_kr, w_ukv_p, x_all, mods0, w_out, ln_g, ln_b)


SUBTILE = 256


def _route(x1, mrow, rw_ref, rb_ref):
    sh, sc = mrow[:, 3 * D:4 * D], mrow[:, 4 * D:5 * D]
    h = x1 * (1.0 + sc) + sh
    h_hi = h.astype(bf16)
    h_lo = (h - h_hi.astype(f32)).astype(bf16)
    logits = _dot(h_hi, rw_ref[0]) + (_dot(h_hi, rw_ref[1]) + _dot(h_lo, rw_ref[0]))
    scores = jax.nn.sigmoid(logits)
    lane = lax.broadcasted_iota(jnp.int32, scores.shape, 1).astype(f32)
    sel = jnp.where(lane < N_EXPERTS, scores + rb_ref[...], -jnp.inf)
    gates = jnp.zeros_like(scores)
    for _ in range(TOP_K):
        m = sel.max(axis=-1, keepdims=True)
        idx = jnp.where(sel == m, lane, float(LANE)).min(axis=-1, keepdims=True)
        hit = lane == idx
        gates = jnp.where(hit, scores, gates)
        sel = jnp.where(hit, -jnp.inf, sel)
    return h_hi, gates / gates.sum(axis=-1, keepdims=True) * ROUTED_SCALE


def _router_kernel(xc_ref, xd_ref, mods_ref, rw_ref, rb_ref, x_ref, h_ref, gates_ref):
    i = pl.program_id(0)
    r = _mod_row(i, TOK_TILE)
    mrow = mods_ref[pl.ds(r, 1), :]
    x1 = _pick(i, TOK_TILE, xc_ref, xd_ref)
    x_ref[...] = x1
    h_ref[...], gates_ref[...] = _route(x1, mrow, rw_ref, rb_ref)


def _router(x1c, x1d, mods_l, router_w_p, router_b_p):
    row_spec = lambda w: pl.BlockSpec((TOK_TILE, w), lambda i: (i, 0))
    xc_spec, xd_spec = _two_stream_specs(TOK_TILE, D)
    return pl.pallas_call(
        _router_kernel,
        grid=(T_ALL // TOK_TILE,),
        in_specs=[xc_spec, xd_spec, _full((8, 6 * D), 1), _full((2, D, LANE), 1), _full((1, LANE), 1)],
        out_specs=[row_spec(D), row_spec(D), row_spec(LANE)],
        out_shape=[jax.ShapeDtypeStruct((T_ALL, D), f32), jax.ShapeDtypeStruct((T_ALL, D), bf16),
                   jax.ShapeDtypeStruct((T_ALL, LANE), f32)],
        compiler_params=_params(1),
        name="router",
    )(x1c, x1d, mods_l, router_w_p, router_b_p)


MOE_TOK = 1536
MOE_EG = 4
MOE_TILE = 512
MOE_FF = MOE_EG * EXPERT_FF


def _moe_kernel(h_ref, gates_ref, x_ref, mods_ref, wg_ref, wu_ref, wd_ref, sg_ref, su_ref, sd_ref, g_ref, b_ref,
                o_ref):
    p = pl.program_id(0)
    e = pl.program_id(1)
    n_tiles = MOE_TOK // MOE_TILE

    def gate_f(t):
        r = _mod_row(p * n_tiles + t, MOE_TILE)
        return mods_ref[pl.ds(r, 1), 5 * D:6 * D]

    def rows_of(t):
        if isinstance(t, int):
            return pl.ds(t * MOE_TILE, MOE_TILE)
        return pl.ds(pl.multiple_of(t * MOE_TILE, MOE_TILE), MOE_TILE)

    @pl.when(e == 0)
    def _():
        sg = sg_ref[...].astype(bf16)
        su = su_ref[...].astype(bf16)
        sd = sd_ref[...].astype(bf16)

        def body(t, c):
            rows = rows_of(t)
            ht = h_ref[rows, :]
            hid = _silu(_dot(ht, sg)) * _dot(ht, su)
            o_ref[rows, :] = ALPHA * x_ref[rows, :] + gate_f(t) * _dot(hid.astype(bf16), sd)
            return c

        lax.fori_loop(0, n_tiles, body, 0)

    wg = jnp.concatenate([wg_ref[k].astype(bf16) for k in range(MOE_EG)], axis=1)
    wu = jnp.concatenate([wu_ref[k].astype(bf16) for k in range(MOE_EG)], axis=1)
    wd = jnp.concatenate([wd_ref[k].astype(bf16) for k in range(MOE_EG)], axis=0)
    lane = lax.broadcasted_iota(jnp.int32, (MOE_TILE, LANE), 1)

    def body(t, c):
        rows = rows_of(t)
        ht = h_ref[rows, :]
        hid = _silu(_dot(ht, wg)) * _dot(ht, wu)
        gt = gates_ref[rows, :]
        parts = []
        for k in range(MOE_EG):
            col = jnp.where(lane == e * MOE_EG + k, gt, 0.0).sum(axis=-1, keepdims=True)
            parts.append((hid[:, EXPERT_FF * k:EXPERT_FF * (k + 1)] * col).astype(bf16))
        o_ref[rows, :] += gate_f(t) * _dot(jnp.concatenate(parts, axis=1), wd)
        return c

    for t in range(n_tiles):
        body(t, 0)

    @pl.when(e == pl.num_programs(1) - 1)
    def _():
        def body(t, c):
            rows = rows_of(t)
            o_ref[rows, :] = _layer_norm(o_ref[rows, :], g_ref[...], b_ref[...])
            return c

        lax.fori_loop(0, n_tiles, body, 0)


def _moe(l, h, gates, x1, mods_l, wg, wu, wd, sg, su, sd, ln_g, ln_b):
    tok = lambda w: pl.BlockSpec((MOE_TOK, w), lambda p, e: (p, 0))
    return pl.pallas_call(
        _moe_kernel,
        grid=(T_ALL // MOE_TOK, N_EXPERTS // MOE_EG),
        in_specs=[tok(D), tok(LANE), tok(D), _full((8, 6 * D), 2),
                  pl.BlockSpec((None, MOE_EG, D, EXPERT_FF), lambda p, e: (l, e, 0, 0)),
                  pl.BlockSpec((None, MOE_EG, D, EXPERT_FF), lambda p, e: (l, e, 0, 0)),
                  pl.BlockSpec((None, MOE_EG, EXPERT_FF, D), lambda p, e: (l, e, 0, 0)),
                  pl.BlockSpec((None, D, SHARED_FF), lambda p, e: (l, 0, 0)),
                  pl.BlockSpec((None, D, SHARED_FF), lambda p, e: (l, 0, 0)),
                  pl.BlockSpec((None, SHARED_FF, D), lambda p, e: (l, 0, 0)),
                  _full((1, D), 2), _full((1, D), 2)],
        out_specs=tok(D),
        out_shape=jax.ShapeDtypeStruct((T_ALL, D), f32),
        compiler_params=_params(2),
        name="moe",
    )(h, gates, x1, mods_l, wg, wu, wd, sg, su, sd, ln_g, ln_b)


def _router_weights(l, router_w, router_bias):
    rw = jnp.pad(router_w[l], ((0, 0), (0, LANE - N_EXPERTS)))
    rw_hi, rw_lo = _split_bf16(rw)
    rb = jnp.pad(router_bias[l], (0, LANE - N_EXPERTS)).reshape(1, LANE)
    return jnp.stack([rw_hi, rw_lo]), rb


S5_Q = 8
S5_NGB = D // LANE


def _s5_in_kernel(x_ref, mods_ref, w_ref, u_ref, u2_ref, slab_ref, *, row_of, seq_len):
    r = row_of(pl.program_id(0))
    mrow = mods_ref[pl.ds(r, 1), :]
    sh, sc = mrow[:, 0:D], mrow[:, D:2 * D]
    h = (x_ref[...] * (1.0 + sc) + sh).astype(bf16)
    u = _dot(h, w_ref[...])
    u_ref[...] = u
    for s in range(S5_NGB):
        slab_ref[s] = u[:, LANE * s:LANE * (s + 1)]
    kt = seq_len // S5_Q
    for s in range(S5_NGB):
        for q in range(TOK_TILE // seq_len):
            for j in range(S5_Q):
                u2_ref[s, q * kt:(q + 1) * kt, LANE * j:LANE * (j + 1)] = (
                    slab_ref[s, pl.ds(q * seq_len + j, kt, stride=S5_Q), :].astype(bf16))


def _s5_in(x_all, tile0, mods1, w_in_c, n_b, seq_len, row_of):
    n_tiles = n_b * seq_len // TOK_TILE
    chunks = TOK_TILE // S5_Q
    return pl.pallas_call(
        functools.partial(_s5_in_kernel, row_of=row_of, seq_len=min(seq_len, TOK_TILE)),
        grid=(n_tiles,),
        in_specs=[pl.BlockSpec((TOK_TILE, D), lambda i: (tile0 + i, 0)), _full((8, 6 * D), 1), _full((D, D), 1)],
        out_specs=[pl.BlockSpec((TOK_TILE, D), lambda i: (i, 0)),
                   pl.BlockSpec((S5_NGB, chunks, D), lambda i: (0, i, 0))],
        out_shape=[jax.ShapeDtypeStruct((n_b * seq_len, D), f32),
                   jax.ShapeDtypeStruct((S5_NGB, n_tiles * chunks, D), bf16)],
        scratch_shapes=[pltpu.VMEM((S5_NGB, TOK_TILE, LANE), f32)],
        compiler_params=_params(1),
        name="s5_in",
    )(x_all, mods1, w_in_c)


S5_GL = (LANE // S5_CH) * S5_P
S5_ROWS_C = (CTX_L // S5_Q) * N_CTX_B
S5_ROWS_D = (DEN_L // S5_Q) * N_DEN_B


def _transpose_exact(t):
    n = t.shape[1]
    eye = (lax.broadcasted_iota(jnp.int32, (n, n), 0) == lax.broadcasted_iota(jnp.int32, (n, n), 1))
    eye = jnp.where(eye, 1.0, 0.0).astype(bf16)
    hi = t.astype(bf16)
    r1 = t - hi.astype(f32)
    mid = r1.astype(bf16)
    lo = (r1 - mid.astype(f32)).astype(bf16)
    return _dot_nt(eye, hi) + (_dot_nt(eye, mid) + _dot_nt(eye, lo))


def _s5_scan_kernel(lre_ref, lim_ref, ldt_ref, btr_ref, bti_ref, ctr_ref, cti_ref, uc_ref, ud_ref, h0_ref,
                    yc_ref, yd_ref, st_ref, win_ref, mso_ref, wit_ref, a_ref, s_ref, hp_ref):
    gl = S5_GL
    rowg = lax.shift_right_logical(lax.broadcasted_iota(jnp.int32, (LANE, gl), 0), 4)
    colg = lax.shift_right_logical(lax.broadcasted_iota(jnp.int32, (LANE, gl), 1), 6)
    same_group = rowg == colg
    reps = LANE // S5_CH

    def expand(t):
        return jnp.where(same_group, jnp.concatenate([t] * reps, axis=0), 0.0)

    for d in range(2):
        fwd = d == 0
        lre, lim = lre_ref[d], lim_ref[d]
        dt = jnp.exp(ldt_ref[d])
        a, w = lre * dt, lim * dt
        pre = [jnp.exp(m * a) * jnp.cos(m * w) for m in range(S5_Q + 1)]
        pim = [jnp.exp(m * a) * jnp.sin(m * w) for m in range(S5_Q + 1)]
        xr, xi = pre[1] - 1.0, pim[1]
        den = lre * lre + lim * lim
        cfr, cfi = (xr * lre + xi * lim) / den, (xi * lre - xr * lim) / den
        btr, bti = _transpose_exact(btr_ref[d]), _transpose_exact(bti_ref[d])
        bexp_r = expand(cfr * btr - cfi * bti)
        bexp_i = expand(cfr * bti + cfi * btr)
        cexp_r, cexp_i = expand(ctr_ref[d]), expand(cti_ref[d])
        for m in range(S5_Q + 1):
            a_ref[m, :, 0:gl] = cexp_r * pre[m] - cexp_i * pim[m]
            a_ref[m, :, gl:2 * gl] = -(cexp_r * pim[m] + cexp_i * pre[m])
        for j in range(S5_Q):
            m = S5_Q - 1 - j if fwd else j
            win_ref[LANE * j:LANE * (j + 1), 0:gl] = (pre[m] * bexp_r - pim[m] * bexp_i).astype(bf16)
            win_ref[LANE * j:LANE * (j + 1), gl:2 * gl] = (pre[m] * bexp_i + pim[m] * bexp_r).astype(bf16)
        for j in range(S5_Q):
            m = j + 1 if fwd else S5_Q - j
            mso_ref[LANE * j:LANE * (j + 1), :] = a_ref[m].astype(bf16)
        b2_hi, b2_lo = _split_bf16(jnp.concatenate([bexp_r, bexp_i], axis=1))
        kt = []
        for tau in range(S5_Q):
            a_hi, a_lo = _split_bf16(a_ref[tau])
            kt.append(_dot_nt(b2_hi, a_hi) + _dot_nt(b2_hi, a_lo) + _dot_nt(b2_lo, a_hi))
        for j in range(S5_Q):
            for jp in range(S5_Q):
                tau = jp - j if fwd else j - jp
                blk = slice(LANE * j, LANE * (j + 1)), slice(LANE * jp, LANE * (jp + 1))
                if fwd:
                    wit_ref[blk] = kt[tau] if tau >= 0 else jnp.zeros((LANE, LANE), f32)
                elif tau >= 0:
                    wit_ref[blk] = wit_ref[blk] + kt[tau]

        l8r, l8i = pre[S5_Q], pim[S5_Q]

        nsl = gl // LANE

        def slabs(ref, rs, first):
            return jnp.concatenate([ref[first + sl, rs, :] for sl in range(nsl)], axis=1)

        def put_slabs(ref, rs, first, val):
            for sl in range(nsl):
                ref[first + sl, rs, :] = val[:, LANE * sl:LANE * (sl + 1)]

        def advance(hr, hi_, sr, si):
            return l8r * hr - l8i * hi_ + sr, l8r * hi_ + l8i * hr + si

        def run(u_ref, y_ref, n_b, n_k, h_init):
            rows = n_b * n_k
            s = _dot(u_ref[0], win_ref[...])
            if n_b % SUB == 0:
                pitch = n_k + 1
                for bb in range(n_b):
                    dst = slice(bb * pitch, bb * pitch + n_k)
                    put_slabs(s_ref, dst, 0, s[bb * n_k:(bb + 1) * n_k, 0:gl])
                    put_slabs(s_ref, dst, nsl, s[bb * n_k:(bb + 1) * n_k, gl:2 * gl])

                def step(i, carry):
                    hr, hi_ = carry
                    rs = pl.ds(i if fwd else n_k - 1 - i, n_b, stride=pitch)
                    put_slabs(hp_ref, rs, 0, hr)
                    put_slabs(hp_ref, rs, nsl, hi_)
                    return advance(hr, hi_, slabs(s_ref, rs, 0), slabs(s_ref, rs, nsl))

                h_fin = lax.fori_loop(0, n_k, step, h_init)
                hp = jnp.concatenate(
                    [jnp.concatenate([hp_ref[sl, bb * pitch:bb * pitch + n_k, :] for sl in range(2 * nsl)], axis=1)
                     for bb in range(n_b)], axis=0).astype(bf16)
            else:
                put_slabs(s_ref, slice(0, rows), 0, s[:, 0:gl])
                put_slabs(s_ref, slice(0, rows), nsl, s[:, gl:2 * gl])
                n_it = n_k // SUB

                def step(i, carry):
                    it = i if fwd else n_it - 1 - i
                    out = []
                    for bb in range(n_b):
                        hr, hi_ = carry[bb]
                        rs = pl.ds(pl.multiple_of(bb * n_k + it * SUB, SUB), SUB)
                        s_re, s_im = slabs(s_ref, rs, 0), slabs(s_ref, rs, nsl)
                        prev_r, prev_i = [None] * SUB, [None] * SUB
                        for sub in (range(SUB) if fwd else reversed(range(SUB))):
                            prev_r[sub], prev_i[sub] = hr, hi_
                            hr, hi_ = advance(hr, hi_, s_re[sub:sub + 1], s_im[sub:sub + 1])
                        put_slabs(hp_ref, rs, 0, jnp.concatenate(prev_r, axis=0))
                        put_slabs(hp_ref, rs, nsl, jnp.concatenate(prev_i, axis=0))
                        out.append((hr, hi_))
                    return tuple(out)

                fin = lax.fori_loop(0, n_it, step, tuple((h_init[0][bb:bb + 1], h_init[1][bb:bb + 1])
                                                          for bb in range(n_b)))
                h_fin = (jnp.concatenate([f[0] for f in fin], axis=0), jnp.concatenate([f[1] for f in fin], axis=0))
                hp = jnp.concatenate([hp_ref[sl, 0:rows, :] for sl in range(2 * nsl)], axis=1).astype(bf16)
            y = _dot_nt(hp, mso_ref[...])
            if fwd:
                y_ref[0] = y
            else:
                y_ref[0] += y
            return h_fin

        zeros = jnp.zeros((N_CTX_B, gl), f32)
        hr, hi_ = run(uc_ref, yc_ref, N_CTX_B, CTX_L // S5_Q, (zeros, zeros))
        st_ref[d, 0] = hr
        st_ref[d, 1] = hi_
        run(ud_ref, yd_ref, N_DEN_B, DEN_L // S5_Q, (h0_ref[d, 0], h0_ref[d, 1]))

    wit = wit_ref[...].astype(bf16)
    yc_ref[0] += _dot(uc_ref[0], wit)
    yd_ref[0] += _dot(ud_ref[0], wit)


def _s5_scan(lam_re, lam_im, log_dt, bt_re, bt_im, ct_re, ct_im, u2c, u2d, h0):
    gl = S5_GL
    vec = pl.BlockSpec((2, 1, gl), lambda g: (0, 0, g))
    tab = pl.BlockSpec((2, S5_CH, gl), lambda g: (0, 0, g))
    btab = pl.BlockSpec((2, gl, S5_CH), lambda g: (0, g, 0))
    rows = lambda n: pl.BlockSpec((1, n, D), lambda g: (g, 0, 0))
    return pl.pallas_call(
        _s5_scan_kernel,
        grid=(S5_NGB,),
        in_specs=[vec, vec, vec, btab, btab, tab, tab, rows(S5_ROWS_C), rows(S5_ROWS_D),
                  pl.BlockSpec((2, 2, N_DEN_B, gl), lambda g: (0, 0, 0, g))],
        out_specs=[rows(S5_ROWS_C), rows(S5_ROWS_D), pl.BlockSpec((2, 2, N_CTX_B, gl), lambda g: (0, 0, 0, g))],
        out_shape=[jax.ShapeDtypeStruct((S5_NGB, S5_ROWS_C, D), f32), jax.ShapeDtypeStruct((S5_NGB, S5_ROWS_D, D), f32),
                   jax.ShapeDtypeStruct((2, 2, N_CTX_B, S5_G * S5_P), f32)],
        scratch_shapes=[pltpu.VMEM((D, 2 * gl), bf16), pltpu.VMEM((D, 2 * gl), bf16), pltpu.VMEM((D, D), f32),
                        pltpu.VMEM((S5_Q + 1, LANE, 2 * gl), f32),
                        pltpu.VMEM((2 * gl // LANE, S5_ROWS_C + 2 * N_CTX_B, LANE), f32),
                        pltpu.VMEM((2 * gl // LANE, S5_ROWS_C + 2 * N_CTX_B, LANE), f32)],
        compiler_params=_params(1),
        name="s5_scan",
    )(lam_re, lam_im, log_dt, bt_re, bt_im, ct_re, ct_im, u2c, u2d, h0)


def _gelu_tanh(x):
    return 0.5 * x * (1.0 + jnp.tanh(np.sqrt(2.0 / np.pi).astype(np.float32) * (x + 0.044715 * (x * x * x))))


def _s5_out_kernel(x_ref, uc_ref, ud_ref, yc_ref, yd_ref, mods_ref, dsk_ref, wout_ref, g_ref, b_ref, rw_ref, rb_ref,
                   x1_ref, h_ref, gates_ref, slab_ref):
    i = pl.program_id(0)
    is_ctx = i < T_CTX // TOK_TILE
    r = _mod_row(i, TOK_TILE)
    mrow = mods_ref[pl.ds(r, 1), :]
    u = _pick(i, TOK_TILE, uc_ref, ud_ref)

    def unchunk(y_ref, seq_len):
        kt = seq_len // S5_Q
        for s in range(S5_NGB):
            for q in range(TOK_TILE // seq_len):
                for j in range(S5_Q):
                    slab_ref[s, pl.ds(q * seq_len + j, kt, stride=S5_Q), :] = (
                        y_ref[s, q * kt:(q + 1) * kt, LANE * j:LANE * (j + 1)])

    lax.cond(is_ctx, lambda: unchunk(yc_ref, min(CTX_L, TOK_TILE)), lambda: unchunk(yd_ref, min(DEN_L, TOK_TILE)))
    for rows in (slice(a, a + SUBTILE) for a in range(0, TOK_TILE, SUBTILE)):
        y = jnp.concatenate([slab_ref[s, rows, :] for s in range(S5_NGB)], axis=1) + dsk_ref[...] * u[rows]
        z = _dot(_gelu_tanh(y).astype(bf16), wout_ref[...])
        out = z[:, 0:D] * jax.nn.sigmoid(z[:, D:2 * D])
        x1 = _layer_norm(ALPHA * x_ref[rows, :] + mrow[:, 2 * D:3 * D] * out, g_ref[...], b_ref[...])
        x1_ref[rows, :] = x1
        h_ref[rows, :], gates_ref[rows, :] = _route(x1, mrow, rw_ref, rb_ref)


def _s5_out(x_all, uc, ud, yc, yd, mods1, d_skip, w_out_c, ln_g, ln_b, rw, rb):
    row_spec = lambda w: pl.BlockSpec((TOK_TILE, w), lambda i: (i, 0))
    uc_spec, ud_spec = _two_stream_specs(TOK_TILE, D)
    n_ctx = T_CTX // TOK_TILE
    chunks = TOK_TILE // S5_Q
    return pl.pallas_call(
        _s5_out_kernel,
        grid=(T_ALL // TOK_TILE,),
        in_specs=[row_spec(D), uc_spec, ud_spec,
                  pl.BlockSpec((S5_NGB, chunks, D), lambda i: (0, jnp.minimum(i, n_ctx - 1), 0)),
                  pl.BlockSpec((S5_NGB, chunks, D), lambda i: (0, jnp.maximum(i - n_ctx, 0), 0)),
                  _full((8, 6 * D), 1), _full((1, D), 1), _full((D, 2 * D), 1), _full((1, D), 1), _full((1, D), 1),
                  _full((2, D, LANE), 1), _full((1, LANE), 1)],
        out_specs=[row_spec(D), row_spec(D), row_spec(LANE)],
        out_shape=[jax.ShapeDtypeStruct((T_ALL, D), f32), jax.ShapeDtypeStruct((T_ALL, D), bf16),
                   jax.ShapeDtypeStruct((T_ALL, LANE), f32)],
        scratch_shapes=[pltpu.VMEM((S5_NGB, TOK_TILE, LANE), f32)],
        compiler_params=_params(1),
        name="s5_out",
    )(x_all, uc, ud, yc, yd, mods1, d_skip, w_out_c, ln_g, ln_b, rw, rb)


def kernel(x_prompt, x_sample, c, cache_attn_k, cache_attn_v, cache_mla_ckv, cache_mla_krope, state_ssm, c_ctx,
           ada_w, ada_b, ln_mix_g, ln_mix_b, ln_ffn_g, ln_ffn_b, w_in_ab, attn_sink, mla_q_norm, mla_kv_norm,
           mla_w_uq, mla_w_ukv, w_out_ab, w_in_c, s5_lam_re, s5_lam_im, s5_log_dt, s5_b_re, s5_b_im, s5_c_re,
           s5_c_im, s5_d, w_out_c, router_w, router_bias, exp_w_gate, exp_w_up, exp_w_down, sh_w_gate, sh_w_up,
           sh_w_down):
    row = lambda v: v.reshape(1, -1)
    xc, xd = x_prompt.reshape(T_CTX, D), x_sample.reshape(T_DEN, D)
    cvec8 = jnp.concatenate([c_ctx[None, :], c, jnp.zeros((8 - 1 - N_DEN_B, D), f32)], axis=0)
    mods = _adaln(cvec8, ada_w, ada_b)

    w_in_p = jnp.pad(w_in_ab[0], ((0, 0), (0, PROJ_W - w_in_ab.shape[-1]))).astype(bf16)
    uq = mla_w_uq[0].reshape(MLA_Q_RANK, MLA_HEADS, MLA_NOPE + MLA_ROPE)
    w_uq_p = jnp.concatenate([uq[:, :, :MLA_NOPE].reshape(MLA_Q_RANK, -1), uq[:, :, MLA_NOPE:].reshape(MLA_Q_RANK, -1)],
                             axis=1).astype(bf16)
    ukv = mla_w_ukv[0].reshape(MLA_KV_RANK, MLA_HEADS, MLA_NOPE + MLA_V)
    w_ukv_p = jnp.concatenate([ukv[:, :, :MLA_NOPE].reshape(MLA_KV_RANK, -1),
                               ukv[:, :, MLA_NOPE:].reshape(MLA_KV_RANK, -1)], axis=1).astype(bf16)
    qa, ka, va, ckv, kr, qm, kvl = _ab_proj(xc, xd, mods[0], w_in_p, row(mla_q_norm[0]), row(mla_kv_norm[0]),
                                            w_uq_p, w_ukv_p)
    w_out_b = w_out_ab[0].astype(bf16)
    g0, b0 = row(ln_mix_g[0]), row(ln_mix_b[0])
    x1c = _ctx_attn(attn_sink[0], qa, ka, va, qm, kvl, kr, xc, mods[0], w_out_b, g0, b0)
    x1d = _den_attn(attn_sink[0], qa, ka, va,
                    cache_attn_k[:, 0].reshape(N_DEN_B, CTX_L, A_KV_HEADS * A_HD),
                    cache_attn_v[:, 0].reshape(N_DEN_B, CTX_L, A_KV_HEADS * A_HD),
                    qm, kvl, kr, cache_mla_ckv[:, 0], cache_mla_krope[:, 0], w_ukv_p, xd, mods[0], w_out_b, g0, b0)
    rw0, rb0 = _router_weights(0, router_w, router_bias)
    x1, h, gates = _router(x1c, x1d, mods[0], rw0, rb0)
    x2 = _moe(0, h, gates, x1, mods[0], exp_w_gate, exp_w_up, exp_w_down, sh_w_gate, sh_w_up, sh_w_down,
              row(ln_ffn_g[0]), row(ln_ffn_b[0]))

    w_in_c_b = w_in_c[0].astype(bf16)
    uc, u2c = _s5_in(x2, 0, mods[1], w_in_c_b, N_CTX_B, CTX_L, lambda i: 0)
    ud, u2d = _s5_in(x2, T_CTX // TOK_TILE, mods[1], w_in_c_b, N_DEN_B, DEN_L, lambda i: 1 + i // (DEN_L // TOK_TILE))
    gp = S5_G * S5_P
    chan_major_b = lambda t: t[0].reshape(2, gp, S5_CH)
    chan_major_c = lambda t: jnp.transpose(t[0], (0, 2, 1, 3)).reshape(2, S5_CH, gp)
    h0 = jnp.transpose(state_ssm[:, 0], (1, 2, 0, 3, 4)).reshape(2, 2, N_DEN_B, gp)
    yc, yd, st = _s5_scan(s5_lam_re[0].reshape(2, 1, gp), s5_lam_im[0].reshape(2, 1, gp),
                          jnp.repeat(s5_log_dt[0], S5_P, axis=-1).reshape(2, 1, gp),
                          chan_major_b(s5_b_re), chan_major_b(s5_b_im), chan_major_c(s5_c_re), chan_major_c(s5_c_im),
                          u2c, u2d, h0)
    rw1, rb1 = _router_weights(1, router_w, router_bias)
    x3, h, gates = _s5_out(x2, uc, ud, yc, yd, mods[1], row(s5_d[0]),
                           w_out_c[0].astype(bf16), row(ln_mix_g[1]), row(ln_mix_b[1]), rw1, rb1)
    x4 = _moe(1, h, gates, x3, mods[1], exp_w_gate, exp_w_up, exp_w_down, sh_w_gate, sh_w_up, sh_w_down,
              row(ln_ffn_g[1]), row(ln_ffn_b[1]))

    y_prompt = x4[:T_CTX].reshape(N_CTX_B, CTX_L, D)
    y_sample = x4[T_CTX:].reshape(N_DEN_B, DEN_L, D)
    new_attn_k = ka[:T_CTX].reshape(N_CTX_B, 1, CTX_L, A_KV_HEADS, A_HD)
    new_attn_v = va[:T_CTX].reshape(N_CTX_B, 1, CTX_L, A_KV_HEADS, A_HD)
    new_mla_ckv = ckv[:T_CTX].reshape(N_CTX_B, 1, CTX_L, MLA_KV_RANK)
    new_mla_krope = kr[:T_CTX, :MLA_ROPE].reshape(N_CTX_B, 1, CTX_L, MLA_ROPE)
    new_state_ssm = jnp.transpose(st, (2, 0, 1, 3)).reshape(N_CTX_B, 1, 2, 2, S5_G, S5_P)
    return (y_prompt, y_sample, new_attn_k, new_attn_v, new_mla_ckv, new_mla_krope, new_state_ssm)
```

```python
import functools

import jax
import jax.numpy as jnp
import numpy as np
from jax import lax
from jax.experimental import pallas as pl
from jax.experimental.pallas import tpu as pltpu

f32 = jnp.float32
bf16 = jnp.bfloat16

D = 1024
N_CTX_B, CTX_L = 16, 256
N_DEN_B, DEN_L = 2, 1024
T_CTX = N_CTX_B * CTX_L
T_DEN = N_DEN_B * DEN_L
T_ALL = T_CTX + T_DEN
GRID_W = 64
WINDOW = 128
ROPE_BASE = 10000.0
A_HEADS, A_KV_HEADS, A_HD = 8, 2, 64
A_GROUP = A_HEADS // A_KV_HEADS
A_SCALE = A_HD ** -0.5
MLA_HEADS, MLA_Q_RANK, MLA_KV_RANK = 8, 256, 128
MLA_NOPE, MLA_ROPE, MLA_V = 64, 32, 64
MLA_SCALE = (MLA_NOPE + MLA_ROPE) ** -0.5
N_EXPERTS, TOP_K, EXPERT_FF, SHARED_FF = 64, 6, 128, 128
ROUTED_SCALE = 2.5
DEPTH = 2
ALPHA = (2.0 * DEPTH) ** 0.25
LN_EPS = 1e-5
RMS_EPS = 1e-6
NEG_INF = -1e30
S5_G, S5_CH, S5_P = 64, 16, 64

LANE = 128
SUB = 8
VMEM_LIMIT = 56 * 1024 * 1024

TOK_TILE = 512


def _mod_row(tile_idx, tile_rows):
    start = tile_idx * tile_rows
    return jnp.where(start < T_CTX, 0, 1 + (start - T_CTX) // DEN_L)


def _layer_norm(y, g, b):
    mu = jnp.mean(y, axis=-1, keepdims=True)
    yc = y - mu
    var = jnp.mean(yc * yc, axis=-1, keepdims=True)
    return yc * lax.rsqrt(var + LN_EPS) * g + b


def _silu(x):
    return x * jax.nn.sigmoid(x)


def _dot(a, b):
    return jnp.dot(a, b, preferred_element_type=f32)


def _dot_nt(a, b):
    return lax.dot_general(a, b, (((1,), (1,)), ((), ())), preferred_element_type=f32)


def _split_bf16(a):
    hi = a.astype(bf16)
    return hi, (a - hi.astype(f32)).astype(bf16)


def _full(shape, n_grid):
    zeros = tuple(0 for _ in shape)
    return pl.BlockSpec(shape, lambda *_: zeros)


def _two_stream_specs(tile_rows, width):
    n_ctx = T_CTX // tile_rows
    return (pl.BlockSpec((tile_rows, width), lambda i: (jnp.minimum(i, n_ctx - 1), 0)),
            pl.BlockSpec((tile_rows, width), lambda i: (jnp.maximum(i - n_ctx, 0), 0)))


def _pick(i, tile_rows, ctx_ref, den_ref):
    return lax.cond(i < T_CTX // tile_rows, lambda: ctx_ref[...], lambda: den_ref[...])


def _params(n_grid):
    return pltpu.CompilerParams(dimension_semantics=("arbitrary",) * n_grid, vmem_limit_bytes=VMEM_LIMIT)


ADA_TN = 1536


def _adaln_kernel(c_ref, w_ref, b_ref, o_ref):
    s_hi, s_lo = _split_bf16(_silu(c_ref[...]))
    w_hi, w_lo = _split_bf16(w_ref[0])
    o_ref[0] = _dot(s_hi, w_hi) + (_dot(s_hi, w_lo) + _dot(s_lo, w_hi)) + b_ref[0]


def _adaln(cvec8, ada_w, ada_b):
    n = 6 * D
    return pl.pallas_call(
        _adaln_kernel,
        grid=(DEPTH, n // ADA_TN),
        in_specs=[
            pl.BlockSpec((8, D), lambda l, j: (0, 0)),
            pl.BlockSpec((1, D, ADA_TN), lambda l, j: (l, 0, j)),
            pl.BlockSpec((1, 1, ADA_TN), lambda l, j: (l, 0, j)),
        ],
        out_specs=pl.BlockSpec((1, 8, ADA_TN), lambda l, j: (l, 0, j)),
        out_shape=jax.ShapeDtypeStruct((DEPTH, 8, n), f32),
        compiler_params=_params(2),
        name="adaln",
    )(cvec8, ada_w, ada_b.reshape(DEPTH, 1, n))


def _rope_table_array(head_dim):
    q = head_dim // 4
    pos = np.arange(DEN_L)
    row, col = (pos // GRID_W).astype(np.float64), (pos % GRID_W).astype(np.float64)
    lane = np.arange(LANE) % head_dim
    is_col = lane >= head_dim // 2
    w = lane % (head_dim // 2)
    first = w < q
    inv_freq = ROPE_BASE ** (-np.arange(q, dtype=np.float64) / q)
    ang = np.where(is_col[None, :], col[:, None], row[:, None]) * inv_freq[w % q][None, :]
    cos, sin = np.cos(ang), np.sin(ang)
    sin_a = np.where(first[None, :], -sin, 0.0)
    sin_b = np.where(first[None, :], 0.0, sin)
    ident = np.stack([np.ones((TOK_TILE, LANE)), np.zeros((TOK_TILE, LANE)), np.zeros((TOK_TILE, LANE))])
    tab = np.concatenate([ident, np.stack([cos, sin_a, sin_b])], axis=1).astype(np.float32)
    return jnp.asarray(tab), q


def _rope_chunk(x, tab_ref, q):
    return x * tab_ref[0] + pltpu.roll(x, LANE - q, 1) * tab_ref[1] + pltpu.roll(x, q, 1) * tab_ref[2]


PROJ_W = 1280
C_QA, C_KA, C_VA, C_CQ, C_CKV, C_KR = 0, 512, 640, 768, 1024, 1152
MLA_NN = MLA_HEADS * MLA_NOPE


def _ab_proj_kernel(xc_ref, xd_ref, mods_ref, w_ref, qn_ref, kvn_ref, wuq_ref, wukv_ref, ta_ref, tm_ref,
                    qa_ref, ka_ref, va_ref, ckv_ref, kr_ref, qm_ref, kvl_ref, *, qa_shift, qm_shift):
    i = pl.program_id(0)
    r = _mod_row(i, TOK_TILE)
    mrow = mods_ref[pl.ds(r, 1), :]
    sh, sc = mrow[:, 0:D], mrow[:, D:2 * D]
    x = _pick(i, TOK_TILE, xc_ref, xd_ref)
    h = (x * (1.0 + sc) + sh).astype(bf16)
    proj = _dot(h, w_ref[...])
    for j in range(4):
        c0 = C_QA + LANE * j
        qa_ref[:, LANE * j:LANE * (j + 1)] = _rope_chunk(proj[:, c0:c0 + LANE], ta_ref, qa_shift).astype(bf16)
    ka_ref[...] = _rope_chunk(proj[:, C_KA:C_KA + LANE], ta_ref, qa_shift)
    va_ref[...] = proj[:, C_VA:C_VA + LANE]
    cq = proj[:, C_CQ:C_CQ + MLA_Q_RANK]
    cq = cq * lax.rsqrt(jnp.mean(cq * cq, axis=-1, keepdims=True) + RMS_EPS) * qn_ref[...]
    ckv = proj[:, C_CKV:C_CKV + MLA_KV_RANK]
    ckv = ckv * lax.rsqrt(jnp.mean(ckv * ckv, axis=-1, keepdims=True) + RMS_EPS) * kvn_ref[...]
    ckv_ref[...] = ckv
    kr_ref[...] = _rope_chunk(proj[:, C_KR:C_KR + LANE], tm_ref, qm_shift)
    qm = _dot(cq.astype(bf16), wuq_ref[...])
    qm_ref[:, 0:MLA_NN] = qm[:, 0:MLA_NN].astype(bf16)
    for j in range(2):
        c0 = MLA_NN + LANE * j
        qm_ref[:, c0:c0 + LANE] = _rope_chunk(qm[:, c0:c0 + LANE], tm_ref, qm_shift).astype(bf16)
    kvl_ref[...] = _dot(ckv.astype(bf16), wukv_ref[...]).astype(bf16)


def _rope_block_index(i):
    tiles_ctx = T_CTX // TOK_TILE
    per_seq = DEN_L // TOK_TILE
    return jnp.where(i < tiles_ctx, 0, 1 + (i - tiles_ctx) % per_seq)


def _ab_proj(xc, xd, mods0, w_in_p, q_norm, kv_norm, w_uq_p, w_ukv_p):
    tab_a, qa_shift = _rope_table_array(A_HD)
    tab_m, qm_shift = _rope_table_array(MLA_ROPE)
    row_spec = lambda w: pl.BlockSpec((TOK_TILE, w), lambda i: (i, 0))
    xc_spec, xd_spec = _two_stream_specs(TOK_TILE, D)
    tab_spec = pl.BlockSpec((3, TOK_TILE, LANE), lambda i: (0, _rope_block_index(i), 0))
    outs = [(512, bf16), (LANE, f32), (LANE, f32), (LANE, f32), (LANE, f32), (768, bf16), (1024, bf16)]
    return pl.pallas_call(
        functools.partial(_ab_proj_kernel, qa_shift=qa_shift, qm_shift=qm_shift),
        grid=(T_ALL // TOK_TILE,),
        in_specs=[xc_spec, xd_spec, _full((8, 6 * D), 1), _full((D, PROJ_W), 1), _full((1, MLA_Q_RANK), 1),
                  _full((1, MLA_KV_RANK), 1), _full((MLA_Q_RANK, 768), 1), _full((MLA_KV_RANK, 1024), 1),
                  tab_spec, tab_spec],
        out_specs=[row_spec(w) for w, _ in outs],
        out_shape=[jax.ShapeDtypeStruct((T_ALL, w), dt) for w, dt in outs],
        compiler_params=_params(1),
        name="ab_proj",
    )(xc, xd, mods0, w_in_p, q_norm, kv_norm, w_uq_p, w_ukv_p, tab_a, tab_m)


def _softmax_blocks(s_refs, p_refs, sink_col=None):
    m = s_refs[0][...].max(axis=-1, keepdims=True)
    for s_ref in s_refs[1:]:
        m = jnp.maximum(m, s_ref[...].max(axis=-1, keepdims=True))
    if sink_col is not None:
        m = jnp.maximum(m, sink_col)
    l = None
    for s_ref, p_ref in zip(s_refs, p_refs):
        p = jnp.exp(s_ref[...] - m)
        p_ref[...] = p.astype(bf16)
        ps = p.sum(axis=-1, keepdims=True)
        l = ps if l is None else l + ps
    if sink_col is not None:
        l = l + jnp.exp(sink_col - m)
    return 1.0 / l


def _sink_column(sink_ref, rows_per_head):
    return jnp.concatenate([jnp.full((rows_per_head, 1), sink_ref[h], f32) for h in range(A_HEADS)], axis=0)


def _mla_q(qm_ref, h):
    rows = qm_ref.shape[0]
    return jnp.concatenate([qm_ref[:, MLA_NOPE * h:MLA_NOPE * (h + 1)],
                            qm_ref[:, MLA_NN + MLA_ROPE * h:MLA_NN + MLA_ROPE * (h + 1)],
                            jnp.zeros((rows, LANE - MLA_NOPE - MLA_ROPE), bf16)], axis=1)


def _mla_k(k_nope_h, k_rope):
    rows = k_nope_h.shape[0]
    return jnp.concatenate([k_nope_h, k_rope, jnp.zeros((rows, LANE - MLA_NOPE - MLA_ROPE), bf16)], axis=1)


def _mix_out_ln(merged_ref, wout_ref, x, mods_ref, r, g_ref, b_ref):
    out = _dot(merged_ref[...], wout_ref[...])
    gate = mods_ref[pl.ds(r, 1), 2 * D:3 * D]
    return _layer_norm(ALPHA * x + gate * out, g_ref[...], b_ref[...])


def _ctx_attn_kernel(sink_ref, qa_ref, ka_ref, va_ref, qm_ref, kvl_ref, kr_ref, x_ref, mods_ref, wout_ref,
                     g_ref, b_ref, o_ref, merged_ref, sa_ref, sm_ref, pa_ref, pm_ref):
    n = CTX_L
    ka = ka_ref[...].astype(bf16)
    va = va_ref[...].astype(bf16)
    for j in range(A_KV_HEADS):
        q4 = jnp.concatenate([qa_ref[:, A_HD * h:A_HD * (h + 1)] for h in range(A_GROUP * j, A_GROUP * (j + 1))],
                             axis=0)
        sa_ref[A_GROUP * n * j:A_GROUP * n * (j + 1), :] = _dot_nt(q4, ka[:, A_HD * j:A_HD * (j + 1)]) * A_SCALE
    kr = kr_ref[:, 0:MLA_ROPE].astype(bf16)
    for h in range(MLA_HEADS):
        k_cat = _mla_k(kvl_ref[:, MLA_NOPE * h:MLA_NOPE * (h + 1)], kr)
        sm_ref[n * h:n * (h + 1), :] = _dot_nt(_mla_q(qm_ref, h), k_cat) * MLA_SCALE
    rla = _softmax_blocks([sa_ref], [pa_ref], _sink_column(sink_ref, n))
    rlm = _softmax_blocks([sm_ref], [pm_ref])
    for j in range(A_KV_HEADS):
        rows = slice(A_GROUP * n * j, A_GROUP * n * (j + 1))
        o4 = _dot(pa_ref[rows, :], va[:, A_HD * j:A_HD * (j + 1)]) * rla[rows]
        for g in range(A_GROUP):
            h = A_GROUP * j + g
            merged_ref[:, A_HD * h:A_HD * (h + 1)] = o4[n * g:n * (g + 1)].astype(bf16)
    for h in range(MLA_HEADS):
        rows = slice(n * h, n * (h + 1))
        v = kvl_ref[:, MLA_NN + MLA_V * h:MLA_NN + MLA_V * (h + 1)]
        merged_ref[:, MLA_NN + MLA_V * h:MLA_NN + MLA_V * (h + 1)] = (_dot(pm_ref[rows, :], v) * rlm[rows]).astype(bf16)
    o_ref[...] = _mix_out_ln(merged_ref, wout_ref, x_ref[...], mods_ref, 0, g_ref, b_ref)


def _ctx_attn(sink, qa, ka, va, qm, kvl, kr, x_all, mods0, w_out, ln_g, ln_b):
    blk = lambda w: pl.BlockSpec((CTX_L, w), lambda b: (b, 0))
    return pl.pallas_call(
        _ctx_attn_kernel,
        grid=(N_CTX_B,),
        in_specs=[pl.BlockSpec(memory_space=pltpu.SMEM), blk(512), blk(LANE), blk(LANE), blk(768), blk(1024),
                  blk(LANE), blk(D), _full((8, 6 * D), 1), _full((D, D), 1), _full((1, D), 1), _full((1, D), 1)],
        out_specs=blk(D),
        out_shape=jax.ShapeDtypeStruct((T_CTX, D), f32),
        scratch_shapes=[pltpu.VMEM((CTX_L, D), bf16),
                        pltpu.VMEM((A_HEADS * CTX_L, CTX_L), f32), pltpu.VMEM((MLA_HEADS * CTX_L, CTX_L), f32),
                        pltpu.VMEM((A_HEADS * CTX_L, CTX_L), bf16), pltpu.VMEM((MLA_HEADS * CTX_L, CTX_L), bf16)],
        compiler_params=_params(1),
        name="ctx_attn",
    )(sink, qa, ka, va, qm, kvl, kr, x_all, mods0, w_out, ln_g, ln_b)


QB = 256
WIN = QB + 2 * WINDOW
DEN_BLK0 = T_CTX // DEN_L
MLA_KEYS = CTX_L + DEN_L


def _den_attn_kernel(sink_ref, qa_ref, ka_ref, va_ref, cak_ref, cav_ref, qm_ref, kvl_ref, kr_ref, cckv_ref, ckr_ref,
                     wukv_ref, x_ref, mods_ref, wout_ref, g_ref, b_ref, o_ref, merged_ref, kcat_ref, vcat_ref,
                     saw_ref, sac_ref, sm_ref, paw_ref, pac_ref, pm_ref):
    b = pl.program_id(0)
    n = pl.program_id(1)

    @pl.when(n == 0)
    def _():
        kvc = _dot(cckv_ref[0].astype(bf16), wukv_ref[...]).astype(bf16)
        kr_ctx = ckr_ref[0].astype(bf16)
        kr_lat = kr_ref[:, 0:MLA_ROPE].astype(bf16)
        for h in range(MLA_HEADS):
            ns = slice(MLA_NOPE * h, MLA_NOPE * (h + 1))
            kcat_ref[h, 0:CTX_L, :] = _mla_k(kvc[:, ns], kr_ctx)
            kcat_ref[h, CTX_L:MLA_KEYS, :] = _mla_k(kvl_ref[:, ns], kr_lat)
        vcat_ref[0:CTX_L, :] = kvc[:, MLA_NN:2 * MLA_NN]
        vcat_ref[CTX_L:MLA_KEYS, :] = kvl_ref[:, MLA_NN:2 * MLA_NN]

    start = pl.multiple_of(jnp.clip(QB * n - WINDOW, 0, DEN_L - WIN), WINDOW)
    grp_rows = A_GROUP * QB
    qpos = QB * n + (lax.broadcasted_iota(jnp.int32, (grp_rows, WIN), 0) & (QB - 1))
    kpos = start + lax.broadcasted_iota(jnp.int32, (grp_rows, WIN), 1)
    valid = jnp.abs(qpos - kpos) <= WINDOW
    kwin = ka_ref[pl.ds(start, WIN), :].astype(bf16)
    vwin = va_ref[pl.ds(start, WIN), :].astype(bf16)
    kctx = [cak_ref[0, :, j, :].astype(bf16) for j in range(A_KV_HEADS)]
    vctx = [cav_ref[0, :, j, :].astype(bf16) for j in range(A_KV_HEADS)]
    for j in range(A_KV_HEADS):
        sl = slice(A_HD * j, A_HD * (j + 1))
        rows = slice(grp_rows * j, grp_rows * (j + 1))
        q4 = jnp.concatenate([qa_ref[:, A_HD * h:A_HD * (h + 1)] for h in range(A_GROUP * j, A_GROUP * (j + 1))],
                             axis=0)
        saw_ref[rows, :] = jnp.where(valid, _dot_nt(q4, kwin[:, sl]) * A_SCALE, NEG_INF)
        sac_ref[rows, :] = _dot_nt(q4, kctx[j]) * A_SCALE
    for h in range(MLA_HEADS):
        sm_ref[QB * h:QB * (h + 1), :] = _dot_nt(_mla_q(qm_ref, h), kcat_ref[h]) * MLA_SCALE
    rla = _softmax_blocks([saw_ref, sac_ref], [paw_ref, pac_ref], _sink_column(sink_ref, QB))
    rlm = _softmax_blocks([sm_ref], [pm_ref])
    for j in range(A_KV_HEADS):
        sl = slice(A_HD * j, A_HD * (j + 1))
        rows = slice(grp_rows * j, grp_rows * (j + 1))
        o4 = (_dot(paw_ref[rows, :], vwin[:, sl]) + _dot(pac_ref[rows, :], vctx[j])) * rla[rows]
        for g in range(A_GROUP):
            h = A_GROUP * j + g
            merged_ref[:, A_HD * h:A_HD * (h + 1)] = o4[QB * g:QB * (g + 1)].astype(bf16)
    for h in range(MLA_HEADS):
        rows = slice(QB * h, QB * (h + 1))
        o = _dot(pm_ref[rows, :], vcat_ref[:, MLA_V * h:MLA_V * (h + 1)]) * rlm[rows]
        merged_ref[:, MLA_NN + MLA_V * h:MLA_NN + MLA_V * (h + 1)] = o.astype(bf16)
    o_ref[...] = _mix_out_ln(merged_ref, wout_ref, x_ref[...], mods_ref, 1 + b, g_ref, b_ref)


def _den_attn(sink, qa, ka, va, cache_k, cache_v, qm, kvl, kr, cache_ckv, cache_kr, w_ukv_p, x_all, mods0, w_out,
              ln_g, ln_b):
    nq = DEN_L // QB
    qblk = lambda w: pl.BlockSpec((QB, w), lambda b, n: (T_CTX // QB + b * nq + n, 0))
    seq = lambda w: pl.BlockSpec((DEN_L, w), lambda b, n: (DEN_BLK0 + b, 0))
    cache = lambda w: pl.BlockSpec((1, CTX_L, w), lambda b, n: (b, 0, 0))
    cache_a = pl.BlockSpec((1, CTX_L, A_KV_HEADS, A_HD), lambda b, n: (b, 0, 0, 0))
    return pl.pallas_call(
        _den_attn_kernel,
        grid=(N_DEN_B, nq),
        in_specs=[pl.BlockSpec(memory_space=pltpu.SMEM), qblk(512), seq(LANE), seq(LANE), cache_a, cache_a,
                  qblk(768), seq(1024), seq(LANE), cache(MLA_KV_RANK), cache(MLA_ROPE),
                  _full((MLA_KV_RANK, 1024), 2), pl.BlockSpec((QB, D), lambda b, n: (b * nq + n, 0)),
                  _full((8, 6 * D), 2), _full((D, D), 2), _full((1, D), 2),
                  _full((1, D), 2)],
        out_specs=pl.BlockSpec((QB, D), lambda b, n: (b * nq + n, 0)),
        out_shape=jax.ShapeDtypeStruct((T_DEN, D), f32),
        scratch_shapes=[pltpu.VMEM((QB, D), bf16), pltpu.VMEM((MLA_HEADS, MLA_KEYS, LANE), bf16),
                        pltpu.VMEM((MLA_KEYS, MLA_NN), bf16)]
        + [pltpu.VMEM((A_HEADS * QB, w), dt) for dt in (f32, bf16) for w in (WIN, CTX_L, MLA_KEYS)],
        compiler_params=_params(2),
        name="den_attn",
    )(sink, qa, ka, va, cache_k, cache_v, qm, kvl, kr, cache_ckv, cache_kr, w_ukv_p, x_all, mods0, w_out, ln_g, ln_b)


SUBTILE = 256


def _route(x1, mrow, rw_ref, rb_ref):
    sh, sc = mrow[:, 3 * D:4 * D], mrow[:, 4 * D:5 * D]
    h = x1 * (1.0 + sc) + sh
    h_hi = h.astype(bf16)
    h_lo = (h - h_hi.astype(f32)).astype(bf16)
    logits = _dot(h_hi, rw_ref[0]) + (_dot(h_hi, rw_ref[1]) + _dot(h_lo, rw_ref[0]))
    scores = jax.nn.sigmoid(logits)
    lane = lax.broadcasted_iota(jnp.int32, scores.shape, 1).astype(f32)
    sel = jnp.where(lane < N_EXPERTS, scores + rb_ref[...], -jnp.inf)
    gates = jnp.zeros_like(scores)
    for _ in range(TOP_K):
        m = sel.max(axis=-1, keepdims=True)
        idx = jnp.where(sel == m, lane, float(LANE)).min(axis=-1, keepdims=True)
        hit = lane == idx
        gates = jnp.where(hit, scores, gates)
        sel = jnp.where(hit, -jnp.inf, sel)
    return h_hi, gates / gates.sum(axis=-1, keepdims=True) * ROUTED_SCALE


def _router_kernel(xc_ref, xd_ref, mods_ref, rw_ref, rb_ref, x_ref, h_ref, gates_ref):
    i = pl.program_id(0)
    r = _mod_row(i, TOK_TILE)
    mrow = mods_ref[pl.ds(r, 1), :]
    x1 = _pick(i, TOK_TILE, xc_ref, xd_ref)
    x_ref[...] = x1
    h_ref[...], gates_ref[...] = _route(x1, mrow, rw_ref, rb_ref)


def _router(x1c, x1d, mods_l, router_w_p, router_b_p):
    row_spec = lambda w: pl.BlockSpec((TOK_TILE, w), lambda i: (i, 0))
    xc_spec, xd_spec = _two_stream_specs(TOK_TILE, D)
    return pl.pallas_call(
        _router_kernel,
        grid=(T_ALL // TOK_TILE,),
        in_specs=[xc_spec, xd_spec, _full((8, 6 * D), 1), _full((2, D, LANE), 1), _full((1, LANE), 1)],
        out_specs=[row_spec(D), row_spec(D), row_spec(LANE)],
        out_shape=[jax.ShapeDtypeStruct((T_ALL, D), f32), jax.ShapeDtypeStruct((T_ALL, D), bf16),
                   jax.ShapeDtypeStruct((T_ALL, LANE), f32)],
        compiler_params=_params(1),
        name="router",
    )(x1c, x1d, mods_l, router_w_p, router_b_p)


MOE_TOK = 1536
MOE_EG = 4
MOE_TILE = 512
MOE_FF = MOE_EG * EXPERT_FF


def _moe_kernel(h_ref, gates_ref, x_ref, mods_ref, wg_ref, wu_ref, wd_ref, sg_ref, su_ref, sd_ref, g_ref, b_ref,
                o_ref):
    p = pl.program_id(0)
    e = pl.program_id(1)
    n_tiles = MOE_TOK // MOE_TILE

    def gate_f(t):
        r = _mod_row(p * n_tiles + t, MOE_TILE)
        return mods_ref[pl.ds(r, 1), 5 * D:6 * D]

    def rows_of(t):
        if isinstance(t, int):
            return pl.ds(t * MOE_TILE, MOE_TILE)
        return pl.ds(pl.multiple_of(t * MOE_TILE, MOE_TILE), MOE_TILE)

    @pl.when(e == 0)
    def _():
        sg = sg_ref[...].astype(bf16)
        su = su_ref[...].astype(bf16)
        sd = sd_ref[...].astype(bf16)

        def body(t, c):
            rows = rows_of(t)
            ht = h_ref[rows, :]
            hid = _silu(_dot(ht, sg)) * _dot(ht, su)
            o_ref[rows, :] = ALPHA * x_ref[rows, :] + gate_f(t) * _dot(hid.astype(bf16), sd)
            return c

        lax.fori_loop(0, n_tiles, body, 0)

    wg = jnp.concatenate([wg_ref[k].astype(bf16) for k in range(MOE_EG)], axis=1)
    wu = jnp.concatenate([wu_ref[k].astype(bf16) for k in range(MOE_EG)], axis=1)
    wd = jnp.concatenate([wd_ref[k].astype(bf16) for k in range(MOE_EG)], axis=0)
    lane = lax.broadcasted_iota(jnp.int32, (MOE_TILE, LANE), 1)

    def body(t, c):
        rows = rows_of(t)
        ht = h_ref[rows, :]
        hid = _silu(_dot(ht, wg)) * _dot(ht, wu)
        gt = gates_ref[rows, :]
        parts = []
        for k in range(MOE_EG):
            col = jnp.where(lane == e * MOE_EG + k, gt, 0.0).sum(axis=-1, keepdims=True)
            parts.append((hid[:, EXPERT_FF * k:EXPERT_FF * (k + 1)] * col).astype(bf16))
        o_ref[rows, :] += gate_f(t) * _dot(jnp.concatenate(parts, axis=1), wd)
        return c

    for t in range(n_tiles):
        body(t, 0)

    @pl.when(e == pl.num_programs(1) - 1)
    def _():
        def body(t, c):
            rows = rows_of(t)
            o_ref[rows, :] = _layer_norm(o_ref[rows, :], g_ref[...], b_ref[...])
            return c

        lax.fori_loop(0, n_tiles, body, 0)


def _moe(l, h, gates, x1, mods_l, wg, wu, wd, sg, su, sd, ln_g, ln_b):
    tok = lambda w: pl.BlockSpec((MOE_TOK, w), lambda p, e: (p, 0))
    return pl.pallas_call(
        _moe_kernel,
        grid=(T_ALL // MOE_TOK, N_EXPERTS // MOE_EG),
        in_specs=[tok(D), tok(LANE), tok(D), _full((8, 6 * D), 2),
                  pl.BlockSpec((None, MOE_EG, D, EXPERT_FF), lambda p, e: (l, e, 0, 0)),
                  pl.BlockSpec((None, MOE_EG, D, EXPERT_FF), lambda p, e: (l, e, 0, 0)),
                  pl.BlockSpec((None, MOE_EG, EXPERT_FF, D), lambda p, e: (l, e, 0, 0)),
                  pl.BlockSpec((None, D, SHARED_FF), lambda p, e: (l, 0, 0)),
                  pl.BlockSpec((None, D, SHARED_FF), lambda p, e: (l, 0, 0)),
                  pl.BlockSpec((None, SHARED_FF, D), lambda p, e: (l, 0, 0)),
                  _full((1, D), 2), _full((1, D), 2)],
        out_specs=tok(D),
        out_shape=jax.ShapeDtypeStruct((T_ALL, D), f32),
        compiler_params=_params(2),
        name="moe",
    )(h, gates, x1, mods_l, wg, wu, wd, sg, su, sd, ln_g, ln_b)


def _router_weights(l, router_w, router_bias):
    rw = jnp.pad(router_w[l], ((0, 0), (0, LANE - N_EXPERTS)))
    rw_hi, rw_lo = _split_bf16(rw)
    rb = jnp.pad(router_bias[l], (0, LANE - N_EXPERTS)).reshape(1, LANE)
    return jnp.stack([rw_hi, rw_lo]), rb


S5_Q = 8
S5_NGB = D // LANE


def _s5_in_kernel(x_ref, mods_ref, w_ref, u_ref, u2_ref, slab_ref, *, row_of, seq_len):
    r = row_of(pl.program_id(0))
    mrow = mods_ref[pl.ds(r, 1), :]
    sh, sc = mrow[:, 0:D], mrow[:, D:2 * D]
    h = (x_ref[...] * (1.0 + sc) + sh).astype(bf16)
    u = _dot(h, w_ref[...])
    u_ref[...] = u
    for s in range(S5_NGB):
        slab_ref[s] = u[:, LANE * s:LANE * (s + 1)]
    kt = seq_len // S5_Q
    for s in range(S5_NGB):
        for q in range(TOK_TILE // seq_len):
            for j in range(S5_Q):
                u2_ref[s, q * kt:(q + 1) * kt, LANE * j:LANE * (j + 1)] = (
                    slab_ref[s, pl.ds(q * seq_len + j, kt, stride=S5_Q), :].astype(bf16))


def _s5_in(x_all, tile0, mods1, w_in_c, n_b, seq_len, row_of):
    n_tiles = n_b * seq_len // TOK_TILE
    chunks = TOK_TILE // S5_Q
    return pl.pallas_call(
        functools.partial(_s5_in_kernel, row_of=row_of, seq_len=min(seq_len, TOK_TILE)),
        grid=(n_tiles,),
        in_specs=[pl.BlockSpec((TOK_TILE, D), lambda i: (tile0 + i, 0)), _full((8, 6 * D), 1), _full((D, D), 1)],
        out_specs=[pl.BlockSpec((TOK_TILE, D), lambda i: (i, 0)),
                   pl.BlockSpec((S5_NGB, chunks, D), lambda i: (0, i, 0))],
        out_shape=[jax.ShapeDtypeStruct((n_b * seq_len, D), f32),
                   jax.ShapeDtypeStruct((S5_NGB, n_tiles * chunks, D), bf16)],
        scratch_shapes=[pltpu.VMEM((S5_NGB, TOK_TILE, LANE), f32)],
        compiler_params=_params(1),
        name="s5_in",
    )(x_all, mods1, w_in_c)


S5_GL = (LANE // S5_CH) * S5_P
S5_ROWS_C = (CTX_L // S5_Q) * N_CTX_B
S5_ROWS_D = (DEN_L // S5_Q) * N_DEN_B


def _s5_scan_kernel(lre_ref, lim_ref, ldt_ref, btr_ref, bti_ref, ctr_ref, cti_ref, uc_ref, ud_ref, h0_ref,
                    yc_ref, yd_ref, st_ref, win_ref, mso_ref, wit_ref, a_ref, s_ref, hp_ref):
    gl = S5_GL
    rowg = lax.shift_right_logical(lax.broadcasted_iota(jnp.int32, (LANE, gl), 0), 4)
    colg = lax.shift_right_logical(lax.broadcasted_iota(jnp.int32, (LANE, gl), 1), 6)
    same_group = rowg == colg
    reps = LANE // S5_CH

    def expand(t):
        return jnp.where(same_group, jnp.concatenate([t] * reps, axis=0), 0.0)

    def expand_c(t):
        return jnp.where(same_group, jnp.concatenate([t] * reps, axis=1), 0.0)

    for d in range(2):
        fwd = d == 0
        lre, lim = lre_ref[d], lim_ref[d]
        dt = jnp.exp(ldt_ref[d])
        a, w = lre * dt, lim * dt
        pre = [jnp.exp(m * a) * jnp.cos(m * w) for m in range(S5_Q + 1)]
        pim = [jnp.exp(m * a) * jnp.sin(m * w) for m in range(S5_Q + 1)]
        xr, xi = pre[1] - 1.0, pim[1]
        den = lre * lre + lim * lim
        cfr, cfi = (xr * lre + xi * lim) / den, (xi * lre - xr * lim) / den
        btr, bti = btr_ref[d], bti_ref[d]
        bexp_r = expand(cfr * btr - cfi * bti)
        bexp_i = expand(cfr * bti + cfi * btr)
        cexp_r, cexp_i = expand_c(ctr_ref[d]), expand_c(cti_ref[d])
        for m in range(S5_Q + 1):
            a_ref[m, :, 0:gl] = cexp_r * pre[m] - cexp_i * pim[m]
            a_ref[m, :, gl:2 * gl] = -(cexp_r * pim[m] + cexp_i * pre[m])
        for j in range(S5_Q):
            m = S5_Q - 1 - j if fwd else j
            win_ref[LANE * j:LANE * (j + 1), 0:gl] = (pre[m] * bexp_r - pim[m] * bexp_i).astype(bf16)
            win_ref[LANE * j:LANE * (j + 1), gl:2 * gl] = (pre[m] * bexp_i + pim[m] * bexp_r).astype(bf16)
        for j in range(S5_Q):
            m = j + 1 if fwd else S5_Q - j
            mso_ref[LANE * j:LANE * (j + 1), :] = a_ref[m].astype(bf16)
        b2_hi, b2_lo = _split_bf16(jnp.concatenate([bexp_r, bexp_i], axis=1))
        kt = []
        for tau in range(S5_Q):
            a_hi, a_lo = _split_bf16(a_ref[tau])
            kt.append(_dot_nt(b2_hi, a_hi) + _dot_nt(b2_hi, a_lo) + _dot_nt(b2_lo, a_hi))
        for j in range(S5_Q):
            for jp in range(S5_Q):
                tau = jp - j if fwd else j - jp
                blk = slice(LANE * j, LANE * (j + 1)), slice(LANE * jp, LANE * (jp + 1))
                if fwd:
                    wit_ref[blk] = kt[tau] if tau >= 0 else jnp.zeros((LANE, LANE), f32)
                elif tau >= 0:
                    wit_ref[blk] = wit_ref[blk] + kt[tau]

        l8r, l8i = pre[S5_Q], pim[S5_Q]

        nsl = gl // LANE

        def slabs(ref, rs, first):
            return jnp.concatenate([ref[first + sl, rs, :] for sl in range(nsl)], axis=1)

        def put_slabs(ref, rs, first, val):
            for sl in range(nsl):
                ref[first + sl, rs, :] = val[:, LANE * sl:LANE * (sl + 1)]

        def advance(hr, hi_, sr, si):
            return l8r * hr - l8i * hi_ + sr, l8r * hi_ + l8i * hr + si

        def run(u_ref, y_ref, n_b, n_k, h_init):
            rows = n_b * n_k
            s = _dot(u_ref[0], win_ref[...])
            if n_b % SUB == 0:
                pitch = n_k + 1
                for bb in range(n_b):
                    dst = slice(bb * pitch, bb * pitch + n_k)
                    put_slabs(s_ref, dst, 0, s[bb * n_k:(bb + 1) * n_k, 0:gl])
                    put_slabs(s_ref, dst, nsl, s[bb * n_k:(bb + 1) * n_k, gl:2 * gl])

                def step(i, carry):
                    hr, hi_ = carry
                    rs = pl.ds(i if fwd else n_k - 1 - i, n_b, stride=pitch)
                    put_slabs(hp_ref, rs, 0, hr)
                    put_slabs(hp_ref, rs, nsl, hi_)
                    return advance(hr, hi_, slabs(s_ref, rs, 0), slabs(s_ref, rs, nsl))

                h_fin = lax.fori_loop(0, n_k, step, h_init)
                hp = jnp.concatenate(
                    [jnp.concatenate([hp_ref[sl, bb * pitch:bb * pitch + n_k, :] for sl in range(2 * nsl)], axis=1)
                     for bb in range(n_b)], axis=0).astype(bf16)
            else:
                put_slabs(s_ref, slice(0, rows), 0, s[:, 0:gl])
                put_slabs(s_ref, slice(0, rows), nsl, s[:, gl:2 * gl])
                n_it = n_k // SUB

                def step(i, carry):
                    it = i if fwd else n_it - 1 - i
                    out = []
                    for bb in range(n_b):
                        hr, hi_ = carry[bb]
                        rs = pl.ds(pl.multiple_of(bb * n_k + it * SUB, SUB), SUB)
                        s_re, s_im = slabs(s_ref, rs, 0), slabs(s_ref, rs, nsl)
                        prev_r, prev_i = [None] * SUB, [None] * SUB
                        for sub in (range(SUB) if fwd else reversed(range(SUB))):
                            prev_r[sub], prev_i[sub] = hr, hi_
                            hr, hi_ = advance(hr, hi_, s_re[sub:sub + 1], s_im[sub:sub + 1])
                        put_slabs(hp_ref, rs, 0, jnp.concatenate(prev_r, axis=0))
                        put_slabs(hp_ref, rs, nsl, jnp.concatenate(prev_i, axis=0))
                        out.append((hr, hi_))
                    return tuple(out)

                fin = lax.fori_loop(0, n_it, step, tuple((h_init[0][bb:bb + 1], h_init[1][bb:bb + 1])
                                                          for bb in range(n_b)))
                h_fin = (jnp.concatenate([f[0] for f in fin], axis=0), jnp.concatenate([f[1] for f in fin], axis=0))
                hp = jnp.concatenate([hp_ref[sl, 0:rows, :] for sl in range(2 * nsl)], axis=1).astype(bf16)
            y = _dot_nt(hp, mso_ref[...])
            if fwd:
                y_ref[0] = y
            else:
                y_ref[0] += y
            return h_fin

        zeros = jnp.zeros((N_CTX_B, gl), f32)
        hr, hi_ = run(uc_ref, yc_ref, N_CTX_B, CTX_L // S5_Q, (zeros, zeros))
        st_ref[d, 0] = hr
        st_ref[d, 1] = hi_
        run(ud_ref, yd_ref, N_DEN_B, DEN_L // S5_Q, (h0_ref[d, 0], h0_ref[d, 1]))

    wit = wit_ref[...].astype(bf16)
    yc_ref[0] += _dot(uc_ref[0], wit)
    yd_ref[0] += _dot(ud_ref[0], wit)


def _s5_scan(lam_re, lam_im, log_dt, bt_re, bt_im, ct_re, ct_im, u2c, u2d, h0):
    gl = S5_GL
    vec = pl.BlockSpec((2, 1, gl), lambda g: (0, 0, g))
    tab = pl.BlockSpec((2, S5_CH, gl), lambda g: (0, 0, g))
    ctab = pl.BlockSpec((2, LANE, S5_P), lambda g: (0, g, 0))
    rows = lambda n: pl.BlockSpec((1, n, D), lambda g: (g, 0, 0))
    return pl.pallas_call(
        _s5_scan_kernel,
        grid=(S5_NGB,),
        in_specs=[vec, vec, vec, tab, tab, ctab, ctab, rows(S5_ROWS_C), rows(S5_ROWS_D),
                  pl.BlockSpec((2, 2, N_DEN_B, gl), lambda g: (0, 0, 0, g))],
        out_specs=[rows(S5_ROWS_C), rows(S5_ROWS_D), pl.BlockSpec((2, 2, N_CTX_B, gl), lambda g: (0, 0, 0, g))],
        out_shape=[jax.ShapeDtypeStruct((S5_NGB, S5_ROWS_C, D), f32), jax.ShapeDtypeStruct((S5_NGB, S5_ROWS_D, D), f32),
                   jax.ShapeDtypeStruct((2, 2, N_CTX_B, S5_G * S5_P), f32)],
        scratch_shapes=[pltpu.VMEM((D, 2 * gl), bf16), pltpu.VMEM((D, 2 * gl), bf16), pltpu.VMEM((D, D), f32),
                        pltpu.VMEM((S5_Q + 1, LANE, 2 * gl), f32),
                        pltpu.VMEM((2 * gl // LANE, S5_ROWS_C + 2 * N_CTX_B, LANE), f32),
                        pltpu.VMEM((2 * gl // LANE, S5_ROWS_C + 2 * N_CTX_B, LANE), f32)],
        compiler_params=_params(1),
        name="s5_scan",
    )(lam_re, lam_im, log_dt, bt_re, bt_im, ct_re, ct_im, u2c, u2d, h0)


def _gelu_tanh(x):
    return 0.5 * x * (1.0 + jnp.tanh(np.sqrt(2.0 / np.pi).astype(np.float32) * (x + 0.044715 * (x * x * x))))


def _s5_out_kernel(x_ref, uc_ref, ud_ref, yc_ref, yd_ref, mods_ref, dsk_ref, wout_ref, g_ref, b_ref, rw_ref, rb_ref,
                   x1_ref, h_ref, gates_ref, slab_ref):
    i = pl.program_id(0)
    is_ctx = i < T_CTX // TOK_TILE
    r = _mod_row(i, TOK_TILE)
    mrow = mods_ref[pl.ds(r, 1), :]
    u = _pick(i, TOK_TILE, uc_ref, ud_ref)

    def unchunk(y_ref, seq_len):
        kt = seq_len // S5_Q
        for s in range(S5_NGB):
            for q in range(TOK_TILE // seq_len):
                for j in range(S5_Q):
                    slab_ref[s, pl.ds(q * seq_len + j, kt, stride=S5_Q), :] = (
                        y_ref[s, q * kt:(q + 1) * kt, LANE * j:LANE * (j + 1)])

    lax.cond(is_ctx, lambda: unchunk(yc_ref, min(CTX_L, TOK_TILE)), lambda: unchunk(yd_ref, min(DEN_L, TOK_TILE)))
    for rows in (slice(a, a + SUBTILE) for a in range(0, TOK_TILE, SUBTILE)):
        y = jnp.concatenate([slab_ref[s, rows, :] for s in range(S5_NGB)], axis=1) + dsk_ref[...] * u[rows]
        z = _dot(_gelu_tanh(y).astype(bf16), wout_ref[...])
        out = z[:, 0:D] * jax.nn.sigmoid(z[:, D:2 * D])
        x1 = _layer_norm(ALPHA * x_ref[rows, :] + mrow[:, 2 * D:3 * D] * out, g_ref[...], b_ref[...])
        x1_ref[rows, :] = x1
        h_ref[rows, :], gates_ref[rows, :] = _route(x1, mrow, rw_ref, rb_ref)


def _s5_out(x_all, uc, ud, yc, yd, mods1, d_skip, w_out_c, ln_g, ln_b, rw, rb):
    row_spec = lambda w: pl.BlockSpec((TOK_TILE, w), lambda i: (i, 0))
    uc_spec, ud_spec = _two_stream_specs(TOK_TILE, D)
    n_ctx = T_CTX // TOK_TILE
    chunks = TOK_TILE // S5_Q
    return pl.pallas_call(
        _s5_out_kernel,
        grid=(T_ALL // TOK_TILE,),
        in_specs=[row_spec(D), uc_spec, ud_spec,
                  pl.BlockSpec((S5_NGB, chunks, D), lambda i: (0, jnp.minimum(i, n_ctx - 1), 0)),
                  pl.BlockSpec((S5_NGB, chunks, D), lambda i: (0, jnp.maximum(i - n_ctx, 0), 0)),
                  _full((8, 6 * D), 1), _full((1, D), 1), _full((D, 2 * D), 1), _full((1, D), 1), _full((1, D), 1),
                  _full((2, D, LANE), 1), _full((1, LANE), 1)],
        out_specs=[row_spec(D), row_spec(D), row_spec(LANE)],
        out_shape=[jax.ShapeDtypeStruct((T_ALL, D), f32), jax.ShapeDtypeStruct((T_ALL, D), bf16),
                   jax.ShapeDtypeStruct((T_ALL, LANE), f32)],
        scratch_shapes=[pltpu.VMEM((S5_NGB, TOK_TILE, LANE), f32)],
        compiler_params=_params(1),
        name="s5_out",
    )(x_all, uc, ud, yc, yd, mods1, d_skip, w_out_c, ln_g, ln_b, rw, rb)


def kernel(x_prompt, x_sample, c, cache_attn_k, cache_attn_v, cache_mla_ckv, cache_mla_krope, state_ssm, c_ctx,
           ada_w, ada_b, ln_mix_g, ln_mix_b, ln_ffn_g, ln_ffn_b, w_in_ab, attn_sink, mla_q_norm, mla_kv_norm,
           mla_w_uq, mla_w_ukv, w_out_ab, w_in_c, s5_lam_re, s5_lam_im, s5_log_dt, s5_b_re, s5_b_im, s5_c_re,
           s5_c_im, s5_d, w_out_c, router_w, router_bias, exp_w_gate, exp_w_up, exp_w_down, sh_w_gate, sh_w_up,
           sh_w_down):
    row = lambda v: v.reshape(1, -1)
    xc, xd = x_prompt.reshape(T_CTX, D), x_sample.reshape(T_DEN, D)
    cvec8 = jnp.concatenate([c_ctx[None, :], c, jnp.zeros((8 - 1 - N_DEN_B, D), f32)], axis=0)
    mods = _adaln(cvec8, ada_w, ada_b)

    w_in_p = jnp.pad(w_in_ab[0], ((0, 0), (0, PROJ_W - w_in_ab.shape[-1]))).astype(bf16)
    uq = mla_w_uq[0].reshape(MLA_Q_RANK, MLA_HEADS, MLA_NOPE + MLA_ROPE)
    w_uq_p = jnp.concatenate([uq[:, :, :MLA_NOPE].reshape(MLA_Q_RANK, -1), uq[:, :, MLA_NOPE:].reshape(MLA_Q_RANK, -1)],
                             axis=1).astype(bf16)
    ukv = mla_w_ukv[0].reshape(MLA_KV_RANK, MLA_HEADS, MLA_NOPE + MLA_V)
    w_ukv_p = jnp.concatenate([ukv[:, :, :MLA_NOPE].reshape(MLA_KV_RANK, -1),
                               ukv[:, :, MLA_NOPE:].reshape(MLA_KV_RANK, -1)], axis=1).astype(bf16)
    qa, ka, va, ckv, kr, qm, kvl = _ab_proj(xc, xd, mods[0], w_in_p, row(mla_q_norm[0]), row(mla_kv_norm[0]),
                                            w_uq_p, w_ukv_p)
    w_out_b = w_out_ab[0].astype(bf16)
    g0, b0 = row(ln_mix_g[0]), row(ln_mix_b[0])
    x1c = _ctx_attn(attn_sink[0], qa, ka, va, qm, kvl, kr, xc, mods[0], w_out_b, g0, b0)
    x1d = _den_attn(attn_sink[0], qa, ka, va,
                    cache_attn_k[:, 0], cache_attn_v[:, 0],
                    qm, kvl, kr, cache_mla_ckv[:, 0], cache_mla_krope[:, 0], w_ukv_p, xd, mods[0], w_out_b, g0, b0)
    rw0, rb0 = _router_weights(0, router_w, router_bias)
    x1, h, gates = _router(x1c, x1d, mods[0], rw0, rb0)
    x2 = _moe(0, h, gates, x1, mods[0], exp_w_gate, exp_w_up, exp_w_down, sh_w_gate, sh_w_up, sh_w_down,
              row(ln_ffn_g[0]), row(ln_ffn_b[0]))

    w_in_c_b = w_in_c[0].astype(bf16)
    uc, u2c = _s5_in(x2, 0, mods[1], w_in_c_b, N_CTX_B, CTX_L, lambda i: 0)
    ud, u2d = _s5_in(x2, T_CTX // TOK_TILE, mods[1], w_in_c_b, N_DEN_B, DEN_L, lambda i: 1 + i // (DEN_L // TOK_TILE))
    gp = S5_G * S5_P
    chan_major_b = lambda t: jnp.transpose(t[0], (0, 3, 1, 2)).reshape(2, S5_CH, gp)
    chan_major_c = lambda t: t[0].reshape(2, S5_G * S5_CH, S5_P)
    h0 = jnp.transpose(state_ssm[:, 0], (1, 2, 0, 3, 4)).reshape(2, 2, N_DEN_B, gp)
    yc, yd, st = _s5_scan(s5_lam_re[0].reshape(2, 1, gp), s5_lam_im[0].reshape(2, 1, gp),
                          jnp.repeat(s5_log_dt[0], S5_P, axis=-1).reshape(2, 1, gp),
                          chan_major_b(s5_b_re), chan_major_b(s5_b_im), chan_major_c(s5_c_re), chan_major_c(s5_c_im),
                          u2c, u2d, h0)
    rw1, rb1 = _router_weights(1, router_w, router_bias)
    x3, h, gates = _s5_out(x2, uc, ud, yc, yd, mods[1], row(s5_d[0]),
                           w_out_c[0].astype(bf16), row(ln_mix_g[1]), row(ln_mix_b[1]), rw1, rb1)
    x4 = _moe(1, h, gates, x3, mods[1], exp_w_gate, exp_w_up, exp_w_down, sh_w_gate, sh_w_up, sh_w_down,
              row(ln_ffn_g[1]), row(ln_ffn_b[1]))

    y_prompt = x4[:T_CTX].reshape(N_CTX_B, CTX_L, D)
    y_sample = x4[T_CTX:].reshape(N_DEN_B, DEN_L, D)
    new_attn_k = ka[:T_CTX].reshape(N_CTX_B, 1, CTX_L, A_KV_HEADS, A_HD)
    new_attn_v = va[:T_CTX].reshape(N_CTX_B, 1, CTX_L, A_KV_HEADS, A_HD)
    new_mla_ckv = ckv[:T_CTX].reshape(N_CTX_B, 1, CTX_L, MLA_KV_RANK)
    new_mla_krope = kr[:T_CTX, :MLA_ROPE].reshape(N_CTX_B, 1, CTX_L, MLA_ROPE)
    new_state_ssm = jnp.transpose(st, (2, 0, 1, 3)).reshape(N_CTX_B, 1, 2, 2, S5_G, S5_P)
    return (y_prompt, y_sample, new_attn_k, new_attn_v, new_mla_ckv, new_mla_krope, new_state_ssm)
```

```python
import functools

import jax
import jax.numpy as jnp
import numpy as np
from jax import lax
from jax.experimental import pallas as pl
from jax.experimental.pallas import tpu as pltpu

f32 = jnp.float32
bf16 = jnp.bfloat16

D = 1024
N_CTX_B, CTX_L = 16, 256
N_DEN_B, DEN_L = 2, 1024
T_CTX = N_CTX_B * CTX_L
T_DEN = N_DEN_B * DEN_L
T_ALL = T_CTX + T_DEN
GRID_W = 64
WINDOW = 128
ROPE_BASE = 10000.0
A_HEADS, A_KV_HEADS, A_HD = 8, 2, 64
A_GROUP = A_HEADS // A_KV_HEADS
A_SCALE = A_HD ** -0.5
MLA_HEADS, MLA_Q_RANK, MLA_KV_RANK = 8, 256, 128
MLA_NOPE, MLA_ROPE, MLA_V = 64, 32, 64
MLA_SCALE = (MLA_NOPE + MLA_ROPE) ** -0.5
N_EXPERTS, TOP_K, EXPERT_FF, SHARED_FF = 64, 6, 128, 128
ROUTED_SCALE = 2.5
DEPTH = 2
ALPHA = (2.0 * DEPTH) ** 0.25
LN_EPS = 1e-5
RMS_EPS = 1e-6
NEG_INF = -1e30
S5_G, S5_CH, S5_P = 64, 16, 64

LANE = 128
SUB = 8
VMEM_LIMIT = 56 * 1024 * 1024

TOK_TILE = 512


def _mod_row(tile_idx, tile_rows):
    start = tile_idx * tile_rows
    return jnp.where(start < T_CTX, 0, 1 + (start - T_CTX) // DEN_L)


def _layer_norm(y, g, b):
    mu = jnp.mean(y, axis=-1, keepdims=True)
    yc = y - mu
    var = jnp.mean(yc * yc, axis=-1, keepdims=True)
    return yc * lax.rsqrt(var + LN_EPS) * g + b


def _silu(x):
    return x * jax.nn.sigmoid(x)


def _dot(a, b):
    return jnp.dot(a, b, preferred_element_type=f32)


def _dot_nt(a, b):
    return lax.dot_general(a, b, (((1,), (1,)), ((), ())), preferred_element_type=f32)


def _split_bf16(a):
    hi = a.astype(bf16)
    return hi, (a - hi.astype(f32)).astype(bf16)


def _full(shape, n_grid):
    zeros = tuple(0 for _ in shape)
    return pl.BlockSpec(shape, lambda *_: zeros)


def _two_stream_specs(tile_rows, width):
    n_ctx = T_CTX // tile_rows
    return (pl.BlockSpec((tile_rows, width), lambda i: (jnp.minimum(i, n_ctx - 1), 0)),
            pl.BlockSpec((tile_rows, width), lambda i: (jnp.maximum(i - n_ctx, 0), 0)))


def _pick(i, tile_rows, ctx_ref, den_ref):
    return lax.cond(i < T_CTX // tile_rows, lambda: ctx_ref[...], lambda: den_ref[...])


def _params(n_grid):
    return pltpu.CompilerParams(dimension_semantics=("arbitrary",) * n_grid, vmem_limit_bytes=VMEM_LIMIT)


ADA_TN = 1536


def _adaln_kernel(c_ref, w_ref, b_ref, o_ref):
    s_hi, s_lo = _split_bf16(_silu(c_ref[...]))
    w_hi, w_lo = _split_bf16(w_ref[0])
    o_ref[0] = _dot(s_hi, w_hi) + (_dot(s_hi, w_lo) + _dot(s_lo, w_hi)) + b_ref[0]


def _adaln(cvec8, ada_w, ada_b):
    n = 6 * D
    return pl.pallas_call(
        _adaln_kernel,
        grid=(DEPTH, n // ADA_TN),
        in_specs=[
            pl.BlockSpec((8, D), lambda l, j: (0, 0)),
            pl.BlockSpec((1, D, ADA_TN), lambda l, j: (l, 0, j)),
            pl.BlockSpec((1, 1, ADA_TN), lambda l, j: (l, 0, j)),
        ],
        out_specs=pl.BlockSpec((1, 8, ADA_TN), lambda l, j: (l, 0, j)),
        out_shape=jax.ShapeDtypeStruct((DEPTH, 8, n), f32),
        compiler_params=_params(2),
        name="adaln",
    )(cvec8, ada_w, ada_b.reshape(DEPTH, 1, n))


def _rope_table_array(head_dim):
    q = head_dim // 4
    pos = np.arange(DEN_L)
    row, col = (pos // GRID_W).astype(np.float64), (pos % GRID_W).astype(np.float64)
    lane = np.arange(LANE) % head_dim
    is_col = lane >= head_dim // 2
    w = lane % (head_dim // 2)
    first = w < q
    inv_freq = ROPE_BASE ** (-np.arange(q, dtype=np.float64) / q)
    ang = np.where(is_col[None, :], col[:, None], row[:, None]) * inv_freq[w % q][None, :]
    cos, sin = np.cos(ang), np.sin(ang)
    sin_a = np.where(first[None, :], -sin, 0.0)
    sin_b = np.where(first[None, :], 0.0, sin)
    ident = np.stack([np.ones((TOK_TILE, LANE)), np.zeros((TOK_TILE, LANE)), np.zeros((TOK_TILE, LANE))])
    tab = np.concatenate([ident, np.stack([cos, sin_a, sin_b])], axis=1).astype(np.float32)
    return jnp.asarray(tab), q


def _rope_chunk(x, tab_ref, q):
    return x * tab_ref[0] + pltpu.roll(x, LANE - q, 1) * tab_ref[1] + pltpu.roll(x, q, 1) * tab_ref[2]


PROJ_W = 1280
C_QA, C_KA, C_VA, C_CQ, C_CKV, C_KR = 0, 512, 640, 768, 1024, 1152
MLA_NN = MLA_HEADS * MLA_NOPE


def _ab_proj_kernel(xc_ref, xd_ref, mods_ref, w_ref, qn_ref, kvn_ref, wuq_ref, wukv_ref, ta_ref, tm_ref,
                    qa_ref, ka_ref, va_ref, ckv_ref, kr_ref, qm_ref, kvl_ref, *, qa_shift, qm_shift):
    i = pl.program_id(0)
    r = _mod_row(i, TOK_TILE)
    mrow = mods_ref[pl.ds(r, 1), :]
    sh, sc = mrow[:, 0:D], mrow[:, D:2 * D]
    x = _pick(i, TOK_TILE, xc_ref, xd_ref)
    h = (x * (1.0 + sc) + sh).astype(bf16)
    proj = _dot(h, w_ref[...])
    for j in range(4):
        c0 = C_QA + LANE * j
        qa_ref[:, LANE * j:LANE * (j + 1)] = _rope_chunk(proj[:, c0:c0 + LANE], ta_ref, qa_shift).astype(bf16)
    ka_ref[...] = _rope_chunk(proj[:, C_KA:C_KA + LANE], ta_ref, qa_shift)
    va_ref[...] = proj[:, C_VA:C_VA + LANE]
    cq = proj[:, C_CQ:C_CQ + MLA_Q_RANK]
    cq = cq * lax.rsqrt(jnp.mean(cq * cq, axis=-1, keepdims=True) + RMS_EPS) * qn_ref[...]
    ckv = proj[:, C_CKV:C_CKV + MLA_KV_RANK]
    ckv = ckv * lax.rsqrt(jnp.mean(ckv * ckv, axis=-1, keepdims=True) + RMS_EPS) * kvn_ref[...]
    ckv_ref[...] = ckv
    kr_ref[...] = _rope_chunk(proj[:, C_KR:C_KR + LANE], tm_ref, qm_shift)
    qm = _dot(cq.astype(bf16), wuq_ref[...])
    qm_ref[:, 0:MLA_NN] = qm[:, 0:MLA_NN].astype(bf16)
    for j in range(2):
        c0 = MLA_NN + LANE * j
        qm_ref[:, c0:c0 + LANE] = _rope_chunk(qm[:, c0:c0 + LANE], tm_ref, qm_shift).astype(bf16)
    kvl_ref[...] = _dot(ckv.astype(bf16), wukv_ref[...]).astype(bf16)


def _rope_block_index(i):
    tiles_ctx = T_CTX // TOK_TILE
    per_seq = DEN_L // TOK_TILE
    return jnp.where(i < tiles_ctx, 0, 1 + (i - tiles_ctx) % per_seq)


def _ab_proj(xc, xd, mods0, w_in_p, q_norm, kv_norm, w_uq_p, w_ukv_p):
    tab_a, qa_shift = _rope_table_array(A_HD)
    tab_m, qm_shift = _rope_table_array(MLA_ROPE)
    row_spec = lambda w: pl.BlockSpec((TOK_TILE, w), lambda i: (i, 0))
    xc_spec, xd_spec = _two_stream_specs(TOK_TILE, D)
    tab_spec = pl.BlockSpec((3, TOK_TILE, LANE), lambda i: (0, _rope_block_index(i), 0))
    outs = [(512, bf16), (LANE, f32), (LANE, f32), (LANE, f32), (LANE, f32), (768, bf16), (1024, bf16)]
    return pl.pallas_call(
        functools.partial(_ab_proj_kernel, qa_shift=qa_shift, qm_shift=qm_shift),
        grid=(T_ALL // TOK_TILE,),
        in_specs=[xc_spec, xd_spec, _full((8, 6 * D), 1), _full((D, PROJ_W), 1), _full((1, MLA_Q_RANK), 1),
                  _full((1, MLA_KV_RANK), 1), _full((MLA_Q_RANK, 768), 1), _full((MLA_KV_RANK, 1024), 1),
                  tab_spec, tab_spec],
        out_specs=[row_spec(w) for w, _ in outs],
        out_shape=[jax.ShapeDtypeStruct((T_ALL, w), dt) for w, dt in outs],
        compiler_params=_params(1),
        name="ab_proj",
    )(xc, xd, mods0, w_in_p, q_norm, kv_norm, w_uq_p, w_ukv_p, tab_a, tab_m)


def _softmax_blocks(s_refs, p_refs, sink_col=None):
    m = s_refs[0][...].max(axis=-1, keepdims=True)
    for s_ref in s_refs[1:]:
        m = jnp.maximum(m, s_ref[...].max(axis=-1, keepdims=True))
    if sink_col is not None:
        m = jnp.maximum(m, sink_col)
    l = None
    for s_ref, p_ref in zip(s_refs, p_refs):
        p = jnp.exp(s_ref[...] - m)
        p_ref[...] = p.astype(bf16)
        ps = p.sum(axis=-1, keepdims=True)
        l = ps if l is None else l + ps
    if sink_col is not None:
        l = l + jnp.exp(sink_col - m)
    return 1.0 / l


def _sink_column(sink_ref, rows_per_head):
    return jnp.concatenate([jnp.full((rows_per_head, 1), sink_ref[h], f32) for h in range(A_HEADS)], axis=0)


def _mla_q(qm_ref, h):
    rows = qm_ref.shape[0]
    return jnp.concatenate([qm_ref[:, MLA_NOPE * h:MLA_NOPE * (h + 1)],
                            qm_ref[:, MLA_NN + MLA_ROPE * h:MLA_NN + MLA_ROPE * (h + 1)],
                            jnp.zeros((rows, LANE - MLA_NOPE - MLA_ROPE), bf16)], axis=1)


def _mla_k(k_nope_h, k_rope):
    rows = k_nope_h.shape[0]
    return jnp.concatenate([k_nope_h, k_rope, jnp.zeros((rows, LANE - MLA_NOPE - MLA_ROPE), bf16)], axis=1)


def _mix_out_ln(merged_ref, wout_ref, x, mods_ref, r, g_ref, b_ref):
    out = _dot(merged_ref[...], wout_ref[...])
    gate = mods_ref[pl.ds(r, 1), 2 * D:3 * D]
    return _layer_norm(ALPHA * x + gate * out, g_ref[...], b_ref[...])


def _ctx_attn_kernel(sink_ref, qa_ref, ka_ref, va_ref, qm_ref, kvl_ref, kr_ref, x_ref, mods_ref, wout_ref,
                     g_ref, b_ref, o_ref, nk_ref, nv_ref, merged_ref, sa_ref, sm_ref, pa_ref, pm_ref):
    for j in range(A_KV_HEADS):
        nk_ref[0, 0, :, j, :] = ka_ref[:, A_HD * j:A_HD * (j + 1)]
        nv_ref[0, 0, :, j, :] = va_ref[:, A_HD * j:A_HD * (j + 1)]
    n = CTX_L
    ka = ka_ref[...].astype(bf16)
    va = va_ref[...].astype(bf16)
    for j in range(A_KV_HEADS):
        q4 = jnp.concatenate([qa_ref[:, A_HD * h:A_HD * (h + 1)] for h in range(A_GROUP * j, A_GROUP * (j + 1))],
                             axis=0)
        sa_ref[A_GROUP * n * j:A_GROUP * n * (j + 1), :] = _dot_nt(q4, ka[:, A_HD * j:A_HD * (j + 1)]) * A_SCALE
    kr = kr_ref[:, 0:MLA_ROPE].astype(bf16)
    for h in range(MLA_HEADS):
        k_cat = _mla_k(kvl_ref[:, MLA_NOPE * h:MLA_NOPE * (h + 1)], kr)
        sm_ref[n * h:n * (h + 1), :] = _dot_nt(_mla_q(qm_ref, h), k_cat) * MLA_SCALE
    rla = _softmax_blocks([sa_ref], [pa_ref], _sink_column(sink_ref, n))
    rlm = _softmax_blocks([sm_ref], [pm_ref])
    for j in range(A_KV_HEADS):
        rows = slice(A_GROUP * n * j, A_GROUP * n * (j + 1))
        o4 = _dot(pa_ref[rows, :], va[:, A_HD * j:A_HD * (j + 1)]) * rla[rows]
        for g in range(A_GROUP):
            h = A_GROUP * j + g
            merged_ref[:, A_HD * h:A_HD * (h + 1)] = o4[n * g:n * (g + 1)].astype(bf16)
    for h in range(MLA_HEADS):
        rows = slice(n * h, n * (h + 1))
        v = kvl_ref[:, MLA_NN + MLA_V * h:MLA_NN + MLA_V * (h + 1)]
        merged_ref[:, MLA_NN + MLA_V * h:MLA_NN + MLA_V * (h + 1)] = (_dot(pm_ref[rows, :], v) * rlm[rows]).astype(bf16)
    o_ref[...] = _mix_out_ln(merged_ref, wout_ref, x_ref[...], mods_ref, 0, g_ref, b_ref)


def _ctx_attn(sink, qa, ka, va, qm, kvl, kr, x_all, mods0, w_out, ln_g, ln_b):
    blk = lambda w: pl.BlockSpec((CTX_L, w), lambda b: (b, 0))
    cache_blk = pl.BlockSpec((1, 1, CTX_L, A_KV_HEADS, A_HD), lambda b: (b, 0, 0, 0, 0))
    cache_shape = jax.ShapeDtypeStruct((N_CTX_B, 1, CTX_L, A_KV_HEADS, A_HD), f32)
    return pl.pallas_call(
        _ctx_attn_kernel,
        grid=(N_CTX_B,),
        in_specs=[pl.BlockSpec(memory_space=pltpu.SMEM), blk(512), blk(LANE), blk(LANE), blk(768), blk(1024),
                  blk(LANE), blk(D), _full((8, 6 * D), 1), _full((D, D), 1), _full((1, D), 1), _full((1, D), 1)],
        out_specs=[blk(D), cache_blk, cache_blk],
        out_shape=[jax.ShapeDtypeStruct((T_CTX, D), f32), cache_shape, cache_shape],
        scratch_shapes=[pltpu.VMEM((CTX_L, D), bf16),
                        pltpu.VMEM((A_HEADS * CTX_L, CTX_L), f32), pltpu.VMEM((MLA_HEADS * CTX_L, CTX_L), f32),
                        pltpu.VMEM((A_HEADS * CTX_L, CTX_L), bf16), pltpu.VMEM((MLA_HEADS * CTX_L, CTX_L), bf16)],
        compiler_params=_params(1),
        name="ctx_attn",
    )(sink, qa, ka, va, qm, kvl, kr, x_all, mods0, w_out, ln_g, ln_b)


QB = 256
WIN = QB + 2 * WINDOW
DEN_BLK0 = T_CTX // DEN_L
MLA_KEYS = CTX_L + DEN_L


def _den_attn_kernel(sink_ref, qa_ref, ka_ref, va_ref, cak_ref, cav_ref, qm_ref, kvl_ref, kr_ref, cckv_ref, ckr_ref,
                     wukv_ref, x_ref, mods_ref, wout_ref, g_ref, b_ref, o_ref, merged_ref, kcat_ref, vcat_ref,
                     saw_ref, sac_ref, sm_ref, paw_ref, pac_ref, pm_ref):
    b = pl.program_id(0)
    n = pl.program_id(1)

    @pl.when(n == 0)
    def _():
        kvc = _dot(cckv_ref[0].astype(bf16), wukv_ref[...]).astype(bf16)
        kr_ctx = ckr_ref[0].astype(bf16)
        kr_lat = kr_ref[:, 0:MLA_ROPE].astype(bf16)
        for h in range(MLA_HEADS):
            ns = slice(MLA_NOPE * h, MLA_NOPE * (h + 1))
            kcat_ref[h, 0:CTX_L, :] = _mla_k(kvc[:, ns], kr_ctx)
            kcat_ref[h, CTX_L:MLA_KEYS, :] = _mla_k(kvl_ref[:, ns], kr_lat)
        vcat_ref[0:CTX_L, :] = kvc[:, MLA_NN:2 * MLA_NN]
        vcat_ref[CTX_L:MLA_KEYS, :] = kvl_ref[:, MLA_NN:2 * MLA_NN]

    start = pl.multiple_of(jnp.clip(QB * n - WINDOW, 0, DEN_L - WIN), WINDOW)
    grp_rows = A_GROUP * QB
    qpos = QB * n + (lax.broadcasted_iota(jnp.int32, (grp_rows, WIN), 0) & (QB - 1))
    kpos = start + lax.broadcasted_iota(jnp.int32, (grp_rows, WIN), 1)
    valid = jnp.abs(qpos - kpos) <= WINDOW
    kwin = ka_ref[pl.ds(start, WIN), :].astype(bf16)
    vwin = va_ref[pl.ds(start, WIN), :].astype(bf16)
    kctx = [cak_ref[0, :, j, :].astype(bf16) for j in range(A_KV_HEADS)]
    vctx = [cav_ref[0, :, j, :].astype(bf16) for j in range(A_KV_HEADS)]
    for j in range(A_KV_HEADS):
        sl = slice(A_HD * j, A_HD * (j + 1))
        rows = slice(grp_rows * j, grp_rows * (j + 1))
        q4 = jnp.concatenate([qa_ref[:, A_HD * h:A_HD * (h + 1)] for h in range(A_GROUP * j, A_GROUP * (j + 1))],
                             axis=0)
        saw_ref[rows, :] = jnp.where(valid, _dot_nt(q4, kwin[:, sl]) * A_SCALE, NEG_INF)
        sac_ref[rows, :] = _dot_nt(q4, kctx[j]) * A_SCALE
    for h in range(MLA_HEADS):
        sm_ref[QB * h:QB * (h + 1), :] = _dot_nt(_mla_q(qm_ref, h), kcat_ref[h]) * MLA_SCALE
    rla = _softmax_blocks([saw_ref, sac_ref], [paw_ref, pac_ref], _sink_column(sink_ref, QB))
    rlm = _softmax_blocks([sm_ref], [pm_ref])
    for j in range(A_KV_HEADS):
        sl = slice(A_HD * j, A_HD * (j + 1))
        rows = slice(grp_rows * j, grp_rows * (j + 1))
        o4 = (_dot(paw_ref[rows, :], vwin[:, sl]) + _dot(pac_ref[rows, :], vctx[j])) * rla[rows]
        for g in range(A_GROUP):
            h = A_GROUP * j + g
            merged_ref[:, A_HD * h:A_HD * (h + 1)] = o4[QB * g:QB * (g + 1)].astype(bf16)
    for h in range(MLA_HEADS):
        rows = slice(QB * h, QB * (h + 1))
        o = _dot(pm_ref[rows, :], vcat_ref[:, MLA_V * h:MLA_V * (h + 1)]) * rlm[rows]
        merged_ref[:, MLA_NN + MLA_V * h:MLA_NN + MLA_V * (h + 1)] = o.astype(bf16)
    o_ref[...] = _mix_out_ln(merged_ref, wout_ref, x_ref[...], mods_ref, 1 + b, g_ref, b_ref)


def _den_attn(sink, qa, ka, va, cache_k, cache_v, qm, kvl, kr, cache_ckv, cache_kr, w_ukv_p, x_all, mods0, w_out,
              ln_g, ln_b):
    nq = DEN_L // QB
    qblk = lambda w: pl.BlockSpec((QB, w), lambda b, n: (T_CTX // QB + b * nq + n, 0))
    seq = lambda w: pl.BlockSpec((DEN_L, w), lambda b, n: (DEN_BLK0 + b, 0))
    cache = lambda w: pl.BlockSpec((1, CTX_L, w), lambda b, n: (b, 0, 0))
    cache_a = pl.BlockSpec((1, CTX_L, A_KV_HEADS, A_HD), lambda b, n: (b, 0, 0, 0))
    return pl.pallas_call(
        _den_attn_kernel,
        grid=(N_DEN_B, nq),
        in_specs=[pl.BlockSpec(memory_space=pltpu.SMEM), qblk(512), seq(LANE), seq(LANE), cache_a, cache_a,
                  qblk(768), seq(1024), seq(LANE), cache(MLA_KV_RANK), cache(MLA_ROPE),
                  _full((MLA_KV_RANK, 1024), 2), pl.BlockSpec((QB, D), lambda b, n: (b * nq + n, 0)),
                  _full((8, 6 * D), 2), _full((D, D), 2), _full((1, D), 2),
                  _full((1, D), 2)],
        out_specs=pl.BlockSpec((QB, D), lambda b, n: (b * nq + n, 0)),
        out_shape=jax.ShapeDtypeStruct((T_DEN, D), f32),
        scratch_shapes=[pltpu.VMEM((QB, D), bf16), pltpu.VMEM((MLA_HEADS, MLA_KEYS, LANE), bf16),
                        pltpu.VMEM((MLA_KEYS, MLA_NN), bf16)]
        + [pltpu.VMEM((A_HEADS * QB, w), dt) for dt in (f32, bf16) for w in (WIN, CTX_L, MLA_KEYS)],
        compiler_params=_params(2),
        name="den_attn",
    )(sink, qa, ka, va, cache_k, cache_v, qm, kvl, kr, cache_ckv, cache_kr, w_ukv_p, x_all, mods0, w_out, ln_g, ln_b)


SUBTILE = 256


def _route(x1, mrow, rw_ref, rb_ref):
    sh, sc = mrow[:, 3 * D:4 * D], mrow[:, 4 * D:5 * D]
    h = x1 * (1.0 + sc) + sh
    h_hi = h.astype(bf16)
    h_lo = (h - h_hi.astype(f32)).astype(bf16)
    logits = _dot(h_hi, rw_ref[0]) + (_dot(h_hi, rw_ref[1]) + _dot(h_lo, rw_ref[0]))
    scores = jax.nn.sigmoid(logits)
    lane = lax.broadcasted_iota(jnp.int32, scores.shape, 1).astype(f32)
    sel = jnp.where(lane < N_EXPERTS, scores + rb_ref[...], -jnp.inf)
    gates = jnp.zeros_like(scores)
    for _ in range(TOP_K):
        m = sel.max(axis=-1, keepdims=True)
        idx = jnp.where(sel == m, lane, float(LANE)).min(axis=-1, keepdims=True)
        hit = lane == idx
        gates = jnp.where(hit, scores, gates)
        sel = jnp.where(hit, -jnp.inf, sel)
    return h_hi, gates / gates.sum(axis=-1, keepdims=True) * ROUTED_SCALE


def _router_kernel(xc_ref, xd_ref, mods_ref, rw_ref, rb_ref, x_ref, h_ref, gates_ref):
    i = pl.program_id(0)
    r = _mod_row(i, TOK_TILE)
    mrow = mods_ref[pl.ds(r, 1), :]
    x1 = _pick(i, TOK_TILE, xc_ref, xd_ref)
    x_ref[...] = x1
    h_ref[...], gates_ref[...] = _route(x1, mrow, rw_ref, rb_ref)


def _router(x1c, x1d, mods_l, router_w_p, router_b_p):
    row_spec = lambda w: pl.BlockSpec((TOK_TILE, w), lambda i: (i, 0))
    xc_spec, xd_spec = _two_stream_specs(TOK_TILE, D)
    return pl.pallas_call(
        _router_kernel,
        grid=(T_ALL // TOK_TILE,),
        in_specs=[xc_spec, xd_spec, _full((8, 6 * D), 1), _full((2, D, LANE), 1), _full((1, LANE), 1)],
        out_specs=[row_spec(D), row_spec(D), row_spec(LANE)],
        out_shape=[jax.ShapeDtypeStruct((T_ALL, D), f32), jax.ShapeDtypeStruct((T_ALL, D), bf16),
                   jax.ShapeDtypeStruct((T_ALL, LANE), f32)],
        compiler_params=_params(1),
        name="router",
    )(x1c, x1d, mods_l, router_w_p, router_b_p)


MOE_TOK = 1536
MOE_EG = 8
MOE_VMEM_LIMIT = 60 * 1024 * 1024
MOE_TILE = 512
MOE_FF = MOE_EG * EXPERT_FF


def _moe_kernel(h_ref, gates_ref, mods_ref, wg_ref, wu_ref, wd_ref, sg_ref, su_ref, sd_ref, o_ref):
    p = pl.program_id(0)
    e = pl.program_id(1)
    n_tiles = MOE_TOK // MOE_TILE

    def gate_f(t):
        r = _mod_row(p * n_tiles + t, MOE_TILE)
        return mods_ref[pl.ds(r, 1), 5 * D:6 * D]

    def rows_of(t):
        if isinstance(t, int):
            return pl.ds(t * MOE_TILE, MOE_TILE)
        return pl.ds(pl.multiple_of(t * MOE_TILE, MOE_TILE), MOE_TILE)

    @pl.when(e == 0)
    def _():
        sg = sg_ref[...].astype(bf16)
        su = su_ref[...].astype(bf16)
        sd = sd_ref[...].astype(bf16)

        def body(t, c):
            rows = rows_of(t)
            ht = h_ref[rows, :]
            hid = _silu(_dot(ht, sg)) * _dot(ht, su)
            o_ref[rows, :] = gate_f(t) * _dot(hid.astype(bf16), sd)
            return c

        lax.fori_loop(0, n_tiles, body, 0)

    wg = jnp.concatenate([wg_ref[k].astype(bf16) for k in range(MOE_EG)], axis=1)
    wu = jnp.concatenate([wu_ref[k].astype(bf16) for k in range(MOE_EG)], axis=1)
    wd = jnp.concatenate([wd_ref[k].astype(bf16) for k in range(MOE_EG)], axis=0)
    lane = lax.broadcasted_iota(jnp.int32, (MOE_TILE, LANE), 1)

    def body(t, c):
        rows = rows_of(t)
        ht = h_ref[rows, :]
        hid = _silu(_dot(ht, wg)) * _dot(ht, wu)
        gt = gates_ref[rows, :]
        parts = []
        for k in range(MOE_EG):
            col = jnp.where(lane == e * MOE_EG + k, gt, 0.0).sum(axis=-1, keepdims=True)
            parts.append((hid[:, EXPERT_FF * k:EXPERT_FF * (k + 1)] * col).astype(bf16))
        o_ref[rows, :] += gate_f(t) * _dot(jnp.concatenate(parts, axis=1), wd)
        return c

    for t in range(n_tiles):
        body(t, 0)


def _moe(l, h, gates, mods_l, wg, wu, wd, sg, su, sd):
    tok = lambda w: pl.BlockSpec((MOE_TOK, w), lambda p, e: (p, 0))
    return pl.pallas_call(
        _moe_kernel,
        grid=(T_ALL // MOE_TOK, N_EXPERTS // MOE_EG),
        in_specs=[tok(D), tok(LANE), _full((8, 6 * D), 2),
                  pl.BlockSpec((None, MOE_EG, D, EXPERT_FF), lambda p, e: (l, e, 0, 0)),
                  pl.BlockSpec((None, MOE_EG, D, EXPERT_FF), lambda p, e: (l, e, 0, 0)),
                  pl.BlockSpec((None, MOE_EG, EXPERT_FF, D), lambda p, e: (l, e, 0, 0)),
                  pl.BlockSpec((None, D, SHARED_FF), lambda p, e: (l, 0, 0)),
                  pl.BlockSpec((None, D, SHARED_FF), lambda p, e: (l, 0, 0)),
                  pl.BlockSpec((None, SHARED_FF, D), lambda p, e: (l, 0, 0))],
        out_specs=tok(D),
        out_shape=jax.ShapeDtypeStruct((T_ALL, D), f32),
        compiler_params=pltpu.CompilerParams(dimension_semantics=("arbitrary", "arbitrary"),
                                             vmem_limit_bytes=MOE_VMEM_LIMIT),
        name="moe",
    )(h, gates, mods_l, wg, wu, wd, sg, su, sd)


def _moe_finish_kernel(x_ref, acc_ref, g_ref, b_ref, o_ref):
    o_ref[...] = _layer_norm(ALPHA * x_ref[...] + acc_ref[...], g_ref[...], b_ref[...])


def _moe_finish(x1, acc, tile0, n_rows, ln_g, ln_b):
    src = pl.BlockSpec((TOK_TILE, D), lambda i: (tile0 + i, 0))
    return pl.pallas_call(
        _moe_finish_kernel,
        grid=(n_rows // TOK_TILE,),
        in_specs=[src, src, _full((1, D), 1), _full((1, D), 1)],
        out_specs=pl.BlockSpec((TOK_TILE, D), lambda i: (i, 0)),
        out_shape=jax.ShapeDtypeStruct((n_rows, D), f32),
        compiler_params=_params(1),
        name="moe_finish",
    )(x1, acc, ln_g, ln_b)


def _router_weights(l, router_w, router_bias):
    rw = jnp.pad(router_w[l], ((0, 0), (0, LANE - N_EXPERTS)))
    rw_hi, rw_lo = _split_bf16(rw)
    rb = jnp.pad(router_bias[l], (0, LANE - N_EXPERTS)).reshape(1, LANE)
    return jnp.stack([rw_hi, rw_lo]), rb


S5_Q = 8
S5_NGB = D // LANE


def _s5_in_kernel(x_ref, acc_ref, lg_ref, lb_ref, mods_ref, w_ref, x2_ref, u_ref, u2_ref, slab_ref, *, row_of,
                  seq_len):
    r = row_of(pl.program_id(0))
    mrow = mods_ref[pl.ds(r, 1), :]
    sh, sc = mrow[:, 0:D], mrow[:, D:2 * D]
    x2 = _layer_norm(ALPHA * x_ref[...] + acc_ref[...], lg_ref[...], lb_ref[...])
    x2_ref[...] = x2
    h = (x2 * (1.0 + sc) + sh).astype(bf16)
    u = _dot(h, w_ref[...])
    u_ref[...] = u
    for s in range(S5_NGB):
        slab_ref[s] = u[:, LANE * s:LANE * (s + 1)]
    kt = seq_len // S5_Q
    for s in range(S5_NGB):
        for q in range(TOK_TILE // seq_len):
            for j in range(S5_Q):
                u2_ref[s, q * kt:(q + 1) * kt, LANE * j:LANE * (j + 1)] = (
                    slab_ref[s, pl.ds(q * seq_len + j, kt, stride=S5_Q), :].astype(bf16))


def _s5_in(x1, acc, ln_g, ln_b, tile0, mods1, w_in_c, n_b, seq_len, row_of):
    n_tiles = n_b * seq_len // TOK_TILE
    chunks = TOK_TILE // S5_Q
    src = pl.BlockSpec((TOK_TILE, D), lambda i: (tile0 + i, 0))
    dst = pl.BlockSpec((TOK_TILE, D), lambda i: (i, 0))
    return pl.pallas_call(
        functools.partial(_s5_in_kernel, row_of=row_of, seq_len=min(seq_len, TOK_TILE)),
        grid=(n_tiles,),
        in_specs=[src, src, _full((1, D), 1), _full((1, D), 1), _full((8, 6 * D), 1), _full((D, D), 1)],
        out_specs=[dst, dst, pl.BlockSpec((S5_NGB, chunks, D), lambda i: (0, i, 0))],
        out_shape=[jax.ShapeDtypeStruct((n_b * seq_len, D), f32), jax.ShapeDtypeStruct((n_b * seq_len, D), f32),
                   jax.ShapeDtypeStruct((S5_NGB, n_tiles * chunks, D), bf16)],
        scratch_shapes=[pltpu.VMEM((S5_NGB, TOK_TILE, LANE), f32)],
        compiler_params=_params(1),
        name="s5_in",
    )(x1, acc, ln_g, ln_b, mods1, w_in_c)


S5_GL = (LANE // S5_CH) * S5_P
S5_ROWS_C = (CTX_L // S5_Q) * N_CTX_B
S5_ROWS_D = (DEN_L // S5_Q) * N_DEN_B


def _s5_scan_kernel(lre_ref, lim_ref, ldt_ref, btr_ref, bti_ref, ctr_ref, cti_ref, uc_ref, ud_ref, h0_ref,
                    yc_ref, yd_ref, st_ref, win_ref, mso_ref, wit_ref, a_ref, s_ref, hp_ref):
    gl = S5_GL
    rowg = lax.shift_right_logical(lax.broadcasted_iota(jnp.int32, (LANE, gl), 0), 4)
    colg = lax.shift_right_logical(lax.broadcasted_iota(jnp.int32, (LANE, gl), 1), 6)
    same_group = rowg == colg
    reps = LANE // S5_CH

    def expand(t):
        return jnp.where(same_group, jnp.concatenate([t] * reps, axis=0), 0.0)

    def expand_c(t):
        return jnp.where(same_group, jnp.concatenate([t] * reps, axis=1), 0.0)

    for d in range(2):
        fwd = d == 0
        lre, lim = lre_ref[d], lim_ref[d]
        dt = jnp.exp(ldt_ref[d])
        a, w = lre * dt, lim * dt
        pre = [jnp.exp(m * a) * jnp.cos(m * w) for m in range(S5_Q + 1)]
        pim = [jnp.exp(m * a) * jnp.sin(m * w) for m in range(S5_Q + 1)]
        xr, xi = pre[1] - 1.0, pim[1]
        den = lre * lre + lim * lim
        cfr, cfi = (xr * lre + xi * lim) / den, (xi * lre - xr * lim) / den
        btr, bti = btr_ref[d], bti_ref[d]
        bexp_r = expand(cfr * btr - cfi * bti)
        bexp_i = expand(cfr * bti + cfi * btr)
        cexp_r, cexp_i = expand_c(ctr_ref[d]), expand_c(cti_ref[d])
        for m in range(S5_Q + 1):
            a_ref[m, :, 0:gl] = cexp_r * pre[m] - cexp_i * pim[m]
            a_ref[m, :, gl:2 * gl] = -(cexp_r * pim[m] + cexp_i * pre[m])
        for j in range(S5_Q):
            m = S5_Q - 1 - j if fwd else j
            win_ref[LANE * j:LANE * (j + 1), 0:gl] = (pre[m] * bexp_r - pim[m] * bexp_i).astype(bf16)
            win_ref[LANE * j:LANE * (j + 1), gl:2 * gl] = (pre[m] * bexp_i + pim[m] * bexp_r).astype(bf16)
        for j in range(S5_Q):
            m = j + 1 if fwd else S5_Q - j
            mso_ref[LANE * j:LANE * (j + 1), :] = a_ref[m].astype(bf16)
        b2_hi, b2_lo = _split_bf16(jnp.concatenate([bexp_r, bexp_i], axis=1))
        kt = []
        for tau in range(S5_Q):
            a_hi, a_lo = _split_bf16(a_ref[tau])
            kt.append(_dot_nt(b2_hi, a_hi) + _dot_nt(b2_hi, a_lo) + _dot_nt(b2_lo, a_hi))
        for j in range(S5_Q):
            for jp in range(S5_Q):
                tau = jp - j if fwd else j - jp
                blk = slice(LANE * j, LANE * (j + 1)), slice(LANE * jp, LANE * (jp + 1))
                if fwd:
                    wit_ref[blk] = kt[tau] if tau >= 0 else jnp.zeros((LANE, LANE), f32)
                elif tau >= 0:
                    wit_ref[blk] = wit_ref[blk] + kt[tau]

        l8r, l8i = pre[S5_Q], pim[S5_Q]

        nsl = gl // LANE

        def slabs(ref, rs, first):
            return jnp.concatenate([ref[first + sl, rs, :] for sl in range(nsl)], axis=1)

        def put_slabs(ref, rs, first, val):
            for sl in range(nsl):
                ref[first + sl, rs, :] = val[:, LANE * sl:LANE * (sl + 1)]

        def advance(hr, hi_, sr, si):
            return l8r * hr - l8i * hi_ + sr, l8r * hi_ + l8i * hr + si

        def run(u_ref, y_ref, n_b, n_k, h_init):
            rows = n_b * n_k
            s = _dot(u_ref[0], win_ref[...])
            if n_b % SUB == 0:
                pitch = n_k + 1
                for bb in range(n_b):
                    dst = slice(bb * pitch, bb * pitch + n_k)
                    put_slabs(s_ref, dst, 0, s[bb * n_k:(bb + 1) * n_k, 0:gl])
                    put_slabs(s_ref, dst, nsl, s[bb * n_k:(bb + 1) * n_k, gl:2 * gl])

                def step(i, carry):
                    hr, hi_ = carry
                    rs = pl.ds(i if fwd else n_k - 1 - i, n_b, stride=pitch)
                    put_slabs(hp_ref, rs, 0, hr)
                    put_slabs(hp_ref, rs, nsl, hi_)
                    return advance(hr, hi_, slabs(s_ref, rs, 0), slabs(s_ref, rs, nsl))

                h_fin = lax.fori_loop(0, n_k, step, h_init)
                hp = jnp.concatenate(
                    [jnp.concatenate([hp_ref[sl, bb * pitch:bb * pitch + n_k, :] for sl in range(2 * nsl)], axis=1)
                     for bb in range(n_b)], axis=0).astype(bf16)
            else:
                put_slabs(s_ref, slice(0, rows), 0, s[:, 0:gl])
                put_slabs(s_ref, slice(0, rows), nsl, s[:, gl:2 * gl])
                n_it = n_k // SUB

                def step(i, carry):
                    it = i if fwd else n_it - 1 - i
                    out = []
                    for bb in range(n_b):
                        hr, hi_ = carry[bb]
                        rs = pl.ds(pl.multiple_of(bb * n_k + it * SUB, SUB), SUB)
                        s_re, s_im = slabs(s_ref, rs, 0), slabs(s_ref, rs, nsl)
                        prev_r, prev_i = [None] * SUB, [None] * SUB
                        for sub in (range(SUB) if fwd else reversed(range(SUB))):
                            prev_r[sub], prev_i[sub] = hr, hi_
                            hr, hi_ = advance(hr, hi_, s_re[sub:sub + 1], s_im[sub:sub + 1])
                        put_slabs(hp_ref, rs, 0, jnp.concatenate(prev_r, axis=0))
                        put_slabs(hp_ref, rs, nsl, jnp.concatenate(prev_i, axis=0))
                        out.append((hr, hi_))
                    return tuple(out)

                fin = lax.fori_loop(0, n_it, step, tuple((h_init[0][bb:bb + 1], h_init[1][bb:bb + 1])
                                                          for bb in range(n_b)))
                h_fin = (jnp.concatenate([f[0] for f in fin], axis=0), jnp.concatenate([f[1] for f in fin], axis=0))
                hp = jnp.concatenate([hp_ref[sl, 0:rows, :] for sl in range(2 * nsl)], axis=1).astype(bf16)
            y = _dot_nt(hp, mso_ref[...])
            if fwd:
                y_ref[0] = y
            else:
                y_ref[0] += y
            return h_fin

        zeros = jnp.zeros((N_CTX_B, gl), f32)
        hr, hi_ = run(uc_ref, yc_ref, N_CTX_B, CTX_L // S5_Q, (zeros, zeros))
        st_ref[d, 0] = hr
        st_ref[d, 1] = hi_
        run(ud_ref, yd_ref, N_DEN_B, DEN_L // S5_Q, (h0_ref[d, 0], h0_ref[d, 1]))

    wit = wit_ref[...].astype(bf16)
    yc_ref[0] += _dot(uc_ref[0], wit)
    yd_ref[0] += _dot(ud_ref[0], wit)


def _s5_scan(lam_re, lam_im, log_dt, bt_re, bt_im, ct_re, ct_im, u2c, u2d, h0):
    gl = S5_GL
    vec = pl.BlockSpec((2, 1, gl), lambda g: (0, 0, g))
    tab = pl.BlockSpec((2, S5_CH, gl), lambda g: (0, 0, g))
    ctab = pl.BlockSpec((2, LANE, S5_P), lambda g: (0, g, 0))
    rows = lambda n: pl.BlockSpec((1, n, D), lambda g: (g, 0, 0))
    return pl.pallas_call(
        _s5_scan_kernel,
        grid=(S5_NGB,),
        in_specs=[vec, vec, vec, tab, tab, ctab, ctab, rows(S5_ROWS_C), rows(S5_ROWS_D),
                  pl.BlockSpec((2, 2, N_DEN_B, gl), lambda g: (0, 0, 0, g))],
        out_specs=[rows(S5_ROWS_C), rows(S5_ROWS_D), pl.BlockSpec((2, 2, N_CTX_B, gl), lambda g: (0, 0, 0, g))],
        out_shape=[jax.ShapeDtypeStruct((S5_NGB, S5_ROWS_C, D), f32), jax.ShapeDtypeStruct((S5_NGB, S5_ROWS_D, D), f32),
                   jax.ShapeDtypeStruct((2, 2, N_CTX_B, S5_G * S5_P), f32)],
        scratch_shapes=[pltpu.VMEM((D, 2 * gl), bf16), pltpu.VMEM((D, 2 * gl), bf16), pltpu.VMEM((D, D), f32),
                        pltpu.VMEM((S5_Q + 1, LANE, 2 * gl), f32),
                        pltpu.VMEM((2 * gl // LANE, S5_ROWS_C + 2 * N_CTX_B, LANE), f32),
                        pltpu.VMEM((2 * gl // LANE, S5_ROWS_C + 2 * N_CTX_B, LANE), f32)],
        compiler_params=_params(1),
        name="s5_scan",
    )(lam_re, lam_im, log_dt, bt_re, bt_im, ct_re, ct_im, u2c, u2d, h0)


def _gelu_tanh(x):
    return 0.5 * x * (1.0 + jnp.tanh(np.sqrt(2.0 / np.pi).astype(np.float32) * (x + 0.044715 * (x * x * x))))


def _s5_out_kernel(xc_ref, xd_ref, uc_ref, ud_ref, yc_ref, yd_ref, mods_ref, dsk_ref, wout_ref, g_ref, b_ref, rw_ref,
                   rb_ref, x1_ref, h_ref, gates_ref, slab_ref):
    i = pl.program_id(0)
    is_ctx = i < T_CTX // TOK_TILE
    r = _mod_row(i, TOK_TILE)
    mrow = mods_ref[pl.ds(r, 1), :]
    u = _pick(i, TOK_TILE, uc_ref, ud_ref)
    x = _pick(i, TOK_TILE, xc_ref, xd_ref)

    def unchunk(y_ref, seq_len):
        kt = seq_len // S5_Q
        for s in range(S5_NGB):
            for q in range(TOK_TILE // seq_len):
                for j in range(S5_Q):
                    slab_ref[s, pl.ds(q * seq_len + j, kt, stride=S5_Q), :] = (
                        y_ref[s, q * kt:(q + 1) * kt, LANE * j:LANE * (j + 1)])

    lax.cond(is_ctx, lambda: unchunk(yc_ref, min(CTX_L, TOK_TILE)), lambda: unchunk(yd_ref, min(DEN_L, TOK_TILE)))
    for rows in (slice(a, a + SUBTILE) for a in range(0, TOK_TILE, SUBTILE)):
        y = jnp.concatenate([slab_ref[s, rows, :] for s in range(S5_NGB)], axis=1) + dsk_ref[...] * u[rows]
        z = _dot(_gelu_tanh(y).astype(bf16), wout_ref[...])
        out = z[:, 0:D] * jax.nn.sigmoid(z[:, D:2 * D])
        x1 = _layer_norm(ALPHA * x[rows] + mrow[:, 2 * D:3 * D] * out, g_ref[...], b_ref[...])
        x1_ref[rows, :] = x1
        h_ref[rows, :], gates_ref[rows, :] = _route(x1, mrow, rw_ref, rb_ref)


def _s5_out(xc, xd, uc, ud, yc, yd, mods1, d_skip, w_out_c, ln_g, ln_b, rw, rb):
    row_spec = lambda w: pl.BlockSpec((TOK_TILE, w), lambda i: (i, 0))
    uc_spec, ud_spec = _two_stream_specs(TOK_TILE, D)
    n_ctx = T_CTX // TOK_TILE
    chunks = TOK_TILE // S5_Q
    return pl.pallas_call(
        _s5_out_kernel,
        grid=(T_ALL // TOK_TILE,),
        in_specs=[uc_spec, ud_spec, uc_spec, ud_spec,
                  pl.BlockSpec((S5_NGB, chunks, D), lambda i: (0, jnp.minimum(i, n_ctx - 1), 0)),
                  pl.BlockSpec((S5_NGB, chunks, D), lambda i: (0, jnp.maximum(i - n_ctx, 0), 0)),
                  _full((8, 6 * D), 1), _full((1, D), 1), _full((D, 2 * D), 1), _full((1, D), 1), _full((1, D), 1),
                  _full((2, D, LANE), 1), _full((1, LANE), 1)],
        out_specs=[row_spec(D), row_spec(D), row_spec(LANE)],
        out_shape=[jax.ShapeDtypeStruct((T_ALL, D), f32), jax.ShapeDtypeStruct((T_ALL, D), bf16),
                   jax.ShapeDtypeStruct((T_ALL, LANE), f32)],
        scratch_shapes=[pltpu.VMEM((S5_NGB, TOK_TILE, LANE), f32)],
        compiler_params=_params(1),
        name="s5_out",
    )(xc, xd, uc, ud, yc, yd, mods1, d_skip, w_out_c, ln_g, ln_b, rw, rb)


def kernel(x_prompt, x_sample, c, cache_attn_k, cache_attn_v, cache_mla_ckv, cache_mla_krope, state_ssm, c_ctx,
           ada_w, ada_b, ln_mix_g, ln_mix_b, ln_ffn_g, ln_ffn_b, w_in_ab, attn_sink, mla_q_norm, mla_kv_norm,
           mla_w_uq, mla_w_ukv, w_out_ab, w_in_c, s5_lam_re, s5_lam_im, s5_log_dt, s5_b_re, s5_b_im, s5_c_re,
           s5_c_im, s5_d, w_out_c, router_w, router_bias, exp_w_gate, exp_w_up, exp_w_down, sh_w_gate, sh_w_up,
           sh_w_down):
    row = lambda v: v.reshape(1, -1)
    xc, xd = x_prompt.reshape(T_CTX, D), x_sample.reshape(T_DEN, D)
    cvec8 = jnp.concatenate([c_ctx[None, :], c, jnp.zeros((8 - 1 - N_DEN_B, D), f32)], axis=0)
    mods = _adaln(cvec8, ada_w, ada_b)

    w_in_p = jnp.pad(w_in_ab[0], ((0, 0), (0, PROJ_W - w_in_ab.shape[-1]))).astype(bf16)
    uq = mla_w_uq[0].reshape(MLA_Q_RANK, MLA_HEADS, MLA_NOPE + MLA_ROPE)
    w_uq_p = jnp.concatenate([uq[:, :, :MLA_NOPE].reshape(MLA_Q_RANK, -1), uq[:, :, MLA_NOPE:].reshape(MLA_Q_RANK, -1)],
                             axis=1).astype(bf16)
    ukv = mla_w_ukv[0].reshape(MLA_KV_RANK, MLA_HEADS, MLA_NOPE + MLA_V)
    w_ukv_p = jnp.concatenate([ukv[:, :, :MLA_NOPE].reshape(MLA_KV_RANK, -1),
                               ukv[:, :, MLA_NOPE:].reshape(MLA_KV_RANK, -1)], axis=1).astype(bf16)
    qa, ka, va, ckv, kr, qm, kvl = _ab_proj(xc, xd, mods[0], w_in_p, row(mla_q_norm[0]), row(mla_kv_norm[0]),
                                            w_uq_p, w_ukv_p)
    w_out_b = w_out_ab[0].astype(bf16)
    g0, b0 = row(ln_mix_g[0]), row(ln_mix_b[0])
    x1c, new_attn_k, new_attn_v = _ctx_attn(attn_sink[0], qa, ka, va, qm, kvl, kr, xc, mods[0], w_out_b, g0, b0)
    x1d = _den_attn(attn_sink[0], qa, ka, va,
                    cache_attn_k[:, 0], cache_attn_v[:, 0],
                    qm, kvl, kr, cache_mla_ckv[:, 0], cache_mla_krope[:, 0], w_ukv_p, xd, mods[0], w_out_b, g0, b0)
    rw0, rb0 = _router_weights(0, router_w, router_bias)
    x1, h, gates = _router(x1c, x1d, mods[0], rw0, rb0)
    acc = _moe(0, h, gates, mods[0], exp_w_gate, exp_w_up, exp_w_down, sh_w_gate, sh_w_up, sh_w_down)

    w_in_c_b = w_in_c[0].astype(bf16)
    lg0, lb0 = row(ln_ffn_g[0]), row(ln_ffn_b[0])
    n_ctx_tiles = T_CTX // TOK_TILE
    x2c, uc, u2c = _s5_in(x1, acc, lg0, lb0, 0, mods[1], w_in_c_b, N_CTX_B, CTX_L, lambda i: 0)
    x2d, ud, u2d = _s5_in(x1, acc, lg0, lb0, n_ctx_tiles, mods[1], w_in_c_b, N_DEN_B, DEN_L,
                          lambda i: 1 + i // (DEN_L // TOK_TILE))
    gp = S5_G * S5_P
    chan_major_b = lambda t: jnp.transpose(t[0], (0, 3, 1, 2)).reshape(2, S5_CH, gp)
    chan_major_c = lambda t: t[0].reshape(2, S5_G * S5_CH, S5_P)
    h0 = jnp.transpose(state_ssm[:, 0], (1, 2, 0, 3, 4)).reshape(2, 2, N_DEN_B, gp)
    yc, yd, st = _s5_scan(s5_lam_re[0].reshape(2, 1, gp), s5_lam_im[0].reshape(2, 1, gp),
                          jnp.repeat(s5_log_dt[0], S5_P, axis=-1).reshape(2, 1, gp),
                          chan_major_b(s5_b_re), chan_major_b(s5_b_im), chan_major_c(s5_c_re), chan_major_c(s5_c_im),
                          u2c, u2d, h0)
    rw1, rb1 = _router_weights(1, router_w, router_bias)
    x3, h, gates = _s5_out(x2c, x2d, uc, ud, yc, yd, mods[1], row(s5_d[0]),
                           w_out_c[0].astype(bf16), row(ln_mix_g[1]), row(ln_mix_b[1]), rw1, rb1)
    acc = _moe(1, h, gates, mods[1], exp_w_gate, exp_w_up, exp_w_down, sh_w_gate, sh_w_up, sh_w_down)
    lg1, lb1 = row(ln_ffn_g[1]), row(ln_ffn_b[1])
    y_prompt = _moe_finish(x3, acc, 0, T_CTX, lg1, lb1).reshape(N_CTX_B, CTX_L, D)
    y_sample = _moe_finish(x3, acc, n_ctx_tiles, T_DEN, lg1, lb1).reshape(N_DEN_B, DEN_L, D)
    new_mla_ckv = ckv[:T_CTX].reshape(N_CTX_B, 1, CTX_L, MLA_KV_RANK)
    new_mla_krope = kr[:T_CTX, :MLA_ROPE].reshape(N_CTX_B, 1, CTX_L, MLA_ROPE)
    new_state_ssm = jnp.transpose(st, (2, 0, 1, 3)).reshape(N_CTX_B, 1, 2, 2, S5_G, S5_P)
    return (y_prompt, y_sample, new_attn_k, new_attn_v, new_mla_ckv, new_mla_krope, new_state_ssm)
```

```python
import functools

import jax
import jax.numpy as jnp
import numpy as np
from jax import lax
from jax.experimental import pallas as pl
from jax.experimental.pallas import tpu as pltpu

f32 = jnp.float32
bf16 = jnp.bfloat16

D = 1024
N_CTX_B, CTX_L = 16, 256
N_DEN_B, DEN_L = 2, 1024
T_CTX = N_CTX_B * CTX_L
T_DEN = N_DEN_B * DEN_L
T_ALL = T_CTX + T_DEN
GRID_W = 64
WINDOW = 128
ROPE_BASE = 10000.0
A_HEADS, A_KV_HEADS, A_HD = 8, 2, 64
A_GROUP = A_HEADS // A_KV_HEADS
A_SCALE = A_HD ** -0.5
MLA_HEADS, MLA_Q_RANK, MLA_KV_RANK = 8, 256, 128
MLA_NOPE, MLA_ROPE, MLA_V = 64, 32, 64
MLA_SCALE = (MLA_NOPE + MLA_ROPE) ** -0.5
N_EXPERTS, TOP_K, EXPERT_FF, SHARED_FF = 64, 6, 128, 128
ROUTED_SCALE = 2.5
DEPTH = 2
ALPHA = (2.0 * DEPTH) ** 0.25
LN_EPS = 1e-5
RMS_EPS = 1e-6
NEG_INF = -1e30
S5_G, S5_CH, S5_P = 64, 16, 64

LANE = 128
SUB = 8
VMEM_LIMIT = 56 * 1024 * 1024

TOK_TILE = 512


def _mod_row(tile_idx, tile_rows):
    start = tile_idx * tile_rows
    return jnp.where(start < T_CTX, 0, 1 + (start - T_CTX) // DEN_L)


def _layer_norm(y, g, b):
    mu = jnp.mean(y, axis=-1, keepdims=True)
    yc = y - mu
    var = jnp.mean(yc * yc, axis=-1, keepdims=True)
    return yc * lax.rsqrt(var + LN_EPS) * g + b


def _silu(x):
    return x * jax.nn.sigmoid(x)


def _dot(a, b):
    return jnp.dot(a, b, preferred_element_type=f32)


def _dot_nt(a, b):
    return lax.dot_general(a, b, (((1,), (1,)), ((), ())), preferred_element_type=f32)


def _split_bf16(a):
    hi = a.astype(bf16)
    return hi, (a - hi.astype(f32)).astype(bf16)


def _full(shape, n_grid):
    zeros = tuple(0 for _ in shape)
    return pl.BlockSpec(shape, lambda *_: zeros)


def _two_stream_specs(tile_rows, width):
    n_ctx = T_CTX // tile_rows
    return (pl.BlockSpec((tile_rows, width), lambda i: (jnp.minimum(i, n_ctx - 1), 0)),
            pl.BlockSpec((tile_rows, width), lambda i: (jnp.maximum(i - n_ctx, 0), 0)))


def _pick(i, tile_rows, ctx_ref, den_ref):
    return lax.cond(i < T_CTX // tile_rows, lambda: ctx_ref[...], lambda: den_ref[...])


def _params(n_grid):
    return pltpu.CompilerParams(dimension_semantics=("arbitrary",) * n_grid, vmem_limit_bytes=VMEM_LIMIT)


ADA_TN = 1536


def _adaln_kernel(c_ref, w_ref, b_ref, o_ref):
    s_hi, s_lo = _split_bf16(_silu(c_ref[...]))
    w_hi, w_lo = _split_bf16(w_ref[0])
    o_ref[0] = _dot(s_hi, w_hi) + (_dot(s_hi, w_lo) + _dot(s_lo, w_hi)) + b_ref[0]


def _adaln(cvec8, ada_w, ada_b):
    n = 6 * D
    return pl.pallas_call(
        _adaln_kernel,
        grid=(DEPTH, n // ADA_TN),
        in_specs=[
            pl.BlockSpec((8, D), lambda l, j: (0, 0)),
            pl.BlockSpec((1, D, ADA_TN), lambda l, j: (l, 0, j)),
            pl.BlockSpec((1, 1, ADA_TN), lambda l, j: (l, 0, j)),
        ],
        out_specs=pl.BlockSpec((1, 8, ADA_TN), lambda l, j: (l, 0, j)),
        out_shape=jax.ShapeDtypeStruct((DEPTH, 8, n), f32),
        compiler_params=_params(2),
        name="adaln",
    )(cvec8, ada_w, ada_b.reshape(DEPTH, 1, n))


def _rope_table_array(head_dim):
    q = head_dim // 4
    pos = np.arange(DEN_L)
    row, col = (pos // GRID_W).astype(np.float64), (pos % GRID_W).astype(np.float64)
    lane = np.arange(LANE) % head_dim
    is_col = lane >= head_dim // 2
    w = lane % (head_dim // 2)
    first = w < q
    inv_freq = ROPE_BASE ** (-np.arange(q, dtype=np.float64) / q)
    ang = np.where(is_col[None, :], col[:, None], row[:, None]) * inv_freq[w % q][None, :]
    cos, sin = np.cos(ang), np.sin(ang)
    sin_a = np.where(first[None, :], -sin, 0.0)
    sin_b = np.where(first[None, :], 0.0, sin)
    ident = np.stack([np.ones((TOK_TILE, LANE)), np.zeros((TOK_TILE, LANE)), np.zeros((TOK_TILE, LANE))])
    tab = np.concatenate([ident, np.stack([cos, sin_a, sin_b])], axis=1).astype(np.float32)
    return jnp.asarray(tab), q


def _rope_chunk(x, tab_ref, q):
    return x * tab_ref[0] + pltpu.roll(x, LANE - q, 1) * tab_ref[1] + pltpu.roll(x, q, 1) * tab_ref[2]


PROJ_W = 1280
C_QA, C_KA, C_VA, C_CQ, C_CKV, C_KR = 0, 512, 640, 768, 1024, 1152
MLA_NN = MLA_HEADS * MLA_NOPE


def _ab_proj_kernel(xc_ref, xd_ref, mods_ref, w_ref, qn_ref, kvn_ref, wuq_ref, wukv_ref, ta_ref, tm_ref,
                    qa_ref, ka_ref, va_ref, ckv_ref, kr_ref, qm_ref, kvl_ref, *, qa_shift, qm_shift):
    i = pl.program_id(0)
    r = _mod_row(i, TOK_TILE)
    mrow = mods_ref[pl.ds(r, 1), :]
    sh, sc = mrow[:, 0:D], mrow[:, D:2 * D]
    x = _pick(i, TOK_TILE, xc_ref, xd_ref)
    h = (x * (1.0 + sc) + sh).astype(bf16)
    proj = _dot(h, w_ref[...])
    for j in range(4):
        c0 = C_QA + LANE * j
        qa_ref[:, LANE * j:LANE * (j + 1)] = _rope_chunk(proj[:, c0:c0 + LANE], ta_ref, qa_shift).astype(bf16)
    ka_ref[...] = _rope_chunk(proj[:, C_KA:C_KA + LANE], ta_ref, qa_shift)
    va_ref[...] = proj[:, C_VA:C_VA + LANE]
    cq = proj[:, C_CQ:C_CQ + MLA_Q_RANK]
    cq = cq * lax.rsqrt(jnp.mean(cq * cq, axis=-1, keepdims=True) + RMS_EPS) * qn_ref[...]
    ckv = proj[:, C_CKV:C_CKV + MLA_KV_RANK]
    ckv = ckv * lax.rsqrt(jnp.mean(ckv * ckv, axis=-1, keepdims=True) + RMS_EPS) * kvn_ref[...]
    ckv_ref[...] = ckv
    kr_ref[...] = _rope_chunk(proj[:, C_KR:C_KR + LANE], tm_ref, qm_shift)
    qm = _dot(cq.astype(bf16), wuq_ref[...])
    qm_ref[:, 0:MLA_NN] = qm[:, 0:MLA_NN].astype(bf16)
    for j in range(2):
        c0 = MLA_NN + LANE * j
        qm_ref[:, c0:c0 + LANE] = _rope_chunk(qm[:, c0:c0 + LANE], tm_ref, qm_shift).astype(bf16)
    kvl_ref[...] = _dot(ckv.astype(bf16), wukv_ref[...]).astype(bf16)


def _rope_block_index(i):
    tiles_ctx = T_CTX // TOK_TILE
    per_seq = DEN_L // TOK_TILE
    return jnp.where(i < tiles_ctx, 0, 1 + (i - tiles_ctx) % per_seq)


def _ab_proj(xc, xd, mods0, w_in_p, q_norm, kv_norm, w_uq_p, w_ukv_p):
    tab_a, qa_shift = _rope_table_array(A_HD)
    tab_m, qm_shift = _rope_table_array(MLA_ROPE)
    row_spec = lambda w: pl.BlockSpec((TOK_TILE, w), lambda i: (i, 0))
    xc_spec, xd_spec = _two_stream_specs(TOK_TILE, D)
    tab_spec = pl.BlockSpec((3, TOK_TILE, LANE), lambda i: (0, _rope_block_index(i), 0))
    outs = [(512, bf16), (LANE, f32), (LANE, f32), (LANE, f32), (LANE, f32), (768, bf16), (1024, bf16)]
    return pl.pallas_call(
        functools.partial(_ab_proj_kernel, qa_shift=qa_shift, qm_shift=qm_shift),
        grid=(T_ALL // TOK_TILE,),
        in_specs=[xc_spec, xd_spec, _full((8, 6 * D), 1), _full((D, PROJ_W), 1), _full((1, MLA_Q_RANK), 1),
                  _full((1, MLA_KV_RANK), 1), _full((MLA_Q_RANK, 768), 1), _full((MLA_KV_RANK, 1024), 1),
                  tab_spec, tab_spec],
        out_specs=[row_spec(w) for w, _ in outs],
        out_shape=[jax.ShapeDtypeStruct((T_ALL, w), dt) for w, dt in outs],
        compiler_params=_params(1),
        name="ab_proj",
    )(xc, xd, mods0, w_in_p, q_norm, kv_norm, w_uq_p, w_ukv_p, tab_a, tab_m)


def _softmax_blocks(s_refs, p_refs, sink_col=None):
    m = s_refs[0][...].max(axis=-1, keepdims=True)
    for s_ref in s_refs[1:]:
        m = jnp.maximum(m, s_ref[...].max(axis=-1, keepdims=True))
    if sink_col is not None:
        m = jnp.maximum(m, sink_col)
    l = None
    for s_ref, p_ref in zip(s_refs, p_refs):
        p = jnp.exp(s_ref[...] - m)
        p_ref[...] = p.astype(bf16)
        ps = p.sum(axis=-1, keepdims=True)
        l = ps if l is None else l + ps
    if sink_col is not None:
        l = l + jnp.exp(sink_col - m)
    return 1.0 / l


def _sink_column(sink_ref, rows_per_head):
    return jnp.concatenate([jnp.full((rows_per_head, 1), sink_ref[h], f32) for h in range(A_HEADS)], axis=0)


def _mla_q(qm_ref, h):
    rows = qm_ref.shape[0]
    return jnp.concatenate([qm_ref[:, MLA_NOPE * h:MLA_NOPE * (h + 1)],
                            qm_ref[:, MLA_NN + MLA_ROPE * h:MLA_NN + MLA_ROPE * (h + 1)],
                            jnp.zeros((rows, LANE - MLA_NOPE - MLA_ROPE), bf16)], axis=1)


def _mla_k(k_nope_h, k_rope):
    rows = k_nope_h.shape[0]
    return jnp.concatenate([k_nope_h, k_rope, jnp.zeros((rows, LANE - MLA_NOPE - MLA_ROPE), bf16)], axis=1)


def _mix_out_ln(merged_ref, wout_ref, x, mods_ref, r, g_ref, b_ref):
    out = _dot(merged_ref[...], wout_ref[...])
    gate = mods_ref[pl.ds(r, 1), 2 * D:3 * D]
    return _layer_norm(ALPHA * x + gate * out, g_ref[...], b_ref[...])


def _ctx_attn_kernel(sink_ref, qa_ref, ka_ref, va_ref, qm_ref, kvl_ref, kr_ref, x_ref, mods_ref, wout_ref,
                     g_ref, b_ref, o_ref, nk_ref, nv_ref, merged_ref, sa_ref, sm_ref, pa_ref, pm_ref):
    nk_ref[0, 0] = ka_ref[...].T.reshape(A_KV_HEADS, A_HD, CTX_L)
    nv_ref[0, 0] = va_ref[...].T.reshape(A_KV_HEADS, A_HD, CTX_L)
    n = CTX_L
    ka = ka_ref[...].astype(bf16)
    va = va_ref[...].astype(bf16)
    for j in range(A_KV_HEADS):
        q4 = jnp.concatenate([qa_ref[:, A_HD * h:A_HD * (h + 1)] for h in range(A_GROUP * j, A_GROUP * (j + 1))],
                             axis=0)
        sa_ref[A_GROUP * n * j:A_GROUP * n * (j + 1), :] = _dot_nt(q4, ka[:, A_HD * j:A_HD * (j + 1)]) * A_SCALE
    kr = kr_ref[:, 0:MLA_ROPE].astype(bf16)
    for h in range(MLA_HEADS):
        k_cat = _mla_k(kvl_ref[:, MLA_NOPE * h:MLA_NOPE * (h + 1)], kr)
        sm_ref[n * h:n * (h + 1), :] = _dot_nt(_mla_q(qm_ref, h), k_cat) * MLA_SCALE
    rla = _softmax_blocks([sa_ref], [pa_ref], _sink_column(sink_ref, n))
    rlm = _softmax_blocks([sm_ref], [pm_ref])
    for j in range(A_KV_HEADS):
        rows = slice(A_GROUP * n * j, A_GROUP * n * (j + 1))
        o4 = _dot(pa_ref[rows, :], va[:, A_HD * j:A_HD * (j + 1)]) * rla[rows]
        for g in range(A_GROUP):
            h = A_GROUP * j + g
            merged_ref[:, A_HD * h:A_HD * (h + 1)] = o4[n * g:n * (g + 1)].astype(bf16)
    for h in range(MLA_HEADS):
        rows = slice(n * h, n * (h + 1))
        v = kvl_ref[:, MLA_NN + MLA_V * h:MLA_NN + MLA_V * (h + 1)]
        merged_ref[:, MLA_NN + MLA_V * h:MLA_NN + MLA_V * (h + 1)] = (_dot(pm_ref[rows, :], v) * rlm[rows]).astype(bf16)
    o_ref[...] = _mix_out_ln(merged_ref, wout_ref, x_ref[...], mods_ref, 0, g_ref, b_ref)


def _ctx_attn(sink, qa, ka, va, qm, kvl, kr, x_all, mods0, w_out, ln_g, ln_b):
    blk = lambda w: pl.BlockSpec((CTX_L, w), lambda b: (b, 0))
    cache_blk = pl.BlockSpec((1, 1, A_KV_HEADS, A_HD, CTX_L), lambda b: (b, 0, 0, 0, 0))
    cache_shape = jax.ShapeDtypeStruct((N_CTX_B, 1, A_KV_HEADS, A_HD, CTX_L), f32)
    return pl.pallas_call(
        _ctx_attn_kernel,
        grid=(N_CTX_B,),
        in_specs=[pl.BlockSpec(memory_space=pltpu.SMEM), blk(512), blk(LANE), blk(LANE), blk(768), blk(1024),
                  blk(LANE), blk(D), _full((8, 6 * D), 1), _full((D, D), 1), _full((1, D), 1), _full((1, D), 1)],
        out_specs=[blk(D), cache_blk, cache_blk],
        out_shape=[jax.ShapeDtypeStruct((T_CTX, D), f32), cache_shape, cache_shape],
        scratch_shapes=[pltpu.VMEM((CTX_L, D), bf16),
                        pltpu.VMEM((A_HEADS * CTX_L, CTX_L), f32), pltpu.VMEM((MLA_HEADS * CTX_L, CTX_L), f32),
                        pltpu.VMEM((A_HEADS * CTX_L, CTX_L), bf16), pltpu.VMEM((MLA_HEADS * CTX_L, CTX_L), bf16)],
        compiler_params=_params(1),
        name="ctx_attn",
    )(sink, qa, ka, va, qm, kvl, kr, x_all, mods0, w_out, ln_g, ln_b)


QB = 256
WIN = QB + 2 * WINDOW
DEN_BLK0 = T_CTX // DEN_L
MLA_KEYS = CTX_L + DEN_L


def _den_attn_kernel(sink_ref, qa_ref, ka_ref, va_ref, cak_ref, cav_ref, qm_ref, kvl_ref, kr_ref, cckv_ref, ckr_ref,
                     wukv_ref, x_ref, mods_ref, wout_ref, g_ref, b_ref, o_ref, merged_ref, kcat_ref, vcat_ref,
                     saw_ref, sac_ref, sm_ref, paw_ref, pac_ref, pm_ref):
    b = pl.program_id(0)
    n = pl.program_id(1)

    @pl.when(n == 0)
    def _():
        kvc = _dot(cckv_ref[0].astype(bf16), wukv_ref[...]).astype(bf16)
        kr_ctx = ckr_ref[0].astype(bf16)
        kr_lat = kr_ref[:, 0:MLA_ROPE].astype(bf16)
        for h in range(MLA_HEADS):
            ns = slice(MLA_NOPE * h, MLA_NOPE * (h + 1))
            kcat_ref[h, 0:CTX_L, :] = _mla_k(kvc[:, ns], kr_ctx)
            kcat_ref[h, CTX_L:MLA_KEYS, :] = _mla_k(kvl_ref[:, ns], kr_lat)
        vcat_ref[0:CTX_L, :] = kvc[:, MLA_NN:2 * MLA_NN]
        vcat_ref[CTX_L:MLA_KEYS, :] = kvl_ref[:, MLA_NN:2 * MLA_NN]

    start = pl.multiple_of(jnp.clip(QB * n - WINDOW, 0, DEN_L - WIN), WINDOW)
    grp_rows = A_GROUP * QB
    qpos = QB * n + (lax.broadcasted_iota(jnp.int32, (grp_rows, WIN), 0) & (QB - 1))
    kpos = start + lax.broadcasted_iota(jnp.int32, (grp_rows, WIN), 1)
    valid = jnp.abs(qpos - kpos) <= WINDOW
    kwin = ka_ref[pl.ds(start, WIN), :].astype(bf16)
    vwin = va_ref[pl.ds(start, WIN), :].astype(bf16)
    kctx_t = [cak_ref[0, j].astype(bf16) for j in range(A_KV_HEADS)]
    vctx_t = [cav_ref[0, j].astype(bf16) for j in range(A_KV_HEADS)]
    for j in range(A_KV_HEADS):
        sl = slice(A_HD * j, A_HD * (j + 1))
        rows = slice(grp_rows * j, grp_rows * (j + 1))
        q4 = jnp.concatenate([qa_ref[:, A_HD * h:A_HD * (h + 1)] for h in range(A_GROUP * j, A_GROUP * (j + 1))],
                             axis=0)
        saw_ref[rows, :] = jnp.where(valid, _dot_nt(q4, kwin[:, sl]) * A_SCALE, NEG_INF)
        sac_ref[rows, :] = _dot(q4, kctx_t[j]) * A_SCALE
    for h in range(MLA_HEADS):
        sm_ref[QB * h:QB * (h + 1), :] = _dot_nt(_mla_q(qm_ref, h), kcat_ref[h]) * MLA_SCALE
    rla = _softmax_blocks([saw_ref, sac_ref], [paw_ref, pac_ref], _sink_column(sink_ref, QB))
    rlm = _softmax_blocks([sm_ref], [pm_ref])
    for j in range(A_KV_HEADS):
        sl = slice(A_HD * j, A_HD * (j + 1))
        rows = slice(grp_rows * j, grp_rows * (j + 1))
        o4 = (_dot(paw_ref[rows, :], vwin[:, sl]) + _dot_nt(pac_ref[rows, :], vctx_t[j])) * rla[rows]
        for g in range(A_GROUP):
            h = A_GROUP * j + g
            merged_ref[:, A_HD * h:A_HD * (h + 1)] = o4[QB * g:QB * (g + 1)].astype(bf16)
    for h in range(MLA_HEADS):
        rows = slice(QB * h, QB * (h + 1))
        o = _dot(pm_ref[rows, :], vcat_ref[:, MLA_V * h:MLA_V * (h + 1)]) * rlm[rows]
        merged_ref[:, MLA_NN + MLA_V * h:MLA_NN + MLA_V * (h + 1)] = o.astype(bf16)
    o_ref[...] = _mix_out_ln(merged_ref, wout_ref, x_ref[...], mods_ref, 1 + b, g_ref, b_ref)


def _den_attn(sink, qa, ka, va, cache_k, cache_v, qm, kvl, kr, cache_ckv, cache_kr, w_ukv_p, x_all, mods0, w_out,
              ln_g, ln_b):
    nq = DEN_L // QB
    qblk = lambda w: pl.BlockSpec((QB, w), lambda b, n: (T_CTX // QB + b * nq + n, 0))
    seq = lambda w: pl.BlockSpec((DEN_L, w), lambda b, n: (DEN_BLK0 + b, 0))
    cache = lambda w: pl.BlockSpec((1, CTX_L, w), lambda b, n: (b, 0, 0))
    cache_a = pl.BlockSpec((1, A_KV_HEADS, A_HD, CTX_L), lambda b, n: (b, 0, 0, 0))
    return pl.pallas_call(
        _den_attn_kernel,
        grid=(N_DEN_B, nq),
        in_specs=[pl.BlockSpec(memory_space=pltpu.SMEM), qblk(512), seq(LANE), seq(LANE), cache_a, cache_a,
                  qblk(768), seq(1024), seq(LANE), cache(MLA_KV_RANK), cache(MLA_ROPE),
                  _full((MLA_KV_RANK, 1024), 2), pl.BlockSpec((QB, D), lambda b, n: (b * nq + n, 0)),
                  _full((8, 6 * D), 2), _full((D, D), 2), _full((1, D), 2),
                  _full((1, D), 2)],
        out_specs=pl.BlockSpec((QB, D), lambda b, n: (b * nq + n, 0)),
        out_shape=jax.ShapeDtypeStruct((T_DEN, D), f32),
        scratch_shapes=[pltpu.VMEM((QB, D), bf16), pltpu.VMEM((MLA_HEADS, MLA_KEYS, LANE), bf16),
                        pltpu.VMEM((MLA_KEYS, MLA_NN), bf16)]
        + [pltpu.VMEM((A_HEADS * QB, w), dt) for dt in (f32, bf16) for w in (WIN, CTX_L, MLA_KEYS)],
        compiler_params=_params(2),
        name="den_attn",
    )(sink, qa, ka, va, cache_k, cache_v, qm, kvl, kr, cache_ckv, cache_kr, w_ukv_p, x_all, mods0, w_out, ln_g, ln_b)


SUBTILE = 256


def _route(x1, mrow, rw_ref, rb_ref):
    sh, sc = mrow[:, 3 * D:4 * D], mrow[:, 4 * D:5 * D]
    h = x1 * (1.0 + sc) + sh
    h_hi = h.astype(bf16)
    h_lo = (h - h_hi.astype(f32)).astype(bf16)
    logits = _dot(h_hi, rw_ref[0]) + (_dot(h_hi, rw_ref[1]) + _dot(h_lo, rw_ref[0]))
    scores = jax.nn.sigmoid(logits)
    lane = lax.broadcasted_iota(jnp.int32, scores.shape, 1).astype(f32)
    sel = jnp.where(lane < N_EXPERTS, scores + rb_ref[...], -jnp.inf)
    gates = jnp.zeros_like(scores)
    for _ in range(TOP_K):
        m = sel.max(axis=-1, keepdims=True)
        idx = jnp.where(sel == m, lane, float(LANE)).min(axis=-1, keepdims=True)
        hit = lane == idx
        gates = jnp.where(hit, scores, gates)
        sel = jnp.where(hit, -jnp.inf, sel)
    return h_hi, gates / gates.sum(axis=-1, keepdims=True) * ROUTED_SCALE


def _router_kernel(xc_ref, xd_ref, mods_ref, rw_ref, rb_ref, x_ref, h_ref, gates_ref):
    i = pl.program_id(0)
    r = _mod_row(i, TOK_TILE)
    mrow = mods_ref[pl.ds(r, 1), :]
    x1 = _pick(i, TOK_TILE, xc_ref, xd_ref)
    x_ref[...] = x1
    h_ref[...], gates_ref[...] = _route(x1, mrow, rw_ref, rb_ref)


def _router(x1c, x1d, mods_l, router_w_p, router_b_p):
    row_spec = lambda w: pl.BlockSpec((TOK_TILE, w), lambda i: (i, 0))
    xc_spec, xd_spec = _two_stream_specs(TOK_TILE, D)
    return pl.pallas_call(
        _router_kernel,
        grid=(T_ALL // TOK_TILE,),
        in_specs=[xc_spec, xd_spec, _full((8, 6 * D), 1), _full((2, D, LANE), 1), _full((1, LANE), 1)],
        out_specs=[row_spec(D), row_spec(D), row_spec(LANE)],
        out_shape=[jax.ShapeDtypeStruct((T_ALL, D), f32), jax.ShapeDtypeStruct((T_ALL, D), bf16),
                   jax.ShapeDtypeStruct((T_ALL, LANE), f32)],
        compiler_params=_params(1),
        name="router",
    )(x1c, x1d, mods_l, router_w_p, router_b_p)


MOE_TOK = 1536
MOE_EG = 8
MOE_VMEM_LIMIT = 60 * 1024 * 1024
MOE_TILE = 512
MOE_FF = MOE_EG * EXPERT_FF


def _moe_kernel(h_ref, gates_ref, mods_ref, wg_ref, wu_ref, wd_ref, sg_ref, su_ref, sd_ref, o_ref):
    p = pl.program_id(0)
    e = pl.program_id(1)
    n_tiles = MOE_TOK // MOE_TILE

    def gate_f(t):
        r = _mod_row(p * n_tiles + t, MOE_TILE)
        return mods_ref[pl.ds(r, 1), 5 * D:6 * D]

    def rows_of(t):
        if isinstance(t, int):
            return pl.ds(t * MOE_TILE, MOE_TILE)
        return pl.ds(pl.multiple_of(t * MOE_TILE, MOE_TILE), MOE_TILE)

    @pl.when(e == 0)
    def _():
        sg = sg_ref[...].astype(bf16)
        su = su_ref[...].astype(bf16)
        sd = sd_ref[...].astype(bf16)

        def body(t, c):
            rows = rows_of(t)
            ht = h_ref[rows, :]
            hid = _silu(_dot(ht, sg)) * _dot(ht, su)
            o_ref[rows, :] = gate_f(t) * _dot(hid.astype(bf16), sd)
            return c

        lax.fori_loop(0, n_tiles, body, 0)

    wg = jnp.concatenate([wg_ref[k].astype(bf16) for k in range(MOE_EG)], axis=1)
    wu = jnp.concatenate([wu_ref[k].astype(bf16) for k in range(MOE_EG)], axis=1)
    wd = jnp.concatenate([wd_ref[k].astype(bf16) for k in range(MOE_EG)], axis=0)
    lane = lax.broadcasted_iota(jnp.int32, (MOE_TILE, LANE), 1)

    def body(t, c):
        rows = rows_of(t)
        ht = h_ref[rows, :]
        hid = _silu(_dot(ht, wg)) * _dot(ht, wu)
        gt = gates_ref[rows, :]
        parts = []
        for k in range(MOE_EG):
            col = jnp.where(lane == e * MOE_EG + k, gt, 0.0).sum(axis=-1, keepdims=True)
            parts.append((hid[:, EXPERT_FF * k:EXPERT_FF * (k + 1)] * col).astype(bf16))
        o_ref[rows, :] += gate_f(t) * _dot(jnp.concatenate(parts, axis=1), wd)
        return c

    for t in range(n_tiles):
        body(t, 0)


def _moe(l, h, gates, mods_l, wg, wu, wd, sg, su, sd):
    tok = lambda w: pl.BlockSpec((MOE_TOK, w), lambda p, e: (p, 0))
    return pl.pallas_call(
        _moe_kernel,
        grid=(T_ALL // MOE_TOK, N_EXPERTS // MOE_EG),
        in_specs=[tok(D), tok(LANE), _full((8, 6 * D), 2),
                  pl.BlockSpec((None, MOE_EG, D, EXPERT_FF), lambda p, e: (l, e, 0, 0)),
                  pl.BlockSpec((None, MOE_EG, D, EXPERT_FF), lambda p, e: (l, e, 0, 0)),
                  pl.BlockSpec((None, MOE_EG, EXPERT_FF, D), lambda p, e: (l, e, 0, 0)),
                  pl.BlockSpec((None, D, SHARED_FF), lambda p, e: (l, 0, 0)),
                  pl.BlockSpec((None, D, SHARED_FF), lambda p, e: (l, 0, 0)),
                  pl.BlockSpec((None, SHARED_FF, D), lambda p, e: (l, 0, 0))],
        out_specs=tok(D),
        out_shape=jax.ShapeDtypeStruct((T_ALL, D), f32),
        compiler_params=pltpu.CompilerParams(dimension_semantics=("arbitrary", "arbitrary"),
                                             vmem_limit_bytes=MOE_VMEM_LIMIT),
        name="moe",
    )(h, gates, mods_l, wg, wu, wd, sg, su, sd)


def _moe_finish_kernel(x_ref, acc_ref, g_ref, b_ref, o_ref):
    o_ref[...] = _layer_norm(ALPHA * x_ref[...] + acc_ref[...], g_ref[...], b_ref[...])


def _moe_finish(x1, acc, tile0, n_rows, ln_g, ln_b):
    src = pl.BlockSpec((TOK_TILE, D), lambda i: (tile0 + i, 0))
    return pl.pallas_call(
        _moe_finish_kernel,
        grid=(n_rows // TOK_TILE,),
        in_specs=[src, src, _full((1, D), 1), _full((1, D), 1)],
        out_specs=pl.BlockSpec((TOK_TILE, D), lambda i: (i, 0)),
        out_shape=jax.ShapeDtypeStruct((n_rows, D), f32),
        compiler_params=_params(1),
        name="moe_finish",
    )(x1, acc, ln_g, ln_b)


def _router_weights(l, router_w, router_bias):
    rw = jnp.pad(router_w[l], ((0, 0), (0, LANE - N_EXPERTS)))
    rw_hi, rw_lo = _split_bf16(rw)
    rb = jnp.pad(router_bias[l], (0, LANE - N_EXPERTS)).reshape(1, LANE)
    return jnp.stack([rw_hi, rw_lo]), rb


S5_Q = 8
S5_NGB = D // LANE


def _s5_in_kernel(x_ref, acc_ref, lg_ref, lb_ref, mods_ref, w_ref, x2_ref, u_ref, u2_ref, slab_ref, *, row_of,
                  seq_len):
    r = row_of(pl.program_id(0))
    mrow = mods_ref[pl.ds(r, 1), :]
    sh, sc = mrow[:, 0:D], mrow[:, D:2 * D]
    x2 = _layer_norm(ALPHA * x_ref[...] + acc_ref[...], lg_ref[...], lb_ref[...])
    x2_ref[...] = x2
    h = (x2 * (1.0 + sc) + sh).astype(bf16)
    u = _dot(h, w_ref[...])
    u_ref[...] = u
    for s in range(S5_NGB):
        slab_ref[s] = u[:, LANE * s:LANE * (s + 1)]
    kt = seq_len // S5_Q
    for s in range(S5_NGB):
        for q in range(TOK_TILE // seq_len):
            for j in range(S5_Q):
                u2_ref[s, q * kt:(q + 1) * kt, LANE * j:LANE * (j + 1)] = (
                    slab_ref[s, pl.ds(q * seq_len + j, kt, stride=S5_Q), :].astype(bf16))


def _s5_in(x1, acc, ln_g, ln_b, tile0, mods1, w_in_c, n_b, seq_len, row_of):
    n_tiles = n_b * seq_len // TOK_TILE
    chunks = TOK_TILE // S5_Q
    src = pl.BlockSpec((TOK_TILE, D), lambda i: (tile0 + i, 0))
    dst = pl.BlockSpec((TOK_TILE, D), lambda i: (i, 0))
    return pl.pallas_call(
        functools.partial(_s5_in_kernel, row_of=row_of, seq_len=min(seq_len, TOK_TILE)),
        grid=(n_tiles,),
        in_specs=[src, src, _full((1, D), 1), _full((1, D), 1), _full((8, 6 * D), 1), _full((D, D), 1)],
        out_specs=[dst, dst, pl.BlockSpec((S5_NGB, chunks, D), lambda i: (0, i, 0))],
        out_shape=[jax.ShapeDtypeStruct((n_b * seq_len, D), f32), jax.ShapeDtypeStruct((n_b * seq_len, D), f32),
                   jax.ShapeDtypeStruct((S5_NGB, n_tiles * chunks, D), bf16)],
        scratch_shapes=[pltpu.VMEM((S5_NGB, TOK_TILE, LANE), f32)],
        compiler_params=_params(1),
        name="s5_in",
    )(x1, acc, ln_g, ln_b, mods1, w_in_c)


S5_GL = (LANE // S5_CH) * S5_P
S5_ROWS_C = (CTX_L // S5_Q) * N_CTX_B
S5_ROWS_D = (DEN_L // S5_Q) * N_DEN_B


def _s5_scan_kernel(lre_ref, lim_ref, ldt_ref, btr_ref, bti_ref, ctr_ref, cti_ref, uc_ref, ud_ref, h0_ref,
                    yc_ref, yd_ref, st_ref, win_ref, mso_ref, wit_ref, a_ref, s_ref, hp_ref):
    gl = S5_GL
    rowg = lax.shift_right_logical(lax.broadcasted_iota(jnp.int32, (LANE, gl), 0), 4)
    colg = lax.shift_right_logical(lax.broadcasted_iota(jnp.int32, (LANE, gl), 1), 6)
    same_group = rowg == colg
    reps = LANE // S5_CH

    def expand(t):
        return jnp.where(same_group, jnp.concatenate([t] * reps, axis=0), 0.0)

    def expand_c(t):
        return jnp.where(same_group, jnp.concatenate([t] * reps, axis=1), 0.0)

    for d in range(2):
        fwd = d == 0
        lre, lim = lre_ref[d], lim_ref[d]
        dt = jnp.exp(ldt_ref[d])
        a, w = lre * dt, lim * dt
        pre = [jnp.exp(m * a) * jnp.cos(m * w) for m in range(S5_Q + 1)]
        pim = [jnp.exp(m * a) * jnp.sin(m * w) for m in range(S5_Q + 1)]
        xr, xi = pre[1] - 1.0, pim[1]
        den = lre * lre + lim * lim
        cfr, cfi = (xr * lre + xi * lim) / den, (xi * lre - xr * lim) / den
        btr, bti = btr_ref[d], bti_ref[d]
        bexp_r = expand(cfr * btr - cfi * bti)
        bexp_i = expand(cfr * bti + cfi * btr)
        cexp_r, cexp_i = expand_c(ctr_ref[d]), expand_c(cti_ref[d])
        for m in range(S5_Q + 1):
            a_ref[m, :, 0:gl] = cexp_r * pre[m] - cexp_i * pim[m]
            a_ref[m, :, gl:2 * gl] = -(cexp_r * pim[m] + cexp_i * pre[m])
        for j in range(S5_Q):
            m = S5_Q - 1 - j if fwd else j
            win_ref[LANE * j:LANE * (j + 1), 0:gl] = (pre[m] * bexp_r - pim[m] * bexp_i).astype(bf16)
            win_ref[LANE * j:LANE * (j + 1), gl:2 * gl] = (pre[m] * bexp_i + pim[m] * bexp_r).astype(bf16)
        for j in range(S5_Q):
            m = j + 1 if fwd else S5_Q - j
            mso_ref[LANE * j:LANE * (j + 1), :] = a_ref[m].astype(bf16)
        b2_hi, b2_lo = _split_bf16(jnp.concatenate([bexp_r, bexp_i], axis=1))
        kt = []
        for tau in range(S5_Q):
            a_hi, a_lo = _split_bf16(a_ref[tau])
            kt.append(_dot_nt(b2_hi, a_hi) + _dot_nt(b2_hi, a_lo) + _dot_nt(b2_lo, a_hi))
        for j in range(S5_Q):
            for jp in range(S5_Q):
                tau = jp - j if fwd else j - jp
                blk = slice(LANE * j, LANE * (j + 1)), slice(LANE * jp, LANE * (jp + 1))
                if fwd:
                    wit_ref[blk] = kt[tau] if tau >= 0 else jnp.zeros((LANE, LANE), f32)
                elif tau >= 0:
                    wit_ref[blk] = wit_ref[blk] + kt[tau]

        l8r, l8i = pre[S5_Q], pim[S5_Q]

        nsl = gl // LANE

        def slabs(ref, rs, first):
            return jnp.concatenate([ref[first + sl, rs, :] for sl in range(nsl)], axis=1)

        def put_slabs(ref, rs, first, val):
            for sl in range(nsl):
                ref[first + sl, rs, :] = val[:, LANE * sl:LANE * (sl + 1)]

        def advance(hr, hi_, sr, si):
            return l8r * hr - l8i * hi_ + sr, l8r * hi_ + l8i * hr + si

        def run(u_ref, y_ref, n_b, n_k, h_init):
            rows = n_b * n_k
            s = _dot(u_ref[0], win_ref[...])
            if n_b % SUB == 0:
                pitch = n_k + 1
                for bb in range(n_b):
                    dst = slice(bb * pitch, bb * pitch + n_k)
                    put_slabs(s_ref, dst, 0, s[bb * n_k:(bb + 1) * n_k, 0:gl])
                    put_slabs(s_ref, dst, nsl, s[bb * n_k:(bb + 1) * n_k, gl:2 * gl])

                def step(i, carry):
                    hr, hi_ = carry
                    rs = pl.ds(i if fwd else n_k - 1 - i, n_b, stride=pitch)
                    put_slabs(hp_ref, rs, 0, hr)
                    put_slabs(hp_ref, rs, nsl, hi_)
                    return advance(hr, hi_, slabs(s_ref, rs, 0), slabs(s_ref, rs, nsl))

                h_fin = lax.fori_loop(0, n_k, step, h_init)
                hp = jnp.concatenate(
                    [jnp.concatenate([hp_ref[sl, bb * pitch:bb * pitch + n_k, :] for sl in range(2 * nsl)], axis=1)
                     for bb in range(n_b)], axis=0).astype(bf16)
            else:
                put_slabs(s_ref, slice(0, rows), 0, s[:, 0:gl])
                put_slabs(s_ref, slice(0, rows), nsl, s[:, gl:2 * gl])
                n_it = n_k // SUB

                def step(i, carry):
                    it = i if fwd else n_it - 1 - i
                    out = []
                    for bb in range(n_b):
                        hr, hi_ = carry[bb]
                        rs = pl.ds(pl.multiple_of(bb * n_k + it * SUB, SUB), SUB)
                        s_re, s_im = slabs(s_ref, rs, 0), slabs(s_ref, rs, nsl)
                        prev_r, prev_i = [None] * SUB, [None] * SUB
                        for sub in (range(SUB) if fwd else reversed(range(SUB))):
                            prev_r[sub], prev_i[sub] = hr, hi_
                            hr, hi_ = advance(hr, hi_, s_re[sub:sub + 1], s_im[sub:sub + 1])
                        put_slabs(hp_ref, rs, 0, jnp.concatenate(prev_r, axis=0))
                        put_slabs(hp_ref, rs, nsl, jnp.concatenate(prev_i, axis=0))
                        out.append((hr, hi_))
                    return tuple(out)

                fin = lax.fori_loop(0, n_it, step, tuple((h_init[0][bb:bb + 1], h_init[1][bb:bb + 1])
                                                          for bb in range(n_b)))
                h_fin = (jnp.concatenate([f[0] for f in fin], axis=0), jnp.concatenate([f[1] for f in fin], axis=0))
                hp = jnp.concatenate([hp_ref[sl, 0:rows, :] for sl in range(2 * nsl)], axis=1).astype(bf16)
            y = _dot_nt(hp, mso_ref[...])
            if fwd:
                y_ref[0] = y
            else:
                y_ref[0] += y
            return h_fin

        zeros = jnp.zeros((N_CTX_B, gl), f32)
        hr, hi_ = run(uc_ref, yc_ref, N_CTX_B, CTX_L // S5_Q, (zeros, zeros))
        st_ref[d, 0] = hr
        st_ref[d, 1] = hi_
        run(ud_ref, yd_ref, N_DEN_B, DEN_L // S5_Q, (h0_ref[d, 0], h0_ref[d, 1]))

    wit = wit_ref[...].astype(bf16)
    yc_ref[0] += _dot(uc_ref[0], wit)
    yd_ref[0] += _dot(ud_ref[0], wit)


def _s5_scan(lam_re, lam_im, log_dt, bt_re, bt_im, ct_re, ct_im, u2c, u2d, h0):
    gl = S5_GL
    vec = pl.BlockSpec((2, 1, gl), lambda g: (0, 0, g))
    tab = pl.BlockSpec((2, S5_CH, gl), lambda g: (0, 0, g))
    ctab = pl.BlockSpec((2, LANE, S5_P), lambda g: (0, g, 0))
    rows = lambda n: pl.BlockSpec((1, n, D), lambda g: (g, 0, 0))
    return pl.pallas_call(
        _s5_scan_kernel,
        grid=(S5_NGB,),
        in_specs=[vec, vec, vec, tab, tab, ctab, ctab, rows(S5_ROWS_C), rows(S5_ROWS_D),
                  pl.BlockSpec((2, 2, N_DEN_B, gl), lambda g: (0, 0, 0, g))],
        out_specs=[rows(S5_ROWS_C), rows(S5_ROWS_D), pl.BlockSpec((2, 2, N_CTX_B, gl), lambda g: (0, 0, 0, g))],
        out_shape=[jax.ShapeDtypeStruct((S5_NGB, S5_ROWS_C, D), f32), jax.ShapeDtypeStruct((S5_NGB, S5_ROWS_D, D), f32),
                   jax.ShapeDtypeStruct((2, 2, N_CTX_B, S5_G * S5_P), f32)],
        scratch_shapes=[pltpu.VMEM((D, 2 * gl), bf16), pltpu.VMEM((D, 2 * gl), bf16), pltpu.VMEM((D, D), f32),
                        pltpu.VMEM((S5_Q + 1, LANE, 2 * gl), f32),
                        pltpu.VMEM((2 * gl // LANE, S5_ROWS_C + 2 * N_CTX_B, LANE), f32),
                        pltpu.VMEM((2 * gl // LANE, S5_ROWS_C + 2 * N_CTX_B, LANE), f32)],
        compiler_params=_params(1),
        name="s5_scan",
    )(lam_re, lam_im, log_dt, bt_re, bt_im, ct_re, ct_im, u2c, u2d, h0)


def _gelu_tanh(x):
    return 0.5 * x * (1.0 + jnp.tanh(np.sqrt(2.0 / np.pi).astype(np.float32) * (x + 0.044715 * (x * x * x))))


def _s5_out_kernel(xc_ref, xd_ref, uc_ref, ud_ref, yc_ref, yd_ref, mods_ref, dsk_ref, wout_ref, g_ref, b_ref, rw_ref,
                   rb_ref, x1_ref, h_ref, gates_ref, slab_ref):
    i = pl.program_id(0)
    is_ctx = i < T_CTX // TOK_TILE
    r = _mod_row(i, TOK_TILE)
    mrow = mods_ref[pl.ds(r, 1), :]
    u = _pick(i, TOK_TILE, uc_ref, ud_ref)

    def unchunk(y_ref, seq_len):
        kt = seq_len // S5_Q
        for s in range(S5_NGB):
            for q in range(TOK_TILE // seq_len):
                for j in range(S5_Q):
                    slab_ref[s, pl.ds(q * seq_len + j, kt, stride=S5_Q), :] = (
                        y_ref[s, q * kt:(q + 1) * kt, LANE * j:LANE * (j + 1)])

    lax.cond(is_ctx, lambda: unchunk(yc_ref, min(CTX_L, TOK_TILE)), lambda: unchunk(yd_ref, min(DEN_L, TOK_TILE)))
    for rows in (slice(a, a + SUBTILE) for a in range(0, TOK_TILE, SUBTILE)):
        y = jnp.concatenate([slab_ref[s, rows, :] for s in range(S5_NGB)], axis=1) + dsk_ref[...] * u[rows]
        z = _dot(_gelu_tanh(y).astype(bf16), wout_ref[...])
        out = z[:, 0:D] * jax.nn.sigmoid(z[:, D:2 * D])
        x = jnp.where(is_ctx, xc_ref[rows, :], xd_ref[rows, :])
        x1 = _layer_norm(ALPHA * x + mrow[:, 2 * D:3 * D] * out, g_ref[...], b_ref[...])
        x1_ref[rows, :] = x1
        h_ref[rows, :], gates_ref[rows, :] = _route(x1, mrow, rw_ref, rb_ref)


def _s5_out(xc, xd, uc, ud, yc, yd, mods1, d_skip, w_out_c, ln_g, ln_b, rw, rb):
    row_spec = lambda w: pl.BlockSpec((TOK_TILE, w), lambda i: (i, 0))
    uc_spec, ud_spec = _two_stream_specs(TOK_TILE, D)
    n_ctx = T_CTX // TOK_TILE
    chunks = TOK_TILE // S5_Q
    return pl.pallas_call(
        _s5_out_kernel,
        grid=(T_ALL // TOK_TILE,),
        in_specs=[uc_spec, ud_spec, uc_spec, ud_spec,
                  pl.BlockSpec((S5_NGB, chunks, D), lambda i: (0, jnp.minimum(i, n_ctx - 1), 0)),
                  pl.BlockSpec((S5_NGB, chunks, D), lambda i: (0, jnp.maximum(i - n_ctx, 0), 0)),
                  _full((8, 6 * D), 1), _full((1, D), 1), _full((D, 2 * D), 1), _full((1, D), 1), _full((1, D), 1),
                  _full((2, D, LANE), 1), _full((1, LANE), 1)],
        out_specs=[row_spec(D), row_spec(D), row_spec(LANE)],
        out_shape=[jax.ShapeDtypeStruct((T_ALL, D), f32), jax.ShapeDtypeStruct((T_ALL, D), bf16),
                   jax.ShapeDtypeStruct((T_ALL, LANE), f32)],
        scratch_shapes=[pltpu.VMEM((S5_NGB, TOK_TILE, LANE), f32)],
        compiler_params=_params(1),
        name="s5_out",
    )(xc, xd, uc, ud, yc, yd, mods1, d_skip, w_out_c, ln_g, ln_b, rw, rb)


def kernel(x_prompt, x_sample, c, cache_attn_k, cache_attn_v, cache_mla_ckv, cache_mla_krope, state_ssm, c_ctx,
           ada_w, ada_b, ln_mix_g, ln_mix_b, ln_ffn_g, ln_ffn_b, w_in_ab, attn_sink, mla_q_norm, mla_kv_norm,
           mla_w_uq, mla_w_ukv, w_out_ab, w_in_c, s5_lam_re, s5_lam_im, s5_log_dt, s5_b_re, s5_b_im, s5_c_re,
           s5_c_im, s5_d, w_out_c, router_w, router_bias, exp_w_gate, exp_w_up, exp_w_down, sh_w_gate, sh_w_up,
           sh_w_down):
    row = lambda v: v.reshape(1, -1)
    xc, xd = x_prompt.reshape(T_CTX, D), x_sample.reshape(T_DEN, D)
    cvec8 = jnp.concatenate([c_ctx[None, :], c, jnp.zeros((8 - 1 - N_DEN_B, D), f32)], axis=0)
    mods = _adaln(cvec8, ada_w, ada_b)

    w_in_p = jnp.pad(w_in_ab[0], ((0, 0), (0, PROJ_W - w_in_ab.shape[-1]))).astype(bf16)
    uq = mla_w_uq[0].reshape(MLA_Q_RANK, MLA_HEADS, MLA_NOPE + MLA_ROPE)
    w_uq_p = jnp.concatenate([uq[:, :, :MLA_NOPE].reshape(MLA_Q_RANK, -1), uq[:, :, MLA_NOPE:].reshape(MLA_Q_RANK, -1)],
                             axis=1).astype(bf16)
    ukv = mla_w_ukv[0].reshape(MLA_KV_RANK, MLA_HEADS, MLA_NOPE + MLA_V)
    w_ukv_p = jnp.concatenate([ukv[:, :, :MLA_NOPE].reshape(MLA_KV_RANK, -1),
                               ukv[:, :, MLA_NOPE:].reshape(MLA_KV_RANK, -1)], axis=1).astype(bf16)
    qa, ka, va, ckv, kr, qm, kvl = _ab_proj(xc, xd, mods[0], w_in_p, row(mla_q_norm[0]), row(mla_kv_norm[0]),
                                            w_uq_p, w_ukv_p)
    w_out_b = w_out_ab[0].astype(bf16)
    g0, b0 = row(ln_mix_g[0]), row(ln_mix_b[0])
    x1c, nk_t, nv_t = _ctx_attn(attn_sink[0], qa, ka, va, qm, kvl, kr, xc, mods[0], w_out_b, g0, b0)
    new_attn_k = jnp.transpose(nk_t, (0, 1, 4, 2, 3))
    new_attn_v = jnp.transpose(nv_t, (0, 1, 4, 2, 3))
    x1d = _den_attn(attn_sink[0], qa, ka, va,
                    jnp.transpose(cache_attn_k[:, 0], (0, 2, 3, 1)), jnp.transpose(cache_attn_v[:, 0], (0, 2, 3, 1)),
                    qm, kvl, kr, cache_mla_ckv[:, 0], cache_mla_krope[:, 0], w_ukv_p, xd, mods[0], w_out_b, g0, b0)
    rw0, rb0 = _router_weights(0, router_w, router_bias)
    x1, h, gates = _router(x1c, x1d, mods[0], rw0, rb0)
    acc = _moe(0, h, gates, mods[0], exp_w_gate, exp_w_up, exp_w_down, sh_w_gate, sh_w_up, sh_w_down)

    w_in_c_b = w_in_c[0].astype(bf16)
    lg0, lb0 = row(ln_ffn_g[0]), row(ln_ffn_b[0])
    n_ctx_tiles = T_CTX // TOK_TILE
    x2c, uc, u2c = _s5_in(x1, acc, lg0, lb0, 0, mods[1], w_in_c_b, N_CTX_B, CTX_L, lambda i: 0)
    x2d, ud, u2d = _s5_in(x1, acc, lg0, lb0, n_ctx_tiles, mods[1], w_in_c_b, N_DEN_B, DEN_L,
                          lambda i: 1 + i // (DEN_L // TOK_TILE))
    gp = S5_G * S5_P
    chan_major_b = lambda t: jnp.transpose(t[0], (0, 3, 1, 2)).reshape(2, S5_CH, gp)
    chan_major_c = lambda t: t[0].reshape(2, S5_G * S5_CH, S5_P)
    h0 = jnp.transpose(state_ssm[:, 0], (1, 2, 0, 3, 4)).reshape(2, 2, N_DEN_B, gp)
    yc, yd, st = _s5_scan(s5_lam_re[0].reshape(2, 1, gp), s5_lam_im[0].reshape(2, 1, gp),
                          jnp.repeat(s5_log_dt[0], S5_P, axis=-1).reshape(2, 1, gp),
                          chan_major_b(s5_b_re), chan_major_b(s5_b_im), chan_major_c(s5_c_re), chan_major_c(s5_c_im),
                          u2c, u2d, h0)
    rw1, rb1 = _router_weights(1, router_w, router_bias)
    x3, h, gates = _s5_out(x2c, x2d, uc, ud, yc, yd, mods[1], row(s5_d[0]),
                           w_out_c[0].astype(bf16), row(ln_mix_g[1]), row(ln_mix_b[1]), rw1, rb1)
    acc = _moe(1, h, gates, mods[1], exp_w_gate, exp_w_up, exp_w_down, sh_w_gate, sh_w_up, sh_w_down)
    lg1, lb1 = row(ln_ffn_g[1]), row(ln_ffn_b[1])
    y_prompt = _moe_finish(x3, acc, 0, T_CTX, lg1, lb1).reshape(N_CTX_B, CTX_L, D)
    y_sample = _moe_finish(x3, acc, n_ctx_tiles, T_DEN, lg1, lb1).reshape(N_DEN_B, DEN_L, D)
    new_mla_ckv = ckv[:T_CTX].reshape(N_CTX_B, 1, CTX_L, MLA_KV_RANK)
    new_mla_krope = kr[:T_CTX, :MLA_ROPE].reshape(N_CTX_B, 1, CTX_L, MLA_ROPE)
    new_state_ssm = jnp.transpose(st, (2, 0, 1, 3)).reshape(N_CTX_B, 1, 2, 2, S5_G, S5_P)
    return (y_prompt, y_sample, new_attn_k, new_attn_v, new_mla_ckv, new_mla_krope, new_state_ssm)
```

```python
import functools

import jax
import jax.numpy as jnp
import numpy as np
from jax import lax
from jax.experimental import pallas as pl
from jax.experimental.pallas import tpu as pltpu

f32 = jnp.float32
bf16 = jnp.bfloat16

D = 1024
N_CTX_B, CTX_L = 16, 256
N_DEN_B, DEN_L = 2, 1024
T_CTX = N_CTX_B * CTX_L
T_DEN = N_DEN_B * DEN_L
T_ALL = T_CTX + T_DEN
GRID_W = 64
WINDOW = 128
ROPE_BASE = 10000.0
A_HEADS, A_KV_HEADS, A_HD = 8, 2, 64
A_GROUP = A_HEADS // A_KV_HEADS
A_SCALE = A_HD ** -0.5
MLA_HEADS, MLA_Q_RANK, MLA_KV_RANK = 8, 256, 128
MLA_NOPE, MLA_ROPE, MLA_V = 64, 32, 64
MLA_SCALE = (MLA_NOPE + MLA_ROPE) ** -0.5
N_EXPERTS, TOP_K, EXPERT_FF, SHARED_FF = 64, 6, 128, 128
ROUTED_SCALE = 2.5
DEPTH = 2
ALPHA = (2.0 * DEPTH) ** 0.25
LN_EPS = 1e-5
RMS_EPS = 1e-6
NEG_INF = -1e30
S5_G, S5_CH, S5_P = 64, 16, 64

LANE = 128
SUB = 8
VMEM_LIMIT = 56 * 1024 * 1024

TOK_TILE = 512


def _mod_row(tile_idx, tile_rows):
    start = tile_idx * tile_rows
    return jnp.where(start < T_CTX, 0, 1 + (start - T_CTX) // DEN_L)


def _layer_norm(y, g, b):
    mu = jnp.mean(y, axis=-1, keepdims=True)
    yc = y - mu
    var = jnp.mean(yc * yc, axis=-1, keepdims=True)
    return yc * lax.rsqrt(var + LN_EPS) * g + b


def _silu(x):
    return x * jax.nn.sigmoid(x)


def _dot(a, b):
    return jnp.dot(a, b, preferred_element_type=f32)


def _dot_nt(a, b):
    return lax.dot_general(a, b, (((1,), (1,)), ((), ())), preferred_element_type=f32)


def _split_bf16(a):
    hi = a.astype(bf16)
    return hi, (a - hi.astype(f32)).astype(bf16)


def _full(shape, n_grid):
    zeros = tuple(0 for _ in shape)
    return pl.BlockSpec(shape, lambda *_: zeros)


def _two_stream_specs(tile_rows, width):
    n_ctx = T_CTX // tile_rows
    return (pl.BlockSpec((tile_rows, width), lambda i: (jnp.minimum(i, n_ctx - 1), 0)),
            pl.BlockSpec((tile_rows, width), lambda i: (jnp.maximum(i - n_ctx, 0), 0)))


def _pick(i, tile_rows, ctx_ref, den_ref):
    return lax.cond(i < T_CTX // tile_rows, lambda: ctx_ref[...], lambda: den_ref[...])


def _params(n_grid):
    return pltpu.CompilerParams(dimension_semantics=("arbitrary",) * n_grid, vmem_limit_bytes=VMEM_LIMIT)


ADA_TN = 1536


def _adaln_kernel(c_ref, w_ref, b_ref, o_ref):
    s_hi, s_lo = _split_bf16(_silu(c_ref[...]))
    w_hi, w_lo = _split_bf16(w_ref[0])
    o_ref[0] = _dot(s_hi, w_hi) + (_dot(s_hi, w_lo) + _dot(s_lo, w_hi)) + b_ref[0]


def _adaln(cvec8, ada_w, ada_b):
    n = 6 * D
    return pl.pallas_call(
        _adaln_kernel,
        grid=(DEPTH, n // ADA_TN),
        in_specs=[
            pl.BlockSpec((8, D), lambda l, j: (0, 0)),
            pl.BlockSpec((1, D, ADA_TN), lambda l, j: (l, 0, j)),
            pl.BlockSpec((1, 1, ADA_TN), lambda l, j: (l, 0, j)),
        ],
        out_specs=pl.BlockSpec((1, 8, ADA_TN), lambda l, j: (l, 0, j)),
        out_shape=jax.ShapeDtypeStruct((DEPTH, 8, n), f32),
        compiler_params=_params(2),
        name="adaln",
    )(cvec8, ada_w, ada_b.reshape(DEPTH, 1, n))


def _rope_table_array(head_dim):
    q = head_dim // 4
    pos = np.arange(DEN_L)
    row, col = (pos // GRID_W).astype(np.float64), (pos % GRID_W).astype(np.float64)
    lane = np.arange(LANE) % head_dim
    is_col = lane >= head_dim // 2
    w = lane % (head_dim // 2)
    first = w < q
    inv_freq = ROPE_BASE ** (-np.arange(q, dtype=np.float64) / q)
    ang = np.where(is_col[None, :], col[:, None], row[:, None]) * inv_freq[w % q][None, :]
    cos, sin = np.cos(ang), np.sin(ang)
    sin_a = np.where(first[None, :], -sin, 0.0)
    sin_b = np.where(first[None, :], 0.0, sin)
    ident = np.stack([np.ones((TOK_TILE, LANE)), np.zeros((TOK_TILE, LANE)), np.zeros((TOK_TILE, LANE))])
    tab = np.concatenate([ident, np.stack([cos, sin_a, sin_b])], axis=1).astype(np.float32)
    return jnp.asarray(tab), q


def _rope_chunk(x, tab_ref, q):
    return x * tab_ref[0] + pltpu.roll(x, LANE - q, 1) * tab_ref[1] + pltpu.roll(x, q, 1) * tab_ref[2]


PROJ_W = 1280
C_QA, C_KA, C_VA, C_CQ, C_CKV, C_KR = 0, 512, 640, 768, 1024, 1152
MLA_NN = MLA_HEADS * MLA_NOPE


def _ab_proj_kernel(xc_ref, xd_ref, mods_ref, w_ref, qn_ref, kvn_ref, wuq_ref, wukv_ref, ta_ref, tm_ref,
                    qa_ref, ka_ref, va_ref, ckv_ref, kr_ref, qm_ref, kvl_ref, *, qa_shift, qm_shift):
    i = pl.program_id(0)
    r = _mod_row(i, TOK_TILE)
    mrow = mods_ref[pl.ds(r, 1), :]
    sh, sc = mrow[:, 0:D], mrow[:, D:2 * D]
    x = _pick(i, TOK_TILE, xc_ref, xd_ref)
    h = (x * (1.0 + sc) + sh).astype(bf16)
    proj = _dot(h, w_ref[...])
    for j in range(4):
        c0 = C_QA + LANE * j
        qa_ref[:, LANE * j:LANE * (j + 1)] = _rope_chunk(proj[:, c0:c0 + LANE], ta_ref, qa_shift).astype(bf16)
    ka_ref[...] = _rope_chunk(proj[:, C_KA:C_KA + LANE], ta_ref, qa_shift)
    va_ref[...] = proj[:, C_VA:C_VA + LANE]
    cq = proj[:, C_CQ:C_CQ + MLA_Q_RANK]
    cq = cq * lax.rsqrt(jnp.mean(cq * cq, axis=-1, keepdims=True) + RMS_EPS) * qn_ref[...]
    ckv = proj[:, C_CKV:C_CKV + MLA_KV_RANK]
    ckv = ckv * lax.rsqrt(jnp.mean(ckv * ckv, axis=-1, keepdims=True) + RMS_EPS) * kvn_ref[...]
    ckv_ref[...] = ckv
    kr_ref[...] = _rope_chunk(proj[:, C_KR:C_KR + LANE], tm_ref, qm_shift)
    qm = _dot(cq.astype(bf16), wuq_ref[...])
    qm_ref[:, 0:MLA_NN] = qm[:, 0:MLA_NN].astype(bf16)
    for j in range(2):
        c0 = MLA_NN + LANE * j
        qm_ref[:, c0:c0 + LANE] = _rope_chunk(qm[:, c0:c0 + LANE], tm_ref, qm_shift).astype(bf16)
    kvl_ref[...] = _dot(ckv.astype(bf16), wukv_ref[...]).astype(bf16)


def _rope_block_index(i):
    tiles_ctx = T_CTX // TOK_TILE
    per_seq = DEN_L // TOK_TILE
    return jnp.where(i < tiles_ctx, 0, 1 + (i - tiles_ctx) % per_seq)


def _ab_proj(xc, xd, mods0, w_in_p, q_norm, kv_norm, w_uq_p, w_ukv_p):
    tab_a, qa_shift = _rope_table_array(A_HD)
    tab_m, qm_shift = _rope_table_array(MLA_ROPE)
    row_spec = lambda w: pl.BlockSpec((TOK_TILE, w), lambda i: (i, 0))
    xc_spec, xd_spec = _two_stream_specs(TOK_TILE, D)
    tab_spec = pl.BlockSpec((3, TOK_TILE, LANE), lambda i: (0, _rope_block_index(i), 0))
    outs = [(512, bf16), (LANE, f32), (LANE, f32), (LANE, f32), (LANE, f32), (768, bf16), (1024, bf16)]
    return pl.pallas_call(
        functools.partial(_ab_proj_kernel, qa_shift=qa_shift, qm_shift=qm_shift),
        grid=(T_ALL // TOK_TILE,),
        in_specs=[xc_spec, xd_spec, _full((8, 6 * D), 1), _full((D, PROJ_W), 1), _full((1, MLA_Q_RANK), 1),
                  _full((1, MLA_KV_RANK), 1), _full((MLA_Q_RANK, 768), 1), _full((MLA_KV_RANK, 1024), 1),
                  tab_spec, tab_spec],
        out_specs=[row_spec(w) for w, _ in outs],
        out_shape=[jax.ShapeDtypeStruct((T_ALL, w), dt) for w, dt in outs],
        compiler_params=_params(1),
        name="ab_proj",
    )(xc, xd, mods0, w_in_p, q_norm, kv_norm, w_uq_p, w_ukv_p, tab_a, tab_m)


def _softmax_blocks(s_refs, p_refs, sink_col=None):
    m = s_refs[0][...].max(axis=-1, keepdims=True)
    for s_ref in s_refs[1:]:
        m = jnp.maximum(m, s_ref[...].max(axis=-1, keepdims=True))
    if sink_col is not None:
        m = jnp.maximum(m, sink_col)
    l = None
    for s_ref, p_ref in zip(s_refs, p_refs):
        p = jnp.exp(s_ref[...] - m).astype(bf16)
        p_ref[...] = p
        ps = _dot(p, jnp.ones((p.shape[1], LANE), bf16))[:, 0:1]
        l = ps if l is None else l + ps
    if sink_col is not None:
        l = l + jnp.exp(sink_col - m)
    return 1.0 / l


def _sink_column(sink_ref, rows_per_head):
    return jnp.concatenate([jnp.full((rows_per_head, 1), sink_ref[h], f32) for h in range(A_HEADS)], axis=0)


def _mla_q(qm_ref, h):
    rows = qm_ref.shape[0]
    return jnp.concatenate([qm_ref[:, MLA_NOPE * h:MLA_NOPE * (h + 1)],
                            qm_ref[:, MLA_NN + MLA_ROPE * h:MLA_NN + MLA_ROPE * (h + 1)],
                            jnp.zeros((rows, LANE - MLA_NOPE - MLA_ROPE), bf16)], axis=1)


def _mla_k(k_nope_h, k_rope):
    rows = k_nope_h.shape[0]
    return jnp.concatenate([k_nope_h, k_rope, jnp.zeros((rows, LANE - MLA_NOPE - MLA_ROPE), bf16)], axis=1)


def _mix_out_ln(merged_ref, wout_ref, x, mods_ref, r, g_ref, b_ref):
    out = _dot(merged_ref[...], wout_ref[...])
    gate = mods_ref[pl.ds(r, 1), 2 * D:3 * D]
    return _layer_norm(ALPHA * x + gate * out, g_ref[...], b_ref[...])


def _ctx_attn_kernel(sink_ref, qa_ref, ka_ref, va_ref, qm_ref, kvl_ref, kr_ref, x_ref, mods_ref, wout_ref,
                     g_ref, b_ref, o_ref, nk_ref, nv_ref, merged_ref, sa_ref, sm_ref, pa_ref, pm_ref):
    nk_ref[0, 0] = ka_ref[...].T.reshape(A_KV_HEADS, A_HD, CTX_L)
    nv_ref[0, 0] = va_ref[...].T.reshape(A_KV_HEADS, A_HD, CTX_L)
    n = CTX_L
    ka = ka_ref[...].astype(bf16)
    va = va_ref[...].astype(bf16)
    for j in range(A_KV_HEADS):
        q4 = jnp.concatenate([qa_ref[:, A_HD * h:A_HD * (h + 1)] for h in range(A_GROUP * j, A_GROUP * (j + 1))],
                             axis=0)
        sa_ref[A_GROUP * n * j:A_GROUP * n * (j + 1), :] = _dot_nt(q4, ka[:, A_HD * j:A_HD * (j + 1)]) * A_SCALE
    kr = kr_ref[:, 0:MLA_ROPE].astype(bf16)
    for h in range(MLA_HEADS):
        k_cat = _mla_k(kvl_ref[:, MLA_NOPE * h:MLA_NOPE * (h + 1)], kr)
        sm_ref[n * h:n * (h + 1), :] = _dot_nt(_mla_q(qm_ref, h), k_cat) * MLA_SCALE
    rla = _softmax_blocks([sa_ref], [pa_ref], _sink_column(sink_ref, n))
    rlm = _softmax_blocks([sm_ref], [pm_ref])
    for j in range(A_KV_HEADS):
        rows = slice(A_GROUP * n * j, A_GROUP * n * (j + 1))
        o4 = _dot(pa_ref[rows, :], va[:, A_HD * j:A_HD * (j + 1)]) * rla[rows]
        for g in range(A_GROUP):
            h = A_GROUP * j + g
            merged_ref[:, A_HD * h:A_HD * (h + 1)] = o4[n * g:n * (g + 1)].astype(bf16)
    for h in range(MLA_HEADS):
        rows = slice(n * h, n * (h + 1))
        v = kvl_ref[:, MLA_NN + MLA_V * h:MLA_NN + MLA_V * (h + 1)]
        merged_ref[:, MLA_NN + MLA_V * h:MLA_NN + MLA_V * (h + 1)] = (_dot(pm_ref[rows, :], v) * rlm[rows]).astype(bf16)
    o_ref[...] = _mix_out_ln(merged_ref, wout_ref, x_ref[...], mods_ref, 0, g_ref, b_ref)


def _ctx_attn(sink, qa, ka, va, qm, kvl, kr, x_all, mods0, w_out, ln_g, ln_b):
    blk = lambda w: pl.BlockSpec((CTX_L, w), lambda b: (b, 0))
    cache_blk = pl.BlockSpec((1, 1, A_KV_HEADS, A_HD, CTX_L), lambda b: (b, 0, 0, 0, 0))
    cache_shape = jax.ShapeDtypeStruct((N_CTX_B, 1, A_KV_HEADS, A_HD, CTX_L), f32)
    return pl.pallas_call(
        _ctx_attn_kernel,
        grid=(N_CTX_B,),
        in_specs=[pl.BlockSpec(memory_space=pltpu.SMEM), blk(512), blk(LANE), blk(LANE), blk(768), blk(1024),
                  blk(LANE), blk(D), _full((8, 6 * D), 1), _full((D, D), 1), _full((1, D), 1), _full((1, D), 1)],
        out_specs=[blk(D), cache_blk, cache_blk],
        out_shape=[jax.ShapeDtypeStruct((T_CTX, D), f32), cache_shape, cache_shape],
        scratch_shapes=[pltpu.VMEM((CTX_L, D), bf16),
                        pltpu.VMEM((A_HEADS * CTX_L, CTX_L), f32), pltpu.VMEM((MLA_HEADS * CTX_L, CTX_L), f32),
                        pltpu.VMEM((A_HEADS * CTX_L, CTX_L), bf16), pltpu.VMEM((MLA_HEADS * CTX_L, CTX_L), bf16)],
        compiler_params=_params(1),
        name="ctx_attn",
    )(sink, qa, ka, va, qm, kvl, kr, x_all, mods0, w_out, ln_g, ln_b)


QB = 256
WIN = QB + 2 * WINDOW
DEN_BLK0 = T_CTX // DEN_L
MLA_KEYS = CTX_L + DEN_L


def _den_attn_kernel(sink_ref, qa_ref, ka_ref, va_ref, cak_ref, cav_ref, qm_ref, kvl_ref, kr_ref, cckv_ref, ckr_ref,
                     wukv_ref, x_ref, mods_ref, wout_ref, g_ref, b_ref, o_ref, merged_ref, kcat_ref, vcat_ref,
                     saw_ref, sac_ref, sm_ref, paw_ref, pac_ref, pm_ref):
    b = pl.program_id(0)
    n = pl.program_id(1)

    @pl.when(n == 0)
    def _():
        kvc = _dot(cckv_ref[0].astype(bf16), wukv_ref[...]).astype(bf16)
        kr_ctx = ckr_ref[0].astype(bf16)
        kr_lat = kr_ref[:, 0:MLA_ROPE].astype(bf16)
        for h in range(MLA_HEADS):
            ns = slice(MLA_NOPE * h, MLA_NOPE * (h + 1))
            kcat_ref[h, 0:CTX_L, :] = _mla_k(kvc[:, ns], kr_ctx)
            kcat_ref[h, CTX_L:MLA_KEYS, :] = _mla_k(kvl_ref[:, ns], kr_lat)
        vcat_ref[0:CTX_L, :] = kvc[:, MLA_NN:2 * MLA_NN]
        vcat_ref[CTX_L:MLA_KEYS, :] = kvl_ref[:, MLA_NN:2 * MLA_NN]

    start = pl.multiple_of(jnp.clip(QB * n - WINDOW, 0, DEN_L - WIN), WINDOW)
    grp_rows = A_GROUP * QB
    qpos = QB * n + (lax.broadcasted_iota(jnp.int32, (grp_rows, WIN), 0) & (QB - 1))
    kpos = start + lax.broadcasted_iota(jnp.int32, (grp_rows, WIN), 1)
    valid = jnp.abs(qpos - kpos) <= WINDOW
    kwin = ka_ref[pl.ds(start, WIN), :].astype(bf16)
    vwin = va_ref[pl.ds(start, WIN), :].astype(bf16)
    kctx_t = [cak_ref[0, j].astype(bf16) for j in range(A_KV_HEADS)]
    vctx_t = [cav_ref[0, j].astype(bf16) for j in range(A_KV_HEADS)]
    for j in range(A_KV_HEADS):
        sl = slice(A_HD * j, A_HD * (j + 1))
        rows = slice(grp_rows * j, grp_rows * (j + 1))
        q4 = jnp.concatenate([qa_ref[:, A_HD * h:A_HD * (h + 1)] for h in range(A_GROUP * j, A_GROUP * (j + 1))],
                             axis=0)
        saw_ref[rows, :] = jnp.where(valid, _dot_nt(q4, kwin[:, sl]) * A_SCALE, NEG_INF)
        sac_ref[rows, :] = _dot(q4, kctx_t[j]) * A_SCALE
    for h in range(MLA_HEADS):
        sm_ref[QB * h:QB * (h + 1), :] = _dot_nt(_mla_q(qm_ref, h), kcat_ref[h]) * MLA_SCALE
    rla = _softmax_blocks([saw_ref, sac_ref], [paw_ref, pac_ref], _sink_column(sink_ref, QB))
    rlm = _softmax_blocks([sm_ref], [pm_ref])
    for j in range(A_KV_HEADS):
        sl = slice(A_HD * j, A_HD * (j + 1))
        rows = slice(grp_rows * j, grp_rows * (j + 1))
        o4 = (_dot(paw_ref[rows, :], vwin[:, sl]) + _dot_nt(pac_ref[rows, :], vctx_t[j])) * rla[rows]
        for g in range(A_GROUP):
            h = A_GROUP * j + g
            merged_ref[:, A_HD * h:A_HD * (h + 1)] = o4[QB * g:QB * (g + 1)].astype(bf16)
    for h in range(MLA_HEADS):
        rows = slice(QB * h, QB * (h + 1))
        o = _dot(pm_ref[rows, :], vcat_ref[:, MLA_V * h:MLA_V * (h + 1)]) * rlm[rows]
        merged_ref[:, MLA_NN + MLA_V * h:MLA_NN + MLA_V * (h + 1)] = o.astype(bf16)
    o_ref[...] = _mix_out_ln(merged_ref, wout_ref, x_ref[...], mods_ref, 1 + b, g_ref, b_ref)


def _den_attn(sink, qa, ka, va, cache_k, cache_v, qm, kvl, kr, cache_ckv, cache_kr, w_ukv_p, x_all, mods0, w_out,
              ln_g, ln_b):
    nq = DEN_L // QB
    qblk = lambda w: pl.BlockSpec((QB, w), lambda b, n: (T_CTX // QB + b * nq + n, 0))
    seq = lambda w: pl.BlockSpec((DEN_L, w), lambda b, n: (DEN_BLK0 + b, 0))
    cache = lambda w: pl.BlockSpec((1, CTX_L, w), lambda b, n: (b, 0, 0))
    cache_a = pl.BlockSpec((1, A_KV_HEADS, A_HD, CTX_L), lambda b, n: (b, 0, 0, 0))
    return pl.pallas_call(
        _den_attn_kernel,
        grid=(N_DEN_B, nq),
        in_specs=[pl.BlockSpec(memory_space=pltpu.SMEM), qblk(512), seq(LANE), seq(LANE), cache_a, cache_a,
                  qblk(768), seq(1024), seq(LANE), cache(MLA_KV_RANK), cache(MLA_ROPE),
                  _full((MLA_KV_RANK, 1024), 2), pl.BlockSpec((QB, D), lambda b, n: (b * nq + n, 0)),
                  _full((8, 6 * D), 2), _full((D, D), 2), _full((1, D), 2),
                  _full((1, D), 2)],
        out_specs=pl.BlockSpec((QB, D), lambda b, n: (b * nq + n, 0)),
        out_shape=jax.ShapeDtypeStruct((T_DEN, D), f32),
        scratch_shapes=[pltpu.VMEM((QB, D), bf16), pltpu.VMEM((MLA_HEADS, MLA_KEYS, LANE), bf16),
                        pltpu.VMEM((MLA_KEYS, MLA_NN), bf16)]
        + [pltpu.VMEM((A_HEADS * QB, w), dt) for dt in (f32, bf16) for w in (WIN, CTX_L, MLA_KEYS)],
        compiler_params=_params(2),
        name="den_attn",
    )(sink, qa, ka, va, cache_k, cache_v, qm, kvl, kr, cache_ckv, cache_kr, w_ukv_p, x_all, mods0, w_out, ln_g, ln_b)


SUBTILE = 256


def _route(x1, mrow, rw_ref, rb_ref):
    sh, sc = mrow[:, 3 * D:4 * D], mrow[:, 4 * D:5 * D]
    h = x1 * (1.0 + sc) + sh
    h_hi = h.astype(bf16)
    h_lo = (h - h_hi.astype(f32)).astype(bf16)
    logits = _dot(h_hi, rw_ref[0]) + (_dot(h_hi, rw_ref[1]) + _dot(h_lo, rw_ref[0]))
    scores = jax.nn.sigmoid(logits)
    lane = lax.broadcasted_iota(jnp.int32, scores.shape, 1).astype(f32)
    sel = jnp.where(lane < N_EXPERTS, scores + rb_ref[...], -jnp.inf)
    gates = jnp.zeros_like(scores)
    for _ in range(TOP_K):
        m = sel.max(axis=-1, keepdims=True)
        idx = jnp.where(sel == m, lane, float(LANE)).min(axis=-1, keepdims=True)
        hit = lane == idx
        gates = jnp.where(hit, scores, gates)
        sel = jnp.where(hit, -jnp.inf, sel)
    return h_hi, gates / gates.sum(axis=-1, keepdims=True) * ROUTED_SCALE


def _router_kernel(xc_ref, xd_ref, mods_ref, rw_ref, rb_ref, x_ref, h_ref, gates_ref):
    i = pl.program_id(0)
    r = _mod_row(i, TOK_TILE)
    mrow = mods_ref[pl.ds(r, 1), :]
    x1 = _pick(i, TOK_TILE, xc_ref, xd_ref)
    x_ref[...] = x1
    h_ref[...], gates_ref[...] = _route(x1, mrow, rw_ref, rb_ref)


def _router(x1c, x1d, mods_l, router_w_p, router_b_p):
    row_spec = lambda w: pl.BlockSpec((TOK_TILE, w), lambda i: (i, 0))
    xc_spec, xd_spec = _two_stream_specs(TOK_TILE, D)
    return pl.pallas_call(
        _router_kernel,
        grid=(T_ALL // TOK_TILE,),
        in_specs=[xc_spec, xd_spec, _full((8, 6 * D), 1), _full((2, D, LANE), 1), _full((1, LANE), 1)],
        out_specs=[row_spec(D), row_spec(D), row_spec(LANE)],
        out_shape=[jax.ShapeDtypeStruct((T_ALL, D), f32), jax.ShapeDtypeStruct((T_ALL, D), bf16),
                   jax.ShapeDtypeStruct((T_ALL, LANE), f32)],
        compiler_params=_params(1),
        name="router",
    )(x1c, x1d, mods_l, router_w_p, router_b_p)


MOE_TOK = 1536
MOE_EG = 8
MOE_VMEM_LIMIT = 60 * 1024 * 1024
MOE_TILE = 512
MOE_FF = MOE_EG * EXPERT_FF


def _moe_kernel(h_ref, gates_ref, mods_ref, wg_ref, wu_ref, wd_ref, sg_ref, su_ref, sd_ref, o_ref):
    p = pl.program_id(0)
    e = pl.program_id(1)
    n_tiles = MOE_TOK // MOE_TILE

    def gate_f(t):
        r = _mod_row(p * n_tiles + t, MOE_TILE)
        return mods_ref[pl.ds(r, 1), 5 * D:6 * D]

    def rows_of(t):
        if isinstance(t, int):
            return pl.ds(t * MOE_TILE, MOE_TILE)
        return pl.ds(pl.multiple_of(t * MOE_TILE, MOE_TILE), MOE_TILE)

    @pl.when(e == 0)
    def _():
        sg = sg_ref[...].astype(bf16)
        su = su_ref[...].astype(bf16)
        sd = sd_ref[...].astype(bf16)

        def body(t, c):
            rows = rows_of(t)
            ht = h_ref[rows, :]
            hid = _silu(_dot(ht, sg)) * _dot(ht, su)
            o_ref[rows, :] = gate_f(t) * _dot(hid.astype(bf16), sd)
            return c

        lax.fori_loop(0, n_tiles, body, 0)

    wg = jnp.concatenate([wg_ref[k].astype(bf16) for k in range(MOE_EG)], axis=1)
    wu = jnp.concatenate([wu_ref[k].astype(bf16) for k in range(MOE_EG)], axis=1)
    wd = jnp.concatenate([wd_ref[k].astype(bf16) for k in range(MOE_EG)], axis=0)
    lane = lax.broadcasted_iota(jnp.int32, (MOE_TILE, LANE), 1)

    def body(t, c):
        rows = rows_of(t)
        ht = h_ref[rows, :]
        hid = _silu(_dot(ht, wg)) * _dot(ht, wu)
        gt = gates_ref[rows, :]
        parts = []
        for k in range(MOE_EG):
            col = jnp.where(lane == e * MOE_EG + k, gt, 0.0).sum(axis=-1, keepdims=True)
            parts.append((hid[:, EXPERT_FF * k:EXPERT_FF * (k + 1)] * col).astype(bf16))
        o_ref[rows, :] += gate_f(t) * _dot(jnp.concatenate(parts, axis=1), wd)
        return c

    for t in range(n_tiles):
        body(t, 0)


def _moe(l, h, gates, mods_l, wg, wu, wd, sg, su, sd):
    tok = lambda w: pl.BlockSpec((MOE_TOK, w), lambda p, e: (p, 0))
    return pl.pallas_call(
        _moe_kernel,
        grid=(T_ALL // MOE_TOK, N_EXPERTS // MOE_EG),
        in_specs=[tok(D), tok(LANE), _full((8, 6 * D), 2),
                  pl.BlockSpec((None, MOE_EG, D, EXPERT_FF), lambda p, e: (l, e, 0, 0)),
                  pl.BlockSpec((None, MOE_EG, D, EXPERT_FF), lambda p, e: (l, e, 0, 0)),
                  pl.BlockSpec((None, MOE_EG, EXPERT_FF, D), lambda p, e: (l, e, 0, 0)),
                  pl.BlockSpec((None, D, SHARED_FF), lambda p, e: (l, 0, 0)),
                  pl.BlockSpec((None, D, SHARED_FF), lambda p, e: (l, 0, 0)),
                  pl.BlockSpec((None, SHARED_FF, D), lambda p, e: (l, 0, 0))],
        out_specs=tok(D),
        out_shape=jax.ShapeDtypeStruct((T_ALL, D), f32),
        compiler_params=pltpu.CompilerParams(dimension_semantics=("arbitrary", "arbitrary"),
                                             vmem_limit_bytes=MOE_VMEM_LIMIT),
        name="moe",
    )(h, gates, mods_l, wg, wu, wd, sg, su, sd)


def _moe_finish_kernel(x_ref, acc_ref, g_ref, b_ref, o_ref):
    o_ref[...] = _layer_norm(ALPHA * x_ref[...] + acc_ref[...], g_ref[...], b_ref[...])


def _moe_finish(x1, acc, tile0, n_rows, ln_g, ln_b):
    src = pl.BlockSpec((TOK_TILE, D), lambda i: (tile0 + i, 0))
    return pl.pallas_call(
        _moe_finish_kernel,
        grid=(n_rows // TOK_TILE,),
        in_specs=[src, src, _full((1, D), 1), _full((1, D), 1)],
        out_specs=pl.BlockSpec((TOK_TILE, D), lambda i: (i, 0)),
        out_shape=jax.ShapeDtypeStruct((n_rows, D), f32),
        compiler_params=_params(1),
        name="moe_finish",
    )(x1, acc, ln_g, ln_b)


def _router_weights(l, router_w, router_bias):
    rw = jnp.pad(router_w[l], ((0, 0), (0, LANE - N_EXPERTS)))
    rw_hi, rw_lo = _split_bf16(rw)
    rb = jnp.pad(router_bias[l], (0, LANE - N_EXPERTS)).reshape(1, LANE)
    return jnp.stack([rw_hi, rw_lo]), rb


S5_Q = 8
S5_NGB = D // LANE


def _s5_in_kernel(x_ref, acc_ref, lg_ref, lb_ref, mods_ref, w_ref, x2_ref, u_ref, u2_ref, slab_ref, *, row_of,
                  seq_len):
    r = row_of(pl.program_id(0))
    mrow = mods_ref[pl.ds(r, 1), :]
    sh, sc = mrow[:, 0:D], mrow[:, D:2 * D]
    x2 = _layer_norm(ALPHA * x_ref[...] + acc_ref[...], lg_ref[...], lb_ref[...])
    x2_ref[...] = x2
    h = (x2 * (1.0 + sc) + sh).astype(bf16)
    u = _dot(h, w_ref[...])
    u_ref[...] = u
    for s in range(S5_NGB):
        slab_ref[s] = u[:, LANE * s:LANE * (s + 1)]
    kt = seq_len // S5_Q
    for s in range(S5_NGB):
        for q in range(TOK_TILE // seq_len):
            for j in range(S5_Q):
                u2_ref[s, q * kt:(q + 1) * kt, LANE * j:LANE * (j + 1)] = (
                    slab_ref[s, pl.ds(q * seq_len + j, kt, stride=S5_Q), :].astype(bf16))


def _s5_in(x1, acc, ln_g, ln_b, tile0, mods1, w_in_c, n_b, seq_len, row_of):
    n_tiles = n_b * seq_len // TOK_TILE
    chunks = TOK_TILE // S5_Q
    src = pl.BlockSpec((TOK_TILE, D), lambda i: (tile0 + i, 0))
    dst = pl.BlockSpec((TOK_TILE, D), lambda i: (i, 0))
    return pl.pallas_call(
        functools.partial(_s5_in_kernel, row_of=row_of, seq_len=min(seq_len, TOK_TILE)),
        grid=(n_tiles,),
        in_specs=[src, src, _full((1, D), 1), _full((1, D), 1), _full((8, 6 * D), 1), _full((D, D), 1)],
        out_specs=[dst, dst, pl.BlockSpec((S5_NGB, chunks, D), lambda i: (0, i, 0))],
        out_shape=[jax.ShapeDtypeStruct((n_b * seq_len, D), f32), jax.ShapeDtypeStruct((n_b * seq_len, D), f32),
                   jax.ShapeDtypeStruct((S5_NGB, n_tiles * chunks, D), bf16)],
        scratch_shapes=[pltpu.VMEM((S5_NGB, TOK_TILE, LANE), f32)],
        compiler_params=_params(1),
        name="s5_in",
    )(x1, acc, ln_g, ln_b, mods1, w_in_c)


S5_GL = (LANE // S5_CH) * S5_P
S5_ROWS_C = (CTX_L // S5_Q) * N_CTX_B
S5_ROWS_D = (DEN_L // S5_Q) * N_DEN_B


def _s5_scan_kernel(lre_ref, lim_ref, ldt_ref, btr_ref, bti_ref, ctr_ref, cti_ref, uc_ref, ud_ref, h0_ref,
                    yc_ref, yd_ref, st_ref, win_ref, mso_ref, wit_ref, a_ref, s_ref, hp_ref):
    gl = S5_GL
    rowg = lax.shift_right_logical(lax.broadcasted_iota(jnp.int32, (LANE, gl), 0), 4)
    colg = lax.shift_right_logical(lax.broadcasted_iota(jnp.int32, (LANE, gl), 1), 6)
    same_group = rowg == colg
    reps = LANE // S5_CH

    def expand(t):
        return jnp.where(same_group, jnp.concatenate([t] * reps, axis=0), 0.0)

    def expand_c(t):
        return jnp.where(same_group, jnp.concatenate([t] * reps, axis=1), 0.0)

    for d in range(2):
        fwd = d == 0
        lre, lim = lre_ref[d], lim_ref[d]
        dt = jnp.exp(ldt_ref[d])
        a, w = lre * dt, lim * dt
        pre = [jnp.exp(m * a) * jnp.cos(m * w) for m in range(S5_Q + 1)]
        pim = [jnp.exp(m * a) * jnp.sin(m * w) for m in range(S5_Q + 1)]
        xr, xi = pre[1] - 1.0, pim[1]
        den = lre * lre + lim * lim
        cfr, cfi = (xr * lre + xi * lim) / den, (xi * lre - xr * lim) / den
        btr, bti = btr_ref[d], bti_ref[d]
        bexp_r = expand(cfr * btr - cfi * bti)
        bexp_i = expand(cfr * bti + cfi * btr)
        cexp_r, cexp_i = expand_c(ctr_ref[d]), expand_c(cti_ref[d])
        for m in range(S5_Q + 1):
            a_ref[m, :, 0:gl] = cexp_r * pre[m] - cexp_i * pim[m]
            a_ref[m, :, gl:2 * gl] = -(cexp_r * pim[m] + cexp_i * pre[m])
        for j in range(S5_Q):
            m = S5_Q - 1 - j if fwd else j
            win_ref[LANE * j:LANE * (j + 1), 0:gl] = (pre[m] * bexp_r - pim[m] * bexp_i).astype(bf16)
            win_ref[LANE * j:LANE * (j + 1), gl:2 * gl] = (pre[m] * bexp_i + pim[m] * bexp_r).astype(bf16)
        for j in range(S5_Q):
            m = j + 1 if fwd else S5_Q - j
            mso_ref[LANE * j:LANE * (j + 1), :] = a_ref[m].astype(bf16)
        b2_hi, b2_lo = _split_bf16(jnp.concatenate([bexp_r, bexp_i], axis=1))
        kt = []
        for tau in range(S5_Q):
            a_hi, a_lo = _split_bf16(a_ref[tau])
            kt.append(_dot_nt(b2_hi, a_hi) + _dot_nt(b2_hi, a_lo) + _dot_nt(b2_lo, a_hi))
        for j in range(S5_Q):
            for jp in range(S5_Q):
                tau = jp - j if fwd else j - jp
                blk = slice(LANE * j, LANE * (j + 1)), slice(LANE * jp, LANE * (jp + 1))
                if fwd:
                    wit_ref[blk] = kt[tau] if tau >= 0 else jnp.zeros((LANE, LANE), f32)
                elif tau >= 0:
                    wit_ref[blk] = wit_ref[blk] + kt[tau]

        l8r, l8i = pre[S5_Q], pim[S5_Q]

        nsl = gl // LANE

        def slabs(ref, rs, first):
            return jnp.concatenate([ref[first + sl, rs, :] for sl in range(nsl)], axis=1)

        def put_slabs(ref, rs, first, val):
            for sl in range(nsl):
                ref[first + sl, rs, :] = val[:, LANE * sl:LANE * (sl + 1)]

        def advance(hr, hi_, sr, si):
            return l8r * hr - l8i * hi_ + sr, l8r * hi_ + l8i * hr + si

        def run(u_ref, y_ref, n_b, n_k, h_init):
            rows = n_b * n_k
            s = _dot(u_ref[0], win_ref[...])
            if n_b % SUB == 0:
                pitch = n_k + 1
                for bb in range(n_b):
                    dst = slice(bb * pitch, bb * pitch + n_k)
                    put_slabs(s_ref, dst, 0, s[bb * n_k:(bb + 1) * n_k, 0:gl])
                    put_slabs(s_ref, dst, nsl, s[bb * n_k:(bb + 1) * n_k, gl:2 * gl])

                def step(i, carry):
                    hr, hi_ = carry
                    rs = pl.ds(i if fwd else n_k - 1 - i, n_b, stride=pitch)
                    put_slabs(hp_ref, rs, 0, hr)
                    put_slabs(hp_ref, rs, nsl, hi_)
                    return advance(hr, hi_, slabs(s_ref, rs, 0), slabs(s_ref, rs, nsl))

                h_fin = lax.fori_loop(0, n_k, step, h_init)
                hp = jnp.concatenate(
                    [jnp.concatenate([hp_ref[sl, bb * pitch:bb * pitch + n_k, :] for sl in range(2 * nsl)], axis=1)
                     for bb in range(n_b)], axis=0).astype(bf16)
            else:
                put_slabs(s_ref, slice(0, rows), 0, s[:, 0:gl])
                put_slabs(s_ref, slice(0, rows), nsl, s[:, gl:2 * gl])
                n_it = n_k // SUB

                def step(i, carry):
                    it = i if fwd else n_it - 1 - i
                    out = []
                    for bb in range(n_b):
                        hr, hi_ = carry[bb]
                        rs = pl.ds(pl.multiple_of(bb * n_k + it * SUB, SUB), SUB)
                        s_re, s_im = slabs(s_ref, rs, 0), slabs(s_ref, rs, nsl)
                        prev_r, prev_i = [None] * SUB, [None] * SUB
                        for sub in (range(SUB) if fwd else reversed(range(SUB))):
                            prev_r[sub], prev_i[sub] = hr, hi_
                            hr, hi_ = advance(hr, hi_, s_re[sub:sub + 1], s_im[sub:sub + 1])
                        put_slabs(hp_ref, rs, 0, jnp.concatenate(prev_r, axis=0))
                        put_slabs(hp_ref, rs, nsl, jnp.concatenate(prev_i, axis=0))
                        out.append((hr, hi_))
                    return tuple(out)

                fin = lax.fori_loop(0, n_it, step, tuple((h_init[0][bb:bb + 1], h_init[1][bb:bb + 1])
                                                          for bb in range(n_b)))
                h_fin = (jnp.concatenate([f[0] for f in fin], axis=0), jnp.concatenate([f[1] for f in fin], axis=0))
                hp = jnp.concatenate([hp_ref[sl, 0:rows, :] for sl in range(2 * nsl)], axis=1).astype(bf16)
            y = _dot_nt(hp, mso_ref[...])
            if fwd:
                y_ref[0] = y
            else:
                y_ref[0] += y
            return h_fin

        zeros = jnp.zeros((N_CTX_B, gl), f32)
        hr, hi_ = run(uc_ref, yc_ref, N_CTX_B, CTX_L // S5_Q, (zeros, zeros))
        st_ref[d, 0] = hr
        st_ref[d, 1] = hi_
        run(ud_ref, yd_ref, N_DEN_B, DEN_L // S5_Q, (h0_ref[d, 0], h0_ref[d, 1]))

    wit = wit_ref[...].astype(bf16)
    yc_ref[0] += _dot(uc_ref[0], wit)
    yd_ref[0] += _dot(ud_ref[0], wit)


def _s5_scan(lam_re, lam_im, log_dt, bt_re, bt_im, ct_re, ct_im, u2c, u2d, h0):
    gl = S5_GL
    vec = pl.BlockSpec((2, 1, gl), lambda g: (0, 0, g))
    tab = pl.BlockSpec((2, S5_CH, gl), lambda g: (0, 0, g))
    ctab = pl.BlockSpec((2, LANE, S5_P), lambda g: (0, g, 0))
    rows = lambda n: pl.BlockSpec((1, n, D), lambda g: (g, 0, 0))
    return pl.pallas_call(
        _s5_scan_kernel,
        grid=(S5_NGB,),
        in_specs=[vec, vec, vec, tab, tab, ctab, ctab, rows(S5_ROWS_C), rows(S5_ROWS_D),
                  pl.BlockSpec((2, 2, N_DEN_B, gl), lambda g: (0, 0, 0, g))],
        out_specs=[rows(S5_ROWS_C), rows(S5_ROWS_D), pl.BlockSpec((2, 2, N_CTX_B, gl), lambda g: (0, 0, 0, g))],
        out_shape=[jax.ShapeDtypeStruct((S5_NGB, S5_ROWS_C, D), f32), jax.ShapeDtypeStruct((S5_NGB, S5_ROWS_D, D), f32),
                   jax.ShapeDtypeStruct((2, 2, N_CTX_B, S5_G * S5_P), f32)],
        scratch_shapes=[pltpu.VMEM((D, 2 * gl), bf16), pltpu.VMEM((D, 2 * gl), bf16), pltpu.VMEM((D, D), f32),
                        pltpu.VMEM((S5_Q + 1, LANE, 2 * gl), f32),
                        pltpu.VMEM((2 * gl // LANE, S5_ROWS_C + 2 * N_CTX_B, LANE), f32),
                        pltpu.VMEM((2 * gl // LANE, S5_ROWS_C + 2 * N_CTX_B, LANE), f32)],
        compiler_params=_params(1),
        name="s5_scan",
    )(lam_re, lam_im, log_dt, bt_re, bt_im, ct_re, ct_im, u2c, u2d, h0)


def _gelu_tanh(x):
    return 0.5 * x * (1.0 + jnp.tanh(np.sqrt(2.0 / np.pi).astype(np.float32) * (x + 0.044715 * (x * x * x))))


def _s5_out_kernel(xc_ref, xd_ref, uc_ref, ud_ref, yc_ref, yd_ref, mods_ref, dsk_ref, wout_ref, g_ref, b_ref, rw_ref,
                   rb_ref, x1_ref, h_ref, gates_ref, slab_ref):
    i = pl.program_id(0)
    is_ctx = i < T_CTX // TOK_TILE
    r = _mod_row(i, TOK_TILE)
    mrow = mods_ref[pl.ds(r, 1), :]
    u = _pick(i, TOK_TILE, uc_ref, ud_ref)

    def unchunk(y_ref, seq_len):
        kt = seq_len // S5_Q
        for s in range(S5_NGB):
            for q in range(TOK_TILE // seq_len):
                for j in range(S5_Q):
                    slab_ref[s, pl.ds(q * seq_len + j, kt, stride=S5_Q), :] = (
                        y_ref[s, q * kt:(q + 1) * kt, LANE * j:LANE * (j + 1)])

    lax.cond(is_ctx, lambda: unchunk(yc_ref, min(CTX_L, TOK_TILE)), lambda: unchunk(yd_ref, min(DEN_L, TOK_TILE)))
    halves = [slice(a, a + SUBTILE) for a in range(0, TOK_TILE, SUBTILE)]
    zs = []
    for rows in halves:
        y = jnp.concatenate([slab_ref[s, rows, :] for s in range(S5_NGB)], axis=1) + dsk_ref[...] * u[rows]
        zs.append(_dot(_gelu_tanh(y).astype(bf16), wout_ref[...]))
    for rows, z in zip(halves, zs):
        out = z[:, 0:D] * jax.nn.sigmoid(z[:, D:2 * D])
        x = jnp.where(is_ctx, xc_ref[rows, :], xd_ref[rows, :])
        x1 = _layer_norm(ALPHA * x + mrow[:, 2 * D:3 * D] * out, g_ref[...], b_ref[...])
        x1_ref[rows, :] = x1
        h_ref[rows, :], gates_ref[rows, :] = _route(x1, mrow, rw_ref, rb_ref)


def _s5_out(xc, xd, uc, ud, yc, yd, mods1, d_skip, w_out_c, ln_g, ln_b, rw, rb):
    row_spec = lambda w: pl.BlockSpec((TOK_TILE, w), lambda i: (i, 0))
    uc_spec, ud_spec = _two_stream_specs(TOK_TILE, D)
    n_ctx = T_CTX // TOK_TILE
    chunks = TOK_TILE // S5_Q
    return pl.pallas_call(
        _s5_out_kernel,
        grid=(T_ALL // TOK_TILE,),
        in_specs=[uc_spec, ud_spec, uc_spec, ud_spec,
                  pl.BlockSpec((S5_NGB, chunks, D), lambda i: (0, jnp.minimum(i, n_ctx - 1), 0)),
                  pl.BlockSpec((S5_NGB, chunks, D), lambda i: (0, jnp.maximum(i - n_ctx, 0), 0)),
                  _full((8, 6 * D), 1), _full((1, D), 1), _full((D, 2 * D), 1), _full((1, D), 1), _full((1, D), 1),
                  _full((2, D, LANE), 1), _full((1, LANE), 1)],
        out_specs=[row_spec(D), row_spec(D), row_spec(LANE)],
        out_shape=[jax.ShapeDtypeStruct((T_ALL, D), f32), jax.ShapeDtypeStruct((T_ALL, D), bf16),
                   jax.ShapeDtypeStruct((T_ALL, LANE), f32)],
        scratch_shapes=[pltpu.VMEM((S5_NGB, TOK_TILE, LANE), f32)],
        compiler_params=_params(1),
        name="s5_out",
    )(xc, xd, uc, ud, yc, yd, mods1, d_skip, w_out_c, ln_g, ln_b, rw, rb)


def kernel(x_prompt, x_sample, c, cache_attn_k, cache_attn_v, cache_mla_ckv, cache_mla_krope, state_ssm, c_ctx,
           ada_w, ada_b, ln_mix_g, ln_mix_b, ln_ffn_g, ln_ffn_b, w_in_ab, attn_sink, mla_q_norm, mla_kv_norm,
           mla_w_uq, mla_w_ukv, w_out_ab, w_in_c, s5_lam_re, s5_lam_im, s5_log_dt, s5_b_re, s5_b_im, s5_c_re,
           s5_c_im, s5_d, w_out_c, router_w, router_bias, exp_w_gate, exp_w_up, exp_w_down, sh_w_gate, sh_w_up,
           sh_w_down):
    row = lambda v: v.reshape(1, -1)
    xc, xd = x_prompt.reshape(T_CTX, D), x_sample.reshape(T_DEN, D)
    cvec8 = jnp.concatenate([c_ctx[None, :], c, jnp.zeros((8 - 1 - N_DEN_B, D), f32)], axis=0)
    mods = _adaln(cvec8, ada_w, ada_b)

    w_in_p = jnp.pad(w_in_ab[0], ((0, 0), (0, PROJ_W - w_in_ab.shape[-1]))).astype(bf16)
    uq = mla_w_uq[0].reshape(MLA_Q_RANK, MLA_HEADS, MLA_NOPE + MLA_ROPE)
    w_uq_p = jnp.concatenate([uq[:, :, :MLA_NOPE].reshape(MLA_Q_RANK, -1), uq[:, :, MLA_NOPE:].reshape(MLA_Q_RANK, -1)],
                             axis=1).astype(bf16)
    ukv = mla_w_ukv[0].reshape(MLA_KV_RANK, MLA_HEADS, MLA_NOPE + MLA_V)
    w_ukv_p = jnp.concatenate([ukv[:, :, :MLA_NOPE].reshape(MLA_KV_RANK, -1),
                               ukv[:, :, MLA_NOPE:].reshape(MLA_KV_RANK, -1)], axis=1).astype(bf16)
    qa, ka, va, ckv, kr, qm, kvl = _ab_proj(xc, xd, mods[0], w_in_p, row(mla_q_norm[0]), row(mla_kv_norm[0]),
                                            w_uq_p, w_ukv_p)
    w_out_b = w_out_ab[0].astype(bf16)
    g0, b0 = row(ln_mix_g[0]), row(ln_mix_b[0])
    x1c, nk_t, nv_t = _ctx_attn(attn_sink[0], qa, ka, va, qm, kvl, kr, xc, mods[0], w_out_b, g0, b0)
    new_attn_k = jnp.transpose(nk_t, (0, 1, 4, 2, 3))
    new_attn_v = jnp.transpose(nv_t, (0, 1, 4, 2, 3))
    x1d = _den_attn(attn_sink[0], qa, ka, va,
                    jnp.transpose(cache_attn_k[:, 0], (0, 2, 3, 1)), jnp.transpose(cache_attn_v[:, 0], (0, 2, 3, 1)),
                    qm, kvl, kr, cache_mla_ckv[:, 0], cache_mla_krope[:, 0], w_ukv_p, xd, mods[0], w_out_b, g0, b0)
    rw0, rb0 = _router_weights(0, router_w, router_bias)
    x1, h, gates = _router(x1c, x1d, mods[0], rw0, rb0)
    acc = _moe(0, h, gates, mods[0], exp_w_gate, exp_w_up, exp_w_down, sh_w_gate, sh_w_up, sh_w_down)

    w_in_c_b = w_in_c[0].astype(bf16)
    lg0, lb0 = row(ln_ffn_g[0]), row(ln_ffn_b[0])
    n_ctx_tiles = T_CTX // TOK_TILE
    x2c, uc, u2c = _s5_in(x1, acc, lg0, lb0, 0, mods[1], w_in_c_b, N_CTX_B, CTX_L, lambda i: 0)
    x2d, ud, u2d = _s5_in(x1, acc, lg0, lb0, n_ctx_tiles, mods[1], w_in_c_b, N_DEN_B, DEN_L,
                          lambda i: 1 + i // (DEN_L // TOK_TILE))
    gp = S5_G * S5_P
    chan_major_b = lambda t: jnp.transpose(t[0], (0, 3, 1, 2)).reshape(2, S5_CH, gp)
    chan_major_c = lambda t: t[0].reshape(2, S5_G * S5_CH, S5_P)
    h0 = jnp.transpose(state_ssm[:, 0], (1, 2, 0, 3, 4)).reshape(2, 2, N_DEN_B, gp)
    yc, yd, st = _s5_scan(s5_lam_re[0].reshape(2, 1, gp), s5_lam_im[0].reshape(2, 1, gp),
                          jnp.repeat(s5_log_dt[0], S5_P, axis=-1).reshape(2, 1, gp),
                          chan_major_b(s5_b_re), chan_major_b(s5_b_im), chan_major_c(s5_c_re), chan_major_c(s5_c_im),
                          u2c, u2d, h0)
    rw1, rb1 = _router_weights(1, router_w, router_bias)
    x3, h, gates = _s5_out(x2c, x2d, uc, ud, yc, yd, mods[1], row(s5_d[0]),
                           w_out_c[0].astype(bf16), row(ln_mix_g[1]), row(ln_mix_b[1]), rw1, rb1)
    acc = _moe(1, h, gates, mods[1], exp_w_gate, exp_w_up, exp_w_down, sh_w_gate, sh_w_up, sh_w_down)
    lg1, lb1 = row(ln_ffn_g[1]), row(ln_ffn_b[1])
    y_prompt = _moe_finish(x3, acc, 0, T_CTX, lg1, lb1).reshape(N_CTX_B, CTX_L, D)
    y_sample = _moe_finish(x3, acc, n_ctx_tiles, T_DEN, lg1, lb1).reshape(N_DEN_B, DEN_L, D)
    new_mla_ckv = ckv[:T_CTX].reshape(N_CTX_B, 1, CTX_L, MLA_KV_RANK)
    new_mla_krope = kr[:T_CTX, :MLA_ROPE].reshape(N_CTX_B, 1, CTX_L, MLA_ROPE)
    new_state_ssm = jnp.transpose(st, (2, 0, 1, 3)).reshape(N_CTX_B, 1, 2, 2, S5_G, S5_P)
    return (y_prompt, y_sample, new_attn_k, new_attn_v, new_mla_ckv, new_mla_krope, new_state_ssm)
```

```python
import functools

import jax
import jax.numpy as jnp
import numpy as np
from jax import lax
from jax.experimental import pallas as pl
from jax.experimental.pallas import tpu as pltpu

f32 = jnp.float32
bf16 = jnp.bfloat16

D = 1024
N_CTX_B, CTX_L = 16, 256
N_DEN_B, DEN_L = 2, 1024
T_CTX = N_CTX_B * CTX_L
T_DEN = N_DEN_B * DEN_L
T_ALL = T_CTX + T_DEN
GRID_W = 64
WINDOW = 128
ROPE_BASE = 10000.0
A_HEADS, A_KV_HEADS, A_HD = 8, 2, 64
A_GROUP = A_HEADS // A_KV_HEADS
A_SCALE = A_HD ** -0.5
MLA_HEADS, MLA_Q_RANK, MLA_KV_RANK = 8, 256, 128
MLA_NOPE, MLA_ROPE, MLA_V = 64, 32, 64
MLA_SCALE = (MLA_NOPE + MLA_ROPE) ** -0.5
N_EXPERTS, TOP_K, EXPERT_FF, SHARED_FF = 64, 6, 128, 128
ROUTED_SCALE = 2.5
DEPTH = 2
ALPHA = (2.0 * DEPTH) ** 0.25
LN_EPS = 1e-5
RMS_EPS = 1e-6
NEG_INF = -1e30
S5_G, S5_CH, S5_P = 64, 16, 64

LANE = 128
SUB = 8
VMEM_LIMIT = 56 * 1024 * 1024

TOK_TILE = 512


def _mod_row(tile_idx, tile_rows):
    start = tile_idx * tile_rows
    return jnp.where(start < T_CTX, 0, 1 + (start - T_CTX) // DEN_L)


def _layer_norm(y, g, b):
    mu = jnp.mean(y, axis=-1, keepdims=True)
    yc = y - mu
    var = jnp.mean(yc * yc, axis=-1, keepdims=True)
    return yc * lax.rsqrt(var + LN_EPS) * g + b


def _silu(x):
    return x * jax.nn.sigmoid(x)


def _dot(a, b):
    return jnp.dot(a, b, preferred_element_type=f32)


def _dot_nt(a, b):
    return lax.dot_general(a, b, (((1,), (1,)), ((), ())), preferred_element_type=f32)


def _split_bf16(a):
    hi = a.astype(bf16)
    return hi, (a - hi.astype(f32)).astype(bf16)


def _full(shape, n_grid):
    zeros = tuple(0 for _ in shape)
    return pl.BlockSpec(shape, lambda *_: zeros)


def _two_stream_specs(tile_rows, width):
    n_ctx = T_CTX // tile_rows
    return (pl.BlockSpec((tile_rows, width), lambda i: (jnp.minimum(i, n_ctx - 1), 0)),
            pl.BlockSpec((tile_rows, width), lambda i: (jnp.maximum(i - n_ctx, 0), 0)))


def _pick(i, tile_rows, ctx_ref, den_ref):
    return lax.cond(i < T_CTX // tile_rows, lambda: ctx_ref[...], lambda: den_ref[...])


def _params(n_grid):
    return pltpu.CompilerParams(dimension_semantics=("arbitrary",) * n_grid, vmem_limit_bytes=VMEM_LIMIT)


ADA_TN = 1536


def _adaln_kernel(c_ref, w_ref, b_ref, o_ref):
    s_hi, s_lo = _split_bf16(_silu(c_ref[...]))
    w_hi, w_lo = _split_bf16(w_ref[0])
    o_ref[0] = _dot(s_hi, w_hi) + (_dot(s_hi, w_lo) + _dot(s_lo, w_hi)) + b_ref[0]


def _adaln(cvec8, ada_w, ada_b):
    n = 6 * D
    return pl.pallas_call(
        _adaln_kernel,
        grid=(DEPTH, n // ADA_TN),
        in_specs=[
            pl.BlockSpec((8, D), lambda l, j: (0, 0)),
            pl.BlockSpec((1, D, ADA_TN), lambda l, j: (l, 0, j)),
            pl.BlockSpec((1, 1, ADA_TN), lambda l, j: (l, 0, j)),
        ],
        out_specs=pl.BlockSpec((1, 8, ADA_TN), lambda l, j: (l, 0, j)),
        out_shape=jax.ShapeDtypeStruct((DEPTH, 8, n), f32),
        compiler_params=_params(2),
        name="adaln",
    )(cvec8, ada_w, ada_b.reshape(DEPTH, 1, n))


def _rope_table_array(head_dim):
    q = head_dim // 4
    pos = np.arange(DEN_L)
    row, col = (pos // GRID_W).astype(np.float64), (pos % GRID_W).astype(np.float64)
    lane = np.arange(LANE) % head_dim
    is_col = lane >= head_dim // 2
    w = lane % (head_dim // 2)
    first = w < q
    inv_freq = ROPE_BASE ** (-np.arange(q, dtype=np.float64) / q)
    ang = np.where(is_col[None, :], col[:, None], row[:, None]) * inv_freq[w % q][None, :]
    cos, sin = np.cos(ang), np.sin(ang)
    sin_a = np.where(first[None, :], -sin, 0.0)
    sin_b = np.where(first[None, :], 0.0, sin)
    ident = np.stack([np.ones((TOK_TILE, LANE)), np.zeros((TOK_TILE, LANE)), np.zeros((TOK_TILE, LANE))])
    tab = np.concatenate([ident, np.stack([cos, sin_a, sin_b])], axis=1).astype(np.float32)
    return jnp.asarray(tab), q


def _rope_chunk(x, tab_ref, q):
    return x * tab_ref[0] + pltpu.roll(x, LANE - q, 1) * tab_ref[1] + pltpu.roll(x, q, 1) * tab_ref[2]


PROJ_W = 1280
C_QA, C_KA, C_VA, C_CQ, C_CKV, C_KR = 0, 512, 640, 768, 1024, 1152
MLA_NN = MLA_HEADS * MLA_NOPE


def _ab_proj_kernel(xc_ref, xd_ref, mods_ref, w_ref, qn_ref, kvn_ref, wuq_ref, wukv_ref, ta_ref, tm_ref,
                    qa_ref, ka_ref, va_ref, ckv_ref, kr_ref, qm_ref, kvl_ref, *, qa_shift, qm_shift):
    i = pl.program_id(0)
    r = _mod_row(i, TOK_TILE)
    mrow = mods_ref[pl.ds(r, 1), :]
    sh, sc = mrow[:, 0:D], mrow[:, D:2 * D]
    x = _pick(i, TOK_TILE, xc_ref, xd_ref)
    h = (x * (1.0 + sc) + sh).astype(bf16)
    proj = _dot(h, w_ref[...])
    for j in range(4):
        c0 = C_QA + LANE * j
        qa_ref[:, LANE * j:LANE * (j + 1)] = _rope_chunk(proj[:, c0:c0 + LANE], ta_ref, qa_shift).astype(bf16)
    ka_ref[...] = _rope_chunk(proj[:, C_KA:C_KA + LANE], ta_ref, qa_shift)
    va_ref[...] = proj[:, C_VA:C_VA + LANE]
    cq = proj[:, C_CQ:C_CQ + MLA_Q_RANK]
    cq = cq * lax.rsqrt(jnp.mean(cq * cq, axis=-1, keepdims=True) + RMS_EPS) * qn_ref[...]
    ckv = proj[:, C_CKV:C_CKV + MLA_KV_RANK]
    ckv = ckv * lax.rsqrt(jnp.mean(ckv * ckv, axis=-1, keepdims=True) + RMS_EPS) * kvn_ref[...]
    ckv_ref[...] = ckv
    kr_ref[...] = _rope_chunk(proj[:, C_KR:C_KR + LANE], tm_ref, qm_shift)
    qm = _dot(cq.astype(bf16), wuq_ref[...])
    qm_ref[:, 0:MLA_NN] = qm[:, 0:MLA_NN].astype(bf16)
    for j in range(2):
        c0 = MLA_NN + LANE * j
        qm_ref[:, c0:c0 + LANE] = _rope_chunk(qm[:, c0:c0 + LANE], tm_ref, qm_shift).astype(bf16)
    kvl_ref[...] = _dot(ckv.astype(bf16), wukv_ref[...]).astype(bf16)


def _rope_block_index(i):
    tiles_ctx = T_CTX // TOK_TILE
    per_seq = DEN_L // TOK_TILE
    return jnp.where(i < tiles_ctx, 0, 1 + (i - tiles_ctx) % per_seq)


def _ab_proj(xc, xd, mods0, w_in_p, q_norm, kv_norm, w_uq_p, w_ukv_p):
    tab_a, qa_shift = _rope_table_array(A_HD)
    tab_m, qm_shift = _rope_table_array(MLA_ROPE)
    row_spec = lambda w: pl.BlockSpec((TOK_TILE, w), lambda i: (i, 0))
    xc_spec, xd_spec = _two_stream_specs(TOK_TILE, D)
    tab_spec = pl.BlockSpec((3, TOK_TILE, LANE), lambda i: (0, _rope_block_index(i), 0))
    outs = [(512, bf16), (LANE, f32), (LANE, f32), (LANE, f32), (LANE, f32), (768, bf16), (1024, bf16)]
    return pl.pallas_call(
        functools.partial(_ab_proj_kernel, qa_shift=qa_shift, qm_shift=qm_shift),
        grid=(T_ALL // TOK_TILE,),
        in_specs=[xc_spec, xd_spec, _full((8, 6 * D), 1), _full((D, PROJ_W), 1), _full((1, MLA_Q_RANK), 1),
                  _full((1, MLA_KV_RANK), 1), _full((MLA_Q_RANK, 768), 1), _full((MLA_KV_RANK, 1024), 1),
                  tab_spec, tab_spec],
        out_specs=[row_spec(w) for w, _ in outs],
        out_shape=[jax.ShapeDtypeStruct((T_ALL, w), dt) for w, dt in outs],
        compiler_params=_params(1),
        name="ab_proj",
    )(xc, xd, mods0, w_in_p, q_norm, kv_norm, w_uq_p, w_ukv_p, tab_a, tab_m)


def _softmax_blocks(s_refs, p_refs, sink_col=None):
    m = s_refs[0][...].max(axis=-1, keepdims=True)
    for s_ref in s_refs[1:]:
        m = jnp.maximum(m, s_ref[...].max(axis=-1, keepdims=True))
    if sink_col is not None:
        m = jnp.maximum(m, sink_col)
    l = None
    for s_ref, p_ref in zip(s_refs, p_refs):
        p = jnp.exp(s_ref[...] - m).astype(bf16)
        p_ref[...] = p
        ps = _dot(p, jnp.ones((p.shape[1], LANE), bf16))[:, 0:1]
        l = ps if l is None else l + ps
    if sink_col is not None:
        l = l + jnp.exp(sink_col - m)
    return 1.0 / l


def _sink_column(sink_ref, rows_per_head):
    return jnp.concatenate([jnp.full((rows_per_head, 1), sink_ref[h], f32) for h in range(A_HEADS)], axis=0)


def _mla_q(qm_ref, h):
    rows = qm_ref.shape[0]
    return jnp.concatenate([qm_ref[:, MLA_NOPE * h:MLA_NOPE * (h + 1)],
                            qm_ref[:, MLA_NN + MLA_ROPE * h:MLA_NN + MLA_ROPE * (h + 1)],
                            jnp.zeros((rows, LANE - MLA_NOPE - MLA_ROPE), bf16)], axis=1)


def _mla_k(k_nope_h, k_rope):
    rows = k_nope_h.shape[0]
    return jnp.concatenate([k_nope_h, k_rope, jnp.zeros((rows, LANE - MLA_NOPE - MLA_ROPE), bf16)], axis=1)


def _mix_out_ln(merged_ref, wout_ref, x, mods_ref, r, g_ref, b_ref):
    out = _dot(merged_ref[...], wout_ref[...])
    gate = mods_ref[pl.ds(r, 1), 2 * D:3 * D]
    return _layer_norm(ALPHA * x + gate * out, g_ref[...], b_ref[...])


def _ctx_attn_kernel(sink_ref, qa_ref, ka_ref, va_ref, qm_ref, kvl_ref, kr_ref, x_ref, mods_ref, wout_ref,
                     g_ref, b_ref, o_ref, nk_ref, nv_ref, merged_ref, sa_ref, sm_ref, pa_ref, pm_ref):
    nk_ref[0, 0] = ka_ref[...].T.reshape(A_KV_HEADS, A_HD, CTX_L)
    nv_ref[0, 0] = va_ref[...].T.reshape(A_KV_HEADS, A_HD, CTX_L)
    n = CTX_L
    ka = ka_ref[...].astype(bf16)
    va = va_ref[...].astype(bf16)
    for j in range(A_KV_HEADS):
        q4 = jnp.concatenate([qa_ref[:, A_HD * h:A_HD * (h + 1)] for h in range(A_GROUP * j, A_GROUP * (j + 1))],
                             axis=0)
        sa_ref[A_GROUP * n * j:A_GROUP * n * (j + 1), :] = _dot_nt(q4, ka[:, A_HD * j:A_HD * (j + 1)]) * A_SCALE
    kr = kr_ref[:, 0:MLA_ROPE].astype(bf16)
    for h in range(MLA_HEADS):
        k_cat = _mla_k(kvl_ref[:, MLA_NOPE * h:MLA_NOPE * (h + 1)], kr)
        sm_ref[n * h:n * (h + 1), :] = _dot_nt(_mla_q(qm_ref, h), k_cat) * MLA_SCALE
    rla = _softmax_blocks([sa_ref], [pa_ref], _sink_column(sink_ref, n))
    rlm = _softmax_blocks([sm_ref], [pm_ref])
    for j in range(A_KV_HEADS):
        rows = slice(A_GROUP * n * j, A_GROUP * n * (j + 1))
        o4 = _dot(pa_ref[rows, :], va[:, A_HD * j:A_HD * (j + 1)]) * rla[rows]
        for g in range(A_GROUP):
            h = A_GROUP * j + g
            merged_ref[:, A_HD * h:A_HD * (h + 1)] = o4[n * g:n * (g + 1)].astype(bf16)
    for h in range(MLA_HEADS):
        rows = slice(n * h, n * (h + 1))
        v = kvl_ref[:, MLA_NN + MLA_V * h:MLA_NN + MLA_V * (h + 1)]
        merged_ref[:, MLA_NN + MLA_V * h:MLA_NN + MLA_V * (h + 1)] = (_dot(pm_ref[rows, :], v) * rlm[rows]).astype(bf16)
    o_ref[...] = _mix_out_ln(merged_ref, wout_ref, x_ref[...], mods_ref, 0, g_ref, b_ref)


def _ctx_attn(sink, qa, ka, va, qm, kvl, kr, x_all, mods0, w_out, ln_g, ln_b):
    blk = lambda w: pl.BlockSpec((CTX_L, w), lambda b: (b, 0))
    cache_blk = pl.BlockSpec((1, 1, A_KV_HEADS, A_HD, CTX_L), lambda b: (b, 0, 0, 0, 0))
    cache_shape = jax.ShapeDtypeStruct((N_CTX_B, 1, A_KV_HEADS, A_HD, CTX_L), f32)
    return pl.pallas_call(
        _ctx_attn_kernel,
        grid=(N_CTX_B,),
        in_specs=[pl.BlockSpec(memory_space=pltpu.SMEM), blk(512), blk(LANE), blk(LANE), blk(768), blk(1024),
                  blk(LANE), blk(D), _full((8, 6 * D), 1), _full((D, D), 1), _full((1, D), 1), _full((1, D), 1)],
        out_specs=[blk(D), cache_blk, cache_blk],
        out_shape=[jax.ShapeDtypeStruct((T_CTX, D), f32), cache_shape, cache_shape],
        scratch_shapes=[pltpu.VMEM((CTX_L, D), bf16),
                        pltpu.VMEM((A_HEADS * CTX_L, CTX_L), f32), pltpu.VMEM((MLA_HEADS * CTX_L, CTX_L), f32),
                        pltpu.VMEM((A_HEADS * CTX_L, CTX_L), bf16), pltpu.VMEM((MLA_HEADS * CTX_L, CTX_L), bf16)],
        compiler_params=_params(1),
        name="ctx_attn",
    )(sink, qa, ka, va, qm, kvl, kr, x_all, mods0, w_out, ln_g, ln_b)


QB = 256
WIN = QB + 2 * WINDOW
DEN_BLK0 = T_CTX // DEN_L
MLA_KEYS = CTX_L + DEN_L


def _den_attn_kernel(sink_ref, qa_ref, ka_ref, va_ref, cak_ref, cav_ref, qm_ref, kvl_ref, kr_ref, cckv_ref, ckr_ref,
                     wukv_ref, x_ref, mods_ref, wout_ref, g_ref, b_ref, o_ref, merged_ref, kcat_ref, vcat_ref,
                     saw_ref, sac_ref, sm_ref, paw_ref, pac_ref, pm_ref):
    b = pl.program_id(0)
    n = pl.program_id(1)

    @pl.when(n == 0)
    def _():
        kvc = _dot(cckv_ref[0].astype(bf16), wukv_ref[...]).astype(bf16)
        kr_ctx = ckr_ref[0].astype(bf16)
        kr_lat = kr_ref[:, 0:MLA_ROPE].astype(bf16)
        for h in range(MLA_HEADS):
            ns = slice(MLA_NOPE * h, MLA_NOPE * (h + 1))
            kcat_ref[h, 0:CTX_L, :] = _mla_k(kvc[:, ns], kr_ctx)
            kcat_ref[h, CTX_L:MLA_KEYS, :] = _mla_k(kvl_ref[:, ns], kr_lat)
        vcat_ref[0:CTX_L, :] = kvc[:, MLA_NN:2 * MLA_NN]
        vcat_ref[CTX_L:MLA_KEYS, :] = kvl_ref[:, MLA_NN:2 * MLA_NN]

    start = pl.multiple_of(jnp.clip(QB * n - WINDOW, 0, DEN_L - WIN), WINDOW)
    grp_rows = A_GROUP * QB
    qpos = QB * n + (lax.broadcasted_iota(jnp.int32, (grp_rows, WIN), 0) & (QB - 1))
    kpos = start + lax.broadcasted_iota(jnp.int32, (grp_rows, WIN), 1)
    valid = jnp.abs(qpos - kpos) <= WINDOW
    kwin = ka_ref[pl.ds(start, WIN), :].astype(bf16)
    vwin = va_ref[pl.ds(start, WIN), :].astype(bf16)
    kctx_t = [cak_ref[0, j].astype(bf16) for j in range(A_KV_HEADS)]
    vctx_t = [cav_ref[0, j].astype(bf16) for j in range(A_KV_HEADS)]
    for j in range(A_KV_HEADS):
        sl = slice(A_HD * j, A_HD * (j + 1))
        rows = slice(grp_rows * j, grp_rows * (j + 1))
        q4 = jnp.concatenate([qa_ref[:, A_HD * h:A_HD * (h + 1)] for h in range(A_GROUP * j, A_GROUP * (j + 1))],
                             axis=0)
        saw_ref[rows, :] = jnp.where(valid, _dot_nt(q4, kwin[:, sl]) * A_SCALE, NEG_INF)
        sac_ref[rows, :] = _dot(q4, kctx_t[j]) * A_SCALE
    for h in range(MLA_HEADS):
        sm_ref[QB * h:QB * (h + 1), :] = _dot_nt(_mla_q(qm_ref, h), kcat_ref[h]) * MLA_SCALE
    rla = _softmax_blocks([saw_ref, sac_ref], [paw_ref, pac_ref], _sink_column(sink_ref, QB))
    rlm = _softmax_blocks([sm_ref], [pm_ref])
    for j in range(A_KV_HEADS):
        sl = slice(A_HD * j, A_HD * (j + 1))
        rows = slice(grp_rows * j, grp_rows * (j + 1))
        o4 = (_dot(paw_ref[rows, :], vwin[:, sl]) + _dot_nt(pac_ref[rows, :], vctx_t[j])) * rla[rows]
        for g in range(A_GROUP):
            h = A_GROUP * j + g
            merged_ref[:, A_HD * h:A_HD * (h + 1)] = o4[QB * g:QB * (g + 1)].astype(bf16)
    for h in range(MLA_HEADS):
        rows = slice(QB * h, QB * (h + 1))
        o = _dot(pm_ref[rows, :], vcat_ref[:, MLA_V * h:MLA_V * (h + 1)]) * rlm[rows]
        merged_ref[:, MLA_NN + MLA_V * h:MLA_NN + MLA_V * (h + 1)] = o.astype(bf16)
    o_ref[...] = _mix_out_ln(merged_ref, wout_ref, x_ref[...], mods_ref, 1 + b, g_ref, b_ref)


def _den_attn(sink, qa, ka, va, cache_k, cache_v, qm, kvl, kr, cache_ckv, cache_kr, w_ukv_p, x_all, mods0, w_out,
              ln_g, ln_b):
    nq = DEN_L // QB
    qblk = lambda w: pl.BlockSpec((QB, w), lambda b, n: (T_CTX // QB + b * nq + n, 0))
    seq = lambda w: pl.BlockSpec((DEN_L, w), lambda b, n: (DEN_BLK0 + b, 0))
    cache = lambda w: pl.BlockSpec((1, CTX_L, w), lambda b, n: (b, 0, 0))
    cache_a = pl.BlockSpec((1, A_KV_HEADS, A_HD, CTX_L), lambda b, n: (b, 0, 0, 0))
    return pl.pallas_call(
        _den_attn_kernel,
        grid=(N_DEN_B, nq),
        in_specs=[pl.BlockSpec(memory_space=pltpu.SMEM), qblk(512), seq(LANE), seq(LANE), cache_a, cache_a,
                  qblk(768), seq(1024), seq(LANE), cache(MLA_KV_RANK), cache(MLA_ROPE),
                  _full((MLA_KV_RANK, 1024), 2), pl.BlockSpec((QB, D), lambda b, n: (b * nq + n, 0)),
                  _full((8, 6 * D), 2), _full((D, D), 2), _full((1, D), 2),
                  _full((1, D), 2)],
        out_specs=pl.BlockSpec((QB, D), lambda b, n: (b * nq + n, 0)),
        out_shape=jax.ShapeDtypeStruct((T_DEN, D), f32),
        scratch_shapes=[pltpu.VMEM((QB, D), bf16), pltpu.VMEM((MLA_HEADS, MLA_KEYS, LANE), bf16),
                        pltpu.VMEM((MLA_KEYS, MLA_NN), bf16)]
        + [pltpu.VMEM((A_HEADS * QB, w), dt) for dt in (f32, bf16) for w in (WIN, CTX_L, MLA_KEYS)],
        compiler_params=_params(2),
        name="den_attn",
    )(sink, qa, ka, va, cache_k, cache_v, qm, kvl, kr, cache_ckv, cache_kr, w_ukv_p, x_all, mods0, w_out, ln_g, ln_b)


SUBTILE = 256


def _route(x1, mrow, rw_ref, rb_ref):
    sh, sc = mrow[:, 3 * D:4 * D], mrow[:, 4 * D:5 * D]
    h = x1 * (1.0 + sc) + sh
    h_hi = h.astype(bf16)
    h_lo = (h - h_hi.astype(f32)).astype(bf16)
    logits = _dot(h_hi, rw_ref[0]) + (_dot(h_hi, rw_ref[1]) + _dot(h_lo, rw_ref[0]))
    scores = jax.nn.sigmoid(logits)
    lane = lax.broadcasted_iota(jnp.int32, scores.shape, 1).astype(f32)
    sel = jnp.where(lane < N_EXPERTS, scores + rb_ref[...], -jnp.inf)
    gates = jnp.zeros_like(scores)
    for _ in range(TOP_K):
        m = sel.max(axis=-1, keepdims=True)
        idx = jnp.where(sel == m, lane, float(LANE)).min(axis=-1, keepdims=True)
        hit = lane == idx
        gates = jnp.where(hit, scores, gates)
        sel = jnp.where(hit, -jnp.inf, sel)
    return h_hi, gates / gates.sum(axis=-1, keepdims=True) * ROUTED_SCALE


def _router_kernel(xc_ref, xd_ref, mods_ref, rw_ref, rb_ref, x_ref, h_ref, gates_ref):
    i = pl.program_id(0)
    r = _mod_row(i, TOK_TILE)
    mrow = mods_ref[pl.ds(r, 1), :]
    x1 = _pick(i, TOK_TILE, xc_ref, xd_ref)
    x_ref[...] = x1
    h_ref[...], gates_ref[...] = _route(x1, mrow, rw_ref, rb_ref)


def _router(x1c, x1d, mods_l, router_w_p, router_b_p):
    row_spec = lambda w: pl.BlockSpec((TOK_TILE, w), lambda i: (i, 0))
    xc_spec, xd_spec = _two_stream_specs(TOK_TILE, D)
    return pl.pallas_call(
        _router_kernel,
        grid=(T_ALL // TOK_TILE,),
        in_specs=[xc_spec, xd_spec, _full((8, 6 * D), 1), _full((2, D, LANE), 1), _full((1, LANE), 1)],
        out_specs=[row_spec(D), row_spec(D), row_spec(LANE)],
        out_shape=[jax.ShapeDtypeStruct((T_ALL, D), f32), jax.ShapeDtypeStruct((T_ALL, D), bf16),
                   jax.ShapeDtypeStruct((T_ALL, LANE), f32)],
        compiler_params=_params(1),
        name="router",
    )(x1c, x1d, mods_l, router_w_p, router_b_p)


MOE_TOK = 1536
MOE_EG = 8
MOE_VMEM_LIMIT = 60 * 1024 * 1024
MOE_TILE = 512
MOE_FF = MOE_EG * EXPERT_FF


def _moe_kernel(h_ref, gates_ref, mods_ref, wg_ref, wu_ref, wd_ref, sg_ref, su_ref, sd_ref, o_ref):
    p = pl.program_id(0)
    e = pl.program_id(1)
    n_tiles = MOE_TOK // MOE_TILE

    def gate_f(t):
        r = _mod_row(p * n_tiles + t, MOE_TILE)
        return mods_ref[pl.ds(r, 1), 5 * D:6 * D]

    def rows_of(t):
        if isinstance(t, int):
            return pl.ds(t * MOE_TILE, MOE_TILE)
        return pl.ds(pl.multiple_of(t * MOE_TILE, MOE_TILE), MOE_TILE)

    @pl.when(e == 0)
    def _():
        sg = sg_ref[...].astype(bf16)
        su = su_ref[...].astype(bf16)
        sd = sd_ref[...].astype(bf16)

        def body(t, c):
            rows = rows_of(t)
            ht = h_ref[rows, :]
            hid = _silu(_dot(ht, sg)) * _dot(ht, su)
            o_ref[rows, :] = gate_f(t) * _dot(hid.astype(bf16), sd)
            return c

        lax.fori_loop(0, n_tiles, body, 0)

    wg = jnp.concatenate([wg_ref[k].astype(bf16) for k in range(MOE_EG)], axis=1)
    wu = jnp.concatenate([wu_ref[k].astype(bf16) for k in range(MOE_EG)], axis=1)
    wd = jnp.concatenate([wd_ref[k].astype(bf16) for k in range(MOE_EG)], axis=0)
    lane = lax.broadcasted_iota(jnp.int32, (MOE_TILE, LANE), 1)

    def body(t, c):
        rows = rows_of(t)
        ht = h_ref[rows, :]
        hid = _silu(_dot(ht, wg)) * _dot(ht, wu)
        gt = gates_ref[rows, :]
        parts = []
        for k in range(MOE_EG):
            col = jnp.where(lane == e * MOE_EG + k, gt, 0.0).sum(axis=-1, keepdims=True)
            parts.append((hid[:, EXPERT_FF * k:EXPERT_FF * (k + 1)] * col).astype(bf16))
        o_ref[rows, :] += gate_f(t) * _dot(jnp.concatenate(parts, axis=1), wd)
        return c

    for t in range(n_tiles):
        body(t, 0)


def _moe(l, h, gates, mods_l, wg, wu, wd, sg, su, sd):
    tok = lambda w: pl.BlockSpec((MOE_TOK, w), lambda p, e: (p, 0))
    return pl.pallas_call(
        _moe_kernel,
        grid=(T_ALL // MOE_TOK, N_EXPERTS // MOE_EG),
        in_specs=[tok(D), tok(LANE), _full((8, 6 * D), 2),
                  pl.BlockSpec((None, MOE_EG, D, EXPERT_FF), lambda p, e: (l, e, 0, 0)),
                  pl.BlockSpec((None, MOE_EG, D, EXPERT_FF), lambda p, e: (l, e, 0, 0)),
                  pl.BlockSpec((None, MOE_EG, EXPERT_FF, D), lambda p, e: (l, e, 0, 0)),
                  pl.BlockSpec((None, D, SHARED_FF), lambda p, e: (l, 0, 0)),
                  pl.BlockSpec((None, D, SHARED_FF), lambda p, e: (l, 0, 0)),
                  pl.BlockSpec((None, SHARED_FF, D), lambda p, e: (l, 0, 0))],
        out_specs=tok(D),
        out_shape=jax.ShapeDtypeStruct((T_ALL, D), f32),
        compiler_params=pltpu.CompilerParams(dimension_semantics=("arbitrary", "arbitrary"),
                                             vmem_limit_bytes=MOE_VMEM_LIMIT),
        name="moe",
    )(h, gates, mods_l, wg, wu, wd, sg, su, sd)


def _moe_finish_kernel(x_ref, acc_ref, g_ref, b_ref, o_ref):
    o_ref[...] = _layer_norm(ALPHA * x_ref[...] + acc_ref[...], g_ref[...], b_ref[...])


def _moe_finish(x1, acc, tile0, n_rows, ln_g, ln_b):
    src = pl.BlockSpec((TOK_TILE, D), lambda i: (tile0 + i, 0))
    return pl.pallas_call(
        _moe_finish_kernel,
        grid=(n_rows // TOK_TILE,),
        in_specs=[src, src, _full((1, D), 1), _full((1, D), 1)],
        out_specs=pl.BlockSpec((TOK_TILE, D), lambda i: (i, 0)),
        out_shape=jax.ShapeDtypeStruct((n_rows, D), f32),
        compiler_params=_params(1),
        name="moe_finish",
    )(x1, acc, ln_g, ln_b)


def _router_weights(l, router_w, router_bias):
    rw = jnp.pad(router_w[l], ((0, 0), (0, LANE - N_EXPERTS)))
    rw_hi, rw_lo = _split_bf16(rw)
    rb = jnp.pad(router_bias[l], (0, LANE - N_EXPERTS)).reshape(1, LANE)
    return jnp.stack([rw_hi, rw_lo]), rb


S5_Q = 8
S5_NGB = D // LANE


def _s5_in_kernel(x_ref, acc_ref, lg_ref, lb_ref, mods_ref, w_ref, x2_ref, u_ref, u2_ref, slab_ref, *, row_of,
                  seq_len):
    r = row_of(pl.program_id(0))
    mrow = mods_ref[pl.ds(r, 1), :]
    sh, sc = mrow[:, 0:D], mrow[:, D:2 * D]
    x2 = _layer_norm(ALPHA * x_ref[...] + acc_ref[...], lg_ref[...], lb_ref[...])
    x2_ref[...] = x2
    h = (x2 * (1.0 + sc) + sh).astype(bf16)
    u = _dot(h, w_ref[...])
    u_ref[...] = u
    for s in range(S5_NGB):
        slab_ref[s] = u[:, LANE * s:LANE * (s + 1)]
    kt = seq_len // S5_Q
    for s in range(S5_NGB):
        for q in range(TOK_TILE // seq_len):
            for j in range(S5_Q):
                u2_ref[s, q * kt:(q + 1) * kt, LANE * j:LANE * (j + 1)] = (
                    slab_ref[s, pl.ds(q * seq_len + j, kt, stride=S5_Q), :].astype(bf16))


def _s5_in(x1, acc, ln_g, ln_b, tile0, mods1, w_in_c, n_b, seq_len, row_of):
    n_tiles = n_b * seq_len // TOK_TILE
    chunks = TOK_TILE // S5_Q
    src = pl.BlockSpec((TOK_TILE, D), lambda i: (tile0 + i, 0))
    dst = pl.BlockSpec((TOK_TILE, D), lambda i: (i, 0))
    return pl.pallas_call(
        functools.partial(_s5_in_kernel, row_of=row_of, seq_len=min(seq_len, TOK_TILE)),
        grid=(n_tiles,),
        in_specs=[src, src, _full((1, D), 1), _full((1, D), 1), _full((8, 6 * D), 1), _full((D, D), 1)],
        out_specs=[dst, dst, pl.BlockSpec((S5_NGB, chunks, D), lambda i: (0, i, 0))],
        out_shape=[jax.ShapeDtypeStruct((n_b * seq_len, D), f32), jax.ShapeDtypeStruct((n_b * seq_len, D), f32),
                   jax.ShapeDtypeStruct((S5_NGB, n_tiles * chunks, D), bf16)],
        scratch_shapes=[pltpu.VMEM((S5_NGB, TOK_TILE, LANE), f32)],
        compiler_params=_params(1),
        name="s5_in",
    )(x1, acc, ln_g, ln_b, mods1, w_in_c)


S5_GL = (LANE // S5_CH) * S5_P
S5_ROWS_C = (CTX_L // S5_Q) * N_CTX_B
S5_ROWS_D = (DEN_L // S5_Q) * N_DEN_B


def _s5_scan_kernel(lre_ref, lim_ref, ldt_ref, btr_ref, bti_ref, ctr_ref, cti_ref, uc_ref, ud_ref, h0_ref,
                    yc_ref, yd_ref, st_ref, win_ref, mso_ref, wit_ref, a_ref, s_ref, hp_ref):
    gl = S5_GL
    rowg = lax.shift_right_logical(lax.broadcasted_iota(jnp.int32, (LANE, gl), 0), 4)
    colg = lax.shift_right_logical(lax.broadcasted_iota(jnp.int32, (LANE, gl), 1), 6)
    same_group = rowg == colg
    reps = LANE // S5_CH

    def expand(t):
        return jnp.where(same_group, jnp.concatenate([t] * reps, axis=0), 0.0)

    def expand_c(t):
        return jnp.where(same_group, jnp.concatenate([t] * reps, axis=1), 0.0)

    for d in range(2):
        fwd = d == 0
        lre, lim = lre_ref[d], lim_ref[d]
        dt = jnp.exp(ldt_ref[d])
        a, w = lre * dt, lim * dt
        pre = [jnp.exp(m * a) * jnp.cos(m * w) for m in range(S5_Q + 1)]
        pim = [jnp.exp(m * a) * jnp.sin(m * w) for m in range(S5_Q + 1)]
        xr, xi = pre[1] - 1.0, pim[1]
        den = lre * lre + lim * lim
        cfr, cfi = (xr * lre + xi * lim) / den, (xi * lre - xr * lim) / den
        btr, bti = btr_ref[d], bti_ref[d]
        bexp_r = expand(cfr * btr - cfi * bti)
        bexp_i = expand(cfr * bti + cfi * btr)
        cexp_r, cexp_i = expand_c(ctr_ref[d]), expand_c(cti_ref[d])
        for m in range(S5_Q + 1):
            a_ref[m, :, 0:gl] = cexp_r * pre[m] - cexp_i * pim[m]
            a_ref[m, :, gl:2 * gl] = -(cexp_r * pim[m] + cexp_i * pre[m])
        for j in range(S5_Q):
            m = S5_Q - 1 - j if fwd else j
            win_ref[LANE * j:LANE * (j + 1), 0:gl] = (pre[m] * bexp_r - pim[m] * bexp_i).astype(bf16)
            win_ref[LANE * j:LANE * (j + 1), gl:2 * gl] = (pre[m] * bexp_i + pim[m] * bexp_r).astype(bf16)
        for j in range(S5_Q):
            m = j + 1 if fwd else S5_Q - j
            mso_ref[LANE * j:LANE * (j + 1), :] = a_ref[m].astype(bf16)
        b2 = jnp.concatenate([bexp_r, bexp_i], axis=1).astype(bf16)
        kt = [_dot_nt(b2, a_ref[tau].astype(bf16)) for tau in range(S5_Q)]
        for j in range(S5_Q):
            for jp in range(S5_Q):
                tau = jp - j if fwd else j - jp
                blk = slice(LANE * j, LANE * (j + 1)), slice(LANE * jp, LANE * (jp + 1))
                if fwd:
                    wit_ref[blk] = kt[tau] if tau >= 0 else jnp.zeros((LANE, LANE), f32)
                elif tau >= 0:
                    wit_ref[blk] = wit_ref[blk] + kt[tau]

        l8r, l8i = pre[S5_Q], pim[S5_Q]

        nsl = gl // LANE

        def slabs(ref, rs, first):
            return jnp.concatenate([ref[first + sl, rs, :] for sl in range(nsl)], axis=1)

        def put_slabs(ref, rs, first, val):
            for sl in range(nsl):
                ref[first + sl, rs, :] = val[:, LANE * sl:LANE * (sl + 1)]

        def advance(hr, hi_, sr, si):
            return l8r * hr - l8i * hi_ + sr, l8r * hi_ + l8i * hr + si

        def run(u_ref, y_ref, n_b, n_k, h_init):
            rows = n_b * n_k
            s = _dot(u_ref[0], win_ref[...])
            if n_b % SUB == 0:
                pitch = n_k + 1
                for bb in range(n_b):
                    dst = slice(bb * pitch, bb * pitch + n_k)
                    put_slabs(s_ref, dst, 0, s[bb * n_k:(bb + 1) * n_k, 0:gl])
                    put_slabs(s_ref, dst, nsl, s[bb * n_k:(bb + 1) * n_k, gl:2 * gl])

                def step(i, carry):
                    hr, hi_ = carry
                    rs = pl.ds(i if fwd else n_k - 1 - i, n_b, stride=pitch)
                    put_slabs(hp_ref, rs, 0, hr)
                    put_slabs(hp_ref, rs, nsl, hi_)
                    return advance(hr, hi_, slabs(s_ref, rs, 0), slabs(s_ref, rs, nsl))

                h_fin = lax.fori_loop(0, n_k, step, h_init)
                hp = jnp.concatenate(
                    [jnp.concatenate([hp_ref[sl, bb * pitch:bb * pitch + n_k, :] for sl in range(2 * nsl)], axis=1)
                     for bb in range(n_b)], axis=0).astype(bf16)
            else:
                put_slabs(s_ref, slice(0, rows), 0, s[:, 0:gl])
                put_slabs(s_ref, slice(0, rows), nsl, s[:, gl:2 * gl])
                n_it = n_k // SUB

                def step(i, carry):
                    it = i if fwd else n_it - 1 - i
                    out = []
                    for bb in range(n_b):
                        hr, hi_ = carry[bb]
                        rs = pl.ds(pl.multiple_of(bb * n_k + it * SUB, SUB), SUB)
                        s_re, s_im = slabs(s_ref, rs, 0), slabs(s_ref, rs, nsl)
                        prev_r, prev_i = [None] * SUB, [None] * SUB
                        for sub in (range(SUB) if fwd else reversed(range(SUB))):
                            prev_r[sub], prev_i[sub] = hr, hi_
                            hr, hi_ = advance(hr, hi_, s_re[sub:sub + 1], s_im[sub:sub + 1])
                        put_slabs(hp_ref, rs, 0, jnp.concatenate(prev_r, axis=0))
                        put_slabs(hp_ref, rs, nsl, jnp.concatenate(prev_i, axis=0))
                        out.append((hr, hi_))
                    return tuple(out)

                fin = lax.fori_loop(0, n_it, step, tuple((h_init[0][bb:bb + 1], h_init[1][bb:bb + 1])
                                                          for bb in range(n_b)))
                h_fin = (jnp.concatenate([f[0] for f in fin], axis=0), jnp.concatenate([f[1] for f in fin], axis=0))
                hp = jnp.concatenate([hp_ref[sl, 0:rows, :] for sl in range(2 * nsl)], axis=1).astype(bf16)
            y = _dot_nt(hp, mso_ref[...])
            if fwd:
                y_ref[0] = y
            else:
                y_ref[0] += y
            return h_fin

        zeros = jnp.zeros((N_CTX_B, gl), f32)
        hr, hi_ = run(uc_ref, yc_ref, N_CTX_B, CTX_L // S5_Q, (zeros, zeros))
        st_ref[d, 0] = hr
        st_ref[d, 1] = hi_
        run(ud_ref, yd_ref, N_DEN_B, DEN_L // S5_Q, (h0_ref[d, 0], h0_ref[d, 1]))

    wit = wit_ref[...].astype(bf16)
    yc_ref[0] += _dot(uc_ref[0], wit)
    yd_ref[0] += _dot(ud_ref[0], wit)


def _s5_scan(lam_re, lam_im, log_dt, bt_re, bt_im, ct_re, ct_im, u2c, u2d, h0):
    gl = S5_GL
    vec = pl.BlockSpec((2, 1, gl), lambda g: (0, 0, g))
    tab = pl.BlockSpec((2, S5_CH, gl), lambda g: (0, 0, g))
    ctab = pl.BlockSpec((2, LANE, S5_P), lambda g: (0, g, 0))
    rows = lambda n: pl.BlockSpec((1, n, D), lambda g: (g, 0, 0))
    return pl.pallas_call(
        _s5_scan_kernel,
        grid=(S5_NGB,),
        in_specs=[vec, vec, vec, tab, tab, ctab, ctab, rows(S5_ROWS_C), rows(S5_ROWS_D),
                  pl.BlockSpec((2, 2, N_DEN_B, gl), lambda g: (0, 0, 0, g))],
        out_specs=[rows(S5_ROWS_C), rows(S5_ROWS_D), pl.BlockSpec((2, 2, N_CTX_B, gl), lambda g: (0, 0, 0, g))],
        out_shape=[jax.ShapeDtypeStruct((S5_NGB, S5_ROWS_C, D), f32), jax.ShapeDtypeStruct((S5_NGB, S5_ROWS_D, D), f32),
                   jax.ShapeDtypeStruct((2, 2, N_CTX_B, S5_G * S5_P), f32)],
        scratch_shapes=[pltpu.VMEM((D, 2 * gl), bf16), pltpu.VMEM((D, 2 * gl), bf16), pltpu.VMEM((D, D), f32),
                        pltpu.VMEM((S5_Q + 1, LANE, 2 * gl), f32),
                        pltpu.VMEM((2 * gl // LANE, S5_ROWS_C + 2 * N_CTX_B, LANE), f32),
                        pltpu.VMEM((2 * gl // LANE, S5_ROWS_C + 2 * N_CTX_B, LANE), f32)],
        compiler_params=_params(1),
        name="s5_scan",
    )(lam_re, lam_im, log_dt, bt_re, bt_im, ct_re, ct_im, u2c, u2d, h0)


def _gelu_tanh(x):
    return 0.5 * x * (1.0 + jnp.tanh(np.sqrt(2.0 / np.pi).astype(np.float32) * (x + 0.044715 * (x * x * x))))


def _s5_out_kernel(xc_ref, xd_ref, uc_ref, ud_ref, yc_ref, yd_ref, mods_ref, dsk_ref, wout_ref, g_ref, b_ref, rw_ref,
                   rb_ref, x1_ref, h_ref, gates_ref, slab_ref):
    i = pl.program_id(0)
    is_ctx = i < T_CTX // TOK_TILE
    r = _mod_row(i, TOK_TILE)
    mrow = mods_ref[pl.ds(r, 1), :]
    u = _pick(i, TOK_TILE, uc_ref, ud_ref)

    def unchunk(y_ref, seq_len):
        kt = seq_len // S5_Q
        for s in range(S5_NGB):
            for q in range(TOK_TILE // seq_len):
                for j in range(S5_Q):
                    slab_ref[s, pl.ds(q * seq_len + j, kt, stride=S5_Q), :] = (
                        y_ref[s, q * kt:(q + 1) * kt, LANE * j:LANE * (j + 1)])

    lax.cond(is_ctx, lambda: unchunk(yc_ref, min(CTX_L, TOK_TILE)), lambda: unchunk(yd_ref, min(DEN_L, TOK_TILE)))
    halves = [slice(a, a + SUBTILE) for a in range(0, TOK_TILE, SUBTILE)]
    zs = []
    for rows in halves:
        y = jnp.concatenate([slab_ref[s, rows, :] for s in range(S5_NGB)], axis=1) + dsk_ref[...] * u[rows]
        zs.append(_dot(_gelu_tanh(y).astype(bf16), wout_ref[...]))
    for rows, z in zip(halves, zs):
        out = z[:, 0:D] * jax.nn.sigmoid(z[:, D:2 * D])
        x = jnp.where(is_ctx, xc_ref[rows, :], xd_ref[rows, :])
        x1 = _layer_norm(ALPHA * x + mrow[:, 2 * D:3 * D] * out, g_ref[...], b_ref[...])
        x1_ref[rows, :] = x1
        h_ref[rows, :], gates_ref[rows, :] = _route(x1, mrow, rw_ref, rb_ref)


def _s5_out(xc, xd, uc, ud, yc, yd, mods1, d_skip, w_out_c, ln_g, ln_b, rw, rb):
    row_spec = lambda w: pl.BlockSpec((TOK_TILE, w), lambda i: (i, 0))
    uc_spec, ud_spec = _two_stream_specs(TOK_TILE, D)
    n_ctx = T_CTX // TOK_TILE
    chunks = TOK_TILE // S5_Q
    return pl.pallas_call(
        _s5_out_kernel,
        grid=(T_ALL // TOK_TILE,),
        in_specs=[uc_spec, ud_spec, uc_spec, ud_spec,
                  pl.BlockSpec((S5_NGB, chunks, D), lambda i: (0, jnp.minimum(i, n_ctx - 1), 0)),
                  pl.BlockSpec((S5_NGB, chunks, D), lambda i: (0, jnp.maximum(i - n_ctx, 0), 0)),
                  _full((8, 6 * D), 1), _full((1, D), 1), _full((D, 2 * D), 1), _full((1, D), 1), _full((1, D), 1),
                  _full((2, D, LANE), 1), _full((1, LANE), 1)],
        out_specs=[row_spec(D), row_spec(D), row_spec(LANE)],
        out_shape=[jax.ShapeDtypeStruct((T_ALL, D), f32), jax.ShapeDtypeStruct((T_ALL, D), bf16),
                   jax.ShapeDtypeStruct((T_ALL, LANE), f32)],
        scratch_shapes=[pltpu.VMEM((S5_NGB, TOK_TILE, LANE), f32)],
        compiler_params=_params(1),
        name="s5_out",
    )(xc, xd, uc, ud, yc, yd, mods1, d_skip, w_out_c, ln_g, ln_b, rw, rb)


def kernel(x_prompt, x_sample, c, cache_attn_k, cache_attn_v, cache_mla_ckv, cache_mla_krope, state_ssm, c_ctx,
           ada_w, ada_b, ln_mix_g, ln_mix_b, ln_ffn_g, ln_ffn_b, w_in_ab, attn_sink, mla_q_norm, mla_kv_norm,
           mla_w_uq, mla_w_ukv, w_out_ab, w_in_c, s5_lam_re, s5_lam_im, s5_log_dt, s5_b_re, s5_b_im, s5_c_re,
           s5_c_im, s5_d, w_out_c, router_w, router_bias, exp_w_gate, exp_w_up, exp_w_down, sh_w_gate, sh_w_up,
           sh_w_down):
    row = lambda v: v.reshape(1, -1)
    xc, xd = x_prompt.reshape(T_CTX, D), x_sample.reshape(T_DEN, D)
    cvec8 = jnp.concatenate([c_ctx[None, :], c, jnp.zeros((8 - 1 - N_DEN_B, D), f32)], axis=0)
    mods = _adaln(cvec8, ada_w, ada_b)

    w_in_p = jnp.pad(w_in_ab[0], ((0, 0), (0, PROJ_W - w_in_ab.shape[-1]))).astype(bf16)
    uq = mla_w_uq[0].reshape(MLA_Q_RANK, MLA_HEADS, MLA_NOPE + MLA_ROPE)
    w_uq_p = jnp.concatenate([uq[:, :, :MLA_NOPE].reshape(MLA_Q_RANK, -1), uq[:, :, MLA_NOPE:].reshape(MLA_Q_RANK, -1)],
                             axis=1).astype(bf16)
    ukv = mla_w_ukv[0].reshape(MLA_KV_RANK, MLA_HEADS, MLA_NOPE + MLA_V)
    w_ukv_p = jnp.concatenate([ukv[:, :, :MLA_NOPE].reshape(MLA_KV_RANK, -1),
                               ukv[:, :, MLA_NOPE:].reshape(MLA_KV_RANK, -1)], axis=1).astype(bf16)
    qa, ka, va, ckv, kr, qm, kvl = _ab_proj(xc, xd, mods[0], w_in_p, row(mla_q_norm[0]), row(mla_kv_norm[0]),
                                            w_uq_p, w_ukv_p)
    w_out_b = w_out_ab[0].astype(bf16)
    g0, b0 = row(ln_mix_g[0]), row(ln_mix_b[0])
    x1c, nk_t, nv_t = _ctx_attn(attn_sink[0], qa, ka, va, qm, kvl, kr, xc, mods[0], w_out_b, g0, b0)
    new_attn_k = jnp.transpose(nk_t, (0, 1, 4, 2, 3))
    new_attn_v = jnp.transpose(nv_t, (0, 1, 4, 2, 3))
    x1d = _den_attn(attn_sink[0], qa, ka, va,
                    jnp.transpose(cache_attn_k[:, 0], (0, 2, 3, 1)), jnp.transpose(cache_attn_v[:, 0], (0, 2, 3, 1)),
                    qm, kvl, kr, cache_mla_ckv[:, 0], cache_mla_krope[:, 0], w_ukv_p, xd, mods[0], w_out_b, g0, b0)
    rw0, rb0 = _router_weights(0, router_w, router_bias)
    x1, h, gates = _router(x1c, x1d, mods[0], rw0, rb0)
    acc = _moe(0, h, gates, mods[0], exp_w_gate, exp_w_up, exp_w_down, sh_w_gate, sh_w_up, sh_w_down)

    w_in_c_b = w_in_c[0].astype(bf16)
    lg0, lb0 = row(ln_ffn_g[0]), row(ln_ffn_b[0])
    n_ctx_tiles = T_CTX // TOK_TILE
    x2c, uc, u2c = _s5_in(x1, acc, lg0, lb0, 0, mods[1], w_in_c_b, N_CTX_B, CTX_L, lambda i: 0)
    x2d, ud, u2d = _s5_in(x1, acc, lg0, lb0, n_ctx_tiles, mods[1], w_in_c_b, N_DEN_B, DEN_L,
                          lambda i: 1 + i // (DEN_L // TOK_TILE))
    gp = S5_G * S5_P
    chan_major_b = lambda t: jnp.transpose(t[0], (0, 3, 1, 2)).reshape(2, S5_CH, gp)
    chan_major_c = lambda t: t[0].reshape(2, S5_G * S5_CH, S5_P)
    h0 = jnp.transpose(state_ssm[:, 0], (1, 2, 0, 3, 4)).reshape(2, 2, N_DEN_B, gp)
    yc, yd, st = _s5_scan(s5_lam_re[0].reshape(2, 1, gp), s5_lam_im[0].reshape(2, 1, gp),
                          jnp.repeat(s5_log_dt[0], S5_P, axis=-1).reshape(2, 1, gp),
                          chan_major_b(s5_b_re), chan_major_b(s5_b_im), chan_major_c(s5_c_re), chan_major_c(s5_c_im),
                          u2c, u2d, h0)
    rw1, rb1 = _router_weights(1, router_w, router_bias)
    x3, h, gates = _s5_out(x2c, x2d, uc, ud, yc, yd, mods[1], row(s5_d[0]),
                           w_out_c[0].astype(bf16), row(ln_mix_g[1]), row(ln_mix_b[1]), rw1, rb1)
    acc = _moe(1, h, gates, mods[1], exp_w_gate, exp_w_up, exp_w_down, sh_w_gate, sh_w_up, sh_w_down)
    lg1, lb1 = row(ln_ffn_g[1]), row(ln_ffn_b[1])
    y_prompt = _moe_finish(x3, acc, 0, T_CTX, lg1, lb1).reshape(N_CTX_B, CTX_L, D)
    y_sample = _moe_finish(x3, acc, n_ctx_tiles, T_DEN, lg1, lb1).reshape(N_DEN_B, DEN_L, D)
    new_mla_ckv = ckv[:T_CTX].reshape(N_CTX_B, 1, CTX_L, MLA_KV_RANK)
    new_mla_krope = kr[:T_CTX, :MLA_ROPE].reshape(N_CTX_B, 1, CTX_L, MLA_ROPE)
    new_state_ssm = jnp.transpose(st, (2, 0, 1, 3)).reshape(N_CTX_B, 1, 2, 2, S5_G, S5_P)
    return (y_prompt, y_sample, new_attn_k, new_attn_v, new_mla_ckv, new_mla_krope, new_state_ssm)
```

```python
import functools

import jax
import jax.numpy as jnp
import numpy as np
from jax import lax
from jax.experimental import pallas as pl
from jax.experimental.pallas import tpu as pltpu

f32 = jnp.float32
bf16 = jnp.bfloat16

D = 1024
N_CTX_B, CTX_L = 16, 256
N_DEN_B, DEN_L = 2, 1024
T_CTX = N_CTX_B * CTX_L
T_DEN = N_DEN_B * DEN_L
T_ALL = T_CTX + T_DEN
GRID_W = 64
WINDOW = 128
ROPE_BASE = 10000.0
A_HEADS, A_KV_HEADS, A_HD = 8, 2, 64
A_GROUP = A_HEADS // A_KV_HEADS
A_SCALE = A_HD ** -0.5
MLA_HEADS, MLA_Q_RANK, MLA_KV_RANK = 8, 256, 128
MLA_NOPE, MLA_ROPE, MLA_V = 64, 32, 64
MLA_SCALE = (MLA_NOPE + MLA_ROPE) ** -0.5
N_EXPERTS, TOP_K, EXPERT_FF, SHARED_FF = 64, 6, 128, 128
ROUTED_SCALE = 2.5
DEPTH = 2
ALPHA = (2.0 * DEPTH) ** 0.25
LN_EPS = 1e-5
RMS_EPS = 1e-6
NEG_INF = -1e30
S5_G, S5_CH, S5_P = 64, 16, 64

LANE = 128
SUB = 8
VMEM_LIMIT = 56 * 1024 * 1024

TOK_TILE = 512


def _mod_row(tile_idx, tile_rows):
    start = tile_idx * tile_rows
    return jnp.where(start < T_CTX, 0, 1 + (start - T_CTX) // DEN_L)


def _layer_norm(y, g, b):
    mu = jnp.mean(y, axis=-1, keepdims=True)
    yc = y - mu
    var = jnp.mean(yc * yc, axis=-1, keepdims=True)
    return yc * lax.rsqrt(var + LN_EPS) * g + b


def _silu(x):
    return x * jax.nn.sigmoid(x)


def _dot(a, b):
    return jnp.dot(a, b, preferred_element_type=f32)


def _dot_nt(a, b):
    return lax.dot_general(a, b, (((1,), (1,)), ((), ())), preferred_element_type=f32)


def _split_bf16(a):
    hi = a.astype(bf16)
    return hi, (a - hi.astype(f32)).astype(bf16)


def _full(shape, n_grid):
    zeros = tuple(0 for _ in shape)
    return pl.BlockSpec(shape, lambda *_: zeros)


def _two_stream_specs(tile_rows, width):
    n_ctx = T_CTX // tile_rows
    return (pl.BlockSpec((tile_rows, width), lambda i: (jnp.minimum(i, n_ctx - 1), 0)),
            pl.BlockSpec((tile_rows, width), lambda i: (jnp.maximum(i - n_ctx, 0), 0)))


def _pick(i, tile_rows, ctx_ref, den_ref):
    return jnp.where(i < T_CTX // tile_rows, ctx_ref[...], den_ref[...])


def _params(n_grid):
    return pltpu.CompilerParams(dimension_semantics=("arbitrary",) * n_grid, vmem_limit_bytes=VMEM_LIMIT)


ADA_TN = 1536


def _adaln_kernel(c_ref, w_ref, b_ref, o_ref):
    s_hi, s_lo = _split_bf16(_silu(c_ref[...]))
    w_hi, w_lo = _split_bf16(w_ref[0])
    o_ref[0] = _dot(s_hi, w_hi) + (_dot(s_hi, w_lo) + _dot(s_lo, w_hi)) + b_ref[0]


def _adaln(cvec8, ada_w, ada_b):
    n = 6 * D
    return pl.pallas_call(
        _adaln_kernel,
        grid=(DEPTH, n // ADA_TN),
        in_specs=[
            pl.BlockSpec((8, D), lambda l, j: (0, 0)),
            pl.BlockSpec((1, D, ADA_TN), lambda l, j: (l, 0, j)),
            pl.BlockSpec((1, 1, ADA_TN), lambda l, j: (l, 0, j)),
        ],
        out_specs=pl.BlockSpec((1, 8, ADA_TN), lambda l, j: (l, 0, j)),
        out_shape=jax.ShapeDtypeStruct((DEPTH, 8, n), f32),
        compiler_params=_params(2),
        name="adaln",
    )(cvec8, ada_w, ada_b.reshape(DEPTH, 1, n))


def _rope_table_array(head_dim):
    q = head_dim // 4
    pos = np.arange(DEN_L)
    row, col = (pos // GRID_W).astype(np.float64), (pos % GRID_W).astype(np.float64)
    lane = np.arange(LANE) % head_dim
    is_col = lane >= head_dim // 2
    w = lane % (head_dim // 2)
    first = w < q
    inv_freq = ROPE_BASE ** (-np.arange(q, dtype=np.float64) / q)
    ang = np.where(is_col[None, :], col[:, None], row[:, None]) * inv_freq[w % q][None, :]
    cos, sin = np.cos(ang), np.sin(ang)
    sin_a = np.where(first[None, :], -sin, 0.0)
    sin_b = np.where(first[None, :], 0.0, sin)
    ident = np.stack([np.ones((TOK_TILE, LANE)), np.zeros((TOK_TILE, LANE)), np.zeros((TOK_TILE, LANE))])
    tab = np.concatenate([ident, np.stack([cos, sin_a, sin_b])], axis=1).astype(np.float32)
    return jnp.asarray(tab), q


def _rope_chunk(x, tab_ref, q):
    return x * tab_ref[0] + pltpu.roll(x, LANE - q, 1) * tab_ref[1] + pltpu.roll(x, q, 1) * tab_ref[2]


PROJ_W = 1280
C_QA, C_KA, C_VA, C_CQ, C_CKV, C_KR = 0, 512, 640, 768, 1024, 1152
MLA_NN = MLA_HEADS * MLA_NOPE


def _ab_proj_kernel(xc_ref, xd_ref, mods_ref, w_ref, qn_ref, kvn_ref, wuq_ref, wukv_ref, ta_ref, tm_ref,
                    qa_ref, ka_ref, va_ref, ckv_ref, kr_ref, qm_ref, kvl_ref, *, qa_shift, qm_shift):
    i = pl.program_id(0)
    r = _mod_row(i, TOK_TILE)
    mrow = mods_ref[pl.ds(r, 1), :]
    sh, sc = mrow[:, 0:D], mrow[:, D:2 * D]
    x = _pick(i, TOK_TILE, xc_ref, xd_ref)
    h = (x * (1.0 + sc) + sh).astype(bf16)
    proj = _dot(h, w_ref[...])
    for j in range(4):
        c0 = C_QA + LANE * j
        qa_ref[:, LANE * j:LANE * (j + 1)] = _rope_chunk(proj[:, c0:c0 + LANE], ta_ref, qa_shift).astype(bf16)
    ka_ref[...] = _rope_chunk(proj[:, C_KA:C_KA + LANE], ta_ref, qa_shift)
    va_ref[...] = proj[:, C_VA:C_VA + LANE]
    cq = proj[:, C_CQ:C_CQ + MLA_Q_RANK]
    cq = cq * lax.rsqrt(jnp.mean(cq * cq, axis=-1, keepdims=True) + RMS_EPS) * qn_ref[...]
    ckv = proj[:, C_CKV:C_CKV + MLA_KV_RANK]
    ckv = ckv * lax.rsqrt(jnp.mean(ckv * ckv, axis=-1, keepdims=True) + RMS_EPS) * kvn_ref[...]
    ckv_ref[...] = ckv
    kr_ref[...] = _rope_chunk(proj[:, C_KR:C_KR + LANE], tm_ref, qm_shift)
    qm = _dot(cq.astype(bf16), wuq_ref[...])
    qm_ref[:, 0:MLA_NN] = qm[:, 0:MLA_NN].astype(bf16)
    for j in range(2):
        c0 = MLA_NN + LANE * j
        qm_ref[:, c0:c0 + LANE] = _rope_chunk(qm[:, c0:c0 + LANE], tm_ref, qm_shift).astype(bf16)
    kvl_ref[...] = _dot(ckv.astype(bf16), wukv_ref[...]).astype(bf16)


def _rope_block_index(i):
    tiles_ctx = T_CTX // TOK_TILE
    per_seq = DEN_L // TOK_TILE
    return jnp.where(i < tiles_ctx, 0, 1 + (i - tiles_ctx) % per_seq)


def _ab_proj(xc, xd, mods0, w_in_p, q_norm, kv_norm, w_uq_p, w_ukv_p):
    tab_a, qa_shift = _rope_table_array(A_HD)
    tab_m, qm_shift = _rope_table_array(MLA_ROPE)
    row_spec = lambda w: pl.BlockSpec((TOK_TILE, w), lambda i: (i, 0))
    xc_spec, xd_spec = _two_stream_specs(TOK_TILE, D)
    tab_spec = pl.BlockSpec((3, TOK_TILE, LANE), lambda i: (0, _rope_block_index(i), 0))
    outs = [(512, bf16), (LANE, f32), (LANE, f32), (LANE, f32), (LANE, f32), (768, bf16), (1024, bf16)]
    return pl.pallas_call(
        functools.partial(_ab_proj_kernel, qa_shift=qa_shift, qm_shift=qm_shift),
        grid=(T_ALL // TOK_TILE,),
        in_specs=[xc_spec, xd_spec, _full((8, 6 * D), 1), _full((D, PROJ_W), 1), _full((1, MLA_Q_RANK), 1),
                  _full((1, MLA_KV_RANK), 1), _full((MLA_Q_RANK, 768), 1), _full((MLA_KV_RANK, 1024), 1),
                  tab_spec, tab_spec],
        out_specs=[row_spec(w) for w, _ in outs],
        out_shape=[jax.ShapeDtypeStruct((T_ALL, w), dt) for w, dt in outs],
        compiler_params=_params(1),
        name="ab_proj",
    )(xc, xd, mods0, w_in_p, q_norm, kv_norm, w_uq_p, w_ukv_p, tab_a, tab_m)


def _softmax_blocks(s_refs, p_refs, sink_col=None):
    m = s_refs[0][...].max(axis=-1, keepdims=True)
    for s_ref in s_refs[1:]:
        m = jnp.maximum(m, s_ref[...].max(axis=-1, keepdims=True))
    if sink_col is not None:
        m = jnp.maximum(m, sink_col)
    l = None
    for s_ref, p_ref in zip(s_refs, p_refs):
        p = jnp.exp(s_ref[...] - m).astype(bf16)
        p_ref[...] = p
        ps = _dot(p, jnp.ones((p.shape[1], LANE), bf16))[:, 0:1]
        l = ps if l is None else l + ps
    if sink_col is not None:
        l = l + jnp.exp(sink_col - m)
    return 1.0 / l


def _sink_column(sink_ref, rows_per_head):
    return jnp.concatenate([jnp.full((rows_per_head, 1), sink_ref[h], f32) for h in range(A_HEADS)], axis=0)


def _mla_q(qm_ref, h):
    rows = qm_ref.shape[0]
    return jnp.concatenate([qm_ref[:, MLA_NOPE * h:MLA_NOPE * (h + 1)],
                            qm_ref[:, MLA_NN + MLA_ROPE * h:MLA_NN + MLA_ROPE * (h + 1)],
                            jnp.zeros((rows, LANE - MLA_NOPE - MLA_ROPE), bf16)], axis=1)


def _mla_k(k_nope_h, k_rope):
    rows = k_nope_h.shape[0]
    return jnp.concatenate([k_nope_h, k_rope, jnp.zeros((rows, LANE - MLA_NOPE - MLA_ROPE), bf16)], axis=1)


def _mix_out_ln(merged_ref, wout_ref, x, mods_ref, r, g_ref, b_ref):
    out = _dot(merged_ref[...], wout_ref[...])
    gate = mods_ref[pl.ds(r, 1), 2 * D:3 * D]
    return _layer_norm(ALPHA * x + gate * out, g_ref[...], b_ref[...])


def _ctx_attn_kernel(sink_ref, qa_ref, ka_ref, va_ref, qm_ref, kvl_ref, kr_ref, x_ref, mods_ref, wout_ref,
                     g_ref, b_ref, o_ref, nk_ref, nv_ref, merged_ref, sa_ref, sm_ref, pa_ref, pm_ref):
    nk_ref[0, 0] = ka_ref[...].T.reshape(A_KV_HEADS, A_HD, CTX_L)
    nv_ref[0, 0] = va_ref[...].T.reshape(A_KV_HEADS, A_HD, CTX_L)
    n = CTX_L
    ka = ka_ref[...].astype(bf16)
    va = va_ref[...].astype(bf16)
    for j in range(A_KV_HEADS):
        q4 = jnp.concatenate([qa_ref[:, A_HD * h:A_HD * (h + 1)] for h in range(A_GROUP * j, A_GROUP * (j + 1))],
                             axis=0)
        sa_ref[A_GROUP * n * j:A_GROUP * n * (j + 1), :] = _dot_nt(q4, ka[:, A_HD * j:A_HD * (j + 1)]) * A_SCALE
    kr = kr_ref[:, 0:MLA_ROPE].astype(bf16)
    for h in range(MLA_HEADS):
        k_cat = _mla_k(kvl_ref[:, MLA_NOPE * h:MLA_NOPE * (h + 1)], kr)
        sm_ref[n * h:n * (h + 1), :] = _dot_nt(_mla_q(qm_ref, h), k_cat) * MLA_SCALE
    rla = _softmax_blocks([sa_ref], [pa_ref], _sink_column(sink_ref, n))
    rlm = _softmax_blocks([sm_ref], [pm_ref])
    for j in range(A_KV_HEADS):
        rows = slice(A_GROUP * n * j, A_GROUP * n * (j + 1))
        o4 = _dot(pa_ref[rows, :], va[:, A_HD * j:A_HD * (j + 1)]) * rla[rows]
        for g in range(A_GROUP):
            h = A_GROUP * j + g
            merged_ref[:, A_HD * h:A_HD * (h + 1)] = o4[n * g:n * (g + 1)].astype(bf16)
    for h in range(MLA_HEADS):
        rows = slice(n * h, n * (h + 1))
        v = kvl_ref[:, MLA_NN + MLA_V * h:MLA_NN + MLA_V * (h + 1)]
        merged_ref[:, MLA_NN + MLA_V * h:MLA_NN + MLA_V * (h + 1)] = (_dot(pm_ref[rows, :], v) * rlm[rows]).astype(bf16)
    o_ref[...] = _mix_out_ln(merged_ref, wout_ref, x_ref[...], mods_ref, 0, g_ref, b_ref)


def _ctx_attn(sink, qa, ka, va, qm, kvl, kr, x_all, mods0, w_out, ln_g, ln_b):
    blk = lambda w: pl.BlockSpec((CTX_L, w), lambda b: (b, 0))
    cache_blk = pl.BlockSpec((1, 1, A_KV_HEADS, A_HD, CTX_L), lambda b: (b, 0, 0, 0, 0))
    cache_shape = jax.ShapeDtypeStruct((N_CTX_B, 1, A_KV_HEADS, A_HD, CTX_L), f32)
    return pl.pallas_call(
        _ctx_attn_kernel,
        grid=(N_CTX_B,),
        in_specs=[pl.BlockSpec(memory_space=pltpu.SMEM), blk(512), blk(LANE), blk(LANE), blk(768), blk(1024),
                  blk(LANE), blk(D), _full((8, 6 * D), 1), _full((D, D), 1), _full((1, D), 1), _full((1, D), 1)],
        out_specs=[blk(D), cache_blk, cache_blk],
        out_shape=[jax.ShapeDtypeStruct((T_CTX, D), f32), cache_shape, cache_shape],
        scratch_shapes=[pltpu.VMEM((CTX_L, D), bf16),
                        pltpu.VMEM((A_HEADS * CTX_L, CTX_L), f32), pltpu.VMEM((MLA_HEADS * CTX_L, CTX_L), f32),
                        pltpu.VMEM((A_HEADS * CTX_L, CTX_L), bf16), pltpu.VMEM((MLA_HEADS * CTX_L, CTX_L), bf16)],
        compiler_params=_params(1),
        name="ctx_attn",
    )(sink, qa, ka, va, qm, kvl, kr, x_all, mods0, w_out, ln_g, ln_b)


QB = 256
WIN = QB + 2 * WINDOW
DEN_BLK0 = T_CTX // DEN_L
MLA_KEYS = CTX_L + DEN_L


def _den_attn_kernel(sink_ref, qa_ref, ka_ref, va_ref, cak_ref, cav_ref, qm_ref, kvl_ref, kr_ref, cckv_ref, ckr_ref,
                     wukv_ref, x_ref, mods_ref, wout_ref, g_ref, b_ref, o_ref, merged_ref, kcat_ref, vcat_ref,
                     saw_ref, sac_ref, sm_ref, paw_ref, pac_ref, pm_ref):
    b = pl.program_id(0)
    n = pl.program_id(1)

    @pl.when(n == 0)
    def _():
        kvc = _dot(cckv_ref[0].astype(bf16), wukv_ref[...]).astype(bf16)
        kr_ctx = ckr_ref[0].astype(bf16)
        kr_lat = kr_ref[:, 0:MLA_ROPE].astype(bf16)
        for h in range(MLA_HEADS):
            ns = slice(MLA_NOPE * h, MLA_NOPE * (h + 1))
            kcat_ref[h, 0:CTX_L, :] = _mla_k(kvc[:, ns], kr_ctx)
            kcat_ref[h, CTX_L:MLA_KEYS, :] = _mla_k(kvl_ref[:, ns], kr_lat)
        vcat_ref[0:CTX_L, :] = kvc[:, MLA_NN:2 * MLA_NN]
        vcat_ref[CTX_L:MLA_KEYS, :] = kvl_ref[:, MLA_NN:2 * MLA_NN]

    start = pl.multiple_of(jnp.clip(QB * n - WINDOW, 0, DEN_L - WIN), WINDOW)
    grp_rows = A_GROUP * QB
    qpos = QB * n + (lax.broadcasted_iota(jnp.int32, (grp_rows, WIN), 0) & (QB - 1))
    kpos = start + lax.broadcasted_iota(jnp.int32, (grp_rows, WIN), 1)
    valid = jnp.abs(qpos - kpos) <= WINDOW
    kwin = ka_ref[pl.ds(start, WIN), :].astype(bf16)
    vwin = va_ref[pl.ds(start, WIN), :].astype(bf16)
    kctx_t = [cak_ref[0, j].astype(bf16) for j in range(A_KV_HEADS)]
    vctx_t = [cav_ref[0, j].astype(bf16) for j in range(A_KV_HEADS)]
    for j in range(A_KV_HEADS):
        sl = slice(A_HD * j, A_HD * (j + 1))
        rows = slice(grp_rows * j, grp_rows * (j + 1))
        q4 = jnp.concatenate([qa_ref[:, A_HD * h:A_HD * (h + 1)] for h in range(A_GROUP * j, A_GROUP * (j + 1))],
                             axis=0)
        saw_ref[rows, :] = jnp.where(valid, _dot_nt(q4, kwin[:, sl]) * A_SCALE, NEG_INF)
        sac_ref[rows, :] = _dot(q4, kctx_t[j]) * A_SCALE
    for h in range(MLA_HEADS):
        sm_ref[QB * h:QB * (h + 1), :] = _dot_nt(_mla_q(qm_ref, h), kcat_ref[h]) * MLA_SCALE
    rla = _softmax_blocks([saw_ref, sac_ref], [paw_ref, pac_ref], _sink_column(sink_ref, QB))
    rlm = _softmax_blocks([sm_ref], [pm_ref])
    for j in range(A_KV_HEADS):
        sl = slice(A_HD * j, A_HD * (j + 1))
        rows = slice(grp_rows * j, grp_rows * (j + 1))
        o4 = (_dot(paw_ref[rows, :], vwin[:, sl]) + _dot_nt(pac_ref[rows, :], vctx_t[j])) * rla[rows]
        for g in range(A_GROUP):
            h = A_GROUP * j + g
            merged_ref[:, A_HD * h:A_HD * (h + 1)] = o4[QB * g:QB * (g + 1)].astype(bf16)
    for h in range(MLA_HEADS):
        rows = slice(QB * h, QB * (h + 1))
        o = _dot(pm_ref[rows, :], vcat_ref[:, MLA_V * h:MLA_V * (h + 1)]) * rlm[rows]
        merged_ref[:, MLA_NN + MLA_V * h:MLA_NN + MLA_V * (h + 1)] = o.astype(bf16)
    o_ref[...] = _mix_out_ln(merged_ref, wout_ref, x_ref[...], mods_ref, 1 + b, g_ref, b_ref)


def _den_attn(sink, qa, ka, va, cache_k, cache_v, qm, kvl, kr, cache_ckv, cache_kr, w_ukv_p, x_all, mods0, w_out,
              ln_g, ln_b):
    nq = DEN_L // QB
    qblk = lambda w: pl.BlockSpec((QB, w), lambda b, n: (T_CTX // QB + b * nq + n, 0))
    seq = lambda w: pl.BlockSpec((DEN_L, w), lambda b, n: (DEN_BLK0 + b, 0))
    cache = lambda w: pl.BlockSpec((1, CTX_L, w), lambda b, n: (b, 0, 0))
    cache_a = pl.BlockSpec((1, A_KV_HEADS, A_HD, CTX_L), lambda b, n: (b, 0, 0, 0))
    return pl.pallas_call(
        _den_attn_kernel,
        grid=(N_DEN_B, nq),
        in_specs=[pl.BlockSpec(memory_space=pltpu.SMEM), qblk(512), seq(LANE), seq(LANE), cache_a, cache_a,
                  qblk(768), seq(1024), seq(LANE), cache(MLA_KV_RANK), cache(MLA_ROPE),
                  _full((MLA_KV_RANK, 1024), 2), pl.BlockSpec((QB, D), lambda b, n: (b * nq + n, 0)),
                  _full((8, 6 * D), 2), _full((D, D), 2), _full((1, D), 2),
                  _full((1, D), 2)],
        out_specs=pl.BlockSpec((QB, D), lambda b, n: (b * nq + n, 0)),
        out_shape=jax.ShapeDtypeStruct((T_DEN, D), f32),
        scratch_shapes=[pltpu.VMEM((QB, D), bf16), pltpu.VMEM((MLA_HEADS, MLA_KEYS, LANE), bf16),
                        pltpu.VMEM((MLA_KEYS, MLA_NN), bf16)]
        + [pltpu.VMEM((A_HEADS * QB, w), dt) for dt in (f32, bf16) for w in (WIN, CTX_L, MLA_KEYS)],
        compiler_params=_params(2),
        name="den_attn",
    )(sink, qa, ka, va, cache_k, cache_v, qm, kvl, kr, cache_ckv, cache_kr, w_ukv_p, x_all, mods0, w_out, ln_g, ln_b)


SUBTILE = 256


def _route(x1, mrow, rw_ref, rb_ref):
    sh, sc = mrow[:, 3 * D:4 * D], mrow[:, 4 * D:5 * D]
    h = x1 * (1.0 + sc) + sh
    h_hi = h.astype(bf16)
    h_lo = (h - h_hi.astype(f32)).astype(bf16)
    logits = _dot(h_hi, rw_ref[0]) + (_dot(h_hi, rw_ref[1]) + _dot(h_lo, rw_ref[0]))
    scores = jax.nn.sigmoid(logits)
    lane = lax.broadcasted_iota(jnp.int32, scores.shape, 1).astype(f32)
    sel = jnp.where(lane < N_EXPERTS, scores + rb_ref[...], -jnp.inf)
    gates = jnp.zeros_like(scores)
    for _ in range(TOP_K):
        m = sel.max(axis=-1, keepdims=True)
        idx = jnp.where(sel == m, lane, float(LANE)).min(axis=-1, keepdims=True)
        hit = lane == idx
        gates = jnp.where(hit, scores, gates)
        sel = jnp.where(hit, -jnp.inf, sel)
    return h_hi, gates / gates.sum(axis=-1, keepdims=True) * ROUTED_SCALE


def _router_kernel(xc_ref, xd_ref, mods_ref, rw_ref, rb_ref, x_ref, h_ref, gates_ref):
    i = pl.program_id(0)
    r = _mod_row(i, TOK_TILE)
    mrow = mods_ref[pl.ds(r, 1), :]
    x1 = _pick(i, TOK_TILE, xc_ref, xd_ref)
    x_ref[...] = x1
    h_ref[...], gates_ref[...] = _route(x1, mrow, rw_ref, rb_ref)


def _router(x1c, x1d, mods_l, router_w_p, router_b_p):
    row_spec = lambda w: pl.BlockSpec((TOK_TILE, w), lambda i: (i, 0))
    xc_spec, xd_spec = _two_stream_specs(TOK_TILE, D)
    return pl.pallas_call(
        _router_kernel,
        grid=(T_ALL // TOK_TILE,),
        in_specs=[xc_spec, xd_spec, _full((8, 6 * D), 1), _full((2, D, LANE), 1), _full((1, LANE), 1)],
        out_specs=[row_spec(D), row_spec(D), row_spec(LANE)],
        out_shape=[jax.ShapeDtypeStruct((T_ALL, D), f32), jax.ShapeDtypeStruct((T_ALL, D), bf16),
                   jax.ShapeDtypeStruct((T_ALL, LANE), f32)],
        compiler_params=_params(1),
        name="router",
    )(x1c, x1d, mods_l, router_w_p, router_b_p)


MOE_TOK = 1536
MOE_EG = 8
MOE_VMEM_LIMIT = 60 * 1024 * 1024
MOE_TILE = 512
MOE_FF = MOE_EG * EXPERT_FF


def _moe_kernel(h_ref, gates_ref, mods_ref, wg_ref, wu_ref, wd_ref, sg_ref, su_ref, sd_ref, o_ref):
    p = pl.program_id(0)
    e = pl.program_id(1)
    n_tiles = MOE_TOK // MOE_TILE

    def gate_f(t):
        r = _mod_row(p * n_tiles + t, MOE_TILE)
        return mods_ref[pl.ds(r, 1), 5 * D:6 * D]

    def rows_of(t):
        if isinstance(t, int):
            return pl.ds(t * MOE_TILE, MOE_TILE)
        return pl.ds(pl.multiple_of(t * MOE_TILE, MOE_TILE), MOE_TILE)

    @pl.when(e == 0)
    def _():
        sg = sg_ref[...].astype(bf16)
        su = su_ref[...].astype(bf16)
        sd = sd_ref[...].astype(bf16)

        def body(t, c):
            rows = rows_of(t)
            ht = h_ref[rows, :]
            hid = _silu(_dot(ht, sg)) * _dot(ht, su)
            o_ref[rows, :] = gate_f(t) * _dot(hid.astype(bf16), sd)
            return c

        lax.fori_loop(0, n_tiles, body, 0)

    wg = jnp.concatenate([wg_ref[k].astype(bf16) for k in range(MOE_EG)], axis=1)
    wu = jnp.concatenate([wu_ref[k].astype(bf16) for k in range(MOE_EG)], axis=1)
    wd = jnp.concatenate([wd_ref[k].astype(bf16) for k in range(MOE_EG)], axis=0)
    lane = lax.broadcasted_iota(jnp.int32, (MOE_TILE, LANE), 1)

    def body(t, c):
        rows = rows_of(t)
        ht = h_ref[rows, :]
        hid = _silu(_dot(ht, wg)) * _dot(ht, wu)
        gt = gates_ref[rows, :]
        parts = []
        for k in range(MOE_EG):
            col = jnp.where(lane == e * MOE_EG + k, gt, 0.0).sum(axis=-1, keepdims=True)
            parts.append((hid[:, EXPERT_FF * k:EXPERT_FF * (k + 1)] * col).astype(bf16))
        o_ref[rows, :] += gate_f(t) * _dot(jnp.concatenate(parts, axis=1), wd)
        return c

    for t in range(n_tiles):
        body(t, 0)


def _moe(l, h, gates, mods_l, wg, wu, wd, sg, su, sd):
    tok = lambda w: pl.BlockSpec((MOE_TOK, w), lambda p, e: (p, 0))
    return pl.pallas_call(
        _moe_kernel,
        grid=(T_ALL // MOE_TOK, N_EXPERTS // MOE_EG),
        in_specs=[tok(D), tok(LANE), _full((8, 6 * D), 2),
                  pl.BlockSpec((None, MOE_EG, D, EXPERT_FF), lambda p, e: (l, e, 0, 0)),
                  pl.BlockSpec((None, MOE_EG, D, EXPERT_FF), lambda p, e: (l, e, 0, 0)),
                  pl.BlockSpec((None, MOE_EG, EXPERT_FF, D), lambda p, e: (l, e, 0, 0)),
                  pl.BlockSpec((None, D, SHARED_FF), lambda p, e: (l, 0, 0)),
                  pl.BlockSpec((None, D, SHARED_FF), lambda p, e: (l, 0, 0)),
                  pl.BlockSpec((None, SHARED_FF, D), lambda p, e: (l, 0, 0))],
        out_specs=tok(D),
        out_shape=jax.ShapeDtypeStruct((T_ALL, D), f32),
        compiler_params=pltpu.CompilerParams(dimension_semantics=("arbitrary", "arbitrary"),
                                             vmem_limit_bytes=MOE_VMEM_LIMIT),
        name="moe",
    )(h, gates, mods_l, wg, wu, wd, sg, su, sd)


def _moe_finish_kernel(x_ref, acc_ref, g_ref, b_ref, o_ref):
    o_ref[...] = _layer_norm(ALPHA * x_ref[...] + acc_ref[...], g_ref[...], b_ref[...])


def _moe_finish(x1, acc, tile0, n_rows, ln_g, ln_b):
    src = pl.BlockSpec((TOK_TILE, D), lambda i: (tile0 + i, 0))
    return pl.pallas_call(
        _moe_finish_kernel,
        grid=(n_rows // TOK_TILE,),
        in_specs=[src, src, _full((1, D), 1), _full((1, D), 1)],
        out_specs=pl.BlockSpec((TOK_TILE, D), lambda i: (i, 0)),
        out_shape=jax.ShapeDtypeStruct((n_rows, D), f32),
        compiler_params=_params(1),
        name="moe_finish",
    )(x1, acc, ln_g, ln_b)


def _router_weights(l, router_w, router_bias):
    rw = jnp.pad(router_w[l], ((0, 0), (0, LANE - N_EXPERTS)))
    rw_hi, rw_lo = _split_bf16(rw)
    rb = jnp.pad(router_bias[l], (0, LANE - N_EXPERTS)).reshape(1, LANE)
    return jnp.stack([rw_hi, rw_lo]), rb


S5_Q = 8
S5_NGB = D // LANE


def _s5_in_kernel(x_ref, acc_ref, lg_ref, lb_ref, mods_ref, w_ref, x2_ref, u_ref, u2_ref, slab_ref, *, row_of,
                  seq_len):
    r = row_of(pl.program_id(0))
    mrow = mods_ref[pl.ds(r, 1), :]
    sh, sc = mrow[:, 0:D], mrow[:, D:2 * D]
    x2 = _layer_norm(ALPHA * x_ref[...] + acc_ref[...], lg_ref[...], lb_ref[...])
    x2_ref[...] = x2
    h = (x2 * (1.0 + sc) + sh).astype(bf16)
    u = _dot(h, w_ref[...])
    u_ref[...] = u
    for s in range(S5_NGB):
        slab_ref[s] = u[:, LANE * s:LANE * (s + 1)]
    kt = seq_len // S5_Q
    for s in range(S5_NGB):
        for q in range(TOK_TILE // seq_len):
            for j in range(S5_Q):
                u2_ref[s, q * kt:(q + 1) * kt, LANE * j:LANE * (j + 1)] = (
                    slab_ref[s, pl.ds(q * seq_len + j, kt, stride=S5_Q), :].astype(bf16))


def _s5_in(x1, acc, ln_g, ln_b, tile0, mods1, w_in_c, n_b, seq_len, row_of):
    n_tiles = n_b * seq_len // TOK_TILE
    chunks = TOK_TILE // S5_Q
    src = pl.BlockSpec((TOK_TILE, D), lambda i: (tile0 + i, 0))
    dst = pl.BlockSpec((TOK_TILE, D), lambda i: (i, 0))
    return pl.pallas_call(
        functools.partial(_s5_in_kernel, row_of=row_of, seq_len=min(seq_len, TOK_TILE)),
        grid=(n_tiles,),
        in_specs=[src, src, _full((1, D), 1), _full((1, D), 1), _full((8, 6 * D), 1), _full((D, D), 1)],
        out_specs=[dst, dst, pl.BlockSpec((S5_NGB, chunks, D), lambda i: (0, i, 0))],
        out_shape=[jax.ShapeDtypeStruct((n_b * seq_len, D), f32), jax.ShapeDtypeStruct((n_b * seq_len, D), f32),
                   jax.ShapeDtypeStruct((S5_NGB, n_tiles * chunks, D), bf16)],
        scratch_shapes=[pltpu.VMEM((S5_NGB, TOK_TILE, LANE), f32)],
        compiler_params=_params(1),
        name="s5_in",
    )(x1, acc, ln_g, ln_b, mods1, w_in_c)


S5_GL = (LANE // S5_CH) * S5_P
S5_ROWS_C = (CTX_L // S5_Q) * N_CTX_B
S5_ROWS_D = (DEN_L // S5_Q) * N_DEN_B


def _s5_scan_kernel(lre_ref, lim_ref, ldt_ref, btr_ref, bti_ref, ctr_ref, cti_ref, uc_ref, ud_ref, h0_ref,
                    yc_ref, yd_ref, st_ref, win_ref, mso_ref, wit_ref, a_ref, s_ref, hp_ref):
    gl = S5_GL
    rowg = lax.shift_right_logical(lax.broadcasted_iota(jnp.int32, (LANE, gl), 0), 4)
    colg = lax.shift_right_logical(lax.broadcasted_iota(jnp.int32, (LANE, gl), 1), 6)
    same_group = rowg == colg
    reps = LANE // S5_CH

    def expand(t):
        return jnp.where(same_group, jnp.concatenate([t] * reps, axis=0), 0.0)

    def expand_c(t):
        return jnp.where(same_group, jnp.concatenate([t] * reps, axis=1), 0.0)

    for d in range(2):
        fwd = d == 0
        lre, lim = lre_ref[d], lim_ref[d]
        dt = jnp.exp(ldt_ref[d])
        a, w = lre * dt, lim * dt
        pre = [jnp.exp(m * a) * jnp.cos(m * w) for m in range(S5_Q + 1)]
        pim = [jnp.exp(m * a) * jnp.sin(m * w) for m in range(S5_Q + 1)]
        xr, xi = pre[1] - 1.0, pim[1]
        den = lre * lre + lim * lim
        cfr, cfi = (xr * lre + xi * lim) / den, (xi * lre - xr * lim) / den
        btr, bti = btr_ref[d], bti_ref[d]
        bexp_r = expand(cfr * btr - cfi * bti)
        bexp_i = expand(cfr * bti + cfi * btr)
        cexp_r, cexp_i = expand_c(ctr_ref[d]), expand_c(cti_ref[d])
        for m in range(S5_Q + 1):
            a_ref[m, :, 0:gl] = cexp_r * pre[m] - cexp_i * pim[m]
            a_ref[m, :, gl:2 * gl] = -(cexp_r * pim[m] + cexp_i * pre[m])
        for j in range(S5_Q):
            m = S5_Q - 1 - j if fwd else j
            win_ref[LANE * j:LANE * (j + 1), 0:gl] = (pre[m] * bexp_r - pim[m] * bexp_i).astype(bf16)
            win_ref[LANE * j:LANE * (j + 1), gl:2 * gl] = (pre[m] * bexp_i + pim[m] * bexp_r).astype(bf16)
        for j in range(S5_Q):
            m = j + 1 if fwd else S5_Q - j
            mso_ref[LANE * j:LANE * (j + 1), :] = a_ref[m].astype(bf16)
        b2 = jnp.concatenate([bexp_r, bexp_i], axis=1).astype(bf16)
        kt = [_dot_nt(b2, a_ref[tau].astype(bf16)) for tau in range(S5_Q)]
        for j in range(S5_Q):
            for jp in range(S5_Q):
                tau = jp - j if fwd else j - jp
                blk = slice(LANE * j, LANE * (j + 1)), slice(LANE * jp, LANE * (jp + 1))
                if fwd:
                    wit_ref[blk] = kt[tau] if tau >= 0 else jnp.zeros((LANE, LANE), f32)
                elif tau >= 0:
                    wit_ref[blk] = wit_ref[blk] + kt[tau]

        l8r, l8i = pre[S5_Q], pim[S5_Q]

        nsl = gl // LANE

        def slabs(ref, rs, first):
            return jnp.concatenate([ref[first + sl, rs, :] for sl in range(nsl)], axis=1)

        def put_slabs(ref, rs, first, val):
            for sl in range(nsl):
                ref[first + sl, rs, :] = val[:, LANE * sl:LANE * (sl + 1)]

        def advance(hr, hi_, sr, si):
            return l8r * hr - l8i * hi_ + sr, l8r * hi_ + l8i * hr + si

        def run(u_ref, y_ref, n_b, n_k, h_init):
            rows = n_b * n_k
            s = _dot(u_ref[0], win_ref[...])
            if n_b % SUB == 0:
                pitch = n_k + 1
                for bb in range(n_b):
                    dst = slice(bb * pitch, bb * pitch + n_k)
                    put_slabs(s_ref, dst, 0, s[bb * n_k:(bb + 1) * n_k, 0:gl])
                    put_slabs(s_ref, dst, nsl, s[bb * n_k:(bb + 1) * n_k, gl:2 * gl])

                def step(i, carry):
                    hr, hi_ = carry
                    rs = pl.ds(i if fwd else n_k - 1 - i, n_b, stride=pitch)
                    put_slabs(hp_ref, rs, 0, hr)
                    put_slabs(hp_ref, rs, nsl, hi_)
                    return advance(hr, hi_, slabs(s_ref, rs, 0), slabs(s_ref, rs, nsl))

                h_fin = lax.fori_loop(0, n_k, step, h_init)
                hp = jnp.concatenate(
                    [jnp.concatenate([hp_ref[sl, bb * pitch:bb * pitch + n_k, :] for sl in range(2 * nsl)], axis=1)
                     for bb in range(n_b)], axis=0).astype(bf16)
            else:
                put_slabs(s_ref, slice(0, rows), 0, s[:, 0:gl])
                put_slabs(s_ref, slice(0, rows), nsl, s[:, gl:2 * gl])
                n_it = n_k // SUB

                def step(i, carry):
                    it = i if fwd else n_it - 1 - i
                    out = []
                    for bb in range(n_b):
                        hr, hi_ = carry[bb]
                        rs = pl.ds(pl.multiple_of(bb * n_k + it * SUB, SUB), SUB)
                        s_re, s_im = slabs(s_ref, rs, 0), slabs(s_ref, rs, nsl)
                        prev_r, prev_i = [None] * SUB, [None] * SUB
                        for sub in (range(SUB) if fwd else reversed(range(SUB))):
                            prev_r[sub], prev_i[sub] = hr, hi_
                            hr, hi_ = advance(hr, hi_, s_re[sub:sub + 1], s_im[sub:sub + 1])
                        put_slabs(hp_ref, rs, 0, jnp.concatenate(prev_r, axis=0))
                        put_slabs(hp_ref, rs, nsl, jnp.concatenate(prev_i, axis=0))
                        out.append((hr, hi_))
                    return tuple(out)

                fin = lax.fori_loop(0, n_it, step, tuple((h_init[0][bb:bb + 1], h_init[1][bb:bb + 1])
                                                          for bb in range(n_b)))
                h_fin = (jnp.concatenate([f[0] for f in fin], axis=0), jnp.concatenate([f[1] for f in fin], axis=0))
                hp = jnp.concatenate([hp_ref[sl, 0:rows, :] for sl in range(2 * nsl)], axis=1).astype(bf16)
            y = _dot_nt(hp, mso_ref[...])
            if fwd:
                y_ref[0] = y
            else:
                y_ref[0] += y
            return h_fin

        zeros = jnp.zeros((N_CTX_B, gl), f32)
        hr, hi_ = run(uc_ref, yc_ref, N_CTX_B, CTX_L // S5_Q, (zeros, zeros))
        st_ref[d, 0] = hr
        st_ref[d, 1] = hi_
        run(ud_ref, yd_ref, N_DEN_B, DEN_L // S5_Q, (h0_ref[d, 0], h0_ref[d, 1]))

    wit = wit_ref[...].astype(bf16)
    yc_ref[0] += _dot(uc_ref[0], wit)
    yd_ref[0] += _dot(ud_ref[0], wit)


def _s5_scan(lam_re, lam_im, log_dt, bt_re, bt_im, ct_re, ct_im, u2c, u2d, h0):
    gl = S5_GL
    vec = pl.BlockSpec((2, 1, gl), lambda g: (0, 0, g))
    tab = pl.BlockSpec((2, S5_CH, gl), lambda g: (0, 0, g))
    ctab = pl.BlockSpec((2, LANE, S5_P), lambda g: (0, g, 0))
    rows = lambda n: pl.BlockSpec((1, n, D), lambda g: (g, 0, 0))
    return pl.pallas_call(
        _s5_scan_kernel,
        grid=(S5_NGB,),
        in_specs=[vec, vec, vec, tab, tab, ctab, ctab, rows(S5_ROWS_C), rows(S5_ROWS_D),
                  pl.BlockSpec((2, 2, N_DEN_B, gl), lambda g: (0, 0, 0, g))],
        out_specs=[rows(S5_ROWS_C), rows(S5_ROWS_D), pl.BlockSpec((2, 2, N_CTX_B, gl), lambda g: (0, 0, 0, g))],
        out_shape=[jax.ShapeDtypeStruct((S5_NGB, S5_ROWS_C, D), f32), jax.ShapeDtypeStruct((S5_NGB, S5_ROWS_D, D), f32),
                   jax.ShapeDtypeStruct((2, 2, N_CTX_B, S5_G * S5_P), f32)],
        scratch_shapes=[pltpu.VMEM((D, 2 * gl), bf16), pltpu.VMEM((D, 2 * gl), bf16), pltpu.VMEM((D, D), f32),
                        pltpu.VMEM((S5_Q + 1, LANE, 2 * gl), f32),
                        pltpu.VMEM((2 * gl // LANE, S5_ROWS_C + 2 * N_CTX_B, LANE), f32),
                        pltpu.VMEM((2 * gl // LANE, S5_ROWS_C + 2 * N_CTX_B, LANE), f32)],
        compiler_params=_params(1),
        name="s5_scan",
    )(lam_re, lam_im, log_dt, bt_re, bt_im, ct_re, ct_im, u2c, u2d, h0)


def _gelu_tanh(x):
    return 0.5 * x * (1.0 + jnp.tanh(np.sqrt(2.0 / np.pi).astype(np.float32) * (x + 0.044715 * (x * x * x))))


def _s5_out_kernel(xc_ref, xd_ref, uc_ref, ud_ref, yc_ref, yd_ref, mods_ref, dsk_ref, wout_ref, g_ref, b_ref, rw_ref,
                   rb_ref, x1_ref, h_ref, gates_ref, slab_ref):
    i = pl.program_id(0)
    is_ctx = i < T_CTX // TOK_TILE
    r = _mod_row(i, TOK_TILE)
    mrow = mods_ref[pl.ds(r, 1), :]
    u = _pick(i, TOK_TILE, uc_ref, ud_ref)

    kt = CTX_L // S5_Q
    for s in range(S5_NGB):
        for q in range(TOK_TILE // CTX_L):
            for j in range(S5_Q):
                blk = (s, slice(q * kt, (q + 1) * kt), slice(LANE * j, LANE * (j + 1)))
                slab_ref[s, pl.ds(q * CTX_L + j, kt, stride=S5_Q), :] = jnp.where(is_ctx, yc_ref[blk], yd_ref[blk])
    halves = [slice(a, a + SUBTILE) for a in range(0, TOK_TILE, SUBTILE)]
    zs = []
    for rows in halves:
        y = jnp.concatenate([slab_ref[s, rows, :] for s in range(S5_NGB)], axis=1) + dsk_ref[...] * u[rows]
        zs.append(_dot(_gelu_tanh(y).astype(bf16), wout_ref[...]))
    for rows, z in zip(halves, zs):
        out = z[:, 0:D] * jax.nn.sigmoid(z[:, D:2 * D])
        x = jnp.where(is_ctx, xc_ref[rows, :], xd_ref[rows, :])
        x1 = _layer_norm(ALPHA * x + mrow[:, 2 * D:3 * D] * out, g_ref[...], b_ref[...])
        x1_ref[rows, :] = x1
        h_ref[rows, :], gates_ref[rows, :] = _route(x1, mrow, rw_ref, rb_ref)


def _s5_out(xc, xd, uc, ud, yc, yd, mods1, d_skip, w_out_c, ln_g, ln_b, rw, rb):
    row_spec = lambda w: pl.BlockSpec((TOK_TILE, w), lambda i: (i, 0))
    uc_spec, ud_spec = _two_stream_specs(TOK_TILE, D)
    n_ctx = T_CTX // TOK_TILE
    chunks = TOK_TILE // S5_Q
    return pl.pallas_call(
        _s5_out_kernel,
        grid=(T_ALL // TOK_TILE,),
        in_specs=[uc_spec, ud_spec, uc_spec, ud_spec,
                  pl.BlockSpec((S5_NGB, chunks, D), lambda i: (0, jnp.minimum(i, n_ctx - 1), 0)),
                  pl.BlockSpec((S5_NGB, chunks, D), lambda i: (0, jnp.maximum(i - n_ctx, 0), 0)),
                  _full((8, 6 * D), 1), _full((1, D), 1), _full((D, 2 * D), 1), _full((1, D), 1), _full((1, D), 1),
                  _full((2, D, LANE), 1), _full((1, LANE), 1)],
        out_specs=[row_spec(D), row_spec(D), row_spec(LANE)],
        out_shape=[jax.ShapeDtypeStruct((T_ALL, D), f32), jax.ShapeDtypeStruct((T_ALL, D), bf16),
                   jax.ShapeDtypeStruct((T_ALL, LANE), f32)],
        scratch_shapes=[pltpu.VMEM((S5_NGB, TOK_TILE, LANE), f32)],
        compiler_params=_params(1),
        name="s5_out",
    )(xc, xd, uc, ud, yc, yd, mods1, d_skip, w_out_c, ln_g, ln_b, rw, rb)


def kernel(x_prompt, x_sample, c, cache_attn_k, cache_attn_v, cache_mla_ckv, cache_mla_krope, state_ssm, c_ctx,
           ada_w, ada_b, ln_mix_g, ln_mix_b, ln_ffn_g, ln_ffn_b, w_in_ab, attn_sink, mla_q_norm, mla_kv_norm,
           mla_w_uq, mla_w_ukv, w_out_ab, w_in_c, s5_lam_re, s5_lam_im, s5_log_dt, s5_b_re, s5_b_im, s5_c_re,
           s5_c_im, s5_d, w_out_c, router_w, router_bias, exp_w_gate, exp_w_up, exp_w_down, sh_w_gate, sh_w_up,
           sh_w_down):
    row = lambda v: v.reshape(1, -1)
    xc, xd = x_prompt.reshape(T_CTX, D), x_sample.reshape(T_DEN, D)
    cvec8 = jnp.concatenate([c_ctx[None, :], c, jnp.zeros((8 - 1 - N_DEN_B, D), f32)], axis=0)
    mods = _adaln(cvec8, ada_w, ada_b)

    w_in_p = jnp.pad(w_in_ab[0], ((0, 0), (0, PROJ_W - w_in_ab.shape[-1]))).astype(bf16)
    uq = mla_w_uq[0].reshape(MLA_Q_RANK, MLA_HEADS, MLA_NOPE + MLA_ROPE)
    w_uq_p = jnp.concatenate([uq[:, :, :MLA_NOPE].reshape(MLA_Q_RANK, -1), uq[:, :, MLA_NOPE:].reshape(MLA_Q_RANK, -1)],
                             axis=1).astype(bf16)
    ukv = mla_w_ukv[0].reshape(MLA_KV_RANK, MLA_HEADS, MLA_NOPE + MLA_V)
    w_ukv_p = jnp.concatenate([ukv[:, :, :MLA_NOPE].reshape(MLA_KV_RANK, -1),
                               ukv[:, :, MLA_NOPE:].reshape(MLA_KV_RANK, -1)], axis=1).astype(bf16)
    qa, ka, va, ckv, kr, qm, kvl = _ab_proj(xc, xd, mods[0], w_in_p, row(mla_q_norm[0]), row(mla_kv_norm[0]),
                                            w_uq_p, w_ukv_p)
    w_out_b = w_out_ab[0].astype(bf16)
    g0, b0 = row(ln_mix_g[0]), row(ln_mix_b[0])
    x1c, nk_t, nv_t = _ctx_attn(attn_sink[0], qa, ka, va, qm, kvl, kr, xc, mods[0], w_out_b, g0, b0)
    new_attn_k = jnp.transpose(nk_t, (0, 1, 4, 2, 3))
    new_attn_v = jnp.transpose(nv_t, (0, 1, 4, 2, 3))
    x1d = _den_attn(attn_sink[0], qa, ka, va,
                    jnp.transpose(cache_attn_k[:, 0], (0, 2, 3, 1)), jnp.transpose(cache_attn_v[:, 0], (0, 2, 3, 1)),
                    qm, kvl, kr, cache_mla_ckv[:, 0], cache_mla_krope[:, 0], w_ukv_p, xd, mods[0], w_out_b, g0, b0)
    rw0, rb0 = _router_weights(0, router_w, router_bias)
    x1, h, gates = _router(x1c, x1d, mods[0], rw0, rb0)
    acc = _moe(0, h, gates, mods[0], exp_w_gate, exp_w_up, exp_w_down, sh_w_gate, sh_w_up, sh_w_down)

    w_in_c_b = w_in_c[0].astype(bf16)
    lg0, lb0 = row(ln_ffn_g[0]), row(ln_ffn_b[0])
    n_ctx_tiles = T_CTX // TOK_TILE
    x2c, uc, u2c = _s5_in(x1, acc, lg0, lb0, 0, mods[1], w_in_c_b, N_CTX_B, CTX_L, lambda i: 0)
    x2d, ud, u2d = _s5_in(x1, acc, lg0, lb0, n_ctx_tiles, mods[1], w_in_c_b, N_DEN_B, DEN_L,
                          lambda i: 1 + i // (DEN_L // TOK_TILE))
    gp = S5_G * S5_P
    chan_major_b = lambda t: jnp.transpose(t[0], (0, 3, 1, 2)).reshape(2, S5_CH, gp)
    chan_major_c = lambda t: t[0].reshape(2, S5_G * S5_CH, S5_P)
    h0 = jnp.transpose(state_ssm[:, 0], (1, 2, 0, 3, 4)).reshape(2, 2, N_DEN_B, gp)
    yc, yd, st = _s5_scan(s5_lam_re[0].reshape(2, 1, gp), s5_lam_im[0].reshape(2, 1, gp),
                          jnp.repeat(s5_log_dt[0], S5_P, axis=-1).reshape(2, 1, gp),
                          chan_major_b(s5_b_re), chan_major_b(s5_b_im), chan_major_c(s5_c_re), chan_major_c(s5_c_im),
                          u2c, u2d, h0)
    rw1, rb1 = _router_weights(1, router_w, router_bias)
    x3, h, gates = _s5_out(x2c, x2d, uc, ud, yc, yd, mods[1], row(s5_d[0]),
                           w_out_c[0].astype(bf16), row(ln_mix_g[1]), row(ln_mix_b[1]), rw1, rb1)
    acc = _moe(1, h, gates, mods[1], exp_w_gate, exp_w_up, exp_w_down, sh_w_gate, sh_w_up, sh_w_down)
    lg1, lb1 = row(ln_ffn_g[1]), row(ln_ffn_b[1])
    y_prompt = _moe_finish(x3, acc, 0, T_CTX, lg1, lb1).reshape(N_CTX_B, CTX_L, D)
    y_sample = _moe_finish(x3, acc, n_ctx_tiles, T_DEN, lg1, lb1).reshape(N_DEN_B, DEN_L, D)
    new_mla_ckv = ckv[:T_CTX].reshape(N_CTX_B, 1, CTX_L, MLA_KV_RANK)
    new_mla_krope = kr[:T_CTX, :MLA_ROPE].reshape(N_CTX_B, 1, CTX_L, MLA_ROPE)
    new_state_ssm = jnp.transpose(st, (2, 0, 1, 3)).reshape(N_CTX_B, 1, 2, 2, S5_G, S5_P)
    return (y_prompt, y_sample, new_attn_k, new_attn_v, new_mla_ckv, new_mla_krope, new_state_ssm)
```

```python
import functools

import jax
import jax.numpy as jnp
import numpy as np
from jax import lax
from jax.experimental import pallas as pl
from jax.experimental.pallas import tpu as pltpu

f32 = jnp.float32
bf16 = jnp.bfloat16

D = 1024
N_CTX_B, CTX_L = 16, 256
N_DEN_B, DEN_L = 2, 1024
T_CTX = N_CTX_B * CTX_L
T_DEN = N_DEN_B * DEN_L
T_ALL = T_CTX + T_DEN
GRID_W = 64
WINDOW = 128
ROPE_BASE = 10000.0
A_HEADS, A_KV_HEADS, A_HD = 8, 2, 64
A_GROUP = A_HEADS // A_KV_HEADS
A_SCALE = A_HD ** -0.5
MLA_HEADS, MLA_Q_RANK, MLA_KV_RANK = 8, 256, 128
MLA_NOPE, MLA_ROPE, MLA_V = 64, 32, 64
MLA_SCALE = (MLA_NOPE + MLA_ROPE) ** -0.5
N_EXPERTS, TOP_K, EXPERT_FF, SHARED_FF = 64, 6, 128, 128
ROUTED_SCALE = 2.5
DEPTH = 2
ALPHA = (2.0 * DEPTH) ** 0.25
LN_EPS = 1e-5
RMS_EPS = 1e-6
NEG_INF = -1e30
S5_G, S5_CH, S5_P = 64, 16, 64

LANE = 128
SUB = 8
VMEM_LIMIT = 56 * 1024 * 1024

TOK_TILE = 512


def _mod_row(tile_idx, tile_rows):
    start = tile_idx * tile_rows
    return jnp.where(start < T_CTX, 0, 1 + (start - T_CTX) // DEN_L)


def _layer_norm(y, g, b):
    mu = jnp.mean(y, axis=-1, keepdims=True)
    yc = y - mu
    var = jnp.mean(yc * yc, axis=-1, keepdims=True)
    return yc * lax.rsqrt(var + LN_EPS) * g + b


def _silu(x):
    return x * jax.nn.sigmoid(x)


def _dot(a, b):
    return jnp.dot(a, b, preferred_element_type=f32)


def _dot_nt(a, b):
    return lax.dot_general(a, b, (((1,), (1,)), ((), ())), preferred_element_type=f32)


def _split_bf16(a):
    hi = a.astype(bf16)
    return hi, (a - hi.astype(f32)).astype(bf16)


def _full(shape, n_grid):
    zeros = tuple(0 for _ in shape)
    return pl.BlockSpec(shape, lambda *_: zeros)


def _two_stream_specs(tile_rows, width):
    n_ctx = T_CTX // tile_rows
    return (pl.BlockSpec((tile_rows, width), lambda i: (jnp.minimum(i, n_ctx - 1), 0)),
            pl.BlockSpec((tile_rows, width), lambda i: (jnp.maximum(i - n_ctx, 0), 0)))


def _pick(i, tile_rows, ctx_ref, den_ref):
    return jnp.where(i < T_CTX // tile_rows, ctx_ref[...], den_ref[...])


def _params(n_grid):
    return pltpu.CompilerParams(dimension_semantics=("arbitrary",) * n_grid, vmem_limit_bytes=VMEM_LIMIT)


ADA_TN = 1536


def _adaln_kernel(c_ref, w_ref, b_ref, o_ref):
    s_hi, s_lo = _split_bf16(_silu(c_ref[...]))
    w_hi, w_lo = _split_bf16(w_ref[0])
    o_ref[0] = _dot(s_hi, w_hi) + (_dot(s_hi, w_lo) + _dot(s_lo, w_hi)) + b_ref[0]


def _adaln(cvec8, ada_w, ada_b):
    n = 6 * D
    return pl.pallas_call(
        _adaln_kernel,
        grid=(DEPTH, n // ADA_TN),
        in_specs=[
            pl.BlockSpec((8, D), lambda l, j: (0, 0)),
            pl.BlockSpec((1, D, ADA_TN), lambda l, j: (l, 0, j)),
            pl.BlockSpec((1, 1, ADA_TN), lambda l, j: (l, 0, j)),
        ],
        out_specs=pl.BlockSpec((1, 8, ADA_TN), lambda l, j: (l, 0, j)),
        out_shape=jax.ShapeDtypeStruct((DEPTH, 8, n), f32),
        compiler_params=_params(2),
        name="adaln",
    )(cvec8, ada_w, ada_b.reshape(DEPTH, 1, n))


def _rope_table_array(head_dim):
    q = head_dim // 4
    pos = np.arange(DEN_L)
    row, col = (pos // GRID_W).astype(np.float64), (pos % GRID_W).astype(np.float64)
    lane = np.arange(LANE) % head_dim
    is_col = lane >= head_dim // 2
    w = lane % (head_dim // 2)
    first = w < q
    inv_freq = ROPE_BASE ** (-np.arange(q, dtype=np.float64) / q)
    ang = np.where(is_col[None, :], col[:, None], row[:, None]) * inv_freq[w % q][None, :]
    cos, sin = np.cos(ang), np.sin(ang)
    sin_a = np.where(first[None, :], -sin, 0.0)
    sin_b = np.where(first[None, :], 0.0, sin)
    ident = np.stack([np.ones((TOK_TILE, LANE)), np.zeros((TOK_TILE, LANE)), np.zeros((TOK_TILE, LANE))])
    tab = np.concatenate([ident, np.stack([cos, sin_a, sin_b])], axis=1).astype(np.float32)
    return jnp.asarray(tab), q


def _rope_chunk(x, tab_ref, q):
    return x * tab_ref[0] + pltpu.roll(x, LANE - q, 1) * tab_ref[1] + pltpu.roll(x, q, 1) * tab_ref[2]


PROJ_W = 1280
C_QA, C_KA, C_VA, C_CQ, C_CKV, C_KR = 0, 512, 640, 768, 1024, 1152
MLA_NN = MLA_HEADS * MLA_NOPE


def _ab_proj_kernel(xc_ref, xd_ref, mods_ref, w_ref, qn_ref, kvn_ref, wuq_ref, wukv_ref, ta_ref, tm_ref,
                    qa_ref, ka_ref, va_ref, ckv_ref, kr_ref, qm_ref, kvl_ref, *, qa_shift, qm_shift):
    i = pl.program_id(0)
    r = _mod_row(i, TOK_TILE)
    mrow = mods_ref[pl.ds(r, 1), :]
    sh, sc = mrow[:, 0:D], mrow[:, D:2 * D]
    x = _pick(i, TOK_TILE, xc_ref, xd_ref)
    h = (x * (1.0 + sc) + sh).astype(bf16)
    proj = _dot(h, w_ref[...])
    for j in range(4):
        c0 = C_QA + LANE * j
        qa_ref[:, LANE * j:LANE * (j + 1)] = _rope_chunk(proj[:, c0:c0 + LANE], ta_ref, qa_shift).astype(bf16)
    ka_ref[...] = _rope_chunk(proj[:, C_KA:C_KA + LANE], ta_ref, qa_shift)
    va_ref[...] = proj[:, C_VA:C_VA + LANE]
    cq = proj[:, C_CQ:C_CQ + MLA_Q_RANK]
    cq = cq * lax.rsqrt(jnp.mean(cq * cq, axis=-1, keepdims=True) + RMS_EPS) * qn_ref[...]
    ckv = proj[:, C_CKV:C_CKV + MLA_KV_RANK]
    ckv = ckv * lax.rsqrt(jnp.mean(ckv * ckv, axis=-1, keepdims=True) + RMS_EPS) * kvn_ref[...]
    ckv_ref[...] = ckv
    kr_ref[...] = _rope_chunk(proj[:, C_KR:C_KR + LANE], tm_ref, qm_shift)
    qm = _dot(cq.astype(bf16), wuq_ref[...])
    qm_ref[:, 0:MLA_NN] = qm[:, 0:MLA_NN].astype(bf16)
    for j in range(2):
        c0 = MLA_NN + LANE * j
        qm_ref[:, c0:c0 + LANE] = _rope_chunk(qm[:, c0:c0 + LANE], tm_ref, qm_shift).astype(bf16)
    kvl_ref[...] = _dot(ckv.astype(bf16), wukv_ref[...]).astype(bf16)


def _rope_block_index(i):
    tiles_ctx = T_CTX // TOK_TILE
    per_seq = DEN_L // TOK_TILE
    return jnp.where(i < tiles_ctx, 0, 1 + (i - tiles_ctx) % per_seq)


def _ab_proj(xc, xd, mods0, w_in_p, q_norm, kv_norm, w_uq_p, w_ukv_p):
    tab_a, qa_shift = _rope_table_array(A_HD)
    tab_m, qm_shift = _rope_table_array(MLA_ROPE)
    row_spec = lambda w: pl.BlockSpec((TOK_TILE, w), lambda i: (i, 0))
    xc_spec, xd_spec = _two_stream_specs(TOK_TILE, D)
    tab_spec = pl.BlockSpec((3, TOK_TILE, LANE), lambda i: (0, _rope_block_index(i), 0))
    outs = [(512, bf16), (LANE, f32), (LANE, f32), (LANE, f32), (LANE, f32), (768, bf16), (1024, bf16)]
    return pl.pallas_call(
        functools.partial(_ab_proj_kernel, qa_shift=qa_shift, qm_shift=qm_shift),
        grid=(T_ALL // TOK_TILE,),
        in_specs=[xc_spec, xd_spec, _full((8, 6 * D), 1), _full((D, PROJ_W), 1), _full((1, MLA_Q_RANK), 1),
                  _full((1, MLA_KV_RANK), 1), _full((MLA_Q_RANK, 768), 1), _full((MLA_KV_RANK, 1024), 1),
                  tab_spec, tab_spec],
        out_specs=[row_spec(w) for w, _ in outs],
        out_shape=[jax.ShapeDtypeStruct((T_ALL, w), dt) for w, dt in outs],
        compiler_params=_params(1),
        name="ab_proj",
    )(xc, xd, mods0, w_in_p, q_norm, kv_norm, w_uq_p, w_ukv_p, tab_a, tab_m)


def _softmax_blocks(s_refs, p_refs, sink_col=None):
    m = s_refs[0][...].max(axis=-1, keepdims=True)
    for s_ref in s_refs[1:]:
        m = jnp.maximum(m, s_ref[...].max(axis=-1, keepdims=True))
    if sink_col is not None:
        m = jnp.maximum(m, sink_col)
    for s_ref, p_ref in zip(s_refs, p_refs):
        p_ref[...] = jnp.exp(s_ref[...] - m).astype(bf16)
    return 0.0 if sink_col is None else jnp.exp(sink_col - m)


def _with_ones(v, axis=1):
    return jnp.concatenate([v, jnp.ones(v.shape, v.dtype)], axis=axis)


def _normalise(o_aug, extra, width):
    return o_aug[:, 0:width] * (1.0 / (o_aug[:, width:width + 1] + extra))


def _sink_column(sink_ref, rows_per_head):
    return jnp.concatenate([jnp.full((rows_per_head, 1), sink_ref[h], f32) for h in range(A_HEADS)], axis=0)


def _mla_q(qm_ref, h):
    rows = qm_ref.shape[0]
    return jnp.concatenate([qm_ref[:, MLA_NOPE * h:MLA_NOPE * (h + 1)],
                            qm_ref[:, MLA_NN + MLA_ROPE * h:MLA_NN + MLA_ROPE * (h + 1)],
                            jnp.zeros((rows, LANE - MLA_NOPE - MLA_ROPE), bf16)], axis=1)


def _mla_k(k_nope_h, k_rope):
    rows = k_nope_h.shape[0]
    return jnp.concatenate([k_nope_h, k_rope, jnp.zeros((rows, LANE - MLA_NOPE - MLA_ROPE), bf16)], axis=1)


def _mix_out_ln(merged_ref, wout_ref, x, mods_ref, r, g_ref, b_ref):
    out = _dot(merged_ref[...], wout_ref[...])
    gate = mods_ref[pl.ds(r, 1), 2 * D:3 * D]
    return _layer_norm(ALPHA * x + gate * out, g_ref[...], b_ref[...])


def _ctx_attn_kernel(sink_ref, qa_ref, ka_ref, va_ref, qm_ref, kvl_ref, kr_ref, x_ref, mods_ref, wout_ref,
                     g_ref, b_ref, o_ref, nk_ref, nv_ref, merged_ref, sa_ref, sm_ref, pa_ref, pm_ref):
    nk_ref[0, 0] = ka_ref[...].T.reshape(A_KV_HEADS, A_HD, CTX_L)
    nv_ref[0, 0] = va_ref[...].T.reshape(A_KV_HEADS, A_HD, CTX_L)
    n = CTX_L
    ka = ka_ref[...].astype(bf16)
    va = va_ref[...].astype(bf16)
    for j in range(A_KV_HEADS):
        q4 = jnp.concatenate([qa_ref[:, A_HD * h:A_HD * (h + 1)] for h in range(A_GROUP * j, A_GROUP * (j + 1))],
                             axis=0)
        sa_ref[A_GROUP * n * j:A_GROUP * n * (j + 1), :] = _dot_nt(q4, ka[:, A_HD * j:A_HD * (j + 1)]) * A_SCALE
    kr = kr_ref[:, 0:MLA_ROPE].astype(bf16)
    for h in range(MLA_HEADS):
        k_cat = _mla_k(kvl_ref[:, MLA_NOPE * h:MLA_NOPE * (h + 1)], kr)
        sm_ref[n * h:n * (h + 1), :] = _dot_nt(_mla_q(qm_ref, h), k_cat) * MLA_SCALE
    sink_a = _softmax_blocks([sa_ref], [pa_ref], _sink_column(sink_ref, n))
    _softmax_blocks([sm_ref], [pm_ref])
    for j in range(A_KV_HEADS):
        rows = slice(A_GROUP * n * j, A_GROUP * n * (j + 1))
        o4 = _normalise(_dot(pa_ref[rows, :], _with_ones(va[:, A_HD * j:A_HD * (j + 1)])), sink_a[rows], A_HD)
        for g in range(A_GROUP):
            h = A_GROUP * j + g
            merged_ref[:, A_HD * h:A_HD * (h + 1)] = o4[n * g:n * (g + 1)].astype(bf16)
    for h in range(MLA_HEADS):
        rows = slice(n * h, n * (h + 1))
        v = _with_ones(kvl_ref[:, MLA_NN + MLA_V * h:MLA_NN + MLA_V * (h + 1)])
        merged_ref[:, MLA_NN + MLA_V * h:MLA_NN + MLA_V * (h + 1)] = (
            _normalise(_dot(pm_ref[rows, :], v), 0.0, MLA_V).astype(bf16))
    o_ref[...] = _mix_out_ln(merged_ref, wout_ref, x_ref[...], mods_ref, 0, g_ref, b_ref)


def _ctx_attn(sink, qa, ka, va, qm, kvl, kr, x_all, mods0, w_out, ln_g, ln_b):
    blk = lambda w: pl.BlockSpec((CTX_L, w), lambda b: (b, 0))
    cache_blk = pl.BlockSpec((1, 1, A_KV_HEADS, A_HD, CTX_L), lambda b: (b, 0, 0, 0, 0))
    cache_shape = jax.ShapeDtypeStruct((N_CTX_B, 1, A_KV_HEADS, A_HD, CTX_L), f32)
    return pl.pallas_call(
        _ctx_attn_kernel,
        grid=(N_CTX_B,),
        in_specs=[pl.BlockSpec(memory_space=pltpu.SMEM), blk(512), blk(LANE), blk(LANE), blk(768), blk(1024),
                  blk(LANE), blk(D), _full((8, 6 * D), 1), _full((D, D), 1), _full((1, D), 1), _full((1, D), 1)],
        out_specs=[blk(D), cache_blk, cache_blk],
        out_shape=[jax.ShapeDtypeStruct((T_CTX, D), f32), cache_shape, cache_shape],
        scratch_shapes=[pltpu.VMEM((CTX_L, D), bf16),
                        pltpu.VMEM((A_HEADS * CTX_L, CTX_L), f32), pltpu.VMEM((MLA_HEADS * CTX_L, CTX_L), f32),
                        pltpu.VMEM((A_HEADS * CTX_L, CTX_L), bf16), pltpu.VMEM((MLA_HEADS * CTX_L, CTX_L), bf16)],
        compiler_params=_params(1),
        name="ctx_attn",
    )(sink, qa, ka, va, qm, kvl, kr, x_all, mods0, w_out, ln_g, ln_b)


QB = 256
WIN = QB + 2 * WINDOW
DEN_BLK0 = T_CTX // DEN_L
MLA_KEYS = CTX_L + DEN_L


def _den_attn_kernel(sink_ref, qa_ref, ka_ref, va_ref, cak_ref, cav_ref, qm_ref, kvl_ref, kr_ref, cckv_ref, ckr_ref,
                     wukv_ref, x_ref, mods_ref, wout_ref, g_ref, b_ref, o_ref, merged_ref, kcat_ref, vcat_ref,
                     saw_ref, sac_ref, sm_ref, paw_ref, pac_ref, pm_ref):
    b = pl.program_id(0)
    n = pl.program_id(1)

    @pl.when(n == 0)
    def _():
        kvc = _dot(cckv_ref[0].astype(bf16), wukv_ref[...]).astype(bf16)
        kr_ctx = ckr_ref[0].astype(bf16)
        kr_lat = kr_ref[:, 0:MLA_ROPE].astype(bf16)
        for h in range(MLA_HEADS):
            ns = slice(MLA_NOPE * h, MLA_NOPE * (h + 1))
            vs = slice(MLA_NN + MLA_V * h, MLA_NN + MLA_V * (h + 1))
            kcat_ref[h, 0:CTX_L, :] = _mla_k(kvc[:, ns], kr_ctx)
            kcat_ref[h, CTX_L:MLA_KEYS, :] = _mla_k(kvl_ref[:, ns], kr_lat)
            vcat_ref[h, 0:CTX_L, :] = _with_ones(kvc[:, vs])
            vcat_ref[h, CTX_L:MLA_KEYS, :] = _with_ones(kvl_ref[:, vs])

    start = pl.multiple_of(jnp.clip(QB * n - WINDOW, 0, DEN_L - WIN), WINDOW)
    grp_rows = A_GROUP * QB
    qpos = QB * n + (lax.broadcasted_iota(jnp.int32, (grp_rows, WIN), 0) & (QB - 1))
    kpos = start + lax.broadcasted_iota(jnp.int32, (grp_rows, WIN), 1)
    valid = jnp.abs(qpos - kpos) <= WINDOW
    kwin = ka_ref[pl.ds(start, WIN), :].astype(bf16)
    vwin = va_ref[pl.ds(start, WIN), :].astype(bf16)
    kctx_t = [cak_ref[0, j].astype(bf16) for j in range(A_KV_HEADS)]
    vctx_t = [cav_ref[0, j].astype(bf16) for j in range(A_KV_HEADS)]
    for j in range(A_KV_HEADS):
        sl = slice(A_HD * j, A_HD * (j + 1))
        rows = slice(grp_rows * j, grp_rows * (j + 1))
        q4 = jnp.concatenate([qa_ref[:, A_HD * h:A_HD * (h + 1)] for h in range(A_GROUP * j, A_GROUP * (j + 1))],
                             axis=0)
        saw_ref[rows, :] = jnp.where(valid, _dot_nt(q4, kwin[:, sl]) * A_SCALE, NEG_INF)
        sac_ref[rows, :] = _dot(q4, kctx_t[j]) * A_SCALE
    for h in range(MLA_HEADS):
        sm_ref[QB * h:QB * (h + 1), :] = _dot_nt(_mla_q(qm_ref, h), kcat_ref[h]) * MLA_SCALE
    sink_a = _softmax_blocks([saw_ref, sac_ref], [paw_ref, pac_ref], _sink_column(sink_ref, QB))
    _softmax_blocks([sm_ref], [pm_ref])
    for j in range(A_KV_HEADS):
        sl = slice(A_HD * j, A_HD * (j + 1))
        rows = slice(grp_rows * j, grp_rows * (j + 1))
        o_aug = (_dot(paw_ref[rows, :], _with_ones(vwin[:, sl]))
                 + _dot_nt(pac_ref[rows, :], _with_ones(vctx_t[j], axis=0)))
        o4 = _normalise(o_aug, sink_a[rows], A_HD)
        for g in range(A_GROUP):
            h = A_GROUP * j + g
            merged_ref[:, A_HD * h:A_HD * (h + 1)] = o4[QB * g:QB * (g + 1)].astype(bf16)
    for h in range(MLA_HEADS):
        rows = slice(QB * h, QB * (h + 1))
        o = _normalise(_dot(pm_ref[rows, :], vcat_ref[h]), 0.0, MLA_V)
        merged_ref[:, MLA_NN + MLA_V * h:MLA_NN + MLA_V * (h + 1)] = o.astype(bf16)
    o_ref[...] = _mix_out_ln(merged_ref, wout_ref, x_ref[...], mods_ref, 1 + b, g_ref, b_ref)


def _den_attn(sink, qa, ka, va, cache_k, cache_v, qm, kvl, kr, cache_ckv, cache_kr, w_ukv_p, x_all, mods0, w_out,
              ln_g, ln_b):
    nq = DEN_L // QB
    qblk = lambda w: pl.BlockSpec((QB, w), lambda b, n: (T_CTX // QB + b * nq + n, 0))
    seq = lambda w: pl.BlockSpec((DEN_L, w), lambda b, n: (DEN_BLK0 + b, 0))
    cache = lambda w: pl.BlockSpec((1, CTX_L, w), lambda b, n: (b, 0, 0))
    cache_a = pl.BlockSpec((1, A_KV_HEADS, A_HD, CTX_L), lambda b, n: (b, 0, 0, 0))
    return pl.pallas_call(
        _den_attn_kernel,
        grid=(N_DEN_B, nq),
        in_specs=[pl.BlockSpec(memory_space=pltpu.SMEM), qblk(512), seq(LANE), seq(LANE), cache_a, cache_a,
                  qblk(768), seq(1024), seq(LANE), cache(MLA_KV_RANK), cache(MLA_ROPE),
                  _full((MLA_KV_RANK, 1024), 2), pl.BlockSpec((QB, D), lambda b, n: (b * nq + n, 0)),
                  _full((8, 6 * D), 2), _full((D, D), 2), _full((1, D), 2),
                  _full((1, D), 2)],
        out_specs=pl.BlockSpec((QB, D), lambda b, n: (b * nq + n, 0)),
        out_shape=jax.ShapeDtypeStruct((T_DEN, D), f32),
        scratch_shapes=[pltpu.VMEM((QB, D), bf16), pltpu.VMEM((MLA_HEADS, MLA_KEYS, LANE), bf16),
                        pltpu.VMEM((MLA_HEADS, MLA_KEYS, 2 * MLA_V), bf16)]
        + [pltpu.VMEM((A_HEADS * QB, w), dt) for dt in (f32, bf16) for w in (WIN, CTX_L, MLA_KEYS)],
        compiler_params=_params(2),
        name="den_attn",
    )(sink, qa, ka, va, cache_k, cache_v, qm, kvl, kr, cache_ckv, cache_kr, w_ukv_p, x_all, mods0, w_out, ln_g, ln_b)


SUBTILE = 256


def _route(x1, mrow, rw_ref, rb_ref):
    sh, sc = mrow[:, 3 * D:4 * D], mrow[:, 4 * D:5 * D]
    h = x1 * (1.0 + sc) + sh
    h_hi = h.astype(bf16)
    h_lo = (h - h_hi.astype(f32)).astype(bf16)
    logits = _dot(h_hi, rw_ref[0]) + (_dot(h_hi, rw_ref[1]) + _dot(h_lo, rw_ref[0]))
    scores = jax.nn.sigmoid(logits)
    lane = lax.broadcasted_iota(jnp.int32, scores.shape, 1).astype(f32)
    sel = jnp.where(lane < N_EXPERTS, scores + rb_ref[...], -jnp.inf)
    gates = jnp.zeros_like(scores)
    for _ in range(TOP_K):
        m = sel.max(axis=-1, keepdims=True)
        idx = jnp.where(sel == m, lane, float(LANE)).min(axis=-1, keepdims=True)
        hit = lane == idx
        gates = jnp.where(hit, scores, gates)
        sel = jnp.where(hit, -jnp.inf, sel)
    return h_hi, gates / gates.sum(axis=-1, keepdims=True) * ROUTED_SCALE


def _router_kernel(xc_ref, xd_ref, mods_ref, rw_ref, rb_ref, x_ref, h_ref, gates_ref):
    i = pl.program_id(0)
    r = _mod_row(i, TOK_TILE)
    mrow = mods_ref[pl.ds(r, 1), :]
    x1 = _pick(i, TOK_TILE, xc_ref, xd_ref)
    x_ref[...] = x1
    h_ref[...], gates_ref[...] = _route(x1, mrow, rw_ref, rb_ref)


def _router(x1c, x1d, mods_l, router_w_p, router_b_p):
    row_spec = lambda w: pl.BlockSpec((TOK_TILE, w), lambda i: (i, 0))
    xc_spec, xd_spec = _two_stream_specs(TOK_TILE, D)
    return pl.pallas_call(
        _router_kernel,
        grid=(T_ALL // TOK_TILE,),
        in_specs=[xc_spec, xd_spec, _full((8, 6 * D), 1), _full((2, D, LANE), 1), _full((1, LANE), 1)],
        out_specs=[row_spec(D), row_spec(D), row_spec(LANE)],
        out_shape=[jax.ShapeDtypeStruct((T_ALL, D), f32), jax.ShapeDtypeStruct((T_ALL, D), bf16),
                   jax.ShapeDtypeStruct((T_ALL, LANE), f32)],
        compiler_params=_params(1),
        name="router",
    )(x1c, x1d, mods_l, router_w_p, router_b_p)


MOE_TOK = 1536
MOE_EG = 8
MOE_VMEM_LIMIT = 60 * 1024 * 1024
MOE_TILE = 512
MOE_FF = MOE_EG * EXPERT_FF


def _moe_kernel(h_ref, gates_ref, mods_ref, wg_ref, wu_ref, wd_ref, sg_ref, su_ref, sd_ref, o_ref):
    p = pl.program_id(0)
    e = pl.program_id(1)
    n_tiles = MOE_TOK // MOE_TILE

    def gate_f(t):
        r = _mod_row(p * n_tiles + t, MOE_TILE)
        return mods_ref[pl.ds(r, 1), 5 * D:6 * D]

    def rows_of(t):
        if isinstance(t, int):
            return pl.ds(t * MOE_TILE, MOE_TILE)
        return pl.ds(pl.multiple_of(t * MOE_TILE, MOE_TILE), MOE_TILE)

    @pl.when(e == 0)
    def _():
        sg = sg_ref[...].astype(bf16)
        su = su_ref[...].astype(bf16)
        sd = sd_ref[...].astype(bf16)

        def body(t, c):
            rows = rows_of(t)
            ht = h_ref[rows, :]
            hid = _silu(_dot(ht, sg)) * _dot(ht, su)
            o_ref[rows, :] = gate_f(t) * _dot(hid.astype(bf16), sd)
            return c

        lax.fori_loop(0, n_tiles, body, 0)

    wg = jnp.concatenate([wg_ref[k].astype(bf16) for k in range(MOE_EG)], axis=1)
    wu = jnp.concatenate([wu_ref[k].astype(bf16) for k in range(MOE_EG)], axis=1)
    wd = jnp.concatenate([wd_ref[k].astype(bf16) for k in range(MOE_EG)], axis=0)
    lane = lax.broadcasted_iota(jnp.int32, (MOE_TILE, LANE), 1)

    def body(t, c):
        rows = rows_of(t)
        ht = h_ref[rows, :]
        hid = _silu(_dot(ht, wg)) * _dot(ht, wu)
        gt = gates_ref[rows, :]
        parts = []
        for k in range(MOE_EG):
            col = jnp.where(lane == e * MOE_EG + k, gt, 0.0).sum(axis=-1, keepdims=True)
            parts.append((hid[:, EXPERT_FF * k:EXPERT_FF * (k + 1)] * col).astype(bf16))
        o_ref[rows, :] += gate_f(t) * _dot(jnp.concatenate(parts, axis=1), wd)
        return c

    for t in range(n_tiles):
        body(t, 0)


def _moe(l, h, gates, mods_l, wg, wu, wd, sg, su, sd):
    tok = lambda w: pl.BlockSpec((MOE_TOK, w), lambda p, e: (p, 0))
    return pl.pallas_call(
        _moe_kernel,
        grid=(T_ALL // MOE_TOK, N_EXPERTS // MOE_EG),
        in_specs=[tok(D), tok(LANE), _full((8, 6 * D), 2),
                  pl.BlockSpec((None, MOE_EG, D, EXPERT_FF), lambda p, e: (l, e, 0, 0)),
                  pl.BlockSpec((None, MOE_EG, D, EXPERT_FF), lambda p, e: (l, e, 0, 0)),
                  pl.BlockSpec((None, MOE_EG, EXPERT_FF, D), lambda p, e: (l, e, 0, 0)),
                  pl.BlockSpec((None, D, SHARED_FF), lambda p, e: (l, 0, 0)),
                  pl.BlockSpec((None, D, SHARED_FF), lambda p, e: (l, 0, 0)),
                  pl.BlockSpec((None, SHARED_FF, D), lambda p, e: (l, 0, 0))],
        out_specs=tok(D),
        out_shape=jax.ShapeDtypeStruct((T_ALL, D), f32),
        compiler_params=pltpu.CompilerParams(dimension_semantics=("arbitrary", "arbitrary"),
                                             vmem_limit_bytes=MOE_VMEM_LIMIT),
        name="moe",
    )(h, gates, mods_l, wg, wu, wd, sg, su, sd)


def _moe_finish_kernel(x_ref, acc_ref, g_ref, b_ref, o_ref):
    o_ref[...] = _layer_norm(ALPHA * x_ref[...] + acc_ref[...], g_ref[...], b_ref[...])


def _moe_finish(x1, acc, tile0, n_rows, ln_g, ln_b):
    src = pl.BlockSpec((TOK_TILE, D), lambda i: (tile0 + i, 0))
    return pl.pallas_call(
        _moe_finish_kernel,
        grid=(n_rows // TOK_TILE,),
        in_specs=[src, src, _full((1, D), 1), _full((1, D), 1)],
        out_specs=pl.BlockSpec((TOK_TILE, D), lambda i: (i, 0)),
        out_shape=jax.ShapeDtypeStruct((n_rows, D), f32),
        compiler_params=_params(1),
        name="moe_finish",
    )(x1, acc, ln_g, ln_b)


def _router_weights(l, router_w, router_bias):
    rw = jnp.pad(router_w[l], ((0, 0), (0, LANE - N_EXPERTS)))
    rw_hi, rw_lo = _split_bf16(rw)
    rb = jnp.pad(router_bias[l], (0, LANE - N_EXPERTS)).reshape(1, LANE)
    return jnp.stack([rw_hi, rw_lo]), rb


S5_Q = 8
S5_NGB = D // LANE


def _s5_in_kernel(x_ref, acc_ref, lg_ref, lb_ref, mods_ref, w_ref, x2_ref, u_ref, u2_ref, slab_ref, *, row_of,
                  seq_len):
    r = row_of(pl.program_id(0))
    mrow = mods_ref[pl.ds(r, 1), :]
    sh, sc = mrow[:, 0:D], mrow[:, D:2 * D]
    x2 = _layer_norm(ALPHA * x_ref[...] + acc_ref[...], lg_ref[...], lb_ref[...])
    x2_ref[...] = x2
    h = (x2 * (1.0 + sc) + sh).astype(bf16)
    u = _dot(h, w_ref[...])
    u_ref[...] = u
    for s in range(S5_NGB):
        slab_ref[s] = u[:, LANE * s:LANE * (s + 1)]
    kt = seq_len // S5_Q
    for s in range(S5_NGB):
        for q in range(TOK_TILE // seq_len):
            for j in range(S5_Q):
                u2_ref[s, q * kt:(q + 1) * kt, LANE * j:LANE * (j + 1)] = (
                    slab_ref[s, pl.ds(q * seq_len + j, kt, stride=S5_Q), :].astype(bf16))


def _s5_in(x1, acc, ln_g, ln_b, tile0, mods1, w_in_c, n_b, seq_len, row_of):
    n_tiles = n_b * seq_len // TOK_TILE
    chunks = TOK_TILE // S5_Q
    src = pl.BlockSpec((TOK_TILE, D), lambda i: (tile0 + i, 0))
    dst = pl.BlockSpec((TOK_TILE, D), lambda i: (i, 0))
    return pl.pallas_call(
        functools.partial(_s5_in_kernel, row_of=row_of, seq_len=min(seq_len, TOK_TILE)),
        grid=(n_tiles,),
        in_specs=[src, src, _full((1, D), 1), _full((1, D), 1), _full((8, 6 * D), 1), _full((D, D), 1)],
        out_specs=[dst, dst, pl.BlockSpec((S5_NGB, chunks, D), lambda i: (0, i, 0))],
        out_shape=[jax.ShapeDtypeStruct((n_b * seq_len, D), f32), jax.ShapeDtypeStruct((n_b * seq_len, D), f32),
                   jax.ShapeDtypeStruct((S5_NGB, n_tiles * chunks, D), bf16)],
        scratch_shapes=[pltpu.VMEM((S5_NGB, TOK_TILE, LANE), f32)],
        compiler_params=_params(1),
        name="s5_in",
    )(x1, acc, ln_g, ln_b, mods1, w_in_c)


S5_GL = (LANE // S5_CH) * S5_P
S5_ROWS_C = (CTX_L // S5_Q) * N_CTX_B
S5_ROWS_D = (DEN_L // S5_Q) * N_DEN_B


def _s5_scan_kernel(lre_ref, lim_ref, ldt_ref, btr_ref, bti_ref, ctr_ref, cti_ref, uc_ref, ud_ref, h0_ref,
                    yc_ref, yd_ref, st_ref, win_ref, mso_ref, wit_ref, a_ref, s_ref, hp_ref):
    gl = S5_GL
    rowg = lax.shift_right_logical(lax.broadcasted_iota(jnp.int32, (LANE, gl), 0), 4)
    colg = lax.shift_right_logical(lax.broadcasted_iota(jnp.int32, (LANE, gl), 1), 6)
    same_group = rowg == colg
    reps = LANE // S5_CH

    def expand(t):
        return jnp.where(same_group, jnp.concatenate([t] * reps, axis=0), 0.0)

    def expand_c(t):
        return jnp.where(same_group, jnp.concatenate([t] * reps, axis=1), 0.0)

    for d in range(2):
        fwd = d == 0
        lre, lim = lre_ref[d], lim_ref[d]
        dt = jnp.exp(ldt_ref[d])
        a, w = lre * dt, lim * dt
        pre = [jnp.exp(m * a) * jnp.cos(m * w) for m in range(S5_Q + 1)]
        pim = [jnp.exp(m * a) * jnp.sin(m * w) for m in range(S5_Q + 1)]
        xr, xi = pre[1] - 1.0, pim[1]
        den = lre * lre + lim * lim
        cfr, cfi = (xr * lre + xi * lim) / den, (xi * lre - xr * lim) / den
        btr, bti = btr_ref[d], bti_ref[d]
        bexp_r = expand(cfr * btr - cfi * bti)
        bexp_i = expand(cfr * bti + cfi * btr)
        cexp_r, cexp_i = expand_c(ctr_ref[d]), expand_c(cti_ref[d])
        for m in range(S5_Q + 1):
            a_ref[m, :, 0:gl] = cexp_r * pre[m] - cexp_i * pim[m]
            a_ref[m, :, gl:2 * gl] = -(cexp_r * pim[m] + cexp_i * pre[m])
        for j in range(S5_Q):
            m = S5_Q - 1 - j if fwd else j
            win_ref[LANE * j:LANE * (j + 1), 0:gl] = (pre[m] * bexp_r - pim[m] * bexp_i).astype(bf16)
            win_ref[LANE * j:LANE * (j + 1), gl:2 * gl] = (pre[m] * bexp_i + pim[m] * bexp_r).astype(bf16)
        for j in range(S5_Q):
            m = j + 1 if fwd else S5_Q - j
            mso_ref[LANE * j:LANE * (j + 1), :] = a_ref[m].astype(bf16)
        b2 = jnp.concatenate([bexp_r, bexp_i], axis=1).astype(bf16)
        kt = [_dot_nt(b2, a_ref[tau].astype(bf16)) for tau in range(S5_Q)]
        for j in range(S5_Q):
            for jp in range(S5_Q):
                tau = jp - j if fwd else j - jp
                blk = slice(LANE * j, LANE * (j + 1)), slice(LANE * jp, LANE * (jp + 1))
                if fwd:
                    wit_ref[blk] = kt[tau] if tau >= 0 else jnp.zeros((LANE, LANE), f32)
                elif tau >= 0:
                    wit_ref[blk] = wit_ref[blk] + kt[tau]

        l8r, l8i = pre[S5_Q], pim[S5_Q]

        nsl = gl // LANE

        def slabs(ref, rs, first):
            return jnp.concatenate([ref[first + sl, rs, :] for sl in range(nsl)], axis=1)

        def put_slabs(ref, rs, first, val):
            for sl in range(nsl):
                ref[first + sl, rs, :] = val[:, LANE * sl:LANE * (sl + 1)]

        def advance(hr, hi_, sr, si):
            return l8r * hr - l8i * hi_ + sr, l8r * hi_ + l8i * hr + si

        def run(u_ref, y_ref, n_b, n_k, h_init):
            rows = n_b * n_k
            s = _dot(u_ref[0], win_ref[...])
            if n_b % SUB == 0:
                pitch = n_k + 1
                for bb in range(n_b):
                    dst = slice(bb * pitch, bb * pitch + n_k)
                    put_slabs(s_ref, dst, 0, s[bb * n_k:(bb + 1) * n_k, 0:gl])
                    put_slabs(s_ref, dst, nsl, s[bb * n_k:(bb + 1) * n_k, gl:2 * gl])

                def step(i, carry):
                    hr, hi_ = carry
                    rs = pl.ds(i if fwd else n_k - 1 - i, n_b, stride=pitch)
                    put_slabs(hp_ref, rs, 0, hr)
                    put_slabs(hp_ref, rs, nsl, hi_)
                    return advance(hr, hi_, slabs(s_ref, rs, 0), slabs(s_ref, rs, nsl))

                h_fin = lax.fori_loop(0, n_k, step, h_init)
                hp = jnp.concatenate(
                    [jnp.concatenate([hp_ref[sl, bb * pitch:bb * pitch + n_k, :] for sl in range(2 * nsl)], axis=1)
                     for bb in range(n_b)], axis=0).astype(bf16)
            else:
                put_slabs(s_ref, slice(0, rows), 0, s[:, 0:gl])
                put_slabs(s_ref, slice(0, rows), nsl, s[:, gl:2 * gl])
                n_it = n_k // SUB

                def step(i, carry):
                    it = i if fwd else n_it - 1 - i
                    out = []
                    for bb in range(n_b):
                        hr, hi_ = carry[bb]
                        rs = pl.ds(pl.multiple_of(bb * n_k + it * SUB, SUB), SUB)
                        s_re, s_im = slabs(s_ref, rs, 0), slabs(s_ref, rs, nsl)
                        prev_r, prev_i = [None] * SUB, [None] * SUB
                        for sub in (range(SUB) if fwd else reversed(range(SUB))):
                            prev_r[sub], prev_i[sub] = hr, hi_
                            hr, hi_ = advance(hr, hi_, s_re[sub:sub + 1], s_im[sub:sub + 1])
                        put_slabs(hp_ref, rs, 0, jnp.concatenate(prev_r, axis=0))
                        put_slabs(hp_ref, rs, nsl, jnp.concatenate(prev_i, axis=0))
                        out.append((hr, hi_))
                    return tuple(out)

                fin = lax.fori_loop(0, n_it, step, tuple((h_init[0][bb:bb + 1], h_init[1][bb:bb + 1])
                                                          for bb in range(n_b)))
                h_fin = (jnp.concatenate([f[0] for f in fin], axis=0), jnp.concatenate([f[1] for f in fin], axis=0))
                hp = jnp.concatenate([hp_ref[sl, 0:rows, :] for sl in range(2 * nsl)], axis=1).astype(bf16)
            y = _dot_nt(hp, mso_ref[...])
            if fwd:
                y_ref[0] = y
            else:
                y_ref[0] += y
            return h_fin

        zeros = jnp.zeros((N_CTX_B, gl), f32)
        hr, hi_ = run(uc_ref, yc_ref, N_CTX_B, CTX_L // S5_Q, (zeros, zeros))
        st_ref[d, 0] = hr
        st_ref[d, 1] = hi_
        run(ud_ref, yd_ref, N_DEN_B, DEN_L // S5_Q, (h0_ref[d, 0], h0_ref[d, 1]))

    wit = wit_ref[...].astype(bf16)
    yc_ref[0] += _dot(uc_ref[0], wit)
    yd_ref[0] += _dot(ud_ref[0], wit)


def _s5_scan(lam_re, lam_im, log_dt, bt_re, bt_im, ct_re, ct_im, u2c, u2d, h0):
    gl = S5_GL
    vec = pl.BlockSpec((2, 1, gl), lambda g: (0, 0, g))
    tab = pl.BlockSpec((2, S5_CH, gl), lambda g: (0, 0, g))
    ctab = pl.BlockSpec((2, LANE, S5_P), lambda g: (0, g, 0))
    rows = lambda n: pl.BlockSpec((1, n, D), lambda g: (g, 0, 0))
    return pl.pallas_call(
        _s5_scan_kernel,
        grid=(S5_NGB,),
        in_specs=[vec, vec, vec, tab, tab, ctab, ctab, rows(S5_ROWS_C), rows(S5_ROWS_D),
                  pl.BlockSpec((2, 2, N_DEN_B, gl), lambda g: (0, 0, 0, g))],
        out_specs=[rows(S5_ROWS_C), rows(S5_ROWS_D), pl.BlockSpec((2, 2, N_CTX_B, gl), lambda g: (0, 0, 0, g))],
        out_shape=[jax.ShapeDtypeStruct((S5_NGB, S5_ROWS_C, D), f32), jax.ShapeDtypeStruct((S5_NGB, S5_ROWS_D, D), f32),
                   jax.ShapeDtypeStruct((2, 2, N_CTX_B, S5_G * S5_P), f32)],
        scratch_shapes=[pltpu.VMEM((D, 2 * gl), bf16), pltpu.VMEM((D, 2 * gl), bf16), pltpu.VMEM((D, D), f32),
                        pltpu.VMEM((S5_Q + 1, LANE, 2 * gl), f32),
                        pltpu.VMEM((2 * gl // LANE, S5_ROWS_C + 2 * N_CTX_B, LANE), f32),
                        pltpu.VMEM((2 * gl // LANE, S5_ROWS_C + 2 * N_CTX_B, LANE), f32)],
        compiler_params=_params(1),
        name="s5_scan",
    )(lam_re, lam_im, log_dt, bt_re, bt_im, ct_re, ct_im, u2c, u2d, h0)


def _gelu_tanh(x):
    return 0.5 * x * (1.0 + jnp.tanh(np.sqrt(2.0 / np.pi).astype(np.float32) * (x + 0.044715 * (x * x * x))))


def _s5_out_kernel(xc_ref, xd_ref, uc_ref, ud_ref, yc_ref, yd_ref, mods_ref, dsk_ref, wout_ref, g_ref, b_ref, rw_ref,
                   rb_ref, x1_ref, h_ref, gates_ref, slab_ref):
    i = pl.program_id(0)
    is_ctx = i < T_CTX // TOK_TILE
    r = _mod_row(i, TOK_TILE)
    mrow = mods_ref[pl.ds(r, 1), :]
    u = _pick(i, TOK_TILE, uc_ref, ud_ref)

    kt = CTX_L // S5_Q
    for s in range(S5_NGB):
        for q in range(TOK_TILE // CTX_L):
            for j in range(S5_Q):
                blk = (s, slice(q * kt, (q + 1) * kt), slice(LANE * j, LANE * (j + 1)))
                slab_ref[s, pl.ds(q * CTX_L + j, kt, stride=S5_Q), :] = jnp.where(is_ctx, yc_ref[blk], yd_ref[blk])
    halves = [slice(a, a + SUBTILE) for a in range(0, TOK_TILE, SUBTILE)]
    zs = []
    for rows in halves:
        y = jnp.concatenate([slab_ref[s, rows, :] for s in range(S5_NGB)], axis=1) + dsk_ref[...] * u[rows]
        zs.append(_dot(_gelu_tanh(y).astype(bf16), wout_ref[...]))
    for rows, z in zip(halves, zs):
        out = z[:, 0:D] * jax.nn.sigmoid(z[:, D:2 * D])
        x = jnp.where(is_ctx, xc_ref[rows, :], xd_ref[rows, :])
        x1 = _layer_norm(ALPHA * x + mrow[:, 2 * D:3 * D] * out, g_ref[...], b_ref[...])
        x1_ref[rows, :] = x1
        h_ref[rows, :], gates_ref[rows, :] = _route(x1, mrow, rw_ref, rb_ref)


def _s5_out(xc, xd, uc, ud, yc, yd, mods1, d_skip, w_out_c, ln_g, ln_b, rw, rb):
    row_spec = lambda w: pl.BlockSpec((TOK_TILE, w), lambda i: (i, 0))
    uc_spec, ud_spec = _two_stream_specs(TOK_TILE, D)
    n_ctx = T_CTX // TOK_TILE
    chunks = TOK_TILE // S5_Q
    return pl.pallas_call(
        _s5_out_kernel,
        grid=(T_ALL // TOK_TILE,),
        in_specs=[uc_spec, ud_spec, uc_spec, ud_spec,
                  pl.BlockSpec((S5_NGB, chunks, D), lambda i: (0, jnp.minimum(i, n_ctx - 1), 0)),
                  pl.BlockSpec((S5_NGB, chunks, D), lambda i: (0, jnp.maximum(i - n_ctx, 0), 0)),
                  _full((8, 6 * D), 1), _full((1, D), 1), _full((D, 2 * D), 1), _full((1, D), 1), _full((1, D), 1),
                  _full((2, D, LANE), 1), _full((1, LANE), 1)],
        out_specs=[row_spec(D), row_spec(D), row_spec(LANE)],
        out_shape=[jax.ShapeDtypeStruct((T_ALL, D), f32), jax.ShapeDtypeStruct((T_ALL, D), bf16),
                   jax.ShapeDtypeStruct((T_ALL, LANE), f32)],
        scratch_shapes=[pltpu.VMEM((S5_NGB, TOK_TILE, LANE), f32)],
        compiler_params=_params(1),
        name="s5_out",
    )(xc, xd, uc, ud, yc, yd, mods1, d_skip, w_out_c, ln_g, ln_b, rw, rb)


def kernel(x_prompt, x_sample, c, cache_attn_k, cache_attn_v, cache_mla_ckv, cache_mla_krope, state_ssm, c_ctx,
           ada_w, ada_b, ln_mix_g, ln_mix_b, ln_ffn_g, ln_ffn_b, w_in_ab, attn_sink, mla_q_norm, mla_kv_norm,
           mla_w_uq, mla_w_ukv, w_out_ab, w_in_c, s5_lam_re, s5_lam_im, s5_log_dt, s5_b_re, s5_b_im, s5_c_re,
           s5_c_im, s5_d, w_out_c, router_w, router_bias, exp_w_gate, exp_w_up, exp_w_down, sh_w_gate, sh_w_up,
           sh_w_down):
    row = lambda v: v.reshape(1, -1)
    xc, xd = x_prompt.reshape(T_CTX, D), x_sample.reshape(T_DEN, D)
    cvec8 = jnp.concatenate([c_ctx[None, :], c, jnp.zeros((8 - 1 - N_DEN_B, D), f32)], axis=0)
    mods = _adaln(cvec8, ada_w, ada_b)

    w_in_p = jnp.pad(w_in_ab[0], ((0, 0), (0, PROJ_W - w_in_ab.shape[-1]))).astype(bf16)
    uq = mla_w_uq[0].reshape(MLA_Q_RANK, MLA_HEADS, MLA_NOPE + MLA_ROPE)
    w_uq_p = jnp.concatenate([uq[:, :, :MLA_NOPE].reshape(MLA_Q_RANK, -1), uq[:, :, MLA_NOPE:].reshape(MLA_Q_RANK, -1)],
                             axis=1).astype(bf16)
    ukv = mla_w_ukv[0].reshape(MLA_KV_RANK, MLA_HEADS, MLA_NOPE + MLA_V)
    w_ukv_p = jnp.concatenate([ukv[:, :, :MLA_NOPE].reshape(MLA_KV_RANK, -1),
                               ukv[:, :, MLA_NOPE:].reshape(MLA_KV_RANK, -1)], axis=1).astype(bf16)
    qa, ka, va, ckv, kr, qm, kvl = _ab_proj(xc, xd, mods[0], w_in_p, row(mla_q_norm[0]), row(mla_kv_norm[0]),
                                            w_uq_p, w_ukv_p)
    w_out_b = w_out_ab[0].astype(bf16)
    g0, b0 = row(ln_mix_g[0]), row(ln_mix_b[0])
    x1c, nk_t, nv_t = _ctx_attn(attn_sink[0], qa, ka, va, qm, kvl, kr, xc, mods[0], w_out_b, g0, b0)
    new_attn_k = jnp.transpose(nk_t, (0, 1, 4, 2, 3))
    new_attn_v = jnp.transpose(nv_t, (0, 1, 4, 2, 3))
    x1d = _den_attn(attn_sink[0], qa, ka, va,
                    jnp.transpose(cache_attn_k[:, 0], (0, 2, 3, 1)), jnp.transpose(cache_attn_v[:, 0], (0, 2, 3, 1)),
                    qm, kvl, kr, cache_mla_ckv[:, 0], cache_mla_krope[:, 0], w_ukv_p, xd, mods[0], w_out_b, g0, b0)
    rw0, rb0 = _router_weights(0, router_w, router_bias)
    x1, h, gates = _router(x1c, x1d, mods[0], rw0, rb0)
    acc = _moe(0, h, gates, mods[0], exp_w_gate, exp_w_up, exp_w_down, sh_w_gate, sh_w_up, sh_w_down)

    w_in_c_b = w_in_c[0].astype(bf16)
    lg0, lb0 = row(ln_ffn_g[0]), row(ln_ffn_b[0])
    n_ctx_tiles = T_CTX // TOK_TILE
    x2c, uc, u2c = _s5_in(x1, acc, lg0, lb0, 0, mods[1], w_in_c_b, N_CTX_B, CTX_L, lambda i: 0)
    x2d, ud, u2d = _s5_in(x1, acc, lg0, lb0, n_ctx_tiles, mods[1], w_in_c_b, N_DEN_B, DEN_L,
                          lambda i: 1 + i // (DEN_L // TOK_TILE))
    gp = S5_G * S5_P
    chan_major_b = lambda t: jnp.transpose(t[0], (0, 3, 1, 2)).reshape(2, S5_CH, gp)
    chan_major_c = lambda t: t[0].reshape(2, S5_G * S5_CH, S5_P)
    h0 = jnp.transpose(state_ssm[:, 0], (1, 2, 0, 3, 4)).reshape(2, 2, N_DEN_B, gp)
    yc, yd, st = _s5_scan(s5_lam_re[0].reshape(2, 1, gp), s5_lam_im[0].reshape(2, 1, gp),
                          jnp.repeat(s5_log_dt[0], S5_P, axis=-1).reshape(2, 1, gp),
                          chan_major_b(s5_b_re), chan_major_b(s5_b_im), chan_major_c(s5_c_re), chan_major_c(s5_c_im),
                          u2c, u2d, h0)
    rw1, rb1 = _router_weights(1, router_w, router_bias)
    x3, h, gates = _s5_out(x2c, x2d, uc, ud, yc, yd, mods[1], row(s5_d[0]),
                           w_out_c[0].astype(bf16), row(ln_mix_g[1]), row(ln_mix_b[1]), rw1, rb1)
    acc = _moe(1, h, gates, mods[1], exp_w_gate, exp_w_up, exp_w_down, sh_w_gate, sh_w_up, sh_w_down)
    lg1, lb1 = row(ln_ffn_g[1]), row(ln_ffn_b[1])
    y_prompt = _moe_finish(x3, acc, 0, T_CTX, lg1, lb1).reshape(N_CTX_B, CTX_L, D)
    y_sample = _moe_finish(x3, acc, n_ctx_tiles, T_DEN, lg1, lb1).reshape(N_DEN_B, DEN_L, D)
    new_mla_ckv = ckv[:T_CTX].reshape(N_CTX_B, 1, CTX_L, MLA_KV_RANK)
    new_mla_krope = kr[:T_CTX, :MLA_ROPE].reshape(N_CTX_B, 1, CTX_L, MLA_ROPE)
    new_state_ssm = jnp.transpose(st, (2, 0, 1, 3)).reshape(N_CTX_B, 1, 2, 2, S5_G, S5_P)
    return (y_prompt, y_sample, new_attn_k, new_attn_v, new_mla_ckv, new_mla_krope, new_state_ssm)
```

```python
import functools

import jax
import jax.numpy as jnp
import numpy as np
from jax import lax
from jax.experimental import pallas as pl
from jax.experimental.pallas import tpu as pltpu

f32 = jnp.float32
bf16 = jnp.bfloat16

D = 1024
N_CTX_B, CTX_L = 16, 256
N_DEN_B, DEN_L = 2, 1024
T_CTX = N_CTX_B * CTX_L
T_DEN = N_DEN_B * DEN_L
T_ALL = T_CTX + T_DEN
GRID_W = 64
WINDOW = 128
ROPE_BASE = 10000.0
A_HEADS, A_KV_HEADS, A_HD = 8, 2, 64
A_GROUP = A_HEADS // A_KV_HEADS
A_SCALE = A_HD ** -0.5
MLA_HEADS, MLA_Q_RANK, MLA_KV_RANK = 8, 256, 128
MLA_NOPE, MLA_ROPE, MLA_V = 64, 32, 64
MLA_SCALE = (MLA_NOPE + MLA_ROPE) ** -0.5
N_EXPERTS, TOP_K, EXPERT_FF, SHARED_FF = 64, 6, 128, 128
ROUTED_SCALE = 2.5
DEPTH = 2
ALPHA = (2.0 * DEPTH) ** 0.25
LN_EPS = 1e-5
RMS_EPS = 1e-6
NEG_INF = -1e30
S5_G, S5_CH, S5_P = 64, 16, 64

LANE = 128
SUB = 8
VMEM_LIMIT = 56 * 1024 * 1024

TOK_TILE = 512


def _mod_row(tile_idx, tile_rows):
    start = tile_idx * tile_rows
    return jnp.where(start < T_CTX, 0, 1 + (start - T_CTX) // DEN_L)


def _layer_norm(y, g, b):
    mu = jnp.mean(y, axis=-1, keepdims=True)
    yc = y - mu
    var = jnp.mean(yc * yc, axis=-1, keepdims=True)
    return yc * lax.rsqrt(var + LN_EPS) * g + b


def _silu(x):
    return x * jax.nn.sigmoid(x)


def _dot(a, b):
    return jnp.dot(a, b, preferred_element_type=f32)


def _dot_nt(a, b):
    return lax.dot_general(a, b, (((1,), (1,)), ((), ())), preferred_element_type=f32)


def _split_bf16(a):
    hi = a.astype(bf16)
    return hi, (a - hi.astype(f32)).astype(bf16)


def _full(shape, n_grid):
    zeros = tuple(0 for _ in shape)
    return pl.BlockSpec(shape, lambda *_: zeros)


def _two_stream_specs(tile_rows, width):
    n_ctx = T_CTX // tile_rows
    return (pl.BlockSpec((tile_rows, width), lambda i: (jnp.minimum(i, n_ctx - 1), 0)),
            pl.BlockSpec((tile_rows, width), lambda i: (jnp.maximum(i - n_ctx, 0), 0)))


def _pick(i, tile_rows, ctx_ref, den_ref):
    return jnp.where(i < T_CTX // tile_rows, ctx_ref[...], den_ref[...])


def _params(n_grid):
    return pltpu.CompilerParams(dimension_semantics=("arbitrary",) * n_grid, vmem_limit_bytes=VMEM_LIMIT)


ADA_TN = 1536


def _adaln_kernel(c_ref, w_ref, b_ref, o_ref):
    s_hi, s_lo = _split_bf16(_silu(c_ref[...]))
    w_hi, w_lo = _split_bf16(w_ref[0])
    o_ref[0] = _dot(s_hi, w_hi) + (_dot(s_hi, w_lo) + _dot(s_lo, w_hi)) + b_ref[0]


def _adaln(cvec8, ada_w, ada_b):
    n = 6 * D
    return pl.pallas_call(
        _adaln_kernel,
        grid=(DEPTH, n // ADA_TN),
        in_specs=[
            pl.BlockSpec((8, D), lambda l, j: (0, 0)),
            pl.BlockSpec((1, D, ADA_TN), lambda l, j: (l, 0, j)),
            pl.BlockSpec((1, 1, ADA_TN), lambda l, j: (l, 0, j)),
        ],
        out_specs=pl.BlockSpec((1, 8, ADA_TN), lambda l, j: (l, 0, j)),
        out_shape=jax.ShapeDtypeStruct((DEPTH, 8, n), f32),
        compiler_params=_params(2),
        name="adaln",
    )(cvec8, ada_w, ada_b.reshape(DEPTH, 1, n))


def _rope_table_array(head_dim):
    q = head_dim // 4
    pos = np.arange(DEN_L)
    row, col = (pos // GRID_W).astype(np.float64), (pos % GRID_W).astype(np.float64)
    lane = np.arange(LANE) % head_dim
    is_col = lane >= head_dim // 2
    w = lane % (head_dim // 2)
    first = w < q
    inv_freq = ROPE_BASE ** (-np.arange(q, dtype=np.float64) / q)
    ang = np.where(is_col[None, :], col[:, None], row[:, None]) * inv_freq[w % q][None, :]
    cos, sin = np.cos(ang), np.sin(ang)
    sin_a = np.where(first[None, :], -sin, 0.0)
    sin_b = np.where(first[None, :], 0.0, sin)
    ident = np.stack([np.ones((TOK_TILE, LANE)), np.zeros((TOK_TILE, LANE)), np.zeros((TOK_TILE, LANE))])
    tab = np.concatenate([ident, np.stack([cos, sin_a, sin_b])], axis=1).astype(np.float32)
    return jnp.asarray(tab), q


def _rope_chunk(x, tab_ref, q):
    return x * tab_ref[0] + pltpu.roll(x, LANE - q, 1) * tab_ref[1] + pltpu.roll(x, q, 1) * tab_ref[2]


PROJ_W = 1280
C_QA, C_KA, C_VA, C_CQ, C_CKV, C_KR = 0, 512, 640, 768, 1024, 1152
MLA_NN = MLA_HEADS * MLA_NOPE


def _ab_proj_kernel(xc_ref, xd_ref, mods_ref, w_ref, qn_ref, kvn_ref, wuq_ref, wukv_ref, ta_ref, tm_ref,
                    qa_ref, ka_ref, va_ref, ckv_ref, kr_ref, qm_ref, kvl_ref, wb_ref, *, qa_shift, qm_shift):
    i = pl.program_id(0)

    @pl.when(i == 0)
    def _():
        n_w = w_ref.shape[1]
        wb_ref[:, PROJ_W - LANE:PROJ_W] = jnp.zeros((D, LANE), bf16)
        wb_ref[:, 0:n_w] = w_ref[...].astype(bf16)

    r = _mod_row(i, TOK_TILE)
    mrow = mods_ref[pl.ds(r, 1), :]
    sh, sc = mrow[:, 0:D], mrow[:, D:2 * D]
    x = _pick(i, TOK_TILE, xc_ref, xd_ref)
    h = (x * (1.0 + sc) + sh).astype(bf16)
    proj = _dot(h, wb_ref[...])
    for j in range(4):
        c0 = C_QA + LANE * j
        qa_ref[:, LANE * j:LANE * (j + 1)] = _rope_chunk(proj[:, c0:c0 + LANE], ta_ref, qa_shift).astype(bf16)
    ka_ref[...] = _rope_chunk(proj[:, C_KA:C_KA + LANE], ta_ref, qa_shift)
    va_ref[...] = proj[:, C_VA:C_VA + LANE]
    cq = proj[:, C_CQ:C_CQ + MLA_Q_RANK]
    cq = cq * lax.rsqrt(jnp.mean(cq * cq, axis=-1, keepdims=True) + RMS_EPS) * qn_ref[...]
    ckv = proj[:, C_CKV:C_CKV + MLA_KV_RANK]
    ckv = ckv * lax.rsqrt(jnp.mean(ckv * ckv, axis=-1, keepdims=True) + RMS_EPS) * kvn_ref[...]
    ckv_ref[...] = ckv
    kr_ref[...] = _rope_chunk(proj[:, C_KR:C_KR + LANE], tm_ref, qm_shift)
    qm = _dot(cq.astype(bf16), wuq_ref[...])
    qm_ref[:, 0:MLA_NN] = qm[:, 0:MLA_NN].astype(bf16)
    for j in range(2):
        c0 = MLA_NN + LANE * j
        qm_ref[:, c0:c0 + LANE] = _rope_chunk(qm[:, c0:c0 + LANE], tm_ref, qm_shift).astype(bf16)
    kvl_ref[...] = _dot(ckv.astype(bf16), wukv_ref[...]).astype(bf16)


def _rope_block_index(i):
    tiles_ctx = T_CTX // TOK_TILE
    per_seq = DEN_L // TOK_TILE
    return jnp.where(i < tiles_ctx, 0, 1 + (i - tiles_ctx) % per_seq)


def _ab_proj(xc, xd, mods0, w_in, q_norm, kv_norm, w_uq_p, w_ukv_p):
    tab_a, qa_shift = _rope_table_array(A_HD)
    tab_m, qm_shift = _rope_table_array(MLA_ROPE)
    row_spec = lambda w: pl.BlockSpec((TOK_TILE, w), lambda i: (i, 0))
    xc_spec, xd_spec = _two_stream_specs(TOK_TILE, D)
    tab_spec = pl.BlockSpec((3, TOK_TILE, LANE), lambda i: (0, _rope_block_index(i), 0))
    outs = [(512, bf16), (LANE, f32), (LANE, f32), (LANE, f32), (LANE, f32), (768, bf16), (1024, bf16)]
    return pl.pallas_call(
        functools.partial(_ab_proj_kernel, qa_shift=qa_shift, qm_shift=qm_shift),
        grid=(T_ALL // TOK_TILE,),
        in_specs=[xc_spec, xd_spec, _full((8, 6 * D), 1), _full(w_in.shape, 1), _full((1, MLA_Q_RANK), 1),
                  _full((1, MLA_KV_RANK), 1), _full((MLA_Q_RANK, 768), 1), _full((MLA_KV_RANK, 1024), 1),
                  tab_spec, tab_spec],
        out_specs=[row_spec(w) for w, _ in outs],
        out_shape=[jax.ShapeDtypeStruct((T_ALL, w), dt) for w, dt in outs],
        scratch_shapes=[pltpu.VMEM((D, PROJ_W), bf16)],
        compiler_params=_params(1),
        name="ab_proj",
    )(xc, xd, mods0, w_in, q_norm, kv_norm, w_uq_p, w_ukv_p, tab_a, tab_m)


def _softmax_blocks(s_refs, p_refs, sink_col=None):
    m = s_refs[0][...].max(axis=-1, keepdims=True)
    for s_ref in s_refs[1:]:
        m = jnp.maximum(m, s_ref[...].max(axis=-1, keepdims=True))
    if sink_col is not None:
        m = jnp.maximum(m, sink_col)
    for s_ref, p_ref in zip(s_refs, p_refs):
        p_ref[...] = jnp.exp(s_ref[...] - m).astype(bf16)
    return 0.0 if sink_col is None else jnp.exp(sink_col - m)


def _with_ones(v, axis=1):
    return jnp.concatenate([v, jnp.ones(v.shape, v.dtype)], axis=axis)


def _normalise(o_aug, extra, width):
    return o_aug[:, 0:width] * (1.0 / (o_aug[:, width:width + 1] + extra))


def _sink_column(sink_ref, rows_per_head):
    return jnp.concatenate([jnp.full((rows_per_head, 1), sink_ref[h], f32) for h in range(A_HEADS)], axis=0)


def _mla_q(qm_ref, h):
    rows = qm_ref.shape[0]
    return jnp.concatenate([qm_ref[:, MLA_NOPE * h:MLA_NOPE * (h + 1)],
                            qm_ref[:, MLA_NN + MLA_ROPE * h:MLA_NN + MLA_ROPE * (h + 1)],
                            jnp.zeros((rows, LANE - MLA_NOPE - MLA_ROPE), bf16)], axis=1)


def _mla_k(k_nope_h, k_rope):
    rows = k_nope_h.shape[0]
    return jnp.concatenate([k_nope_h, k_rope, jnp.zeros((rows, LANE - MLA_NOPE - MLA_ROPE), bf16)], axis=1)


def _mix_out_ln(merged_ref, wout_ref, x, mods_ref, r, g_ref, b_ref):
    out = _dot(merged_ref[...], wout_ref[...])
    gate = mods_ref[pl.ds(r, 1), 2 * D:3 * D]
    return _layer_norm(ALPHA * x + gate * out, g_ref[...], b_ref[...])


def _ctx_attn_kernel(sink_ref, qa_ref, ka_ref, va_ref, qm_ref, kvl_ref, kr_ref, x_ref, mods_ref, wout_ref,
                     g_ref, b_ref, o_ref, nk_ref, nv_ref, merged_ref, sa_ref, sm_ref, pa_ref, pm_ref):
    nk_ref[0, 0] = ka_ref[...].T.reshape(A_KV_HEADS, A_HD, CTX_L)
    nv_ref[0, 0] = va_ref[...].T.reshape(A_KV_HEADS, A_HD, CTX_L)
    n = CTX_L
    ka = ka_ref[...].astype(bf16)
    va = va_ref[...].astype(bf16)
    for j in range(A_KV_HEADS):
        q4 = jnp.concatenate([qa_ref[:, A_HD * h:A_HD * (h + 1)] for h in range(A_GROUP * j, A_GROUP * (j + 1))],
                             axis=0)
        sa_ref[A_GROUP * n * j:A_GROUP * n * (j + 1), :] = _dot_nt(q4, ka[:, A_HD * j:A_HD * (j + 1)]) * A_SCALE
    kr = kr_ref[:, 0:MLA_ROPE].astype(bf16)
    for h in range(MLA_HEADS):
        k_cat = _mla_k(kvl_ref[:, MLA_NOPE * h:MLA_NOPE * (h + 1)], kr)
        sm_ref[n * h:n * (h + 1), :] = _dot_nt(_mla_q(qm_ref, h), k_cat) * MLA_SCALE
    sink_a = _softmax_blocks([sa_ref], [pa_ref], _sink_column(sink_ref, n))
    _softmax_blocks([sm_ref], [pm_ref])
    for j in range(A_KV_HEADS):
        rows = slice(A_GROUP * n * j, A_GROUP * n * (j + 1))
        o4 = _normalise(_dot(pa_ref[rows, :], _with_ones(va[:, A_HD * j:A_HD * (j + 1)])), sink_a[rows], A_HD)
        for g in range(A_GROUP):
            h = A_GROUP * j + g
            merged_ref[:, A_HD * h:A_HD * (h + 1)] = o4[n * g:n * (g + 1)].astype(bf16)
    for h in range(MLA_HEADS):
        rows = slice(n * h, n * (h + 1))
        v = _with_ones(kvl_ref[:, MLA_NN + MLA_V * h:MLA_NN + MLA_V * (h + 1)])
        merged_ref[:, MLA_NN + MLA_V * h:MLA_NN + MLA_V * (h + 1)] = (
            _normalise(_dot(pm_ref[rows, :], v), 0.0, MLA_V).astype(bf16))
    o_ref[...] = _mix_out_ln(merged_ref, wout_ref, x_ref[...], mods_ref, 0, g_ref, b_ref)


def _ctx_attn(sink, qa, ka, va, qm, kvl, kr, x_all, mods0, w_out, ln_g, ln_b):
    blk = lambda w: pl.BlockSpec((CTX_L, w), lambda b: (b, 0))
    cache_blk = pl.BlockSpec((1, 1, A_KV_HEADS, A_HD, CTX_L), lambda b: (b, 0, 0, 0, 0))
    cache_shape = jax.ShapeDtypeStruct((N_CTX_B, 1, A_KV_HEADS, A_HD, CTX_L), f32)
    return pl.pallas_call(
        _ctx_attn_kernel,
        grid=(N_CTX_B,),
        in_specs=[pl.BlockSpec(memory_space=pltpu.SMEM), blk(512), blk(LANE), blk(LANE), blk(768), blk(1024),
                  blk(LANE), blk(D), _full((8, 6 * D), 1), _full((D, D), 1), _full((1, D), 1), _full((1, D), 1)],
        out_specs=[blk(D), cache_blk, cache_blk],
        out_shape=[jax.ShapeDtypeStruct((T_CTX, D), f32), cache_shape, cache_shape],
        scratch_shapes=[pltpu.VMEM((CTX_L, D), bf16),
                        pltpu.VMEM((A_HEADS * CTX_L, CTX_L), f32), pltpu.VMEM((MLA_HEADS * CTX_L, CTX_L), f32),
                        pltpu.VMEM((A_HEADS * CTX_L, CTX_L), bf16), pltpu.VMEM((MLA_HEADS * CTX_L, CTX_L), bf16)],
        compiler_params=_params(1),
        name="ctx_attn",
    )(sink, qa, ka, va, qm, kvl, kr, x_all, mods0, w_out, ln_g, ln_b)


QB = 256
WIN = QB + 2 * WINDOW
DEN_BLK0 = T_CTX // DEN_L
MLA_KEYS = CTX_L + DEN_L


def _den_attn_kernel(sink_ref, qa_ref, ka_ref, va_ref, cak_ref, cav_ref, qm_ref, kvl_ref, kr_ref, cckv_ref, ckr_ref,
                     wukv_ref, x_ref, mods_ref, wout_ref, g_ref, b_ref, o_ref, merged_ref, kcat_ref, vcat_ref,
                     saw_ref, sac_ref, sm_ref, paw_ref, pac_ref, pm_ref):
    b = pl.program_id(0)
    n = pl.program_id(1)

    @pl.when(n == 0)
    def _():
        kvc = _dot(cckv_ref[0].astype(bf16), wukv_ref[...]).astype(bf16)
        kr_ctx = ckr_ref[0].astype(bf16)
        kr_lat = kr_ref[:, 0:MLA_ROPE].astype(bf16)
        for h in range(MLA_HEADS):
            ns = slice(MLA_NOPE * h, MLA_NOPE * (h + 1))
            vs = slice(MLA_NN + MLA_V * h, MLA_NN + MLA_V * (h + 1))
            kcat_ref[h, 0:CTX_L, :] = _mla_k(kvc[:, ns], kr_ctx)
            kcat_ref[h, CTX_L:MLA_KEYS, :] = _mla_k(kvl_ref[:, ns], kr_lat)
            vcat_ref[h, 0:CTX_L, :] = _with_ones(kvc[:, vs])
            vcat_ref[h, CTX_L:MLA_KEYS, :] = _with_ones(kvl_ref[:, vs])

    start = pl.multiple_of(jnp.clip(QB * n - WINDOW, 0, DEN_L - WIN), WINDOW)
    grp_rows = A_GROUP * QB
    qpos = QB * n + (lax.broadcasted_iota(jnp.int32, (grp_rows, WIN), 0) & (QB - 1))
    kpos = start + lax.broadcasted_iota(jnp.int32, (grp_rows, WIN), 1)
    valid = jnp.abs(qpos - kpos) <= WINDOW
    kwin = ka_ref[pl.ds(start, WIN), :].astype(bf16)
    vwin = va_ref[pl.ds(start, WIN), :].astype(bf16)
    kctx_t = [cak_ref[0, j].astype(bf16) for j in range(A_KV_HEADS)]
    vctx_t = [cav_ref[0, j].astype(bf16) for j in range(A_KV_HEADS)]
    for j in range(A_KV_HEADS):
        sl = slice(A_HD * j, A_HD * (j + 1))
        rows = slice(grp_rows * j, grp_rows * (j + 1))
        q4 = jnp.concatenate([qa_ref[:, A_HD * h:A_HD * (h + 1)] for h in range(A_GROUP * j, A_GROUP * (j + 1))],
                             axis=0)
        saw_ref[rows, :] = jnp.where(valid, _dot_nt(q4, kwin[:, sl]) * A_SCALE, NEG_INF)
        sac_ref[rows, :] = _dot(q4, kctx_t[j]) * A_SCALE
    for h in range(MLA_HEADS):
        sm_ref[QB * h:QB * (h + 1), :] = _dot_nt(_mla_q(qm_ref, h), kcat_ref[h]) * MLA_SCALE
    sink_a = _softmax_blocks([saw_ref, sac_ref], [paw_ref, pac_ref], _sink_column(sink_ref, QB))
    _softmax_blocks([sm_ref], [pm_ref])
    for j in range(A_KV_HEADS):
        sl = slice(A_HD * j, A_HD * (j + 1))
        rows = slice(grp_rows * j, grp_rows * (j + 1))
        o_aug = (_dot(paw_ref[rows, :], _with_ones(vwin[:, sl]))
                 + _dot_nt(pac_ref[rows, :], _with_ones(vctx_t[j], axis=0)))
        o4 = _normalise(o_aug, sink_a[rows], A_HD)
        for g in range(A_GROUP):
            h = A_GROUP * j + g
            merged_ref[:, A_HD * h:A_HD * (h + 1)] = o4[QB * g:QB * (g + 1)].astype(bf16)
    for h in range(MLA_HEADS):
        rows = slice(QB * h, QB * (h + 1))
        o = _normalise(_dot(pm_ref[rows, :], vcat_ref[h]), 0.0, MLA_V)
        merged_ref[:, MLA_NN + MLA_V * h:MLA_NN + MLA_V * (h + 1)] = o.astype(bf16)
    o_ref[...] = _mix_out_ln(merged_ref, wout_ref, x_ref[...], mods_ref, 1 + b, g_ref, b_ref)


def _den_attn(sink, qa, ka, va, cache_k, cache_v, qm, kvl, kr, cache_ckv, cache_kr, w_ukv_p, x_all, mods0, w_out,
              ln_g, ln_b):
    nq = DEN_L // QB
    qblk = lambda w: pl.BlockSpec((QB, w), lambda b, n: (T_CTX // QB + b * nq + n, 0))
    seq = lambda w: pl.BlockSpec((DEN_L, w), lambda b, n: (DEN_BLK0 + b, 0))
    cache = lambda w: pl.BlockSpec((1, CTX_L, w), lambda b, n: (b, 0, 0))
    cache_a = pl.BlockSpec((1, A_KV_HEADS, A_HD, CTX_L), lambda b, n: (b, 0, 0, 0))
    return pl.pallas_call(
        _den_attn_kernel,
        grid=(N_DEN_B, nq),
        in_specs=[pl.BlockSpec(memory_space=pltpu.SMEM), qblk(512), seq(LANE), seq(LANE), cache_a, cache_a,
                  qblk(768), seq(1024), seq(LANE), cache(MLA_KV_RANK), cache(MLA_ROPE),
                  _full((MLA_KV_RANK, 1024), 2), pl.BlockSpec((QB, D), lambda b, n: (b * nq + n, 0)),
                  _full((8, 6 * D), 2), _full((D, D), 2), _full((1, D), 2),
                  _full((1, D), 2)],
        out_specs=pl.BlockSpec((QB, D), lambda b, n: (b * nq + n, 0)),
        out_shape=jax.ShapeDtypeStruct((T_DEN, D), f32),
        scratch_shapes=[pltpu.VMEM((QB, D), bf16), pltpu.VMEM((MLA_HEADS, MLA_KEYS, LANE), bf16),
                        pltpu.VMEM((MLA_HEADS, MLA_KEYS, 2 * MLA_V), bf16)]
        + [pltpu.VMEM((A_HEADS * QB, w), dt) for dt in (f32, bf16) for w in (WIN, CTX_L, MLA_KEYS)],
        compiler_params=_params(2),
        name="den_attn",
    )(sink, qa, ka, va, cache_k, cache_v, qm, kvl, kr, cache_ckv, cache_kr, w_ukv_p, x_all, mods0, w_out, ln_g, ln_b)


SUBTILE = 256


def _route(x1, mrow, rw_ref, rb_ref):
    sh, sc = mrow[:, 3 * D:4 * D], mrow[:, 4 * D:5 * D]
    h = x1 * (1.0 + sc) + sh
    h_hi = h.astype(bf16)
    h_lo = (h - h_hi.astype(f32)).astype(bf16)
    logits = _dot(h_hi, rw_ref[0]) + (_dot(h_hi, rw_ref[1]) + _dot(h_lo, rw_ref[0]))
    scores = jax.nn.sigmoid(logits)
    lane = lax.broadcasted_iota(jnp.int32, scores.shape, 1).astype(f32)
    sel = jnp.where(lane < N_EXPERTS, scores + rb_ref[...], -jnp.inf)
    gates = jnp.zeros_like(scores)
    for _ in range(TOP_K):
        m = sel.max(axis=-1, keepdims=True)
        idx = jnp.where(sel == m, lane, float(LANE)).min(axis=-1, keepdims=True)
        hit = lane == idx
        gates = jnp.where(hit, scores, gates)
        sel = jnp.where(hit, -jnp.inf, sel)
    return h_hi, gates / gates.sum(axis=-1, keepdims=True) * ROUTED_SCALE


def _router_kernel(xc_ref, xd_ref, mods_ref, rw_ref, rb_ref, x_ref, h_ref, gates_ref):
    i = pl.program_id(0)
    r = _mod_row(i, TOK_TILE)
    mrow = mods_ref[pl.ds(r, 1), :]
    x1 = _pick(i, TOK_TILE, xc_ref, xd_ref)
    x_ref[...] = x1
    h_ref[...], gates_ref[...] = _route(x1, mrow, rw_ref, rb_ref)


def _router(x1c, x1d, mods_l, router_w_p, router_b_p):
    row_spec = lambda w: pl.BlockSpec((TOK_TILE, w), lambda i: (i, 0))
    xc_spec, xd_spec = _two_stream_specs(TOK_TILE, D)
    return pl.pallas_call(
        _router_kernel,
        grid=(T_ALL // TOK_TILE,),
        in_specs=[xc_spec, xd_spec, _full((8, 6 * D), 1), _full((2, D, LANE), 1), _full((1, LANE), 1)],
        out_specs=[row_spec(D), row_spec(D), row_spec(LANE)],
        out_shape=[jax.ShapeDtypeStruct((T_ALL, D), f32), jax.ShapeDtypeStruct((T_ALL, D), bf16),
                   jax.ShapeDtypeStruct((T_ALL, LANE), f32)],
        compiler_params=_params(1),
        name="router",
    )(x1c, x1d, mods_l, router_w_p, router_b_p)


MOE_TOK = 1536
MOE_EG = 8
MOE_VMEM_LIMIT = 60 * 1024 * 1024
MOE_TILE = 512
MOE_FF = MOE_EG * EXPERT_FF


def _moe_kernel(h_ref, gates_ref, mods_ref, wg_ref, wu_ref, wd_ref, sg_ref, su_ref, sd_ref, o_ref):
    p = pl.program_id(0)
    e = pl.program_id(1)
    n_tiles = MOE_TOK // MOE_TILE

    def gate_f(t):
        r = _mod_row(p * n_tiles + t, MOE_TILE)
        return mods_ref[pl.ds(r, 1), 5 * D:6 * D]

    def rows_of(t):
        if isinstance(t, int):
            return pl.ds(t * MOE_TILE, MOE_TILE)
        return pl.ds(pl.multiple_of(t * MOE_TILE, MOE_TILE), MOE_TILE)

    @pl.when(e == 0)
    def _():
        sg = sg_ref[...].astype(bf16)
        su = su_ref[...].astype(bf16)
        sd = sd_ref[...].astype(bf16)

        def body(t, c):
            rows = rows_of(t)
            ht = h_ref[rows, :]
            hid = _silu(_dot(ht, sg)) * _dot(ht, su)
            o_ref[rows, :] = gate_f(t) * _dot(hid.astype(bf16), sd)
            return c

        lax.fori_loop(0, n_tiles, body, 0)

    wg = jnp.concatenate([wg_ref[k].astype(bf16) for k in range(MOE_EG)], axis=1)
    wu = jnp.concatenate([wu_ref[k].astype(bf16) for k in range(MOE_EG)], axis=1)
    wd = jnp.concatenate([wd_ref[k].astype(bf16) for k in range(MOE_EG)], axis=0)
    lane = lax.broadcasted_iota(jnp.int32, (MOE_TILE, LANE), 1)

    def body(t, c):
        rows = rows_of(t)
        ht = h_ref[rows, :]
        hid = _silu(_dot(ht, wg)) * _dot(ht, wu)
        gt = gates_ref[rows, :]
        parts = []
        for k in range(MOE_EG):
            col = jnp.where(lane == e * MOE_EG + k, gt, 0.0).sum(axis=-1, keepdims=True)
            parts.append((hid[:, EXPERT_FF * k:EXPERT_FF * (k + 1)] * col).astype(bf16))
        o_ref[rows, :] += gate_f(t) * _dot(jnp.concatenate(parts, axis=1), wd)
        return c

    for t in range(n_tiles):
        body(t, 0)


def _moe(l, h, gates, mods_l, wg, wu, wd, sg, su, sd):
    tok = lambda w: pl.BlockSpec((MOE_TOK, w), lambda p, e: (p, 0))
    return pl.pallas_call(
        _moe_kernel,
        grid=(T_ALL // MOE_TOK, N_EXPERTS // MOE_EG),
        in_specs=[tok(D), tok(LANE), _full((8, 6 * D), 2),
                  pl.BlockSpec((None, MOE_EG, D, EXPERT_FF), lambda p, e: (l, e, 0, 0)),
                  pl.BlockSpec((None, MOE_EG, D, EXPERT_FF), lambda p, e: (l, e, 0, 0)),
                  pl.BlockSpec((None, MOE_EG, EXPERT_FF, D), lambda p, e: (l, e, 0, 0)),
                  pl.BlockSpec((None, D, SHARED_FF), lambda p, e: (l, 0, 0)),
                  pl.BlockSpec((None, D, SHARED_FF), lambda p, e: (l, 0, 0)),
                  pl.BlockSpec((None, SHARED_FF, D), lambda p, e: (l, 0, 0))],
        out_specs=tok(D),
        out_shape=jax.ShapeDtypeStruct((T_ALL, D), f32),
        compiler_params=pltpu.CompilerParams(dimension_semantics=("arbitrary", "arbitrary"),
                                             vmem_limit_bytes=MOE_VMEM_LIMIT),
        name="moe",
    )(h, gates, mods_l, wg, wu, wd, sg, su, sd)


def _moe_finish_kernel(x_ref, acc_ref, g_ref, b_ref, o_ref):
    o_ref[...] = _layer_norm(ALPHA * x_ref[...] + acc_ref[...], g_ref[...], b_ref[...])


def _moe_finish(x1, acc, tile0, n_rows, ln_g, ln_b):
    src = pl.BlockSpec((TOK_TILE, D), lambda i: (tile0 + i, 0))
    return pl.pallas_call(
        _moe_finish_kernel,
        grid=(n_rows // TOK_TILE,),
        in_specs=[src, src, _full((1, D), 1), _full((1, D), 1)],
        out_specs=pl.BlockSpec((TOK_TILE, D), lambda i: (i, 0)),
        out_shape=jax.ShapeDtypeStruct((n_rows, D), f32),
        compiler_params=_params(1),
        name="moe_finish",
    )(x1, acc, ln_g, ln_b)


def _router_weights(l, router_w, router_bias):
    rw = jnp.pad(router_w[l], ((0, 0), (0, LANE - N_EXPERTS)))
    rw_hi, rw_lo = _split_bf16(rw)
    rb = jnp.pad(router_bias[l], (0, LANE - N_EXPERTS)).reshape(1, LANE)
    return jnp.stack([rw_hi, rw_lo]), rb


S5_Q = 8
S5_NGB = D // LANE


def _s5_in_kernel(x_ref, acc_ref, lg_ref, lb_ref, mods_ref, w_ref, x2_ref, u_ref, u2_ref, slab_ref, *, row_of,
                  seq_len):
    r = row_of(pl.program_id(0))
    mrow = mods_ref[pl.ds(r, 1), :]
    sh, sc = mrow[:, 0:D], mrow[:, D:2 * D]
    x2 = _layer_norm(ALPHA * x_ref[...] + acc_ref[...], lg_ref[...], lb_ref[...])
    x2_ref[...] = x2
    h = (x2 * (1.0 + sc) + sh).astype(bf16)
    u = _dot(h, w_ref[...])
    u_ref[...] = u
    for s in range(S5_NGB):
        slab_ref[s] = u[:, LANE * s:LANE * (s + 1)]
    kt = seq_len // S5_Q
    for s in range(S5_NGB):
        for q in range(TOK_TILE // seq_len):
            for j in range(S5_Q):
                u2_ref[s, q * kt:(q + 1) * kt, LANE * j:LANE * (j + 1)] = (
                    slab_ref[s, pl.ds(q * seq_len + j, kt, stride=S5_Q), :].astype(bf16))


def _s5_in(x1, acc, ln_g, ln_b, tile0, mods1, w_in_c, n_b, seq_len, row_of):
    n_tiles = n_b * seq_len // TOK_TILE
    chunks = TOK_TILE // S5_Q
    src = pl.BlockSpec((TOK_TILE, D), lambda i: (tile0 + i, 0))
    dst = pl.BlockSpec((TOK_TILE, D), lambda i: (i, 0))
    return pl.pallas_call(
        functools.partial(_s5_in_kernel, row_of=row_of, seq_len=min(seq_len, TOK_TILE)),
        grid=(n_tiles,),
        in_specs=[src, src, _full((1, D), 1), _full((1, D), 1), _full((8, 6 * D), 1), _full((D, D), 1)],
        out_specs=[dst, dst, pl.BlockSpec((S5_NGB, chunks, D), lambda i: (0, i, 0))],
        out_shape=[jax.ShapeDtypeStruct((n_b * seq_len, D), f32), jax.ShapeDtypeStruct((n_b * seq_len, D), f32),
                   jax.ShapeDtypeStruct((S5_NGB, n_tiles * chunks, D), bf16)],
        scratch_shapes=[pltpu.VMEM((S5_NGB, TOK_TILE, LANE), f32)],
        compiler_params=_params(1),
        name="s5_in",
    )(x1, acc, ln_g, ln_b, mods1, w_in_c)


S5_GL = (LANE // S5_CH) * S5_P
S5_ROWS_C = (CTX_L // S5_Q) * N_CTX_B
S5_ROWS_D = (DEN_L // S5_Q) * N_DEN_B


def _s5_scan_kernel(lre_ref, lim_ref, ldt_ref, btr_ref, bti_ref, ctr_ref, cti_ref, uc_ref, ud_ref, h0_ref,
                    yc_ref, yd_ref, st_ref, win_ref, mso_ref, wit_ref, a_ref, s_ref, hp_ref):
    gl = S5_GL
    rowg = lax.shift_right_logical(lax.broadcasted_iota(jnp.int32, (LANE, gl), 0), 4)
    colg = lax.shift_right_logical(lax.broadcasted_iota(jnp.int32, (LANE, gl), 1), 6)
    same_group = rowg == colg
    reps = LANE // S5_CH

    def expand(t):
        return jnp.where(same_group, jnp.concatenate([t] * reps, axis=0), 0.0)

    def expand_c(t):
        return jnp.where(same_group, jnp.concatenate([t] * reps, axis=1), 0.0)

    for d in range(2):
        fwd = d == 0
        lre, lim = lre_ref[d], lim_ref[d]
        dt = jnp.exp(ldt_ref[d])
        a, w = lre * dt, lim * dt
        pre = [jnp.exp(m * a) * jnp.cos(m * w) for m in range(S5_Q + 1)]
        pim = [jnp.exp(m * a) * jnp.sin(m * w) for m in range(S5_Q + 1)]
        xr, xi = pre[1] - 1.0, pim[1]
        den = lre * lre + lim * lim
        cfr, cfi = (xr * lre + xi * lim) / den, (xi * lre - xr * lim) / den
        btr, bti = btr_ref[d], bti_ref[d]
        bexp_r = expand(cfr * btr - cfi * bti)
        bexp_i = expand(cfr * bti + cfi * btr)
        cexp_r, cexp_i = expand_c(ctr_ref[d]), expand_c(cti_ref[d])
        for m in range(S5_Q + 1):
            a_ref[m, :, 0:gl] = cexp_r * pre[m] - cexp_i * pim[m]
            a_ref[m, :, gl:2 * gl] = -(cexp_r * pim[m] + cexp_i * pre[m])
        for j in range(S5_Q):
            m = S5_Q - 1 - j if fwd else j
            win_ref[LANE * j:LANE * (j + 1), 0:gl] = (pre[m] * bexp_r - pim[m] * bexp_i).astype(bf16)
            win_ref[LANE * j:LANE * (j + 1), gl:2 * gl] = (pre[m] * bexp_i + pim[m] * bexp_r).astype(bf16)
        for j in range(S5_Q):
            m = j + 1 if fwd else S5_Q - j
            mso_ref[LANE * j:LANE * (j + 1), :] = a_ref[m].astype(bf16)
        b2 = jnp.concatenate([bexp_r, bexp_i], axis=1).astype(bf16)
        kt = [_dot_nt(b2, a_ref[tau].astype(bf16)) for tau in range(S5_Q)]
        for j in range(S5_Q):
            for jp in range(S5_Q):
                tau = jp - j if fwd else j - jp
                blk = slice(LANE * j, LANE * (j + 1)), slice(LANE * jp, LANE * (jp + 1))
                if fwd:
                    wit_ref[blk] = kt[tau] if tau >= 0 else jnp.zeros((LANE, LANE), f32)
                elif tau >= 0:
                    wit_ref[blk] = wit_ref[blk] + kt[tau]

        l8r, l8i = pre[S5_Q], pim[S5_Q]

        nsl = gl // LANE

        def slabs(ref, rs, first):
            return jnp.concatenate([ref[first + sl, rs, :] for sl in range(nsl)], axis=1)

        def put_slabs(ref, rs, first, val):
            for sl in range(nsl):
                ref[first + sl, rs, :] = val[:, LANE * sl:LANE * (sl + 1)]

        def advance(hr, hi_, sr, si):
            return l8r * hr - l8i * hi_ + sr, l8r * hi_ + l8i * hr + si

        def run(u_ref, y_ref, n_b, n_k, h_init):
            rows = n_b * n_k
            s = _dot(u_ref[0], win_ref[...])
            if n_b % SUB == 0:
                pitch = n_k + 1
                for bb in range(n_b):
                    dst = slice(bb * pitch, bb * pitch + n_k)
                    put_slabs(s_ref, dst, 0, s[bb * n_k:(bb + 1) * n_k, 0:gl])
                    put_slabs(s_ref, dst, nsl, s[bb * n_k:(bb + 1) * n_k, gl:2 * gl])

                def step(i, carry):
                    hr, hi_ = carry
                    rs = pl.ds(i if fwd else n_k - 1 - i, n_b, stride=pitch)
                    put_slabs(hp_ref, rs, 0, hr)
                    put_slabs(hp_ref, rs, nsl, hi_)
                    return advance(hr, hi_, slabs(s_ref, rs, 0), slabs(s_ref, rs, nsl))

                h_fin = lax.fori_loop(0, n_k, step, h_init)
                hp = jnp.concatenate(
                    [jnp.concatenate([hp_ref[sl, bb * pitch:bb * pitch + n_k, :] for sl in range(2 * nsl)], axis=1)
                     for bb in range(n_b)], axis=0).astype(bf16)
            else:
                put_slabs(s_ref, slice(0, rows), 0, s[:, 0:gl])
                put_slabs(s_ref, slice(0, rows), nsl, s[:, gl:2 * gl])
                n_it = n_k // SUB

                def step(i, carry):
                    it = i if fwd else n_it - 1 - i
                    out = []
                    for bb in range(n_b):
                        hr, hi_ = carry[bb]
                        rs = pl.ds(pl.multiple_of(bb * n_k + it * SUB, SUB), SUB)
                        s_re, s_im = slabs(s_ref, rs, 0), slabs(s_ref, rs, nsl)
                        prev_r, prev_i = [None] * SUB, [None] * SUB
                        for sub in (range(SUB) if fwd else reversed(range(SUB))):
                            prev_r[sub], prev_i[sub] = hr, hi_
                            hr, hi_ = advance(hr, hi_, s_re[sub:sub + 1], s_im[sub:sub + 1])
                        put_slabs(hp_ref, rs, 0, jnp.concatenate(prev_r, axis=0))
                        put_slabs(hp_ref, rs, nsl, jnp.concatenate(prev_i, axis=0))
                        out.append((hr, hi_))
                    return tuple(out)

                fin = lax.fori_loop(0, n_it, step, tuple((h_init[0][bb:bb + 1], h_init[1][bb:bb + 1])
                                                          for bb in range(n_b)))
                h_fin = (jnp.concatenate([f[0] for f in fin], axis=0), jnp.concatenate([f[1] for f in fin], axis=0))
                hp = jnp.concatenate([hp_ref[sl, 0:rows, :] for sl in range(2 * nsl)], axis=1).astype(bf16)
            y = _dot_nt(hp, mso_ref[...])
            if fwd:
                y_ref[0] = y
            else:
                y_ref[0] += y
            return h_fin

        zeros = jnp.zeros((N_CTX_B, gl), f32)
        hr, hi_ = run(uc_ref, yc_ref, N_CTX_B, CTX_L // S5_Q, (zeros, zeros))
        st_ref[d, 0] = hr
        st_ref[d, 1] = hi_
        run(ud_ref, yd_ref, N_DEN_B, DEN_L // S5_Q, (h0_ref[d, 0], h0_ref[d, 1]))

    wit = wit_ref[...].astype(bf16)
    yc_ref[0] += _dot(uc_ref[0], wit)
    yd_ref[0] += _dot(ud_ref[0], wit)


def _s5_scan(lam_re, lam_im, log_dt, bt_re, bt_im, ct_re, ct_im, u2c, u2d, h0):
    gl = S5_GL
    vec = pl.BlockSpec((2, 1, gl), lambda g: (0, 0, g))
    tab = pl.BlockSpec((2, S5_CH, gl), lambda g: (0, 0, g))
    ctab = pl.BlockSpec((2, LANE, S5_P), lambda g: (0, g, 0))
    rows = lambda n: pl.BlockSpec((1, n, D), lambda g: (g, 0, 0))
    return pl.pallas_call(
        _s5_scan_kernel,
        grid=(S5_NGB,),
        in_specs=[vec, vec, vec, tab, tab, ctab, ctab, rows(S5_ROWS_C), rows(S5_ROWS_D),
                  pl.BlockSpec((2, 2, N_DEN_B, gl), lambda g: (0, 0, 0, g))],
        out_specs=[rows(S5_ROWS_C), rows(S5_ROWS_D), pl.BlockSpec((2, 2, N_CTX_B, gl), lambda g: (0, 0, 0, g))],
        out_shape=[jax.ShapeDtypeStruct((S5_NGB, S5_ROWS_C, D), f32), jax.ShapeDtypeStruct((S5_NGB, S5_ROWS_D, D), f32),
                   jax.ShapeDtypeStruct((2, 2, N_CTX_B, S5_G * S5_P), f32)],
        scratch_shapes=[pltpu.VMEM((D, 2 * gl), bf16), pltpu.VMEM((D, 2 * gl), bf16), pltpu.VMEM((D, D), f32),
                        pltpu.VMEM((S5_Q + 1, LANE, 2 * gl), f32),
                        pltpu.VMEM((2 * gl // LANE, S5_ROWS_C + 2 * N_CTX_B, LANE), f32),
                        pltpu.VMEM((2 * gl // LANE, S5_ROWS_C + 2 * N_CTX_B, LANE), f32)],
        compiler_params=_params(1),
        name="s5_scan",
    )(lam_re, lam_im, log_dt, bt_re, bt_im, ct_re, ct_im, u2c, u2d, h0)


def _gelu_tanh(x):
    return 0.5 * x * (1.0 + jnp.tanh(np.sqrt(2.0 / np.pi).astype(np.float32) * (x + 0.044715 * (x * x * x))))


def _s5_out_kernel(xc_ref, xd_ref, uc_ref, ud_ref, yc_ref, yd_ref, mods_ref, dsk_ref, wout_ref, g_ref, b_ref, rw_ref,
                   rb_ref, x1_ref, h_ref, gates_ref, slab_ref, wob_ref):
    i = pl.program_id(0)

    @pl.when(i == 0)
    def _():
        wob_ref[...] = wout_ref[...].astype(bf16)

    is_ctx = i < T_CTX // TOK_TILE
    r = _mod_row(i, TOK_TILE)
    mrow = mods_ref[pl.ds(r, 1), :]
    u = _pick(i, TOK_TILE, uc_ref, ud_ref)

    kt = CTX_L // S5_Q
    for s in range(S5_NGB):
        for q in range(TOK_TILE // CTX_L):
            for j in range(S5_Q):
                blk = (s, slice(q * kt, (q + 1) * kt), slice(LANE * j, LANE * (j + 1)))
                slab_ref[s, pl.ds(q * CTX_L + j, kt, stride=S5_Q), :] = jnp.where(is_ctx, yc_ref[blk], yd_ref[blk])
    halves = [slice(a, a + SUBTILE) for a in range(0, TOK_TILE, SUBTILE)]
    zs = []
    for rows in halves:
        y = jnp.concatenate([slab_ref[s, rows, :] for s in range(S5_NGB)], axis=1) + dsk_ref[...] * u[rows]
        zs.append(_dot(_gelu_tanh(y).astype(bf16), wob_ref[...]))
    for rows, z in zip(halves, zs):
        out = z[:, 0:D] * jax.nn.sigmoid(z[:, D:2 * D])
        x = jnp.where(is_ctx, xc_ref[rows, :], xd_ref[rows, :])
        x1 = _layer_norm(ALPHA * x + mrow[:, 2 * D:3 * D] * out, g_ref[...], b_ref[...])
        x1_ref[rows, :] = x1
        h_ref[rows, :], gates_ref[rows, :] = _route(x1, mrow, rw_ref, rb_ref)


def _s5_out(xc, xd, uc, ud, yc, yd, mods1, d_skip, w_out_c, ln_g, ln_b, rw, rb):
    row_spec = lambda w: pl.BlockSpec((TOK_TILE, w), lambda i: (i, 0))
    uc_spec, ud_spec = _two_stream_specs(TOK_TILE, D)
    n_ctx = T_CTX // TOK_TILE
    chunks = TOK_TILE // S5_Q
    return pl.pallas_call(
        _s5_out_kernel,
        grid=(T_ALL // TOK_TILE,),
        in_specs=[uc_spec, ud_spec, uc_spec, ud_spec,
                  pl.BlockSpec((S5_NGB, chunks, D), lambda i: (0, jnp.minimum(i, n_ctx - 1), 0)),
                  pl.BlockSpec((S5_NGB, chunks, D), lambda i: (0, jnp.maximum(i - n_ctx, 0), 0)),
                  _full((8, 6 * D), 1), _full((1, D), 1), _full((D, 2 * D), 1), _full((1, D), 1), _full((1, D), 1),
                  _full((2, D, LANE), 1), _full((1, LANE), 1)],
        out_specs=[row_spec(D), row_spec(D), row_spec(LANE)],
        out_shape=[jax.ShapeDtypeStruct((T_ALL, D), f32), jax.ShapeDtypeStruct((T_ALL, D), bf16),
                   jax.ShapeDtypeStruct((T_ALL, LANE), f32)],
        scratch_shapes=[pltpu.VMEM((S5_NGB, TOK_TILE, LANE), f32), pltpu.VMEM((D, 2 * D), bf16)],
        compiler_params=_params(1),
        name="s5_out",
    )(xc, xd, uc, ud, yc, yd, mods1, d_skip, w_out_c, ln_g, ln_b, rw, rb)


def kernel(x_prompt, x_sample, c, cache_attn_k, cache_attn_v, cache_mla_ckv, cache_mla_krope, state_ssm, c_ctx,
           ada_w, ada_b, ln_mix_g, ln_mix_b, ln_ffn_g, ln_ffn_b, w_in_ab, attn_sink, mla_q_norm, mla_kv_norm,
           mla_w_uq, mla_w_ukv, w_out_ab, w_in_c, s5_lam_re, s5_lam_im, s5_log_dt, s5_b_re, s5_b_im, s5_c_re,
           s5_c_im, s5_d, w_out_c, router_w, router_bias, exp_w_gate, exp_w_up, exp_w_down, sh_w_gate, sh_w_up,
           sh_w_down):
    row = lambda v: v.reshape(1, -1)
    xc, xd = x_prompt.reshape(T_CTX, D), x_sample.reshape(T_DEN, D)
    cvec8 = jnp.concatenate([c_ctx[None, :], c, jnp.zeros((8 - 1 - N_DEN_B, D), f32)], axis=0)
    mods = _adaln(cvec8, ada_w, ada_b)

    uq = mla_w_uq[0].reshape(MLA_Q_RANK, MLA_HEADS, MLA_NOPE + MLA_ROPE)
    w_uq_p = jnp.concatenate([uq[:, :, :MLA_NOPE].reshape(MLA_Q_RANK, -1), uq[:, :, MLA_NOPE:].reshape(MLA_Q_RANK, -1)],
                             axis=1).astype(bf16)
    ukv = mla_w_ukv[0].reshape(MLA_KV_RANK, MLA_HEADS, MLA_NOPE + MLA_V)
    w_ukv_p = jnp.concatenate([ukv[:, :, :MLA_NOPE].reshape(MLA_KV_RANK, -1),
                               ukv[:, :, MLA_NOPE:].reshape(MLA_KV_RANK, -1)], axis=1).astype(bf16)
    qa, ka, va, ckv, kr, qm, kvl = _ab_proj(xc, xd, mods[0], w_in_ab[0], row(mla_q_norm[0]), row(mla_kv_norm[0]),
                                            w_uq_p, w_ukv_p)
    w_out_b = w_out_ab[0].astype(bf16)
    g0, b0 = row(ln_mix_g[0]), row(ln_mix_b[0])
    x1c, nk_t, nv_t = _ctx_attn(attn_sink[0], qa, ka, va, qm, kvl, kr, xc, mods[0], w_out_b, g0, b0)
    new_attn_k = jnp.transpose(nk_t, (0, 1, 4, 2, 3))
    new_attn_v = jnp.transpose(nv_t, (0, 1, 4, 2, 3))
    x1d = _den_attn(attn_sink[0], qa, ka, va,
                    jnp.transpose(cache_attn_k[:, 0], (0, 2, 3, 1)), jnp.transpose(cache_attn_v[:, 0], (0, 2, 3, 1)),
                    qm, kvl, kr, cache_mla_ckv[:, 0], cache_mla_krope[:, 0], w_ukv_p, xd, mods[0], w_out_b, g0, b0)
    rw0, rb0 = _router_weights(0, router_w, router_bias)
    x1, h, gates = _router(x1c, x1d, mods[0], rw0, rb0)
    acc = _moe(0, h, gates, mods[0], exp_w_gate, exp_w_up, exp_w_down, sh_w_gate, sh_w_up, sh_w_down)

    w_in_c_b = w_in_c[0].astype(bf16)
    lg0, lb0 = row(ln_ffn_g[0]), row(ln_ffn_b[0])
    n_ctx_tiles = T_CTX // TOK_TILE
    x2c, uc, u2c = _s5_in(x1, acc, lg0, lb0, 0, mods[1], w_in_c_b, N_CTX_B, CTX_L, lambda i: 0)
    x2d, ud, u2d = _s5_in(x1, acc, lg0, lb0, n_ctx_tiles, mods[1], w_in_c_b, N_DEN_B, DEN_L,
                          lambda i: 1 + i // (DEN_L // TOK_TILE))
    gp = S5_G * S5_P
    chan_major_b = lambda t: jnp.transpose(t[0], (0, 3, 1, 2)).reshape(2, S5_CH, gp)
    chan_major_c = lambda t: t[0].reshape(2, S5_G * S5_CH, S5_P)
    h0 = jnp.transpose(state_ssm[:, 0], (1, 2, 0, 3, 4)).reshape(2, 2, N_DEN_B, gp)
    yc, yd, st = _s5_scan(s5_lam_re[0].reshape(2, 1, gp), s5_lam_im[0].reshape(2, 1, gp),
                          jnp.repeat(s5_log_dt[0], S5_P, axis=-1).reshape(2, 1, gp),
                          chan_major_b(s5_b_re), chan_major_b(s5_b_im), chan_major_c(s5_c_re), chan_major_c(s5_c_im),
                          u2c, u2d, h0)
    rw1, rb1 = _router_weights(1, router_w, router_bias)
    x3, h, gates = _s5_out(x2c, x2d, uc, ud, yc, yd, mods[1], row(s5_d[0]),
                           w_out_c[0], row(ln_mix_g[1]), row(ln_mix_b[1]), rw1, rb1)
    acc = _moe(1, h, gates, mods[1], exp_w_gate, exp_w_up, exp_w_down, sh_w_gate, sh_w_up, sh_w_down)
    lg1, lb1 = row(ln_ffn_g[1]), row(ln_ffn_b[1])
    y_prompt = _moe_finish(x3, acc, 0, T_CTX, lg1, lb1).reshape(N_CTX_B, CTX_L, D)
    y_sample = _moe_finish(x3, acc, n_ctx_tiles, T_DEN, lg1, lb1).reshape(N_DEN_B, DEN_L, D)
    new_mla_ckv = ckv[:T_CTX].reshape(N_CTX_B, 1, CTX_L, MLA_KV_RANK)
    new_mla_krope = kr[:T_CTX, :MLA_ROPE].reshape(N_CTX_B, 1, CTX_L, MLA_ROPE)
    new_state_ssm = jnp.transpose(st, (2, 0, 1, 3)).reshape(N_CTX_B, 1, 2, 2, S5_G, S5_P)
    return (y_prompt, y_sample, new_attn_k, new_attn_v, new_mla_ckv, new_mla_krope, new_state_ssm)
```

```python
import functools

import jax
import jax.numpy as jnp
import numpy as np
from jax import lax
from jax.experimental import pallas as pl
from jax.experimental.pallas import tpu as pltpu

f32 = jnp.float32
bf16 = jnp.bfloat16

D = 1024
N_CTX_B, CTX_L = 16, 256
N_DEN_B, DEN_L = 2, 1024
T_CTX = N_CTX_B * CTX_L
T_DEN = N_DEN_B * DEN_L
T_ALL = T_CTX + T_DEN
GRID_W = 64
WINDOW = 128
ROPE_BASE = 10000.0
A_HEADS, A_KV_HEADS, A_HD = 8, 2, 64
A_GROUP = A_HEADS // A_KV_HEADS
A_SCALE = A_HD ** -0.5
MLA_HEADS, MLA_Q_RANK, MLA_KV_RANK = 8, 256, 128
MLA_NOPE, MLA_ROPE, MLA_V = 64, 32, 64
MLA_SCALE = (MLA_NOPE + MLA_ROPE) ** -0.5
N_EXPERTS, TOP_K, EXPERT_FF, SHARED_FF = 64, 6, 128, 128
ROUTED_SCALE = 2.5
DEPTH = 2
ALPHA = (2.0 * DEPTH) ** 0.25
LN_EPS = 1e-5
RMS_EPS = 1e-6
NEG_INF = -1e30
S5_G, S5_CH, S5_P = 64, 16, 64

LANE = 128
SUB = 8
VMEM_LIMIT = 56 * 1024 * 1024

TOK_TILE = 512


def _mod_row(tile_idx, tile_rows):
    start = tile_idx * tile_rows
    return jnp.where(start < T_CTX, 0, 1 + (start - T_CTX) // DEN_L)


def _layer_norm(y, g, b):
    mu = jnp.mean(y, axis=-1, keepdims=True)
    yc = y - mu
    var = jnp.mean(yc * yc, axis=-1, keepdims=True)
    return yc * lax.rsqrt(var + LN_EPS) * g + b


def _silu(x):
    return x * jax.nn.sigmoid(x)


def _dot(a, b):
    return jnp.dot(a, b, preferred_element_type=f32)


def _dot_nt(a, b):
    return lax.dot_general(a, b, (((1,), (1,)), ((), ())), preferred_element_type=f32)


def _split_bf16(a):
    hi = a.astype(bf16)
    return hi, (a - hi.astype(f32)).astype(bf16)


def _full(shape, n_grid):
    zeros = tuple(0 for _ in shape)
    return pl.BlockSpec(shape, lambda *_: zeros)


def _two_stream_specs(tile_rows, width):
    n_ctx = T_CTX // tile_rows
    return (pl.BlockSpec((tile_rows, width), lambda i: (jnp.minimum(i, n_ctx - 1), 0)),
            pl.BlockSpec((tile_rows, width), lambda i: (jnp.maximum(i - n_ctx, 0), 0)))


def _pick(i, tile_rows, ctx_ref, den_ref):
    return jnp.where(i < T_CTX // tile_rows, ctx_ref[...], den_ref[...])


def _params(n_grid):
    return pltpu.CompilerParams(dimension_semantics=("arbitrary",) * n_grid, vmem_limit_bytes=VMEM_LIMIT)


ADA_TN = 1536


def _adaln_kernel(c_ref, w_ref, b_ref, *o_refs):
    s_hi, s_lo = _split_bf16(_silu(c_ref[...]))
    w_hi, w_lo = _split_bf16(w_ref[0])
    val = _dot(s_hi, w_hi) + (_dot(s_hi, w_lo) + _dot(s_lo, w_hi)) + b_ref[0]
    for k, o_ref in enumerate(o_refs):
        @pl.when(pl.program_id(0) == k)
        def _():
            o_ref[...] = val


def _adaln(cvec8, ada_w, ada_b):
    n = 6 * D
    nb = n // ADA_TN
    layer_spec = lambda k: pl.BlockSpec((8, ADA_TN), lambda l, j: (0, jnp.clip((l - k) * nb + j, 0, nb - 1)))
    return pl.pallas_call(
        _adaln_kernel,
        grid=(DEPTH, n // ADA_TN),
        in_specs=[
            pl.BlockSpec((8, D), lambda l, j: (0, 0)),
            pl.BlockSpec((1, D, ADA_TN), lambda l, j: (l, 0, j)),
            pl.BlockSpec((1, 1, ADA_TN), lambda l, j: (l, 0, j)),
        ],
        out_specs=[layer_spec(k) for k in range(DEPTH)],
        out_shape=[jax.ShapeDtypeStruct((8, n), f32) for _ in range(DEPTH)],
        compiler_params=_params(2),
        name="adaln",
    )(cvec8, ada_w, ada_b.reshape(DEPTH, 1, n))


def _rope_table_array(head_dim):
    q = head_dim // 4
    pos = np.arange(DEN_L)
    row, col = (pos // GRID_W).astype(np.float64), (pos % GRID_W).astype(np.float64)
    lane = np.arange(LANE) % head_dim
    is_col = lane >= head_dim // 2
    w = lane % (head_dim // 2)
    first = w < q
    inv_freq = ROPE_BASE ** (-np.arange(q, dtype=np.float64) / q)
    ang = np.where(is_col[None, :], col[:, None], row[:, None]) * inv_freq[w % q][None, :]
    cos, sin = np.cos(ang), np.sin(ang)
    sin_a = np.where(first[None, :], -sin, 0.0)
    sin_b = np.where(first[None, :], 0.0, sin)
    ident = np.stack([np.ones((TOK_TILE, LANE)), np.zeros((TOK_TILE, LANE)), np.zeros((TOK_TILE, LANE))])
    tab = np.concatenate([ident, np.stack([cos, sin_a, sin_b])], axis=1).astype(np.float32)
    return jnp.asarray(tab), q


def _rope_chunk(x, tab_ref, q):
    return x * tab_ref[0] + pltpu.roll(x, LANE - q, 1) * tab_ref[1] + pltpu.roll(x, q, 1) * tab_ref[2]


PROJ_W = 1280
C_QA, C_KA, C_VA, C_CQ, C_CKV, C_KR = 0, 512, 640, 768, 1024, 1152
MLA_NN = MLA_HEADS * MLA_NOPE


def _ab_proj_kernel(xc_ref, xd_ref, mods_ref, w_ref, qn_ref, kvn_ref, wuq_ref, wukv_ref, ta_ref, tm_ref,
                    qa_ref, ka_ref, va_ref, ckv_ref, kr_ref, qm_ref, kvl_ref, krc_ref, wb_ref, *, qa_shift, qm_shift):
    i = pl.program_id(0)

    @pl.when(i == 0)
    def _():
        n_w = w_ref.shape[1]
        wb_ref[:, PROJ_W - LANE:PROJ_W] = jnp.zeros((D, LANE), bf16)
        wb_ref[:, 0:n_w] = w_ref[...].astype(bf16)

    r = _mod_row(i, TOK_TILE)
    mrow = mods_ref[pl.ds(r, 1), :]
    sh, sc = mrow[:, 0:D], mrow[:, D:2 * D]
    x = _pick(i, TOK_TILE, xc_ref, xd_ref)
    h = (x * (1.0 + sc) + sh).astype(bf16)
    proj = _dot(h, wb_ref[...])
    for j in range(4):
        c0 = C_QA + LANE * j
        qa_ref[:, LANE * j:LANE * (j + 1)] = _rope_chunk(proj[:, c0:c0 + LANE], ta_ref, qa_shift).astype(bf16)
    ka_ref[...] = _rope_chunk(proj[:, C_KA:C_KA + LANE], ta_ref, qa_shift)
    va_ref[...] = proj[:, C_VA:C_VA + LANE]
    cq = proj[:, C_CQ:C_CQ + MLA_Q_RANK]
    cq = cq * lax.rsqrt(jnp.mean(cq * cq, axis=-1, keepdims=True) + RMS_EPS) * qn_ref[...]
    ckv = proj[:, C_CKV:C_CKV + MLA_KV_RANK]
    ckv = ckv * lax.rsqrt(jnp.mean(ckv * ckv, axis=-1, keepdims=True) + RMS_EPS) * kvn_ref[...]
    kr = _rope_chunk(proj[:, C_KR:C_KR + LANE], tm_ref, qm_shift)
    kr_ref[...] = kr

    @pl.when(i < T_CTX // TOK_TILE)
    def _():
        ckv_ref[...] = ckv
        kr_t = kr.T
        for b in range(TOK_TILE // CTX_L):
            krc_ref[b, 0] = kr_t[0:MLA_ROPE, CTX_L * b:CTX_L * (b + 1)]

    qm = _dot(cq.astype(bf16), wuq_ref[...])
    qm_ref[:, 0:MLA_NN] = qm[:, 0:MLA_NN].astype(bf16)
    for j in range(2):
        c0 = MLA_NN + LANE * j
        qm_ref[:, c0:c0 + LANE] = _rope_chunk(qm[:, c0:c0 + LANE], tm_ref, qm_shift).astype(bf16)
    kvl_ref[...] = _dot(ckv.astype(bf16), wukv_ref[...]).astype(bf16)


def _rope_block_index(i):
    tiles_ctx = T_CTX // TOK_TILE
    per_seq = DEN_L // TOK_TILE
    return jnp.where(i < tiles_ctx, 0, 1 + (i - tiles_ctx) % per_seq)


def _ab_proj(xc, xd, mods0, w_in, q_norm, kv_norm, w_uq_p, w_ukv_p):
    tab_a, qa_shift = _rope_table_array(A_HD)
    tab_m, qm_shift = _rope_table_array(MLA_ROPE)
    row_spec = lambda w: pl.BlockSpec((TOK_TILE, w), lambda i: (i, 0))
    xc_spec, xd_spec = _two_stream_specs(TOK_TILE, D)
    tab_spec = pl.BlockSpec((3, TOK_TILE, LANE), lambda i: (0, _rope_block_index(i), 0))
    outs = [(512, bf16), (LANE, f32), (LANE, f32), (LANE, f32), (LANE, f32), (768, bf16), (1024, bf16)]
    last_ctx = T_CTX // TOK_TILE - 1
    per = TOK_TILE // CTX_L
    ctx_spec = pl.BlockSpec((TOK_TILE, LANE), lambda i: (jnp.minimum(i, last_ctx), 0))
    krc_spec = pl.BlockSpec((per, 1, MLA_ROPE, CTX_L), lambda i: (jnp.minimum(i, last_ctx), 0, 0, 0))
    return pl.pallas_call(
        functools.partial(_ab_proj_kernel, qa_shift=qa_shift, qm_shift=qm_shift),
        grid=(T_ALL // TOK_TILE,),
        in_specs=[xc_spec, xd_spec, _full((8, 6 * D), 1), _full(w_in.shape, 1), _full((1, MLA_Q_RANK), 1),
                  _full((1, MLA_KV_RANK), 1), _full((MLA_Q_RANK, 768), 1), _full((MLA_KV_RANK, 1024), 1),
                  tab_spec, tab_spec],
        out_specs=[ctx_spec if k == 3 else row_spec(w) for k, (w, _) in enumerate(outs)] + [krc_spec],
        out_shape=[jax.ShapeDtypeStruct((T_CTX if k == 3 else T_ALL, w), dt) for k, (w, dt) in enumerate(outs)]
        + [jax.ShapeDtypeStruct((N_CTX_B, 1, MLA_ROPE, CTX_L), f32)],
        scratch_shapes=[pltpu.VMEM((D, PROJ_W), bf16)],
        compiler_params=_params(1),
        name="ab_proj",
    )(xc, xd, mods0, w_in, q_norm, kv_norm, w_uq_p, w_ukv_p, tab_a, tab_m)


def _softmax_blocks(s_refs, p_refs, sink_col=None):
    m = s_refs[0][...].max(axis=-1, keepdims=True)
    for s_ref in s_refs[1:]:
        m = jnp.maximum(m, s_ref[...].max(axis=-1, keepdims=True))
    if sink_col is not None:
        m = jnp.maximum(m, sink_col)
    for s_ref, p_ref in zip(s_refs, p_refs):
        p_ref[...] = jnp.exp(s_ref[...] - m).astype(bf16)
    return 0.0 if sink_col is None else jnp.exp(sink_col - m)


def _with_ones(v, axis=1):
    return jnp.concatenate([v, jnp.ones(v.shape, v.dtype)], axis=axis)


def _normalise(o_aug, extra, width):
    return o_aug[:, 0:width] * (1.0 / (o_aug[:, width:width + 1] + extra))


def _sink_column(sink_ref, rows_per_head):
    return jnp.concatenate([jnp.full((rows_per_head, 1), sink_ref[h], f32) for h in range(A_HEADS)], axis=0)


def _mla_q(qm_ref, h):
    rows = qm_ref.shape[0]
    return jnp.concatenate([qm_ref[:, MLA_NOPE * h:MLA_NOPE * (h + 1)],
                            qm_ref[:, MLA_NN + MLA_ROPE * h:MLA_NN + MLA_ROPE * (h + 1)],
                            jnp.zeros((rows, LANE - MLA_NOPE - MLA_ROPE), bf16)], axis=1)


def _mla_k(k_nope_h, k_rope):
    rows = k_nope_h.shape[0]
    return jnp.concatenate([k_nope_h, k_rope, jnp.zeros((rows, LANE - MLA_NOPE - MLA_ROPE), bf16)], axis=1)


def _mix_out_ln(merged_ref, wout_ref, x, mods_ref, r, g_ref, b_ref):
    out = _dot(merged_ref[...], wout_ref[...])
    gate = mods_ref[pl.ds(r, 1), 2 * D:3 * D]
    return _layer_norm(ALPHA * x + gate * out, g_ref[...], b_ref[...])


def _ctx_attn_kernel(sink_ref, qa_ref, ka_ref, va_ref, qm_ref, kvl_ref, kr_ref, x_ref, mods_ref, wout_ref,
                     g_ref, b_ref, o_ref, nk_ref, nv_ref, merged_ref, sa_ref, sm_ref, pa_ref, pm_ref):
    nk_ref[0, 0] = ka_ref[...].T.reshape(A_KV_HEADS, A_HD, CTX_L)
    nv_ref[0, 0] = va_ref[...].T.reshape(A_KV_HEADS, A_HD, CTX_L)
    n = CTX_L
    ka = ka_ref[...].astype(bf16)
    va = va_ref[...].astype(bf16)
    for j in range(A_KV_HEADS):
        q4 = jnp.concatenate([qa_ref[:, A_HD * h:A_HD * (h + 1)] for h in range(A_GROUP * j, A_GROUP * (j + 1))],
                             axis=0)
        sa_ref[A_GROUP * n * j:A_GROUP * n * (j + 1), :] = _dot_nt(q4, ka[:, A_HD * j:A_HD * (j + 1)]) * A_SCALE
    kr = kr_ref[:, 0:MLA_ROPE].astype(bf16)
    for h in range(MLA_HEADS):
        k_cat = _mla_k(kvl_ref[:, MLA_NOPE * h:MLA_NOPE * (h + 1)], kr)
        sm_ref[n * h:n * (h + 1), :] = _dot_nt(_mla_q(qm_ref, h), k_cat) * MLA_SCALE
    sink_a = _softmax_blocks([sa_ref], [pa_ref], _sink_column(sink_ref, n))
    _softmax_blocks([sm_ref], [pm_ref])
    for j in range(A_KV_HEADS):
        rows = slice(A_GROUP * n * j, A_GROUP * n * (j + 1))
        o4 = _normalise(_dot(pa_ref[rows, :], _with_ones(va[:, A_HD * j:A_HD * (j + 1)])), sink_a[rows], A_HD)
        for g in range(A_GROUP):
            h = A_GROUP * j + g
            merged_ref[:, A_HD * h:A_HD * (h + 1)] = o4[n * g:n * (g + 1)].astype(bf16)
    for h in range(MLA_HEADS):
        rows = slice(n * h, n * (h + 1))
        v = _with_ones(kvl_ref[:, MLA_NN + MLA_V * h:MLA_NN + MLA_V * (h + 1)])
        merged_ref[:, MLA_NN + MLA_V * h:MLA_NN + MLA_V * (h + 1)] = (
            _normalise(_dot(pm_ref[rows, :], v), 0.0, MLA_V).astype(bf16))
    o_ref[...] = _mix_out_ln(merged_ref, wout_ref, x_ref[...], mods_ref, 0, g_ref, b_ref)


def _ctx_attn(sink, qa, ka, va, qm, kvl, kr, x_all, mods0, w_out, ln_g, ln_b):
    blk = lambda w: pl.BlockSpec((CTX_L, w), lambda b: (b, 0))
    cache_blk = pl.BlockSpec((1, 1, A_KV_HEADS, A_HD, CTX_L), lambda b: (b, 0, 0, 0, 0))
    cache_shape = jax.ShapeDtypeStruct((N_CTX_B, 1, A_KV_HEADS, A_HD, CTX_L), f32)
    return pl.pallas_call(
        _ctx_attn_kernel,
        grid=(N_CTX_B,),
        in_specs=[pl.BlockSpec(memory_space=pltpu.SMEM), blk(512), blk(LANE), blk(LANE), blk(768), blk(1024),
                  blk(LANE), blk(D), _full((8, 6 * D), 1), _full((D, D), 1), _full((1, D), 1), _full((1, D), 1)],
        out_specs=[blk(D), cache_blk, cache_blk],
        out_shape=[jax.ShapeDtypeStruct((T_CTX, D), f32), cache_shape, cache_shape],
        scratch_shapes=[pltpu.VMEM((CTX_L, D), bf16),
                        pltpu.VMEM((A_HEADS * CTX_L, CTX_L), f32), pltpu.VMEM((MLA_HEADS * CTX_L, CTX_L), f32),
                        pltpu.VMEM((A_HEADS * CTX_L, CTX_L), bf16), pltpu.VMEM((MLA_HEADS * CTX_L, CTX_L), bf16)],
        compiler_params=_params(1),
        name="ctx_attn",
    )(sink, qa, ka, va, qm, kvl, kr, x_all, mods0, w_out, ln_g, ln_b)


QB = 256
WIN = QB + 2 * WINDOW
DEN_BLK0 = T_CTX // DEN_L
MLA_KEYS = CTX_L + DEN_L


def _den_attn_kernel(sink_ref, qa_ref, ka_ref, va_ref, cak_ref, cav_ref, qm_ref, kvl_ref, kr_ref, cckv_ref, ckr_ref,
                     wukv_ref, x_ref, mods_ref, wout_ref, g_ref, b_ref, o_ref, merged_ref, kcat_ref, vcat_ref,
                     saw_ref, sac_ref, sm_ref, paw_ref, pac_ref, pm_ref):
    b = pl.program_id(0)
    n = pl.program_id(1)

    @pl.when(n == 0)
    def _():
        kvc = _dot(cckv_ref[0].astype(bf16), wukv_ref[...]).astype(bf16)
        kr_ctx = ckr_ref[0].astype(bf16)
        kr_lat = kr_ref[:, 0:MLA_ROPE].astype(bf16)
        for h in range(MLA_HEADS):
            ns = slice(MLA_NOPE * h, MLA_NOPE * (h + 1))
            vs = slice(MLA_NN + MLA_V * h, MLA_NN + MLA_V * (h + 1))
            kcat_ref[h, 0:CTX_L, :] = _mla_k(kvc[:, ns], kr_ctx)
            kcat_ref[h, CTX_L:MLA_KEYS, :] = _mla_k(kvl_ref[:, ns], kr_lat)
            vcat_ref[h, 0:CTX_L, :] = _with_ones(kvc[:, vs])
            vcat_ref[h, CTX_L:MLA_KEYS, :] = _with_ones(kvl_ref[:, vs])

    start = pl.multiple_of(jnp.clip(QB * n - WINDOW, 0, DEN_L - WIN), WINDOW)
    grp_rows = A_GROUP * QB
    qpos = QB * n + (lax.broadcasted_iota(jnp.int32, (grp_rows, WIN), 0) & (QB - 1))
    kpos = start + lax.broadcasted_iota(jnp.int32, (grp_rows, WIN), 1)
    valid = jnp.abs(qpos - kpos) <= WINDOW
    kwin = ka_ref[pl.ds(start, WIN), :].astype(bf16)
    vwin = va_ref[pl.ds(start, WIN), :].astype(bf16)
    kctx_t = [cak_ref[0, j].astype(bf16) for j in range(A_KV_HEADS)]
    vctx_t = [cav_ref[0, j].astype(bf16) for j in range(A_KV_HEADS)]
    for j in range(A_KV_HEADS):
        sl = slice(A_HD * j, A_HD * (j + 1))
        rows = slice(grp_rows * j, grp_rows * (j + 1))
        q4 = jnp.concatenate([qa_ref[:, A_HD * h:A_HD * (h + 1)] for h in range(A_GROUP * j, A_GROUP * (j + 1))],
                             axis=0)
        saw_ref[rows, :] = jnp.where(valid, _dot_nt(q4, kwin[:, sl]) * A_SCALE, NEG_INF)
        sac_ref[rows, :] = _dot(q4, kctx_t[j]) * A_SCALE
    for h in range(MLA_HEADS):
        sm_ref[QB * h:QB * (h + 1), :] = _dot_nt(_mla_q(qm_ref, h), kcat_ref[h]) * MLA_SCALE
    sink_a = _softmax_blocks([saw_ref, sac_ref], [paw_ref, pac_ref], _sink_column(sink_ref, QB))
    _softmax_blocks([sm_ref], [pm_ref])
    for j in range(A_KV_HEADS):
        sl = slice(A_HD * j, A_HD * (j + 1))
        rows = slice(grp_rows * j, grp_rows * (j + 1))
        o_aug = (_dot(paw_ref[rows, :], _with_ones(vwin[:, sl]))
                 + _dot_nt(pac_ref[rows, :], _with_ones(vctx_t[j], axis=0)))
        o4 = _normalise(o_aug, sink_a[rows], A_HD)
        for g in range(A_GROUP):
            h = A_GROUP * j + g
            merged_ref[:, A_HD * h:A_HD * (h + 1)] = o4[QB * g:QB * (g + 1)].astype(bf16)
    for h in range(MLA_HEADS):
        rows = slice(QB * h, QB * (h + 1))
        o = _normalise(_dot(pm_ref[rows, :], vcat_ref[h]), 0.0, MLA_V)
        merged_ref[:, MLA_NN + MLA_V * h:MLA_NN + MLA_V * (h + 1)] = o.astype(bf16)
    o_ref[...] = _mix_out_ln(merged_ref, wout_ref, x_ref[...], mods_ref, 1 + b, g_ref, b_ref)


def _den_attn(sink, qa, ka, va, cache_k, cache_v, qm, kvl, kr, cache_ckv, cache_kr, w_ukv_p, x_all, mods0, w_out,
              ln_g, ln_b):
    nq = DEN_L // QB
    qblk = lambda w: pl.BlockSpec((QB, w), lambda b, n: (T_CTX // QB + b * nq + n, 0))
    seq = lambda w: pl.BlockSpec((DEN_L, w), lambda b, n: (DEN_BLK0 + b, 0))
    cache = lambda w: pl.BlockSpec((1, CTX_L, w), lambda b, n: (b, 0, 0))
    cache_a = pl.BlockSpec((1, A_KV_HEADS, A_HD, CTX_L), lambda b, n: (b, 0, 0, 0))
    return pl.pallas_call(
        _den_attn_kernel,
        grid=(N_DEN_B, nq),
        in_specs=[pl.BlockSpec(memory_space=pltpu.SMEM), qblk(512), seq(LANE), seq(LANE), cache_a, cache_a,
                  qblk(768), seq(1024), seq(LANE), cache(MLA_KV_RANK), cache(MLA_ROPE),
                  _full((MLA_KV_RANK, 1024), 2), pl.BlockSpec((QB, D), lambda b, n: (b * nq + n, 0)),
                  _full((8, 6 * D), 2), _full((D, D), 2), _full((1, D), 2),
                  _full((1, D), 2)],
        out_specs=pl.BlockSpec((QB, D), lambda b, n: (b * nq + n, 0)),
        out_shape=jax.ShapeDtypeStruct((T_DEN, D), f32),
        scratch_shapes=[pltpu.VMEM((QB, D), bf16), pltpu.VMEM((MLA_HEADS, MLA_KEYS, LANE), bf16),
                        pltpu.VMEM((MLA_HEADS, MLA_KEYS, 2 * MLA_V), bf16)]
        + [pltpu.VMEM((A_HEADS * QB, w), dt) for dt in (f32, bf16) for w in (WIN, CTX_L, MLA_KEYS)],
        compiler_params=_params(2),
        name="den_attn",
    )(sink, qa, ka, va, cache_k, cache_v, qm, kvl, kr, cache_ckv, cache_kr, w_ukv_p, x_all, mods0, w_out, ln_g, ln_b)


SUBTILE = 256


def _route(x1, mrow, rw_ref, rb_ref):
    sh, sc = mrow[:, 3 * D:4 * D], mrow[:, 4 * D:5 * D]
    h = x1 * (1.0 + sc) + sh
    h_hi = h.astype(bf16)
    h_lo = (h - h_hi.astype(f32)).astype(bf16)
    logits = _dot(h_hi, rw_ref[0]) + (_dot(h_hi, rw_ref[1]) + _dot(h_lo, rw_ref[0]))
    scores = jax.nn.sigmoid(logits)
    lane = lax.broadcasted_iota(jnp.int32, scores.shape, 1).astype(f32)
    sel = jnp.where(lane < N_EXPERTS, scores + rb_ref[...], -jnp.inf)
    gates = jnp.zeros_like(scores)
    for _ in range(TOP_K):
        m = sel.max(axis=-1, keepdims=True)
        idx = jnp.where(sel == m, lane, float(LANE)).min(axis=-1, keepdims=True)
        hit = lane == idx
        gates = jnp.where(hit, scores, gates)
        sel = jnp.where(hit, -jnp.inf, sel)
    return h_hi, gates / gates.sum(axis=-1, keepdims=True) * ROUTED_SCALE


def _router_kernel(xc_ref, xd_ref, mods_ref, rw_ref, rb_ref, x_ref, h_ref, gates_ref):
    i = pl.program_id(0)
    r = _mod_row(i, TOK_TILE)
    mrow = mods_ref[pl.ds(r, 1), :]
    x1 = _pick(i, TOK_TILE, xc_ref, xd_ref)
    x_ref[...] = x1
    h_ref[...], gates_ref[...] = _route(x1, mrow, rw_ref, rb_ref)


def _router(x1c, x1d, mods_l, router_w_p, router_b_p):
    row_spec = lambda w: pl.BlockSpec((TOK_TILE, w), lambda i: (i, 0))
    xc_spec, xd_spec = _two_stream_specs(TOK_TILE, D)
    return pl.pallas_call(
        _router_kernel,
        grid=(T_ALL // TOK_TILE,),
        in_specs=[xc_spec, xd_spec, _full((8, 6 * D), 1), _full((2, D, LANE), 1), _full((1, LANE), 1)],
        out_specs=[row_spec(D), row_spec(D), row_spec(LANE)],
        out_shape=[jax.ShapeDtypeStruct((T_ALL, D), f32), jax.ShapeDtypeStruct((T_ALL, D), bf16),
                   jax.ShapeDtypeStruct((T_ALL, LANE), f32)],
        compiler_params=_params(1),
        name="router",
    )(x1c, x1d, mods_l, router_w_p, router_b_p)


MOE_TOK = 1536
MOE_EG = 8
MOE_VMEM_LIMIT = 60 * 1024 * 1024
MOE_TILE = 512
MOE_FF = MOE_EG * EXPERT_FF


def _moe_kernel(h_ref, gates_ref, mods_ref, wg_ref, wu_ref, wd_ref, sg_ref, su_ref, sd_ref, o_ref):
    p = pl.program_id(0)
    e = pl.program_id(1)
    n_tiles = MOE_TOK // MOE_TILE

    def gate_f(t):
        r = _mod_row(p * n_tiles + t, MOE_TILE)
        return mods_ref[pl.ds(r, 1), 5 * D:6 * D]

    def rows_of(t):
        if isinstance(t, int):
            return pl.ds(t * MOE_TILE, MOE_TILE)
        return pl.ds(pl.multiple_of(t * MOE_TILE, MOE_TILE), MOE_TILE)

    @pl.when(e == 0)
    def _():
        sg = sg_ref[...].astype(bf16)
        su = su_ref[...].astype(bf16)
        sd = sd_ref[...].astype(bf16)

        def body(t, c):
            rows = rows_of(t)
            ht = h_ref[rows, :]
            hid = _silu(_dot(ht, sg)) * _dot(ht, su)
            o_ref[rows, :] = gate_f(t) * _dot(hid.astype(bf16), sd)
            return c

        lax.fori_loop(0, n_tiles, body, 0)

    wg = jnp.concatenate([wg_ref[k].astype(bf16) for k in range(MOE_EG)], axis=1)
    wu = jnp.concatenate([wu_ref[k].astype(bf16) for k in range(MOE_EG)], axis=1)
    wd = jnp.concatenate([wd_ref[k].astype(bf16) for k in range(MOE_EG)], axis=0)
    lane = lax.broadcasted_iota(jnp.int32, (MOE_TILE, LANE), 1)

    def body(t, c):
        rows = rows_of(t)
        ht = h_ref[rows, :]
        hid = _silu(_dot(ht, wg)) * _dot(ht, wu)
        gt = gates_ref[rows, :]
        parts = []
        for k in range(MOE_EG):
            col = jnp.where(lane == e * MOE_EG + k, gt, 0.0).sum(axis=-1, keepdims=True)
            parts.append((hid[:, EXPERT_FF * k:EXPERT_FF * (k + 1)] * col).astype(bf16))
        o_ref[rows, :] += gate_f(t) * _dot(jnp.concatenate(parts, axis=1), wd)
        return c

    for t in range(n_tiles):
        body(t, 0)


def _moe(l, h, gates, mods_l, wg, wu, wd, sg, su, sd):
    tok = lambda w: pl.BlockSpec((MOE_TOK, w), lambda p, e: (p, 0))
    return pl.pallas_call(
        _moe_kernel,
        grid=(T_ALL // MOE_TOK, N_EXPERTS // MOE_EG),
        in_specs=[tok(D), tok(LANE), _full((8, 6 * D), 2),
                  pl.BlockSpec((None, MOE_EG, D, EXPERT_FF), lambda p, e: (l, e, 0, 0)),
                  pl.BlockSpec((None, MOE_EG, D, EXPERT_FF), lambda p, e: (l, e, 0, 0)),
                  pl.BlockSpec((None, MOE_EG, EXPERT_FF, D), lambda p, e: (l, e, 0, 0)),
                  pl.BlockSpec((None, D, SHARED_FF), lambda p, e: (l, 0, 0)),
                  pl.BlockSpec((None, D, SHARED_FF), lambda p, e: (l, 0, 0)),
                  pl.BlockSpec((None, SHARED_FF, D), lambda p, e: (l, 0, 0))],
        out_specs=tok(D),
        out_shape=jax.ShapeDtypeStruct((T_ALL, D), f32),
        compiler_params=pltpu.CompilerParams(dimension_semantics=("arbitrary", "arbitrary"),
                                             vmem_limit_bytes=MOE_VMEM_LIMIT),
        name="moe",
    )(h, gates, mods_l, wg, wu, wd, sg, su, sd)


def _moe_finish_kernel(x_ref, acc_ref, g_ref, b_ref, o_ref):
    o_ref[...] = _layer_norm(ALPHA * x_ref[...] + acc_ref[...], g_ref[...], b_ref[...])


def _moe_finish(x1, acc, tile0, n_rows, ln_g, ln_b):
    src = pl.BlockSpec((TOK_TILE, D), lambda i: (tile0 + i, 0))
    return pl.pallas_call(
        _moe_finish_kernel,
        grid=(n_rows // TOK_TILE,),
        in_specs=[src, src, _full((1, D), 1), _full((1, D), 1)],
        out_specs=pl.BlockSpec((TOK_TILE, D), lambda i: (i, 0)),
        out_shape=jax.ShapeDtypeStruct((n_rows, D), f32),
        compiler_params=_params(1),
        name="moe_finish",
    )(x1, acc, ln_g, ln_b)


def _router_weights(l, router_w, router_bias):
    rw = jnp.pad(router_w[l], ((0, 0), (0, LANE - N_EXPERTS)))
    rw_hi, rw_lo = _split_bf16(rw)
    rb = jnp.pad(router_bias[l], (0, LANE - N_EXPERTS)).reshape(1, LANE)
    return jnp.stack([rw_hi, rw_lo]), rb


S5_Q = 8
S5_NGB = D // LANE


def _s5_in_kernel(x_ref, acc_ref, lg_ref, lb_ref, mods_ref, w_ref, x2_ref, u_ref, u2_ref, slab_ref, *, row_of,
                  seq_len):
    r = row_of(pl.program_id(0))
    mrow = mods_ref[pl.ds(r, 1), :]
    sh, sc = mrow[:, 0:D], mrow[:, D:2 * D]
    x2 = _layer_norm(ALPHA * x_ref[...] + acc_ref[...], lg_ref[...], lb_ref[...])
    x2_ref[...] = x2
    h = (x2 * (1.0 + sc) + sh).astype(bf16)
    u = _dot(h, w_ref[...])
    u_ref[...] = u
    for s in range(S5_NGB):
        slab_ref[s] = u[:, LANE * s:LANE * (s + 1)]
    kt = seq_len // S5_Q
    for s in range(S5_NGB):
        for q in range(TOK_TILE // seq_len):
            for j in range(S5_Q):
                u2_ref[s, q * kt:(q + 1) * kt, LANE * j:LANE * (j + 1)] = (
                    slab_ref[s, pl.ds(q * seq_len + j, kt, stride=S5_Q), :].astype(bf16))


def _s5_in(x1, acc, ln_g, ln_b, tile0, mods1, w_in_c, n_b, seq_len, row_of):
    n_tiles = n_b * seq_len // TOK_TILE
    chunks = TOK_TILE // S5_Q
    src = pl.BlockSpec((TOK_TILE, D), lambda i: (tile0 + i, 0))
    dst = pl.BlockSpec((TOK_TILE, D), lambda i: (i, 0))
    return pl.pallas_call(
        functools.partial(_s5_in_kernel, row_of=row_of, seq_len=min(seq_len, TOK_TILE)),
        grid=(n_tiles,),
        in_specs=[src, src, _full((1, D), 1), _full((1, D), 1), _full((8, 6 * D), 1), _full((D, D), 1)],
        out_specs=[dst, dst, pl.BlockSpec((S5_NGB, chunks, D), lambda i: (0, i, 0))],
        out_shape=[jax.ShapeDtypeStruct((n_b * seq_len, D), f32), jax.ShapeDtypeStruct((n_b * seq_len, D), f32),
                   jax.ShapeDtypeStruct((S5_NGB, n_tiles * chunks, D), bf16)],
        scratch_shapes=[pltpu.VMEM((S5_NGB, TOK_TILE, LANE), f32)],
        compiler_params=_params(1),
        name="s5_in",
    )(x1, acc, ln_g, ln_b, mods1, w_in_c)


S5_GL = (LANE // S5_CH) * S5_P
S5_ROWS_C = (CTX_L // S5_Q) * N_CTX_B
S5_ROWS_D = (DEN_L // S5_Q) * N_DEN_B


def _s5_scan_kernel(lre_ref, lim_ref, ldt_ref, btr_ref, bti_ref, ctr_ref, cti_ref, uc_ref, ud_ref, h0_ref,
                    yc_ref, yd_ref, st_ref, win_ref, mso_ref, wit_ref, a_ref, s_ref, hp_ref):
    gl = S5_GL
    rowg = lax.shift_right_logical(lax.broadcasted_iota(jnp.int32, (LANE, gl), 0), 4)
    colg = lax.shift_right_logical(lax.broadcasted_iota(jnp.int32, (LANE, gl), 1), 6)
    same_group = rowg == colg
    reps = LANE // S5_CH

    def expand(t):
        return jnp.where(same_group, jnp.concatenate([t] * reps, axis=0), 0.0)

    def expand_c(t):
        return jnp.where(same_group, jnp.concatenate([t] * reps, axis=1), 0.0)

    for d in range(2):
        fwd = d == 0
        lre, lim = lre_ref[d], lim_ref[d]
        dt = jnp.exp(ldt_ref[d])
        a, w = lre * dt, lim * dt
        pre = [jnp.exp(m * a) * jnp.cos(m * w) for m in range(S5_Q + 1)]
        pim = [jnp.exp(m * a) * jnp.sin(m * w) for m in range(S5_Q + 1)]
        xr, xi = pre[1] - 1.0, pim[1]
        den = lre * lre + lim * lim
        cfr, cfi = (xr * lre + xi * lim) / den, (xi * lre - xr * lim) / den
        btr, bti = btr_ref[d], bti_ref[d]
        bexp_r = expand(cfr * btr - cfi * bti)
        bexp_i = expand(cfr * bti + cfi * btr)
        cexp_r, cexp_i = expand_c(ctr_ref[d]), expand_c(cti_ref[d])
        for m in range(S5_Q + 1):
            a_ref[m, :, 0:gl] = cexp_r * pre[m] - cexp_i * pim[m]
            a_ref[m, :, gl:2 * gl] = -(cexp_r * pim[m] + cexp_i * pre[m])
        for j in range(S5_Q):
            m = S5_Q - 1 - j if fwd else j
            win_ref[LANE * j:LANE * (j + 1), 0:gl] = (pre[m] * bexp_r - pim[m] * bexp_i).astype(bf16)
            win_ref[LANE * j:LANE * (j + 1), gl:2 * gl] = (pre[m] * bexp_i + pim[m] * bexp_r).astype(bf16)
        for j in range(S5_Q):
            m = j + 1 if fwd else S5_Q - j
            mso_ref[LANE * j:LANE * (j + 1), :] = a_ref[m].astype(bf16)
        b2 = jnp.concatenate([bexp_r, bexp_i], axis=1).astype(bf16)
        kt = [_dot_nt(b2, a_ref[tau].astype(bf16)) for tau in range(S5_Q)]
        for j in range(S5_Q):
            for jp in range(S5_Q):
                tau = jp - j if fwd else j - jp
                blk = slice(LANE * j, LANE * (j + 1)), slice(LANE * jp, LANE * (jp + 1))
                if fwd:
                    wit_ref[blk] = kt[tau] if tau >= 0 else jnp.zeros((LANE, LANE), f32)
                elif tau >= 0:
                    wit_ref[blk] = wit_ref[blk] + kt[tau]

        l8r, l8i = pre[S5_Q], pim[S5_Q]

        nsl = gl // LANE

        def slabs(ref, rs, first):
            return jnp.concatenate([ref[first + sl, rs, :] for sl in range(nsl)], axis=1)

        def put_slabs(ref, rs, first, val):
            for sl in range(nsl):
                ref[first + sl, rs, :] = val[:, LANE * sl:LANE * (sl + 1)]

        def advance(hr, hi_, sr, si):
            return l8r * hr - l8i * hi_ + sr, l8r * hi_ + l8i * hr + si

        def run(u_ref, y_ref, n_b, n_k, h_init):
            rows = n_b * n_k
            s = _dot(u_ref[0], win_ref[...])
            if n_b % SUB == 0:
                pitch = n_k + 1
                for bb in range(n_b):
                    dst = slice(bb * pitch, bb * pitch + n_k)
                    put_slabs(s_ref, dst, 0, s[bb * n_k:(bb + 1) * n_k, 0:gl])
                    put_slabs(s_ref, dst, nsl, s[bb * n_k:(bb + 1) * n_k, gl:2 * gl])

                def step(i, carry):
                    hr, hi_ = carry
                    rs = pl.ds(i if fwd else n_k - 1 - i, n_b, stride=pitch)
                    put_slabs(hp_ref, rs, 0, hr)
                    put_slabs(hp_ref, rs, nsl, hi_)
                    return advance(hr, hi_, slabs(s_ref, rs, 0), slabs(s_ref, rs, nsl))

                h_fin = lax.fori_loop(0, n_k, step, h_init)
                hp = jnp.concatenate(
                    [jnp.concatenate([hp_ref[sl, bb * pitch:bb * pitch + n_k, :] for sl in range(2 * nsl)], axis=1)
                     for bb in range(n_b)], axis=0).astype(bf16)
            else:
                put_slabs(s_ref, slice(0, rows), 0, s[:, 0:gl])
                put_slabs(s_ref, slice(0, rows), nsl, s[:, gl:2 * gl])
                n_it = n_k // SUB

                def step(i, carry):
                    it = i if fwd else n_it - 1 - i
                    out = []
                    for bb in range(n_b):
                        hr, hi_ = carry[bb]
                        rs = pl.ds(pl.multiple_of(bb * n_k + it * SUB, SUB), SUB)
                        s_re, s_im = slabs(s_ref, rs, 0), slabs(s_ref, rs, nsl)
                        prev_r, prev_i = [None] * SUB, [None] * SUB
                        for sub in (range(SUB) if fwd else reversed(range(SUB))):
                            prev_r[sub], prev_i[sub] = hr, hi_
                            hr, hi_ = advance(hr, hi_, s_re[sub:sub + 1], s_im[sub:sub + 1])
                        put_slabs(hp_ref, rs, 0, jnp.concatenate(prev_r, axis=0))
                        put_slabs(hp_ref, rs, nsl, jnp.concatenate(prev_i, axis=0))
                        out.append((hr, hi_))
                    return tuple(out)

                fin = lax.fori_loop(0, n_it, step, tuple((h_init[0][bb:bb + 1], h_init[1][bb:bb + 1])
                                                          for bb in range(n_b)))
                h_fin = (jnp.concatenate([f[0] for f in fin], axis=0), jnp.concatenate([f[1] for f in fin], axis=0))
                hp = jnp.concatenate([hp_ref[sl, 0:rows, :] for sl in range(2 * nsl)], axis=1).astype(bf16)
            y = _dot_nt(hp, mso_ref[...])
            if fwd:
                y_ref[0] = y
            else:
                y_ref[0] += y
            return h_fin

        zeros = jnp.zeros((N_CTX_B, gl), f32)
        hr, hi_ = run(uc_ref, yc_ref, N_CTX_B, CTX_L // S5_Q, (zeros, zeros))
        st_ref[d, 0] = hr
        st_ref[d, 1] = hi_
        run(ud_ref, yd_ref, N_DEN_B, DEN_L // S5_Q, (h0_ref[d, 0], h0_ref[d, 1]))

    wit = wit_ref[...].astype(bf16)
    yc_ref[0] += _dot(uc_ref[0], wit)
    yd_ref[0] += _dot(ud_ref[0], wit)


def _s5_scan(lam_re, lam_im, log_dt, bt_re, bt_im, ct_re, ct_im, u2c, u2d, h0):
    gl = S5_GL
    vec = pl.BlockSpec((2, 1, gl), lambda g: (0, 0, g))
    tab = pl.BlockSpec((2, S5_CH, gl), lambda g: (0, 0, g))
    ctab = pl.BlockSpec((2, LANE, S5_P), lambda g: (0, g, 0))
    rows = lambda n: pl.BlockSpec((1, n, D), lambda g: (g, 0, 0))
    return pl.pallas_call(
        _s5_scan_kernel,
        grid=(S5_NGB,),
        in_specs=[vec, vec, vec, tab, tab, ctab, ctab, rows(S5_ROWS_C), rows(S5_ROWS_D),
                  pl.BlockSpec((2, 2, N_DEN_B, gl), lambda g: (0, 0, 0, g))],
        out_specs=[rows(S5_ROWS_C), rows(S5_ROWS_D), pl.BlockSpec((2, 2, N_CTX_B, gl), lambda g: (0, 0, 0, g))],
        out_shape=[jax.ShapeDtypeStruct((S5_NGB, S5_ROWS_C, D), f32), jax.ShapeDtypeStruct((S5_NGB, S5_ROWS_D, D), f32),
                   jax.ShapeDtypeStruct((2, 2, N_CTX_B, S5_G * S5_P), f32)],
        scratch_shapes=[pltpu.VMEM((D, 2 * gl), bf16), pltpu.VMEM((D, 2 * gl), bf16), pltpu.VMEM((D, D), f32),
                        pltpu.VMEM((S5_Q + 1, LANE, 2 * gl), f32),
                        pltpu.VMEM((2 * gl // LANE, S5_ROWS_C + 2 * N_CTX_B, LANE), f32),
                        pltpu.VMEM((2 * gl // LANE, S5_ROWS_C + 2 * N_CTX_B, LANE), f32)],
        compiler_params=_params(1),
        name="s5_scan",
    )(lam_re, lam_im, log_dt, bt_re, bt_im, ct_re, ct_im, u2c, u2d, h0)


def _gelu_tanh(x):
    return 0.5 * x * (1.0 + jnp.tanh(np.sqrt(2.0 / np.pi).astype(np.float32) * (x + 0.044715 * (x * x * x))))


def _s5_out_kernel(xc_ref, xd_ref, uc_ref, ud_ref, yc_ref, yd_ref, mods_ref, dsk_ref, wout_ref, g_ref, b_ref, rw_ref,
                   rb_ref, x1_ref, h_ref, gates_ref, slab_ref, wob_ref):
    i = pl.program_id(0)

    @pl.when(i == 0)
    def _():
        wob_ref[...] = wout_ref[...].astype(bf16)

    is_ctx = i < T_CTX // TOK_TILE
    r = _mod_row(i, TOK_TILE)
    mrow = mods_ref[pl.ds(r, 1), :]
    u = _pick(i, TOK_TILE, uc_ref, ud_ref)

    kt = CTX_L // S5_Q
    for s in range(S5_NGB):
        for q in range(TOK_TILE // CTX_L):
            for j in range(S5_Q):
                blk = (s, slice(q * kt, (q + 1) * kt), slice(LANE * j, LANE * (j + 1)))
                slab_ref[s, pl.ds(q * CTX_L + j, kt, stride=S5_Q), :] = jnp.where(is_ctx, yc_ref[blk], yd_ref[blk])
    halves = [slice(a, a + SUBTILE) for a in range(0, TOK_TILE, SUBTILE)]
    zs = []
    for rows in halves:
        y = jnp.concatenate([slab_ref[s, rows, :] for s in range(S5_NGB)], axis=1) + dsk_ref[...] * u[rows]
        zs.append(_dot(_gelu_tanh(y).astype(bf16), wob_ref[...]))
    for rows, z in zip(halves, zs):
        out = z[:, 0:D] * jax.nn.sigmoid(z[:, D:2 * D])
        x = jnp.where(is_ctx, xc_ref[rows, :], xd_ref[rows, :])
        x1 = _layer_norm(ALPHA * x + mrow[:, 2 * D:3 * D] * out, g_ref[...], b_ref[...])
        x1_ref[rows, :] = x1
        h_ref[rows, :], gates_ref[rows, :] = _route(x1, mrow, rw_ref, rb_ref)


def _s5_out(xc, xd, uc, ud, yc, yd, mods1, d_skip, w_out_c, ln_g, ln_b, rw, rb):
    row_spec = lambda w: pl.BlockSpec((TOK_TILE, w), lambda i: (i, 0))
    uc_spec, ud_spec = _two_stream_specs(TOK_TILE, D)
    n_ctx = T_CTX // TOK_TILE
    chunks = TOK_TILE // S5_Q
    return pl.pallas_call(
        _s5_out_kernel,
        grid=(T_ALL // TOK_TILE,),
        in_specs=[uc_spec, ud_spec, uc_spec, ud_spec,
                  pl.BlockSpec((S5_NGB, chunks, D), lambda i: (0, jnp.minimum(i, n_ctx - 1), 0)),
                  pl.BlockSpec((S5_NGB, chunks, D), lambda i: (0, jnp.maximum(i - n_ctx, 0), 0)),
                  _full((8, 6 * D), 1), _full((1, D), 1), _full((D, 2 * D), 1), _full((1, D), 1), _full((1, D), 1),
                  _full((2, D, LANE), 1), _full((1, LANE), 1)],
        out_specs=[row_spec(D), row_spec(D), row_spec(LANE)],
        out_shape=[jax.ShapeDtypeStruct((T_ALL, D), f32), jax.ShapeDtypeStruct((T_ALL, D), bf16),
                   jax.ShapeDtypeStruct((T_ALL, LANE), f32)],
        scratch_shapes=[pltpu.VMEM((S5_NGB, TOK_TILE, LANE), f32), pltpu.VMEM((D, 2 * D), bf16)],
        compiler_params=_params(1),
        name="s5_out",
    )(xc, xd, uc, ud, yc, yd, mods1, d_skip, w_out_c, ln_g, ln_b, rw, rb)


def kernel(x_prompt, x_sample, c, cache_attn_k, cache_attn_v, cache_mla_ckv, cache_mla_krope, state_ssm, c_ctx,
           ada_w, ada_b, ln_mix_g, ln_mix_b, ln_ffn_g, ln_ffn_b, w_in_ab, attn_sink, mla_q_norm, mla_kv_norm,
           mla_w_uq, mla_w_ukv, w_out_ab, w_in_c, s5_lam_re, s5_lam_im, s5_log_dt, s5_b_re, s5_b_im, s5_c_re,
           s5_c_im, s5_d, w_out_c, router_w, router_bias, exp_w_gate, exp_w_up, exp_w_down, sh_w_gate, sh_w_up,
           sh_w_down):
    row = lambda v: v.reshape(1, -1)
    xc, xd = x_prompt.reshape(T_CTX, D), x_sample.reshape(T_DEN, D)
    cvec8 = jnp.concatenate([c_ctx[None, :], c, jnp.zeros((8 - 1 - N_DEN_B, D), f32)], axis=0)
    mods = _adaln(cvec8, ada_w, ada_b)

    uq = mla_w_uq[0].reshape(MLA_Q_RANK, MLA_HEADS, MLA_NOPE + MLA_ROPE)
    w_uq_p = jnp.concatenate([uq[:, :, :MLA_NOPE].reshape(MLA_Q_RANK, -1), uq[:, :, MLA_NOPE:].reshape(MLA_Q_RANK, -1)],
                             axis=1).astype(bf16)
    ukv = mla_w_ukv[0].reshape(MLA_KV_RANK, MLA_HEADS, MLA_NOPE + MLA_V)
    w_ukv_p = jnp.concatenate([ukv[:, :, :MLA_NOPE].reshape(MLA_KV_RANK, -1),
                               ukv[:, :, MLA_NOPE:].reshape(MLA_KV_RANK, -1)], axis=1).astype(bf16)
    qa, ka, va, ckv, kr, qm, kvl, kr_ctx_t = _ab_proj(xc, xd, mods[0], w_in_ab[0], row(mla_q_norm[0]), row(mla_kv_norm[0]),
                                            w_uq_p, w_ukv_p)
    w_out_b = w_out_ab[0].astype(bf16)
    g0, b0 = row(ln_mix_g[0]), row(ln_mix_b[0])
    x1c, nk_t, nv_t = _ctx_attn(attn_sink[0], qa, ka, va, qm, kvl, kr, xc, mods[0], w_out_b, g0, b0)
    new_attn_k = jnp.transpose(nk_t, (0, 1, 4, 2, 3))
    new_attn_v = jnp.transpose(nv_t, (0, 1, 4, 2, 3))
    x1d = _den_attn(attn_sink[0], qa, ka, va,
                    jnp.transpose(cache_attn_k[:, 0], (0, 2, 3, 1)), jnp.transpose(cache_attn_v[:, 0], (0, 2, 3, 1)),
                    qm, kvl, kr, cache_mla_ckv[:, 0], cache_mla_krope[:, 0], w_ukv_p, xd, mods[0], w_out_b, g0, b0)
    rw0, rb0 = _router_weights(0, router_w, router_bias)
    x1, h, gates = _router(x1c, x1d, mods[0], rw0, rb0)
    acc = _moe(0, h, gates, mods[0], exp_w_gate, exp_w_up, exp_w_down, sh_w_gate, sh_w_up, sh_w_down)

    w_in_c_b = w_in_c[0].astype(bf16)
    lg0, lb0 = row(ln_ffn_g[0]), row(ln_ffn_b[0])
    n_ctx_tiles = T_CTX // TOK_TILE
    x2c, uc, u2c = _s5_in(x1, acc, lg0, lb0, 0, mods[1], w_in_c_b, N_CTX_B, CTX_L, lambda i: 0)
    x2d, ud, u2d = _s5_in(x1, acc, lg0, lb0, n_ctx_tiles, mods[1], w_in_c_b, N_DEN_B, DEN_L,
                          lambda i: 1 + i // (DEN_L // TOK_TILE))
    gp = S5_G * S5_P
    chan_major_b = lambda t: jnp.transpose(t[0], (0, 3, 1, 2)).reshape(2, S5_CH, gp)
    chan_major_c = lambda t: t[0].reshape(2, S5_G * S5_CH, S5_P)
    h0 = jnp.transpose(state_ssm[:, 0], (1, 2, 0, 3, 4)).reshape(2, 2, N_DEN_B, gp)
    yc, yd, st = _s5_scan(s5_lam_re[0].reshape(2, 1, gp), s5_lam_im[0].reshape(2, 1, gp),
                          jnp.repeat(s5_log_dt[0], S5_P, axis=-1).reshape(2, 1, gp),
                          chan_major_b(s5_b_re), chan_major_b(s5_b_im), chan_major_c(s5_c_re), chan_major_c(s5_c_im),
                          u2c, u2d, h0)
    rw1, rb1 = _router_weights(1, router_w, router_bias)
    x3, h, gates = _s5_out(x2c, x2d, uc, ud, yc, yd, mods[1], row(s5_d[0]),
                           w_out_c[0], row(ln_mix_g[1]), row(ln_mix_b[1]), rw1, rb1)
    acc = _moe(1, h, gates, mods[1], exp_w_gate, exp_w_up, exp_w_down, sh_w_gate, sh_w_up, sh_w_down)
    lg1, lb1 = row(ln_ffn_g[1]), row(ln_ffn_b[1])
    y_prompt = _moe_finish(x3, acc, 0, T_CTX, lg1, lb1).reshape(N_CTX_B, CTX_L, D)
    y_sample = _moe_finish(x3, acc, n_ctx_tiles, T_DEN, lg1, lb1).reshape(N_DEN_B, DEN_L, D)
    new_mla_ckv = ckv.reshape(N_CTX_B, 1, CTX_L, MLA_KV_RANK)
    new_mla_krope = jnp.transpose(kr_ctx_t, (0, 1, 3, 2))
    new_state_ssm = jnp.transpose(st, (2, 0, 1, 3)).reshape(N_CTX_B, 1, 2, 2, S5_G, S5_P)
    return (y_prompt, y_sample, new_attn_k, new_attn_v, new_mla_ckv, new_mla_krope, new_state_ssm)
```

```python
import functools

import jax
import jax.numpy as jnp
import numpy as np
from jax import lax
from jax.experimental import pallas as pl
from jax.experimental.pallas import tpu as pltpu

f32 = jnp.float32
bf16 = jnp.bfloat16

D = 1024
N_CTX_B, CTX_L = 16, 256
N_DEN_B, DEN_L = 2, 1024
T_CTX = N_CTX_B * CTX_L
T_DEN = N_DEN_B * DEN_L
T_ALL = T_CTX + T_DEN
GRID_W = 64
WINDOW = 128
ROPE_BASE = 10000.0
A_HEADS, A_KV_HEADS, A_HD = 8, 2, 64
A_GROUP = A_HEADS // A_KV_HEADS
A_SCALE = A_HD ** -0.5
MLA_HEADS, MLA_Q_RANK, MLA_KV_RANK = 8, 256, 128
MLA_NOPE, MLA_ROPE, MLA_V = 64, 32, 64
MLA_SCALE = (MLA_NOPE + MLA_ROPE) ** -0.5
N_EXPERTS, TOP_K, EXPERT_FF, SHARED_FF = 64, 6, 128, 128
ROUTED_SCALE = 2.5
DEPTH = 2
ALPHA = (2.0 * DEPTH) ** 0.25
LN_EPS = 1e-5
RMS_EPS = 1e-6
NEG_INF = -1e30
S5_G, S5_CH, S5_P = 64, 16, 64

LANE = 128
SUB = 8
VMEM_LIMIT = 56 * 1024 * 1024

TOK_TILE = 512


def _mod_row(tile_idx, tile_rows):
    start = tile_idx * tile_rows
    return jnp.where(start < T_CTX, 0, 1 + (start - T_CTX) // DEN_L)


def _layer_norm(y, g, b):
    mu = jnp.mean(y, axis=-1, keepdims=True)
    yc = y - mu
    var = jnp.mean(yc * yc, axis=-1, keepdims=True)
    return yc * lax.rsqrt(var + LN_EPS) * g + b


def _silu(x):
    return x * jax.nn.sigmoid(x)


def _dot(a, b):
    return jnp.dot(a, b, preferred_element_type=f32)


def _dot_nt(a, b):
    return lax.dot_general(a, b, (((1,), (1,)), ((), ())), preferred_element_type=f32)


def _split_bf16(a):
    hi = a.astype(bf16)
    return hi, (a - hi.astype(f32)).astype(bf16)


def _full(shape, n_grid):
    zeros = tuple(0 for _ in shape)
    return pl.BlockSpec(shape, lambda *_: zeros)


def _two_stream_specs(tile_rows, width):
    n_ctx = T_CTX // tile_rows
    return (pl.BlockSpec((tile_rows, width), lambda i: (jnp.minimum(i, n_ctx - 1), 0)),
            pl.BlockSpec((tile_rows, width), lambda i: (jnp.maximum(i - n_ctx, 0), 0)))


def _pick(i, tile_rows, ctx_ref, den_ref):
    return jnp.where(i < T_CTX // tile_rows, ctx_ref[...], den_ref[...])


def _params(n_grid):
    return pltpu.CompilerParams(dimension_semantics=("arbitrary",) * n_grid, vmem_limit_bytes=VMEM_LIMIT)


ADA_TN = 1536


def _adaln_kernel(c_ref, w_ref, b_ref, *o_refs):
    s_hi, s_lo = _split_bf16(_silu(c_ref[...]))
    w_hi, w_lo = _split_bf16(w_ref[0])
    val = _dot(s_hi, w_hi) + (_dot(s_hi, w_lo) + _dot(s_lo, w_hi)) + b_ref[0]
    for k, o_ref in enumerate(o_refs):
        @pl.when(pl.program_id(0) == k)
        def _():
            o_ref[...] = val


def _adaln(cvec8, ada_w, ada_b):
    n = 6 * D
    nb = n // ADA_TN
    layer_spec = lambda k: pl.BlockSpec((8, ADA_TN), lambda l, j: (0, jnp.clip((l - k) * nb + j, 0, nb - 1)))
    return pl.pallas_call(
        _adaln_kernel,
        grid=(DEPTH, n // ADA_TN),
        in_specs=[
            pl.BlockSpec((8, D), lambda l, j: (0, 0)),
            pl.BlockSpec((1, D, ADA_TN), lambda l, j: (l, 0, j)),
            pl.BlockSpec((1, 1, ADA_TN), lambda l, j: (l, 0, j)),
        ],
        out_specs=[layer_spec(k) for k in range(DEPTH)],
        out_shape=[jax.ShapeDtypeStruct((8, n), f32) for _ in range(DEPTH)],
        compiler_params=_params(2),
        name="adaln",
    )(cvec8, ada_w, ada_b.reshape(DEPTH, 1, n))


def _rope_table_array(head_dim):
    q = head_dim // 4
    pos = np.arange(DEN_L)
    row, col = (pos // GRID_W).astype(np.float64), (pos % GRID_W).astype(np.float64)
    lane = np.arange(LANE) % head_dim
    is_col = lane >= head_dim // 2
    w = lane % (head_dim // 2)
    first = w < q
    inv_freq = ROPE_BASE ** (-np.arange(q, dtype=np.float64) / q)
    ang = np.where(is_col[None, :], col[:, None], row[:, None]) * inv_freq[w % q][None, :]
    cos, sin = np.cos(ang), np.sin(ang)
    sin_a = np.where(first[None, :], -sin, 0.0)
    sin_b = np.where(first[None, :], 0.0, sin)
    ident = np.stack([np.ones((TOK_TILE, LANE)), np.zeros((TOK_TILE, LANE)), np.zeros((TOK_TILE, LANE))])
    tab = np.concatenate([ident, np.stack([cos, sin_a, sin_b])], axis=1).astype(np.float32)
    return jnp.asarray(tab), q


def _rope_chunk(x, tab_ref, q):
    return x * tab_ref[0] + pltpu.roll(x, LANE - q, 1) * tab_ref[1] + pltpu.roll(x, q, 1) * tab_ref[2]


PROJ_W = 1280
C_QA, C_KA, C_VA, C_CQ, C_CKV, C_KR = 0, 512, 640, 768, 1024, 1152
MLA_NN = MLA_HEADS * MLA_NOPE


def _ab_proj_kernel(xc_ref, xd_ref, mods_ref, w_ref, qn_ref, kvn_ref, wuq_ref, wukv_ref, ta_ref, tm_ref,
                    qa_ref, ka_ref, va_ref, ckv_ref, kr_ref, qm_ref, kvl_ref, krc_ref, wb_ref, *, qa_shift, qm_shift):
    i = pl.program_id(0)

    @pl.when(i == 0)
    def _():
        n_w = w_ref.shape[1]
        wb_ref[:, PROJ_W - LANE:PROJ_W] = jnp.zeros((D, LANE), bf16)
        wb_ref[:, 0:n_w] = w_ref[...].astype(bf16)

    r = _mod_row(i, TOK_TILE)
    mrow = mods_ref[pl.ds(r, 1), :]
    sh, sc = mrow[:, 0:D], mrow[:, D:2 * D]
    x = _pick(i, TOK_TILE, xc_ref, xd_ref)
    h = (x * (1.0 + sc) + sh).astype(bf16)
    proj = _dot(h, wb_ref[...])
    for j in range(4):
        c0 = C_QA + LANE * j
        qa_ref[:, LANE * j:LANE * (j + 1)] = _rope_chunk(proj[:, c0:c0 + LANE], ta_ref, qa_shift).astype(bf16)
    ka_ref[...] = _rope_chunk(proj[:, C_KA:C_KA + LANE], ta_ref, qa_shift)
    va_ref[...] = proj[:, C_VA:C_VA + LANE]
    cq = proj[:, C_CQ:C_CQ + MLA_Q_RANK]
    cq = cq * lax.rsqrt(jnp.mean(cq * cq, axis=-1, keepdims=True) + RMS_EPS) * qn_ref[...]
    ckv = proj[:, C_CKV:C_CKV + MLA_KV_RANK]
    ckv = ckv * lax.rsqrt(jnp.mean(ckv * ckv, axis=-1, keepdims=True) + RMS_EPS) * kvn_ref[...]
    kr = _rope_chunk(proj[:, C_KR:C_KR + LANE], tm_ref, qm_shift)
    kr_ref[...] = kr

    qm = _dot(cq.astype(bf16), wuq_ref[...])
    qm_ref[:, 0:MLA_NN] = qm[:, 0:MLA_NN].astype(bf16)
    for j in range(2):
        c0 = MLA_NN + LANE * j
        qm_ref[:, c0:c0 + LANE] = _rope_chunk(qm[:, c0:c0 + LANE], tm_ref, qm_shift).astype(bf16)
    kvl_ref[...] = _dot(ckv.astype(bf16), wukv_ref[...]).astype(bf16)

    @pl.when(i < T_CTX // TOK_TILE)
    def _():
        ckv_ref[...] = ckv
        kr_t = kr_ref[...].T
        for b in range(TOK_TILE // CTX_L):
            krc_ref[b, 0] = kr_t[0:MLA_ROPE, CTX_L * b:CTX_L * (b + 1)]


def _rope_block_index(i):
    tiles_ctx = T_CTX // TOK_TILE
    per_seq = DEN_L // TOK_TILE
    return jnp.where(i < tiles_ctx, 0, 1 + (i - tiles_ctx) % per_seq)


def _ab_proj(xc, xd, mods0, w_in, q_norm, kv_norm, w_uq_p, w_ukv_p):
    tab_a, qa_shift = _rope_table_array(A_HD)
    tab_m, qm_shift = _rope_table_array(MLA_ROPE)
    row_spec = lambda w: pl.BlockSpec((TOK_TILE, w), lambda i: (i, 0))
    xc_spec, xd_spec = _two_stream_specs(TOK_TILE, D)
    tab_spec = pl.BlockSpec((3, TOK_TILE, LANE), lambda i: (0, _rope_block_index(i), 0))
    outs = [(512, bf16), (LANE, f32), (LANE, f32), (LANE, f32), (LANE, f32), (768, bf16), (1024, bf16)]
    last_ctx = T_CTX // TOK_TILE - 1
    per = TOK_TILE // CTX_L
    ctx_spec = pl.BlockSpec((TOK_TILE, LANE), lambda i: (jnp.minimum(i, last_ctx), 0))
    krc_spec = pl.BlockSpec((per, 1, MLA_ROPE, CTX_L), lambda i: (jnp.minimum(i, last_ctx), 0, 0, 0))
    return pl.pallas_call(
        functools.partial(_ab_proj_kernel, qa_shift=qa_shift, qm_shift=qm_shift),
        grid=(T_ALL // TOK_TILE,),
        in_specs=[xc_spec, xd_spec, _full((8, 6 * D), 1), _full(w_in.shape, 1), _full((1, MLA_Q_RANK), 1),
                  _full((1, MLA_KV_RANK), 1), _full((MLA_Q_RANK, 768), 1), _full((MLA_KV_RANK, 1024), 1),
                  tab_spec, tab_spec],
        out_specs=[ctx_spec if k == 3 else row_spec(w) for k, (w, _) in enumerate(outs)] + [krc_spec],
        out_shape=[jax.ShapeDtypeStruct((T_CTX if k == 3 else T_ALL, w), dt) for k, (w, dt) in enumerate(outs)]
        + [jax.ShapeDtypeStruct((N_CTX_B, 1, MLA_ROPE, CTX_L), f32)],
        scratch_shapes=[pltpu.VMEM((D, PROJ_W), bf16)],
        compiler_params=_params(1),
        name="ab_proj",
    )(xc, xd, mods0, w_in, q_norm, kv_norm, w_uq_p, w_ukv_p, tab_a, tab_m)


def _softmax_blocks(s_refs, p_refs, sink_col=None):
    m = s_refs[0][...].max(axis=-1, keepdims=True)
    for s_ref in s_refs[1:]:
        m = jnp.maximum(m, s_ref[...].max(axis=-1, keepdims=True))
    if sink_col is not None:
        m = jnp.maximum(m, sink_col)
    for s_ref, p_ref in zip(s_refs, p_refs):
        p_ref[...] = jnp.exp(s_ref[...] - m).astype(bf16)
    return 0.0 if sink_col is None else jnp.exp(sink_col - m)


def _with_ones(v, axis=1):
    return jnp.concatenate([v, jnp.ones(v.shape, v.dtype)], axis=axis)


def _normalise(o_aug, extra, width):
    return o_aug[:, 0:width] * (1.0 / (o_aug[:, width:width + 1] + extra))


def _sink_column(sink_ref, rows_per_head):
    return jnp.concatenate([jnp.full((rows_per_head, 1), sink_ref[h], f32) for h in range(A_HEADS)], axis=0)


def _mla_q(qm_ref, h):
    rows = qm_ref.shape[0]
    return jnp.concatenate([qm_ref[:, MLA_NOPE * h:MLA_NOPE * (h + 1)],
                            qm_ref[:, MLA_NN + MLA_ROPE * h:MLA_NN + MLA_ROPE * (h + 1)],
                            jnp.zeros((rows, LANE - MLA_NOPE - MLA_ROPE), bf16)], axis=1)


def _mla_k(k_nope_h, k_rope):
    rows = k_nope_h.shape[0]
    return jnp.concatenate([k_nope_h, k_rope, jnp.zeros((rows, LANE - MLA_NOPE - MLA_ROPE), bf16)], axis=1)


def _mix_out_ln(merged_ref, wout_ref, x, mods_ref, r, g_ref, b_ref):
    out = _dot(merged_ref[...], wout_ref[...])
    gate = mods_ref[pl.ds(r, 1), 2 * D:3 * D]
    return _layer_norm(ALPHA * x + gate * out, g_ref[...], b_ref[...])


def _ctx_attn_kernel(sink_ref, qa_ref, ka_ref, va_ref, qm_ref, kvl_ref, kr_ref, x_ref, mods_ref, wout_ref,
                     g_ref, b_ref, o_ref, nk_ref, nv_ref, merged_ref, sa_ref, sm_ref, pa_ref, pm_ref):
    nk_ref[0, 0] = ka_ref[...].T.reshape(A_KV_HEADS, A_HD, CTX_L)
    nv_ref[0, 0] = va_ref[...].T.reshape(A_KV_HEADS, A_HD, CTX_L)
    n = CTX_L
    ka = ka_ref[...].astype(bf16)
    va = va_ref[...].astype(bf16)
    for j in range(A_KV_HEADS):
        q4 = jnp.concatenate([qa_ref[:, A_HD * h:A_HD * (h + 1)] for h in range(A_GROUP * j, A_GROUP * (j + 1))],
                             axis=0)
        sa_ref[A_GROUP * n * j:A_GROUP * n * (j + 1), :] = _dot_nt(q4, ka[:, A_HD * j:A_HD * (j + 1)]) * A_SCALE
    kr = kr_ref[:, 0:MLA_ROPE].astype(bf16)
    for h in range(MLA_HEADS):
        k_cat = _mla_k(kvl_ref[:, MLA_NOPE * h:MLA_NOPE * (h + 1)], kr)
        sm_ref[n * h:n * (h + 1), :] = _dot_nt(_mla_q(qm_ref, h), k_cat) * MLA_SCALE
    sink_a = _softmax_blocks([sa_ref], [pa_ref], _sink_column(sink_ref, n))
    _softmax_blocks([sm_ref], [pm_ref])
    for j in range(A_KV_HEADS):
        rows = slice(A_GROUP * n * j, A_GROUP * n * (j + 1))
        o4 = _normalise(_dot(pa_ref[rows, :], _with_ones(va[:, A_HD * j:A_HD * (j + 1)])), sink_a[rows], A_HD)
        for g in range(A_GROUP):
            h = A_GROUP * j + g
            merged_ref[:, A_HD * h:A_HD * (h + 1)] = o4[n * g:n * (g + 1)].astype(bf16)
    for h in range(MLA_HEADS):
        rows = slice(n * h, n * (h + 1))
        v = _with_ones(kvl_ref[:, MLA_NN + MLA_V * h:MLA_NN + MLA_V * (h + 1)])
        merged_ref[:, MLA_NN + MLA_V * h:MLA_NN + MLA_V * (h + 1)] = (
            _normalise(_dot(pm_ref[rows, :], v), 0.0, MLA_V).astype(bf16))
    o_ref[...] = _mix_out_ln(merged_ref, wout_ref, x_ref[...], mods_ref, 0, g_ref, b_ref)


def _ctx_attn(sink, qa, ka, va, qm, kvl, kr, x_all, mods0, w_out, ln_g, ln_b):
    blk = lambda w: pl.BlockSpec((CTX_L, w), lambda b: (b, 0))
    cache_blk = pl.BlockSpec((1, 1, A_KV_HEADS, A_HD, CTX_L), lambda b: (b, 0, 0, 0, 0))
    cache_shape = jax.ShapeDtypeStruct((N_CTX_B, 1, A_KV_HEADS, A_HD, CTX_L), f32)
    return pl.pallas_call(
        _ctx_attn_kernel,
        grid=(N_CTX_B,),
        in_specs=[pl.BlockSpec(memory_space=pltpu.SMEM), blk(512), blk(LANE), blk(LANE), blk(768), blk(1024),
                  blk(LANE), blk(D), _full((8, 6 * D), 1), _full((D, D), 1), _full((1, D), 1), _full((1, D), 1)],
        out_specs=[blk(D), cache_blk, cache_blk],
        out_shape=[jax.ShapeDtypeStruct((T_CTX, D), f32), cache_shape, cache_shape],
        scratch_shapes=[pltpu.VMEM((CTX_L, D), bf16),
                        pltpu.VMEM((A_HEADS * CTX_L, CTX_L), f32), pltpu.VMEM((MLA_HEADS * CTX_L, CTX_L), f32),
                        pltpu.VMEM((A_HEADS * CTX_L, CTX_L), bf16), pltpu.VMEM((MLA_HEADS * CTX_L, CTX_L), bf16)],
        compiler_params=_params(1),
        name="ctx_attn",
    )(sink, qa, ka, va, qm, kvl, kr, x_all, mods0, w_out, ln_g, ln_b)


QB = 256
WIN = QB + 2 * WINDOW
DEN_BLK0 = T_CTX // DEN_L
MLA_KEYS = CTX_L + DEN_L


def _den_attn_kernel(sink_ref, qa_ref, ka_ref, va_ref, cak_ref, cav_ref, qm_ref, kvl_ref, kr_ref, cckv_ref, ckr_ref,
                     wukv_ref, x_ref, mods_ref, wout_ref, g_ref, b_ref, o_ref, merged_ref, kcat_ref, vcat_ref,
                     saw_ref, sac_ref, sm_ref, paw_ref, pac_ref, pm_ref):
    b = pl.program_id(0)
    n = pl.program_id(1)

    @pl.when(n == 0)
    def _():
        kvc = _dot(cckv_ref[0].astype(bf16), wukv_ref[...]).astype(bf16)
        kr_ctx = ckr_ref[0].astype(bf16)
        kr_lat = kr_ref[:, 0:MLA_ROPE].astype(bf16)
        for h in range(MLA_HEADS):
            ns = slice(MLA_NOPE * h, MLA_NOPE * (h + 1))
            vs = slice(MLA_NN + MLA_V * h, MLA_NN + MLA_V * (h + 1))
            kcat_ref[h, 0:CTX_L, :] = _mla_k(kvc[:, ns], kr_ctx)
            kcat_ref[h, CTX_L:MLA_KEYS, :] = _mla_k(kvl_ref[:, ns], kr_lat)
            vcat_ref[h, 0:CTX_L, :] = _with_ones(kvc[:, vs])
            vcat_ref[h, CTX_L:MLA_KEYS, :] = _with_ones(kvl_ref[:, vs])

    start = pl.multiple_of(jnp.clip(QB * n - WINDOW, 0, DEN_L - WIN), WINDOW)
    grp_rows = A_GROUP * QB
    qpos = QB * n + (lax.broadcasted_iota(jnp.int32, (grp_rows, WIN), 0) & (QB - 1))
    kpos = start + lax.broadcasted_iota(jnp.int32, (grp_rows, WIN), 1)
    valid = jnp.abs(qpos - kpos) <= WINDOW
    kwin = ka_ref[pl.ds(start, WIN), :].astype(bf16)
    vwin = va_ref[pl.ds(start, WIN), :].astype(bf16)
    kctx_t = [cak_ref[0, j].astype(bf16) for j in range(A_KV_HEADS)]
    vctx_t = [cav_ref[0, j].astype(bf16) for j in range(A_KV_HEADS)]
    for j in range(A_KV_HEADS):
        sl = slice(A_HD * j, A_HD * (j + 1))
        rows = slice(grp_rows * j, grp_rows * (j + 1))
        q4 = jnp.concatenate([qa_ref[:, A_HD * h:A_HD * (h + 1)] for h in range(A_GROUP * j, A_GROUP * (j + 1))],
                             axis=0)
        saw_ref[rows, :] = jnp.where(valid, _dot_nt(q4, kwin[:, sl]) * A_SCALE, NEG_INF)
        sac_ref[rows, :] = _dot(q4, kctx_t[j]) * A_SCALE
    for h in range(MLA_HEADS):
        sm_ref[QB * h:QB * (h + 1), :] = _dot_nt(_mla_q(qm_ref, h), kcat_ref[h]) * MLA_SCALE
    sink_a = _softmax_blocks([saw_ref, sac_ref], [paw_ref, pac_ref], _sink_column(sink_ref, QB))
    _softmax_blocks([sm_ref], [pm_ref])
    for j in range(A_KV_HEADS):
        sl = slice(A_HD * j, A_HD * (j + 1))
        rows = slice(grp_rows * j, grp_rows * (j + 1))
        o_aug = (_dot(paw_ref[rows, :], _with_ones(vwin[:, sl]))
                 + _dot_nt(pac_ref[rows, :], _with_ones(vctx_t[j], axis=0)))
        o4 = _normalise(o_aug, sink_a[rows], A_HD)
        for g in range(A_GROUP):
            h = A_GROUP * j + g
            merged_ref[:, A_HD * h:A_HD * (h + 1)] = o4[QB * g:QB * (g + 1)].astype(bf16)
    for h in range(MLA_HEADS):
        rows = slice(QB * h, QB * (h + 1))
        o = _normalise(_dot(pm_ref[rows, :], vcat_ref[h]), 0.0, MLA_V)
        merged_ref[:, MLA_NN + MLA_V * h:MLA_NN + MLA_V * (h + 1)] = o.astype(bf16)
    o_ref[...] = _mix_out_ln(merged_ref, wout_ref, x_ref[...], mods_ref, 1 + b, g_ref, b_ref)


def _den_attn(sink, qa, ka, va, cache_k, cache_v, qm, kvl, kr, cache_ckv, cache_kr, w_ukv_p, x_all, mods0, w_out,
              ln_g, ln_b):
    nq = DEN_L // QB
    qblk = lambda w: pl.BlockSpec((QB, w), lambda b, n: (T_CTX // QB + b * nq + n, 0))
    seq = lambda w: pl.BlockSpec((DEN_L, w), lambda b, n: (DEN_BLK0 + b, 0))
    cache = lambda w: pl.BlockSpec((1, CTX_L, w), lambda b, n: (b, 0, 0))
    cache_a = pl.BlockSpec((1, A_KV_HEADS, A_HD, CTX_L), lambda b, n: (b, 0, 0, 0))
    return pl.pallas_call(
        _den_attn_kernel,
        grid=(N_DEN_B, nq),
        in_specs=[pl.BlockSpec(memory_space=pltpu.SMEM), qblk(512), seq(LANE), seq(LANE), cache_a, cache_a,
                  qblk(768), seq(1024), seq(LANE), cache(MLA_KV_RANK), cache(MLA_ROPE),
                  _full((MLA_KV_RANK, 1024), 2), pl.BlockSpec((QB, D), lambda b, n: (b * nq + n, 0)),
                  _full((8, 6 * D), 2), _full((D, D), 2), _full((1, D), 2),
                  _full((1, D), 2)],
        out_specs=pl.BlockSpec((QB, D), lambda b, n: (b * nq + n, 0)),
        out_shape=jax.ShapeDtypeStruct((T_DEN, D), f32),
        scratch_shapes=[pltpu.VMEM((QB, D), bf16), pltpu.VMEM((MLA_HEADS, MLA_KEYS, LANE), bf16),
                        pltpu.VMEM((MLA_HEADS, MLA_KEYS, 2 * MLA_V), bf16)]
        + [pltpu.VMEM((A_HEADS * QB, w), dt) for dt in (f32, bf16) for w in (WIN, CTX_L, MLA_KEYS)],
        compiler_params=_params(2),
        name="den_attn",
    )(sink, qa, ka, va, cache_k, cache_v, qm, kvl, kr, cache_ckv, cache_kr, w_ukv_p, x_all, mods0, w_out, ln_g, ln_b)


SUBTILE = 256


def _route(x1, mrow, rw_ref, rb_ref):
    sh, sc = mrow[:, 3 * D:4 * D], mrow[:, 4 * D:5 * D]
    h = x1 * (1.0 + sc) + sh
    h_hi = h.astype(bf16)
    h_lo = (h - h_hi.astype(f32)).astype(bf16)
    logits = _dot(h_hi, rw_ref[0]) + (_dot(h_hi, rw_ref[1]) + _dot(h_lo, rw_ref[0]))
    scores = jax.nn.sigmoid(logits)
    lane = lax.broadcasted_iota(jnp.int32, scores.shape, 1).astype(f32)
    sel = jnp.where(lane < N_EXPERTS, scores + rb_ref[...], -jnp.inf)
    gates = jnp.zeros_like(scores)
    for _ in range(TOP_K):
        m = sel.max(axis=-1, keepdims=True)
        idx = jnp.where(sel == m, lane, float(LANE)).min(axis=-1, keepdims=True)
        hit = lane == idx
        gates = jnp.where(hit, scores, gates)
        sel = jnp.where(hit, -jnp.inf, sel)
    return h_hi, gates / gates.sum(axis=-1, keepdims=True) * ROUTED_SCALE


def _router_kernel(xc_ref, xd_ref, mods_ref, rw_ref, rb_ref, x_ref, h_ref, gates_ref):
    i = pl.program_id(0)
    r = _mod_row(i, TOK_TILE)
    mrow = mods_ref[pl.ds(r, 1), :]
    x1 = _pick(i, TOK_TILE, xc_ref, xd_ref)
    x_ref[...] = x1
    h_ref[...], gates_ref[...] = _route(x1, mrow, rw_ref, rb_ref)


def _router(x1c, x1d, mods_l, router_w_p, router_b_p):
    row_spec = lambda w: pl.BlockSpec((TOK_TILE, w), lambda i: (i, 0))
    xc_spec, xd_spec = _two_stream_specs(TOK_TILE, D)
    return pl.pallas_call(
        _router_kernel,
        grid=(T_ALL // TOK_TILE,),
        in_specs=[xc_spec, xd_spec, _full((8, 6 * D), 1), _full((2, D, LANE), 1), _full((1, LANE), 1)],
        out_specs=[row_spec(D), row_spec(D), row_spec(LANE)],
        out_shape=[jax.ShapeDtypeStruct((T_ALL, D), f32), jax.ShapeDtypeStruct((T_ALL, D), bf16),
                   jax.ShapeDtypeStruct((T_ALL, LANE), f32)],
        compiler_params=_params(1),
        name="router",
    )(x1c, x1d, mods_l, router_w_p, router_b_p)


MOE_TOK = 1536
MOE_EG = 8
MOE_VMEM_LIMIT = 60 * 1024 * 1024
MOE_TILE = 512
MOE_FF = MOE_EG * EXPERT_FF


def _moe_kernel(h_ref, gates_ref, mods_ref, wg_ref, wu_ref, wd_ref, sg_ref, su_ref, sd_ref, o_ref):
    p = pl.program_id(0)
    e = pl.program_id(1)
    n_tiles = MOE_TOK // MOE_TILE

    def gate_f(t):
        r = _mod_row(p * n_tiles + t, MOE_TILE)
        return mods_ref[pl.ds(r, 1), 5 * D:6 * D]

    def rows_of(t):
        if isinstance(t, int):
            return pl.ds(t * MOE_TILE, MOE_TILE)
        return pl.ds(pl.multiple_of(t * MOE_TILE, MOE_TILE), MOE_TILE)

    @pl.when(e == 0)
    def _():
        sg = sg_ref[...].astype(bf16)
        su = su_ref[...].astype(bf16)
        sd = sd_ref[...].astype(bf16)

        def body(t, c):
            rows = rows_of(t)
            ht = h_ref[rows, :]
            hid = _silu(_dot(ht, sg)) * _dot(ht, su)
            o_ref[rows, :] = gate_f(t) * _dot(hid.astype(bf16), sd)
            return c

        lax.fori_loop(0, n_tiles, body, 0)

    wg = jnp.concatenate([wg_ref[k].astype(bf16) for k in range(MOE_EG)], axis=1)
    wu = jnp.concatenate([wu_ref[k].astype(bf16) for k in range(MOE_EG)], axis=1)
    wd = jnp.concatenate([wd_ref[k].astype(bf16) for k in range(MOE_EG)], axis=0)
    lane = lax.broadcasted_iota(jnp.int32, (MOE_TILE, LANE), 1)

    def body(t, c):
        rows = rows_of(t)
        ht = h_ref[rows, :]
        hid = _silu(_dot(ht, wg)) * _dot(ht, wu)
        gt = gates_ref[rows, :]
        parts = []
        for k in range(MOE_EG):
            col = jnp.where(lane == e * MOE_EG + k, gt, 0.0).sum(axis=-1, keepdims=True)
            parts.append((hid[:, EXPERT_FF * k:EXPERT_FF * (k + 1)] * col).astype(bf16))
        o_ref[rows, :] += gate_f(t) * _dot(jnp.concatenate(parts, axis=1), wd)
        return c

    for t in range(n_tiles):
        body(t, 0)


def _moe(l, h, gates, mods_l, wg, wu, wd, sg, su, sd):
    tok = lambda w: pl.BlockSpec((MOE_TOK, w), lambda p, e: (p, 0))
    return pl.pallas_call(
        _moe_kernel,
        grid=(T_ALL // MOE_TOK, N_EXPERTS // MOE_EG),
        in_specs=[tok(D), tok(LANE), _full((8, 6 * D), 2),
                  pl.BlockSpec((None, MOE_EG, D, EXPERT_FF), lambda p, e: (l, e, 0, 0)),
                  pl.BlockSpec((None, MOE_EG, D, EXPERT_FF), lambda p, e: (l, e, 0, 0)),
                  pl.BlockSpec((None, MOE_EG, EXPERT_FF, D), lambda p, e: (l, e, 0, 0)),
                  pl.BlockSpec((None, D, SHARED_FF), lambda p, e: (l, 0, 0)),
                  pl.BlockSpec((None, D, SHARED_FF), lambda p, e: (l, 0, 0)),
                  pl.BlockSpec((None, SHARED_FF, D), lambda p, e: (l, 0, 0))],
        out_specs=tok(D),
        out_shape=jax.ShapeDtypeStruct((T_ALL, D), f32),
        compiler_params=pltpu.CompilerParams(dimension_semantics=("arbitrary", "arbitrary"),
                                             vmem_limit_bytes=MOE_VMEM_LIMIT),
        name="moe",
    )(h, gates, mods_l, wg, wu, wd, sg, su, sd)


def _moe_finish_kernel(x_ref, acc_ref, g_ref, b_ref, o_ref):
    o_ref[...] = _layer_norm(ALPHA * x_ref[...] + acc_ref[...], g_ref[...], b_ref[...])


def _moe_finish(x1, acc, tile0, n_rows, ln_g, ln_b):
    src = pl.BlockSpec((TOK_TILE, D), lambda i: (tile0 + i, 0))
    return pl.pallas_call(
        _moe_finish_kernel,
        grid=(n_rows // TOK_TILE,),
        in_specs=[src, src, _full((1, D), 1), _full((1, D), 1)],
        out_specs=pl.BlockSpec((TOK_TILE, D), lambda i: (i, 0)),
        out_shape=jax.ShapeDtypeStruct((n_rows, D), f32),
        compiler_params=_params(1),
        name="moe_finish",
    )(x1, acc, ln_g, ln_b)


def _router_weights(l, router_w, router_bias):
    rw = jnp.pad(router_w[l], ((0, 0), (0, LANE - N_EXPERTS)))
    rw_hi, rw_lo = _split_bf16(rw)
    rb = jnp.pad(router_bias[l], (0, LANE - N_EXPERTS)).reshape(1, LANE)
    return jnp.stack([rw_hi, rw_lo]), rb


S5_Q = 8
S5_NGB = D // LANE


def _s5_in_kernel(x_ref, acc_ref, lg_ref, lb_ref, mods_ref, w_ref, x2_ref, u_ref, u2_ref, slab_ref, *, row_of,
                  seq_len):
    r = row_of(pl.program_id(0))
    mrow = mods_ref[pl.ds(r, 1), :]
    sh, sc = mrow[:, 0:D], mrow[:, D:2 * D]
    x2 = _layer_norm(ALPHA * x_ref[...] + acc_ref[...], lg_ref[...], lb_ref[...])
    x2_ref[...] = x2
    h = (x2 * (1.0 + sc) + sh).astype(bf16)
    u = _dot(h, w_ref[...])
    u_ref[...] = u
    for s in range(S5_NGB):
        slab_ref[s] = u[:, LANE * s:LANE * (s + 1)]
    kt = seq_len // S5_Q
    for s in range(S5_NGB):
        for q in range(TOK_TILE // seq_len):
            for j in range(S5_Q):
                u2_ref[s, q * kt:(q + 1) * kt, LANE * j:LANE * (j + 1)] = (
                    slab_ref[s, pl.ds(q * seq_len + j, kt, stride=S5_Q), :].astype(bf16))


def _s5_in(x1, acc, ln_g, ln_b, tile0, mods1, w_in_c, n_b, seq_len, row_of):
    n_tiles = n_b * seq_len // TOK_TILE
    chunks = TOK_TILE // S5_Q
    src = pl.BlockSpec((TOK_TILE, D), lambda i: (tile0 + i, 0))
    dst = pl.BlockSpec((TOK_TILE, D), lambda i: (i, 0))
    return pl.pallas_call(
        functools.partial(_s5_in_kernel, row_of=row_of, seq_len=min(seq_len, TOK_TILE)),
        grid=(n_tiles,),
        in_specs=[src, src, _full((1, D), 1), _full((1, D), 1), _full((8, 6 * D), 1), _full((D, D), 1)],
        out_specs=[dst, dst, pl.BlockSpec((S5_NGB, chunks, D), lambda i: (0, i, 0))],
        out_shape=[jax.ShapeDtypeStruct((n_b * seq_len, D), f32), jax.ShapeDtypeStruct((n_b * seq_len, D), f32),
                   jax.ShapeDtypeStruct((S5_NGB, n_tiles * chunks, D), bf16)],
        scratch_shapes=[pltpu.VMEM((S5_NGB, TOK_TILE, LANE), f32)],
        compiler_params=_params(1),
        name="s5_in",
    )(x1, acc, ln_g, ln_b, mods1, w_in_c)


S5_GL = (LANE // S5_CH) * S5_P
S5_ROWS_C = (CTX_L // S5_Q) * N_CTX_B
S5_ROWS_D = (DEN_L // S5_Q) * N_DEN_B


def _s5_scan_kernel(lre_ref, lim_ref, ldt_ref, btr_ref, bti_ref, ctr_ref, cti_ref, uc_ref, ud_ref, h0_ref,
                    yc_ref, yd_ref, st_ref, win_ref, mso_ref, wit_ref, a_ref, s_ref, hp_ref):
    gl = S5_GL
    rowg = lax.shift_right_logical(lax.broadcasted_iota(jnp.int32, (LANE, gl), 0), 4)
    colg = lax.shift_right_logical(lax.broadcasted_iota(jnp.int32, (LANE, gl), 1), 6)
    same_group = rowg == colg
    reps = LANE // S5_CH

    def expand(t):
        return jnp.where(same_group, jnp.concatenate([t] * reps, axis=0), 0.0)

    def expand_c(t):
        return jnp.where(same_group, jnp.concatenate([t] * reps, axis=1), 0.0)

    for d in range(2):
        fwd = d == 0
        lre, lim = lre_ref[d], lim_ref[d]
        dt = jnp.exp(ldt_ref[d])
        a, w = lre * dt, lim * dt
        pre = [jnp.exp(m * a) * jnp.cos(m * w) for m in range(S5_Q + 1)]
        pim = [jnp.exp(m * a) * jnp.sin(m * w) for m in range(S5_Q + 1)]
        xr, xi = pre[1] - 1.0, pim[1]
        den = lre * lre + lim * lim
        cfr, cfi = (xr * lre + xi * lim) / den, (xi * lre - xr * lim) / den
        btr, bti = btr_ref[d], bti_ref[d]
        bexp_r = expand(cfr * btr - cfi * bti)
        bexp_i = expand(cfr * bti + cfi * btr)
        cexp_r, cexp_i = expand_c(ctr_ref[d]), expand_c(cti_ref[d])
        for m in range(S5_Q + 1):
            a_ref[m, :, 0:gl] = cexp_r * pre[m] - cexp_i * pim[m]
            a_ref[m, :, gl:2 * gl] = -(cexp_r * pim[m] + cexp_i * pre[m])
        for j in range(S5_Q):
            m = S5_Q - 1 - j if fwd else j
            win_ref[LANE * j:LANE * (j + 1), 0:gl] = (pre[m] * bexp_r - pim[m] * bexp_i).astype(bf16)
            win_ref[LANE * j:LANE * (j + 1), gl:2 * gl] = (pre[m] * bexp_i + pim[m] * bexp_r).astype(bf16)
        for j in range(S5_Q):
            m = j + 1 if fwd else S5_Q - j
            mso_ref[LANE * j:LANE * (j + 1), :] = a_ref[m].astype(bf16)
        b2 = jnp.concatenate([bexp_r, bexp_i], axis=1).astype(bf16)
        kt = [_dot_nt(b2, a_ref[tau].astype(bf16)) for tau in range(S5_Q)]
        for j in range(S5_Q):
            for jp in range(S5_Q):
                tau = jp - j if fwd else j - jp
                blk = slice(LANE * j, LANE * (j + 1)), slice(LANE * jp, LANE * (jp + 1))
                if fwd:
                    wit_ref[blk] = kt[tau] if tau >= 0 else jnp.zeros((LANE, LANE), f32)
                elif tau >= 0:
                    wit_ref[blk] = wit_ref[blk] + kt[tau]

        l8r, l8i = pre[S5_Q], pim[S5_Q]

        nsl = gl // LANE

        def slabs(ref, rs, first):
            return jnp.concatenate([ref[first + sl, rs, :] for sl in range(nsl)], axis=1)

        def put_slabs(ref, rs, first, val):
            for sl in range(nsl):
                ref[first + sl, rs, :] = val[:, LANE * sl:LANE * (sl + 1)]

        def advance(hr, hi_, sr, si):
            return l8r * hr - l8i * hi_ + sr, l8r * hi_ + l8i * hr + si

        def run(u_ref, y_ref, n_b, n_k, h_init):
            rows = n_b * n_k
            s = _dot(u_ref[0], win_ref[...])
            if n_b % SUB == 0:
                pitch = n_k + 1
                for bb in range(n_b):
                    dst = slice(bb * pitch, bb * pitch + n_k)
                    put_slabs(s_ref, dst, 0, s[bb * n_k:(bb + 1) * n_k, 0:gl])
                    put_slabs(s_ref, dst, nsl, s[bb * n_k:(bb + 1) * n_k, gl:2 * gl])

                def step(i, carry):
                    hr, hi_ = carry
                    rs = pl.ds(i if fwd else n_k - 1 - i, n_b, stride=pitch)
                    put_slabs(hp_ref, rs, 0, hr)
                    put_slabs(hp_ref, rs, nsl, hi_)
                    return advance(hr, hi_, slabs(s_ref, rs, 0), slabs(s_ref, rs, nsl))

                h_fin = lax.fori_loop(0, n_k, step, h_init)
                hp = jnp.concatenate(
                    [jnp.concatenate([hp_ref[sl, bb * pitch:bb * pitch + n_k, :] for sl in range(2 * nsl)], axis=1)
                     for bb in range(n_b)], axis=0).astype(bf16)
            else:
                put_slabs(s_ref, slice(0, rows), 0, s[:, 0:gl])
                put_slabs(s_ref, slice(0, rows), nsl, s[:, gl:2 * gl])
                n_it = n_k // SUB

                def step(i, carry):
                    it = i if fwd else n_it - 1 - i
                    out = []
                    for bb in range(n_b):
                        hr, hi_ = carry[bb]
                        rs = pl.ds(pl.multiple_of(bb * n_k + it * SUB, SUB), SUB)
                        s_re, s_im = slabs(s_ref, rs, 0), slabs(s_ref, rs, nsl)
                        prev_r, prev_i = [None] * SUB, [None] * SUB
                        for sub in (range(SUB) if fwd else reversed(range(SUB))):
                            prev_r[sub], prev_i[sub] = hr, hi_
                            hr, hi_ = advance(hr, hi_, s_re[sub:sub + 1], s_im[sub:sub + 1])
                        put_slabs(hp_ref, rs, 0, jnp.concatenate(prev_r, axis=0))
                        put_slabs(hp_ref, rs, nsl, jnp.concatenate(prev_i, axis=0))
                        out.append((hr, hi_))
                    return tuple(out)

                fin = lax.fori_loop(0, n_it, step, tuple((h_init[0][bb:bb + 1], h_init[1][bb:bb + 1])
                                                          for bb in range(n_b)))
                h_fin = (jnp.concatenate([f[0] for f in fin], axis=0), jnp.concatenate([f[1] for f in fin], axis=0))
                hp = jnp.concatenate([hp_ref[sl, 0:rows, :] for sl in range(2 * nsl)], axis=1).astype(bf16)
            y = _dot_nt(hp, mso_ref[...])
            if fwd:
                y_ref[0] = y
            else:
                y_ref[0] += y
            return h_fin

        zeros = jnp.zeros((N_CTX_B, gl), f32)
        hr, hi_ = run(uc_ref, yc_ref, N_CTX_B, CTX_L // S5_Q, (zeros, zeros))
        st_ref[d, 0] = hr
        st_ref[d, 1] = hi_
        run(ud_ref, yd_ref, N_DEN_B, DEN_L // S5_Q, (h0_ref[d, 0], h0_ref[d, 1]))

    wit = wit_ref[...].astype(bf16)
    yc_ref[0] += _dot(uc_ref[0], wit)
    yd_ref[0] += _dot(ud_ref[0], wit)


def _s5_scan(lam_re, lam_im, log_dt, bt_re, bt_im, ct_re, ct_im, u2c, u2d, h0):
    gl = S5_GL
    vec = pl.BlockSpec((2, 1, gl), lambda g: (0, 0, g))
    tab = pl.BlockSpec((2, S5_CH, gl), lambda g: (0, 0, g))
    ctab = pl.BlockSpec((2, LANE, S5_P), lambda g: (0, g, 0))
    rows = lambda n: pl.BlockSpec((1, n, D), lambda g: (g, 0, 0))
    return pl.pallas_call(
        _s5_scan_kernel,
        grid=(S5_NGB,),
        in_specs=[vec, vec, vec, tab, tab, ctab, ctab, rows(S5_ROWS_C), rows(S5_ROWS_D),
                  pl.BlockSpec((2, 2, N_DEN_B, gl), lambda g: (0, 0, 0, g))],
        out_specs=[rows(S5_ROWS_C), rows(S5_ROWS_D), pl.BlockSpec((2, 2, N_CTX_B, gl), lambda g: (0, 0, 0, g))],
        out_shape=[jax.ShapeDtypeStruct((S5_NGB, S5_ROWS_C, D), f32), jax.ShapeDtypeStruct((S5_NGB, S5_ROWS_D, D), f32),
                   jax.ShapeDtypeStruct((2, 2, N_CTX_B, S5_G * S5_P), f32)],
        scratch_shapes=[pltpu.VMEM((D, 2 * gl), bf16), pltpu.VMEM((D, 2 * gl), bf16), pltpu.VMEM((D, D), f32),
                        pltpu.VMEM((S5_Q + 1, LANE, 2 * gl), f32),
                        pltpu.VMEM((2 * gl // LANE, S5_ROWS_C + 2 * N_CTX_B, LANE), f32),
                        pltpu.VMEM((2 * gl // LANE, S5_ROWS_C + 2 * N_CTX_B, LANE), f32)],
        compiler_params=_params(1),
        name="s5_scan",
    )(lam_re, lam_im, log_dt, bt_re, bt_im, ct_re, ct_im, u2c, u2d, h0)


def _gelu_tanh(x):
    return 0.5 * x * (1.0 + jnp.tanh(np.sqrt(2.0 / np.pi).astype(np.float32) * (x + 0.044715 * (x * x * x))))


def _s5_out_kernel(xc_ref, xd_ref, uc_ref, ud_ref, yc_ref, yd_ref, mods_ref, dsk_ref, wout_ref, g_ref, b_ref, rw_ref,
                   rb_ref, x1_ref, h_ref, gates_ref, slab_ref, wob_ref):
    i = pl.program_id(0)

    @pl.when(i == 0)
    def _():
        wob_ref[...] = wout_ref[...].astype(bf16)

    is_ctx = i < T_CTX // TOK_TILE
    r = _mod_row(i, TOK_TILE)
    mrow = mods_ref[pl.ds(r, 1), :]
    u = _pick(i, TOK_TILE, uc_ref, ud_ref)

    kt = CTX_L // S5_Q
    for s in range(S5_NGB):
        for q in range(TOK_TILE // CTX_L):
            for j in range(S5_Q):
                blk = (s, slice(q * kt, (q + 1) * kt), slice(LANE * j, LANE * (j + 1)))
                slab_ref[s, pl.ds(q * CTX_L + j, kt, stride=S5_Q), :] = jnp.where(is_ctx, yc_ref[blk], yd_ref[blk])
    halves = [slice(a, a + SUBTILE) for a in range(0, TOK_TILE, SUBTILE)]
    zs = []
    for rows in halves:
        y = jnp.concatenate([slab_ref[s, rows, :] for s in range(S5_NGB)], axis=1) + dsk_ref[...] * u[rows]
        zs.append(_dot(_gelu_tanh(y).astype(bf16), wob_ref[...]))
    for rows, z in zip(halves, zs):
        out = z[:, 0:D] * jax.nn.sigmoid(z[:, D:2 * D])
        x = jnp.where(is_ctx, xc_ref[rows, :], xd_ref[rows, :])
        x1 = _layer_norm(ALPHA * x + mrow[:, 2 * D:3 * D] * out, g_ref[...], b_ref[...])
        x1_ref[rows, :] = x1
        h_ref[rows, :], gates_ref[rows, :] = _route(x1, mrow, rw_ref, rb_ref)


def _s5_out(xc, xd, uc, ud, yc, yd, mods1, d_skip, w_out_c, ln_g, ln_b, rw, rb):
    row_spec = lambda w: pl.BlockSpec((TOK_TILE, w), lambda i: (i, 0))
    uc_spec, ud_spec = _two_stream_specs(TOK_TILE, D)
    n_ctx = T_CTX // TOK_TILE
    chunks = TOK_TILE // S5_Q
    return pl.pallas_call(
        _s5_out_kernel,
        grid=(T_ALL // TOK_TILE,),
        in_specs=[uc_spec, ud_spec, uc_spec, ud_spec,
                  pl.BlockSpec((S5_NGB, chunks, D), lambda i: (0, jnp.minimum(i, n_ctx - 1), 0)),
                  pl.BlockSpec((S5_NGB, chunks, D), lambda i: (0, jnp.maximum(i - n_ctx, 0), 0)),
                  _full((8, 6 * D), 1), _full((1, D), 1), _full((D, 2 * D), 1), _full((1, D), 1), _full((1, D), 1),
                  _full((2, D, LANE), 1), _full((1, LANE), 1)],
        out_specs=[row_spec(D), row_spec(D), row_spec(LANE)],
        out_shape=[jax.ShapeDtypeStruct((T_ALL, D), f32), jax.ShapeDtypeStruct((T_ALL, D), bf16),
                   jax.ShapeDtypeStruct((T_ALL, LANE), f32)],
        scratch_shapes=[pltpu.VMEM((S5_NGB, TOK_TILE, LANE), f32), pltpu.VMEM((D, 2 * D), bf16)],
        compiler_params=_params(1),
        name="s5_out",
    )(xc, xd, uc, ud, yc, yd, mods1, d_skip, w_out_c, ln_g, ln_b, rw, rb)


def kernel(x_prompt, x_sample, c, cache_attn_k, cache_attn_v, cache_mla_ckv, cache_mla_krope, state_ssm, c_ctx,
           ada_w, ada_b, ln_mix_g, ln_mix_b, ln_ffn_g, ln_ffn_b, w_in_ab, attn_sink, mla_q_norm, mla_kv_norm,
           mla_w_uq, mla_w_ukv, w_out_ab, w_in_c, s5_lam_re, s5_lam_im, s5_log_dt, s5_b_re, s5_b_im, s5_c_re,
           s5_c_im, s5_d, w_out_c, router_w, router_bias, exp_w_gate, exp_w_up, exp_w_down, sh_w_gate, sh_w_up,
           sh_w_down):
    row = lambda v: v.reshape(1, -1)
    xc, xd = x_prompt.reshape(T_CTX, D), x_sample.reshape(T_DEN, D)
    cvec8 = jnp.concatenate([c_ctx[None, :], c, jnp.zeros((8 - 1 - N_DEN_B, D), f32)], axis=0)
    mods = _adaln(cvec8, ada_w, ada_b)

    uq = mla_w_uq[0].reshape(MLA_Q_RANK, MLA_HEADS, MLA_NOPE + MLA_ROPE)
    w_uq_p = jnp.concatenate([uq[:, :, :MLA_NOPE].reshape(MLA_Q_RANK, -1), uq[:, :, MLA_NOPE:].reshape(MLA_Q_RANK, -1)],
                             axis=1).astype(bf16)
    ukv = mla_w_ukv[0].reshape(MLA_KV_RANK, MLA_HEADS, MLA_NOPE + MLA_V)
    w_ukv_p = jnp.concatenate([ukv[:, :, :MLA_NOPE].reshape(MLA_KV_RANK, -1),
                               ukv[:, :, MLA_NOPE:].reshape(MLA_KV_RANK, -1)], axis=1).astype(bf16)
    qa, ka, va, ckv, kr, qm, kvl, kr_ctx_t = _ab_proj(xc, xd, mods[0], w_in_ab[0], row(mla_q_norm[0]), row(mla_kv_norm[0]),
                                            w_uq_p, w_ukv_p)
    w_out_b = w_out_ab[0].astype(bf16)
    g0, b0 = row(ln_mix_g[0]), row(ln_mix_b[0])
    x1c, nk_t, nv_t = _ctx_attn(attn_sink[0], qa, ka, va, qm, kvl, kr, xc, mods[0], w_out_b, g0, b0)
    new_attn_k = jnp.transpose(nk_t, (0, 1, 4, 2, 3))
    new_attn_v = jnp.transpose(nv_t, (0, 1, 4, 2, 3))
    x1d = _den_attn(attn_sink[0], qa, ka, va,
                    jnp.transpose(cache_attn_k[:, 0], (0, 2, 3, 1)), jnp.transpose(cache_attn_v[:, 0], (0, 2, 3, 1)),
                    qm, kvl, kr, cache_mla_ckv[:, 0], cache_mla_krope[:, 0], w_ukv_p, xd, mods[0], w_out_b, g0, b0)
    rw0, rb0 = _router_weights(0, router_w, router_bias)
    x1, h, gates = _router(x1c, x1d, mods[0], rw0, rb0)
    acc = _moe(0, h, gates, mods[0], exp_w_gate, exp_w_up, exp_w_down, sh_w_gate, sh_w_up, sh_w_down)

    w_in_c_b = w_in_c[0].astype(bf16)
    lg0, lb0 = row(ln_ffn_g[0]), row(ln_ffn_b[0])
    n_ctx_tiles = T_CTX // TOK_TILE
    x2c, uc, u2c = _s5_in(x1, acc, lg0, lb0, 0, mods[1], w_in_c_b, N_CTX_B, CTX_L, lambda i: 0)
    x2d, ud, u2d = _s5_in(x1, acc, lg0, lb0, n_ctx_tiles, mods[1], w_in_c_b, N_DEN_B, DEN_L,
                          lambda i: 1 + i // (DEN_L // TOK_TILE))
    gp = S5_G * S5_P
    chan_major_b = lambda t: jnp.transpose(t[0], (0, 3, 1, 2)).reshape(2, S5_CH, gp)
    chan_major_c = lambda t: t[0].reshape(2, S5_G * S5_CH, S5_P)
    h0 = jnp.transpose(state_ssm[:, 0], (1, 2, 0, 3, 4)).reshape(2, 2, N_DEN_B, gp)
    yc, yd, st = _s5_scan(s5_lam_re[0].reshape(2, 1, gp), s5_lam_im[0].reshape(2, 1, gp),
                          jnp.repeat(s5_log_dt[0], S5_P, axis=-1).reshape(2, 1, gp),
                          chan_major_b(s5_b_re), chan_major_b(s5_b_im), chan_major_c(s5_c_re), chan_major_c(s5_c_im),
                          u2c, u2d, h0)
    rw1, rb1 = _router_weights(1, router_w, router_bias)
    x3, h, gates = _s5_out(x2c, x2d, uc, ud, yc, yd, mods[1], row(s5_d[0]),
                           w_out_c[0], row(ln_mix_g[1]), row(ln_mix_b[1]), rw1, rb1)
    acc = _moe(1, h, gates, mods[1], exp_w_gate, exp_w_up, exp_w_down, sh_w_gate, sh_w_up, sh_w_down)
    lg1, lb1 = row(ln_ffn_g[1]), row(ln_ffn_b[1])
    y_prompt = _moe_finish(x3, acc, 0, T_CTX, lg1, lb1).reshape(N_CTX_B, CTX_L, D)
    y_sample = _moe_finish(x3, acc, n_ctx_tiles, T_DEN, lg1, lb1).reshape(N_DEN_B, DEN_L, D)
    new_mla_ckv = ckv.reshape(N_CTX_B, 1, CTX_L, MLA_KV_RANK)
    new_mla_krope = jnp.transpose(kr_ctx_t, (0, 1, 3, 2))
    new_state_ssm = jnp.transpose(st, (2, 0, 1, 3)).reshape(N_CTX_B, 1, 2, 2, S5_G, S5_P)
    return (y_prompt, y_sample, new_attn_k, new_attn_v, new_mla_ckv, new_mla_krope, new_state_ssm)
```

```python
import functools

import jax
import jax.numpy as jnp
import numpy as np
from jax import lax
from jax.experimental import pallas as pl
from jax.experimental.pallas import tpu as pltpu

f32 = jnp.float32
bf16 = jnp.bfloat16

D = 1024
N_CTX_B, CTX_L = 16, 256
N_DEN_B, DEN_L = 2, 1024
T_CTX = N_CTX_B * CTX_L
T_DEN = N_DEN_B * DEN_L
T_ALL = T_CTX + T_DEN
GRID_W = 64
WINDOW = 128
ROPE_BASE = 10000.0
A_HEADS, A_KV_HEADS, A_HD = 8, 2, 64
A_GROUP = A_HEADS // A_KV_HEADS
A_SCALE = A_HD ** -0.5
MLA_HEADS, MLA_Q_RANK, MLA_KV_RANK = 8, 256, 128
MLA_NOPE, MLA_ROPE, MLA_V = 64, 32, 64
MLA_SCALE = (MLA_NOPE + MLA_ROPE) ** -0.5
N_EXPERTS, TOP_K, EXPERT_FF, SHARED_FF = 64, 6, 128, 128
ROUTED_SCALE = 2.5
DEPTH = 2
ALPHA = (2.0 * DEPTH) ** 0.25
LN_EPS = 1e-5
RMS_EPS = 1e-6
NEG_INF = -1e30
S5_G, S5_CH, S5_P = 64, 16, 64

LANE = 128
SUB = 8
VMEM_LIMIT = 56 * 1024 * 1024

TOK_TILE = 512


def _mod_row(tile_idx, tile_rows):
    start = tile_idx * tile_rows
    return jnp.where(start < T_CTX, 0, 1 + (start - T_CTX) // DEN_L)


def _layer_norm(y, g, b):
    mu = jnp.mean(y, axis=-1, keepdims=True)
    yc = y - mu
    var = jnp.mean(yc * yc, axis=-1, keepdims=True)
    return yc * lax.rsqrt(var + LN_EPS) * g + b


def _silu(x):
    return x * jax.nn.sigmoid(x)


def _dot(a, b):
    return jnp.dot(a, b, preferred_element_type=f32)


def _dot_nt(a, b):
    return lax.dot_general(a, b, (((1,), (1,)), ((), ())), preferred_element_type=f32)


def _split_bf16(a):
    hi = a.astype(bf16)
    return hi, (a - hi.astype(f32)).astype(bf16)


def _full(shape, n_grid):
    zeros = tuple(0 for _ in shape)
    return pl.BlockSpec(shape, lambda *_: zeros)


def _two_stream_specs(tile_rows, width):
    n_ctx = T_CTX // tile_rows
    return (pl.BlockSpec((tile_rows, width), lambda i: (jnp.minimum(i, n_ctx - 1), 0)),
            pl.BlockSpec((tile_rows, width), lambda i: (jnp.maximum(i - n_ctx, 0), 0)))


def _pick(i, tile_rows, ctx_ref, den_ref):
    return jnp.where(i < T_CTX // tile_rows, ctx_ref[...], den_ref[...])


def _params(n_grid):
    return pltpu.CompilerParams(dimension_semantics=("arbitrary",) * n_grid, vmem_limit_bytes=VMEM_LIMIT)


ADA_TN = 1536


def _adaln_kernel(c_ref, w_ref, b_ref, *o_refs):
    s_hi, s_lo = _split_bf16(_silu(c_ref[...]))
    w_hi, w_lo = _split_bf16(w_ref[0])
    val = _dot(s_hi, w_hi) + (_dot(s_hi, w_lo) + _dot(s_lo, w_hi)) + b_ref[0]
    for k, o_ref in enumerate(o_refs):
        @pl.when(pl.program_id(0) == k)
        def _():
            o_ref[...] = val


def _adaln(cvec8, ada_w, ada_b):
    n = 6 * D
    nb = n // ADA_TN
    layer_spec = lambda k: pl.BlockSpec((8, ADA_TN), lambda l, j: (0, jnp.clip((l - k) * nb + j, 0, nb - 1)))
    return pl.pallas_call(
        _adaln_kernel,
        grid=(DEPTH, n // ADA_TN),
        in_specs=[
            pl.BlockSpec((8, D), lambda l, j: (0, 0)),
            pl.BlockSpec((1, D, ADA_TN), lambda l, j: (l, 0, j)),
            pl.BlockSpec((1, 1, ADA_TN), lambda l, j: (l, 0, j)),
        ],
        out_specs=[layer_spec(k) for k in range(DEPTH)],
        out_shape=[jax.ShapeDtypeStruct((8, n), f32) for _ in range(DEPTH)],
        compiler_params=_params(2),
        name="adaln",
    )(cvec8, ada_w, ada_b.reshape(DEPTH, 1, n))


def _rope_table_array(head_dim):
    q = head_dim // 4
    pos = np.arange(DEN_L)
    row, col = (pos // GRID_W).astype(np.float64), (pos % GRID_W).astype(np.float64)
    lane = np.arange(LANE) % head_dim
    is_col = lane >= head_dim // 2
    w = lane % (head_dim // 2)
    first = w < q
    inv_freq = ROPE_BASE ** (-np.arange(q, dtype=np.float64) / q)
    ang = np.where(is_col[None, :], col[:, None], row[:, None]) * inv_freq[w % q][None, :]
    cos, sin = np.cos(ang), np.sin(ang)
    sin_a = np.where(first[None, :], -sin, 0.0)
    sin_b = np.where(first[None, :], 0.0, sin)
    ident = np.stack([np.ones((TOK_TILE, LANE)), np.zeros((TOK_TILE, LANE)), np.zeros((TOK_TILE, LANE))])
    tab = np.concatenate([ident, np.stack([cos, sin_a, sin_b])], axis=1).astype(np.float32)
    return jnp.asarray(tab), q


def _rope_chunk(x, tab_ref, q):
    return x * tab_ref[0] + pltpu.roll(x, LANE - q, 1) * tab_ref[1] + pltpu.roll(x, q, 1) * tab_ref[2]


PROJ_W = 1280
C_QA, C_KA, C_VA, C_CQ, C_CKV, C_KR = 0, 512, 640, 768, 1024, 1152
MLA_NN = MLA_HEADS * MLA_NOPE


def _ab_proj_kernel(xc_ref, xd_ref, mods_ref, w_ref, qn_ref, kvn_ref, wuq_ref, wukv_ref, ta_ref, tm_ref,
                    qa_ref, ka_ref, va_ref, ckv_ref, kr_ref, qm_ref, kvl_ref, krc_ref, wb_ref, *, qa_shift, qm_shift):
    i = pl.program_id(0)

    @pl.when(i == 0)
    def _():
        n_w = w_ref.shape[1]
        wb_ref[:, PROJ_W - LANE:PROJ_W] = jnp.zeros((D, LANE), bf16)
        wb_ref[:, 0:n_w] = w_ref[...].astype(bf16)

    r = _mod_row(i, TOK_TILE)
    mrow = mods_ref[pl.ds(r, 1), :]
    sh, sc = mrow[:, 0:D], mrow[:, D:2 * D]
    x = _pick(i, TOK_TILE, xc_ref, xd_ref)
    h = (x * (1.0 + sc) + sh).astype(bf16)
    proj = _dot(h, wb_ref[...])
    for j in range(4):
        c0 = C_QA + LANE * j
        qa_ref[:, LANE * j:LANE * (j + 1)] = _rope_chunk(proj[:, c0:c0 + LANE], ta_ref, qa_shift).astype(bf16)
    ka_ref[...] = _rope_chunk(proj[:, C_KA:C_KA + LANE], ta_ref, qa_shift)
    va_ref[...] = proj[:, C_VA:C_VA + LANE]
    cq = proj[:, C_CQ:C_CQ + MLA_Q_RANK]
    cq = cq * lax.rsqrt(jnp.mean(cq * cq, axis=-1, keepdims=True) + RMS_EPS) * qn_ref[...]
    ckv = proj[:, C_CKV:C_CKV + MLA_KV_RANK]
    ckv = ckv * lax.rsqrt(jnp.mean(ckv * ckv, axis=-1, keepdims=True) + RMS_EPS) * kvn_ref[...]
    kr = _rope_chunk(proj[:, C_KR:C_KR + LANE], tm_ref, qm_shift)
    kr_ref[...] = kr

    qm = _dot(cq.astype(bf16), wuq_ref[...])
    qm_ref[:, 0:MLA_NN] = qm[:, 0:MLA_NN].astype(bf16)
    for j in range(2):
        c0 = MLA_NN + LANE * j
        qm_ref[:, c0:c0 + LANE] = _rope_chunk(qm[:, c0:c0 + LANE], tm_ref, qm_shift).astype(bf16)
    kvl_ref[...] = _dot(ckv.astype(bf16), wukv_ref[...]).astype(bf16)

    @pl.when(i < T_CTX // TOK_TILE)
    def _():
        ckv_ref[...] = ckv
        kr_t = kr_ref[...].T
        for b in range(TOK_TILE // CTX_L):
            krc_ref[b, 0] = kr_t[0:MLA_ROPE, CTX_L * b:CTX_L * (b + 1)]


def _rope_block_index(i):
    tiles_ctx = T_CTX // TOK_TILE
    per_seq = DEN_L // TOK_TILE
    return jnp.where(i < tiles_ctx, 0, 1 + (i - tiles_ctx) % per_seq)


def _ab_proj(xc, xd, mods0, w_in, q_norm, kv_norm, w_uq_p, w_ukv_p):
    tab_a, qa_shift = _rope_table_array(A_HD)
    tab_m, qm_shift = _rope_table_array(MLA_ROPE)
    row_spec = lambda w: pl.BlockSpec((TOK_TILE, w), lambda i: (i, 0))
    xc_spec, xd_spec = _two_stream_specs(TOK_TILE, D)
    tab_spec = pl.BlockSpec((3, TOK_TILE, LANE), lambda i: (0, _rope_block_index(i), 0))
    outs = [(512, bf16), (LANE, f32), (LANE, f32), (LANE, f32), (LANE, f32), (768, bf16), (1024, bf16)]
    last_ctx = T_CTX // TOK_TILE - 1
    per = TOK_TILE // CTX_L
    ctx_spec = pl.BlockSpec((TOK_TILE, LANE), lambda i: (jnp.minimum(i, last_ctx), 0))
    krc_spec = pl.BlockSpec((per, 1, MLA_ROPE, CTX_L), lambda i: (jnp.minimum(i, last_ctx), 0, 0, 0))
    return pl.pallas_call(
        functools.partial(_ab_proj_kernel, qa_shift=qa_shift, qm_shift=qm_shift),
        grid=(T_ALL // TOK_TILE,),
        in_specs=[xc_spec, xd_spec, _full((8, 6 * D), 1), _full(w_in.shape, 1), _full((1, MLA_Q_RANK), 1),
                  _full((1, MLA_KV_RANK), 1), _full((MLA_Q_RANK, 768), 1), _full((MLA_KV_RANK, 1024), 1),
                  tab_spec, tab_spec],
        out_specs=[ctx_spec if k == 3 else row_spec(w) for k, (w, _) in enumerate(outs)] + [krc_spec],
        out_shape=[jax.ShapeDtypeStruct((T_CTX if k == 3 else T_ALL, w), dt) for k, (w, dt) in enumerate(outs)]
        + [jax.ShapeDtypeStruct((N_CTX_B, 1, MLA_ROPE, CTX_L), f32)],
        scratch_shapes=[pltpu.VMEM((D, PROJ_W), bf16)],
        compiler_params=_params(1),
        name="ab_proj",
    )(xc, xd, mods0, w_in, q_norm, kv_norm, w_uq_p, w_ukv_p, tab_a, tab_m)


def _softmax_blocks(s_refs, p_refs, sink_col=None):
    m = s_refs[0][...].max(axis=-1, keepdims=True)
    for s_ref in s_refs[1:]:
        m = jnp.maximum(m, s_ref[...].max(axis=-1, keepdims=True))
    if sink_col is not None:
        m = jnp.maximum(m, sink_col)
    for s_ref, p_ref in zip(s_refs, p_refs):
        p_ref[...] = jnp.exp(s_ref[...] - m).astype(bf16)
    return 0.0 if sink_col is None else jnp.exp(sink_col - m)


def _with_ones(v, axis=1):
    return jnp.concatenate([v, jnp.ones(v.shape, v.dtype)], axis=axis)


def _normalise(o_aug, extra, width):
    return o_aug[:, 0:width] * (1.0 / (o_aug[:, width:width + 1] + extra))


def _sink_column(sink_ref, rows_per_head):
    return jnp.concatenate([jnp.full((rows_per_head, 1), sink_ref[h], f32) for h in range(A_HEADS)], axis=0)


def _mla_q(qm_ref, h):
    rows = qm_ref.shape[0]
    return jnp.concatenate([qm_ref[:, MLA_NOPE * h:MLA_NOPE * (h + 1)],
                            qm_ref[:, MLA_NN + MLA_ROPE * h:MLA_NN + MLA_ROPE * (h + 1)],
                            jnp.zeros((rows, LANE - MLA_NOPE - MLA_ROPE), bf16)], axis=1)


def _mla_k(k_nope_h, k_rope):
    rows = k_nope_h.shape[0]
    return jnp.concatenate([k_nope_h, k_rope, jnp.zeros((rows, LANE - MLA_NOPE - MLA_ROPE), bf16)], axis=1)


def _mix_out_ln(merged_ref, wout_ref, x, mods_ref, r, g_ref, b_ref):
    out = _dot(merged_ref[...], wout_ref[...])
    gate = mods_ref[pl.ds(r, 1), 2 * D:3 * D]
    return _layer_norm(ALPHA * x + gate * out, g_ref[...], b_ref[...])


def _ctx_attn_kernel(sink_ref, qa_ref, ka_ref, va_ref, qm_ref, kvl_ref, kr_ref, x_ref, mods_ref, wout_ref,
                     g_ref, b_ref, o_ref, nk_ref, nv_ref, merged_ref, sa_ref, sm_ref, pa_ref, pm_ref):
    nk_ref[0, 0] = ka_ref[...].T.reshape(A_KV_HEADS, A_HD, CTX_L)
    nv_ref[0, 0] = va_ref[...].T.reshape(A_KV_HEADS, A_HD, CTX_L)
    n = CTX_L
    ka = ka_ref[...].astype(bf16)
    va = va_ref[...].astype(bf16)
    for j in range(A_KV_HEADS):
        q4 = jnp.concatenate([qa_ref[:, A_HD * h:A_HD * (h + 1)] for h in range(A_GROUP * j, A_GROUP * (j + 1))],
                             axis=0)
        sa_ref[A_GROUP * n * j:A_GROUP * n * (j + 1), :] = _dot_nt(q4, ka[:, A_HD * j:A_HD * (j + 1)]) * A_SCALE
    kr = kr_ref[:, 0:MLA_ROPE].astype(bf16)
    for h in range(MLA_HEADS):
        k_cat = _mla_k(kvl_ref[:, MLA_NOPE * h:MLA_NOPE * (h + 1)], kr)
        sm_ref[n * h:n * (h + 1), :] = _dot_nt(_mla_q(qm_ref, h), k_cat) * MLA_SCALE
    sink_a = _softmax_blocks([sa_ref], [pa_ref], _sink_column(sink_ref, n))
    _softmax_blocks([sm_ref], [pm_ref])
    for j in range(A_KV_HEADS):
        rows = slice(A_GROUP * n * j, A_GROUP * n * (j + 1))
        o4 = _normalise(_dot(pa_ref[rows, :], _with_ones(va[:, A_HD * j:A_HD * (j + 1)])), sink_a[rows], A_HD)
        for g in range(A_GROUP):
            h = A_GROUP * j + g
            merged_ref[:, A_HD * h:A_HD * (h + 1)] = o4[n * g:n * (g + 1)].astype(bf16)
    for h in range(MLA_HEADS):
        rows = slice(n * h, n * (h + 1))
        v = _with_ones(kvl_ref[:, MLA_NN + MLA_V * h:MLA_NN + MLA_V * (h + 1)])
        merged_ref[:, MLA_NN + MLA_V * h:MLA_NN + MLA_V * (h + 1)] = (
            _normalise(_dot(pm_ref[rows, :], v), 0.0, MLA_V).astype(bf16))
    o_ref[...] = _mix_out_ln(merged_ref, wout_ref, x_ref[...], mods_ref, 0, g_ref, b_ref)


def _ctx_attn(sink, qa, ka, va, qm, kvl, kr, x_all, mods0, w_out, ln_g, ln_b):
    blk = lambda w: pl.BlockSpec((CTX_L, w), lambda b: (b, 0))
    cache_blk = pl.BlockSpec((1, 1, A_KV_HEADS, A_HD, CTX_L), lambda b: (b, 0, 0, 0, 0))
    cache_shape = jax.ShapeDtypeStruct((N_CTX_B, 1, A_KV_HEADS, A_HD, CTX_L), f32)
    return pl.pallas_call(
        _ctx_attn_kernel,
        grid=(N_CTX_B,),
        in_specs=[pl.BlockSpec(memory_space=pltpu.SMEM), blk(512), blk(LANE), blk(LANE), blk(768), blk(1024),
                  blk(LANE), blk(D), _full((8, 6 * D), 1), _full((D, D), 1), _full((1, D), 1), _full((1, D), 1)],
        out_specs=[blk(D), cache_blk, cache_blk],
        out_shape=[jax.ShapeDtypeStruct((T_CTX, D), f32), cache_shape, cache_shape],
        scratch_shapes=[pltpu.VMEM((CTX_L, D), bf16),
                        pltpu.VMEM((A_HEADS * CTX_L, CTX_L), f32), pltpu.VMEM((MLA_HEADS * CTX_L, CTX_L), f32),
                        pltpu.VMEM((A_HEADS * CTX_L, CTX_L), bf16), pltpu.VMEM((MLA_HEADS * CTX_L, CTX_L), bf16)],
        compiler_params=_params(1),
        name="ctx_attn",
    )(sink, qa, ka, va, qm, kvl, kr, x_all, mods0, w_out, ln_g, ln_b)


QB = 256
WIN = QB + 2 * WINDOW
DEN_BLK0 = T_CTX // DEN_L
MLA_KEYS = CTX_L + DEN_L


def _den_attn_kernel(sink_ref, qa_ref, ka_ref, va_ref, cak_ref, cav_ref, qm_ref, kvl_ref, kr_ref, cckv_ref, ckr_ref,
                     wukv_ref, x_ref, mods_ref, wout_ref, g_ref, b_ref, o_ref, merged_ref, kcat_ref, vcat_ref,
                     saw_ref, sac_ref, sm_ref, paw_ref, pac_ref, pm_ref):
    b = pl.program_id(0)
    n = pl.program_id(1)

    @pl.when(n == 0)
    def _():
        kvc = _dot(cckv_ref[0].astype(bf16), wukv_ref[...]).astype(bf16)
        kr_ctx = ckr_ref[0].astype(bf16)
        kr_lat = kr_ref[:, 0:MLA_ROPE].astype(bf16)
        for h in range(MLA_HEADS):
            ns = slice(MLA_NOPE * h, MLA_NOPE * (h + 1))
            vs = slice(MLA_NN + MLA_V * h, MLA_NN + MLA_V * (h + 1))
            kcat_ref[h, 0:CTX_L, :] = _mla_k(kvc[:, ns], kr_ctx)
            kcat_ref[h, CTX_L:MLA_KEYS, :] = _mla_k(kvl_ref[:, ns], kr_lat)
            vcat_ref[h, 0:CTX_L, :] = _with_ones(kvc[:, vs])
            vcat_ref[h, CTX_L:MLA_KEYS, :] = _with_ones(kvl_ref[:, vs])

    start = pl.multiple_of(jnp.clip(QB * n - WINDOW, 0, DEN_L - WIN), WINDOW)
    grp_rows = A_GROUP * QB
    qpos = QB * n + (lax.broadcasted_iota(jnp.int32, (grp_rows, WIN), 0) & (QB - 1))
    kpos = start + lax.broadcasted_iota(jnp.int32, (grp_rows, WIN), 1)
    valid = jnp.abs(qpos - kpos) <= WINDOW
    kwin = ka_ref[pl.ds(start, WIN), :].astype(bf16)
    vwin = va_ref[pl.ds(start, WIN), :].astype(bf16)
    kctx_t = [cak_ref[0, j].astype(bf16) for j in range(A_KV_HEADS)]
    vctx_t = [cav_ref[0, j].astype(bf16) for j in range(A_KV_HEADS)]
    for j in range(A_KV_HEADS):
        sl = slice(A_HD * j, A_HD * (j + 1))
        rows = slice(grp_rows * j, grp_rows * (j + 1))
        q4 = jnp.concatenate([qa_ref[:, A_HD * h:A_HD * (h + 1)] for h in range(A_GROUP * j, A_GROUP * (j + 1))],
                             axis=0)
        saw_ref[rows, :] = jnp.where(valid, _dot_nt(q4, kwin[:, sl]) * A_SCALE, NEG_INF)
        sac_ref[rows, :] = _dot(q4, kctx_t[j]) * A_SCALE
    for h in range(MLA_HEADS):
        sm_ref[QB * h:QB * (h + 1), :] = _dot_nt(_mla_q(qm_ref, h), kcat_ref[h]) * MLA_SCALE
    sink_a = _softmax_blocks([saw_ref, sac_ref], [paw_ref, pac_ref], _sink_column(sink_ref, QB))
    _softmax_blocks([sm_ref], [pm_ref])
    for j in range(A_KV_HEADS):
        sl = slice(A_HD * j, A_HD * (j + 1))
        rows = slice(grp_rows * j, grp_rows * (j + 1))
        o_aug = (_dot(paw_ref[rows, :], _with_ones(vwin[:, sl]))
                 + _dot_nt(pac_ref[rows, :], _with_ones(vctx_t[j], axis=0)))
        o4 = _normalise(o_aug, sink_a[rows], A_HD)
        for g in range(A_GROUP):
            h = A_GROUP * j + g
            merged_ref[:, A_HD * h:A_HD * (h + 1)] = o4[QB * g:QB * (g + 1)].astype(bf16)
    for h in range(MLA_HEADS):
        rows = slice(QB * h, QB * (h + 1))
        o = _normalise(_dot(pm_ref[rows, :], vcat_ref[h]), 0.0, MLA_V)
        merged_ref[:, MLA_NN + MLA_V * h:MLA_NN + MLA_V * (h + 1)] = o.astype(bf16)
    o_ref[...] = _mix_out_ln(merged_ref, wout_ref, x_ref[...], mods_ref, 1 + b, g_ref, b_ref)


def _den_attn(sink, qa, ka, va, cache_k, cache_v, qm, kvl, kr, cache_ckv, cache_kr, w_ukv_p, x_all, mods0, w_out,
              ln_g, ln_b):
    nq = DEN_L // QB
    qblk = lambda w: pl.BlockSpec((QB, w), lambda b, n: (T_CTX // QB + b * nq + n, 0))
    seq = lambda w: pl.BlockSpec((DEN_L, w), lambda b, n: (DEN_BLK0 + b, 0))
    cache = lambda w: pl.BlockSpec((1, CTX_L, w), lambda b, n: (b, 0, 0))
    cache_a = pl.BlockSpec((1, A_KV_HEADS, A_HD, CTX_L), lambda b, n: (b, 0, 0, 0))
    return pl.pallas_call(
        _den_attn_kernel,
        grid=(N_DEN_B, nq),
        in_specs=[pl.BlockSpec(memory_space=pltpu.SMEM), qblk(512), seq(LANE), seq(LANE), cache_a, cache_a,
                  qblk(768), seq(1024), seq(LANE), cache(MLA_KV_RANK), cache(MLA_ROPE),
                  _full((MLA_KV_RANK, 1024), 2), pl.BlockSpec((QB, D), lambda b, n: (b * nq + n, 0)),
                  _full((8, 6 * D), 2), _full((D, D), 2), _full((1, D), 2),
                  _full((1, D), 2)],
        out_specs=pl.BlockSpec((QB, D), lambda b, n: (b * nq + n, 0)),
        out_shape=jax.ShapeDtypeStruct((T_DEN, D), f32),
        scratch_shapes=[pltpu.VMEM((QB, D), bf16), pltpu.VMEM((MLA_HEADS, MLA_KEYS, LANE), bf16),
                        pltpu.VMEM((MLA_HEADS, MLA_KEYS, 2 * MLA_V), bf16)]
        + [pltpu.VMEM((A_HEADS * QB, w), dt) for dt in (f32, bf16) for w in (WIN, CTX_L, MLA_KEYS)],
        compiler_params=_params(2),
        name="den_attn",
    )(sink, qa, ka, va, cache_k, cache_v, qm, kvl, kr, cache_ckv, cache_kr, w_ukv_p, x_all, mods0, w_out, ln_g, ln_b)


SUBTILE = 256


def _route(x1, mrow, rw_ref, rb_ref):
    sh, sc = mrow[:, 3 * D:4 * D], mrow[:, 4 * D:5 * D]
    h = x1 * (1.0 + sc) + sh
    h_hi = h.astype(bf16)
    h_lo = (h - h_hi.astype(f32)).astype(bf16)
    logits = _dot(h_hi, rw_ref[0]) + (_dot(h_hi, rw_ref[1]) + _dot(h_lo, rw_ref[0]))
    scores = jax.nn.sigmoid(logits)
    lane = lax.broadcasted_iota(jnp.int32, scores.shape, 1).astype(f32)
    sel = jnp.where(lane < N_EXPERTS, scores + rb_ref[...], -jnp.inf)
    gates = jnp.zeros_like(scores)
    for _ in range(TOP_K):
        m = sel.max(axis=-1, keepdims=True)
        idx = jnp.where(sel == m, lane, float(LANE)).min(axis=-1, keepdims=True)
        hit = lane == idx
        gates = jnp.where(hit, scores, gates)
        sel = jnp.where(hit, -jnp.inf, sel)
    return h_hi, gates / gates.sum(axis=-1, keepdims=True) * ROUTED_SCALE


def _router_kernel(xc_ref, xd_ref, mods_ref, rw_ref, rb_ref, x_ref, h_ref, gates_ref):
    i = pl.program_id(0)
    r = _mod_row(i, TOK_TILE)
    mrow = mods_ref[pl.ds(r, 1), :]
    x1 = _pick(i, TOK_TILE, xc_ref, xd_ref)
    x_ref[...] = x1
    h_ref[...], gates_ref[...] = _route(x1, mrow, rw_ref, rb_ref)


def _router(x1c, x1d, mods_l, router_w_p, router_b_p):
    row_spec = lambda w: pl.BlockSpec((TOK_TILE, w), lambda i: (i, 0))
    xc_spec, xd_spec = _two_stream_specs(TOK_TILE, D)
    return pl.pallas_call(
        _router_kernel,
        grid=(T_ALL // TOK_TILE,),
        in_specs=[xc_spec, xd_spec, _full((8, 6 * D), 1), _full((2, D, LANE), 1), _full((1, LANE), 1)],
        out_specs=[row_spec(D), row_spec(D), row_spec(LANE)],
        out_shape=[jax.ShapeDtypeStruct((T_ALL, D), f32), jax.ShapeDtypeStruct((T_ALL, D), bf16),
                   jax.ShapeDtypeStruct((T_ALL, LANE), f32)],
        compiler_params=_params(1),
        name="router",
    )(x1c, x1d, mods_l, router_w_p, router_b_p)


MOE_TOK = 1536
MOE_EG = 8
MOE_VMEM_LIMIT = 60 * 1024 * 1024
MOE_TILE = 512
MOE_FF = MOE_EG * EXPERT_FF


def _moe_kernel(h_ref, gates_ref, mods_ref, wg_ref, wu_ref, wd_ref, sg_ref, su_ref, sd_ref, o_ref):
    p = pl.program_id(0)
    e = pl.program_id(1)
    n_tiles = MOE_TOK // MOE_TILE

    def gate_f(t):
        r = _mod_row(p * n_tiles + t, MOE_TILE)
        return mods_ref[pl.ds(r, 1), 5 * D:6 * D]

    def rows_of(t):
        if isinstance(t, int):
            return pl.ds(t * MOE_TILE, MOE_TILE)
        return pl.ds(pl.multiple_of(t * MOE_TILE, MOE_TILE), MOE_TILE)

    @pl.when(e == 0)
    def _():
        sg = sg_ref[...].astype(bf16)
        su = su_ref[...].astype(bf16)
        sd = sd_ref[...].astype(bf16)

        def body(t, c):
            rows = rows_of(t)
            ht = h_ref[rows, :]
            hid = _silu(_dot(ht, sg)) * _dot(ht, su)
            o_ref[rows, :] = gate_f(t) * _dot(hid.astype(bf16), sd)
            return c

        lax.fori_loop(0, n_tiles, body, 0)

    wg = jnp.concatenate([wg_ref[k].astype(bf16) for k in range(MOE_EG)], axis=1)
    wu = jnp.concatenate([wu_ref[k].astype(bf16) for k in range(MOE_EG)], axis=1)
    wd = jnp.concatenate([wd_ref[k].astype(bf16) for k in range(MOE_EG)], axis=0)
    lane = lax.broadcasted_iota(jnp.int32, (MOE_TILE, LANE), 1)

    def body(t, c):
        rows = rows_of(t)
        ht = h_ref[rows, :]
        hid = _silu(_dot(ht, wg)) * _dot(ht, wu)
        gt = gates_ref[rows, :]
        parts = []
        for k in range(MOE_EG):
            col = jnp.where(lane == e * MOE_EG + k, gt, 0.0).sum(axis=-1, keepdims=True)
            parts.append((hid[:, EXPERT_FF * k:EXPERT_FF * (k + 1)] * col).astype(bf16))
        o_ref[rows, :] += gate_f(t) * _dot(jnp.concatenate(parts, axis=1), wd)
        return c

    for t in range(n_tiles):
        body(t, 0)


def _moe(l, h, gates, mods_l, wg, wu, wd, sg, su, sd):
    tok = lambda w: pl.BlockSpec((MOE_TOK, w), lambda p, e: (p, 0))
    return pl.pallas_call(
        _moe_kernel,
        grid=(T_ALL // MOE_TOK, N_EXPERTS // MOE_EG),
        in_specs=[tok(D), tok(LANE), _full((8, 6 * D), 2),
                  pl.BlockSpec((None, MOE_EG, D, EXPERT_FF), lambda p, e: (l, e, 0, 0)),
                  pl.BlockSpec((None, MOE_EG, D, EXPERT_FF), lambda p, e: (l, e, 0, 0)),
                  pl.BlockSpec((None, MOE_EG, EXPERT_FF, D), lambda p, e: (l, e, 0, 0)),
                  pl.BlockSpec((None, D, SHARED_FF), lambda p, e: (l, 0, 0)),
                  pl.BlockSpec((None, D, SHARED_FF), lambda p, e: (l, 0, 0)),
                  pl.BlockSpec((None, SHARED_FF, D), lambda p, e: (l, 0, 0))],
        out_specs=tok(D),
        out_shape=jax.ShapeDtypeStruct((T_ALL, D), f32),
        compiler_params=pltpu.CompilerParams(dimension_semantics=("arbitrary", "arbitrary"),
                                             vmem_limit_bytes=MOE_VMEM_LIMIT),
        name="moe",
    )(h, gates, mods_l, wg, wu, wd, sg, su, sd)


def _moe_finish_kernel(x_ref, acc_ref, g_ref, b_ref, o_ref):
    o_ref[...] = _layer_norm(ALPHA * x_ref[...] + acc_ref[...], g_ref[...], b_ref[...])


def _moe_finish(x1, acc, tile0, n_rows, ln_g, ln_b):
    src = pl.BlockSpec((TOK_TILE, D), lambda i: (tile0 + i, 0))
    return pl.pallas_call(
        _moe_finish_kernel,
        grid=(n_rows // TOK_TILE,),
        in_specs=[src, src, _full((1, D), 1), _full((1, D), 1)],
        out_specs=pl.BlockSpec((TOK_TILE, D), lambda i: (i, 0)),
        out_shape=jax.ShapeDtypeStruct((n_rows, D), f32),
        compiler_params=_params(1),
        name="moe_finish",
    )(x1, acc, ln_g, ln_b)


def _router_weights(l, router_w, router_bias):
    rw = jnp.pad(router_w[l], ((0, 0), (0, LANE - N_EXPERTS)))
    rw_hi, rw_lo = _split_bf16(rw)
    rb = jnp.pad(router_bias[l], (0, LANE - N_EXPERTS)).reshape(1, LANE)
    return jnp.stack([rw_hi, rw_lo]), rb


S5_Q = 8
S5_NGB = D // LANE


def _s5_in_kernel(x_ref, acc_ref, lg_ref, lb_ref, mods_ref, w_ref, x2_ref, u_ref, u2_ref, slab_ref, *, row_of,
                  seq_len):
    r = row_of(pl.program_id(0))
    mrow = mods_ref[pl.ds(r, 1), :]
    sh, sc = mrow[:, 0:D], mrow[:, D:2 * D]
    x2 = _layer_norm(ALPHA * x_ref[...] + acc_ref[...], lg_ref[...], lb_ref[...])
    x2_ref[...] = x2
    h = (x2 * (1.0 + sc) + sh).astype(bf16)
    u = _dot(h, w_ref[...])
    u_ref[...] = u
    for s in range(S5_NGB):
        slab_ref[s] = u[:, LANE * s:LANE * (s + 1)]
    kt = seq_len // S5_Q
    for s in range(S5_NGB):
        for q in range(TOK_TILE // seq_len):
            for j in range(S5_Q):
                u2_ref[s, q * kt:(q + 1) * kt, LANE * j:LANE * (j + 1)] = (
                    slab_ref[s, pl.ds(q * seq_len + j, kt, stride=S5_Q), :].astype(bf16))


def _s5_in(x1, acc, ln_g, ln_b, tile0, mods1, w_in_c, n_b, seq_len, row_of):
    n_tiles = n_b * seq_len // TOK_TILE
    chunks = TOK_TILE // S5_Q
    src = pl.BlockSpec((TOK_TILE, D), lambda i: (tile0 + i, 0))
    dst = pl.BlockSpec((TOK_TILE, D), lambda i: (i, 0))
    return pl.pallas_call(
        functools.partial(_s5_in_kernel, row_of=row_of, seq_len=min(seq_len, TOK_TILE)),
        grid=(n_tiles,),
        in_specs=[src, src, _full((1, D), 1), _full((1, D), 1), _full((8, 6 * D), 1), _full((D, D), 1)],
        out_specs=[dst, dst, pl.BlockSpec((S5_NGB, chunks, D), lambda i: (0, i, 0))],
        out_shape=[jax.ShapeDtypeStruct((n_b * seq_len, D), f32), jax.ShapeDtypeStruct((n_b * seq_len, D), f32),
                   jax.ShapeDtypeStruct((S5_NGB, n_tiles * chunks, D), bf16)],
        scratch_shapes=[pltpu.VMEM((S5_NGB, TOK_TILE, LANE), f32)],
        compiler_params=_params(1),
        name="s5_in",
    )(x1, acc, ln_g, ln_b, mods1, w_in_c)


S5_GL = (LANE // S5_CH) * S5_P
S5_ROWS_C = (CTX_L // S5_Q) * N_CTX_B
S5_ROWS_D = (DEN_L // S5_Q) * N_DEN_B


def _s5_scan_kernel(lam_ref, bt_ref, ctr_ref, cti_ref, uc_ref, ud_ref, h0_ref,
                    yc_ref, yd_ref, st_ref, win_ref, mso_ref, wit_ref, a_ref, s_ref, hp_ref):
    gl = S5_GL
    rowg = lax.shift_right_logical(lax.broadcasted_iota(jnp.int32, (LANE, gl), 0), 4)
    colg = lax.shift_right_logical(lax.broadcasted_iota(jnp.int32, (LANE, gl), 1), 6)
    same_group = rowg == colg
    reps = LANE // S5_CH

    def expand(t):
        return jnp.where(same_group, jnp.concatenate([t] * reps, axis=0), 0.0)

    def expand_c(t):
        return jnp.where(same_group, jnp.concatenate([t] * reps, axis=1), 0.0)

    for d in range(2):
        fwd = d == 0
        lre, lim = lam_ref[d, 0:1, :], lam_ref[d, 1:2, :]
        dt = jnp.exp(lam_ref[d, 2:3, :])
        a, w = lre * dt, lim * dt
        pre = [jnp.exp(m * a) * jnp.cos(m * w) for m in range(S5_Q + 1)]
        pim = [jnp.exp(m * a) * jnp.sin(m * w) for m in range(S5_Q + 1)]
        xr, xi = pre[1] - 1.0, pim[1]
        den = lre * lre + lim * lim
        cfr, cfi = (xr * lre + xi * lim) / den, (xi * lre - xr * lim) / den
        btr, bti = bt_ref[0, d], bt_ref[1, d]
        bexp_r = expand(cfr * btr - cfi * bti)
        bexp_i = expand(cfr * bti + cfi * btr)
        cexp_r, cexp_i = expand_c(ctr_ref[d]), expand_c(cti_ref[d])
        for m in range(S5_Q + 1):
            a_ref[m, :, 0:gl] = cexp_r * pre[m] - cexp_i * pim[m]
            a_ref[m, :, gl:2 * gl] = -(cexp_r * pim[m] + cexp_i * pre[m])
        for j in range(S5_Q):
            m = S5_Q - 1 - j if fwd else j
            win_ref[LANE * j:LANE * (j + 1), 0:gl] = (pre[m] * bexp_r - pim[m] * bexp_i).astype(bf16)
            win_ref[LANE * j:LANE * (j + 1), gl:2 * gl] = (pre[m] * bexp_i + pim[m] * bexp_r).astype(bf16)
        for j in range(S5_Q):
            m = j + 1 if fwd else S5_Q - j
            mso_ref[LANE * j:LANE * (j + 1), :] = a_ref[m].astype(bf16)
        b2 = jnp.concatenate([bexp_r, bexp_i], axis=1).astype(bf16)
        kt = [_dot_nt(b2, a_ref[tau].astype(bf16)) for tau in range(S5_Q)]
        for j in range(S5_Q):
            for jp in range(S5_Q):
                tau = jp - j if fwd else j - jp
                blk = slice(LANE * j, LANE * (j + 1)), slice(LANE * jp, LANE * (jp + 1))
                if fwd:
                    wit_ref[blk] = kt[tau] if tau >= 0 else jnp.zeros((LANE, LANE), f32)
                elif tau >= 0:
                    wit_ref[blk] = wit_ref[blk] + kt[tau]

        l8r, l8i = pre[S5_Q], pim[S5_Q]

        nsl = gl // LANE

        def slabs(ref, rs, first):
            return jnp.concatenate([ref[first + sl, rs, :] for sl in range(nsl)], axis=1)

        def put_slabs(ref, rs, first, val):
            for sl in range(nsl):
                ref[first + sl, rs, :] = val[:, LANE * sl:LANE * (sl + 1)]

        def advance(hr, hi_, sr, si):
            return l8r * hr - l8i * hi_ + sr, l8r * hi_ + l8i * hr + si

        def run(u_ref, y_ref, n_b, n_k, h_init):
            rows = n_b * n_k
            s = _dot(u_ref[0], win_ref[...])
            if n_b % SUB == 0:
                pitch = n_k + 1
                for bb in range(n_b):
                    dst = slice(bb * pitch, bb * pitch + n_k)
                    put_slabs(s_ref, dst, 0, s[bb * n_k:(bb + 1) * n_k, 0:gl])
                    put_slabs(s_ref, dst, nsl, s[bb * n_k:(bb + 1) * n_k, gl:2 * gl])

                def step(i, carry):
                    hr, hi_ = carry
                    rs = pl.ds(i if fwd else n_k - 1 - i, n_b, stride=pitch)
                    put_slabs(hp_ref, rs, 0, hr)
                    put_slabs(hp_ref, rs, nsl, hi_)
                    return advance(hr, hi_, slabs(s_ref, rs, 0), slabs(s_ref, rs, nsl))

                h_fin = lax.fori_loop(0, n_k, step, h_init)
                hp = jnp.concatenate(
                    [jnp.concatenate([hp_ref[sl, bb * pitch:bb * pitch + n_k, :] for sl in range(2 * nsl)], axis=1)
                     for bb in range(n_b)], axis=0).astype(bf16)
            else:
                put_slabs(s_ref, slice(0, rows), 0, s[:, 0:gl])
                put_slabs(s_ref, slice(0, rows), nsl, s[:, gl:2 * gl])
                n_it = n_k // SUB

                def step(i, carry):
                    it = i if fwd else n_it - 1 - i
                    out = []
                    for bb in range(n_b):
                        hr, hi_ = carry[bb]
                        rs = pl.ds(pl.multiple_of(bb * n_k + it * SUB, SUB), SUB)
                        s_re, s_im = slabs(s_ref, rs, 0), slabs(s_ref, rs, nsl)
                        prev_r, prev_i = [None] * SUB, [None] * SUB
                        for sub in (range(SUB) if fwd else reversed(range(SUB))):
                            prev_r[sub], prev_i[sub] = hr, hi_
                            hr, hi_ = advance(hr, hi_, s_re[sub:sub + 1], s_im[sub:sub + 1])
                        put_slabs(hp_ref, rs, 0, jnp.concatenate(prev_r, axis=0))
                        put_slabs(hp_ref, rs, nsl, jnp.concatenate(prev_i, axis=0))
                        out.append((hr, hi_))
                    return tuple(out)

                fin = lax.fori_loop(0, n_it, step, tuple((h_init[0][bb:bb + 1], h_init[1][bb:bb + 1])
                                                          for bb in range(n_b)))
                h_fin = (jnp.concatenate([f[0] for f in fin], axis=0), jnp.concatenate([f[1] for f in fin], axis=0))
                hp = jnp.concatenate([hp_ref[sl, 0:rows, :] for sl in range(2 * nsl)], axis=1).astype(bf16)
            y = _dot_nt(hp, mso_ref[...])
            if fwd:
                y_ref[0] = y
            else:
                y_ref[0] += y
            return h_fin

        zeros = jnp.zeros((N_CTX_B, gl), f32)
        hr, hi_ = run(uc_ref, yc_ref, N_CTX_B, CTX_L // S5_Q, (zeros, zeros))
        st_ref[d, 0] = hr
        st_ref[d, 1] = hi_
        run(ud_ref, yd_ref, N_DEN_B, DEN_L // S5_Q, (h0_ref[d, 0], h0_ref[d, 1]))

    wit = wit_ref[...].astype(bf16)
    yc_ref[0] += _dot(uc_ref[0], wit)
    yd_ref[0] += _dot(ud_ref[0], wit)


def _s5_scan(lam3, bt, ct_re, ct_im, u2c, u2d, h0):
    gl = S5_GL
    vec = pl.BlockSpec((2, 3, gl), lambda g: (0, 0, g))
    tab = pl.BlockSpec((2, 2, S5_CH, gl), lambda g: (0, 0, 0, g))
    ctab = pl.BlockSpec((2, LANE, S5_P), lambda g: (0, g, 0))
    rows = lambda n: pl.BlockSpec((1, n, D), lambda g: (g, 0, 0))
    return pl.pallas_call(
        _s5_scan_kernel,
        grid=(S5_NGB,),
        in_specs=[vec, tab, ctab, ctab, rows(S5_ROWS_C), rows(S5_ROWS_D),
                  pl.BlockSpec((2, 2, N_DEN_B, gl), lambda g: (0, 0, 0, g))],
        out_specs=[rows(S5_ROWS_C), rows(S5_ROWS_D), pl.BlockSpec((2, 2, N_CTX_B, gl), lambda g: (0, 0, 0, g))],
        out_shape=[jax.ShapeDtypeStruct((S5_NGB, S5_ROWS_C, D), f32), jax.ShapeDtypeStruct((S5_NGB, S5_ROWS_D, D), f32),
                   jax.ShapeDtypeStruct((2, 2, N_CTX_B, S5_G * S5_P), f32)],
        scratch_shapes=[pltpu.VMEM((D, 2 * gl), bf16), pltpu.VMEM((D, 2 * gl), bf16), pltpu.VMEM((D, D), f32),
                        pltpu.VMEM((S5_Q + 1, LANE, 2 * gl), f32),
                        pltpu.VMEM((2 * gl // LANE, S5_ROWS_C + 2 * N_CTX_B, LANE), f32),
                        pltpu.VMEM((2 * gl // LANE, S5_ROWS_C + 2 * N_CTX_B, LANE), f32)],
        compiler_params=_params(1),
        name="s5_scan",
    )(lam3, bt, ct_re, ct_im, u2c, u2d, h0)


def _gelu_tanh(x):
    return 0.5 * x * (1.0 + jnp.tanh(np.sqrt(2.0 / np.pi).astype(np.float32) * (x + 0.044715 * (x * x * x))))


def _s5_out_kernel(xc_ref, xd_ref, uc_ref, ud_ref, yc_ref, yd_ref, mods_ref, dsk_ref, wout_ref, g_ref, b_ref, rw_ref,
                   rb_ref, x1_ref, h_ref, gates_ref, slab_ref, wob_ref):
    i = pl.program_id(0)

    @pl.when(i == 0)
    def _():
        wob_ref[...] = wout_ref[...].astype(bf16)

    is_ctx = i < T_CTX // TOK_TILE
    r = _mod_row(i, TOK_TILE)
    mrow = mods_ref[pl.ds(r, 1), :]
    u = _pick(i, TOK_TILE, uc_ref, ud_ref)

    kt = CTX_L // S5_Q
    for s in range(S5_NGB):
        for q in range(TOK_TILE // CTX_L):
            for j in range(S5_Q):
                blk = (s, slice(q * kt, (q + 1) * kt), slice(LANE * j, LANE * (j + 1)))
                slab_ref[s, pl.ds(q * CTX_L + j, kt, stride=S5_Q), :] = jnp.where(is_ctx, yc_ref[blk], yd_ref[blk])
    halves = [slice(a, a + SUBTILE) for a in range(0, TOK_TILE, SUBTILE)]
    zs = []
    for rows in halves:
        y = jnp.concatenate([slab_ref[s, rows, :] for s in range(S5_NGB)], axis=1) + dsk_ref[...] * u[rows]
        zs.append(_dot(_gelu_tanh(y).astype(bf16), wob_ref[...]))
    for rows, z in zip(halves, zs):
        out = z[:, 0:D] * jax.nn.sigmoid(z[:, D:2 * D])
        x = jnp.where(is_ctx, xc_ref[rows, :], xd_ref[rows, :])
        x1 = _layer_norm(ALPHA * x + mrow[:, 2 * D:3 * D] * out, g_ref[...], b_ref[...])
        x1_ref[rows, :] = x1
        h_ref[rows, :], gates_ref[rows, :] = _route(x1, mrow, rw_ref, rb_ref)


def _s5_out(xc, xd, uc, ud, yc, yd, mods1, d_skip, w_out_c, ln_g, ln_b, rw, rb):
    row_spec = lambda w: pl.BlockSpec((TOK_TILE, w), lambda i: (i, 0))
    uc_spec, ud_spec = _two_stream_specs(TOK_TILE, D)
    n_ctx = T_CTX // TOK_TILE
    chunks = TOK_TILE // S5_Q
    return pl.pallas_call(
        _s5_out_kernel,
        grid=(T_ALL // TOK_TILE,),
        in_specs=[uc_spec, ud_spec, uc_spec, ud_spec,
                  pl.BlockSpec((S5_NGB, chunks, D), lambda i: (0, jnp.minimum(i, n_ctx - 1), 0)),
                  pl.BlockSpec((S5_NGB, chunks, D), lambda i: (0, jnp.maximum(i - n_ctx, 0), 0)),
                  _full((8, 6 * D), 1), _full((1, D), 1), _full((D, 2 * D), 1), _full((1, D), 1), _full((1, D), 1),
                  _full((2, D, LANE), 1), _full((1, LANE), 1)],
        out_specs=[row_spec(D), row_spec(D), row_spec(LANE)],
        out_shape=[jax.ShapeDtypeStruct((T_ALL, D), f32), jax.ShapeDtypeStruct((T_ALL, D), bf16),
                   jax.ShapeDtypeStruct((T_ALL, LANE), f32)],
        scratch_shapes=[pltpu.VMEM((S5_NGB, TOK_TILE, LANE), f32), pltpu.VMEM((D, 2 * D), bf16)],
        compiler_params=_params(1),
        name="s5_out",
    )(xc, xd, uc, ud, yc, yd, mods1, d_skip, w_out_c, ln_g, ln_b, rw, rb)


def kernel(x_prompt, x_sample, c, cache_attn_k, cache_attn_v, cache_mla_ckv, cache_mla_krope, state_ssm, c_ctx,
           ada_w, ada_b, ln_mix_g, ln_mix_b, ln_ffn_g, ln_ffn_b, w_in_ab, attn_sink, mla_q_norm, mla_kv_norm,
           mla_w_uq, mla_w_ukv, w_out_ab, w_in_c, s5_lam_re, s5_lam_im, s5_log_dt, s5_b_re, s5_b_im, s5_c_re,
           s5_c_im, s5_d, w_out_c, router_w, router_bias, exp_w_gate, exp_w_up, exp_w_down, sh_w_gate, sh_w_up,
           sh_w_down):
    row = lambda v: v.reshape(1, -1)
    xc, xd = x_prompt.reshape(T_CTX, D), x_sample.reshape(T_DEN, D)
    cvec8 = jnp.concatenate([c_ctx[None, :], c, jnp.zeros((8 - 1 - N_DEN_B, D), f32)], axis=0)
    mods = _adaln(cvec8, ada_w, ada_b)

    uq = mla_w_uq[0].reshape(MLA_Q_RANK, MLA_HEADS, MLA_NOPE + MLA_ROPE)
    w_uq_p = jnp.concatenate([uq[:, :, :MLA_NOPE].reshape(MLA_Q_RANK, -1), uq[:, :, MLA_NOPE:].reshape(MLA_Q_RANK, -1)],
                             axis=1).astype(bf16)
    ukv = mla_w_ukv[0].reshape(MLA_KV_RANK, MLA_HEADS, MLA_NOPE + MLA_V)
    w_ukv_p = jnp.concatenate([ukv[:, :, :MLA_NOPE].reshape(MLA_KV_RANK, -1),
                               ukv[:, :, MLA_NOPE:].reshape(MLA_KV_RANK, -1)], axis=1).astype(bf16)
    qa, ka, va, ckv, kr, qm, kvl, kr_ctx_t = _ab_proj(xc, xd, mods[0], w_in_ab[0], row(mla_q_norm[0]), row(mla_kv_norm[0]),
                                            w_uq_p, w_ukv_p)
    w_out_b = w_out_ab[0].astype(bf16)
    g0, b0 = row(ln_mix_g[0]), row(ln_mix_b[0])
    x1c, nk_t, nv_t = _ctx_attn(attn_sink[0], qa, ka, va, qm, kvl, kr, xc, mods[0], w_out_b, g0, b0)
    new_attn_k = jnp.transpose(nk_t, (0, 1, 4, 2, 3))
    new_attn_v = jnp.transpose(nv_t, (0, 1, 4, 2, 3))
    x1d = _den_attn(attn_sink[0], qa, ka, va,
                    jnp.transpose(cache_attn_k[:, 0], (0, 2, 3, 1)), jnp.transpose(cache_attn_v[:, 0], (0, 2, 3, 1)),
                    qm, kvl, kr, cache_mla_ckv[:, 0], cache_mla_krope[:, 0], w_ukv_p, xd, mods[0], w_out_b, g0, b0)
    rw0, rb0 = _router_weights(0, router_w, router_bias)
    x1, h, gates = _router(x1c, x1d, mods[0], rw0, rb0)
    acc = _moe(0, h, gates, mods[0], exp_w_gate, exp_w_up, exp_w_down, sh_w_gate, sh_w_up, sh_w_down)

    w_in_c_b = w_in_c[0].astype(bf16)
    lg0, lb0 = row(ln_ffn_g[0]), row(ln_ffn_b[0])
    n_ctx_tiles = T_CTX // TOK_TILE
    x2c, uc, u2c = _s5_in(x1, acc, lg0, lb0, 0, mods[1], w_in_c_b, N_CTX_B, CTX_L, lambda i: 0)
    x2d, ud, u2d = _s5_in(x1, acc, lg0, lb0, n_ctx_tiles, mods[1], w_in_c_b, N_DEN_B, DEN_L,
                          lambda i: 1 + i // (DEN_L // TOK_TILE))
    gp = S5_G * S5_P
    bt = jnp.transpose(jnp.stack([s5_b_re[0], s5_b_im[0]]), (0, 1, 4, 2, 3)).reshape(2, 2, S5_CH, gp)
    lam3 = jnp.stack([s5_lam_re[0].reshape(2, gp), s5_lam_im[0].reshape(2, gp),
                      jnp.repeat(s5_log_dt[0], S5_P, axis=-1).reshape(2, gp)], axis=1)
    chan_major_c = lambda t: t[0].reshape(2, S5_G * S5_CH, S5_P)
    h0 = jnp.transpose(state_ssm[:, 0], (1, 2, 0, 3, 4)).reshape(2, 2, N_DEN_B, gp)
    yc, yd, st = _s5_scan(lam3, bt, chan_major_c(s5_c_re), chan_major_c(s5_c_im), u2c, u2d, h0)
    rw1, rb1 = _router_weights(1, router_w, router_bias)
    x3, h, gates = _s5_out(x2c, x2d, uc, ud, yc, yd, mods[1], row(s5_d[0]),
                           w_out_c[0], row(ln_mix_g[1]), row(ln_mix_b[1]), rw1, rb1)
    acc = _moe(1, h, gates, mods[1], exp_w_gate, exp_w_up, exp_w_down, sh_w_gate, sh_w_up, sh_w_down)
    lg1, lb1 = row(ln_ffn_g[1]), row(ln_ffn_b[1])
    y_prompt = _moe_finish(x3, acc, 0, T_CTX, lg1, lb1).reshape(N_CTX_B, CTX_L, D)
    y_sample = _moe_finish(x3, acc, n_ctx_tiles, T_DEN, lg1, lb1).reshape(N_DEN_B, DEN_L, D)
    new_mla_ckv = ckv.reshape(N_CTX_B, 1, CTX_L, MLA_KV_RANK)
    new_mla_krope = jnp.transpose(kr_ctx_t, (0, 1, 3, 2))
    new_state_ssm = jnp.transpose(st, (2, 0, 1, 3)).reshape(N_CTX_B, 1, 2, 2, S5_G, S5_P)
    return (y_prompt, y_sample, new_attn_k, new_attn_v, new_mla_ckv, new_mla_krope, new_state_ssm)
```

```python
import functools

import jax
import jax.numpy as jnp
import numpy as np
from jax import lax
from jax.experimental import pallas as pl
from jax.experimental.pallas import tpu as pltpu

f32 = jnp.float32
bf16 = jnp.bfloat16

D = 1024
N_CTX_B, CTX_L = 16, 256
N_DEN_B, DEN_L = 2, 1024
T_CTX = N_CTX_B * CTX_L
T_DEN = N_DEN_B * DEN_L
T_ALL = T_CTX + T_DEN
GRID_W = 64
WINDOW = 128
ROPE_BASE = 10000.0
A_HEADS, A_KV_HEADS, A_HD = 8, 2, 64
A_GROUP = A_HEADS // A_KV_HEADS
A_SCALE = A_HD ** -0.5
MLA_HEADS, MLA_Q_RANK, MLA_KV_RANK = 8, 256, 128
MLA_NOPE, MLA_ROPE, MLA_V = 64, 32, 64
MLA_SCALE = (MLA_NOPE + MLA_ROPE) ** -0.5
N_EXPERTS, TOP_K, EXPERT_FF, SHARED_FF = 64, 6, 128, 128
ROUTED_SCALE = 2.5
DEPTH = 2
ALPHA = (2.0 * DEPTH) ** 0.25
LN_EPS = 1e-5
RMS_EPS = 1e-6
NEG_INF = -1e30
S5_G, S5_CH, S5_P = 64, 16, 64

LANE = 128
SUB = 8
VMEM_LIMIT = 56 * 1024 * 1024

TOK_TILE = 512


def _mod_row(tile_idx, tile_rows):
    start = tile_idx * tile_rows
    return jnp.where(start < T_CTX, 0, 1 + (start - T_CTX) // DEN_L)


def _layer_norm(y, g, b):
    mu = jnp.mean(y, axis=-1, keepdims=True)
    yc = y - mu
    var = jnp.mean(yc * yc, axis=-1, keepdims=True)
    return yc * lax.rsqrt(var + LN_EPS) * g + b


def _silu(x):
    return x * jax.nn.sigmoid(x)


def _dot(a, b):
    return jnp.dot(a, b, preferred_element_type=f32)


def _dot_nt(a, b):
    return lax.dot_general(a, b, (((1,), (1,)), ((), ())), preferred_element_type=f32)


def _split_bf16(a):
    hi = a.astype(bf16)
    return hi, (a - hi.astype(f32)).astype(bf16)


def _full(shape, n_grid):
    zeros = tuple(0 for _ in shape)
    return pl.BlockSpec(shape, lambda *_: zeros)


def _two_stream_specs(tile_rows, width):
    n_ctx = T_CTX // tile_rows
    return (pl.BlockSpec((tile_rows, width), lambda i: (jnp.minimum(i, n_ctx - 1), 0)),
            pl.BlockSpec((tile_rows, width), lambda i: (jnp.maximum(i - n_ctx, 0), 0)))


def _pick(i, tile_rows, ctx_ref, den_ref):
    return jnp.where(i < T_CTX // tile_rows, ctx_ref[...], den_ref[...])


def _params(n_grid):
    return pltpu.CompilerParams(dimension_semantics=("arbitrary",) * n_grid, vmem_limit_bytes=VMEM_LIMIT)


ADA_TN = 1536


def _adaln_kernel(c_ref, w_ref, b_ref, *o_refs):
    s_hi, s_lo = _split_bf16(_silu(c_ref[...]))
    w_hi, w_lo = _split_bf16(w_ref[0])
    bias = b_ref[pl.ds(pl.program_id(0), 1), :]
    val = _dot(s_hi, w_hi) + (_dot(s_hi, w_lo) + _dot(s_lo, w_hi)) + bias
    for k, o_ref in enumerate(o_refs):
        @pl.when(pl.program_id(0) == k)
        def _():
            o_ref[...] = val


def _adaln(cvec8, ada_w, ada_b):
    n = 6 * D
    nb = n // ADA_TN
    layer_spec = lambda k: pl.BlockSpec((8, ADA_TN), lambda l, j: (0, jnp.clip((l - k) * nb + j, 0, nb - 1)))
    return pl.pallas_call(
        _adaln_kernel,
        grid=(DEPTH, n // ADA_TN),
        in_specs=[
            pl.BlockSpec((8, D), lambda l, j: (0, 0)),
            pl.BlockSpec((1, D, ADA_TN), lambda l, j: (l, 0, j)),
            pl.BlockSpec((DEPTH, ADA_TN), lambda l, j: (0, j)),
        ],
        out_specs=[layer_spec(k) for k in range(DEPTH)],
        out_shape=[jax.ShapeDtypeStruct((8, n), f32) for _ in range(DEPTH)],
        compiler_params=_params(2),
        name="adaln",
    )(cvec8, ada_w, ada_b)


def _rope_table_array(head_dim):
    q = head_dim // 4
    pos = np.arange(DEN_L)
    row, col = (pos // GRID_W).astype(np.float64), (pos % GRID_W).astype(np.float64)
    lane = np.arange(LANE) % head_dim
    is_col = lane >= head_dim // 2
    w = lane % (head_dim // 2)
    first = w < q
    inv_freq = ROPE_BASE ** (-np.arange(q, dtype=np.float64) / q)
    ang = np.where(is_col[None, :], col[:, None], row[:, None]) * inv_freq[w % q][None, :]
    cos, sin = np.cos(ang), np.sin(ang)
    sin_a = np.where(first[None, :], -sin, 0.0)
    sin_b = np.where(first[None, :], 0.0, sin)
    ident = np.stack([np.ones((TOK_TILE, LANE)), np.zeros((TOK_TILE, LANE)), np.zeros((TOK_TILE, LANE))])
    tab = np.concatenate([ident, np.stack([cos, sin_a, sin_b])], axis=1).astype(np.float32)
    return jnp.asarray(tab), q


def _rope_chunk(x, tab_ref, q):
    return x * tab_ref[0] + pltpu.roll(x, LANE - q, 1) * tab_ref[1] + pltpu.roll(x, q, 1) * tab_ref[2]


PROJ_W = 1280
C_QA, C_KA, C_VA, C_CQ, C_CKV, C_KR = 0, 512, 640, 768, 1024, 1152
MLA_NN = MLA_HEADS * MLA_NOPE


def _ab_proj_kernel(xc_ref, xd_ref, mods_ref, w_ref, qn_ref, kvn_ref, wuq_ref, wukv_ref, ta_ref, tm_ref,
                    qa_ref, ka_ref, va_ref, ckv_ref, kr_ref, qm_ref, kvl_ref, krc_ref, wb_ref, *, qa_shift, qm_shift):
    i = pl.program_id(0)

    @pl.when(i == 0)
    def _():
        n_w = w_ref.shape[1]
        wb_ref[:, PROJ_W - LANE:PROJ_W] = jnp.zeros((D, LANE), bf16)
        wb_ref[:, 0:n_w] = w_ref[...].astype(bf16)

    r = _mod_row(i, TOK_TILE)
    mrow = mods_ref[pl.ds(r, 1), :]
    sh, sc = mrow[:, 0:D], mrow[:, D:2 * D]
    x = _pick(i, TOK_TILE, xc_ref, xd_ref)
    h = (x * (1.0 + sc) + sh).astype(bf16)
    proj = _dot(h, wb_ref[...])
    for j in range(4):
        c0 = C_QA + LANE * j
        qa_ref[:, LANE * j:LANE * (j + 1)] = _rope_chunk(proj[:, c0:c0 + LANE], ta_ref, qa_shift).astype(bf16)
    ka_ref[...] = _rope_chunk(proj[:, C_KA:C_KA + LANE], ta_ref, qa_shift)
    va_ref[...] = proj[:, C_VA:C_VA + LANE]
    cq = proj[:, C_CQ:C_CQ + MLA_Q_RANK]
    cq = cq * lax.rsqrt(jnp.mean(cq * cq, axis=-1, keepdims=True) + RMS_EPS) * qn_ref[...]
    ckv = proj[:, C_CKV:C_CKV + MLA_KV_RANK]
    ckv = ckv * lax.rsqrt(jnp.mean(ckv * ckv, axis=-1, keepdims=True) + RMS_EPS) * kvn_ref[...]
    kr = _rope_chunk(proj[:, C_KR:C_KR + LANE], tm_ref, qm_shift)
    kr_ref[...] = kr

    qm = _dot(cq.astype(bf16), wuq_ref[...])
    qm_ref[:, 0:MLA_NN] = qm[:, 0:MLA_NN].astype(bf16)
    for j in range(2):
        c0 = MLA_NN + LANE * j
        qm_ref[:, c0:c0 + LANE] = _rope_chunk(qm[:, c0:c0 + LANE], tm_ref, qm_shift).astype(bf16)
    kvl_ref[...] = _dot(ckv.astype(bf16), wukv_ref[...]).astype(bf16)

    @pl.when(i < T_CTX // TOK_TILE)
    def _():
        ckv_ref[...] = ckv
        kr_t = kr_ref[...].T
        for b in range(TOK_TILE // CTX_L):
            krc_ref[b, 0] = kr_t[0:MLA_ROPE, CTX_L * b:CTX_L * (b + 1)]


def _rope_block_index(i):
    tiles_ctx = T_CTX // TOK_TILE
    per_seq = DEN_L // TOK_TILE
    return jnp.where(i < tiles_ctx, 0, 1 + (i - tiles_ctx) % per_seq)


def _ab_proj(xc, xd, mods0, w_in, q_norm, kv_norm, w_uq_p, w_ukv_p):
    tab_a, qa_shift = _rope_table_array(A_HD)
    tab_m, qm_shift = _rope_table_array(MLA_ROPE)
    row_spec = lambda w: pl.BlockSpec((TOK_TILE, w), lambda i: (i, 0))
    xc_spec, xd_spec = _two_stream_specs(TOK_TILE, D)
    tab_spec = pl.BlockSpec((3, TOK_TILE, LANE), lambda i: (0, _rope_block_index(i), 0))
    outs = [(512, bf16), (LANE, f32), (LANE, f32), (LANE, f32), (LANE, f32), (768, bf16), (1024, bf16)]
    last_ctx = T_CTX // TOK_TILE - 1
    per = TOK_TILE // CTX_L
    ctx_spec = pl.BlockSpec((TOK_TILE, LANE), lambda i: (jnp.minimum(i, last_ctx), 0))
    krc_spec = pl.BlockSpec((per, 1, MLA_ROPE, CTX_L), lambda i: (jnp.minimum(i, last_ctx), 0, 0, 0))
    return pl.pallas_call(
        functools.partial(_ab_proj_kernel, qa_shift=qa_shift, qm_shift=qm_shift),
        grid=(T_ALL // TOK_TILE,),
        in_specs=[xc_spec, xd_spec, _full((8, 6 * D), 1), _full(w_in.shape, 1), _full((1, MLA_Q_RANK), 1),
                  _full((1, MLA_KV_RANK), 1), _full((MLA_Q_RANK, 768), 1), _full((MLA_KV_RANK, 1024), 1),
                  tab_spec, tab_spec],
        out_specs=[ctx_spec if k == 3 else row_spec(w) for k, (w, _) in enumerate(outs)] + [krc_spec],
        out_shape=[jax.ShapeDtypeStruct((T_CTX if k == 3 else T_ALL, w), dt) for k, (w, dt) in enumerate(outs)]
        + [jax.ShapeDtypeStruct((N_CTX_B, 1, MLA_ROPE, CTX_L), f32)],
        scratch_shapes=[pltpu.VMEM((D, PROJ_W), bf16)],
        compiler_params=_params(1),
        name="ab_proj",
    )(xc, xd, mods0, w_in, q_norm, kv_norm, w_uq_p, w_ukv_p, tab_a, tab_m)


def _softmax_blocks(s_refs, p_refs, sink_col=None):
    m = s_refs[0][...].max(axis=-1, keepdims=True)
    for s_ref in s_refs[1:]:
        m = jnp.maximum(m, s_ref[...].max(axis=-1, keepdims=True))
    if sink_col is not None:
        m = jnp.maximum(m, sink_col)
    for s_ref, p_ref in zip(s_refs, p_refs):
        p_ref[...] = jnp.exp(s_ref[...] - m).astype(bf16)
    return 0.0 if sink_col is None else jnp.exp(sink_col - m)


def _with_ones(v, axis=1):
    return jnp.concatenate([v, jnp.ones(v.shape, v.dtype)], axis=axis)


def _normalise(o_aug, extra, width):
    return o_aug[:, 0:width] * (1.0 / (o_aug[:, width:width + 1] + extra))


def _sink_column(sink_ref, rows_per_head):
    return jnp.concatenate([jnp.full((rows_per_head, 1), sink_ref[h], f32) for h in range(A_HEADS)], axis=0)


def _mla_q(qm_ref, h):
    rows = qm_ref.shape[0]
    return jnp.concatenate([qm_ref[:, MLA_NOPE * h:MLA_NOPE * (h + 1)],
                            qm_ref[:, MLA_NN + MLA_ROPE * h:MLA_NN + MLA_ROPE * (h + 1)],
                            jnp.zeros((rows, LANE - MLA_NOPE - MLA_ROPE), bf16)], axis=1)


def _mla_k(k_nope_h, k_rope):
    rows = k_nope_h.shape[0]
    return jnp.concatenate([k_nope_h, k_rope, jnp.zeros((rows, LANE - MLA_NOPE - MLA_ROPE), bf16)], axis=1)


def _mix_out_ln(merged_ref, wout_ref, x, mods_ref, r, g_ref, b_ref):
    out = _dot(merged_ref[...], wout_ref[...])
    gate = mods_ref[pl.ds(r, 1), 2 * D:3 * D]
    return _layer_norm(ALPHA * x + gate * out, g_ref[...], b_ref[...])


def _ctx_attn_kernel(sink_ref, qa_ref, ka_ref, va_ref, qm_ref, kvl_ref, kr_ref, x_ref, mods_ref, wout_ref,
                     g_ref, b_ref, o_ref, nk_ref, nv_ref, merged_ref, sa_ref, sm_ref, pa_ref, pm_ref):
    nk_ref[0, 0] = ka_ref[...].T.reshape(A_KV_HEADS, A_HD, CTX_L)
    nv_ref[0, 0] = va_ref[...].T.reshape(A_KV_HEADS, A_HD, CTX_L)
    n = CTX_L
    ka = ka_ref[...].astype(bf16)
    va = va_ref[...].astype(bf16)
    for j in range(A_KV_HEADS):
        q4 = jnp.concatenate([qa_ref[:, A_HD * h:A_HD * (h + 1)] for h in range(A_GROUP * j, A_GROUP * (j + 1))],
                             axis=0)
        sa_ref[A_GROUP * n * j:A_GROUP * n * (j + 1), :] = _dot_nt(q4, ka[:, A_HD * j:A_HD * (j + 1)]) * A_SCALE
    kr = kr_ref[:, 0:MLA_ROPE].astype(bf16)
    for h in range(MLA_HEADS):
        k_cat = _mla_k(kvl_ref[:, MLA_NOPE * h:MLA_NOPE * (h + 1)], kr)
        sm_ref[n * h:n * (h + 1), :] = _dot_nt(_mla_q(qm_ref, h), k_cat) * MLA_SCALE
    sink_a = _softmax_blocks([sa_ref], [pa_ref], _sink_column(sink_ref, n))
    _softmax_blocks([sm_ref], [pm_ref])
    for j in range(A_KV_HEADS):
        rows = slice(A_GROUP * n * j, A_GROUP * n * (j + 1))
        o4 = _normalise(_dot(pa_ref[rows, :], _with_ones(va[:, A_HD * j:A_HD * (j + 1)])), sink_a[rows], A_HD)
        for g in range(A_GROUP):
            h = A_GROUP * j + g
            merged_ref[:, A_HD * h:A_HD * (h + 1)] = o4[n * g:n * (g + 1)].astype(bf16)
    for h in range(MLA_HEADS):
        rows = slice(n * h, n * (h + 1))
        v = _with_ones(kvl_ref[:, MLA_NN + MLA_V * h:MLA_NN + MLA_V * (h + 1)])
        merged_ref[:, MLA_NN + MLA_V * h:MLA_NN + MLA_V * (h + 1)] = (
            _normalise(_dot(pm_ref[rows, :], v), 0.0, MLA_V).astype(bf16))
    o_ref[...] = _mix_out_ln(merged_ref, wout_ref, x_ref[...], mods_ref, 0, g_ref, b_ref)


def _ctx_attn(sink, qa, ka, va, qm, kvl, kr, x_all, mods0, w_out, ln_g, ln_b):
    blk = lambda w: pl.BlockSpec((CTX_L, w), lambda b: (b, 0))
    cache_blk = pl.BlockSpec((1, 1, A_KV_HEADS, A_HD, CTX_L), lambda b: (b, 0, 0, 0, 0))
    cache_shape = jax.ShapeDtypeStruct((N_CTX_B, 1, A_KV_HEADS, A_HD, CTX_L), f32)
    return pl.pallas_call(
        _ctx_attn_kernel,
        grid=(N_CTX_B,),
        in_specs=[pl.BlockSpec(memory_space=pltpu.SMEM), blk(512), blk(LANE), blk(LANE), blk(768), blk(1024),
                  blk(LANE), blk(D), _full((8, 6 * D), 1), _full((D, D), 1), _full((1, D), 1), _full((1, D), 1)],
        out_specs=[blk(D), cache_blk, cache_blk],
        out_shape=[jax.ShapeDtypeStruct((T_CTX, D), f32), cache_shape, cache_shape],
        scratch_shapes=[pltpu.VMEM((CTX_L, D), bf16),
                        pltpu.VMEM((A_HEADS * CTX_L, CTX_L), f32), pltpu.VMEM((MLA_HEADS * CTX_L, CTX_L), f32),
                        pltpu.VMEM((A_HEADS * CTX_L, CTX_L), bf16), pltpu.VMEM((MLA_HEADS * CTX_L, CTX_L), bf16)],
        compiler_params=_params(1),
        name="ctx_attn",
    )(sink, qa, ka, va, qm, kvl, kr, x_all, mods0, w_out, ln_g, ln_b)


QB = 256
WIN = QB + 2 * WINDOW
DEN_BLK0 = T_CTX // DEN_L
MLA_KEYS = CTX_L + DEN_L


def _den_attn_kernel(sink_ref, qa_ref, ka_ref, va_ref, cak_ref, cav_ref, qm_ref, kvl_ref, kr_ref, cckv_ref, ckr_ref,
                     wukv_ref, x_ref, mods_ref, wout_ref, g_ref, b_ref, o_ref, merged_ref, kcat_ref, vcat_ref,
                     saw_ref, sac_ref, sm_ref, paw_ref, pac_ref, pm_ref):
    b = pl.program_id(0)
    n = pl.program_id(1)

    @pl.when(n == 0)
    def _():
        kvc = _dot(cckv_ref[0].astype(bf16), wukv_ref[...]).astype(bf16)
        kr_ctx = ckr_ref[0].T.astype(bf16)
        kr_lat = kr_ref[:, 0:MLA_ROPE].astype(bf16)
        for h in range(MLA_HEADS):
            ns = slice(MLA_NOPE * h, MLA_NOPE * (h + 1))
            vs = slice(MLA_NN + MLA_V * h, MLA_NN + MLA_V * (h + 1))
            kcat_ref[h, 0:CTX_L, :] = _mla_k(kvc[:, ns], kr_ctx)
            kcat_ref[h, CTX_L:MLA_KEYS, :] = _mla_k(kvl_ref[:, ns], kr_lat)
            vcat_ref[h, 0:CTX_L, :] = _with_ones(kvc[:, vs])
            vcat_ref[h, CTX_L:MLA_KEYS, :] = _with_ones(kvl_ref[:, vs])

    start = pl.multiple_of(jnp.clip(QB * n - WINDOW, 0, DEN_L - WIN), WINDOW)
    grp_rows = A_GROUP * QB
    qpos = QB * n + (lax.broadcasted_iota(jnp.int32, (grp_rows, WIN), 0) & (QB - 1))
    kpos = start + lax.broadcasted_iota(jnp.int32, (grp_rows, WIN), 1)
    valid = jnp.abs(qpos - kpos) <= WINDOW
    kwin = ka_ref[pl.ds(start, WIN), :].astype(bf16)
    vwin = va_ref[pl.ds(start, WIN), :].astype(bf16)
    kctx_t = [cak_ref[0, j].astype(bf16) for j in range(A_KV_HEADS)]
    vctx_t = [cav_ref[0, j].astype(bf16) for j in range(A_KV_HEADS)]
    for j in range(A_KV_HEADS):
        sl = slice(A_HD * j, A_HD * (j + 1))
        rows = slice(grp_rows * j, grp_rows * (j + 1))
        q4 = jnp.concatenate([qa_ref[:, A_HD * h:A_HD * (h + 1)] for h in range(A_GROUP * j, A_GROUP * (j + 1))],
                             axis=0)
        saw_ref[rows, :] = jnp.where(valid, _dot_nt(q4, kwin[:, sl]) * A_SCALE, NEG_INF)
        sac_ref[rows, :] = _dot(q4, kctx_t[j]) * A_SCALE
    for h in range(MLA_HEADS):
        sm_ref[QB * h:QB * (h + 1), :] = _dot_nt(_mla_q(qm_ref, h), kcat_ref[h]) * MLA_SCALE
    sink_a = _softmax_blocks([saw_ref, sac_ref], [paw_ref, pac_ref], _sink_column(sink_ref, QB))
    _softmax_blocks([sm_ref], [pm_ref])
    for j in range(A_KV_HEADS):
        sl = slice(A_HD * j, A_HD * (j + 1))
        rows = slice(grp_rows * j, grp_rows * (j + 1))
        o_aug = (_dot(paw_ref[rows, :], _with_ones(vwin[:, sl]))
                 + _dot_nt(pac_ref[rows, :], _with_ones(vctx_t[j], axis=0)))
        o4 = _normalise(o_aug, sink_a[rows], A_HD)
        for g in range(A_GROUP):
            h = A_GROUP * j + g
            merged_ref[:, A_HD * h:A_HD * (h + 1)] = o4[QB * g:QB * (g + 1)].astype(bf16)
    for h in range(MLA_HEADS):
        rows = slice(QB * h, QB * (h + 1))
        o = _normalise(_dot(pm_ref[rows, :], vcat_ref[h]), 0.0, MLA_V)
        merged_ref[:, MLA_NN + MLA_V * h:MLA_NN + MLA_V * (h + 1)] = o.astype(bf16)
    o_ref[...] = _mix_out_ln(merged_ref, wout_ref, x_ref[...], mods_ref, 1 + b, g_ref, b_ref)


def _den_attn(sink, qa, ka, va, cache_k, cache_v, qm, kvl, kr, cache_ckv, cache_kr, w_ukv_p, x_all, mods0, w_out,
              ln_g, ln_b):
    nq = DEN_L // QB
    qblk = lambda w: pl.BlockSpec((QB, w), lambda b, n: (T_CTX // QB + b * nq + n, 0))
    seq = lambda w: pl.BlockSpec((DEN_L, w), lambda b, n: (DEN_BLK0 + b, 0))
    cache = lambda w: pl.BlockSpec((1, CTX_L, w), lambda b, n: (b, 0, 0))
    cache_a = pl.BlockSpec((1, A_KV_HEADS, A_HD, CTX_L), lambda b, n: (b, 0, 0, 0))
    return pl.pallas_call(
        _den_attn_kernel,
        grid=(N_DEN_B, nq),
        in_specs=[pl.BlockSpec(memory_space=pltpu.SMEM), qblk(512), seq(LANE), seq(LANE), cache_a, cache_a,
                  qblk(768), seq(1024), seq(LANE), cache(MLA_KV_RANK),
                  pl.BlockSpec((1, MLA_ROPE, CTX_L), lambda b, n: (b, 0, 0)),
                  _full((MLA_KV_RANK, 1024), 2), pl.BlockSpec((QB, D), lambda b, n: (b * nq + n, 0)),
                  _full((8, 6 * D), 2), _full((D, D), 2), _full((1, D), 2),
                  _full((1, D), 2)],
        out_specs=pl.BlockSpec((QB, D), lambda b, n: (b * nq + n, 0)),
        out_shape=jax.ShapeDtypeStruct((T_DEN, D), f32),
        scratch_shapes=[pltpu.VMEM((QB, D), bf16), pltpu.VMEM((MLA_HEADS, MLA_KEYS, LANE), bf16),
                        pltpu.VMEM((MLA_HEADS, MLA_KEYS, 2 * MLA_V), bf16)]
        + [pltpu.VMEM((A_HEADS * QB, w), dt) for dt in (f32, bf16) for w in (WIN, CTX_L, MLA_KEYS)],
        compiler_params=_params(2),
        name="den_attn",
    )(sink, qa, ka, va, cache_k, cache_v, qm, kvl, kr, cache_ckv, cache_kr, w_ukv_p, x_all, mods0, w_out, ln_g, ln_b)


SUBTILE = 256


def _route(x1, mrow, rw_ref, rb_ref):
    sh, sc = mrow[:, 3 * D:4 * D], mrow[:, 4 * D:5 * D]
    h = x1 * (1.0 + sc) + sh
    h_hi = h.astype(bf16)
    h_lo = (h - h_hi.astype(f32)).astype(bf16)
    logits = _dot(h_hi, rw_ref[0]) + (_dot(h_hi, rw_ref[1]) + _dot(h_lo, rw_ref[0]))
    scores = jax.nn.sigmoid(logits)
    lane = lax.broadcasted_iota(jnp.int32, scores.shape, 1).astype(f32)
    sel = jnp.where(lane < N_EXPERTS, scores + rb_ref[...], -jnp.inf)
    gates = jnp.zeros_like(scores)
    for _ in range(TOP_K):
        m = sel.max(axis=-1, keepdims=True)
        idx = jnp.where(sel == m, lane, float(LANE)).min(axis=-1, keepdims=True)
        hit = lane == idx
        gates = jnp.where(hit, scores, gates)
        sel = jnp.where(hit, -jnp.inf, sel)
    return h_hi, gates / gates.sum(axis=-1, keepdims=True) * ROUTED_SCALE


def _router_kernel(xc_ref, xd_ref, mods_ref, rw_ref, rb_ref, x_ref, h_ref, gates_ref):
    i = pl.program_id(0)
    r = _mod_row(i, TOK_TILE)
    mrow = mods_ref[pl.ds(r, 1), :]
    x1 = _pick(i, TOK_TILE, xc_ref, xd_ref)
    x_ref[...] = x1
    h_ref[...], gates_ref[...] = _route(x1, mrow, rw_ref, rb_ref)


def _router(x1c, x1d, mods_l, router_w_p, router_b_p):
    row_spec = lambda w: pl.BlockSpec((TOK_TILE, w), lambda i: (i, 0))
    xc_spec, xd_spec = _two_stream_specs(TOK_TILE, D)
    return pl.pallas_call(
        _router_kernel,
        grid=(T_ALL // TOK_TILE,),
        in_specs=[xc_spec, xd_spec, _full((8, 6 * D), 1), _full((2, D, LANE), 1), _full((1, LANE), 1)],
        out_specs=[row_spec(D), row_spec(D), row_spec(LANE)],
        out_shape=[jax.ShapeDtypeStruct((T_ALL, D), f32), jax.ShapeDtypeStruct((T_ALL, D), bf16),
                   jax.ShapeDtypeStruct((T_ALL, LANE), f32)],
        compiler_params=_params(1),
        name="router",
    )(x1c, x1d, mods_l, router_w_p, router_b_p)


MOE_TOK = 1536
MOE_EG = 8
MOE_VMEM_LIMIT = 60 * 1024 * 1024
MOE_TILE = 512
MOE_FF = MOE_EG * EXPERT_FF


def _moe_kernel(h_ref, gates_ref, mods_ref, wg_ref, wu_ref, wd_ref, sg_ref, su_ref, sd_ref, o_ref):
    p = pl.program_id(0)
    e = pl.program_id(1)
    n_tiles = MOE_TOK // MOE_TILE

    def gate_f(t):
        r = _mod_row(p * n_tiles + t, MOE_TILE)
        return mods_ref[pl.ds(r, 1), 5 * D:6 * D]

    def rows_of(t):
        if isinstance(t, int):
            return pl.ds(t * MOE_TILE, MOE_TILE)
        return pl.ds(pl.multiple_of(t * MOE_TILE, MOE_TILE), MOE_TILE)

    @pl.when(e == 0)
    def _():
        sg = sg_ref[...].astype(bf16)
        su = su_ref[...].astype(bf16)
        sd = sd_ref[...].astype(bf16)

        def body(t, c):
            rows = rows_of(t)
            ht = h_ref[rows, :]
            hid = _silu(_dot(ht, sg)) * _dot(ht, su)
            o_ref[rows, :] = gate_f(t) * _dot(hid.astype(bf16), sd)
            return c

        lax.fori_loop(0, n_tiles, body, 0)

    wg = jnp.concatenate([wg_ref[k].astype(bf16) for k in range(MOE_EG)], axis=1)
    wu = jnp.concatenate([wu_ref[k].astype(bf16) for k in range(MOE_EG)], axis=1)
    wd = jnp.concatenate([wd_ref[k].astype(bf16) for k in range(MOE_EG)], axis=0)
    lane = lax.broadcasted_iota(jnp.int32, (MOE_TILE, LANE), 1)

    def body(t, c):
        rows = rows_of(t)
        ht = h_ref[rows, :]
        hid = _silu(_dot(ht, wg)) * _dot(ht, wu)
        gt = gates_ref[rows, :]
        parts = []
        for k in range(MOE_EG):
            col = jnp.where(lane == e * MOE_EG + k, gt, 0.0).sum(axis=-1, keepdims=True)
            parts.append((hid[:, EXPERT_FF * k:EXPERT_FF * (k + 1)] * col).astype(bf16))
        o_ref[rows, :] += gate_f(t) * _dot(jnp.concatenate(parts, axis=1), wd)
        return c

    for t in range(n_tiles):
        body(t, 0)


def _moe(l, h, gates, mods_l, wg, wu, wd, sg, su, sd):
    tok = lambda w: pl.BlockSpec((MOE_TOK, w), lambda p, e: (p, 0))
    return pl.pallas_call(
        _moe_kernel,
        grid=(T_ALL // MOE_TOK, N_EXPERTS // MOE_EG),
        in_specs=[tok(D), tok(LANE), _full((8, 6 * D), 2),
                  pl.BlockSpec((None, MOE_EG, D, EXPERT_FF), lambda p, e: (l, e, 0, 0)),
                  pl.BlockSpec((None, MOE_EG, D, EXPERT_FF), lambda p, e: (l, e, 0, 0)),
                  pl.BlockSpec((None, MOE_EG, EXPERT_FF, D), lambda p, e: (l, e, 0, 0)),
                  pl.BlockSpec((None, D, SHARED_FF), lambda p, e: (l, 0, 0)),
                  pl.BlockSpec((None, D, SHARED_FF), lambda p, e: (l, 0, 0)),
                  pl.BlockSpec((None, SHARED_FF, D), lambda p, e: (l, 0, 0))],
        out_specs=tok(D),
        out_shape=jax.ShapeDtypeStruct((T_ALL, D), f32),
        compiler_params=pltpu.CompilerParams(dimension_semantics=("arbitrary", "arbitrary"),
                                             vmem_limit_bytes=MOE_VMEM_LIMIT),
        name="moe",
    )(h, gates, mods_l, wg, wu, wd, sg, su, sd)


def _moe_finish_kernel(x_ref, acc_ref, g_ref, b_ref, o_ref):
    o_ref[...] = _layer_norm(ALPHA * x_ref[...] + acc_ref[...], g_ref[...], b_ref[...])


def _moe_finish(x1, acc, tile0, n_rows, ln_g, ln_b):
    src = pl.BlockSpec((TOK_TILE, D), lambda i: (tile0 + i, 0))
    return pl.pallas_call(
        _moe_finish_kernel,
        grid=(n_rows // TOK_TILE,),
        in_specs=[src, src, _full((1, D), 1), _full((1, D), 1)],
        out_specs=pl.BlockSpec((TOK_TILE, D), lambda i: (i, 0)),
        out_shape=jax.ShapeDtypeStruct((n_rows, D), f32),
        compiler_params=_params(1),
        name="moe_finish",
    )(x1, acc, ln_g, ln_b)


def _router_weights(l, router_w, router_bias):
    rw = jnp.pad(router_w[l], ((0, 0), (0, LANE - N_EXPERTS)))
    rw_hi, rw_lo = _split_bf16(rw)
    rb = jnp.pad(router_bias[l], (0, LANE - N_EXPERTS)).reshape(1, LANE)
    return jnp.stack([rw_hi, rw_lo]), rb


S5_Q = 8
S5_NGB = D // LANE


def _s5_in_kernel(x_ref, acc_ref, lg_ref, lb_ref, mods_ref, w_ref, x2_ref, u_ref, u2_ref, slab_ref, *, row_of,
                  seq_len):
    r = row_of(pl.program_id(0))
    mrow = mods_ref[pl.ds(r, 1), :]
    sh, sc = mrow[:, 0:D], mrow[:, D:2 * D]
    x2 = _layer_norm(ALPHA * x_ref[...] + acc_ref[...], lg_ref[...], lb_ref[...])
    x2_ref[...] = x2
    h = (x2 * (1.0 + sc) + sh).astype(bf16)
    u = _dot(h, w_ref[...])
    u_ref[...] = u
    for s in range(S5_NGB):
        slab_ref[s] = u[:, LANE * s:LANE * (s + 1)]
    kt = seq_len // S5_Q
    for s in range(S5_NGB):
        for q in range(TOK_TILE // seq_len):
            for j in range(S5_Q):
                u2_ref[s, q * kt:(q + 1) * kt, LANE * j:LANE * (j + 1)] = (
                    slab_ref[s, pl.ds(q * seq_len + j, kt, stride=S5_Q), :].astype(bf16))


def _s5_in(x1, acc, ln_g, ln_b, tile0, mods1, w_in_c, n_b, seq_len, row_of):
    n_tiles = n_b * seq_len // TOK_TILE
    chunks = TOK_TILE // S5_Q
    src = pl.BlockSpec((TOK_TILE, D), lambda i: (tile0 + i, 0))
    dst = pl.BlockSpec((TOK_TILE, D), lambda i: (i, 0))
    return pl.pallas_call(
        functools.partial(_s5_in_kernel, row_of=row_of, seq_len=min(seq_len, TOK_TILE)),
        grid=(n_tiles,),
        in_specs=[src, src, _full((1, D), 1), _full((1, D), 1), _full((8, 6 * D), 1), _full((D, D), 1)],
        out_specs=[dst, dst, pl.BlockSpec((S5_NGB, chunks, D), lambda i: (0, i, 0))],
        out_shape=[jax.ShapeDtypeStruct((n_b * seq_len, D), f32), jax.ShapeDtypeStruct((n_b * seq_len, D), f32),
                   jax.ShapeDtypeStruct((S5_NGB, n_tiles * chunks, D), bf16)],
        scratch_shapes=[pltpu.VMEM((S5_NGB, TOK_TILE, LANE), f32)],
        compiler_params=_params(1),
        name="s5_in",
    )(x1, acc, ln_g, ln_b, mods1, w_in_c)


S5_GL = (LANE // S5_CH) * S5_P
S5_ROWS_C = (CTX_L // S5_Q) * N_CTX_B
S5_ROWS_D = (DEN_L // S5_Q) * N_DEN_B


def _s5_scan_kernel(lam_ref, bt_ref, ctr_ref, cti_ref, uc_ref, ud_ref, h0_ref,
                    yc_ref, yd_ref, st_ref, win_ref, mso_ref, wit_ref, a_ref, s_ref, hp_ref):
    gl = S5_GL
    rowg = lax.shift_right_logical(lax.broadcasted_iota(jnp.int32, (LANE, gl), 0), 4)
    colg = lax.shift_right_logical(lax.broadcasted_iota(jnp.int32, (LANE, gl), 1), 6)
    same_group = rowg == colg
    reps = LANE // S5_CH

    def expand(t):
        return jnp.where(same_group, jnp.concatenate([t] * reps, axis=0), 0.0)

    def expand_c(t):
        return jnp.where(same_group, jnp.concatenate([t] * reps, axis=1), 0.0)

    for d in range(2):
        fwd = d == 0
        lre, lim = lam_ref[d, 0:1, :], lam_ref[d, 1:2, :]
        dt = jnp.exp(lam_ref[d, 2:3, :])
        a, w = lre * dt, lim * dt
        pre = [jnp.exp(m * a) * jnp.cos(m * w) for m in range(S5_Q + 1)]
        pim = [jnp.exp(m * a) * jnp.sin(m * w) for m in range(S5_Q + 1)]
        xr, xi = pre[1] - 1.0, pim[1]
        den = lre * lre + lim * lim
        cfr, cfi = (xr * lre + xi * lim) / den, (xi * lre - xr * lim) / den
        btr, bti = bt_ref[0, d], bt_ref[1, d]
        bexp_r = expand(cfr * btr - cfi * bti)
        bexp_i = expand(cfr * bti + cfi * btr)
        cexp_r, cexp_i = expand_c(ctr_ref[d]), expand_c(cti_ref[d])
        for m in range(S5_Q + 1):
            a_ref[m, :, 0:gl] = cexp_r * pre[m] - cexp_i * pim[m]
            a_ref[m, :, gl:2 * gl] = -(cexp_r * pim[m] + cexp_i * pre[m])
        for j in range(S5_Q):
            m = S5_Q - 1 - j if fwd else j
            win_ref[LANE * j:LANE * (j + 1), 0:gl] = (pre[m] * bexp_r - pim[m] * bexp_i).astype(bf16)
            win_ref[LANE * j:LANE * (j + 1), gl:2 * gl] = (pre[m] * bexp_i + pim[m] * bexp_r).astype(bf16)
        for j in range(S5_Q):
            m = j + 1 if fwd else S5_Q - j
            mso_ref[LANE * j:LANE * (j + 1), :] = a_ref[m].astype(bf16)
        b2 = jnp.concatenate([bexp_r, bexp_i], axis=1).astype(bf16)
        kt = [_dot_nt(b2, a_ref[tau].astype(bf16)) for tau in range(S5_Q)]
        for j in range(S5_Q):
            for jp in range(S5_Q):
                tau = jp - j if fwd else j - jp
                blk = slice(LANE * j, LANE * (j + 1)), slice(LANE * jp, LANE * (jp + 1))
                if fwd:
                    wit_ref[blk] = kt[tau] if tau >= 0 else jnp.zeros((LANE, LANE), f32)
                elif tau >= 0:
                    wit_ref[blk] = wit_ref[blk] + kt[tau]

        l8r, l8i = pre[S5_Q], pim[S5_Q]

        nsl = gl // LANE

        def slabs(ref, rs, first):
            return jnp.concatenate([ref[first + sl, rs, :] for sl in range(nsl)], axis=1)

        def put_slabs(ref, rs, first, val):
            for sl in range(nsl):
                ref[first + sl, rs, :] = val[:, LANE * sl:LANE * (sl + 1)]

        def advance(hr, hi_, sr, si):
            return l8r * hr - l8i * hi_ + sr, l8r * hi_ + l8i * hr + si

        def run(u_ref, y_ref, n_b, n_k, h_init):
            rows = n_b * n_k
            s = _dot(u_ref[0], win_ref[...])
            if n_b % SUB == 0:
                pitch = n_k + 1
                for bb in range(n_b):
                    dst = slice(bb * pitch, bb * pitch + n_k)
                    put_slabs(s_ref, dst, 0, s[bb * n_k:(bb + 1) * n_k, 0:gl])
                    put_slabs(s_ref, dst, nsl, s[bb * n_k:(bb + 1) * n_k, gl:2 * gl])

                def step(i, carry):
                    hr, hi_ = carry
                    rs = pl.ds(i if fwd else n_k - 1 - i, n_b, stride=pitch)
                    put_slabs(hp_ref, rs, 0, hr)
                    put_slabs(hp_ref, rs, nsl, hi_)
                    return advance(hr, hi_, slabs(s_ref, rs, 0), slabs(s_ref, rs, nsl))

                h_fin = lax.fori_loop(0, n_k, step, h_init)
                hp = jnp.concatenate(
                    [jnp.concatenate([hp_ref[sl, bb * pitch:bb * pitch + n_k, :] for sl in range(2 * nsl)], axis=1)
                     for bb in range(n_b)], axis=0).astype(bf16)
            else:
                put_slabs(s_ref, slice(0, rows), 0, s[:, 0:gl])
                put_slabs(s_ref, slice(0, rows), nsl, s[:, gl:2 * gl])
                n_it = n_k // SUB

                def step(i, carry):
                    it = i if fwd else n_it - 1 - i
                    out = []
                    for bb in range(n_b):
                        hr, hi_ = carry[bb]
                        rs = pl.ds(pl.multiple_of(bb * n_k + it * SUB, SUB), SUB)
                        s_re, s_im = slabs(s_ref, rs, 0), slabs(s_ref, rs, nsl)
                        prev_r, prev_i = [None] * SUB, [None] * SUB
                        for sub in (range(SUB) if fwd else reversed(range(SUB))):
                            prev_r[sub], prev_i[sub] = hr, hi_
                            hr, hi_ = advance(hr, hi_, s_re[sub:sub + 1], s_im[sub:sub + 1])
                        put_slabs(hp_ref, rs, 0, jnp.concatenate(prev_r, axis=0))
                        put_slabs(hp_ref, rs, nsl, jnp.concatenate(prev_i, axis=0))
                        out.append((hr, hi_))
                    return tuple(out)

                fin = lax.fori_loop(0, n_it, step, tuple((h_init[0][bb:bb + 1], h_init[1][bb:bb + 1])
                                                          for bb in range(n_b)))
                h_fin = (jnp.concatenate([f[0] for f in fin], axis=0), jnp.concatenate([f[1] for f in fin], axis=0))
                hp = jnp.concatenate([hp_ref[sl, 0:rows, :] for sl in range(2 * nsl)], axis=1).astype(bf16)
            y = _dot_nt(hp, mso_ref[...])
            if fwd:
                y_ref[0] = y
            else:
                y_ref[0] += y
            return h_fin

        zeros = jnp.zeros((N_CTX_B, gl), f32)
        hr, hi_ = run(uc_ref, yc_ref, N_CTX_B, CTX_L // S5_Q, (zeros, zeros))
        st_ref[d, 0] = hr
        st_ref[d, 1] = hi_
        run(ud_ref, yd_ref, N_DEN_B, DEN_L // S5_Q, (h0_ref[d, 0], h0_ref[d, 1]))

    wit = wit_ref[...].astype(bf16)
    yc_ref[0] += _dot(uc_ref[0], wit)
    yd_ref[0] += _dot(ud_ref[0], wit)


def _s5_scan(lam3, bt, ct_re, ct_im, u2c, u2d, h0):
    gl = S5_GL
    vec = pl.BlockSpec((2, 3, gl), lambda g: (0, 0, g))
    tab = pl.BlockSpec((2, 2, S5_CH, gl), lambda g: (0, 0, 0, g))
    ctab = pl.BlockSpec((2, LANE, S5_P), lambda g: (0, g, 0))
    rows = lambda n: pl.BlockSpec((1, n, D), lambda g: (g, 0, 0))
    return pl.pallas_call(
        _s5_scan_kernel,
        grid=(S5_NGB,),
        in_specs=[vec, tab, ctab, ctab, rows(S5_ROWS_C), rows(S5_ROWS_D),
                  pl.BlockSpec((2, 2, N_DEN_B, gl), lambda g: (0, 0, 0, g))],
        out_specs=[rows(S5_ROWS_C), rows(S5_ROWS_D), pl.BlockSpec((2, 2, N_CTX_B, gl), lambda g: (0, 0, 0, g))],
        out_shape=[jax.ShapeDtypeStruct((S5_NGB, S5_ROWS_C, D), f32), jax.ShapeDtypeStruct((S5_NGB, S5_ROWS_D, D), f32),
                   jax.ShapeDtypeStruct((2, 2, N_CTX_B, S5_G * S5_P), f32)],
        scratch_shapes=[pltpu.VMEM((D, 2 * gl), bf16), pltpu.VMEM((D, 2 * gl), bf16), pltpu.VMEM((D, D), f32),
                        pltpu.VMEM((S5_Q + 1, LANE, 2 * gl), f32),
                        pltpu.VMEM((2 * gl // LANE, S5_ROWS_C + 2 * N_CTX_B, LANE), f32),
                        pltpu.VMEM((2 * gl // LANE, S5_ROWS_C + 2 * N_CTX_B, LANE), f32)],
        compiler_params=_params(1),
        name="s5_scan",
    )(lam3, bt, ct_re, ct_im, u2c, u2d, h0)


def _gelu_tanh(x):
    return 0.5 * x * (1.0 + jnp.tanh(np.sqrt(2.0 / np.pi).astype(np.float32) * (x + 0.044715 * (x * x * x))))


def _s5_out_kernel(xc_ref, xd_ref, uc_ref, ud_ref, yc_ref, yd_ref, mods_ref, dsk_ref, wout_ref, g_ref, b_ref, rw_ref,
                   rb_ref, x1_ref, h_ref, gates_ref, slab_ref, wob_ref):
    i = pl.program_id(0)

    @pl.when(i == 0)
    def _():
        wob_ref[...] = wout_ref[...].astype(bf16)

    is_ctx = i < T_CTX // TOK_TILE
    r = _mod_row(i, TOK_TILE)
    mrow = mods_ref[pl.ds(r, 1), :]
    u = _pick(i, TOK_TILE, uc_ref, ud_ref)

    kt = CTX_L // S5_Q
    for s in range(S5_NGB):
        for q in range(TOK_TILE // CTX_L):
            for j in range(S5_Q):
                blk = (s, slice(q * kt, (q + 1) * kt), slice(LANE * j, LANE * (j + 1)))
                slab_ref[s, pl.ds(q * CTX_L + j, kt, stride=S5_Q), :] = jnp.where(is_ctx, yc_ref[blk], yd_ref[blk])
    halves = [slice(a, a + SUBTILE) for a in range(0, TOK_TILE, SUBTILE)]
    zs = []
    for rows in halves:
        y = jnp.concatenate([slab_ref[s, rows, :] for s in range(S5_NGB)], axis=1) + dsk_ref[...] * u[rows]
        zs.append(_dot(_gelu_tanh(y).astype(bf16), wob_ref[...]))
    for rows, z in zip(halves, zs):
        out = z[:, 0:D] * jax.nn.sigmoid(z[:, D:2 * D])
        x = jnp.where(is_ctx, xc_ref[rows, :], xd_ref[rows, :])
        x1 = _layer_norm(ALPHA * x + mrow[:, 2 * D:3 * D] * out, g_ref[...], b_ref[...])
        x1_ref[rows, :] = x1
        h_ref[rows, :], gates_ref[rows, :] = _route(x1, mrow, rw_ref, rb_ref)


def _s5_out(xc, xd, uc, ud, yc, yd, mods1, d_skip, w_out_c, ln_g, ln_b, rw, rb):
    row_spec = lambda w: pl.BlockSpec((TOK_TILE, w), lambda i: (i, 0))
    uc_spec, ud_spec = _two_stream_specs(TOK_TILE, D)
    n_ctx = T_CTX // TOK_TILE
    chunks = TOK_TILE // S5_Q
    return pl.pallas_call(
        _s5_out_kernel,
        grid=(T_ALL // TOK_TILE,),
        in_specs=[uc_spec, ud_spec, uc_spec, ud_spec,
                  pl.BlockSpec((S5_NGB, chunks, D), lambda i: (0, jnp.minimum(i, n_ctx - 1), 0)),
                  pl.BlockSpec((S5_NGB, chunks, D), lambda i: (0, jnp.maximum(i - n_ctx, 0), 0)),
                  _full((8, 6 * D), 1), _full((1, D), 1), _full((D, 2 * D), 1), _full((1, D), 1), _full((1, D), 1),
                  _full((2, D, LANE), 1), _full((1, LANE), 1)],
        out_specs=[row_spec(D), row_spec(D), row_spec(LANE)],
        out_shape=[jax.ShapeDtypeStruct((T_ALL, D), f32), jax.ShapeDtypeStruct((T_ALL, D), bf16),
                   jax.ShapeDtypeStruct((T_ALL, LANE), f32)],
        scratch_shapes=[pltpu.VMEM((S5_NGB, TOK_TILE, LANE), f32), pltpu.VMEM((D, 2 * D), bf16)],
        compiler_params=_params(1),
        name="s5_out",
    )(xc, xd, uc, ud, yc, yd, mods1, d_skip, w_out_c, ln_g, ln_b, rw, rb)


def kernel(x_prompt, x_sample, c, cache_attn_k, cache_attn_v, cache_mla_ckv, cache_mla_krope, state_ssm, c_ctx,
           ada_w, ada_b, ln_mix_g, ln_mix_b, ln_ffn_g, ln_ffn_b, w_in_ab, attn_sink, mla_q_norm, mla_kv_norm,
           mla_w_uq, mla_w_ukv, w_out_ab, w_in_c, s5_lam_re, s5_lam_im, s5_log_dt, s5_b_re, s5_b_im, s5_c_re,
           s5_c_im, s5_d, w_out_c, router_w, router_bias, exp_w_gate, exp_w_up, exp_w_down, sh_w_gate, sh_w_up,
           sh_w_down):
    row = lambda v: v.reshape(1, -1)
    xc, xd = x_prompt.reshape(T_CTX, D), x_sample.reshape(T_DEN, D)
    cvec8 = jnp.concatenate([c_ctx[None, :], c, jnp.zeros((8 - 1 - N_DEN_B, D), f32)], axis=0)
    mods = _adaln(cvec8, ada_w, ada_b)

    uq = mla_w_uq[0].reshape(MLA_Q_RANK, MLA_HEADS, MLA_NOPE + MLA_ROPE)
    w_uq_p = jnp.concatenate([uq[:, :, :MLA_NOPE].reshape(MLA_Q_RANK, -1), uq[:, :, MLA_NOPE:].reshape(MLA_Q_RANK, -1)],
                             axis=1).astype(bf16)
    ukv = mla_w_ukv[0].reshape(MLA_KV_RANK, MLA_HEADS, MLA_NOPE + MLA_V)
    w_ukv_p = jnp.concatenate([ukv[:, :, :MLA_NOPE].reshape(MLA_KV_RANK, -1),
                               ukv[:, :, MLA_NOPE:].reshape(MLA_KV_RANK, -1)], axis=1).astype(bf16)
    qa, ka, va, ckv, kr, qm, kvl, kr_ctx_t = _ab_proj(xc, xd, mods[0], w_in_ab[0], row(mla_q_norm[0]), row(mla_kv_norm[0]),
                                            w_uq_p, w_ukv_p)
    w_out_b = w_out_ab[0].astype(bf16)
    g0, b0 = row(ln_mix_g[0]), row(ln_mix_b[0])
    x1c, nk_t, nv_t = _ctx_attn(attn_sink[0], qa, ka, va, qm, kvl, kr, xc, mods[0], w_out_b, g0, b0)
    new_attn_k = jnp.transpose(nk_t, (0, 1, 4, 2, 3))
    new_attn_v = jnp.transpose(nv_t, (0, 1, 4, 2, 3))
    x1d = _den_attn(attn_sink[0], qa, ka, va,
                    jnp.transpose(cache_attn_k[:, 0], (0, 2, 3, 1)), jnp.transpose(cache_attn_v[:, 0], (0, 2, 3, 1)),
                    qm, kvl, kr, cache_mla_ckv[:, 0], jnp.transpose(cache_mla_krope[:, 0], (0, 2, 1)), w_ukv_p, xd, mods[0], w_out_b, g0, b0)
    rw0, rb0 = _router_weights(0, router_w, router_bias)
    x1, h, gates = _router(x1c, x1d, mods[0], rw0, rb0)
    acc = _moe(0, h, gates, mods[0], exp_w_gate, exp_w_up, exp_w_down, sh_w_gate, sh_w_up, sh_w_down)

    w_in_c_b = w_in_c[0].astype(bf16)
    lg0, lb0 = row(ln_ffn_g[0]), row(ln_ffn_b[0])
    n_ctx_tiles = T_CTX // TOK_TILE
    x2c, uc, u2c = _s5_in(x1, acc, lg0, lb0, 0, mods[1], w_in_c_b, N_CTX_B, CTX_L, lambda i: 0)
    x2d, ud, u2d = _s5_in(x1, acc, lg0, lb0, n_ctx_tiles, mods[1], w_in_c_b, N_DEN_B, DEN_L,
                          lambda i: 1 + i // (DEN_L // TOK_TILE))
    gp = S5_G * S5_P
    bt = jnp.transpose(jnp.stack([s5_b_re[0], s5_b_im[0]]), (0, 1, 4, 2, 3)).reshape(2, 2, S5_CH, gp)
    lam3 = jnp.stack([s5_lam_re[0].reshape(2, gp), s5_lam_im[0].reshape(2, gp),
                      jnp.repeat(s5_log_dt[0], S5_P, axis=-1).reshape(2, gp)], axis=1)
    chan_major_c = lambda t: t[0].reshape(2, S5_G * S5_CH, S5_P)
    h0 = jnp.transpose(state_ssm[:, 0], (1, 2, 0, 3, 4)).reshape(2, 2, N_DEN_B, gp)
    yc, yd, st = _s5_scan(lam3, bt, chan_major_c(s5_c_re), chan_major_c(s5_c_im), u2c, u2d, h0)
    rw1, rb1 = _router_weights(1, router_w, router_bias)
    x3, h, gates = _s5_out(x2c, x2d, uc, ud, yc, yd, mods[1], row(s5_d[0]),
                           w_out_c[0], row(ln_mix_g[1]), row(ln_mix_b[1]), rw1, rb1)
    acc = _moe(1, h, gates, mods[1], exp_w_gate, exp_w_up, exp_w_down, sh_w_gate, sh_w_up, sh_w_down)
    lg1, lb1 = row(ln_ffn_g[1]), row(ln_ffn_b[1])
    y_prompt = _moe_finish(x3, acc, 0, T_CTX, lg1, lb1).reshape(N_CTX_B, CTX_L, D)
    y_sample = _moe_finish(x3, acc, n_ctx_tiles, T_DEN, lg1, lb1).reshape(N_DEN_B, DEN_L, D)
    new_mla_ckv = ckv.reshape(N_CTX_B, 1, CTX_L, MLA_KV_RANK)
    new_mla_krope = jnp.transpose(kr_ctx_t, (0, 1, 3, 2))
    new_state_ssm = jnp.transpose(st, (2, 0, 1, 3)).reshape(N_CTX_B, 1, 2, 2, S5_G, S5_P)
    return (y_prompt, y_sample, new_attn_k, new_attn_v, new_mla_ckv, new_mla_krope, new_state_ssm)
```

```python
import functools

import jax
import jax.numpy as jnp
import numpy as np
from jax import lax
from jax.experimental import pallas as pl
from jax.experimental.pallas import tpu as pltpu

f32 = jnp.float32
bf16 = jnp.bfloat16

D = 1024
N_CTX_B, CTX_L = 16, 256
N_DEN_B, DEN_L = 2, 1024
T_CTX = N_CTX_B * CTX_L
T_DEN = N_DEN_B * DEN_L
T_ALL = T_CTX + T_DEN
GRID_W = 64
WINDOW = 128
ROPE_BASE = 10000.0
A_HEADS, A_KV_HEADS, A_HD = 8, 2, 64
A_GROUP = A_HEADS // A_KV_HEADS
A_SCALE = A_HD ** -0.5
MLA_HEADS, MLA_Q_RANK, MLA_KV_RANK = 8, 256, 128
MLA_NOPE, MLA_ROPE, MLA_V = 64, 32, 64
MLA_SCALE = (MLA_NOPE + MLA_ROPE) ** -0.5
N_EXPERTS, TOP_K, EXPERT_FF, SHARED_FF = 64, 6, 128, 128
ROUTED_SCALE = 2.5
DEPTH = 2
ALPHA = (2.0 * DEPTH) ** 0.25
LN_EPS = 1e-5
RMS_EPS = 1e-6
NEG_INF = -1e30
S5_G, S5_CH, S5_P = 64, 16, 64

LANE = 128
SUB = 8
VMEM_LIMIT = 56 * 1024 * 1024

TOK_TILE = 512


def _mod_row(tile_idx, tile_rows):
    start = tile_idx * tile_rows
    return jnp.where(start < T_CTX, 0, 1 + (start - T_CTX) // DEN_L)


def _layer_norm(y, g, b):
    mu = jnp.mean(y, axis=-1, keepdims=True)
    yc = y - mu
    var = jnp.mean(yc * yc, axis=-1, keepdims=True)
    return yc * lax.rsqrt(var + LN_EPS) * g + b


def _silu(x):
    return x * jax.nn.sigmoid(x)


def _dot(a, b):
    return jnp.dot(a, b, preferred_element_type=f32)


def _dot_nt(a, b):
    return lax.dot_general(a, b, (((1,), (1,)), ((), ())), preferred_element_type=f32)


def _split_bf16(a):
    hi = a.astype(bf16)
    return hi, (a - hi.astype(f32)).astype(bf16)


def _full(shape, n_grid):
    zeros = tuple(0 for _ in shape)
    return pl.BlockSpec(shape, lambda *_: zeros)


def _two_stream_specs(tile_rows, width):
    n_ctx = T_CTX // tile_rows
    return (pl.BlockSpec((tile_rows, width), lambda i: (jnp.minimum(i, n_ctx - 1), 0)),
            pl.BlockSpec((tile_rows, width), lambda i: (jnp.maximum(i - n_ctx, 0), 0)))


def _pick(i, tile_rows, ctx_ref, den_ref):
    return jnp.where(i < T_CTX // tile_rows, ctx_ref[...], den_ref[...])


def _params(n_grid):
    return pltpu.CompilerParams(dimension_semantics=("arbitrary",) * n_grid, vmem_limit_bytes=VMEM_LIMIT)


ADA_TN = 1536


def _adaln_kernel(c_ref, w_ref, b_ref, *o_refs):
    s_hi, s_lo = _split_bf16(_silu(c_ref[...]))
    w_hi, w_lo = _split_bf16(w_ref[0])
    bias = b_ref[pl.ds(pl.program_id(0), 1), :]
    val = _dot(s_hi, w_hi) + (_dot(s_hi, w_lo) + _dot(s_lo, w_hi)) + bias
    for k, o_ref in enumerate(o_refs):
        @pl.when(pl.program_id(0) == k)
        def _():
            o_ref[...] = val


def _adaln(cvec8, ada_w, ada_b):
    n = 6 * D
    nb = n // ADA_TN
    layer_spec = lambda k: pl.BlockSpec((8, ADA_TN), lambda l, j: (0, jnp.clip((l - k) * nb + j, 0, nb - 1)))
    return pl.pallas_call(
        _adaln_kernel,
        grid=(DEPTH, n // ADA_TN),
        in_specs=[
            pl.BlockSpec((8, D), lambda l, j: (0, 0)),
            pl.BlockSpec((1, D, ADA_TN), lambda l, j: (l, 0, j)),
            pl.BlockSpec((DEPTH, ADA_TN), lambda l, j: (0, j)),
        ],
        out_specs=[layer_spec(k) for k in range(DEPTH)],
        out_shape=[jax.ShapeDtypeStruct((8, n), f32) for _ in range(DEPTH)],
        compiler_params=_params(2),
        name="adaln",
    )(cvec8, ada_w, ada_b)


def _rope_table_array(head_dim):
    q = head_dim // 4
    pos = np.arange(DEN_L)
    row, col = (pos // GRID_W).astype(np.float64), (pos % GRID_W).astype(np.float64)
    lane = np.arange(LANE) % head_dim
    is_col = lane >= head_dim // 2
    w = lane % (head_dim // 2)
    first = w < q
    inv_freq = ROPE_BASE ** (-np.arange(q, dtype=np.float64) / q)
    ang = np.where(is_col[None, :], col[:, None], row[:, None]) * inv_freq[w % q][None, :]
    cos, sin = np.cos(ang), np.sin(ang)
    sin_a = np.where(first[None, :], -sin, 0.0)
    sin_b = np.where(first[None, :], 0.0, sin)
    ident = np.stack([np.ones((TOK_TILE, LANE)), np.zeros((TOK_TILE, LANE)), np.zeros((TOK_TILE, LANE))])
    tab = np.concatenate([ident, np.stack([cos, sin_a, sin_b])], axis=1).astype(np.float32)
    return jnp.asarray(tab), q


def _rope_chunk(x, tab_ref, q):
    return x * tab_ref[0] + pltpu.roll(x, LANE - q, 1) * tab_ref[1] + pltpu.roll(x, q, 1) * tab_ref[2]


PROJ_W = 1280
C_QA, C_KA, C_VA, C_CQ, C_CKV, C_KR = 0, 512, 640, 768, 1024, 1152
MLA_NN = MLA_HEADS * MLA_NOPE


def _ab_proj_kernel(xc_ref, xd_ref, mods_ref, w_ref, qn_ref, kvn_ref, wuq_ref, wukv_ref, ta_ref, tm_ref,
                    qa_ref, ka_ref, va_ref, ckv_ref, kr_ref, qm_ref, kvl_ref, krc_ref, wb_ref, *, qa_shift, qm_shift):
    i = pl.program_id(0)

    @pl.when(i == 0)
    def _():
        n_w = w_ref.shape[0]
        wb_ref[:, PROJ_W - LANE:PROJ_W] = jnp.zeros((D, LANE), bf16)
        for c0 in range(0, n_w, LANE):
            c1 = min(c0 + LANE, n_w)
            wb_ref[:, c0:c1] = w_ref[c0:c1, :].T.astype(bf16)

    r = _mod_row(i, TOK_TILE)
    mrow = mods_ref[pl.ds(r, 1), :]
    sh, sc = mrow[:, 0:D], mrow[:, D:2 * D]
    x = _pick(i, TOK_TILE, xc_ref, xd_ref)
    h = (x * (1.0 + sc) + sh).astype(bf16)
    proj = _dot(h, wb_ref[...])
    for j in range(4):
        c0 = C_QA + LANE * j
        qa_ref[:, LANE * j:LANE * (j + 1)] = _rope_chunk(proj[:, c0:c0 + LANE], ta_ref, qa_shift).astype(bf16)
    ka_ref[...] = _rope_chunk(proj[:, C_KA:C_KA + LANE], ta_ref, qa_shift)
    va_ref[...] = proj[:, C_VA:C_VA + LANE]
    cq = proj[:, C_CQ:C_CQ + MLA_Q_RANK]
    cq = cq * lax.rsqrt(jnp.mean(cq * cq, axis=-1, keepdims=True) + RMS_EPS) * qn_ref[...]
    ckv = proj[:, C_CKV:C_CKV + MLA_KV_RANK]
    ckv = ckv * lax.rsqrt(jnp.mean(ckv * ckv, axis=-1, keepdims=True) + RMS_EPS) * kvn_ref[...]
    kr = _rope_chunk(proj[:, C_KR:C_KR + LANE], tm_ref, qm_shift)
    kr_ref[...] = kr

    qm = _dot(cq.astype(bf16), wuq_ref[...])
    qm_ref[:, 0:MLA_NN] = qm[:, 0:MLA_NN].astype(bf16)
    for j in range(2):
        c0 = MLA_NN + LANE * j
        qm_ref[:, c0:c0 + LANE] = _rope_chunk(qm[:, c0:c0 + LANE], tm_ref, qm_shift).astype(bf16)
    kvl_ref[...] = _dot(ckv.astype(bf16), wukv_ref[...]).astype(bf16)

    @pl.when(i < T_CTX // TOK_TILE)
    def _():
        ckv_ref[...] = ckv
        kr_t = kr_ref[...].T
        for b in range(TOK_TILE // CTX_L):
            krc_ref[b, 0] = kr_t[0:MLA_ROPE, CTX_L * b:CTX_L * (b + 1)]


def _rope_block_index(i):
    tiles_ctx = T_CTX // TOK_TILE
    per_seq = DEN_L // TOK_TILE
    return jnp.where(i < tiles_ctx, 0, 1 + (i - tiles_ctx) % per_seq)


def _ab_proj(xc, xd, mods0, w_in, q_norm, kv_norm, w_uq_p, w_ukv_p):
    tab_a, qa_shift = _rope_table_array(A_HD)
    tab_m, qm_shift = _rope_table_array(MLA_ROPE)
    row_spec = lambda w: pl.BlockSpec((TOK_TILE, w), lambda i: (i, 0))
    xc_spec, xd_spec = _two_stream_specs(TOK_TILE, D)
    tab_spec = pl.BlockSpec((3, TOK_TILE, LANE), lambda i: (0, _rope_block_index(i), 0))
    outs = [(512, bf16), (LANE, f32), (LANE, f32), (LANE, f32), (LANE, f32), (768, bf16), (1024, bf16)]
    last_ctx = T_CTX // TOK_TILE - 1
    per = TOK_TILE // CTX_L
    ctx_spec = pl.BlockSpec((TOK_TILE, LANE), lambda i: (jnp.minimum(i, last_ctx), 0))
    krc_spec = pl.BlockSpec((per, 1, MLA_ROPE, CTX_L), lambda i: (jnp.minimum(i, last_ctx), 0, 0, 0))
    return pl.pallas_call(
        functools.partial(_ab_proj_kernel, qa_shift=qa_shift, qm_shift=qm_shift),
        grid=(T_ALL // TOK_TILE,),
        in_specs=[xc_spec, xd_spec, _full((8, 6 * D), 1), _full(w_in.shape, 1), _full((1, MLA_Q_RANK), 1),
                  _full((1, MLA_KV_RANK), 1), _full((MLA_Q_RANK, 768), 1), _full((MLA_KV_RANK, 1024), 1),
                  tab_spec, tab_spec],
        out_specs=[ctx_spec if k == 3 else row_spec(w) for k, (w, _) in enumerate(outs)] + [krc_spec],
        out_shape=[jax.ShapeDtypeStruct((T_CTX if k == 3 else T_ALL, w), dt) for k, (w, dt) in enumerate(outs)]
        + [jax.ShapeDtypeStruct((N_CTX_B, 1, MLA_ROPE, CTX_L), f32)],
        scratch_shapes=[pltpu.VMEM((D, PROJ_W), bf16)],
        compiler_params=_params(1),
        name="ab_proj",
    )(xc, xd, mods0, w_in, q_norm, kv_norm, w_uq_p, w_ukv_p, tab_a, tab_m)


def _softmax_blocks(s_refs, p_refs, sink_col=None):
    m = s_refs[0][...].max(axis=-1, keepdims=True)
    for s_ref in s_refs[1:]:
        m = jnp.maximum(m, s_ref[...].max(axis=-1, keepdims=True))
    if sink_col is not None:
        m = jnp.maximum(m, sink_col)
    for s_ref, p_ref in zip(s_refs, p_refs):
        p_ref[...] = jnp.exp(s_ref[...] - m).astype(bf16)
    return 0.0 if sink_col is None else jnp.exp(sink_col - m)


def _with_ones(v, axis=1):
    return jnp.concatenate([v, jnp.ones(v.shape, v.dtype)], axis=axis)


def _normalise(o_aug, extra, width):
    return o_aug[:, 0:width] * (1.0 / (o_aug[:, width:width + 1] + extra))


def _sink_column(sink_ref, rows_per_head):
    return jnp.concatenate([jnp.full((rows_per_head, 1), sink_ref[h], f32) for h in range(A_HEADS)], axis=0)


def _mla_q(qm_ref, h):
    rows = qm_ref.shape[0]
    return jnp.concatenate([qm_ref[:, MLA_NOPE * h:MLA_NOPE * (h + 1)],
                            qm_ref[:, MLA_NN + MLA_ROPE * h:MLA_NN + MLA_ROPE * (h + 1)],
                            jnp.zeros((rows, LANE - MLA_NOPE - MLA_ROPE), bf16)], axis=1)


def _mla_k(k_nope_h, k_rope):
    rows = k_nope_h.shape[0]
    return jnp.concatenate([k_nope_h, k_rope, jnp.zeros((rows, LANE - MLA_NOPE - MLA_ROPE), bf16)], axis=1)


def _mix_out_ln(merged_ref, wout_ref, x, mods_ref, r, g_ref, b_ref):
    out = _dot(merged_ref[...], wout_ref[...])
    gate = mods_ref[pl.ds(r, 1), 2 * D:3 * D]
    return _layer_norm(ALPHA * x + gate * out, g_ref[...], b_ref[...])


def _ctx_attn_kernel(sink_ref, qa_ref, ka_ref, va_ref, qm_ref, kvl_ref, kr_ref, x_ref, mods_ref, wout_ref,
                     g_ref, b_ref, o_ref, nk_ref, nv_ref, merged_ref, sa_ref, sm_ref, pa_ref, pm_ref):
    nk_ref[0, 0] = ka_ref[...].T.reshape(A_KV_HEADS, A_HD, CTX_L)
    nv_ref[0, 0] = va_ref[...].T.reshape(A_KV_HEADS, A_HD, CTX_L)
    n = CTX_L
    ka = ka_ref[...].astype(bf16)
    va = va_ref[...].astype(bf16)
    for j in range(A_KV_HEADS):
        q4 = jnp.concatenate([qa_ref[:, A_HD * h:A_HD * (h + 1)] for h in range(A_GROUP * j, A_GROUP * (j + 1))],
                             axis=0)
        sa_ref[A_GROUP * n * j:A_GROUP * n * (j + 1), :] = _dot_nt(q4, ka[:, A_HD * j:A_HD * (j + 1)]) * A_SCALE
    kr = kr_ref[:, 0:MLA_ROPE].astype(bf16)
    for h in range(MLA_HEADS):
        k_cat = _mla_k(kvl_ref[:, MLA_NOPE * h:MLA_NOPE * (h + 1)], kr)
        sm_ref[n * h:n * (h + 1), :] = _dot_nt(_mla_q(qm_ref, h), k_cat) * MLA_SCALE
    sink_a = _softmax_blocks([sa_ref], [pa_ref], _sink_column(sink_ref, n))
    _softmax_blocks([sm_ref], [pm_ref])
    for j in range(A_KV_HEADS):
        rows = slice(A_GROUP * n * j, A_GROUP * n * (j + 1))
        o4 = _normalise(_dot(pa_ref[rows, :], _with_ones(va[:, A_HD * j:A_HD * (j + 1)])), sink_a[rows], A_HD)
        for g in range(A_GROUP):
            h = A_GROUP * j + g
            merged_ref[:, A_HD * h:A_HD * (h + 1)] = o4[n * g:n * (g + 1)].astype(bf16)
    for h in range(MLA_HEADS):
        rows = slice(n * h, n * (h + 1))
        v = _with_ones(kvl_ref[:, MLA_NN + MLA_V * h:MLA_NN + MLA_V * (h + 1)])
        merged_ref[:, MLA_NN + MLA_V * h:MLA_NN + MLA_V * (h + 1)] = (
            _normalise(_dot(pm_ref[rows, :], v), 0.0, MLA_V).astype(bf16))
    o_ref[...] = _mix_out_ln(merged_ref, wout_ref, x_ref[...], mods_ref, 0, g_ref, b_ref)


def _ctx_attn(sink, qa, ka, va, qm, kvl, kr, x_all, mods0, w_out, ln_g, ln_b):
    blk = lambda w: pl.BlockSpec((CTX_L, w), lambda b: (b, 0))
    cache_blk = pl.BlockSpec((1, 1, A_KV_HEADS, A_HD, CTX_L), lambda b: (b, 0, 0, 0, 0))
    cache_shape = jax.ShapeDtypeStruct((N_CTX_B, 1, A_KV_HEADS, A_HD, CTX_L), f32)
    return pl.pallas_call(
        _ctx_attn_kernel,
        grid=(N_CTX_B,),
        in_specs=[pl.BlockSpec(memory_space=pltpu.SMEM), blk(512), blk(LANE), blk(LANE), blk(768), blk(1024),
                  blk(LANE), blk(D), _full((8, 6 * D), 1), _full((D, D), 1), _full((1, D), 1), _full((1, D), 1)],
        out_specs=[blk(D), cache_blk, cache_blk],
        out_shape=[jax.ShapeDtypeStruct((T_CTX, D), f32), cache_shape, cache_shape],
        scratch_shapes=[pltpu.VMEM((CTX_L, D), bf16),
                        pltpu.VMEM((A_HEADS * CTX_L, CTX_L), f32), pltpu.VMEM((MLA_HEADS * CTX_L, CTX_L), f32),
                        pltpu.VMEM((A_HEADS * CTX_L, CTX_L), bf16), pltpu.VMEM((MLA_HEADS * CTX_L, CTX_L), bf16)],
        compiler_params=_params(1),
        name="ctx_attn",
    )(sink, qa, ka, va, qm, kvl, kr, x_all, mods0, w_out, ln_g, ln_b)


QB = 256
WIN = QB + 2 * WINDOW
DEN_BLK0 = T_CTX // DEN_L
MLA_KEYS = CTX_L + DEN_L


def _den_attn_kernel(sink_ref, qa_ref, ka_ref, va_ref, cak_ref, cav_ref, qm_ref, kvl_ref, kr_ref, cckv_ref, ckr_ref,
                     wukv_ref, x_ref, mods_ref, wout_ref, g_ref, b_ref, o_ref, merged_ref, kcat_ref, vcat_ref,
                     saw_ref, sac_ref, sm_ref, paw_ref, pac_ref, pm_ref):
    b = pl.program_id(0)
    n = pl.program_id(1)

    @pl.when(n == 0)
    def _():
        kvc = _dot(cckv_ref[0].astype(bf16), wukv_ref[...]).astype(bf16)
        kr_ctx = ckr_ref[0].T.astype(bf16)
        kr_lat = kr_ref[:, 0:MLA_ROPE].astype(bf16)
        for h in range(MLA_HEADS):
            ns = slice(MLA_NOPE * h, MLA_NOPE * (h + 1))
            vs = slice(MLA_NN + MLA_V * h, MLA_NN + MLA_V * (h + 1))
            kcat_ref[h, 0:CTX_L, :] = _mla_k(kvc[:, ns], kr_ctx)
            kcat_ref[h, CTX_L:MLA_KEYS, :] = _mla_k(kvl_ref[:, ns], kr_lat)
            vcat_ref[h, 0:CTX_L, :] = _with_ones(kvc[:, vs])
            vcat_ref[h, CTX_L:MLA_KEYS, :] = _with_ones(kvl_ref[:, vs])

    start = pl.multiple_of(jnp.clip(QB * n - WINDOW, 0, DEN_L - WIN), WINDOW)
    grp_rows = A_GROUP * QB
    qpos = QB * n + (lax.broadcasted_iota(jnp.int32, (grp_rows, WIN), 0) & (QB - 1))
    kpos = start + lax.broadcasted_iota(jnp.int32, (grp_rows, WIN), 1)
    valid = jnp.abs(qpos - kpos) <= WINDOW
    kwin = ka_ref[pl.ds(start, WIN), :].astype(bf16)
    vwin = va_ref[pl.ds(start, WIN), :].astype(bf16)
    kctx_t = [cak_ref[0, j].astype(bf16) for j in range(A_KV_HEADS)]
    vctx_t = [cav_ref[0, j].astype(bf16) for j in range(A_KV_HEADS)]
    for j in range(A_KV_HEADS):
        sl = slice(A_HD * j, A_HD * (j + 1))
        rows = slice(grp_rows * j, grp_rows * (j + 1))
        q4 = jnp.concatenate([qa_ref[:, A_HD * h:A_HD * (h + 1)] for h in range(A_GROUP * j, A_GROUP * (j + 1))],
                             axis=0)
        saw_ref[rows, :] = jnp.where(valid, _dot_nt(q4, kwin[:, sl]) * A_SCALE, NEG_INF)
        sac_ref[rows, :] = _dot(q4, kctx_t[j]) * A_SCALE
    for h in range(MLA_HEADS):
        sm_ref[QB * h:QB * (h + 1), :] = _dot_nt(_mla_q(qm_ref, h), kcat_ref[h]) * MLA_SCALE
    sink_a = _softmax_blocks([saw_ref, sac_ref], [paw_ref, pac_ref], _sink_column(sink_ref, QB))
    _softmax_blocks([sm_ref], [pm_ref])
    for j in range(A_KV_HEADS):
        sl = slice(A_HD * j, A_HD * (j + 1))
        rows = slice(grp_rows * j, grp_rows * (j + 1))
        o_aug = (_dot(paw_ref[rows, :], _with_ones(vwin[:, sl]))
                 + _dot_nt(pac_ref[rows, :], _with_ones(vctx_t[j], axis=0)))
        o4 = _normalise(o_aug, sink_a[rows], A_HD)
        for g in range(A_GROUP):
            h = A_GROUP * j + g
            merged_ref[:, A_HD * h:A_HD * (h + 1)] = o4[QB * g:QB * (g + 1)].astype(bf16)
    for h in range(MLA_HEADS):
        rows = slice(QB * h, QB * (h + 1))
        o = _normalise(_dot(pm_ref[rows, :], vcat_ref[h]), 0.0, MLA_V)
        merged_ref[:, MLA_NN + MLA_V * h:MLA_NN + MLA_V * (h + 1)] = o.astype(bf16)
    o_ref[...] = _mix_out_ln(merged_ref, wout_ref, x_ref[...], mods_ref, 1 + b, g_ref, b_ref)


def _den_attn(sink, qa, ka, va, cache_k, cache_v, qm, kvl, kr, cache_ckv, cache_kr, w_ukv_p, x_all, mods0, w_out,
              ln_g, ln_b):
    nq = DEN_L // QB
    qblk = lambda w: pl.BlockSpec((QB, w), lambda b, n: (T_CTX // QB + b * nq + n, 0))
    seq = lambda w: pl.BlockSpec((DEN_L, w), lambda b, n: (DEN_BLK0 + b, 0))
    cache = lambda w: pl.BlockSpec((1, CTX_L, w), lambda b, n: (b, 0, 0))
    cache_a = pl.BlockSpec((1, A_KV_HEADS, A_HD, CTX_L), lambda b, n: (b, 0, 0, 0))
    return pl.pallas_call(
        _den_attn_kernel,
        grid=(N_DEN_B, nq),
        in_specs=[pl.BlockSpec(memory_space=pltpu.SMEM), qblk(512), seq(LANE), seq(LANE), cache_a, cache_a,
                  qblk(768), seq(1024), seq(LANE), cache(MLA_KV_RANK),
                  pl.BlockSpec((1, MLA_ROPE, CTX_L), lambda b, n: (b, 0, 0)),
                  _full((MLA_KV_RANK, 1024), 2), pl.BlockSpec((QB, D), lambda b, n: (b * nq + n, 0)),
                  _full((8, 6 * D), 2), _full((D, D), 2), _full((1, D), 2),
                  _full((1, D), 2)],
        out_specs=pl.BlockSpec((QB, D), lambda b, n: (b * nq + n, 0)),
        out_shape=jax.ShapeDtypeStruct((T_DEN, D), f32),
        scratch_shapes=[pltpu.VMEM((QB, D), bf16), pltpu.VMEM((MLA_HEADS, MLA_KEYS, LANE), bf16),
                        pltpu.VMEM((MLA_HEADS, MLA_KEYS, 2 * MLA_V), bf16)]
        + [pltpu.VMEM((A_HEADS * QB, w), dt) for dt in (f32, bf16) for w in (WIN, CTX_L, MLA_KEYS)],
        compiler_params=_params(2),
        name="den_attn",
    )(sink, qa, ka, va, cache_k, cache_v, qm, kvl, kr, cache_ckv, cache_kr, w_ukv_p, x_all, mods0, w_out, ln_g, ln_b)


SUBTILE = 256


def _route(x1, mrow, rw_ref, rb_ref):
    sh, sc = mrow[:, 3 * D:4 * D], mrow[:, 4 * D:5 * D]
    h = x1 * (1.0 + sc) + sh
    h_hi = h.astype(bf16)
    h_lo = (h - h_hi.astype(f32)).astype(bf16)
    logits = _dot(h_hi, rw_ref[0]) + (_dot(h_hi, rw_ref[1]) + _dot(h_lo, rw_ref[0]))
    scores = jax.nn.sigmoid(logits)
    lane = lax.broadcasted_iota(jnp.int32, scores.shape, 1).astype(f32)
    sel = jnp.where(lane < N_EXPERTS, scores + rb_ref[...], -jnp.inf)
    gates = jnp.zeros_like(scores)
    for _ in range(TOP_K):
        m = sel.max(axis=-1, keepdims=True)
        idx = jnp.where(sel == m, lane, float(LANE)).min(axis=-1, keepdims=True)
        hit = lane == idx
        gates = jnp.where(hit, scores, gates)
        sel = jnp.where(hit, -jnp.inf, sel)
    return h_hi, gates / gates.sum(axis=-1, keepdims=True) * ROUTED_SCALE


def _router_kernel(xc_ref, xd_ref, mods_ref, rw_ref, rb_ref, x_ref, h_ref, gates_ref):
    i = pl.program_id(0)
    r = _mod_row(i, TOK_TILE)
    mrow = mods_ref[pl.ds(r, 1), :]
    x1 = _pick(i, TOK_TILE, xc_ref, xd_ref)
    x_ref[...] = x1
    h_ref[...], gates_ref[...] = _route(x1, mrow, rw_ref, rb_ref)


def _router(x1c, x1d, mods_l, router_w_p, router_b_p):
    row_spec = lambda w: pl.BlockSpec((TOK_TILE, w), lambda i: (i, 0))
    xc_spec, xd_spec = _two_stream_specs(TOK_TILE, D)
    return pl.pallas_call(
        _router_kernel,
        grid=(T_ALL // TOK_TILE,),
        in_specs=[xc_spec, xd_spec, _full((8, 6 * D), 1), _full((2, D, LANE), 1), _full((1, LANE), 1)],
        out_specs=[row_spec(D), row_spec(D), row_spec(LANE)],
        out_shape=[jax.ShapeDtypeStruct((T_ALL, D), f32), jax.ShapeDtypeStruct((T_ALL, D), bf16),
                   jax.ShapeDtypeStruct((T_ALL, LANE), f32)],
        compiler_params=_params(1),
        name="router",
    )(x1c, x1d, mods_l, router_w_p, router_b_p)


MOE_TOK = 1536
MOE_EG = 8
MOE_VMEM_LIMIT = 60 * 1024 * 1024
MOE_TILE = 512
MOE_FF = MOE_EG * EXPERT_FF


def _moe_kernel(h_ref, gates_ref, mods_ref, wg_ref, wu_ref, wd_ref, sg_ref, su_ref, sd_ref, o_ref):
    p = pl.program_id(0)
    e = pl.program_id(1)
    n_tiles = MOE_TOK // MOE_TILE

    def gate_f(t):
        r = _mod_row(p * n_tiles + t, MOE_TILE)
        return mods_ref[pl.ds(r, 1), 5 * D:6 * D]

    def rows_of(t):
        if isinstance(t, int):
            return pl.ds(t * MOE_TILE, MOE_TILE)
        return pl.ds(pl.multiple_of(t * MOE_TILE, MOE_TILE), MOE_TILE)

    @pl.when(e == 0)
    def _():
        sg = sg_ref[...].astype(bf16)
        su = su_ref[...].astype(bf16)
        sd = sd_ref[...].astype(bf16)

        def body(t, c):
            rows = rows_of(t)
            ht = h_ref[rows, :]
            hid = _silu(_dot(ht, sg)) * _dot(ht, su)
            o_ref[rows, :] = gate_f(t) * _dot(hid.astype(bf16), sd)
            return c

        lax.fori_loop(0, n_tiles, body, 0)

    wg = jnp.concatenate([wg_ref[k].astype(bf16) for k in range(MOE_EG)], axis=1)
    wu = jnp.concatenate([wu_ref[k].astype(bf16) for k in range(MOE_EG)], axis=1)
    wd = jnp.concatenate([wd_ref[k].astype(bf16) for k in range(MOE_EG)], axis=0)
    lane = lax.broadcasted_iota(jnp.int32, (MOE_TILE, LANE), 1)

    def body(t, c):
        rows = rows_of(t)
        ht = h_ref[rows, :]
        hid = _silu(_dot(ht, wg)) * _dot(ht, wu)
        gt = gates_ref[rows, :]
        parts = []
        for k in range(MOE_EG):
            col = jnp.where(lane == e * MOE_EG + k, gt, 0.0).sum(axis=-1, keepdims=True)
            parts.append((hid[:, EXPERT_FF * k:EXPERT_FF * (k + 1)] * col).astype(bf16))
        o_ref[rows, :] += gate_f(t) * _dot(jnp.concatenate(parts, axis=1), wd)
        return c

    for t in range(n_tiles):
        body(t, 0)


def _moe(l, h, gates, mods_l, wg, wu, wd, sg, su, sd):
    tok = lambda w: pl.BlockSpec((MOE_TOK, w), lambda p, e: (p, 0))
    return pl.pallas_call(
        _moe_kernel,
        grid=(T_ALL // MOE_TOK, N_EXPERTS // MOE_EG),
        in_specs=[tok(D), tok(LANE), _full((8, 6 * D), 2),
                  pl.BlockSpec((None, MOE_EG, D, EXPERT_FF), lambda p, e: (l, e, 0, 0)),
                  pl.BlockSpec((None, MOE_EG, D, EXPERT_FF), lambda p, e: (l, e, 0, 0)),
                  pl.BlockSpec((None, MOE_EG, EXPERT_FF, D), lambda p, e: (l, e, 0, 0)),
                  pl.BlockSpec((None, D, SHARED_FF), lambda p, e: (l, 0, 0)),
                  pl.BlockSpec((None, D, SHARED_FF), lambda p, e: (l, 0, 0)),
                  pl.BlockSpec((None, SHARED_FF, D), lambda p, e: (l, 0, 0))],
        out_specs=tok(D),
        out_shape=jax.ShapeDtypeStruct((T_ALL, D), f32),
        compiler_params=pltpu.CompilerParams(dimension_semantics=("arbitrary", "arbitrary"),
                                             vmem_limit_bytes=MOE_VMEM_LIMIT),
        name="moe",
    )(h, gates, mods_l, wg, wu, wd, sg, su, sd)


def _moe_finish_kernel(x_ref, acc_ref, g_ref, b_ref, o_ref):
    o_ref[...] = _layer_norm(ALPHA * x_ref[...] + acc_ref[...], g_ref[...], b_ref[...])


def _moe_finish(x1, acc, tile0, n_rows, ln_g, ln_b):
    src = pl.BlockSpec((TOK_TILE, D), lambda i: (tile0 + i, 0))
    return pl.pallas_call(
        _moe_finish_kernel,
        grid=(n_rows // TOK_TILE,),
        in_specs=[src, src, _full((1, D), 1), _full((1, D), 1)],
        out_specs=pl.BlockSpec((TOK_TILE, D), lambda i: (i, 0)),
        out_shape=jax.ShapeDtypeStruct((n_rows, D), f32),
        compiler_params=_params(1),
        name="moe_finish",
    )(x1, acc, ln_g, ln_b)


def _router_weights(l, router_w, router_bias):
    rw = jnp.pad(router_w[l], ((0, 0), (0, LANE - N_EXPERTS)))
    rw_hi, rw_lo = _split_bf16(rw)
    rb = jnp.pad(router_bias[l], (0, LANE - N_EXPERTS)).reshape(1, LANE)
    return jnp.stack([rw_hi, rw_lo]), rb


S5_Q = 8
S5_NGB = D // LANE


def _s5_in_kernel(x_ref, acc_ref, lg_ref, lb_ref, mods_ref, w_ref, x2_ref, u_ref, u2_ref, slab_ref, *, row_of,
                  seq_len):
    r = row_of(pl.program_id(0))
    mrow = mods_ref[pl.ds(r, 1), :]
    sh, sc = mrow[:, 0:D], mrow[:, D:2 * D]
    x2 = _layer_norm(ALPHA * x_ref[...] + acc_ref[...], lg_ref[...], lb_ref[...])
    x2_ref[...] = x2
    h = (x2 * (1.0 + sc) + sh).astype(bf16)
    u = _dot(h, w_ref[...])
    u_ref[...] = u
    for s in range(S5_NGB):
        slab_ref[s] = u[:, LANE * s:LANE * (s + 1)]
    kt = seq_len // S5_Q
    for s in range(S5_NGB):
        for q in range(TOK_TILE // seq_len):
            for j in range(S5_Q):
                u2_ref[s, q * kt:(q + 1) * kt, LANE * j:LANE * (j + 1)] = (
                    slab_ref[s, pl.ds(q * seq_len + j, kt, stride=S5_Q), :].astype(bf16))


def _s5_in(x1, acc, ln_g, ln_b, tile0, mods1, w_in_c, n_b, seq_len, row_of):
    n_tiles = n_b * seq_len // TOK_TILE
    chunks = TOK_TILE // S5_Q
    src = pl.BlockSpec((TOK_TILE, D), lambda i: (tile0 + i, 0))
    dst = pl.BlockSpec((TOK_TILE, D), lambda i: (i, 0))
    return pl.pallas_call(
        functools.partial(_s5_in_kernel, row_of=row_of, seq_len=min(seq_len, TOK_TILE)),
        grid=(n_tiles,),
        in_specs=[src, src, _full((1, D), 1), _full((1, D), 1), _full((8, 6 * D), 1), _full((D, D), 1)],
        out_specs=[dst, dst, pl.BlockSpec((S5_NGB, chunks, D), lambda i: (0, i, 0))],
        out_shape=[jax.ShapeDtypeStruct((n_b * seq_len, D), f32), jax.ShapeDtypeStruct((n_b * seq_len, D), f32),
                   jax.ShapeDtypeStruct((S5_NGB, n_tiles * chunks, D), bf16)],
        scratch_shapes=[pltpu.VMEM((S5_NGB, TOK_TILE, LANE), f32)],
        compiler_params=_params(1),
        name="s5_in",
    )(x1, acc, ln_g, ln_b, mods1, w_in_c)


S5_GL = (LANE // S5_CH) * S5_P
S5_ROWS_C = (CTX_L // S5_Q) * N_CTX_B
S5_ROWS_D = (DEN_L // S5_Q) * N_DEN_B


def _s5_scan_kernel(lam_ref, bt_ref, ctr_ref, cti_ref, uc_ref, ud_ref, h0_ref,
                    yc_ref, yd_ref, st_ref, win_ref, mso_ref, wit_ref, a_ref, s_ref, hp_ref):
    gl = S5_GL
    rowg = lax.shift_right_logical(lax.broadcasted_iota(jnp.int32, (LANE, gl), 0), 4)
    colg = lax.shift_right_logical(lax.broadcasted_iota(jnp.int32, (LANE, gl), 1), 6)
    same_group = rowg == colg
    reps = LANE // S5_CH

    def expand(t):
        return jnp.where(same_group, jnp.concatenate([t] * reps, axis=0), 0.0)

    def expand_c(t):
        return jnp.where(same_group, jnp.concatenate([t] * reps, axis=1), 0.0)

    for d in range(2):
        fwd = d == 0
        lre, lim = lam_ref[d, 0:1, :], lam_ref[d, 1:2, :]
        dt = jnp.exp(lam_ref[d, 2:3, :])
        a, w = lre * dt, lim * dt
        pre = [jnp.exp(m * a) * jnp.cos(m * w) for m in range(S5_Q + 1)]
        pim = [jnp.exp(m * a) * jnp.sin(m * w) for m in range(S5_Q + 1)]
        xr, xi = pre[1] - 1.0, pim[1]
        den = lre * lre + lim * lim
        cfr, cfi = (xr * lre + xi * lim) / den, (xi * lre - xr * lim) / den
        btr, bti = bt_ref[0, d], bt_ref[1, d]
        bexp_r = expand(cfr * btr - cfi * bti)
        bexp_i = expand(cfr * bti + cfi * btr)
        cexp_r, cexp_i = expand_c(ctr_ref[d]), expand_c(cti_ref[d])
        for m in range(S5_Q + 1):
            a_ref[m, :, 0:gl] = cexp_r * pre[m] - cexp_i * pim[m]
            a_ref[m, :, gl:2 * gl] = -(cexp_r * pim[m] + cexp_i * pre[m])
        for j in range(S5_Q):
            m = S5_Q - 1 - j if fwd else j
            win_ref[LANE * j:LANE * (j + 1), 0:gl] = (pre[m] * bexp_r - pim[m] * bexp_i).astype(bf16)
            win_ref[LANE * j:LANE * (j + 1), gl:2 * gl] = (pre[m] * bexp_i + pim[m] * bexp_r).astype(bf16)
        for j in range(S5_Q):
            m = j + 1 if fwd else S5_Q - j
            mso_ref[LANE * j:LANE * (j + 1), :] = a_ref[m].astype(bf16)
        b2 = jnp.concatenate([bexp_r, bexp_i], axis=1).astype(bf16)
        kt = [_dot_nt(b2, a_ref[tau].astype(bf16)) for tau in range(S5_Q)]
        for j in range(S5_Q):
            for jp in range(S5_Q):
                tau = jp - j if fwd else j - jp
                blk = slice(LANE * j, LANE * (j + 1)), slice(LANE * jp, LANE * (jp + 1))
                if fwd:
                    wit_ref[blk] = kt[tau] if tau >= 0 else jnp.zeros((LANE, LANE), f32)
                elif tau >= 0:
                    wit_ref[blk] = wit_ref[blk] + kt[tau]

        l8r, l8i = pre[S5_Q], pim[S5_Q]

        nsl = gl // LANE

        def slabs(ref, rs, first):
            return jnp.concatenate([ref[first + sl, rs, :] for sl in range(nsl)], axis=1)

        def put_slabs(ref, rs, first, val):
            for sl in range(nsl):
                ref[first + sl, rs, :] = val[:, LANE * sl:LANE * (sl + 1)]

        def advance(hr, hi_, sr, si):
            return l8r * hr - l8i * hi_ + sr, l8r * hi_ + l8i * hr + si

        def run(u_ref, y_ref, n_b, n_k, h_init):
            rows = n_b * n_k
            s = _dot(u_ref[0], win_ref[...])
            if n_b % SUB == 0:
                pitch = n_k + 1
                for bb in range(n_b):
                    dst = slice(bb * pitch, bb * pitch + n_k)
                    put_slabs(s_ref, dst, 0, s[bb * n_k:(bb + 1) * n_k, 0:gl])
                    put_slabs(s_ref, dst, nsl, s[bb * n_k:(bb + 1) * n_k, gl:2 * gl])

                def step(i, carry):
                    hr, hi_ = carry
                    rs = pl.ds(i if fwd else n_k - 1 - i, n_b, stride=pitch)
                    put_slabs(hp_ref, rs, 0, hr)
                    put_slabs(hp_ref, rs, nsl, hi_)
                    return advance(hr, hi_, slabs(s_ref, rs, 0), slabs(s_ref, rs, nsl))

                h_fin = lax.fori_loop(0, n_k, step, h_init)
                hp = jnp.concatenate(
                    [jnp.concatenate([hp_ref[sl, bb * pitch:bb * pitch + n_k, :] for sl in range(2 * nsl)], axis=1)
                     for bb in range(n_b)], axis=0).astype(bf16)
            else:
                put_slabs(s_ref, slice(0, rows), 0, s[:, 0:gl])
                put_slabs(s_ref, slice(0, rows), nsl, s[:, gl:2 * gl])
                n_it = n_k // SUB

                def step(i, carry):
                    it = i if fwd else n_it - 1 - i
                    out = []
                    for bb in range(n_b):
                        hr, hi_ = carry[bb]
                        rs = pl.ds(pl.multiple_of(bb * n_k + it * SUB, SUB), SUB)
                        s_re, s_im = slabs(s_ref, rs, 0), slabs(s_ref, rs, nsl)
                        prev_r, prev_i = [None] * SUB, [None] * SUB
                        for sub in (range(SUB) if fwd else reversed(range(SUB))):
                            prev_r[sub], prev_i[sub] = hr, hi_
                            hr, hi_ = advance(hr, hi_, s_re[sub:sub + 1], s_im[sub:sub + 1])
                        put_slabs(hp_ref, rs, 0, jnp.concatenate(prev_r, axis=0))
                        put_slabs(hp_ref, rs, nsl, jnp.concatenate(prev_i, axis=0))
                        out.append((hr, hi_))
                    return tuple(out)

                fin = lax.fori_loop(0, n_it, step, tuple((h_init[0][bb:bb + 1], h_init[1][bb:bb + 1])
                                                          for bb in range(n_b)))
                h_fin = (jnp.concatenate([f[0] for f in fin], axis=0), jnp.concatenate([f[1] for f in fin], axis=0))
                hp = jnp.concatenate([hp_ref[sl, 0:rows, :] for sl in range(2 * nsl)], axis=1).astype(bf16)
            y = _dot_nt(hp, mso_ref[...])
            if fwd:
                y_ref[0] = y
            else:
                y_ref[0] += y
            return h_fin

        zeros = jnp.zeros((N_CTX_B, gl), f32)
        hr, hi_ = run(uc_ref, yc_ref, N_CTX_B, CTX_L // S5_Q, (zeros, zeros))
        st_ref[d, 0] = hr
        st_ref[d, 1] = hi_
        run(ud_ref, yd_ref, N_DEN_B, DEN_L // S5_Q, (h0_ref[d, 0], h0_ref[d, 1]))

    wit = wit_ref[...].astype(bf16)
    yc_ref[0] += _dot(uc_ref[0], wit)
    yd_ref[0] += _dot(ud_ref[0], wit)


def _s5_scan(lam3, bt, ct_re, ct_im, u2c, u2d, h0):
    gl = S5_GL
    vec = pl.BlockSpec((2, 3, gl), lambda g: (0, 0, g))
    tab = pl.BlockSpec((2, 2, S5_CH, gl), lambda g: (0, 0, 0, g))
    ctab = pl.BlockSpec((2, LANE, S5_P), lambda g: (0, g, 0))
    rows = lambda n: pl.BlockSpec((1, n, D), lambda g: (g, 0, 0))
    return pl.pallas_call(
        _s5_scan_kernel,
        grid=(S5_NGB,),
        in_specs=[vec, tab, ctab, ctab, rows(S5_ROWS_C), rows(S5_ROWS_D),
                  pl.BlockSpec((2, 2, N_DEN_B, gl), lambda g: (0, 0, 0, g))],
        out_specs=[rows(S5_ROWS_C), rows(S5_ROWS_D), pl.BlockSpec((2, 2, N_CTX_B, gl), lambda g: (0, 0, 0, g))],
        out_shape=[jax.ShapeDtypeStruct((S5_NGB, S5_ROWS_C, D), f32), jax.ShapeDtypeStruct((S5_NGB, S5_ROWS_D, D), f32),
                   jax.ShapeDtypeStruct((2, 2, N_CTX_B, S5_G * S5_P), f32)],
        scratch_shapes=[pltpu.VMEM((D, 2 * gl), bf16), pltpu.VMEM((D, 2 * gl), bf16), pltpu.VMEM((D, D), f32),
                        pltpu.VMEM((S5_Q + 1, LANE, 2 * gl), f32),
                        pltpu.VMEM((2 * gl // LANE, S5_ROWS_C + 2 * N_CTX_B, LANE), f32),
                        pltpu.VMEM((2 * gl // LANE, S5_ROWS_C + 2 * N_CTX_B, LANE), f32)],
        compiler_params=_params(1),
        name="s5_scan",
    )(lam3, bt, ct_re, ct_im, u2c, u2d, h0)


def _gelu_tanh(x):
    return 0.5 * x * (1.0 + jnp.tanh(np.sqrt(2.0 / np.pi).astype(np.float32) * (x + 0.044715 * (x * x * x))))


def _s5_out_kernel(xc_ref, xd_ref, uc_ref, ud_ref, yc_ref, yd_ref, mods_ref, dsk_ref, wout_ref, g_ref, b_ref, rw_ref,
                   rb_ref, x1_ref, h_ref, gates_ref, slab_ref, wob_ref):
    i = pl.program_id(0)

    @pl.when(i == 0)
    def _():
        wob_ref[...] = wout_ref[...].astype(bf16)

    is_ctx = i < T_CTX // TOK_TILE
    r = _mod_row(i, TOK_TILE)
    mrow = mods_ref[pl.ds(r, 1), :]
    u = _pick(i, TOK_TILE, uc_ref, ud_ref)

    kt = CTX_L // S5_Q
    for s in range(S5_NGB):
        for q in range(TOK_TILE // CTX_L):
            for j in range(S5_Q):
                blk = (s, slice(q * kt, (q + 1) * kt), slice(LANE * j, LANE * (j + 1)))
                slab_ref[s, pl.ds(q * CTX_L + j, kt, stride=S5_Q), :] = jnp.where(is_ctx, yc_ref[blk], yd_ref[blk])
    halves = [slice(a, a + SUBTILE) for a in range(0, TOK_TILE, SUBTILE)]
    zs = []
    for rows in halves:
        y = jnp.concatenate([slab_ref[s, rows, :] for s in range(S5_NGB)], axis=1) + dsk_ref[...] * u[rows]
        zs.append(_dot(_gelu_tanh(y).astype(bf16), wob_ref[...]))
    for rows, z in zip(halves, zs):
        out = z[:, 0:D] * jax.nn.sigmoid(z[:, D:2 * D])
        x = jnp.where(is_ctx, xc_ref[rows, :], xd_ref[rows, :])
        x1 = _layer_norm(ALPHA * x + mrow[:, 2 * D:3 * D] * out, g_ref[...], b_ref[...])
        x1_ref[rows, :] = x1
        h_ref[rows, :], gates_ref[rows, :] = _route(x1, mrow, rw_ref, rb_ref)


def _s5_out(xc, xd, uc, ud, yc, yd, mods1, d_skip, w_out_c, ln_g, ln_b, rw, rb):
    row_spec = lambda w: pl.BlockSpec((TOK_TILE, w), lambda i: (i, 0))
    uc_spec, ud_spec = _two_stream_specs(TOK_TILE, D)
    n_ctx = T_CTX // TOK_TILE
    chunks = TOK_TILE // S5_Q
    return pl.pallas_call(
        _s5_out_kernel,
        grid=(T_ALL // TOK_TILE,),
        in_specs=[uc_spec, ud_spec, uc_spec, ud_spec,
                  pl.BlockSpec((S5_NGB, chunks, D), lambda i: (0, jnp.minimum(i, n_ctx - 1), 0)),
                  pl.BlockSpec((S5_NGB, chunks, D), lambda i: (0, jnp.maximum(i - n_ctx, 0), 0)),
                  _full((8, 6 * D), 1), _full((1, D), 1), _full((D, 2 * D), 1), _full((1, D), 1), _full((1, D), 1),
                  _full((2, D, LANE), 1), _full((1, LANE), 1)],
        out_specs=[row_spec(D), row_spec(D), row_spec(LANE)],
        out_shape=[jax.ShapeDtypeStruct((T_ALL, D), f32), jax.ShapeDtypeStruct((T_ALL, D), bf16),
                   jax.ShapeDtypeStruct((T_ALL, LANE), f32)],
        scratch_shapes=[pltpu.VMEM((S5_NGB, TOK_TILE, LANE), f32), pltpu.VMEM((D, 2 * D), bf16)],
        compiler_params=_params(1),
        name="s5_out",
    )(xc, xd, uc, ud, yc, yd, mods1, d_skip, w_out_c, ln_g, ln_b, rw, rb)


def kernel(x_prompt, x_sample, c, cache_attn_k, cache_attn_v, cache_mla_ckv, cache_mla_krope, state_ssm, c_ctx,
           ada_w, ada_b, ln_mix_g, ln_mix_b, ln_ffn_g, ln_ffn_b, w_in_ab, attn_sink, mla_q_norm, mla_kv_norm,
           mla_w_uq, mla_w_ukv, w_out_ab, w_in_c, s5_lam_re, s5_lam_im, s5_log_dt, s5_b_re, s5_b_im, s5_c_re,
           s5_c_im, s5_d, w_out_c, router_w, router_bias, exp_w_gate, exp_w_up, exp_w_down, sh_w_gate, sh_w_up,
           sh_w_down):
    row = lambda v: v.reshape(1, -1)
    xc, xd = x_prompt.reshape(T_CTX, D), x_sample.reshape(T_DEN, D)
    cvec8 = jnp.concatenate([c_ctx[None, :], c, jnp.zeros((8 - 1 - N_DEN_B, D), f32)], axis=0)
    mods = _adaln(cvec8, ada_w, ada_b)

    uq = mla_w_uq[0].reshape(MLA_Q_RANK, MLA_HEADS, MLA_NOPE + MLA_ROPE)
    w_uq_p = jnp.concatenate([uq[:, :, :MLA_NOPE].reshape(MLA_Q_RANK, -1), uq[:, :, MLA_NOPE:].reshape(MLA_Q_RANK, -1)],
                             axis=1).astype(bf16)
    ukv = mla_w_ukv[0].reshape(MLA_KV_RANK, MLA_HEADS, MLA_NOPE + MLA_V)
    w_ukv_p = jnp.concatenate([ukv[:, :, :MLA_NOPE].reshape(MLA_KV_RANK, -1),
                               ukv[:, :, MLA_NOPE:].reshape(MLA_KV_RANK, -1)], axis=1).astype(bf16)
    qa, ka, va, ckv, kr, qm, kvl, kr_ctx_t = _ab_proj(xc, xd, mods[0], w_in_ab[0].T, row(mla_q_norm[0]), row(mla_kv_norm[0]),
                                            w_uq_p, w_ukv_p)
    w_out_b = w_out_ab[0].astype(bf16)
    g0, b0 = row(ln_mix_g[0]), row(ln_mix_b[0])
    x1c, nk_t, nv_t = _ctx_attn(attn_sink[0], qa, ka, va, qm, kvl, kr, xc, mods[0], w_out_b, g0, b0)
    new_attn_k = jnp.transpose(nk_t, (0, 1, 4, 2, 3))
    new_attn_v = jnp.transpose(nv_t, (0, 1, 4, 2, 3))
    x1d = _den_attn(attn_sink[0], qa, ka, va,
                    jnp.transpose(cache_attn_k[:, 0], (0, 2, 3, 1)), jnp.transpose(cache_attn_v[:, 0], (0, 2, 3, 1)),
                    qm, kvl, kr, cache_mla_ckv[:, 0], jnp.transpose(cache_mla_krope[:, 0], (0, 2, 1)), w_ukv_p, xd, mods[0], w_out_b, g0, b0)
    rw0, rb0 = _router_weights(0, router_w, router_bias)
    x1, h, gates = _router(x1c, x1d, mods[0], rw0, rb0)
    acc = _moe(0, h, gates, mods[0], exp_w_gate, exp_w_up, exp_w_down, sh_w_gate, sh_w_up, sh_w_down)

    w_in_c_b = w_in_c[0].astype(bf16)
    lg0, lb0 = row(ln_ffn_g[0]), row(ln_ffn_b[0])
    n_ctx_tiles = T_CTX // TOK_TILE
    x2c, uc, u2c = _s5_in(x1, acc, lg0, lb0, 0, mods[1], w_in_c_b, N_CTX_B, CTX_L, lambda i: 0)
    x2d, ud, u2d = _s5_in(x1, acc, lg0, lb0, n_ctx_tiles, mods[1], w_in_c_b, N_DEN_B, DEN_L,
                          lambda i: 1 + i // (DEN_L // TOK_TILE))
    gp = S5_G * S5_P
    bt = jnp.transpose(jnp.stack([s5_b_re[0], s5_b_im[0]]), (0, 1, 4, 2, 3)).reshape(2, 2, S5_CH, gp)
    lam3 = jnp.stack([s5_lam_re[0].reshape(2, gp), s5_lam_im[0].reshape(2, gp),
                      jnp.repeat(s5_log_dt[0], S5_P, axis=-1).reshape(2, gp)], axis=1)
    chan_major_c = lambda t: t[0].reshape(2, S5_G * S5_CH, S5_P)
    h0 = jnp.transpose(state_ssm[:, 0], (1, 2, 0, 3, 4)).reshape(2, 2, N_DEN_B, gp)
    yc, yd, st = _s5_scan(lam3, bt, chan_major_c(s5_c_re), chan_major_c(s5_c_im), u2c, u2d, h0)
    rw1, rb1 = _router_weights(1, router_w, router_bias)
    x3, h, gates = _s5_out(x2c, x2d, uc, ud, yc, yd, mods[1], row(s5_d[0]),
                           w_out_c[0], row(ln_mix_g[1]), row(ln_mix_b[1]), rw1, rb1)
    acc = _moe(1, h, gates, mods[1], exp_w_gate, exp_w_up, exp_w_down, sh_w_gate, sh_w_up, sh_w_down)
    lg1, lb1 = row(ln_ffn_g[1]), row(ln_ffn_b[1])
    y_prompt = _moe_finish(x3, acc, 0, T_CTX, lg1, lb1).reshape(N_CTX_B, CTX_L, D)
    y_sample = _moe_finish(x3, acc, n_ctx_tiles, T_DEN, lg1, lb1).reshape(N_DEN_B, DEN_L, D)
    new_mla_ckv = ckv.reshape(N_CTX_B, 1, CTX_L, MLA_KV_RANK)
    new_mla_krope = jnp.transpose(kr_ctx_t, (0, 1, 3, 2))
    new_state_ssm = jnp.transpose(st, (2, 0, 1, 3)).reshape(N_CTX_B, 1, 2, 2, S5_G, S5_P)
    return (y_prompt, y_sample, new_attn_k, new_attn_v, new_mla_ckv, new_mla_krope, new_state_ssm)
```

```python
import functools

import jax
import jax.numpy as jnp
import numpy as np
from jax import lax
from jax.experimental import pallas as pl
from jax.experimental.pallas import tpu as pltpu

f32 = jnp.float32
bf16 = jnp.bfloat16

D = 1024
N_CTX_B, CTX_L = 16, 256
N_DEN_B, DEN_L = 2, 1024
T_CTX = N_CTX_B * CTX_L
T_DEN = N_DEN_B * DEN_L
T_ALL = T_CTX + T_DEN
GRID_W = 64
WINDOW = 128
ROPE_BASE = 10000.0
A_HEADS, A_KV_HEADS, A_HD = 8, 2, 64
A_GROUP = A_HEADS // A_KV_HEADS
A_SCALE = A_HD ** -0.5
MLA_HEADS, MLA_Q_RANK, MLA_KV_RANK = 8, 256, 128
MLA_NOPE, MLA_ROPE, MLA_V = 64, 32, 64
MLA_SCALE = (MLA_NOPE + MLA_ROPE) ** -0.5
N_EXPERTS, TOP_K, EXPERT_FF, SHARED_FF = 64, 6, 128, 128
ROUTED_SCALE = 2.5
DEPTH = 2
ALPHA = (2.0 * DEPTH) ** 0.25
LN_EPS = 1e-5
RMS_EPS = 1e-6
NEG_INF = -1e30
S5_G, S5_CH, S5_P = 64, 16, 64

LANE = 128
SUB = 8
VMEM_LIMIT = 56 * 1024 * 1024

TOK_TILE = 512


def _mod_row(tile_idx, tile_rows):
    start = tile_idx * tile_rows
    return jnp.where(start < T_CTX, 0, 1 + (start - T_CTX) // DEN_L)


def _layer_norm(y, g, b):
    mu = jnp.mean(y, axis=-1, keepdims=True)
    yc = y - mu
    var = jnp.mean(yc * yc, axis=-1, keepdims=True)
    return yc * lax.rsqrt(var + LN_EPS) * g + b


def _silu(x):
    return x * jax.nn.sigmoid(x)


def _dot(a, b):
    return jnp.dot(a, b, preferred_element_type=f32)


def _dot_nt(a, b):
    return lax.dot_general(a, b, (((1,), (1,)), ((), ())), preferred_element_type=f32)


def _split_bf16(a):
    hi = a.astype(bf16)
    return hi, (a - hi.astype(f32)).astype(bf16)


def _full(shape, n_grid):
    zeros = tuple(0 for _ in shape)
    return pl.BlockSpec(shape, lambda *_: zeros)


def _two_stream_specs(tile_rows, width):
    n_ctx = T_CTX // tile_rows
    return (pl.BlockSpec((tile_rows, width), lambda i: (jnp.minimum(i, n_ctx - 1), 0)),
            pl.BlockSpec((tile_rows, width), lambda i: (jnp.maximum(i - n_ctx, 0), 0)))


def _pick(i, tile_rows, ctx_ref, den_ref):
    return jnp.where(i < T_CTX // tile_rows, ctx_ref[...], den_ref[...])


def _params(n_grid):
    return pltpu.CompilerParams(dimension_semantics=("arbitrary",) * n_grid, vmem_limit_bytes=VMEM_LIMIT)


ADA_TN = 1536


def _adaln_kernel(cctx_ref, c_ref, w_ref, b_ref, *o_refs):
    cvec8 = jnp.concatenate([cctx_ref[...], c_ref[...], jnp.zeros((8 - 1 - N_DEN_B, D), f32)], axis=0)
    s_hi, s_lo = _split_bf16(_silu(cvec8))
    w_hi, w_lo = _split_bf16(w_ref[0])
    bias = b_ref[pl.ds(pl.program_id(0), 1), :]
    val = _dot(s_hi, w_hi) + (_dot(s_hi, w_lo) + _dot(s_lo, w_hi)) + bias
    for k, o_ref in enumerate(o_refs):
        @pl.when(pl.program_id(0) == k)
        def _():
            o_ref[...] = val


def _adaln(c_ctx_row, c, ada_w, ada_b):
    n = 6 * D
    nb = n // ADA_TN
    layer_spec = lambda k: pl.BlockSpec((8, ADA_TN), lambda l, j: (0, jnp.clip((l - k) * nb + j, 0, nb - 1)))
    return pl.pallas_call(
        _adaln_kernel,
        grid=(DEPTH, n // ADA_TN),
        in_specs=[
            pl.BlockSpec((1, D), lambda l, j: (0, 0)),
            pl.BlockSpec((N_DEN_B, D), lambda l, j: (0, 0)),
            pl.BlockSpec((1, D, ADA_TN), lambda l, j: (l, 0, j)),
            pl.BlockSpec((DEPTH, ADA_TN), lambda l, j: (0, j)),
        ],
        out_specs=[layer_spec(k) for k in range(DEPTH)],
        out_shape=[jax.ShapeDtypeStruct((8, n), f32) for _ in range(DEPTH)],
        compiler_params=_params(2),
        name="adaln",
    )(c_ctx_row, c, ada_w, ada_b)


def _rope_table_array(head_dim):
    q = head_dim // 4
    pos = np.arange(DEN_L)
    row, col = (pos // GRID_W).astype(np.float64), (pos % GRID_W).astype(np.float64)
    lane = np.arange(LANE) % head_dim
    is_col = lane >= head_dim // 2
    w = lane % (head_dim // 2)
    first = w < q
    inv_freq = ROPE_BASE ** (-np.arange(q, dtype=np.float64) / q)
    ang = np.where(is_col[None, :], col[:, None], row[:, None]) * inv_freq[w % q][None, :]
    cos, sin = np.cos(ang), np.sin(ang)
    sin_a = np.where(first[None, :], -sin, 0.0)
    sin_b = np.where(first[None, :], 0.0, sin)
    ident = np.stack([np.ones((TOK_TILE, LANE)), np.zeros((TOK_TILE, LANE)), np.zeros((TOK_TILE, LANE))])
    tab = np.concatenate([ident, np.stack([cos, sin_a, sin_b])], axis=1).astype(np.float32)
    return jnp.asarray(tab), q


def _rope_chunk(x, tab_ref, q):
    return x * tab_ref[0] + pltpu.roll(x, LANE - q, 1) * tab_ref[1] + pltpu.roll(x, q, 1) * tab_ref[2]


PROJ_W = 1280
C_QA, C_KA, C_VA, C_CQ, C_CKV, C_KR = 0, 512, 640, 768, 1024, 1152
MLA_NN = MLA_HEADS * MLA_NOPE


def _ab_proj_kernel(xc_ref, xd_ref, mods_ref, w_ref, qn_ref, kvn_ref, wuq_ref, wukv_ref, ta_ref, tm_ref,
                    qa_ref, ka_ref, va_ref, ckv_ref, kr_ref, qm_ref, kvl_ref, krc_ref, wb_ref, *, qa_shift, qm_shift):
    i = pl.program_id(0)

    @pl.when(i == 0)
    def _():
        n_w = w_ref.shape[0]
        wb_ref[:, PROJ_W - LANE:PROJ_W] = jnp.zeros((D, LANE), bf16)
        for c0 in range(0, n_w, LANE):
            c1 = min(c0 + LANE, n_w)
            wb_ref[:, c0:c1] = w_ref[c0:c1, :].T.astype(bf16)

    r = _mod_row(i, TOK_TILE)
    mrow = mods_ref[pl.ds(r, 1), :]
    sh, sc = mrow[:, 0:D], mrow[:, D:2 * D]
    x = _pick(i, TOK_TILE, xc_ref, xd_ref)
    h = (x * (1.0 + sc) + sh).astype(bf16)
    proj = _dot(h, wb_ref[...])
    for j in range(4):
        c0 = C_QA + LANE * j
        qa_ref[:, LANE * j:LANE * (j + 1)] = _rope_chunk(proj[:, c0:c0 + LANE], ta_ref, qa_shift).astype(bf16)
    ka_ref[...] = _rope_chunk(proj[:, C_KA:C_KA + LANE], ta_ref, qa_shift)
    va_ref[...] = proj[:, C_VA:C_VA + LANE]
    cq = proj[:, C_CQ:C_CQ + MLA_Q_RANK]
    cq = cq * lax.rsqrt(jnp.mean(cq * cq, axis=-1, keepdims=True) + RMS_EPS) * qn_ref[...]
    ckv = proj[:, C_CKV:C_CKV + MLA_KV_RANK]
    ckv = ckv * lax.rsqrt(jnp.mean(ckv * ckv, axis=-1, keepdims=True) + RMS_EPS) * kvn_ref[...]
    kr = _rope_chunk(proj[:, C_KR:C_KR + LANE], tm_ref, qm_shift)
    kr_ref[...] = kr

    qm = _dot(cq.astype(bf16), wuq_ref[...])
    qm_ref[:, 0:MLA_NN] = qm[:, 0:MLA_NN].astype(bf16)
    for j in range(2):
        c0 = MLA_NN + LANE * j
        qm_ref[:, c0:c0 + LANE] = _rope_chunk(qm[:, c0:c0 + LANE], tm_ref, qm_shift).astype(bf16)
    kvl_ref[...] = _dot(ckv.astype(bf16), wukv_ref[...]).astype(bf16)

    @pl.when(i < T_CTX // TOK_TILE)
    def _():
        ckv_ref[...] = ckv
        kr_t = kr_ref[...].T
        for b in range(TOK_TILE // CTX_L):
            krc_ref[b, 0] = kr_t[0:MLA_ROPE, CTX_L * b:CTX_L * (b + 1)]


def _rope_block_index(i):
    tiles_ctx = T_CTX // TOK_TILE
    per_seq = DEN_L // TOK_TILE
    return jnp.where(i < tiles_ctx, 0, 1 + (i - tiles_ctx) % per_seq)


def _ab_proj(xc, xd, mods0, w_in, q_norm, kv_norm, w_uq_p, w_ukv_p):
    tab_a, qa_shift = _rope_table_array(A_HD)
    tab_m, qm_shift = _rope_table_array(MLA_ROPE)
    row_spec = lambda w: pl.BlockSpec((TOK_TILE, w), lambda i: (i, 0))
    xc_spec, xd_spec = _two_stream_specs(TOK_TILE, D)
    tab_spec = pl.BlockSpec((3, TOK_TILE, LANE), lambda i: (0, _rope_block_index(i), 0))
    outs = [(512, bf16), (LANE, f32), (LANE, f32), (LANE, f32), (LANE, f32), (768, bf16), (1024, bf16)]
    last_ctx = T_CTX // TOK_TILE - 1
    per = TOK_TILE // CTX_L
    ctx_spec = pl.BlockSpec((TOK_TILE, LANE), lambda i: (jnp.minimum(i, last_ctx), 0))
    krc_spec = pl.BlockSpec((per, 1, MLA_ROPE, CTX_L), lambda i: (jnp.minimum(i, last_ctx), 0, 0, 0))
    return pl.pallas_call(
        functools.partial(_ab_proj_kernel, qa_shift=qa_shift, qm_shift=qm_shift),
        grid=(T_ALL // TOK_TILE,),
        in_specs=[xc_spec, xd_spec, _full((8, 6 * D), 1), _full(w_in.shape, 1), _full((1, MLA_Q_RANK), 1),
                  _full((1, MLA_KV_RANK), 1), _full((MLA_Q_RANK, 768), 1), _full((MLA_KV_RANK, 1024), 1),
                  tab_spec, tab_spec],
        out_specs=[ctx_spec if k == 3 else row_spec(w) for k, (w, _) in enumerate(outs)] + [krc_spec],
        out_shape=[jax.ShapeDtypeStruct((T_CTX if k == 3 else T_ALL, w), dt) for k, (w, dt) in enumerate(outs)]
        + [jax.ShapeDtypeStruct((N_CTX_B, 1, MLA_ROPE, CTX_L), f32)],
        scratch_shapes=[pltpu.VMEM((D, PROJ_W), bf16)],
        compiler_params=_params(1),
        name="ab_proj",
    )(xc, xd, mods0, w_in, q_norm, kv_norm, w_uq_p, w_ukv_p, tab_a, tab_m)


def _softmax_blocks(s_refs, p_refs, sink_col=None):
    m = s_refs[0][...].max(axis=-1, keepdims=True)
    for s_ref in s_refs[1:]:
        m = jnp.maximum(m, s_ref[...].max(axis=-1, keepdims=True))
    if sink_col is not None:
        m = jnp.maximum(m, sink_col)
    for s_ref, p_ref in zip(s_refs, p_refs):
        p_ref[...] = jnp.exp(s_ref[...] - m).astype(bf16)
    return 0.0 if sink_col is None else jnp.exp(sink_col - m)


def _with_ones(v, axis=1):
    return jnp.concatenate([v, jnp.ones(v.shape, v.dtype)], axis=axis)


def _normalise(o_aug, extra, width):
    return o_aug[:, 0:width] * (1.0 / (o_aug[:, width:width + 1] + extra))


def _sink_column(sink_ref, rows_per_head):
    return jnp.concatenate([jnp.full((rows_per_head, 1), sink_ref[h], f32) for h in range(A_HEADS)], axis=0)


def _mla_q(qm_ref, h):
    rows = qm_ref.shape[0]
    return jnp.concatenate([qm_ref[:, MLA_NOPE * h:MLA_NOPE * (h + 1)],
                            qm_ref[:, MLA_NN + MLA_ROPE * h:MLA_NN + MLA_ROPE * (h + 1)],
                            jnp.zeros((rows, LANE - MLA_NOPE - MLA_ROPE), bf16)], axis=1)


def _mla_k(k_nope_h, k_rope):
    rows = k_nope_h.shape[0]
    return jnp.concatenate([k_nope_h, k_rope, jnp.zeros((rows, LANE - MLA_NOPE - MLA_ROPE), bf16)], axis=1)


def _mix_out_ln(merged_ref, wout_ref, x, mods_ref, r, g_ref, b_ref):
    out = _dot(merged_ref[...], wout_ref[...])
    gate = mods_ref[pl.ds(r, 1), 2 * D:3 * D]
    return _layer_norm(ALPHA * x + gate * out, g_ref[...], b_ref[...])


def _ctx_attn_kernel(sink_ref, qa_ref, ka_ref, va_ref, qm_ref, kvl_ref, kr_ref, x_ref, mods_ref, wout_ref,
                     g_ref, b_ref, o_ref, nk_ref, nv_ref, merged_ref, sa_ref, sm_ref, pa_ref, pm_ref):
    nk_ref[0, 0] = ka_ref[...].T.reshape(A_KV_HEADS, A_HD, CTX_L)
    nv_ref[0, 0] = va_ref[...].T.reshape(A_KV_HEADS, A_HD, CTX_L)
    n = CTX_L
    ka = ka_ref[...].astype(bf16)
    va = va_ref[...].astype(bf16)
    for j in range(A_KV_HEADS):
        q4 = jnp.concatenate([qa_ref[:, A_HD * h:A_HD * (h + 1)] for h in range(A_GROUP * j, A_GROUP * (j + 1))],
                             axis=0)
        sa_ref[A_GROUP * n * j:A_GROUP * n * (j + 1), :] = _dot_nt(q4, ka[:, A_HD * j:A_HD * (j + 1)]) * A_SCALE
    kr = kr_ref[:, 0:MLA_ROPE].astype(bf16)
    for h in range(MLA_HEADS):
        k_cat = _mla_k(kvl_ref[:, MLA_NOPE * h:MLA_NOPE * (h + 1)], kr)
        sm_ref[n * h:n * (h + 1), :] = _dot_nt(_mla_q(qm_ref, h), k_cat) * MLA_SCALE
    sink_a = _softmax_blocks([sa_ref], [pa_ref], _sink_column(sink_ref, n))
    _softmax_blocks([sm_ref], [pm_ref])
    for j in range(A_KV_HEADS):
        rows = slice(A_GROUP * n * j, A_GROUP * n * (j + 1))
        o4 = _normalise(_dot(pa_ref[rows, :], _with_ones(va[:, A_HD * j:A_HD * (j + 1)])), sink_a[rows], A_HD)
        for g in range(A_GROUP):
            h = A_GROUP * j + g
            merged_ref[:, A_HD * h:A_HD * (h + 1)] = o4[n * g:n * (g + 1)].astype(bf16)
    for h in range(MLA_HEADS):
        rows = slice(n * h, n * (h + 1))
        v = _with_ones(kvl_ref[:, MLA_NN + MLA_V * h:MLA_NN + MLA_V * (h + 1)])
        merged_ref[:, MLA_NN + MLA_V * h:MLA_NN + MLA_V * (h + 1)] = (
            _normalise(_dot(pm_ref[rows, :], v), 0.0, MLA_V).astype(bf16))
    o_ref[...] = _mix_out_ln(merged_ref, wout_ref, x_ref[...], mods_ref, 0, g_ref, b_ref)


def _ctx_attn(sink, qa, ka, va, qm, kvl, kr, x_all, mods0, w_out, ln_g, ln_b):
    blk = lambda w: pl.BlockSpec((CTX_L, w), lambda b: (b, 0))
    cache_blk = pl.BlockSpec((1, 1, A_KV_HEADS, A_HD, CTX_L), lambda b: (b, 0, 0, 0, 0))
    cache_shape = jax.ShapeDtypeStruct((N_CTX_B, 1, A_KV_HEADS, A_HD, CTX_L), f32)
    return pl.pallas_call(
        _ctx_attn_kernel,
        grid=(N_CTX_B,),
        in_specs=[pl.BlockSpec(memory_space=pltpu.SMEM), blk(512), blk(LANE), blk(LANE), blk(768), blk(1024),
                  blk(LANE), blk(D), _full((8, 6 * D), 1), _full((D, D), 1), _full((1, D), 1), _full((1, D), 1)],
        out_specs=[blk(D), cache_blk, cache_blk],
        out_shape=[jax.ShapeDtypeStruct((T_CTX, D), f32), cache_shape, cache_shape],
        scratch_shapes=[pltpu.VMEM((CTX_L, D), bf16),
                        pltpu.VMEM((A_HEADS * CTX_L, CTX_L), f32), pltpu.VMEM((MLA_HEADS * CTX_L, CTX_L), f32),
                        pltpu.VMEM((A_HEADS * CTX_L, CTX_L), bf16), pltpu.VMEM((MLA_HEADS * CTX_L, CTX_L), bf16)],
        compiler_params=_params(1),
        name="ctx_attn",
    )(sink, qa, ka, va, qm, kvl, kr, x_all, mods0, w_out, ln_g, ln_b)


QB = 256
WIN = QB + 2 * WINDOW
DEN_BLK0 = T_CTX // DEN_L
MLA_KEYS = CTX_L + DEN_L


def _den_attn_kernel(sink_ref, qa_ref, ka_ref, va_ref, cak_ref, cav_ref, qm_ref, kvl_ref, kr_ref, cckv_ref, ckr_ref,
                     wukv_ref, x_ref, mods_ref, wout_ref, g_ref, b_ref, o_ref, merged_ref, kcat_ref, vcat_ref,
                     saw_ref, sac_ref, sm_ref, paw_ref, pac_ref, pm_ref):
    b = pl.program_id(0)
    n = pl.program_id(1)

    @pl.when(n == 0)
    def _():
        kvc = _dot(cckv_ref[0].astype(bf16), wukv_ref[...]).astype(bf16)
        kr_ctx = ckr_ref[0].T.astype(bf16)
        kr_lat = kr_ref[:, 0:MLA_ROPE].astype(bf16)
        for h in range(MLA_HEADS):
            ns = slice(MLA_NOPE * h, MLA_NOPE * (h + 1))
            vs = slice(MLA_NN + MLA_V * h, MLA_NN + MLA_V * (h + 1))
            kcat_ref[h, 0:CTX_L, :] = _mla_k(kvc[:, ns], kr_ctx)
            kcat_ref[h, CTX_L:MLA_KEYS, :] = _mla_k(kvl_ref[:, ns], kr_lat)
            vcat_ref[h, 0:CTX_L, :] = _with_ones(kvc[:, vs])
            vcat_ref[h, CTX_L:MLA_KEYS, :] = _with_ones(kvl_ref[:, vs])

    start = pl.multiple_of(jnp.clip(QB * n - WINDOW, 0, DEN_L - WIN), WINDOW)
    grp_rows = A_GROUP * QB
    qpos = QB * n + (lax.broadcasted_iota(jnp.int32, (grp_rows, WIN), 0) & (QB - 1))
    kpos = start + lax.broadcasted_iota(jnp.int32, (grp_rows, WIN), 1)
    valid = jnp.abs(qpos - kpos) <= WINDOW
    kwin = ka_ref[pl.ds(start, WIN), :].astype(bf16)
    vwin = va_ref[pl.ds(start, WIN), :].astype(bf16)
    kctx_t = [cak_ref[0, j].astype(bf16) for j in range(A_KV_HEADS)]
    vctx_t = [cav_ref[0, j].astype(bf16) for j in range(A_KV_HEADS)]
    for j in range(A_KV_HEADS):
        sl = slice(A_HD * j, A_HD * (j + 1))
        rows = slice(grp_rows * j, grp_rows * (j + 1))
        q4 = jnp.concatenate([qa_ref[:, A_HD * h:A_HD * (h + 1)] for h in range(A_GROUP * j, A_GROUP * (j + 1))],
                             axis=0)
        saw_ref[rows, :] = jnp.where(valid, _dot_nt(q4, kwin[:, sl]) * A_SCALE, NEG_INF)
        sac_ref[rows, :] = _dot(q4, kctx_t[j]) * A_SCALE
    for h in range(MLA_HEADS):
        sm_ref[QB * h:QB * (h + 1), :] = _dot_nt(_mla_q(qm_ref, h), kcat_ref[h]) * MLA_SCALE
    sink_a = _softmax_blocks([saw_ref, sac_ref], [paw_ref, pac_ref], _sink_column(sink_ref, QB))
    _softmax_blocks([sm_ref], [pm_ref])
    for j in range(A_KV_HEADS):
        sl = slice(A_HD * j, A_HD * (j + 1))
        rows = slice(grp_rows * j, grp_rows * (j + 1))
        o_aug = (_dot(paw_ref[rows, :], _with_ones(vwin[:, sl]))
                 + _dot_nt(pac_ref[rows, :], _with_ones(vctx_t[j], axis=0)))
        o4 = _normalise(o_aug, sink_a[rows], A_HD)
        for g in range(A_GROUP):
            h = A_GROUP * j + g
            merged_ref[:, A_HD * h:A_HD * (h + 1)] = o4[QB * g:QB * (g + 1)].astype(bf16)
    for h in range(MLA_HEADS):
        rows = slice(QB * h, QB * (h + 1))
        o = _normalise(_dot(pm_ref[rows, :], vcat_ref[h]), 0.0, MLA_V)
        merged_ref[:, MLA_NN + MLA_V * h:MLA_NN + MLA_V * (h + 1)] = o.astype(bf16)
    o_ref[...] = _mix_out_ln(merged_ref, wout_ref, x_ref[...], mods_ref, 1 + b, g_ref, b_ref)


def _den_attn(sink, qa, ka, va, cache_k, cache_v, qm, kvl, kr, cache_ckv, cache_kr, w_ukv_p, x_all, mods0, w_out,
              ln_g, ln_b):
    nq = DEN_L // QB
    qblk = lambda w: pl.BlockSpec((QB, w), lambda b, n: (T_CTX // QB + b * nq + n, 0))
    seq = lambda w: pl.BlockSpec((DEN_L, w), lambda b, n: (DEN_BLK0 + b, 0))
    cache = lambda w: pl.BlockSpec((1, CTX_L, w), lambda b, n: (b, 0, 0))
    cache_a = pl.BlockSpec((1, A_KV_HEADS, A_HD, CTX_L), lambda b, n: (b, 0, 0, 0))
    return pl.pallas_call(
        _den_attn_kernel,
        grid=(N_DEN_B, nq),
        in_specs=[pl.BlockSpec(memory_space=pltpu.SMEM), qblk(512), seq(LANE), seq(LANE), cache_a, cache_a,
                  qblk(768), seq(1024), seq(LANE), cache(MLA_KV_RANK),
                  pl.BlockSpec((1, MLA_ROPE, CTX_L), lambda b, n: (b, 0, 0)),
                  _full((MLA_KV_RANK, 1024), 2), pl.BlockSpec((QB, D), lambda b, n: (b * nq + n, 0)),
                  _full((8, 6 * D), 2), _full((D, D), 2), _full((1, D), 2),
                  _full((1, D), 2)],
        out_specs=pl.BlockSpec((QB, D), lambda b, n: (b * nq + n, 0)),
        out_shape=jax.ShapeDtypeStruct((T_DEN, D), f32),
        scratch_shapes=[pltpu.VMEM((QB, D), bf16), pltpu.VMEM((MLA_HEADS, MLA_KEYS, LANE), bf16),
                        pltpu.VMEM((MLA_HEADS, MLA_KEYS, 2 * MLA_V), bf16)]
        + [pltpu.VMEM((A_HEADS * QB, w), dt) for dt in (f32, bf16) for w in (WIN, CTX_L, MLA_KEYS)],
        compiler_params=_params(2),
        name="den_attn",
    )(sink, qa, ka, va, cache_k, cache_v, qm, kvl, kr, cache_ckv, cache_kr, w_ukv_p, x_all, mods0, w_out, ln_g, ln_b)


SUBTILE = 256


def _route(x1, mrow, rw_ref, rb_ref):
    sh, sc = mrow[:, 3 * D:4 * D], mrow[:, 4 * D:5 * D]
    h = x1 * (1.0 + sc) + sh
    h_hi = h.astype(bf16)
    h_lo = (h - h_hi.astype(f32)).astype(bf16)
    logits = _dot(h_hi, rw_ref[0]) + (_dot(h_hi, rw_ref[1]) + _dot(h_lo, rw_ref[0]))
    scores = jax.nn.sigmoid(logits)
    lane = lax.broadcasted_iota(jnp.int32, scores.shape, 1).astype(f32)
    sel = jnp.where(lane < N_EXPERTS, scores + rb_ref[...], -jnp.inf)
    gates = jnp.zeros_like(scores)
    for _ in range(TOP_K):
        m = sel.max(axis=-1, keepdims=True)
        idx = jnp.where(sel == m, lane, float(LANE)).min(axis=-1, keepdims=True)
        hit = lane == idx
        gates = jnp.where(hit, scores, gates)
        sel = jnp.where(hit, -jnp.inf, sel)
    return h_hi, gates / gates.sum(axis=-1, keepdims=True) * ROUTED_SCALE


def _router_kernel(xc_ref, xd_ref, mods_ref, rw_ref, rb_ref, x_ref, h_ref, gates_ref):
    i = pl.program_id(0)
    r = _mod_row(i, TOK_TILE)
    mrow = mods_ref[pl.ds(r, 1), :]
    x1 = _pick(i, TOK_TILE, xc_ref, xd_ref)
    x_ref[...] = x1
    h_ref[...], gates_ref[...] = _route(x1, mrow, rw_ref, rb_ref)


def _router(x1c, x1d, mods_l, router_w_p, router_b_p):
    row_spec = lambda w: pl.BlockSpec((TOK_TILE, w), lambda i: (i, 0))
    xc_spec, xd_spec = _two_stream_specs(TOK_TILE, D)
    return pl.pallas_call(
        _router_kernel,
        grid=(T_ALL // TOK_TILE,),
        in_specs=[xc_spec, xd_spec, _full((8, 6 * D), 1), _full((2, D, LANE), 1), _full((1, LANE), 1)],
        out_specs=[row_spec(D), row_spec(D), row_spec(LANE)],
        out_shape=[jax.ShapeDtypeStruct((T_ALL, D), f32), jax.ShapeDtypeStruct((T_ALL, D), bf16),
                   jax.ShapeDtypeStruct((T_ALL, LANE), f32)],
        compiler_params=_params(1),
        name="router",
    )(x1c, x1d, mods_l, router_w_p, router_b_p)


MOE_TOK = 1536
MOE_EG = 8
MOE_VMEM_LIMIT = 60 * 1024 * 1024
MOE_TILE = 512
MOE_FF = MOE_EG * EXPERT_FF


def _moe_kernel(h_ref, gates_ref, mods_ref, wg_ref, wu_ref, wd_ref, sg_ref, su_ref, sd_ref, o_ref):
    p = pl.program_id(0)
    e = pl.program_id(1)
    n_tiles = MOE_TOK // MOE_TILE

    def gate_f(t):
        r = _mod_row(p * n_tiles + t, MOE_TILE)
        return mods_ref[pl.ds(r, 1), 5 * D:6 * D]

    def rows_of(t):
        if isinstance(t, int):
            return pl.ds(t * MOE_TILE, MOE_TILE)
        return pl.ds(pl.multiple_of(t * MOE_TILE, MOE_TILE), MOE_TILE)

    @pl.when(e == 0)
    def _():
        sg = sg_ref[...].astype(bf16)
        su = su_ref[...].astype(bf16)
        sd = sd_ref[...].astype(bf16)

        def body(t, c):
            rows = rows_of(t)
            ht = h_ref[rows, :]
            hid = _silu(_dot(ht, sg)) * _dot(ht, su)
            o_ref[rows, :] = gate_f(t) * _dot(hid.astype(bf16), sd)
            return c

        lax.fori_loop(0, n_tiles, body, 0)

    wg = jnp.concatenate([wg_ref[k].astype(bf16) for k in range(MOE_EG)], axis=1)
    wu = jnp.concatenate([wu_ref[k].astype(bf16) for k in range(MOE_EG)], axis=1)
    wd = jnp.concatenate([wd_ref[k].astype(bf16) for k in range(MOE_EG)], axis=0)
    lane = lax.broadcasted_iota(jnp.int32, (MOE_TILE, LANE), 1)

    def body(t, c):
        rows = rows_of(t)
        ht = h_ref[rows, :]
        hid = _silu(_dot(ht, wg)) * _dot(ht, wu)
        gt = gates_ref[rows, :]
        parts = []
        for k in range(MOE_EG):
            col = jnp.where(lane == e * MOE_EG + k, gt, 0.0).sum(axis=-1, keepdims=True)
            parts.append((hid[:, EXPERT_FF * k:EXPERT_FF * (k + 1)] * col).astype(bf16))
        o_ref[rows, :] += gate_f(t) * _dot(jnp.concatenate(parts, axis=1), wd)
        return c

    for t in range(n_tiles):
        body(t, 0)


def _moe(l, h, gates, mods_l, wg, wu, wd, sg, su, sd):
    tok = lambda w: pl.BlockSpec((MOE_TOK, w), lambda p, e: (p, 0))
    return pl.pallas_call(
        _moe_kernel,
        grid=(T_ALL // MOE_TOK, N_EXPERTS // MOE_EG),
        in_specs=[tok(D), tok(LANE), _full((8, 6 * D), 2),
                  pl.BlockSpec((None, MOE_EG, D, EXPERT_FF), lambda p, e: (l, e, 0, 0)),
                  pl.BlockSpec((None, MOE_EG, D, EXPERT_FF), lambda p, e: (l, e, 0, 0)),
                  pl.BlockSpec((None, MOE_EG, EXPERT_FF, D), lambda p, e: (l, e, 0, 0)),
                  pl.BlockSpec((None, D, SHARED_FF), lambda p, e: (l, 0, 0)),
                  pl.BlockSpec((None, D, SHARED_FF), lambda p, e: (l, 0, 0)),
                  pl.BlockSpec((None, SHARED_FF, D), lambda p, e: (l, 0, 0))],
        out_specs=tok(D),
        out_shape=jax.ShapeDtypeStruct((T_ALL, D), f32),
        compiler_params=pltpu.CompilerParams(dimension_semantics=("arbitrary", "arbitrary"),
                                             vmem_limit_bytes=MOE_VMEM_LIMIT),
        name="moe",
    )(h, gates, mods_l, wg, wu, wd, sg, su, sd)


def _moe_finish_kernel(x_ref, acc_ref, g_ref, b_ref, o_ref):
    o_ref[...] = _layer_norm(ALPHA * x_ref[...] + acc_ref[...], g_ref[...], b_ref[...])


def _moe_finish(x1, acc, tile0, n_rows, ln_g, ln_b):
    src = pl.BlockSpec((TOK_TILE, D), lambda i: (tile0 + i, 0))
    return pl.pallas_call(
        _moe_finish_kernel,
        grid=(n_rows // TOK_TILE,),
        in_specs=[src, src, _full((1, D), 1), _full((1, D), 1)],
        out_specs=pl.BlockSpec((TOK_TILE, D), lambda i: (i, 0)),
        out_shape=jax.ShapeDtypeStruct((n_rows, D), f32),
        compiler_params=_params(1),
        name="moe_finish",
    )(x1, acc, ln_g, ln_b)


def _router_weights(l, router_w, router_bias):
    rw = jnp.pad(router_w[l], ((0, 0), (0, LANE - N_EXPERTS)))
    rw_hi, rw_lo = _split_bf16(rw)
    rb = jnp.pad(router_bias[l], (0, LANE - N_EXPERTS)).reshape(1, LANE)
    return jnp.stack([rw_hi, rw_lo]), rb


S5_Q = 8
S5_NGB = D // LANE


def _s5_in_kernel(x_ref, acc_ref, lg_ref, lb_ref, mods_ref, w_ref, x2_ref, u_ref, u2_ref, slab_ref, *, row_of,
                  seq_len):
    r = row_of(pl.program_id(0))
    mrow = mods_ref[pl.ds(r, 1), :]
    sh, sc = mrow[:, 0:D], mrow[:, D:2 * D]
    x2 = _layer_norm(ALPHA * x_ref[...] + acc_ref[...], lg_ref[...], lb_ref[...])
    x2_ref[...] = x2
    h = (x2 * (1.0 + sc) + sh).astype(bf16)
    u = _dot(h, w_ref[...])
    u_ref[...] = u
    for s in range(S5_NGB):
        slab_ref[s] = u[:, LANE * s:LANE * (s + 1)]
    kt = seq_len // S5_Q
    for s in range(S5_NGB):
        for q in range(TOK_TILE // seq_len):
            for j in range(S5_Q):
                u2_ref[s, q * kt:(q + 1) * kt, LANE * j:LANE * (j + 1)] = (
                    slab_ref[s, pl.ds(q * seq_len + j, kt, stride=S5_Q), :].astype(bf16))


def _s5_in(x1, acc, ln_g, ln_b, tile0, mods1, w_in_c, n_b, seq_len, row_of):
    n_tiles = n_b * seq_len // TOK_TILE
    chunks = TOK_TILE // S5_Q
    src = pl.BlockSpec((TOK_TILE, D), lambda i: (tile0 + i, 0))
    dst = pl.BlockSpec((TOK_TILE, D), lambda i: (i, 0))
    return pl.pallas_call(
        functools.partial(_s5_in_kernel, row_of=row_of, seq_len=min(seq_len, TOK_TILE)),
        grid=(n_tiles,),
        in_specs=[src, src, _full((1, D), 1), _full((1, D), 1), _full((8, 6 * D), 1), _full((D, D), 1)],
        out_specs=[dst, dst, pl.BlockSpec((S5_NGB, chunks, D), lambda i: (0, i, 0))],
        out_shape=[jax.ShapeDtypeStruct((n_b * seq_len, D), f32), jax.ShapeDtypeStruct((n_b * seq_len, D), f32),
                   jax.ShapeDtypeStruct((S5_NGB, n_tiles * chunks, D), bf16)],
        scratch_shapes=[pltpu.VMEM((S5_NGB, TOK_TILE, LANE), f32)],
        compiler_params=_params(1),
        name="s5_in",
    )(x1, acc, ln_g, ln_b, mods1, w_in_c)


S5_GL = (LANE // S5_CH) * S5_P
S5_ROWS_C = (CTX_L // S5_Q) * N_CTX_B
S5_ROWS_D = (DEN_L // S5_Q) * N_DEN_B


def _s5_scan_kernel(lam_ref, bt_ref, ctr_ref, cti_ref, uc_ref, ud_ref, h0_ref,
                    yc_ref, yd_ref, st_ref, win_ref, mso_ref, wit_ref, a_ref, s_ref, hp_ref):
    gl = S5_GL
    rowg = lax.shift_right_logical(lax.broadcasted_iota(jnp.int32, (LANE, gl), 0), 4)
    colg = lax.shift_right_logical(lax.broadcasted_iota(jnp.int32, (LANE, gl), 1), 6)
    same_group = rowg == colg
    reps = LANE // S5_CH

    def expand(t):
        return jnp.where(same_group, jnp.concatenate([t] * reps, axis=0), 0.0)

    def expand_c(t):
        return jnp.where(same_group, jnp.concatenate([t] * reps, axis=1), 0.0)

    for d in range(2):
        fwd = d == 0
        lre, lim = lam_ref[d, 0:1, :], lam_ref[d, 1:2, :]
        dt = jnp.exp(lam_ref[d, 2:3, :])
        a, w = lre * dt, lim * dt
        pre = [jnp.exp(m * a) * jnp.cos(m * w) for m in range(S5_Q + 1)]
        pim = [jnp.exp(m * a) * jnp.sin(m * w) for m in range(S5_Q + 1)]
        xr, xi = pre[1] - 1.0, pim[1]
        den = lre * lre + lim * lim
        cfr, cfi = (xr * lre + xi * lim) / den, (xi * lre - xr * lim) / den
        btr, bti = bt_ref[0, d], bt_ref[1, d]
        bexp_r = expand(cfr * btr - cfi * bti)
        bexp_i = expand(cfr * bti + cfi * btr)
        cexp_r, cexp_i = expand_c(ctr_ref[d]), expand_c(cti_ref[d])
        for m in range(S5_Q + 1):
            a_ref[m, :, 0:gl] = cexp_r * pre[m] - cexp_i * pim[m]
            a_ref[m, :, gl:2 * gl] = -(cexp_r * pim[m] + cexp_i * pre[m])
        for j in range(S5_Q):
            m = S5_Q - 1 - j if fwd else j
            win_ref[LANE * j:LANE * (j + 1), 0:gl] = (pre[m] * bexp_r - pim[m] * bexp_i).astype(bf16)
            win_ref[LANE * j:LANE * (j + 1), gl:2 * gl] = (pre[m] * bexp_i + pim[m] * bexp_r).astype(bf16)
        for j in range(S5_Q):
            m = j + 1 if fwd else S5_Q - j
            mso_ref[LANE * j:LANE * (j + 1), :] = a_ref[m].astype(bf16)
        b2 = jnp.concatenate([bexp_r, bexp_i], axis=1).astype(bf16)
        kt = [_dot_nt(b2, a_ref[tau].astype(bf16)) for tau in range(S5_Q)]
        for j in range(S5_Q):
            for jp in range(S5_Q):
                tau = jp - j if fwd else j - jp
                blk = slice(LANE * j, LANE * (j + 1)), slice(LANE * jp, LANE * (jp + 1))
                if fwd:
                    wit_ref[blk] = kt[tau] if tau >= 0 else jnp.zeros((LANE, LANE), f32)
                elif tau >= 0:
                    wit_ref[blk] = wit_ref[blk] + kt[tau]

        l8r, l8i = pre[S5_Q], pim[S5_Q]

        nsl = gl // LANE

        def slabs(ref, rs, first):
            return jnp.concatenate([ref[first + sl, rs, :] for sl in range(nsl)], axis=1)

        def put_slabs(ref, rs, first, val):
            for sl in range(nsl):
                ref[first + sl, rs, :] = val[:, LANE * sl:LANE * (sl + 1)]

        def advance(hr, hi_, sr, si):
            return l8r * hr - l8i * hi_ + sr, l8r * hi_ + l8i * hr + si

        def run(u_ref, y_ref, n_b, n_k, h_init):
            rows = n_b * n_k
            s = _dot(u_ref[0], win_ref[...])
            if n_b % SUB == 0:
                pitch = n_k + 1
                for bb in range(n_b):
                    dst = slice(bb * pitch, bb * pitch + n_k)
                    put_slabs(s_ref, dst, 0, s[bb * n_k:(bb + 1) * n_k, 0:gl])
                    put_slabs(s_ref, dst, nsl, s[bb * n_k:(bb + 1) * n_k, gl:2 * gl])

                def step(i, carry):
                    hr, hi_ = carry
                    rs = pl.ds(i if fwd else n_k - 1 - i, n_b, stride=pitch)
                    put_slabs(hp_ref, rs, 0, hr)
                    put_slabs(hp_ref, rs, nsl, hi_)
                    return advance(hr, hi_, slabs(s_ref, rs, 0), slabs(s_ref, rs, nsl))

                h_fin = lax.fori_loop(0, n_k, step, h_init)
                hp = jnp.concatenate(
                    [jnp.concatenate([hp_ref[sl, bb * pitch:bb * pitch + n_k, :] for sl in range(2 * nsl)], axis=1)
                     for bb in range(n_b)], axis=0).astype(bf16)
            else:
                put_slabs(s_ref, slice(0, rows), 0, s[:, 0:gl])
                put_slabs(s_ref, slice(0, rows), nsl, s[:, gl:2 * gl])
                n_it = n_k // SUB

                def step(i, carry):
                    it = i if fwd else n_it - 1 - i
                    out = []
                    for bb in range(n_b):
                        hr, hi_ = carry[bb]
                        rs = pl.ds(pl.multiple_of(bb * n_k + it * SUB, SUB), SUB)
                        s_re, s_im = slabs(s_ref, rs, 0), slabs(s_ref, rs, nsl)
                        prev_r, prev_i = [None] * SUB, [None] * SUB
                        for sub in (range(SUB) if fwd else reversed(range(SUB))):
                            prev_r[sub], prev_i[sub] = hr, hi_
                            hr, hi_ = advance(hr, hi_, s_re[sub:sub + 1], s_im[sub:sub + 1])
                        put_slabs(hp_ref, rs, 0, jnp.concatenate(prev_r, axis=0))
                        put_slabs(hp_ref, rs, nsl, jnp.concatenate(prev_i, axis=0))
                        out.append((hr, hi_))
                    return tuple(out)

                fin = lax.fori_loop(0, n_it, step, tuple((h_init[0][bb:bb + 1], h_init[1][bb:bb + 1])
                                                          for bb in range(n_b)))
                h_fin = (jnp.concatenate([f[0] for f in fin], axis=0), jnp.concatenate([f[1] for f in fin], axis=0))
                hp = jnp.concatenate([hp_ref[sl, 0:rows, :] for sl in range(2 * nsl)], axis=1).astype(bf16)
            y = _dot_nt(hp, mso_ref[...])
            if fwd:
                y_ref[0] = y
            else:
                y_ref[0] += y
            return h_fin

        zeros = jnp.zeros((N_CTX_B, gl), f32)
        hr, hi_ = run(uc_ref, yc_ref, N_CTX_B, CTX_L // S5_Q, (zeros, zeros))
        st_ref[d, 0] = hr
        st_ref[d, 1] = hi_
        run(ud_ref, yd_ref, N_DEN_B, DEN_L // S5_Q, (h0_ref[d, 0], h0_ref[d, 1]))

    wit = wit_ref[...].astype(bf16)
    yc_ref[0] += _dot(uc_ref[0], wit)
    yd_ref[0] += _dot(ud_ref[0], wit)


def _s5_scan(lam3, bt, ct_re, ct_im, u2c, u2d, h0):
    gl = S5_GL
    vec = pl.BlockSpec((2, 3, gl), lambda g: (0, 0, g))
    tab = pl.BlockSpec((2, 2, S5_CH, gl), lambda g: (0, 0, 0, g))
    ctab = pl.BlockSpec((2, LANE, S5_P), lambda g: (0, g, 0))
    rows = lambda n: pl.BlockSpec((1, n, D), lambda g: (g, 0, 0))
    return pl.pallas_call(
        _s5_scan_kernel,
        grid=(S5_NGB,),
        in_specs=[vec, tab, ctab, ctab, rows(S5_ROWS_C), rows(S5_ROWS_D),
                  pl.BlockSpec((2, 2, N_DEN_B, gl), lambda g: (0, 0, 0, g))],
        out_specs=[rows(S5_ROWS_C), rows(S5_ROWS_D), pl.BlockSpec((2, 2, N_CTX_B, gl), lambda g: (0, 0, 0, g))],
        out_shape=[jax.ShapeDtypeStruct((S5_NGB, S5_ROWS_C, D), f32), jax.ShapeDtypeStruct((S5_NGB, S5_ROWS_D, D), f32),
                   jax.ShapeDtypeStruct((2, 2, N_CTX_B, S5_G * S5_P), f32)],
        scratch_shapes=[pltpu.VMEM((D, 2 * gl), bf16), pltpu.VMEM((D, 2 * gl), bf16), pltpu.VMEM((D, D), f32),
                        pltpu.VMEM((S5_Q + 1, LANE, 2 * gl), f32),
                        pltpu.VMEM((2 * gl // LANE, S5_ROWS_C + 2 * N_CTX_B, LANE), f32),
                        pltpu.VMEM((2 * gl // LANE, S5_ROWS_C + 2 * N_CTX_B, LANE), f32)],
        compiler_params=_params(1),
        name="s5_scan",
    )(lam3, bt, ct_re, ct_im, u2c, u2d, h0)


def _gelu_tanh(x):
    return 0.5 * x * (1.0 + jnp.tanh(np.sqrt(2.0 / np.pi).astype(np.float32) * (x + 0.044715 * (x * x * x))))


def _s5_out_kernel(xc_ref, xd_ref, uc_ref, ud_ref, yc_ref, yd_ref, mods_ref, dsk_ref, wout_ref, g_ref, b_ref, rw_ref,
                   rb_ref, x1_ref, h_ref, gates_ref, slab_ref, wob_ref):
    i = pl.program_id(0)

    @pl.when(i == 0)
    def _():
        wob_ref[...] = wout_ref[...].astype(bf16)

    is_ctx = i < T_CTX // TOK_TILE
    r = _mod_row(i, TOK_TILE)
    mrow = mods_ref[pl.ds(r, 1), :]
    u = _pick(i, TOK_TILE, uc_ref, ud_ref)

    kt = CTX_L // S5_Q
    for s in range(S5_NGB):
        for q in range(TOK_TILE // CTX_L):
            for j in range(S5_Q):
                blk = (s, slice(q * kt, (q + 1) * kt), slice(LANE * j, LANE * (j + 1)))
                slab_ref[s, pl.ds(q * CTX_L + j, kt, stride=S5_Q), :] = jnp.where(is_ctx, yc_ref[blk], yd_ref[blk])
    halves = [slice(a, a + SUBTILE) for a in range(0, TOK_TILE, SUBTILE)]
    zs = []
    for rows in halves:
        y = jnp.concatenate([slab_ref[s, rows, :] for s in range(S5_NGB)], axis=1) + dsk_ref[...] * u[rows]
        zs.append(_dot(_gelu_tanh(y).astype(bf16), wob_ref[...]))
    for rows, z in zip(halves, zs):
        out = z[:, 0:D] * jax.nn.sigmoid(z[:, D:2 * D])
        x = jnp.where(is_ctx, xc_ref[rows, :], xd_ref[rows, :])
        x1 = _layer_norm(ALPHA * x + mrow[:, 2 * D:3 * D] * out, g_ref[...], b_ref[...])
        x1_ref[rows, :] = x1
        h_ref[rows, :], gates_ref[rows, :] = _route(x1, mrow, rw_ref, rb_ref)


def _s5_out(xc, xd, uc, ud, yc, yd, mods1, d_skip, w_out_c, ln_g, ln_b, rw, rb):
    row_spec = lambda w: pl.BlockSpec((TOK_TILE, w), lambda i: (i, 0))
    uc_spec, ud_spec = _two_stream_specs(TOK_TILE, D)
    n_ctx = T_CTX // TOK_TILE
    chunks = TOK_TILE // S5_Q
    return pl.pallas_call(
        _s5_out_kernel,
        grid=(T_ALL // TOK_TILE,),
        in_specs=[uc_spec, ud_spec, uc_spec, ud_spec,
                  pl.BlockSpec((S5_NGB, chunks, D), lambda i: (0, jnp.minimum(i, n_ctx - 1), 0)),
                  pl.BlockSpec((S5_NGB, chunks, D), lambda i: (0, jnp.maximum(i - n_ctx, 0), 0)),
                  _full((8, 6 * D), 1), _full((1, D), 1), _full((D, 2 * D), 1), _full((1, D), 1), _full((1, D), 1),
                  _full((2, D, LANE), 1), _full((1, LANE), 1)],
        out_specs=[row_spec(D), row_spec(D), row_spec(LANE)],
        out_shape=[jax.ShapeDtypeStruct((T_ALL, D), f32), jax.ShapeDtypeStruct((T_ALL, D), bf16),
                   jax.ShapeDtypeStruct((T_ALL, LANE), f32)],
        scratch_shapes=[pltpu.VMEM((S5_NGB, TOK_TILE, LANE), f32), pltpu.VMEM((D, 2 * D), bf16)],
        compiler_params=_params(1),
        name="s5_out",
    )(xc, xd, uc, ud, yc, yd, mods1, d_skip, w_out_c, ln_g, ln_b, rw, rb)


def kernel(x_prompt, x_sample, c, cache_attn_k, cache_attn_v, cache_mla_ckv, cache_mla_krope, state_ssm, c_ctx,
           ada_w, ada_b, ln_mix_g, ln_mix_b, ln_ffn_g, ln_ffn_b, w_in_ab, attn_sink, mla_q_norm, mla_kv_norm,
           mla_w_uq, mla_w_ukv, w_out_ab, w_in_c, s5_lam_re, s5_lam_im, s5_log_dt, s5_b_re, s5_b_im, s5_c_re,
           s5_c_im, s5_d, w_out_c, router_w, router_bias, exp_w_gate, exp_w_up, exp_w_down, sh_w_gate, sh_w_up,
           sh_w_down):
    row = lambda v: v.reshape(1, -1)
    xc, xd = x_prompt.reshape(T_CTX, D), x_sample.reshape(T_DEN, D)
    mods = _adaln(row(c_ctx), c, ada_w, ada_b)

    uq = mla_w_uq[0].reshape(MLA_Q_RANK, MLA_HEADS, MLA_NOPE + MLA_ROPE)
    w_uq_p = jnp.concatenate([uq[:, :, :MLA_NOPE].reshape(MLA_Q_RANK, -1), uq[:, :, MLA_NOPE:].reshape(MLA_Q_RANK, -1)],
                             axis=1).astype(bf16)
    ukv = mla_w_ukv[0].reshape(MLA_KV_RANK, MLA_HEADS, MLA_NOPE + MLA_V)
    w_ukv_p = jnp.concatenate([ukv[:, :, :MLA_NOPE].reshape(MLA_KV_RANK, -1),
                               ukv[:, :, MLA_NOPE:].reshape(MLA_KV_RANK, -1)], axis=1).astype(bf16)
    qa, ka, va, ckv, kr, qm, kvl, kr_ctx_t = _ab_proj(xc, xd, mods[0], w_in_ab[0].T, row(mla_q_norm[0]), row(mla_kv_norm[0]),
                                            w_uq_p, w_ukv_p)
    w_out_b = w_out_ab[0].astype(bf16)
    g0, b0 = row(ln_mix_g[0]), row(ln_mix_b[0])
    x1c, nk_t, nv_t = _ctx_attn(attn_sink[0], qa, ka, va, qm, kvl, kr, xc, mods[0], w_out_b, g0, b0)
    new_attn_k = jnp.transpose(nk_t, (0, 1, 4, 2, 3))
    new_attn_v = jnp.transpose(nv_t, (0, 1, 4, 2, 3))
    x1d = _den_attn(attn_sink[0], qa, ka, va,
                    jnp.transpose(cache_attn_k[:, 0], (0, 2, 3, 1)), jnp.transpose(cache_attn_v[:, 0], (0, 2, 3, 1)),
                    qm, kvl, kr, cache_mla_ckv[:, 0], jnp.transpose(cache_mla_krope[:, 0], (0, 2, 1)), w_ukv_p, xd, mods[0], w_out_b, g0, b0)
    rw0, rb0 = _router_weights(0, router_w, router_bias)
    x1, h, gates = _router(x1c, x1d, mods[0], rw0, rb0)
    acc = _moe(0, h, gates, mods[0], exp_w_gate, exp_w_up, exp_w_down, sh_w_gate, sh_w_up, sh_w_down)

    w_in_c_b = w_in_c[0].astype(bf16)
    lg0, lb0 = row(ln_ffn_g[0]), row(ln_ffn_b[0])
    n_ctx_tiles = T_CTX // TOK_TILE
    x2c, uc, u2c = _s5_in(x1, acc, lg0, lb0, 0, mods[1], w_in_c_b, N_CTX_B, CTX_L, lambda i: 0)
    x2d, ud, u2d = _s5_in(x1, acc, lg0, lb0, n_ctx_tiles, mods[1], w_in_c_b, N_DEN_B, DEN_L,
                          lambda i: 1 + i // (DEN_L // TOK_TILE))
    gp = S5_G * S5_P
    bt = jnp.transpose(jnp.stack([s5_b_re[0], s5_b_im[0]]), (0, 1, 4, 2, 3)).reshape(2, 2, S5_CH, gp)
    lam3 = jnp.stack([s5_lam_re[0].reshape(2, gp), s5_lam_im[0].reshape(2, gp),
                      jnp.repeat(s5_log_dt[0], S5_P, axis=-1).reshape(2, gp)], axis=1)
    chan_major_c = lambda t: t[0].reshape(2, S5_G * S5_CH, S5_P)
    h0 = jnp.transpose(state_ssm[:, 0], (1, 2, 0, 3, 4)).reshape(2, 2, N_DEN_B, gp)
    yc, yd, st = _s5_scan(lam3, bt, chan_major_c(s5_c_re), chan_major_c(s5_c_im), u2c, u2d, h0)
    rw1, rb1 = _router_weights(1, router_w, router_bias)
    x3, h, gates = _s5_out(x2c, x2d, uc, ud, yc, yd, mods[1], row(s5_d[0]),
                           w_out_c[0], row(ln_mix_g[1]), row(ln_mix_b[1]), rw1, rb1)
    acc = _moe(1, h, gates, mods[1], exp_w_gate, exp_w_up, exp_w_down, sh_w_gate, sh_w_up, sh_w_down)
    lg1, lb1 = row(ln_ffn_g[1]), row(ln_ffn_b[1])
    y_prompt = _moe_finish(x3, acc, 0, T_CTX, lg1, lb1).reshape(N_CTX_B, CTX_L, D)
    y_sample = _moe_finish(x3, acc, n_ctx_tiles, T_DEN, lg1, lb1).reshape(N_DEN_B, DEN_L, D)
    new_mla_ckv = ckv.reshape(N_CTX_B, 1, CTX_L, MLA_KV_RANK)
    new_mla_krope = jnp.transpose(kr_ctx_t, (0, 1, 3, 2))
    new_state_ssm = jnp.transpose(st, (2, 0, 1, 3)).reshape(N_CTX_B, 1, 2, 2, S5_G, S5_P)
    return (y_prompt, y_sample, new_attn_k, new_attn_v, new_mla_ckv, new_mla_krope, new_state_ssm)
```

```python
import functools

import jax
import jax.numpy as jnp
import numpy as np
from jax import lax
from jax.experimental import pallas as pl
from jax.experimental.pallas import tpu as pltpu

f32 = jnp.float32
bf16 = jnp.bfloat16

D = 1024
N_CTX_B, CTX_L = 16, 256
N_DEN_B, DEN_L = 2, 1024
T_CTX = N_CTX_B * CTX_L
T_DEN = N_DEN_B * DEN_L
T_ALL = T_CTX + T_DEN
GRID_W = 64
WINDOW = 128
ROPE_BASE = 10000.0
A_HEADS, A_KV_HEADS, A_HD = 8, 2, 64
A_GROUP = A_HEADS // A_KV_HEADS
A_SCALE = A_HD ** -0.5
MLA_HEADS, MLA_Q_RANK, MLA_KV_RANK = 8, 256, 128
MLA_NOPE, MLA_ROPE, MLA_V = 64, 32, 64
MLA_SCALE = (MLA_NOPE + MLA_ROPE) ** -0.5
N_EXPERTS, TOP_K, EXPERT_FF, SHARED_FF = 64, 6, 128, 128
ROUTED_SCALE = 2.5
DEPTH = 2
ALPHA = (2.0 * DEPTH) ** 0.25
LN_EPS = 1e-5
RMS_EPS = 1e-6
NEG_INF = -1e30
S5_G, S5_CH, S5_P = 64, 16, 64

LANE = 128
SUB = 8
VMEM_LIMIT = 56 * 1024 * 1024

TOK_TILE = 512


def _mod_row(tile_idx, tile_rows):
    start = tile_idx * tile_rows
    return jnp.where(start < T_CTX, 0, 1 + (start - T_CTX) // DEN_L)


def _layer_norm(y, g, b):
    mu = jnp.mean(y, axis=-1, keepdims=True)
    yc = y - mu
    var = jnp.mean(yc * yc, axis=-1, keepdims=True)
    return yc * lax.rsqrt(var + LN_EPS) * g + b


def _silu(x):
    return x * jax.nn.sigmoid(x)


def _dot(a, b):
    return jnp.dot(a, b, preferred_element_type=f32)


def _dot_nt(a, b):
    return lax.dot_general(a, b, (((1,), (1,)), ((), ())), preferred_element_type=f32)


def _split_bf16(a):
    hi = a.astype(bf16)
    return hi, (a - hi.astype(f32)).astype(bf16)


def _full(shape, n_grid):
    zeros = tuple(0 for _ in shape)
    return pl.BlockSpec(shape, lambda *_: zeros)


def _two_stream_specs(tile_rows, width):
    n_ctx = T_CTX // tile_rows
    return (pl.BlockSpec((tile_rows, width), lambda i: (jnp.minimum(i, n_ctx - 1), 0)),
            pl.BlockSpec((tile_rows, width), lambda i: (jnp.maximum(i - n_ctx, 0), 0)))


def _pick(i, tile_rows, ctx_ref, den_ref):
    return jnp.where(i < T_CTX // tile_rows, ctx_ref[...], den_ref[...])


def _params(n_grid):
    return pltpu.CompilerParams(dimension_semantics=("arbitrary",) * n_grid, vmem_limit_bytes=VMEM_LIMIT)


ADA_TN = 1536


def _adaln_kernel(cctx_ref, c_ref, w_ref, b_ref, *o_refs):
    cvec8 = jnp.concatenate([cctx_ref[...], c_ref[...], jnp.zeros((8 - 1 - N_DEN_B, D), f32)], axis=0)
    s_hi, s_lo = _split_bf16(_silu(cvec8))
    w_hi, w_lo = _split_bf16(w_ref[0])
    bias = b_ref[pl.ds(pl.program_id(0), 1), :]
    val = _dot(s_hi, w_hi) + (_dot(s_hi, w_lo) + _dot(s_lo, w_hi)) + bias
    for k, o_ref in enumerate(o_refs):
        @pl.when(pl.program_id(0) == k)
        def _():
            o_ref[...] = val


def _adaln(c_ctx_row, c, ada_w, ada_b):
    n = 6 * D
    nb = n // ADA_TN
    layer_spec = lambda k: pl.BlockSpec((8, ADA_TN), lambda l, j: (0, jnp.clip((l - k) * nb + j, 0, nb - 1)))
    return pl.pallas_call(
        _adaln_kernel,
        grid=(DEPTH, n // ADA_TN),
        in_specs=[
            pl.BlockSpec((1, D), lambda l, j: (0, 0)),
            pl.BlockSpec((N_DEN_B, D), lambda l, j: (0, 0)),
            pl.BlockSpec((1, D, ADA_TN), lambda l, j: (l, 0, j)),
            pl.BlockSpec((DEPTH, ADA_TN), lambda l, j: (0, j)),
        ],
        out_specs=[layer_spec(k) for k in range(DEPTH)],
        out_shape=[jax.ShapeDtypeStruct((8, n), f32) for _ in range(DEPTH)],
        compiler_params=_params(2),
        name="adaln",
    )(c_ctx_row, c, ada_w, ada_b)


def _rope_table_array(head_dim):
    q = head_dim // 4
    pos = np.arange(DEN_L)
    row, col = (pos // GRID_W).astype(np.float64), (pos % GRID_W).astype(np.float64)
    lane = np.arange(LANE) % head_dim
    is_col = lane >= head_dim // 2
    w = lane % (head_dim // 2)
    first = w < q
    inv_freq = ROPE_BASE ** (-np.arange(q, dtype=np.float64) / q)
    ang = np.where(is_col[None, :], col[:, None], row[:, None]) * inv_freq[w % q][None, :]
    cos, sin = np.cos(ang), np.sin(ang)
    sin_a = np.where(first[None, :], -sin, 0.0)
    sin_b = np.where(first[None, :], 0.0, sin)
    ident = np.stack([np.ones((TOK_TILE, LANE)), np.zeros((TOK_TILE, LANE)), np.zeros((TOK_TILE, LANE))])
    tab = np.concatenate([ident, np.stack([cos, sin_a, sin_b])], axis=1).astype(np.float32)
    return jnp.asarray(tab), q


def _rope_chunk(x, tab_ref, q):
    return x * tab_ref[0] + pltpu.roll(x, LANE - q, 1) * tab_ref[1] + pltpu.roll(x, q, 1) * tab_ref[2]


PROJ_W = 1280
C_QA, C_KA, C_VA, C_CQ, C_CKV, C_KR = 0, 512, 640, 768, 1024, 1152
MLA_NN = MLA_HEADS * MLA_NOPE


def _ab_proj_kernel(xc_ref, xd_ref, mods_ref, w_ref, qn_ref, kvn_ref, wuq_ref, wukv_ref, ta_ref, tm_ref,
                    qa_ref, ka_ref, va_ref, ckv_ref, kr_ref, qm_ref, kvl_ref, krc_ref, wb_ref, *, qa_shift, qm_shift):
    i = pl.program_id(0)

    @pl.when(i == 0)
    def _():
        n_w = w_ref.shape[0]
        wb_ref[:, PROJ_W - LANE:PROJ_W] = jnp.zeros((D, LANE), bf16)
        for c0 in range(0, n_w, LANE):
            c1 = min(c0 + LANE, n_w)
            wb_ref[:, c0:c1] = w_ref[c0:c1, :].T.astype(bf16)

    r = _mod_row(i, TOK_TILE)
    mrow = mods_ref[pl.ds(r, 1), :]
    sh, sc = mrow[:, 0:D], mrow[:, D:2 * D]
    x = _pick(i, TOK_TILE, xc_ref, xd_ref)
    h = (x * (1.0 + sc) + sh).astype(bf16)
    proj = _dot(h, wb_ref[...])
    for j in range(4):
        c0 = C_QA + LANE * j
        qa_ref[:, LANE * j:LANE * (j + 1)] = _rope_chunk(proj[:, c0:c0 + LANE], ta_ref, qa_shift).astype(bf16)
    ka_ref[...] = _rope_chunk(proj[:, C_KA:C_KA + LANE], ta_ref, qa_shift)
    va_ref[...] = proj[:, C_VA:C_VA + LANE]
    cq = proj[:, C_CQ:C_CQ + MLA_Q_RANK]
    cq = cq * lax.rsqrt(jnp.mean(cq * cq, axis=-1, keepdims=True) + RMS_EPS) * qn_ref[...]
    ckv = proj[:, C_CKV:C_CKV + MLA_KV_RANK]
    ckv = ckv * lax.rsqrt(jnp.mean(ckv * ckv, axis=-1, keepdims=True) + RMS_EPS) * kvn_ref[...]
    kr = _rope_chunk(proj[:, C_KR:C_KR + LANE], tm_ref, qm_shift)
    kr_ref[...] = kr

    qm = _dot(cq.astype(bf16), wuq_ref[...])
    qm_ref[:, 0:MLA_NN] = qm[:, 0:MLA_NN].astype(bf16)
    for j in range(2):
        c0 = MLA_NN + LANE * j
        qm_ref[:, c0:c0 + LANE] = _rope_chunk(qm[:, c0:c0 + LANE], tm_ref, qm_shift).astype(bf16)
    kvl_ref[...] = _dot(ckv.astype(bf16), wukv_ref[...]).astype(bf16)

    @pl.when(i < T_CTX // TOK_TILE)
    def _():
        ckv_ref[...] = ckv
        kr_t = kr_ref[...].T
        for b in range(TOK_TILE // CTX_L):
            krc_ref[b, 0] = kr_t[0:MLA_ROPE, CTX_L * b:CTX_L * (b + 1)]


def _rope_block_index(i):
    tiles_ctx = T_CTX // TOK_TILE
    per_seq = DEN_L // TOK_TILE
    return jnp.where(i < tiles_ctx, 0, 1 + (i - tiles_ctx) % per_seq)


def _ab_proj(xc, xd, mods0, w_in, q_norm, kv_norm, w_uq_p, w_ukv_p):
    tab_a, qa_shift = _rope_table_array(A_HD)
    tab_m, qm_shift = _rope_table_array(MLA_ROPE)
    row_spec = lambda w: pl.BlockSpec((TOK_TILE, w), lambda i: (i, 0))
    xc_spec, xd_spec = _two_stream_specs(TOK_TILE, D)
    tab_spec = pl.BlockSpec((3, TOK_TILE, LANE), lambda i: (0, _rope_block_index(i), 0))
    outs = [(512, bf16), (LANE, f32), (LANE, f32), (LANE, f32), (LANE, f32), (768, bf16), (1024, bf16)]
    last_ctx = T_CTX // TOK_TILE - 1
    per = TOK_TILE // CTX_L
    ctx_spec = pl.BlockSpec((TOK_TILE, LANE), lambda i: (jnp.minimum(i, last_ctx), 0))
    krc_spec = pl.BlockSpec((per, 1, MLA_ROPE, CTX_L), lambda i: (jnp.minimum(i, last_ctx), 0, 0, 0))
    return pl.pallas_call(
        functools.partial(_ab_proj_kernel, qa_shift=qa_shift, qm_shift=qm_shift),
        grid=(T_ALL // TOK_TILE,),
        in_specs=[xc_spec, xd_spec, _full((8, 6 * D), 1), _full(w_in.shape, 1), _full((1, MLA_Q_RANK), 1),
                  _full((1, MLA_KV_RANK), 1), _full((MLA_Q_RANK, 768), 1), _full((MLA_KV_RANK, 1024), 1),
                  tab_spec, tab_spec],
        out_specs=[ctx_spec if k == 3 else row_spec(w) for k, (w, _) in enumerate(outs)] + [krc_spec],
        out_shape=[jax.ShapeDtypeStruct((T_CTX if k == 3 else T_ALL, w), dt) for k, (w, dt) in enumerate(outs)]
        + [jax.ShapeDtypeStruct((N_CTX_B, 1, MLA_ROPE, CTX_L), f32)],
        scratch_shapes=[pltpu.VMEM((D, PROJ_W), bf16)],
        compiler_params=_params(1),
        name="ab_proj",
    )(xc, xd, mods0, w_in, q_norm, kv_norm, w_uq_p, w_ukv_p, tab_a, tab_m)


def _softmax_blocks(s_refs, p_refs, sink_col=None):
    m = s_refs[0][...].max(axis=-1, keepdims=True)
    for s_ref in s_refs[1:]:
        m = jnp.maximum(m, s_ref[...].max(axis=-1, keepdims=True))
    if sink_col is not None:
        m = jnp.maximum(m, sink_col)
    for s_ref, p_ref in zip(s_refs, p_refs):
        p_ref[...] = jnp.exp(s_ref[...] - m).astype(bf16)
    return 0.0 if sink_col is None else jnp.exp(sink_col - m)


def _with_ones(v, axis=1):
    return jnp.concatenate([v, jnp.ones(v.shape, v.dtype)], axis=axis)


def _normalise(o_aug, extra, width):
    return o_aug[:, 0:width] * (1.0 / (o_aug[:, width:width + 1] + extra))


def _sink_column(sink_ref, rows_per_head):
    return jnp.concatenate([jnp.full((rows_per_head, 1), sink_ref[h], f32) for h in range(A_HEADS)], axis=0)


def _mla_q(qm_ref, h):
    rows = qm_ref.shape[0]
    return jnp.concatenate([qm_ref[:, MLA_NOPE * h:MLA_NOPE * (h + 1)],
                            qm_ref[:, MLA_NN + MLA_ROPE * h:MLA_NN + MLA_ROPE * (h + 1)],
                            jnp.zeros((rows, LANE - MLA_NOPE - MLA_ROPE), bf16)], axis=1)


def _mla_k(k_nope_h, k_rope):
    rows = k_nope_h.shape[0]
    return jnp.concatenate([k_nope_h, k_rope, jnp.zeros((rows, LANE - MLA_NOPE - MLA_ROPE), bf16)], axis=1)


def _mix_out_ln(merged_ref, wout_ref, x, mods_ref, r, g_ref, b_ref):
    out = _dot(merged_ref[...], wout_ref[...])
    gate = mods_ref[pl.ds(r, 1), 2 * D:3 * D]
    return _layer_norm(ALPHA * x + gate * out, g_ref[...], b_ref[...])


def _ctx_attn_kernel(sink_ref, qa_ref, ka_ref, va_ref, qm_ref, kvl_ref, kr_ref, x_ref, mods_ref, wout_ref,
                     g_ref, b_ref, o_ref, nk_ref, nv_ref, merged_ref, sa_ref, sm_ref, pa_ref, pm_ref):
    nk_ref[0, 0] = ka_ref[...].T.reshape(A_KV_HEADS, A_HD, CTX_L)
    nv_ref[0, 0] = va_ref[...].T.reshape(A_KV_HEADS, A_HD, CTX_L)
    n = CTX_L
    ka = ka_ref[...].astype(bf16)
    va = va_ref[...].astype(bf16)
    for j in range(A_KV_HEADS):
        q4 = jnp.concatenate([qa_ref[:, A_HD * h:A_HD * (h + 1)] for h in range(A_GROUP * j, A_GROUP * (j + 1))],
                             axis=0)
        sa_ref[A_GROUP * n * j:A_GROUP * n * (j + 1), :] = _dot_nt(q4, ka[:, A_HD * j:A_HD * (j + 1)]) * A_SCALE
    kr = kr_ref[:, 0:MLA_ROPE].astype(bf16)
    for h in range(MLA_HEADS):
        k_cat = _mla_k(kvl_ref[:, MLA_NOPE * h:MLA_NOPE * (h + 1)], kr)
        sm_ref[n * h:n * (h + 1), :] = _dot_nt(_mla_q(qm_ref, h), k_cat) * MLA_SCALE
    sink_a = _softmax_blocks([sa_ref], [pa_ref], _sink_column(sink_ref, n))
    _softmax_blocks([sm_ref], [pm_ref])
    for j in range(A_KV_HEADS):
        rows = slice(A_GROUP * n * j, A_GROUP * n * (j + 1))
        o4 = _normalise(_dot(pa_ref[rows, :], _with_ones(va[:, A_HD * j:A_HD * (j + 1)])), sink_a[rows], A_HD)
        for g in range(A_GROUP):
            h = A_GROUP * j + g
            merged_ref[:, A_HD * h:A_HD * (h + 1)] = o4[n * g:n * (g + 1)].astype(bf16)
    for h in range(MLA_HEADS):
        rows = slice(n * h, n * (h + 1))
        v = _with_ones(kvl_ref[:, MLA_NN + MLA_V * h:MLA_NN + MLA_V * (h + 1)])
        merged_ref[:, MLA_NN + MLA_V * h:MLA_NN + MLA_V * (h + 1)] = (
            _normalise(_dot(pm_ref[rows, :], v), 0.0, MLA_V).astype(bf16))
    o_ref[...] = _mix_out_ln(merged_ref, wout_ref, x_ref[...], mods_ref, 0, g_ref, b_ref)


def _ctx_attn(sink, qa, ka, va, qm, kvl, kr, x_all, mods0, w_out, ln_g, ln_b):
    blk = lambda w: pl.BlockSpec((CTX_L, w), lambda b: (b, 0))
    cache_blk = pl.BlockSpec((1, 1, A_KV_HEADS, A_HD, CTX_L), lambda b: (b, 0, 0, 0, 0))
    cache_shape = jax.ShapeDtypeStruct((N_CTX_B, 1, A_KV_HEADS, A_HD, CTX_L), f32)
    return pl.pallas_call(
        _ctx_attn_kernel,
        grid=(N_CTX_B,),
        in_specs=[pl.BlockSpec(memory_space=pltpu.SMEM), blk(512), blk(LANE), blk(LANE), blk(768), blk(1024),
                  blk(LANE), blk(D), _full((8, 6 * D), 1), _full((D, D), 1), _full((1, D), 1), _full((1, D), 1)],
        out_specs=[blk(D), cache_blk, cache_blk],
        out_shape=[jax.ShapeDtypeStruct((T_CTX, D), f32), cache_shape, cache_shape],
        scratch_shapes=[pltpu.VMEM((CTX_L, D), bf16),
                        pltpu.VMEM((A_HEADS * CTX_L, CTX_L), f32), pltpu.VMEM((MLA_HEADS * CTX_L, CTX_L), f32),
                        pltpu.VMEM((A_HEADS * CTX_L, CTX_L), bf16), pltpu.VMEM((MLA_HEADS * CTX_L, CTX_L), bf16)],
        compiler_params=_params(1),
        name="ctx_attn",
    )(sink, qa, ka, va, qm, kvl, kr, x_all, mods0, w_out, ln_g, ln_b)


QB = 256
WIN = QB + 2 * WINDOW
DEN_BLK0 = T_CTX // DEN_L
MLA_KEYS = CTX_L + DEN_L


def _den_attn_kernel(sink_ref, qa_ref, ka_ref, va_ref, cak_ref, cav_ref, qm_ref, kvl_ref, kr_ref, cckv_ref, ckr_ref,
                     wukv_ref, x_ref, mods_ref, wout_ref, g_ref, b_ref, o_ref, merged_ref, kcat_ref, vcat_ref,
                     saw_ref, sac_ref, sm_ref, paw_ref, pac_ref, pm_ref):
    b = pl.program_id(0)
    n = pl.program_id(1)

    @pl.when(n == 0)
    def _():
        kvc = _dot(cckv_ref[0].astype(bf16), wukv_ref[...]).astype(bf16)
        kr_ctx = ckr_ref[0].T.astype(bf16)
        kr_lat = kr_ref[:, 0:MLA_ROPE].astype(bf16)
        for h in range(MLA_HEADS):
            ns = slice(MLA_NOPE * h, MLA_NOPE * (h + 1))
            vs = slice(MLA_NN + MLA_V * h, MLA_NN + MLA_V * (h + 1))
            kcat_ref[h, 0:CTX_L, :] = _mla_k(kvc[:, ns], kr_ctx)
            kcat_ref[h, CTX_L:MLA_KEYS, :] = _mla_k(kvl_ref[:, ns], kr_lat)
            vcat_ref[h, 0:CTX_L, :] = _with_ones(kvc[:, vs])
            vcat_ref[h, CTX_L:MLA_KEYS, :] = _with_ones(kvl_ref[:, vs])

    start = pl.multiple_of(jnp.clip(QB * n - WINDOW, 0, DEN_L - WIN), WINDOW)
    grp_rows = A_GROUP * QB
    qpos = QB * n + (lax.broadcasted_iota(jnp.int32, (grp_rows, WIN), 0) & (QB - 1))
    kpos = start + lax.broadcasted_iota(jnp.int32, (grp_rows, WIN), 1)
    valid = jnp.abs(qpos - kpos) <= WINDOW
    kwin = ka_ref[pl.ds(start, WIN), :].astype(bf16)
    vwin = va_ref[pl.ds(start, WIN), :].astype(bf16)
    kctx_t = [cak_ref[0, j].astype(bf16) for j in range(A_KV_HEADS)]
    vctx_t = [cav_ref[0, j].astype(bf16) for j in range(A_KV_HEADS)]
    for j in range(A_KV_HEADS):
        sl = slice(A_HD * j, A_HD * (j + 1))
        rows = slice(grp_rows * j, grp_rows * (j + 1))
        q4 = jnp.concatenate([qa_ref[:, A_HD * h:A_HD * (h + 1)] for h in range(A_GROUP * j, A_GROUP * (j + 1))],
                             axis=0)
        saw_ref[rows, :] = jnp.where(valid, _dot_nt(q4, kwin[:, sl]) * A_SCALE, NEG_INF)
        sac_ref[rows, :] = _dot(q4, kctx_t[j]) * A_SCALE
    for h in range(MLA_HEADS):
        sm_ref[QB * h:QB * (h + 1), :] = _dot_nt(_mla_q(qm_ref, h), kcat_ref[h]) * MLA_SCALE
    sink_a = _softmax_blocks([saw_ref, sac_ref], [paw_ref, pac_ref], _sink_column(sink_ref, QB))
    _softmax_blocks([sm_ref], [pm_ref])
    for j in range(A_KV_HEADS):
        sl = slice(A_HD * j, A_HD * (j + 1))
        rows = slice(grp_rows * j, grp_rows * (j + 1))
        o_aug = (_dot(paw_ref[rows, :], _with_ones(vwin[:, sl]))
                 + _dot_nt(pac_ref[rows, :], _with_ones(vctx_t[j], axis=0)))
        o4 = _normalise(o_aug, sink_a[rows], A_HD)
        for g in range(A_GROUP):
            h = A_GROUP * j + g
            merged_ref[:, A_HD * h:A_HD * (h + 1)] = o4[QB * g:QB * (g + 1)].astype(bf16)
    for h in range(MLA_HEADS):
        rows = slice(QB * h, QB * (h + 1))
        o = _normalise(_dot(pm_ref[rows, :], vcat_ref[h]), 0.0, MLA_V)
        merged_ref[:, MLA_NN + MLA_V * h:MLA_NN + MLA_V * (h + 1)] = o.astype(bf16)
    o_ref[...] = _mix_out_ln(merged_ref, wout_ref, x_ref[...], mods_ref, 1 + b, g_ref, b_ref)


def _den_attn(sink, qa, ka, va, cache_k, cache_v, qm, kvl, kr, cache_ckv, cache_kr, w_ukv_p, x_all, mods0, w_out,
              ln_g, ln_b):
    nq = DEN_L // QB
    qblk = lambda w: pl.BlockSpec((QB, w), lambda b, n: (T_CTX // QB + b * nq + n, 0))
    seq = lambda w: pl.BlockSpec((DEN_L, w), lambda b, n: (DEN_BLK0 + b, 0))
    cache = lambda w: pl.BlockSpec((1, CTX_L, w), lambda b, n: (b, 0, 0))
    cache_a = pl.BlockSpec((1, A_KV_HEADS, A_HD, CTX_L), lambda b, n: (b, 0, 0, 0))
    return pl.pallas_call(
        _den_attn_kernel,
        grid=(N_DEN_B, nq),
        in_specs=[pl.BlockSpec(memory_space=pltpu.SMEM), qblk(512), seq(LANE), seq(LANE), cache_a, cache_a,
                  qblk(768), seq(1024), seq(LANE), cache(MLA_KV_RANK),
                  pl.BlockSpec((1, MLA_ROPE, CTX_L), lambda b, n: (b, 0, 0)),
                  _full((MLA_KV_RANK, 1024), 2), pl.BlockSpec((QB, D), lambda b, n: (b * nq + n, 0)),
                  _full((8, 6 * D), 2), _full((D, D), 2), _full((1, D), 2),
                  _full((1, D), 2)],
        out_specs=pl.BlockSpec((QB, D), lambda b, n: (b * nq + n, 0)),
        out_shape=jax.ShapeDtypeStruct((T_DEN, D), f32),
        scratch_shapes=[pltpu.VMEM((QB, D), bf16), pltpu.VMEM((MLA_HEADS, MLA_KEYS, LANE), bf16),
                        pltpu.VMEM((MLA_HEADS, MLA_KEYS, 2 * MLA_V), bf16)]
        + [pltpu.VMEM((A_HEADS * QB, w), dt) for dt in (f32, bf16) for w in (WIN, CTX_L, MLA_KEYS)],
        compiler_params=_params(2),
        name="den_attn",
    )(sink, qa, ka, va, cache_k, cache_v, qm, kvl, kr, cache_ckv, cache_kr, w_ukv_p, x_all, mods0, w_out, ln_g, ln_b)


SUBTILE = 256


def _route(x1, mrow, rw_ref, rb_ref):
    sh, sc = mrow[:, 3 * D:4 * D], mrow[:, 4 * D:5 * D]
    h = x1 * (1.0 + sc) + sh
    h_hi = h.astype(bf16)
    h_lo = (h - h_hi.astype(f32)).astype(bf16)
    logits = _dot(h_hi, rw_ref[0]) + (_dot(h_hi, rw_ref[1]) + _dot(h_lo, rw_ref[0]))
    scores = jax.nn.sigmoid(logits)
    lane = lax.broadcasted_iota(jnp.int32, scores.shape, 1).astype(f32)
    sel = jnp.where(lane < N_EXPERTS, scores + rb_ref[...], -jnp.inf)
    gates = jnp.zeros_like(scores)
    for _ in range(TOP_K):
        m = sel.max(axis=-1, keepdims=True)
        idx = jnp.where(sel == m, lane, float(LANE)).min(axis=-1, keepdims=True)
        hit = lane == idx
        gates = jnp.where(hit, scores, gates)
        sel = jnp.where(hit, -jnp.inf, sel)
    return h_hi, gates / gates.sum(axis=-1, keepdims=True) * ROUTED_SCALE


def _router_kernel(xc_ref, xd_ref, mods_ref, rw_ref, rb_ref, x_ref, h_ref, gates_ref):
    i = pl.program_id(0)
    r = _mod_row(i, TOK_TILE)
    mrow = mods_ref[pl.ds(r, 1), :]
    x1 = _pick(i, TOK_TILE, xc_ref, xd_ref)
    x_ref[...] = x1
    h_ref[...], gates_ref[...] = _route(x1, mrow, rw_ref, rb_ref)


def _router(x1c, x1d, mods_l, router_w_p, router_b_p):
    row_spec = lambda w: pl.BlockSpec((TOK_TILE, w), lambda i: (i, 0))
    xc_spec, xd_spec = _two_stream_specs(TOK_TILE, D)
    return pl.pallas_call(
        _router_kernel,
        grid=(T_ALL // TOK_TILE,),
        in_specs=[xc_spec, xd_spec, _full((8, 6 * D), 1), _full((2, D, LANE), 1), _full((1, LANE), 1)],
        out_specs=[row_spec(D), row_spec(D), row_spec(LANE)],
        out_shape=[jax.ShapeDtypeStruct((T_ALL, D), f32), jax.ShapeDtypeStruct((T_ALL, D), bf16),
                   jax.ShapeDtypeStruct((T_ALL, LANE), f32)],
        compiler_params=_params(1),
        name="router",
    )(x1c, x1d, mods_l, router_w_p, router_b_p)


MOE_TOK = 1536
MOE_EG = 8
MOE_VMEM_LIMIT = 60 * 1024 * 1024
MOE_TILE = 512
MOE_FF = MOE_EG * EXPERT_FF


def _moe_kernel(h_ref, gates_ref, mods_ref, wg_ref, wu_ref, wd_ref, sg_ref, su_ref, sd_ref, o_ref):
    p = pl.program_id(0)
    e = pl.program_id(1)
    n_tiles = MOE_TOK // MOE_TILE

    def gate_f(t):
        r = _mod_row(p * n_tiles + t, MOE_TILE)
        return mods_ref[pl.ds(r, 1), 5 * D:6 * D]

    def rows_of(t):
        if isinstance(t, int):
            return pl.ds(t * MOE_TILE, MOE_TILE)
        return pl.ds(pl.multiple_of(t * MOE_TILE, MOE_TILE), MOE_TILE)

    @pl.when(e == 0)
    def _():
        sg = sg_ref[...].astype(bf16)
        su = su_ref[...].astype(bf16)
        sd = sd_ref[...].astype(bf16)

        def body(t, c):
            rows = rows_of(t)
            ht = h_ref[rows, :]
            hid = _silu(_dot(ht, sg)) * _dot(ht, su)
            o_ref[rows, :] = gate_f(t) * _dot(hid.astype(bf16), sd)
            return c

        lax.fori_loop(0, n_tiles, body, 0)

    wg = jnp.concatenate([wg_ref[k].astype(bf16) for k in range(MOE_EG)], axis=1)
    wu = jnp.concatenate([wu_ref[k].astype(bf16) for k in range(MOE_EG)], axis=1)
    wd = jnp.concatenate([wd_ref[k].astype(bf16) for k in range(MOE_EG)], axis=0)
    lane = lax.broadcasted_iota(jnp.int32, (MOE_TILE, LANE), 1)

    def body(t, c):
        rows = rows_of(t)
        ht = h_ref[rows, :]
        hid = _silu(_dot(ht, wg)) * _dot(ht, wu)
        gt = gates_ref[rows, :]
        parts = []
        for k in range(MOE_EG):
            col = jnp.where(lane == e * MOE_EG + k, gt, 0.0).sum(axis=-1, keepdims=True)
            parts.append((hid[:, EXPERT_FF * k:EXPERT_FF * (k + 1)] * col).astype(bf16))
        o_ref[rows, :] += gate_f(t) * _dot(jnp.concatenate(parts, axis=1), wd)
        return c

    for t in range(n_tiles):
        body(t, 0)


def _moe(l, h, gates, mods_l, wg, wu, wd, sg, su, sd):
    tok = lambda w: pl.BlockSpec((MOE_TOK, w), lambda p, e: (p, 0))
    return pl.pallas_call(
        _moe_kernel,
        grid=(T_ALL // MOE_TOK, N_EXPERTS // MOE_EG),
        in_specs=[tok(D), tok(LANE), _full((8, 6 * D), 2),
                  pl.BlockSpec((None, MOE_EG, D, EXPERT_FF), lambda p, e: (l, e, 0, 0)),
                  pl.BlockSpec((None, MOE_EG, D, EXPERT_FF), lambda p, e: (l, e, 0, 0)),
                  pl.BlockSpec((None, MOE_EG, EXPERT_FF, D), lambda p, e: (l, e, 0, 0)),
                  pl.BlockSpec((None, D, SHARED_FF), lambda p, e: (l, 0, 0)),
                  pl.BlockSpec((None, D, SHARED_FF), lambda p, e: (l, 0, 0)),
                  pl.BlockSpec((None, SHARED_FF, D), lambda p, e: (l, 0, 0))],
        out_specs=tok(D),
        out_shape=jax.ShapeDtypeStruct((T_ALL, D), f32),
        compiler_params=pltpu.CompilerParams(dimension_semantics=("arbitrary", "arbitrary"),
                                             vmem_limit_bytes=MOE_VMEM_LIMIT),
        name="moe",
    )(h, gates, mods_l, wg, wu, wd, sg, su, sd)


FIN_TILE = 1024


def _moe_finish_kernel(x_ref, acc_ref, g_ref, b_ref, oc_ref, od_ref):
    is_ctx = pl.program_id(0) < T_CTX // FIN_TILE
    for cond, o_ref in ((is_ctx, oc_ref), (jnp.logical_not(is_ctx), od_ref)):
        @pl.when(cond)
        def _():
            o_ref[...] = _layer_norm(ALPHA * x_ref[...] + acc_ref[...], g_ref[...], b_ref[...])


def _moe_finish(x1, acc, ln_g, ln_b):
    n_ctx = T_CTX // FIN_TILE
    src = pl.BlockSpec((FIN_TILE, D), lambda i: (i, 0))
    return pl.pallas_call(
        _moe_finish_kernel,
        grid=(T_ALL // FIN_TILE,),
        in_specs=[src, src, _full((1, D), 1), _full((1, D), 1)],
        out_specs=[pl.BlockSpec((FIN_TILE, D), lambda i: (jnp.minimum(i, n_ctx - 1), 0)),
                   pl.BlockSpec((FIN_TILE, D), lambda i: (jnp.maximum(i - n_ctx, 0), 0))],
        out_shape=[jax.ShapeDtypeStruct((T_CTX, D), f32), jax.ShapeDtypeStruct((T_DEN, D), f32)],
        compiler_params=_params(1),
        name="moe_finish",
    )(x1, acc, ln_g, ln_b)


def _router_weights(l, router_w, router_bias):
    rw = jnp.pad(router_w[l], ((0, 0), (0, LANE - N_EXPERTS)))
    rw_hi, rw_lo = _split_bf16(rw)
    rb = jnp.pad(router_bias[l], (0, LANE - N_EXPERTS)).reshape(1, LANE)
    return jnp.stack([rw_hi, rw_lo]), rb


S5_Q = 8
S5_NGB = D // LANE


def _s5_in_kernel(x_ref, acc_ref, lg_ref, lb_ref, mods_ref, w_ref, x2_ref, u_ref, u2_ref, slab_ref, *, row_of,
                  seq_len):
    r = row_of(pl.program_id(0))
    mrow = mods_ref[pl.ds(r, 1), :]
    sh, sc = mrow[:, 0:D], mrow[:, D:2 * D]
    x2 = _layer_norm(ALPHA * x_ref[...] + acc_ref[...], lg_ref[...], lb_ref[...])
    x2_ref[...] = x2
    h = (x2 * (1.0 + sc) + sh).astype(bf16)
    u = _dot(h, w_ref[...])
    u_ref[...] = u
    for s in range(S5_NGB):
        slab_ref[s] = u[:, LANE * s:LANE * (s + 1)]
    kt = seq_len // S5_Q
    for s in range(S5_NGB):
        for q in range(TOK_TILE // seq_len):
            for j in range(S5_Q):
                u2_ref[s, q * kt:(q + 1) * kt, LANE * j:LANE * (j + 1)] = (
                    slab_ref[s, pl.ds(q * seq_len + j, kt, stride=S5_Q), :].astype(bf16))


def _s5_in(x1, acc, ln_g, ln_b, tile0, mods1, w_in_c, n_b, seq_len, row_of):
    n_tiles = n_b * seq_len // TOK_TILE
    chunks = TOK_TILE // S5_Q
    src = pl.BlockSpec((TOK_TILE, D), lambda i: (tile0 + i, 0))
    dst = pl.BlockSpec((TOK_TILE, D), lambda i: (i, 0))
    return pl.pallas_call(
        functools.partial(_s5_in_kernel, row_of=row_of, seq_len=min(seq_len, TOK_TILE)),
        grid=(n_tiles,),
        in_specs=[src, src, _full((1, D), 1), _full((1, D), 1), _full((8, 6 * D), 1), _full((D, D), 1)],
        out_specs=[dst, dst, pl.BlockSpec((S5_NGB, chunks, D), lambda i: (0, i, 0))],
        out_shape=[jax.ShapeDtypeStruct((n_b * seq_len, D), f32), jax.ShapeDtypeStruct((n_b * seq_len, D), f32),
                   jax.ShapeDtypeStruct((S5_NGB, n_tiles * chunks, D), bf16)],
        scratch_shapes=[pltpu.VMEM((S5_NGB, TOK_TILE, LANE), f32)],
        compiler_params=_params(1),
        name="s5_in",
    )(x1, acc, ln_g, ln_b, mods1, w_in_c)


S5_GL = (LANE // S5_CH) * S5_P
S5_ROWS_C = (CTX_L // S5_Q) * N_CTX_B
S5_ROWS_D = (DEN_L // S5_Q) * N_DEN_B


def _s5_scan_kernel(lam_ref, bt_ref, ctr_ref, cti_ref, uc_ref, ud_ref, h0_ref,
                    yc_ref, yd_ref, st_ref, win_ref, mso_ref, wit_ref, a_ref, s_ref, hp_ref):
    gl = S5_GL
    rowg = lax.shift_right_logical(lax.broadcasted_iota(jnp.int32, (LANE, gl), 0), 4)
    colg = lax.shift_right_logical(lax.broadcasted_iota(jnp.int32, (LANE, gl), 1), 6)
    same_group = rowg == colg
    reps = LANE // S5_CH

    def expand(t):
        return jnp.where(same_group, jnp.concatenate([t] * reps, axis=0), 0.0)

    def expand_c(t):
        return jnp.where(same_group, jnp.concatenate([t] * reps, axis=1), 0.0)

    for d in range(2):
        fwd = d == 0
        lre, lim = lam_ref[d, 0:1, :], lam_ref[d, 1:2, :]
        dt = jnp.exp(lam_ref[d, 2:3, :])
        a, w = lre * dt, lim * dt
        pre = [jnp.exp(m * a) * jnp.cos(m * w) for m in range(S5_Q + 1)]
        pim = [jnp.exp(m * a) * jnp.sin(m * w) for m in range(S5_Q + 1)]
        xr, xi = pre[1] - 1.0, pim[1]
        den = lre * lre + lim * lim
        cfr, cfi = (xr * lre + xi * lim) / den, (xi * lre - xr * lim) / den
        btr, bti = bt_ref[0, d], bt_ref[1, d]
        bexp_r = expand(cfr * btr - cfi * bti)
        bexp_i = expand(cfr * bti + cfi * btr)
        cexp_r, cexp_i = expand_c(ctr_ref[d]), expand_c(cti_ref[d])
        for m in range(S5_Q + 1):
            a_ref[m, :, 0:gl] = cexp_r * pre[m] - cexp_i * pim[m]
            a_ref[m, :, gl:2 * gl] = -(cexp_r * pim[m] + cexp_i * pre[m])
        for j in range(S5_Q):
            m = S5_Q - 1 - j if fwd else j
            win_ref[LANE * j:LANE * (j + 1), 0:gl] = (pre[m] * bexp_r - pim[m] * bexp_i).astype(bf16)
            win_ref[LANE * j:LANE * (j + 1), gl:2 * gl] = (pre[m] * bexp_i + pim[m] * bexp_r).astype(bf16)
        for j in range(S5_Q):
            m = j + 1 if fwd else S5_Q - j
            mso_ref[LANE * j:LANE * (j + 1), :] = a_ref[m].astype(bf16)
        b2 = jnp.concatenate([bexp_r, bexp_i], axis=1).astype(bf16)
        kt = [_dot_nt(b2, a_ref[tau].astype(bf16)) for tau in range(S5_Q)]
        for j in range(S5_Q):
            for jp in range(S5_Q):
                tau = jp - j if fwd else j - jp
                blk = slice(LANE * j, LANE * (j + 1)), slice(LANE * jp, LANE * (jp + 1))
                if fwd:
                    wit_ref[blk] = kt[tau] if tau >= 0 else jnp.zeros((LANE, LANE), f32)
                elif tau >= 0:
                    wit_ref[blk] = wit_ref[blk] + kt[tau]

        l8r, l8i = pre[S5_Q], pim[S5_Q]

        nsl = gl // LANE

        def slabs(ref, rs, first):
            return jnp.concatenate([ref[first + sl, rs, :] for sl in range(nsl)], axis=1)

        def put_slabs(ref, rs, first, val):
            for sl in range(nsl):
                ref[first + sl, rs, :] = val[:, LANE * sl:LANE * (sl + 1)]

        def advance(hr, hi_, sr, si):
            return l8r * hr - l8i * hi_ + sr, l8r * hi_ + l8i * hr + si

        def run(u_ref, y_ref, n_b, n_k, h_init):
            rows = n_b * n_k
            s = _dot(u_ref[0], win_ref[...])
            if n_b % SUB == 0:
                pitch = n_k + 1
                for bb in range(n_b):
                    dst = slice(bb * pitch, bb * pitch + n_k)
                    put_slabs(s_ref, dst, 0, s[bb * n_k:(bb + 1) * n_k, 0:gl])
                    put_slabs(s_ref, dst, nsl, s[bb * n_k:(bb + 1) * n_k, gl:2 * gl])

                def step(i, carry):
                    hr, hi_ = carry
                    rs = pl.ds(i if fwd else n_k - 1 - i, n_b, stride=pitch)
                    put_slabs(hp_ref, rs, 0, hr)
                    put_slabs(hp_ref, rs, nsl, hi_)
                    return advance(hr, hi_, slabs(s_ref, rs, 0), slabs(s_ref, rs, nsl))

                h_fin = lax.fori_loop(0, n_k, step, h_init)
                hp = jnp.concatenate(
                    [jnp.concatenate([hp_ref[sl, bb * pitch:bb * pitch + n_k, :] for sl in range(2 * nsl)], axis=1)
                     for bb in range(n_b)], axis=0).astype(bf16)
            else:
                put_slabs(s_ref, slice(0, rows), 0, s[:, 0:gl])
                put_slabs(s_ref, slice(0, rows), nsl, s[:, gl:2 * gl])
                n_it = n_k // SUB

                def step(i, carry):
                    it = i if fwd else n_it - 1 - i
                    out = []
                    for bb in range(n_b):
                        hr, hi_ = carry[bb]
                        rs = pl.ds(pl.multiple_of(bb * n_k + it * SUB, SUB), SUB)
                        s_re, s_im = slabs(s_ref, rs, 0), slabs(s_ref, rs, nsl)
                        prev_r, prev_i = [None] * SUB, [None] * SUB
                        for sub in (range(SUB) if fwd else reversed(range(SUB))):
                            prev_r[sub], prev_i[sub] = hr, hi_
                            hr, hi_ = advance(hr, hi_, s_re[sub:sub + 1], s_im[sub:sub + 1])
                        put_slabs(hp_ref, rs, 0, jnp.concatenate(prev_r, axis=0))
                        put_slabs(hp_ref, rs, nsl, jnp.concatenate(prev_i, axis=0))
                        out.append((hr, hi_))
                    return tuple(out)

                fin = lax.fori_loop(0, n_it, step, tuple((h_init[0][bb:bb + 1], h_init[1][bb:bb + 1])
                                                          for bb in range(n_b)))
                h_fin = (jnp.concatenate([f[0] for f in fin], axis=0), jnp.concatenate([f[1] for f in fin], axis=0))
                hp = jnp.concatenate([hp_ref[sl, 0:rows, :] for sl in range(2 * nsl)], axis=1).astype(bf16)
            y = _dot_nt(hp, mso_ref[...])
            if fwd:
                y_ref[0] = y
            else:
                y_ref[0] += y
            return h_fin

        zeros = jnp.zeros((N_CTX_B, gl), f32)
        hr, hi_ = run(uc_ref, yc_ref, N_CTX_B, CTX_L // S5_Q, (zeros, zeros))
        st_ref[d, 0] = hr
        st_ref[d, 1] = hi_
        run(ud_ref, yd_ref, N_DEN_B, DEN_L // S5_Q, (h0_ref[d, 0], h0_ref[d, 1]))

    wit = wit_ref[...].astype(bf16)
    yc_ref[0] += _dot(uc_ref[0], wit)
    yd_ref[0] += _dot(ud_ref[0], wit)


def _s5_scan(lam3, bt, ct_re, ct_im, u2c, u2d, h0):
    gl = S5_GL
    vec = pl.BlockSpec((2, 3, gl), lambda g: (0, 0, g))
    tab = pl.BlockSpec((2, 2, S5_CH, gl), lambda g: (0, 0, 0, g))
    ctab = pl.BlockSpec((2, LANE, S5_P), lambda g: (0, g, 0))
    rows = lambda n: pl.BlockSpec((1, n, D), lambda g: (g, 0, 0))
    return pl.pallas_call(
        _s5_scan_kernel,
        grid=(S5_NGB,),
        in_specs=[vec, tab, ctab, ctab, rows(S5_ROWS_C), rows(S5_ROWS_D),
                  pl.BlockSpec((2, 2, N_DEN_B, gl), lambda g: (0, 0, 0, g))],
        out_specs=[rows(S5_ROWS_C), rows(S5_ROWS_D), pl.BlockSpec((2, 2, N_CTX_B, gl), lambda g: (0, 0, 0, g))],
        out_shape=[jax.ShapeDtypeStruct((S5_NGB, S5_ROWS_C, D), f32), jax.ShapeDtypeStruct((S5_NGB, S5_ROWS_D, D), f32),
                   jax.ShapeDtypeStruct((2, 2, N_CTX_B, S5_G * S5_P), f32)],
        scratch_shapes=[pltpu.VMEM((D, 2 * gl), bf16), pltpu.VMEM((D, 2 * gl), bf16), pltpu.VMEM((D, D), f32),
                        pltpu.VMEM((S5_Q + 1, LANE, 2 * gl), f32),
                        pltpu.VMEM((2 * gl // LANE, S5_ROWS_C + 2 * N_CTX_B, LANE), f32),
                        pltpu.VMEM((2 * gl // LANE, S5_ROWS_C + 2 * N_CTX_B, LANE), f32)],
        compiler_params=_params(1),
        name="s5_scan",
    )(lam3, bt, ct_re, ct_im, u2c, u2d, h0)


def _gelu_tanh(x):
    return 0.5 * x * (1.0 + jnp.tanh(np.sqrt(2.0 / np.pi).astype(np.float32) * (x + 0.044715 * (x * x * x))))


def _s5_out_kernel(xc_ref, xd_ref, uc_ref, ud_ref, yc_ref, yd_ref, mods_ref, dsk_ref, wout_ref, g_ref, b_ref, rw_ref,
                   rb_ref, x1_ref, h_ref, gates_ref, slab_ref, wob_ref):
    i = pl.program_id(0)

    @pl.when(i == 0)
    def _():
        wob_ref[...] = wout_ref[...].astype(bf16)

    is_ctx = i < T_CTX // TOK_TILE
    r = _mod_row(i, TOK_TILE)
    mrow = mods_ref[pl.ds(r, 1), :]
    u = _pick(i, TOK_TILE, uc_ref, ud_ref)

    kt = CTX_L // S5_Q
    for s in range(S5_NGB):
        for q in range(TOK_TILE // CTX_L):
            for j in range(S5_Q):
                blk = (s, slice(q * kt, (q + 1) * kt), slice(LANE * j, LANE * (j + 1)))
                slab_ref[s, pl.ds(q * CTX_L + j, kt, stride=S5_Q), :] = jnp.where(is_ctx, yc_ref[blk], yd_ref[blk])
    halves = [slice(a, a + SUBTILE) for a in range(0, TOK_TILE, SUBTILE)]
    zs = []
    for rows in halves:
        y = jnp.concatenate([slab_ref[s, rows, :] for s in range(S5_NGB)], axis=1) + dsk_ref[...] * u[rows]
        zs.append(_dot(_gelu_tanh(y).astype(bf16), wob_ref[...]))
    for rows, z in zip(halves, zs):
        out = z[:, 0:D] * jax.nn.sigmoid(z[:, D:2 * D])
        x = jnp.where(is_ctx, xc_ref[rows, :], xd_ref[rows, :])
        x1 = _layer_norm(ALPHA * x + mrow[:, 2 * D:3 * D] * out, g_ref[...], b_ref[...])
        x1_ref[rows, :] = x1
        h_ref[rows, :], gates_ref[rows, :] = _route(x1, mrow, rw_ref, rb_ref)


def _s5_out(xc, xd, uc, ud, yc, yd, mods1, d_skip, w_out_c, ln_g, ln_b, rw, rb):
    row_spec = lambda w: pl.BlockSpec((TOK_TILE, w), lambda i: (i, 0))
    uc_spec, ud_spec = _two_stream_specs(TOK_TILE, D)
    n_ctx = T_CTX // TOK_TILE
    chunks = TOK_TILE // S5_Q
    return pl.pallas_call(
        _s5_out_kernel,
        grid=(T_ALL // TOK_TILE,),
        in_specs=[uc_spec, ud_spec, uc_spec, ud_spec,
                  pl.BlockSpec((S5_NGB, chunks, D), lambda i: (0, jnp.minimum(i, n_ctx - 1), 0)),
                  pl.BlockSpec((S5_NGB, chunks, D), lambda i: (0, jnp.maximum(i - n_ctx, 0), 0)),
                  _full((8, 6 * D), 1), _full((1, D), 1), _full((D, 2 * D), 1), _full((1, D), 1), _full((1, D), 1),
                  _full((2, D, LANE), 1), _full((1, LANE), 1)],
        out_specs=[row_spec(D), row_spec(D), row_spec(LANE)],
        out_shape=[jax.ShapeDtypeStruct((T_ALL, D), f32), jax.ShapeDtypeStruct((T_ALL, D), bf16),
                   jax.ShapeDtypeStruct((T_ALL, LANE), f32)],
        scratch_shapes=[pltpu.VMEM((S5_NGB, TOK_TILE, LANE), f32), pltpu.VMEM((D, 2 * D), bf16)],
        compiler_params=_params(1),
        name="s5_out",
    )(xc, xd, uc, ud, yc, yd, mods1, d_skip, w_out_c, ln_g, ln_b, rw, rb)


def kernel(x_prompt, x_sample, c, cache_attn_k, cache_attn_v, cache_mla_ckv, cache_mla_krope, state_ssm, c_ctx,
           ada_w, ada_b, ln_mix_g, ln_mix_b, ln_ffn_g, ln_ffn_b, w_in_ab, attn_sink, mla_q_norm, mla_kv_norm,
           mla_w_uq, mla_w_ukv, w_out_ab, w_in_c, s5_lam_re, s5_lam_im, s5_log_dt, s5_b_re, s5_b_im, s5_c_re,
           s5_c_im, s5_d, w_out_c, router_w, router_bias, exp_w_gate, exp_w_up, exp_w_down, sh_w_gate, sh_w_up,
           sh_w_down):
    row = lambda v: v.reshape(1, -1)
    xc, xd = x_prompt.reshape(T_CTX, D), x_sample.reshape(T_DEN, D)
    mods = _adaln(row(c_ctx), c, ada_w, ada_b)

    uq = mla_w_uq[0].reshape(MLA_Q_RANK, MLA_HEADS, MLA_NOPE + MLA_ROPE)
    w_uq_p = jnp.concatenate([uq[:, :, :MLA_NOPE].reshape(MLA_Q_RANK, -1), uq[:, :, MLA_NOPE:].reshape(MLA_Q_RANK, -1)],
                             axis=1).astype(bf16)
    ukv = mla_w_ukv[0].reshape(MLA_KV_RANK, MLA_HEADS, MLA_NOPE + MLA_V)
    w_ukv_p = jnp.concatenate([ukv[:, :, :MLA_NOPE].reshape(MLA_KV_RANK, -1),
                               ukv[:, :, MLA_NOPE:].reshape(MLA_KV_RANK, -1)], axis=1).astype(bf16)
    qa, ka, va, ckv, kr, qm, kvl, kr_ctx_t = _ab_proj(xc, xd, mods[0], w_in_ab[0].T, row(mla_q_norm[0]), row(mla_kv_norm[0]),
                                            w_uq_p, w_ukv_p)
    w_out_b = w_out_ab[0].astype(bf16)
    g0, b0 = row(ln_mix_g[0]), row(ln_mix_b[0])
    x1c, nk_t, nv_t = _ctx_attn(attn_sink[0], qa, ka, va, qm, kvl, kr, xc, mods[0], w_out_b, g0, b0)
    new_attn_k = jnp.transpose(nk_t, (0, 1, 4, 2, 3))
    new_attn_v = jnp.transpose(nv_t, (0, 1, 4, 2, 3))
    x1d = _den_attn(attn_sink[0], qa, ka, va,
                    jnp.transpose(cache_attn_k[:, 0], (0, 2, 3, 1)), jnp.transpose(cache_attn_v[:, 0], (0, 2, 3, 1)),
                    qm, kvl, kr, cache_mla_ckv[:, 0], jnp.transpose(cache_mla_krope[:, 0], (0, 2, 1)), w_ukv_p, xd, mods[0], w_out_b, g0, b0)
    rw0, rb0 = _router_weights(0, router_w, router_bias)
    x1, h, gates = _router(x1c, x1d, mods[0], rw0, rb0)
    acc = _moe(0, h, gates, mods[0], exp_w_gate, exp_w_up, exp_w_down, sh_w_gate, sh_w_up, sh_w_down)

    w_in_c_b = w_in_c[0].astype(bf16)
    lg0, lb0 = row(ln_ffn_g[0]), row(ln_ffn_b[0])
    n_ctx_tiles = T_CTX // TOK_TILE
    x2c, uc, u2c = _s5_in(x1, acc, lg0, lb0, 0, mods[1], w_in_c_b, N_CTX_B, CTX_L, lambda i: 0)
    x2d, ud, u2d = _s5_in(x1, acc, lg0, lb0, n_ctx_tiles, mods[1], w_in_c_b, N_DEN_B, DEN_L,
                          lambda i: 1 + i // (DEN_L // TOK_TILE))
    gp = S5_G * S5_P
    bt = jnp.transpose(jnp.stack([s5_b_re[0], s5_b_im[0]]), (0, 1, 4, 2, 3)).reshape(2, 2, S5_CH, gp)
    lam3 = jnp.stack([s5_lam_re[0].reshape(2, gp), s5_lam_im[0].reshape(2, gp),
                      jnp.repeat(s5_log_dt[0], S5_P, axis=-1).reshape(2, gp)], axis=1)
    chan_major_c = lambda t: t[0].reshape(2, S5_G * S5_CH, S5_P)
    h0 = jnp.transpose(state_ssm[:, 0], (1, 2, 0, 3, 4)).reshape(2, 2, N_DEN_B, gp)
    yc, yd, st = _s5_scan(lam3, bt, chan_major_c(s5_c_re), chan_major_c(s5_c_im), u2c, u2d, h0)
    rw1, rb1 = _router_weights(1, router_w, router_bias)
    x3, h, gates = _s5_out(x2c, x2d, uc, ud, yc, yd, mods[1], row(s5_d[0]),
                           w_out_c[0], row(ln_mix_g[1]), row(ln_mix_b[1]), rw1, rb1)
    acc = _moe(1, h, gates, mods[1], exp_w_gate, exp_w_up, exp_w_down, sh_w_gate, sh_w_up, sh_w_down)
    lg1, lb1 = row(ln_ffn_g[1]), row(ln_ffn_b[1])
    y_c, y_d = _moe_finish(x3, acc, lg1, lb1)
    y_prompt, y_sample = y_c.reshape(N_CTX_B, CTX_L, D), y_d.reshape(N_DEN_B, DEN_L, D)
    new_mla_ckv = ckv.reshape(N_CTX_B, 1, CTX_L, MLA_KV_RANK)
    new_mla_krope = jnp.transpose(kr_ctx_t, (0, 1, 3, 2))
    new_state_ssm = jnp.transpose(st, (2, 0, 1, 3)).reshape(N_CTX_B, 1, 2, 2, S5_G, S5_P)
    return (y_prompt, y_sample, new_attn_k, new_attn_v, new_mla_ckv, new_mla_krope, new_state_ssm)
```
